```python
import math
import jax
import jax.numpy as jnp
from jax import lax
import numpy as np

D_MODEL = 1024
BATCH = 8
SEQ = 2048
DEPTH = 2

GRID_W = 64
CTX_LEN = 256
HEAD_DIM = 64

HY_CH = 256
HY_SHORT = 3
HY_BANDS = 16
HY_EMB = 1 + 2 * HY_BANDS
HY_FFN = 64
HY_DECAY_MIN = -math.log(1e-2) / 1.5
HY_DECAY_MAX = -math.log(1e-2) / 0.3
HY_FILTER_SCALE = 0.05

WA_HEADS = 4
WA_KV_HEADS = 2
WINDOW = 128
WA_BLOCK = 128

NA_HEADS = 4
NA_KR = 8
NA_KC = 16

GDN_HEADS = 4
GDN_CONV = 4
GDN_CHUNK = 64

HY_IN = 3 * HY_CH
WA_IN = (WA_HEADS + 2 * WA_KV_HEADS) * HEAD_DIM
NA_IN = 3 * NA_HEADS * HEAD_DIM
GDN_W = GDN_HEADS * HEAD_DIM
GDN_IN = 4 * GDN_W + 4 * GDN_HEADS
D_IN = HY_IN + WA_IN + NA_IN + GDN_IN
D_MIX = HY_CH + (WA_HEADS + NA_HEADS) * HEAD_DIM + GDN_W

N_EXPERTS = 32
TOP_K = 4
D_EXPERT = 256
D_SHARED = 256
ROUTED_SCALE = 2.5
MOE_BLOCK = 128

ROPE_BASE = 10000.0
EPS = 1e-6
NEG = -1e30

kernel_name = 'hymba_hybrid_dit_moe_block'


def rmsnorm(x, g):
    xf = x.astype(jnp.float32)
    y = xf * lax.rsqrt(jnp.mean(xf * xf, axis=-1, keepdims=True) + EPS)
    return (y * g.astype(jnp.float32)).astype(x.dtype)


def depthwise_conv(x, w, left):
    k = w.shape[0]
    length = x.shape[1]
    xp = jnp.pad(x, ((0, 0), (left, k - 1 - left), (0, 0)))
    out = xp[:, 0:length] * w[0]
    for i in range(1, k):
        out = out + xp[:, i:i + length] * w[i]
    return out


def axial_rope(x, row, col):
    quarter = HEAD_DIM // 4
    half = HEAD_DIM // 2
    inv_freq = ROPE_BASE ** (-jnp.arange(quarter, dtype=jnp.float32) / quarter)

    def rotate(xa, pos):
        ang = pos.astype(jnp.float32)[:, None] * inv_freq[None, :]
        cos = jnp.cos(ang)[None, :, None, :]
        sin = jnp.sin(ang)[None, :, None, :]
        x1, x2 = xa[..., :quarter], xa[..., quarter:]
        return jnp.concatenate([x1 * cos - x2 * sin, x1 * sin + x2 * cos], axis=-1)

    xf = x.astype(jnp.float32)
    return jnp.concatenate([rotate(xf[..., :half], row), rotate(xf[..., half:], col)], axis=-1).astype(x.dtype)


def hyena_filters(length, w1, b1, w2, b2, w3, freq):
    t = jnp.arange(length, dtype=jnp.float32)
    t_norm = t / max(length - 1, 1)
    bands = jnp.linspace(1e-4, HY_BANDS - 1, HY_BANDS, dtype=jnp.float32)
    ang = (2.0 * math.pi / length) * t[:, None] * bands[None, :]
    z = jnp.concatenate([t_norm[:, None], jnp.cos(ang), -jnp.sin(ang)], axis=-1)
    fr = freq.astype(jnp.float32)
    h = jnp.sin(fr * (z @ w1.astype(jnp.float32) + b1.astype(jnp.float32)))
    h = jnp.sin(fr * (h @ w2.astype(jnp.float32) + b2.astype(jnp.float32)))
    h = h @ w3.astype(jnp.float32)
    decay = jnp.tile(jnp.linspace(HY_DECAY_MIN, HY_DECAY_MAX, HY_CH, dtype=jnp.float32), 2)
    h = h * jnp.exp(-t_norm[:, None] * decay[None, :])
    return h[:, :HY_CH], h[:, HY_CH:]


def bidirectional_long_conv(u, h_fwd, h_bwd, bias):
    length = u.shape[1]
    n = 2 * length
    kbuf = jnp.concatenate([h_fwd, jnp.zeros((1, HY_CH), jnp.float32), h_bwd[1:][::-1]], axis=0)
    uf = jnp.fft.rfft(u, n=n, axis=1)
    kf = jnp.fft.rfft(kbuf, n=n, axis=0)
    y = jnp.fft.irfft(uf * kf[None], n=n, axis=1)[:, :length]
    return y + u * bias


def hyena_mixer(p, conv_w, w1, b1, w2, b2, w3, freq, bias):
    length = p.shape[1]
    u = depthwise_conv(p, conv_w, (HY_SHORT - 1) // 2).astype(jnp.float32)
    x0, x1, v = u[..., :HY_CH], u[..., HY_CH:2 * HY_CH], u[..., 2 * HY_CH:]
    h_fwd, h_bwd = hyena_filters(length, w1, b1, w2, b2, w3, freq)
    z = bidirectional_long_conv(x1 * v, h_fwd, h_bwd, bias.astype(jnp.float32))
    return (x0 * z).astype(p.dtype)


def window_attention(p, pc, sink, row, col, need_ctx):
    b, s, _ = p.shape
    lc = pc.shape[1]
    grp = WA_HEADS // WA_KV_HEADS
    qd, kvd = WA_HEADS * HEAD_DIM, WA_KV_HEADS * HEAD_DIM
    scale = HEAD_DIM ** -0.5

    def split(t):
        n = t.shape[1]
        return (t[..., :qd].reshape(b, n, WA_KV_HEADS, grp, HEAD_DIM),
                t[..., qd:qd + kvd].reshape(b, n, WA_KV_HEADS, HEAD_DIM),
                t[..., qd + kvd:].reshape(b, n, WA_KV_HEADS, HEAD_DIM))

    q, k, v = split(p)
    qc, kc, vc = split(pc)
    q = axial_rope(q.reshape(b, s, WA_HEADS, HEAD_DIM), row, col).reshape(b, s, WA_KV_HEADS, grp, HEAD_DIM)
    k = axial_rope(k, row, col)
    sink_hg = sink.astype(jnp.float32).reshape(WA_KV_HEADS, grp)

    nb = s // WA_BLOCK
    span = WA_BLOCK + 2 * WINDOW
    pad = ((0, 0), (WINDOW, WINDOW), (0, 0), (0, 0))
    idx = jnp.arange(nb)[:, None] * WA_BLOCK + jnp.arange(span)[None, :]
    kb = jnp.pad(k, pad)[:, idx]
    vb = jnp.pad(v, pad)[:, idx]
    qb = q.reshape(b, nb, WA_BLOCK, WA_KV_HEADS, grp, HEAD_DIM)
    qpos = jnp.arange(nb)[:, None] * WA_BLOCK + jnp.arange(WA_BLOCK)[None, :]
    kpos = idx - WINDOW
    valid = ((jnp.abs(qpos[:, :, None] - kpos[:, None, :]) <= WINDOW)
             & (kpos[:, None, :] >= 0) & (kpos[:, None, :] < s))
    s_loc = jnp.einsum('bnqhgd,bnkhd->bnhgqk', qb, kb, preferred_element_type=jnp.float32) * scale
    s_loc = jnp.where(valid[None, :, None, None], s_loc, NEG)
    s_ctx = jnp.einsum('bnqhgd,bkhd->bnhgqk', qb, kc, preferred_element_type=jnp.float32) * scale
    sink_col = jnp.broadcast_to(sink_hg[None, None, :, :, None, None], s_ctx.shape[:-1] + (1,))
    prob = jax.nn.softmax(jnp.concatenate([sink_col, s_ctx, s_loc], axis=-1), axis=-1)
    o = (jnp.einsum('bnhgqk,bkhd->bnqhgd', prob[..., 1:1 + lc], vc.astype(jnp.float32))
         + jnp.einsum('bnhgqk,bnkhd->bnqhgd', prob[..., 1 + lc:], vb.astype(jnp.float32)))
    y = o.reshape(b, s, qd).astype(p.dtype)

    yc = None
    if need_ctx:
        sc = jnp.einsum('bqhgd,bkhd->bhgqk', qc, kc, preferred_element_type=jnp.float32) * scale
        sink_c = jnp.broadcast_to(sink_hg[None, :, :, None, None], sc.shape[:-1] + (1,))
        pcx = jax.nn.softmax(jnp.concatenate([sink_c, sc], axis=-1), axis=-1)
        yc = jnp.einsum('bhgqk,bkhd->bqhgd', pcx[..., 1:], vc.astype(jnp.float32)).reshape(b, lc, qd).astype(p.dtype)
    return y, yc


def neighborhood_attention(p, pc, rpb, need_ctx):
    b, s, _ = p.shape
    lc = pc.shape[1]
    hd = NA_HEADS * HEAD_DIM
    scale = HEAD_DIM ** -0.5
    rows = s // GRID_W
    kr = min(NA_KR, rows)
    q = p[..., :hd].reshape(b, rows, GRID_W, NA_HEADS, HEAD_DIM)
    k = p[..., hd:2 * hd].reshape(b, rows, GRID_W, NA_HEADS, HEAD_DIM)
    v = p[..., 2 * hd:].reshape(b, rows, GRID_W, NA_HEADS, HEAD_DIM)
    qc = pc[..., :hd].reshape(b, lc, NA_HEADS, HEAD_DIM)
    kc = pc[..., hd:2 * hd].reshape(b, lc, NA_HEADS, HEAD_DIM)
    vc = pc[..., 2 * hd:].reshape(b, lc, NA_HEADS, HEAD_DIM)

    r = jnp.arange(rows)
    key_rows = jnp.clip(r - kr // 2, 0, rows - kr)[:, None] + jnp.arange(kr)[None, :]
    kg = k[:, key_rows].reshape(b, rows, kr * GRID_W, NA_HEADS, HEAD_DIM)
    vg = v[:, key_rows].reshape(b, rows, kr * GRID_W, NA_HEADS, HEAD_DIM)
    cq = jnp.arange(GRID_W)
    c_start = jnp.clip(cq - NA_KC // 2, 0, GRID_W - NA_KC)
    col_ok = (cq[None, :] >= c_start[:, None]) & (cq[None, :] < c_start[:, None] + NA_KC)
    mask = jnp.broadcast_to(col_ok[:, None, :], (GRID_W, kr, GRID_W)).reshape(GRID_W, kr * GRID_W)
    dr = key_rows - r[:, None]
    dc = cq[None, :] - cq[:, None]
    ri = jnp.clip(dr + NA_KR - 1, 0, 2 * NA_KR - 2)[:, None, :, None]
    ci = jnp.clip(dc + NA_KC - 1, 0, 2 * NA_KC - 2)[None, :, None, :]
    bias = rpb.astype(jnp.float32)[:, ri, ci]
    bias = bias.reshape(NA_HEADS, rows, GRID_W, kr * GRID_W).transpose(1, 0, 2, 3)

    s_loc = jnp.einsum('brqhd,brkhd->brhqk', q, kg, preferred_element_type=jnp.float32) * scale + bias[None]
    s_loc = jnp.where(mask[None, None, None], s_loc, NEG)
    s_ctx = jnp.einsum('brqhd,bkhd->brhqk', q, kc, preferred_element_type=jnp.float32) * scale
    prob = jax.nn.softmax(jnp.concatenate([s_ctx, s_loc], axis=-1), axis=-1)
    o = (jnp.einsum('brhqk,bkhd->brqhd', prob[..., :lc], vc.astype(jnp.float32))
         + jnp.einsum('brhqk,brkhd->brqhd', prob[..., lc:], vg.astype(jnp.float32)))
    y = o.reshape(b, s, hd).astype(p.dtype)

    yc = None
    if need_ctx:
        sc = jnp.einsum('bqhd,bkhd->bhqk', qc, kc, preferred_element_type=jnp.float32) * scale
        pcx = jax.nn.softmax(sc, axis=-1)
        yc = jnp.einsum('bhqk,bkhd->bqhd', pcx, vc.astype(jnp.float32)).reshape(b, lc, hd).astype(p.dtype)
    return y, yc


def l2norm(x):
    return x * lax.rsqrt(jnp.sum(x * x, axis=-1, keepdims=True) + EPS)


def gdn_inputs(p, conv_w, a_log, dt_bias):
    b, n, _ = p.shape
    qkv = jax.nn.silu(depthwise_conv(p[..., :3 * GDN_W], conv_w, GDN_CONV // 2).astype(jnp.float32))
    q = l2norm(qkv[..., :GDN_W].reshape(b, n, GDN_HEADS, HEAD_DIM)) * HEAD_DIM ** -0.5
    k = l2norm(qkv[..., GDN_W:2 * GDN_W].reshape(b, n, GDN_HEADS, HEAD_DIM))
    v = qkv[..., 2 * GDN_W:].reshape(b, n, GDN_HEADS, HEAD_DIM)
    gate = p[..., 3 * GDN_W:4 * GDN_W].astype(jnp.float32)
    a = p[..., 4 * GDN_W:4 * GDN_W + 2 * GDN_HEADS].astype(jnp.float32).reshape(b, n, 2, GDN_HEADS)
    bb = p[..., 4 * GDN_W + 2 * GDN_HEADS:].astype(jnp.float32).reshape(b, n, 2, GDN_HEADS)
    g = -jnp.exp(a_log.astype(jnp.float32)) * jax.nn.softplus(a + dt_bias.astype(jnp.float32))
    beta = jax.nn.sigmoid(bb)
    return q, k, v, gate, g, beta


def gdn_chunked(q, k, v, g, beta, s0, emit):
    b, n, h, d = q.shape
    nc = n // GDN_CHUNK

    def to_chunks(x):
        x = x.reshape((b, nc, GDN_CHUNK, h) + x.shape[3:])
        return jnp.moveaxis(x, (1, 3), (0, 2))

    qc, kc, vc = to_chunks(q), to_chunks(k), to_chunks(v)
    gc, bc = to_chunks(g), to_chunks(beta)
    gcum = jnp.cumsum(gc, axis=-1)
    tri = jnp.tril(jnp.ones((GDN_CHUNK, GDN_CHUNK), bool))
    strict = jnp.tril(jnp.ones((GDN_CHUNK, GDN_CHUNK), bool), -1)
    decay = jnp.exp(jnp.where(tri, gcum[..., :, None] - gcum[..., None, :], NEG))
    kb = kc * bc[..., None]
    lmat = jnp.where(strict, jnp.einsum('nbhid,nbhjd->nbhij', kb, kc) * decay, 0.0)
    eye = jnp.eye(GDN_CHUNK, dtype=jnp.float32)
    tmat = lax.linalg.triangular_solve(eye + lmat, jnp.broadcast_to(eye, lmat.shape), left_side=True, lower=True)
    u = tmat @ (vc * bc[..., None])
    w = tmat @ (kb * jnp.exp(gcum)[..., None])
    k_dec = kc * jnp.exp(gcum[..., -1:] - gcum)[..., None]
    g_last = jnp.exp(gcum[..., -1])
    xs = (u, w, k_dec, g_last)
    if emit:
        attn = jnp.einsum('nbhid,nbhjd->nbhij', qc, kc) * decay
        q_dec = qc * jnp.exp(gcum)[..., None]
        xs = xs + (attn, q_dec)

    def step(state, xs_c):
        u_c, w_c, kd_c, gl = xs_c[:4]
        u_new = u_c - jnp.einsum('bhcd,bhde->bhce', w_c, state)
        s_new = state * gl[..., None, None] + jnp.einsum('bhcd,bhce->bhde', kd_c, u_new)
        if emit:
            a_c, qd_c = xs_c[4:]
            o = jnp.einsum('bhcd,bhde->bhce', qd_c, state) + jnp.einsum('bhij,bhje->bhie', a_c, u_new)
            return s_new, o
        return s_new, None

    s_final, o = lax.scan(step, s0, xs)
    if emit:
        o = jnp.moveaxis(o, (0, 2), (1, 3)).reshape(b, n, h, d)
    return o, s_final


def gdn_mixer(p, pc, conv_w, a_log, dt_bias, norm_g, need_ctx):
    q, k, v, gate, g, beta = gdn_inputs(p, conv_w, a_log, dt_bias)
    qc, kc, vc, gatec, gc, betac = gdn_inputs(pc, conv_w, a_log, dt_bias)
    b = p.shape[0]
    outs, outs_c = [], []
    for direction in range(2):
        flip = (lambda t: t[:, ::-1]) if direction == 1 else (lambda t: t)
        s0 = jnp.zeros((b, GDN_HEADS, HEAD_DIM, HEAD_DIM), jnp.float32)
        oc, s_ctx = gdn_chunked(flip(qc), flip(kc), flip(vc), flip(gc[:, :, direction]),
                                flip(betac[:, :, direction]), s0, need_ctx)
        o, _ = gdn_chunked(flip(q), flip(k), flip(v), flip(g[:, :, direction]),
                           flip(beta[:, :, direction]), s_ctx, True)
        outs.append(flip(o))
        if need_ctx:
            outs_c.append(flip(oc))

    def finish(o, gt):
        bb, n = o.shape[:2]
        return (rmsnorm(o, norm_g).reshape(bb, n, GDN_W) * jax.nn.silu(gt)).astype(p.dtype)

    y = finish(outs[0] + outs[1], gate)
    yc = finish(outs_c[0] + outs_c[1], gatec) if need_ctx else None
    return y, yc


def moe_ffn(t, w_router, e_bias, w_gate, w_up, w_down, ws_gate, ws_up, ws_down):
    n_tok, d = t.shape
    scores = jax.nn.sigmoid(jnp.dot(t, w_router, preferred_element_type=jnp.float32))
    _, top_idx = lax.top_k(scores + e_bias.astype(jnp.float32), TOP_K)
    top_s = jnp.take_along_axis(scores, top_idx, axis=-1)
    top_w = top_s / jnp.sum(top_s, axis=-1, keepdims=True) * ROUTED_SCALE
    n_asg = n_tok * TOP_K
    flat_e = top_idx.reshape(n_asg)
    flat_w = top_w.reshape(n_asg)
    counts = jnp.bincount(flat_e, length=N_EXPERTS)
    padded = (counts + MOE_BLOCK - 1) // MOE_BLOCK * MOE_BLOCK
    pad_end = jnp.cumsum(padded)
    pad_start = pad_end - padded
    raw_start = jnp.cumsum(counts) - counts
    order = jnp.argsort(flat_e)
    sorted_e = flat_e[order]
    dest = pad_start[sorted_e] + jnp.arange(n_asg) - raw_start[sorted_e]
    n_blocks = -(-n_asg // MOE_BLOCK) + N_EXPERTS
    n_rows = n_blocks * MOE_BLOCK
    row_tok = jnp.full((n_rows,), n_tok, jnp.int32).at[dest].set((order // TOP_K).astype(jnp.int32))
    row_w = jnp.zeros((n_rows,), jnp.float32).at[dest].set(flat_w[order])
    blk_e = jnp.minimum(jnp.searchsorted(pad_end, jnp.arange(n_blocks) * MOE_BLOCK, side='right'), N_EXPERTS - 1)
    t_pad = jnp.concatenate([t, jnp.zeros((1, d), t.dtype)], axis=0)
    xb = t_pad[row_tok].reshape(n_blocks, MOE_BLOCK, d)

    def expert_block(args):
        xblk, e = args
        hid = jax.nn.silu(xblk @ w_gate[e]) * (xblk @ w_up[e])
        return hid @ w_down[e]

    yb = lax.map(expert_block, (xb, blk_e)).reshape(n_rows, d)
    routed = jax.ops.segment_sum(yb * row_w[:, None].astype(yb.dtype), row_tok, num_segments=n_tok + 1)[:n_tok]
    shared = (jax.nn.silu(t @ ws_gate) * (t @ ws_up)) @ ws_down
    return routed + shared


def setup_inputs(seed: int = 0) -> dict:
    key = jax.random.key(seed)
    ks = iter(jax.random.split(key, 40))
    f32 = jnp.float32

    def nrm(shape, scale):
        return scale * jax.random.normal(next(ks), shape, f32)

    nl, d = DEPTH, D_MODEL
    a_init = jax.random.uniform(next(ks), (nl, 2, GDN_HEADS), f32, 1.0, 16.0)
    dt = jnp.exp(jax.random.uniform(next(ks), (nl, 2, GDN_HEADS), f32, math.log(1e-3), math.log(1e-1)))
    return {
        'x': nrm((BATCH, SEQ, d), 1.0),
        'c': nrm((BATCH, d), 1.0),
        'ctx': nrm((BATCH, CTX_LEN, d), 1.0),
        'c_ctx': nrm((d,), 1.0),
        'w_ada': nrm((nl, d, 6 * d), 0.5 * d ** -0.5),
        'b_ada': nrm((nl, 6 * d), 0.02),
        'norm1': 1.0 + nrm((nl, d), 0.02),
        'norm2': 1.0 + nrm((nl, d), 0.02),
        'norm_f': 1.0 + nrm((d,), 0.02),
        'w_in': nrm((nl, d, D_IN), d ** -0.5),
        'w_out': nrm((nl, D_MIX, d), D_MIX ** -0.5),
        'hy_conv': nrm((nl, HY_SHORT, HY_IN), HY_SHORT ** -0.5),
        'hy_w1': nrm((nl, HY_EMB, HY_FFN), HY_EMB ** -0.5),
        'hy_b1': nrm((nl, HY_FFN), 0.1),
        'hy_w2': nrm((nl, HY_FFN, HY_FFN), HY_FFN ** -0.5),
        'hy_b2': nrm((nl, HY_FFN), 0.1),
        'hy_w3': nrm((nl, HY_FFN, 2 * HY_CH), HY_FILTER_SCALE * HY_FFN ** -0.5),
        'hy_freq': 1.0 + nrm((nl, HY_FFN), 0.1),
        'hy_bias': nrm((nl, HY_CH), 0.5),
        'wa_sink': nrm((nl, WA_HEADS), 0.5),
        'na_rpb': nrm((nl, NA_HEADS, 2 * NA_KR - 1, 2 * NA_KC - 1), 0.02),
        'gdn_conv': nrm((nl, GDN_CONV, 3 * GDN_W), GDN_CONV ** -0.5),
        'gdn_a_log': jnp.log(a_init),
        'gdn_dt_bias': dt + jnp.log(-jnp.expm1(-dt)),
        'gdn_norm': 1.0 + nrm((nl, HEAD_DIM), 0.02),
        'moe_router': nrm((nl, d, N_EXPERTS), d ** -0.5),
        'moe_bias': nrm((nl, N_EXPERTS), 0.01),
        'moe_gate': nrm((nl, N_EXPERTS, d, D_EXPERT), d ** -0.5),
        'moe_up': nrm((nl, N_EXPERTS, d, D_EXPERT), d ** -0.5),
        'moe_down': nrm((nl, N_EXPERTS, D_EXPERT, d), D_EXPERT ** -0.5),
        'sh_gate': nrm((nl, d, D_SHARED), d ** -0.5),
        'sh_up': nrm((nl, d, D_SHARED), d ** -0.5),
        'sh_down': nrm((nl, D_SHARED, d), D_SHARED ** -0.5),
    }


def reference(x, c, ctx, c_ctx, w_ada, b_ada, norm1, norm2, norm_f, w_in, w_out,
              hy_conv, hy_w1, hy_b1, hy_w2, hy_b2, hy_w3, hy_freq, hy_bias,
              wa_sink, na_rpb, gdn_conv, gdn_a_log, gdn_dt_bias, gdn_norm,
              moe_router, moe_bias, moe_gate, moe_up, moe_down, sh_gate, sh_up, sh_down):
    b, s, d = x.shape
    lc = ctx.shape[1]
    pos = jnp.arange(s)
    row, col = pos // GRID_W, pos % GRID_W
    xc = ctx
    silu_c = jax.nn.silu(c)
    silu_cc = jax.nn.silu(c_ctx)
    o1, o2, o3 = HY_IN, HY_IN + WA_IN, HY_IN + WA_IN + NA_IN
    for l in range(DEPTH):
        need_ctx = l < DEPTH - 1
        sh1, sc1, g1, sh2, sc2, g2 = [m[:, None, :] for m in jnp.split(silu_c @ w_ada[l] + b_ada[l], 6, axis=-1)]
        sh1c, sc1c, g1c, sh2c, sc2c, g2c = jnp.split(silu_cc @ w_ada[l] + b_ada[l], 6, axis=-1)
        h = rmsnorm(x, norm1[l]) * (1.0 + sc1) + sh1
        hc = rmsnorm(xc, norm1[l]) * (1.0 + sc1c) + sh1c
        p = h @ w_in[l]
        pc = hc @ w_in[l]
        hy_args = (hy_conv[l], hy_w1[l], hy_b1[l], hy_w2[l], hy_b2[l], hy_w3[l], hy_freq[l], hy_bias[l])
        y_hy = hyena_mixer(p[..., :o1], *hy_args)
        y_wa, yc_wa = window_attention(p[..., o1:o2], pc[..., o1:o2], wa_sink[l], row, col, need_ctx)
        y_na, yc_na = neighborhood_attention(p[..., o2:o3], pc[..., o2:o3], na_rpb[l], need_ctx)
        y_gd, yc_gd = gdn_mixer(p[..., o3:], pc[..., o3:], gdn_conv[l], gdn_a_log[l], gdn_dt_bias[l], gdn_norm[l], need_ctx)
        x = x + g1 * (jnp.concatenate([y_hy, y_wa, y_na, y_gd], axis=-1) @ w_out[l])
        moe_args = (moe_router[l], moe_bias[l], moe_gate[l], moe_up[l], moe_down[l], sh_gate[l], sh_up[l], sh_down[l])
        h2 = rmsnorm(x, norm2[l]) * (1.0 + sc2) + sh2
        if need_ctx:
            yc_hy = hyena_mixer(pc[..., :o1], *hy_args)
            xc = xc + g1c * (jnp.concatenate([yc_hy, yc_wa, yc_na, yc_gd], axis=-1) @ w_out[l])
            h2c = rmsnorm(xc, norm2[l]) * (1.0 + sc2c) + sh2c
            f = moe_ffn(jnp.concatenate([h2.reshape(b * s, d), h2c.reshape(b * lc, d)], axis=0), *moe_args)
            x = x + g2 * f[:b * s].reshape(b, s, d)
            xc = xc + g2c * f[b * s:].reshape(b, lc, d)
        else:
            x = x + g2 * moe_ffn(h2.reshape(b * s, d), *moe_args).reshape(b, s, d)
    return rmsnorm(x, norm_f)
```

```python
import functools
import math

import jax
import jax.numpy as jnp
from jax import lax
from jax.experimental import pallas as pl
from jax.experimental.pallas import tpu as pltpu

F32 = jnp.float32
BF16 = jnp.bfloat16
HI = lax.Precision.HIGHEST

D = 1024
B = 8
S = 2048
LC = 256
GW = 64
HD = 64
NL = B * S
NC = B * LC
ROWS = NL + NC
TM = 256
NT_LAT = NL // TM
NT_ALL = ROWS // TM
CTX_BLK0 = NL // LC

HY_CH = 256
HY_BANDS = 16
HY_DECAY_MIN = -math.log(1e-2) / 1.5
HY_DECAY_MAX = -math.log(1e-2) / 0.3
WINDOW = 128
NA_KR = 8
NA_KC = 16
CHUNK = 64
N_EXP = 32
TOP_K = 4
D_EXP = 256
ROUTED_SCALE = 2.5
MOE_BLK = 256
EPS = 1e-6
NEG = -1e30
SCALE = HD ** -0.5
VMEM_LIMIT = 56 * 1024 * 1024


def _cp(*sem):
    return pltpu.CompilerParams(dimension_semantics=tuple(sem), vmem_limit_bytes=VMEM_LIMIT)


def _dot(a, b, precision=None):
    return jnp.dot(a, b, preferred_element_type=F32, precision=precision)


def _dot_nt(a, b, precision=None):
    return lax.dot_general(a, b, (((1,), (1,)), ((), ())), preferred_element_type=F32, precision=precision)


def _dot_tn(a, b, precision=None):
    return lax.dot_general(a, b, (((0,), (0,)), ((), ())), preferred_element_type=F32, precision=precision)


def _sigmoid(x):
    return 1.0 / (1.0 + jnp.exp(-x))


def _silu(x):
    return x * _sigmoid(x)


def _softplus(x):
    return jnp.maximum(x, 0.0) + jnp.log(1.0 + jnp.exp(-jnp.abs(x)))


def _shift_rows(x, d):
    n = x.shape[0]
    if d == 0:
        return x
    y = pltpu.roll(x, (-d) % n, axis=0)
    t = lax.broadcasted_iota(jnp.int32, x.shape, 0)
    ok = (t + d >= 0) & (t + d < n)
    return jnp.where(ok, y, 0.0)


def _ada_kernel(c_ref, w_ref, b_ref, o_ref):
    s = _silu(c_ref[...])
    o_ref[0] = _dot(s.astype(BF16), w_ref[0].astype(BF16)) + b_ref[0]


def _ada(cvec, w_ada, b_ada):
    nl = w_ada.shape[0]
    tn = 1536
    return pl.pallas_call(
        _ada_kernel,
        grid=(nl, 6 * D // tn),
        in_specs=[
            pl.BlockSpec((16, D), lambda l, j: (0, 0)),
            pl.BlockSpec((1, D, tn), lambda l, j: (l, 0, j)),
            pl.BlockSpec((1, 1, tn), lambda l, j: (l, 0, j)),
        ],
        out_specs=pl.BlockSpec((1, 16, tn), lambda l, j: (l, 0, j)),
        out_shape=jax.ShapeDtypeStruct((nl, 16, 6 * D), F32),
        compiler_params=_cp("arbitrary", "arbitrary"),
        name="adaln",
    )(cvec, w_ada, b_ada.reshape(nl, 1, 6 * D))


def _modulated_norm(x, g, shift, scale):
    ms = jnp.mean(x * x, axis=-1, keepdims=True)
    y = x * lax.rsqrt(ms + EPS) * g
    return y * (1.0 + scale) + shift


def _inproj_kernel(x_ref, mod_ref, g_ref, why_ref, wwa_ref, wna_ref, wgd_ref, wab_ref,
                   cos_ref, sa_ref, sb_ref, ohy, owa, ona, ogd, oab):
    m = mod_ref[0]
    h = _modulated_norm(x_ref[...], g_ref[...], m[0:1], m[1:2]).astype(BF16)
    ohy[...] = _dot(h, why_ref[...]).astype(ohy.dtype)
    ona[...] = _dot(h, wna_ref[...]).astype(ona.dtype)
    ogd[...] = _dot(h, wgd_ref[...]).astype(ogd.dtype)
    oab[...] = _dot(h, wab_ref[...])
    a = _dot(h, wwa_ref[...])
    for c in range(4):
        sl = slice(c * 128, (c + 1) * 128)
        ac = a[:, sl]
        r = (ac * cos_ref[:, sl] + pltpu.roll(ac, 112, axis=1) * sa_ref[:, sl]
             + pltpu.roll(ac, 16, axis=1) * sb_ref[:, sl])
        owa[:, sl] = r.astype(owa.dtype)


def _inproj(xa, mod, l, g, ws, tabs):
    why, wwa, wna, wgd, wab = ws
    cos_t, sa_t, sb_t = tabs
    row = lambda i: (i, 0)
    const = lambda i: (0, 0)
    tab = lambda i: (jnp.where(i < NT_LAT, i % (S // TM), S // TM), 0)
    return pl.pallas_call(
        _inproj_kernel,
        grid=(NT_ALL,),
        in_specs=[
            pl.BlockSpec((TM, D), row),
            pl.BlockSpec((1, 6, D), lambda i: (l * 16 + (i * TM) // S, 0, 0)),
            pl.BlockSpec((1, D), const),
            pl.BlockSpec((D, 768), const),
            pl.BlockSpec((D, 512), const),
            pl.BlockSpec((D, 768), const),
            pl.BlockSpec((D, 1024), const),
            pl.BlockSpec((D, 128), const),
            pl.BlockSpec((TM, 512), tab),
            pl.BlockSpec((TM, 512), tab),
            pl.BlockSpec((TM, 512), tab),
        ],
        out_specs=[
            pl.BlockSpec((TM, 768), row),
            pl.BlockSpec((TM, 512), row),
            pl.BlockSpec((TM, 768), row),
            pl.BlockSpec((TM, 1024), row),
            pl.BlockSpec((TM, 128), row),
        ],
        out_shape=[
            jax.ShapeDtypeStruct((ROWS, 768), BF16),
            jax.ShapeDtypeStruct((ROWS, 512), BF16),
            jax.ShapeDtypeStruct((ROWS, 768), BF16),
            jax.ShapeDtypeStruct((ROWS, 1024), BF16),
            jax.ShapeDtypeStruct((ROWS, 128), F32),
        ],
        compiler_params=_cp("arbitrary"),
        name="inproj",
    )(xa, mod, g, why, wwa, wna, wgd, wab, cos_t, sa_t, sb_t)


def _rope_tables():
    quarter = HD // 4
    pos = jnp.arange(S)
    inv = 10000.0 ** (-jnp.arange(quarter, dtype=F32) / quarter)
    ang_r = (pos // GW).astype(F32)[:, None] * inv[None, :]
    ang_c = (pos % GW).astype(F32)[:, None] * inv[None, :]
    z = jnp.zeros_like(ang_r)
    cos_h = jnp.concatenate([jnp.cos(ang_r)] * 2 + [jnp.cos(ang_c)] * 2, axis=1)
    sa_h = jnp.concatenate([-jnp.sin(ang_r), z, -jnp.sin(ang_c), z], axis=1)
    sb_h = jnp.concatenate([z, jnp.sin(ang_r), z, jnp.sin(ang_c)], axis=1)
    ones = jnp.ones((S, 128), F32)
    zeros = jnp.zeros((S, 128), F32)
    cos_t = jnp.concatenate([jnp.tile(cos_h, (1, 6)), ones], axis=1)
    sa_t = jnp.concatenate([jnp.tile(sa_h, (1, 6)), zeros], axis=1)
    sb_t = jnp.concatenate([jnp.tile(sb_h, (1, 6)), zeros], axis=1)
    ident = jnp.ones((TM, 512), F32)
    none = jnp.zeros((TM, 512), F32)
    return (jnp.concatenate([cos_t, ident], axis=0), jnp.concatenate([sa_t, none], axis=0),
            jnp.concatenate([sb_t, none], axis=0))


def _dft_tables(length):
    n = 2 * length
    f = jnp.arange(length, dtype=jnp.int32)
    m = ((2 * f[:, None] + 1) * (2 * f[None, :] + 1)) % (4 * n)
    th = m.astype(F32) * (2.0 * math.pi / (4 * n))
    phi = (2 * f + 1).astype(F32) * (math.pi / (2 * n))
    return (jnp.cos(th).astype(BF16), jnp.sin(th).astype(BF16),
            jnp.cos(phi)[:, None], jnp.sin(phi)[:, None])


def _hyena_features(length):
    t = jnp.arange(length, dtype=F32)
    t_norm = t / max(length - 1, 1)
    bands = jnp.linspace(1e-4, HY_BANDS - 1, HY_BANDS, dtype=F32)
    ang = (2.0 * math.pi / length) * t[:, None] * bands[None, :]
    z = jnp.concatenate([t_norm[:, None], jnp.cos(ang), -jnp.sin(ang)], axis=-1)
    z = jnp.pad(z, ((0, 0), (0, 128 - z.shape[1])))
    decay = jnp.tile(jnp.linspace(HY_DECAY_MIN, HY_DECAY_MAX, HY_CH, dtype=F32), 2)
    return z, jnp.exp(-t_norm[:, None] * decay[None, :])


def _hyfilt_kernel(z_ref, w1_ref, b1_ref, w2_ref, b2_ref, w3_ref, fr_ref, dec_ref,
                   c_ref, s_ref, cp_ref, sp_ref, kre_ref, kim_ref, pq_ref, *, length):
    @pl.when(pl.program_id(0) == 0)
    def _():
        fr = fr_ref[...]
        h = jnp.sin(fr * (_dot(z_ref[...], w1_ref[...], HI) + b1_ref[...]))
        h = jnp.sin(fr * (_dot(h, w2_ref[...], HI) + b2_ref[...]))
        h = _dot(h, w3_ref[...], HI) * dec_ref[...]
        hf = h[:, :HY_CH]
        t = lax.broadcasted_iota(jnp.int32, (length, HY_CH), 0)
        hb = jnp.where(t == 0, 0.0, h[:, HY_CH:])
        pq_ref[:, :HY_CH] = (hf + hb).astype(BF16)
        pq_ref[:, HY_CH:] = (hb - hf).astype(BF16)

    pq = pq_ref[...]
    cpq = _dot(c_ref[...], pq)
    spq = _dot(s_ref[...], pq)
    cphi, sphi = cp_ref[...], sp_ref[...]
    norm = 1.0 / length
    kre_ref[...] = (cphi * cpq[:, :HY_CH] + sphi * spq[:, :HY_CH]) * norm
    kim_ref[...] = (cphi * spq[:, HY_CH:] - sphi * cpq[:, HY_CH:]) * norm


def _hyena_filter_spectrum(length, w1, b1, w2, b2, w3, freq, dft):
    z, dec = _hyena_features(length)
    c_m, s_m, cphi, sphi = dft
    w1p = jnp.pad(w1, ((0, 128 - w1.shape[0]), (0, 0)))
    tf = min(512, length)
    const = lambda j: (0, 0)
    blk = lambda j: (j, 0)
    return pl.pallas_call(
        functools.partial(_hyfilt_kernel, length=length),
        grid=(length // tf,),
        in_specs=[
            pl.BlockSpec((length, 128), const),
            pl.BlockSpec((128, 64), const),
            pl.BlockSpec((1, 64), const),
            pl.BlockSpec((64, 64), const),
            pl.BlockSpec((1, 64), const),
            pl.BlockSpec((64, 2 * HY_CH), const),
            pl.BlockSpec((1, 64), const),
            pl.BlockSpec((length, 2 * HY_CH), const),
            pl.BlockSpec((tf, length), blk),
            pl.BlockSpec((tf, length), blk),
            pl.BlockSpec((tf, 1), blk),
            pl.BlockSpec((tf, 1), blk),
        ],
        out_specs=[pl.BlockSpec((tf, HY_CH), blk)] * 2,
        out_shape=[jax.ShapeDtypeStruct((length, HY_CH), F32)] * 2,
        scratch_shapes=[pltpu.VMEM((length, 2 * HY_CH), BF16)],
        compiler_params=_cp("arbitrary"),
        name=f"hyena_filter_{length}",
    )(z, w1p, b1[None, :], w2, b2[None, :], w3, freq[None, :], dec, c_m, s_m, cphi, sphi)


def _short_conv3(x_ref, cw):
    x = x_ref[...].astype(F32)
    return _shift_rows(x, -1) * cw[0:1] + x * cw[1:2] + _shift_rows(x, 1) * cw[2:3]


def _hyena_kernel(p0_ref, p1_ref, p2_ref, cw_ref, bias_ref, c_ref, s_ref, kre_ref, kim_ref, *rest):
    o_ref = rest[-1]
    cw = cw_ref[...]
    x0 = _short_conv3(p0_ref, cw[:, :HY_CH])
    k = _short_conv3(p1_ref, cw[:, HY_CH:2 * HY_CH]) * _short_conv3(p2_ref, cw[:, 2 * HY_CH:])
    kb = k.astype(BF16)
    a = _dot(c_ref[...], kb)
    b = _dot(s_ref[...], kb)
    kre, kim = kre_ref[...], kim_ref[...]
    yre = (a * kre + b * kim).astype(BF16)
    yim = (b * kre - a * kim).astype(BF16)
    y = _dot(c_ref[...], yre) + _dot(s_ref[...], yim)
    o_ref[...] = (x0 * (y + k * bias_ref[...])).astype(o_ref.dtype)


def _hyena(p_hy, y_prev, length, blk0, conv_w, bias, dft, kre, kim):
    c_m, s_m = dft[0], dft[1]
    const = lambda b: (0, 0)
    once = pl.Buffered(1)
    in_specs = [
        pl.BlockSpec((length, HY_CH), lambda b: (blk0 + b, 0)),
        pl.BlockSpec((length, HY_CH), lambda b: (blk0 + b, 1)),
        pl.BlockSpec((length, HY_CH), lambda b: (blk0 + b, 2)),
        pl.BlockSpec((3, 768), const),
        pl.BlockSpec((1, HY_CH), const),
        pl.BlockSpec((length, length), const, pipeline_mode=once),
        pl.BlockSpec((length, length), const, pipeline_mode=once),
        pl.BlockSpec((length, HY_CH), const, pipeline_mode=once),
        pl.BlockSpec((length, HY_CH), const, pipeline_mode=once),
    ]
    args = [p_hy, p_hy, p_hy, conv_w, bias[None, :], c_m, s_m, kre, kim]
    aliases = {}
    if y_prev is not None:
        in_specs.append(pl.BlockSpec(memory_space=pl.ANY))
        args.append(y_prev)
        aliases = {9: 0}
    return pl.pallas_call(
        _hyena_kernel,
        grid=(B,),
        in_specs=in_specs,
        out_specs=pl.BlockSpec((length, HY_CH), lambda b: (blk0 + b, 0)),
        out_shape=jax.ShapeDtypeStruct((ROWS, HY_CH), BF16),
        input_output_aliases=aliases,
        compiler_params=_cp("arbitrary"),
        name=f"hyena_{length}",
    )(*args)


def _softmax_pv(parts, extra_logit=None):
    m = None
    for s, _ in parts:
        mi = jnp.max(s, axis=-1, keepdims=True)
        m = mi if m is None else jnp.maximum(m, mi)
    if extra_logit is not None:
        m = jnp.maximum(m, extra_logit)
    den = 0.0 if extra_logit is None else jnp.exp(extra_logit - m)
    acc = None
    for s, v in parts:
        e = jnp.exp(s - m)
        den = den + jnp.sum(e, axis=-1, keepdims=True)
        o = _dot(e.astype(BF16), v)
        acc = o if acc is None else acc + o
    return acc / den


def _wa_kernel(sink_ref, q_ref, k_ref, v_ref, kc_ref, vc_ref, o_ref):
    n = pl.program_id(1)
    start = pl.multiple_of(jnp.clip((n - 1) * 128, 0, S - 384), 128)
    kw = k_ref[pl.ds(start, 384), :]
    vw = v_ref[pl.ds(start, 384), :]
    kc, vc = kc_ref[...], vc_ref[...]
    q = q_ref[...]
    rr = lax.broadcasted_iota(jnp.int32, (256, 384), 0)
    qpos = n * 128 + jnp.where(rr >= 128, rr - 128, rr)
    kpos = start + lax.broadcasted_iota(jnp.int32, (256, 384), 1)
    valid = jnp.abs(qpos - kpos) <= WINDOW
    r1 = lax.broadcasted_iota(jnp.int32, (256, 1), 0)
    for hk in range(2):
        q2 = jnp.concatenate([q[:, (2 * hk) * HD:(2 * hk + 1) * HD],
                              q[:, (2 * hk + 1) * HD:(2 * hk + 2) * HD]], axis=0)
        hs = slice(hk * HD, (hk + 1) * HD)
        s_loc = jnp.where(valid, _dot_nt(q2, kw[:, hs]) * SCALE, NEG)
        s_ctx = _dot_nt(q2, kc[:, hs]) * SCALE
        sink = jnp.where(r1 >= 128, sink_ref[2 * hk + 1], sink_ref[2 * hk])
        o = _softmax_pv([(s_ctx, vc[:, hs]), (s_loc, vw[:, hs])], sink)
        o_ref[:, (2 * hk) * HD:(2 * hk + 1) * HD] = o[:128].astype(o_ref.dtype)
        o_ref[:, (2 * hk + 1) * HD:(2 * hk + 2) * HD] = o[128:].astype(o_ref.dtype)


def _window_attention(p_wa, sink):
    nb = S // 128
    return pl.pallas_call(
        _wa_kernel,
        grid=(B, nb),
        in_specs=[
            pl.BlockSpec(memory_space=pltpu.SMEM),
            pl.BlockSpec((128, 256), lambda b, n: (b * nb + n, 0)),
            pl.BlockSpec((S, 128), lambda b, n: (b, 2)),
            pl.BlockSpec((S, 128), lambda b, n: (b, 3)),
            pl.BlockSpec((LC, 128), lambda b, n: (CTX_BLK0 + b, 2)),
            pl.BlockSpec((LC, 128), lambda b, n: (CTX_BLK0 + b, 3)),
        ],
        out_specs=pl.BlockSpec((128, 256), lambda b, n: (b * nb + n, 0)),
        out_shape=jax.ShapeDtypeStruct((ROWS, 256), BF16),
        compiler_params=_cp("arbitrary", "arbitrary"),
        name="window_attn",
    )(sink, p_wa, p_wa, p_wa, p_wa, p_wa)


def _na_kernel(q_ref, k_ref, v_ref, kc_ref, vc_ref, bias_ref, o_ref):
    r = pl.program_id(1)
    start = pl.multiple_of(jnp.clip(r - NA_KR // 2, 0, S // GW - NA_KR) * GW, GW)
    kw = k_ref[pl.ds(start, NA_KR * GW), :]
    vw = v_ref[pl.ds(start, NA_KR * GW), :]
    kc, vc = kc_ref[...], vc_ref[...]
    q = q_ref[...]
    for h in range(4):
        hs = slice(h * HD, (h + 1) * HD)
        s_loc = _dot_nt(q[:, hs], kw[:, hs]) * SCALE + bias_ref[h, 0]
        s_ctx = _dot_nt(q[:, hs], kc[:, hs]) * SCALE
        o = _softmax_pv([(s_ctx, vc[:, hs]), (s_loc, vw[:, hs])])
        o_ref[:, hs] = o.astype(o_ref.dtype)


def _na_bias_table(rpb):
    o = jnp.arange(NA_KR)[:, None, None, None]
    j = jnp.arange(NA_KR)[None, None, :, None]
    qc = jnp.arange(GW)[None, :, None, None]
    kcol = jnp.arange(GW)[None, None, None, :]
    ri = jnp.broadcast_to(o + j, (NA_KR, GW, NA_KR, GW))
    ci = jnp.broadcast_to(jnp.clip(kcol - qc + NA_KC - 1, 0, 2 * NA_KC - 2), (NA_KR, GW, NA_KR, GW))
    c_start = jnp.clip(qc - NA_KC // 2, 0, GW - NA_KC)
    ok = jnp.broadcast_to((kcol >= c_start) & (kcol < c_start + NA_KC), (NA_KR, GW, NA_KR, GW))
    tbl = jnp.where(ok[None], rpb.astype(F32)[:, ri, ci], NEG)
    return tbl.reshape(4, NA_KR, GW, NA_KR * GW)


def _neighborhood_attention(p_na, bias_tbl):
    rows = S // GW

    def off(r):
        return jnp.clip(r - NA_KR // 2, 0, rows - NA_KR) - r + NA_KR - 1

    return pl.pallas_call(
        _na_kernel,
        grid=(B, rows),
        in_specs=[
            pl.BlockSpec((GW, 256), lambda b, r: (b * rows + r, 0)),
            pl.BlockSpec((S, 256), lambda b, r: (b, 1)),
            pl.BlockSpec((S, 256), lambda b, r: (b, 2)),
            pl.BlockSpec((LC, 256), lambda b, r: (CTX_BLK0 + b, 1)),
            pl.BlockSpec((LC, 256), lambda b, r: (CTX_BLK0 + b, 2)),
            pl.BlockSpec((4, 1, GW, NA_KR * GW), lambda b, r: (0, off(r), 0, 0)),
        ],
        out_specs=pl.BlockSpec((GW, 256), lambda b, r: (b * rows + r, 0)),
        out_shape=jax.ShapeDtypeStruct((ROWS, 256), BF16),
        compiler_params=_cp("arbitrary", "arbitrary"),
        name="neighborhood_attn",
    )(p_na, p_na, p_na, p_na, p_na, bias_tbl)


def _ctx_attn_kernel(sink_ref, qw_ref, kw_ref, vw_ref, qn_ref, kn_ref, vn_ref, ywa_in, yna_in, owa_ref, ona_ref):
    del ywa_in, yna_in
    q, k, v = qw_ref[...], kw_ref[...], vw_ref[...]
    r1 = lax.broadcasted_iota(jnp.int32, (2 * LC, 1), 0)
    for hk in range(2):
        q2 = jnp.concatenate([q[:, (2 * hk) * HD:(2 * hk + 1) * HD],
                              q[:, (2 * hk + 1) * HD:(2 * hk + 2) * HD]], axis=0)
        hs = slice(hk * HD, (hk + 1) * HD)
        sink = jnp.where(r1 >= LC, sink_ref[2 * hk + 1], sink_ref[2 * hk])
        o = _softmax_pv([(_dot_nt(q2, k[:, hs]) * SCALE, v[:, hs])], sink)
        owa_ref[:, (2 * hk) * HD:(2 * hk + 1) * HD] = o[:LC].astype(owa_ref.dtype)
        owa_ref[:, (2 * hk + 1) * HD:(2 * hk + 2) * HD] = o[LC:].astype(owa_ref.dtype)
    q, k, v = qn_ref[...], kn_ref[...], vn_ref[...]
    for h in range(4):
        hs = slice(h * HD, (h + 1) * HD)
        o = _softmax_pv([(_dot_nt(q[:, hs], k[:, hs]) * SCALE, v[:, hs])])
        ona_ref[:, hs] = o.astype(ona_ref.dtype)


def _ctx_attention(p_wa, p_na, sink, y_wa, y_na):
    blk = lambda c: (lambda b: (CTX_BLK0 + b, c))
    return pl.pallas_call(
        _ctx_attn_kernel,
        grid=(B,),
        in_specs=[
            pl.BlockSpec(memory_space=pltpu.SMEM),
            pl.BlockSpec((LC, 256), blk(0)),
            pl.BlockSpec((LC, 128), blk(2)),
            pl.BlockSpec((LC, 128), blk(3)),
            pl.BlockSpec((LC, 256), blk(0)),
            pl.BlockSpec((LC, 256), blk(1)),
            pl.BlockSpec((LC, 256), blk(2)),
            pl.BlockSpec(memory_space=pl.ANY),
            pl.BlockSpec(memory_space=pl.ANY),
        ],
        out_specs=[pl.BlockSpec((LC, 256), blk(0)), pl.BlockSpec((LC, 256), blk(0))],
        out_shape=[jax.ShapeDtypeStruct((ROWS, 256), BF16)] * 2,
        input_output_aliases={7: 0, 8: 1},
        compiler_params=_cp("arbitrary"),
        name="ctx_attn",
    )(sink, p_wa, p_wa, p_wa, p_na, p_na, p_na, y_wa, y_na)


def _gdn_prep_kernel(p_ref, ab_ref, cw_ref, alog_ref, dtb_ref, qkv_ref, gb_ref):
    x = p_ref[:, :768].astype(F32)
    cw = cw_ref[...]
    u = (_shift_rows(x, -2) * cw[0:1] + _shift_rows(x, -1) * cw[1:2] + x * cw[2:3]
         + _shift_rows(x, 1) * cw[3:4])
    u = _silu(u)
    for j in range(8):
        sl = slice(j * HD, (j + 1) * HD)
        xs = u[:, sl]
        nrm = lax.rsqrt(jnp.sum(xs * xs, axis=-1, keepdims=True) + EPS)
        qkv_ref[:, sl] = xs * (nrm * SCALE if j < 4 else nrm)
    qkv_ref[:, 512:768] = u[:, 512:768]
    ab = ab_ref[...]
    g = -jnp.exp(alog_ref[...]) * _softplus(ab + dtb_ref[...])
    lane = lax.broadcasted_iota(jnp.int32, ab.shape, 1)
    gb_ref[...] = jnp.where(lane < 8, g, _sigmoid(ab))


def _gdn_prep(p_gd, p_ab, length, blk0, conv_w, a_log, dt_bias, prev):
    const = lambda b: (0, 0)
    alog = jnp.pad(a_log.reshape(1, 8), ((0, 0), (0, 120)))
    dtb = jnp.pad(dt_bias.reshape(1, 8), ((0, 0), (0, 120)))
    in_specs = [
        pl.BlockSpec((length, 1024), lambda b: (blk0 + b, 0)),
        pl.BlockSpec((length, 128), lambda b: (blk0 + b, 0)),
        pl.BlockSpec((4, 768), const),
        pl.BlockSpec((1, 128), const),
        pl.BlockSpec((1, 128), const),
    ]
    args = [p_gd, p_ab, conv_w, alog, dtb]
    aliases = {}
    if prev is not None:
        in_specs += [pl.BlockSpec(memory_space=pl.ANY)] * 2
        args += list(prev)
        aliases = {5: 0, 6: 1}

    def body(p_ref, ab_ref, cw_ref, alog_ref, dtb_ref, *rest):
        _gdn_prep_kernel(p_ref, ab_ref, cw_ref, alog_ref, dtb_ref, rest[-2], rest[-1])

    return pl.pallas_call(
        body,
        grid=(B,),
        in_specs=in_specs,
        out_specs=[pl.BlockSpec((length, 768), lambda b: (blk0 + b, 0)),
                   pl.BlockSpec((length, 128), lambda b: (blk0 + b, 0))],
        out_shape=[jax.ShapeDtypeStruct((ROWS, 768), F32), jax.ShapeDtypeStruct((ROWS, 128), F32)],
        input_output_aliases=aliases,
        compiler_params=_cp("arbitrary"),
        name=f"gdn_prep_{length}",
    )(*args)


def _gdn_kernel(q_ref, k_ref, v_ref, gb_ref, o_ref, state_ref):
    d = pl.program_id(1)
    c = pl.program_id(2)

    @pl.when(c == 0)
    def _():
        state_ref[...] = jnp.zeros_like(state_ref)

    ii = lax.broadcasted_iota(jnp.int32, (CHUNK, CHUNK), 0)
    jj = lax.broadcasted_iota(jnp.int32, (CHUNK, CHUNK), 1)
    tri = jj <= ii
    strict = jj < ii
    eye = (ii == jj).astype(F32)
    perm = jnp.where(d == 1, (ii + jj == CHUNK - 1).astype(F32), eye)
    q = _dot(perm, q_ref[...], HI)
    k = _dot(perm, k_ref[...], HI)
    v = _dot(perm, v_ref[...], HI)
    gb = _dot(perm, gb_ref[...], HI)
    gcum_all = _dot(tri.astype(F32), gb, HI)
    first_col = (jj == 0).astype(F32)
    outs = []
    for h in range(4):
        hs = slice(h * HD, (h + 1) * HD)
        gc = jnp.where(d == 1, gcum_all[:, 4 + h:5 + h], gcum_all[:, h:h + 1])
        beta = jnp.where(d == 1, gb[:, 12 + h:13 + h], gb[:, 8 + h:9 + h])
        gc_b = jnp.broadcast_to(gc, (CHUNK, CHUNK))
        gc_row = _dot_nt(first_col, gc_b, HI)
        decay = jnp.exp(jnp.where(tri, gc_b - gc_row, NEG))
        qh, kh, vh = q[:, hs], k[:, hs], v[:, hs]
        kb = kh * beta
        neg_l = jnp.where(strict, -(_dot_nt(kb, kh, HI) * decay), 0.0)
        t = eye + neg_l
        pw = neg_l
        for _ in range(5):
            pw = _dot(pw, pw, HI)
            t = t + _dot(t, pw, HI)
        eg = jnp.exp(gc)
        uw = _dot(t, jnp.concatenate([vh * beta, kb * eg], axis=1), HI)
        u, w = uw[:, :HD], uw[:, HD:]
        g_last = gc[CHUNK - 1:CHUNK, :]
        k_dec = kh * jnp.exp(g_last - gc)
        attn = _dot_nt(qh, kh, HI) * decay
        st = state_ref[h]
        u_new = u - _dot(w, st, HI)
        o = _dot(qh * eg, st, HI) + _dot(attn, u_new, HI)
        state_ref[h] = st * jnp.exp(g_last) + _dot_tn(k_dec, u_new, HI)
        outs.append(o)
    o_ref[0] = _dot(perm, jnp.concatenate(outs, axis=1), HI)


def _gdn_scan(qkv, gb):
    nc_ctx = LC // CHUNK
    nc_lat = S // CHUNK
    ctx0 = NL // CHUNK

    def blk(b, d, c):
        ctx_blk = ctx0 + b * nc_ctx + jnp.where(d == 1, nc_ctx - 1 - c, c)
        cl = c - nc_ctx
        lat_blk = b * nc_lat + jnp.where(d == 1, nc_lat - 1 - cl, cl)
        return jnp.where(c < nc_ctx, ctx_blk, lat_blk)

    return pl.pallas_call(
        _gdn_kernel,
        grid=(B, 2, nc_ctx + nc_lat),
        in_specs=[
            pl.BlockSpec((CHUNK, 256), lambda b, d, c: (blk(b, d, c), 0)),
            pl.BlockSpec((CHUNK, 256), lambda b, d, c: (blk(b, d, c), 1)),
            pl.BlockSpec((CHUNK, 256), lambda b, d, c: (blk(b, d, c), 2)),
            pl.BlockSpec((CHUNK, 128), lambda b, d, c: (blk(b, d, c), 0)),
        ],
        out_specs=pl.BlockSpec((1, CHUNK, 256), lambda b, d, c: (d, blk(b, d, c), 0)),
        out_shape=jax.ShapeDtypeStruct((2, ROWS, 256), F32),
        scratch_shapes=[pltpu.VMEM((4, HD, HD), F32)],
        compiler_params=_cp("arbitrary", "arbitrary", "arbitrary"),
        name="gdn_scan",
    )(qkv, qkv, qkv, gb)


def _gdn_finish_kernel(o_ref, gate_ref, ng_ref, y_ref):
    o = o_ref[0] + o_ref[1]
    gi = lax.broadcasted_iota(jnp.int32, (256, 256), 0) // HD
    gj = lax.broadcasted_iota(jnp.int32, (256, 256), 1) // HD
    group_mean = jnp.where(gi == gj, 1.0 / HD, 0.0)
    ms = _dot(o * o, group_mean, HI)
    y = o * lax.rsqrt(ms + EPS) * ng_ref[...]
    y_ref[...] = (y * _silu(gate_ref[...].astype(F32))).astype(y_ref.dtype)


def _gdn_finish(o2, p_gd, norm_g, n_tiles):
    return pl.pallas_call(
        _gdn_finish_kernel,
        grid=(n_tiles,),
        in_specs=[
            pl.BlockSpec((2, TM, 256), lambda i: (0, i, 0)),
            pl.BlockSpec((TM, 256), lambda i: (i, 3)),
            pl.BlockSpec((1, 256), lambda i: (0, 0)),
        ],
        out_specs=pl.BlockSpec((TM, 256), lambda i: (i, 0)),
        out_shape=jax.ShapeDtypeStruct((ROWS, 256), BF16),
        compiler_params=_cp("arbitrary"),
        name="gdn_finish",
    )(o2, p_gd, jnp.tile(norm_g, 4)[None, :])


def _outproj_kernel(x_ref, mod_ref, g2_ref, yh_ref, yw_ref, yn_ref, yg_ref, wo_ref, wr_ref, eb_ref,
                    xo_ref, h2_ref, idx_ref, tw_ref):
    m = mod_ref[0]
    acc = (_dot(yh_ref[...], wo_ref[0:256, :]) + _dot(yw_ref[...], wo_ref[256:512, :])
           + _dot(yn_ref[...], wo_ref[512:768, :]) + _dot(yg_ref[...], wo_ref[768:1024, :]))
    x = x_ref[...] + m[2:3] * acc
    xo_ref[...] = x
    h2 = _modulated_norm(x, g2_ref[...], m[3:4], m[4:5])
    h2_ref[...] = h2.astype(h2_ref.dtype)
    scores = _sigmoid(_dot(h2, wr_ref[...], HI))
    lane = lax.broadcasted_iota(jnp.int32, scores.shape, 1)
    sel = jnp.where(lane < N_EXP, scores + eb_ref[...], -jnp.inf)
    lane_f = lane.astype(F32)
    idx_out = jnp.zeros(scores.shape, F32)
    s_out = jnp.zeros(scores.shape, F32)
    for kk in range(TOP_K):
        mx = jnp.max(sel, axis=-1, keepdims=True)
        idx = jnp.min(jnp.where(sel == mx, lane_f, 128.0), axis=-1, keepdims=True)
        hit = lane_f == idx
        sk = jnp.sum(jnp.where(hit, scores, 0.0), axis=-1, keepdims=True)
        sel = jnp.where(hit, -jnp.inf, sel)
        idx_out = jnp.where(lane == kk, idx, idx_out)
        s_out = jnp.where(lane == kk, sk, s_out)
    tot = jnp.sum(s_out, axis=-1, keepdims=True)
    idx_ref[...] = idx_out.astype(jnp.int32)
    tw_ref[...] = s_out / tot * ROUTED_SCALE


def _outproj(xa, mod, l, g2, ys, wo, wr, eb, n_tiles):
    row = lambda i: (i, 0)
    const = lambda i: (0, 0)
    rows = n_tiles * TM
    return pl.pallas_call(
        _outproj_kernel,
        grid=(n_tiles,),
        in_specs=[
            pl.BlockSpec((TM, D), row),
            pl.BlockSpec((1, 6, D), lambda i: (l * 16 + (i * TM) // S, 0, 0)),
            pl.BlockSpec((1, D), const),
            pl.BlockSpec((TM, 256), row),
            pl.BlockSpec((TM, 256), row),
            pl.BlockSpec((TM, 256), row),
            pl.BlockSpec((TM, 256), row),
            pl.BlockSpec((D, D), const),
            pl.BlockSpec((D, 128), const),
            pl.BlockSpec((1, 128), const),
        ],
        out_specs=[
            pl.BlockSpec((TM, D), row),
            pl.BlockSpec((TM, D), row),
            pl.BlockSpec((TM, 128), row),
            pl.BlockSpec((TM, 128), row),
        ],
        out_shape=[
            jax.ShapeDtypeStruct((rows, D), F32),
            jax.ShapeDtypeStruct((rows, D), BF16),
            jax.ShapeDtypeStruct((rows, 128), jnp.int32),
            jax.ShapeDtypeStruct((rows, 128), F32),
        ],
        compiler_params=_cp("arbitrary"),
        name="outproj_router",
    )(xa, mod, g2, *ys, wo, wr, eb)


def _moe_kernel(be_ref, x_ref, rw_ref, wg_ref, wu_ref, wd_ref, o_ref):
    del be_ref
    x = x_ref[...]
    g = _dot(x, wg_ref[0].astype(BF16))
    u = _dot(x, wu_ref[0].astype(BF16))
    hid = (_silu(g) * u).astype(BF16)
    y = _dot(hid, wd_ref[0].astype(BF16))
    o_ref[...] = (y * rw_ref[...]).astype(o_ref.dtype)


def _moe_experts(blk_e, xb, row_w, wg, wu, wd):
    n_rows = xb.shape[0]
    n_blocks = n_rows // MOE_BLK
    grid_spec = pltpu.PrefetchScalarGridSpec(
        num_scalar_prefetch=1,
        grid=(n_blocks,),
        in_specs=[
            pl.BlockSpec((MOE_BLK, D), lambda i, be: (i, 0)),
            pl.BlockSpec((MOE_BLK, 1), lambda i, be: (i, 0)),
            pl.BlockSpec((1, D, D_EXP), lambda i, be: (be[i], 0, 0)),
            pl.BlockSpec((1, D, D_EXP), lambda i, be: (be[i], 0, 0)),
            pl.BlockSpec((1, D_EXP, D), lambda i, be: (be[i], 0, 0)),
        ],
        out_specs=pl.BlockSpec((MOE_BLK, D), lambda i, be: (i, 0)),
    )
    return pl.pallas_call(
        _moe_kernel,
        grid_spec=grid_spec,
        out_shape=jax.ShapeDtypeStruct((n_rows, D), F32),
        compiler_params=_cp("arbitrary"),
        name="moe_experts",
    )(blk_e, xb, row_w, wg, wu, wd)


def _dispatch(top_idx, top_w):
    n_tok = top_idx.shape[0]
    n_asg = n_tok * TOP_K
    flat_e = top_idx.reshape(n_asg)
    flat_w = top_w.reshape(n_asg)
    counts = jnp.sum((flat_e[:, None] == jnp.arange(N_EXP)[None, :]).astype(jnp.int32), axis=0)
    padded = (counts + MOE_BLK - 1) // MOE_BLK * MOE_BLK
    pad_end = jnp.cumsum(padded)
    pad_start = pad_end - padded
    raw_start = jnp.cumsum(counts) - counts
    order = jnp.argsort(flat_e, stable=True)
    sorted_e = flat_e[order]
    dest = (pad_start[sorted_e] + jnp.arange(n_asg) - raw_start[sorted_e]).astype(jnp.int32)
    n_blocks = n_asg // MOE_BLK + N_EXP
    n_rows = n_blocks * MOE_BLK
    row_tok = jnp.zeros((n_rows,), jnp.int32).at[dest].set((order // TOP_K).astype(jnp.int32))
    row_w = jnp.zeros((n_rows,), F32).at[dest].set(flat_w[order])
    blk_e = jnp.minimum(jnp.searchsorted(pad_end, jnp.arange(n_blocks) * MOE_BLK, side='right'),
                        N_EXP - 1).astype(jnp.int32)
    pos = jnp.zeros((n_asg,), jnp.int32).at[order].set(dest)
    return row_tok, row_w, blk_e, pos.reshape(n_tok, TOP_K)


def _ffn_out_kernel(x_ref, mod_ref, h2_ref, r_ref, wg_ref, wu_ref, wd_ref, nf_ref, o_ref, *, final):
    m = mod_ref[0]
    h2 = h2_ref[...]
    hid = (_silu(_dot(h2, wg_ref[...])) * _dot(h2, wu_ref[...])).astype(BF16)
    x = x_ref[...] + m[5:6] * (r_ref[...] + _dot(hid, wd_ref[...]))
    if final:
        ms = jnp.mean(x * x, axis=-1, keepdims=True)
        x = x * lax.rsqrt(ms + EPS) * nf_ref[...]
    o_ref[...] = x


def _ffn_out(xn, mod, l, h2, routed, wg, wu, wd, norm_f, n_tiles, final):
    row = lambda i: (i, 0)
    const = lambda i: (0, 0)
    return pl.pallas_call(
        functools.partial(_ffn_out_kernel, final=final),
        grid=(n_tiles,),
        in_specs=[
            pl.BlockSpec((TM, D), row),
            pl.BlockSpec((1, 6, D), lambda i: (l * 16 + (i * TM) // S, 0, 0)),
            pl.BlockSpec((TM, D), row),
            pl.BlockSpec((TM, D), row),
            pl.BlockSpec((D, D_EXP), const),
            pl.BlockSpec((D, D_EXP), const),
            pl.BlockSpec((D_EXP, D), const),
            pl.BlockSpec((1, D), const),
        ],
        out_specs=pl.BlockSpec((TM, D), row),
        out_shape=jax.ShapeDtypeStruct((n_tiles * TM, D), F32),
        compiler_params=_cp("arbitrary"),
        name="shared_ffn_residual",
    )(xn, mod, h2, routed, wg, wu, wd, norm_f)


def kernel(x, c, ctx, c_ctx, w_ada, b_ada, norm1, norm2, norm_f, w_in, w_out, hy_conv, hy_w1, hy_b1, hy_w2, hy_b2, hy_w3, hy_freq, hy_bias, wa_sink, na_rpb, gdn_conv, gdn_a_log, gdn_dt_bias, gdn_norm, moe_router, moe_bias, moe_gate, moe_up, moe_down, sh_gate, sh_up, sh_down):
    depth = w_ada.shape[0]
    xa = jnp.concatenate([x.reshape(NL, D), ctx.reshape(NC, D)], axis=0)
    cvec = jnp.concatenate([c, c_ctx[None, :], jnp.zeros((16 - B - 1, D), F32)], axis=0)
    mod = _ada(cvec, w_ada, b_ada).reshape(depth * 16, 6, D)
    rope = _rope_tables()
    dft_lat = _dft_tables(S)
    dft_ctx = _dft_tables(LC)
    o1, o2, o3 = 768, 768 + 512, 768 + 512 + 768

    for l in range(depth):
        need_ctx = l < depth - 1
        n_tiles = NT_ALL if need_ctx else NT_LAT
        wl = w_in[l].astype(BF16)
        ws = (wl[:, :o1], wl[:, o1:o2], wl[:, o2:o3], wl[:, o3:o3 + 1024],
              jnp.pad(wl[:, o3 + 1024:], ((0, 0), (0, 128 - 16))))
        p_hy, p_wa, p_na, p_gd, p_ab = _inproj(xa, mod, l, norm1[l][None, :], ws, rope)

        filt = (hy_w1[l], hy_b1[l], hy_w2[l], hy_b2[l], hy_w3[l], hy_freq[l])
        kre, kim = _hyena_filter_spectrum(S, *filt, dft_lat)
        y_hy = _hyena(p_hy, None, S, 0, hy_conv[l], hy_bias[l], dft_lat, kre, kim)
        y_wa = _window_attention(p_wa, wa_sink[l])
        y_na = _neighborhood_attention(p_na, _na_bias_table(na_rpb[l]))
        if need_ctx:
            kre_c, kim_c = _hyena_filter_spectrum(LC, *filt, dft_ctx)
            y_hy = _hyena(p_hy, y_hy, LC, CTX_BLK0, hy_conv[l], hy_bias[l], dft_ctx, kre_c, kim_c)
            y_wa, y_na = _ctx_attention(p_wa, p_na, wa_sink[l], y_wa, y_na)
        prep = _gdn_prep(p_gd, p_ab, S, 0, gdn_conv[l], gdn_a_log[l], gdn_dt_bias[l], None)
        qkv, gb = _gdn_prep(p_gd, p_ab, LC, CTX_BLK0, gdn_conv[l], gdn_a_log[l], gdn_dt_bias[l], prep)
        y_gd = _gdn_finish(_gdn_scan(qkv, gb), p_gd, gdn_norm[l], n_tiles)

        wr = jnp.pad(moe_router[l], ((0, 0), (0, 128 - N_EXP)))
        eb = jnp.pad(moe_bias[l], (0, 128 - N_EXP))[None, :]
        xn, h2, top_idx, top_w = _outproj(xa, mod, l, norm2[l][None, :], (y_hy, y_wa, y_na, y_gd),
                                          w_out[l].astype(BF16), wr, eb, n_tiles)
        row_tok, row_w, blk_e, pos = _dispatch(top_idx[:, :TOP_K], top_w[:, :TOP_K])
        yb = _moe_experts(blk_e, h2[row_tok], row_w[:, None], moe_gate[l], moe_up[l], moe_down[l])
        routed = jnp.sum(yb[pos], axis=1)
        xa_new = _ffn_out(xn, mod, l, h2, routed, sh_gate[l].astype(BF16), sh_up[l].astype(BF16),
                          sh_down[l].astype(BF16), norm_f[None, :], n_tiles, final=not need_ctx)
        if need_ctx:
            xa = xa_new
        else:
            return xa_new.reshape(B, S, D)
```

```python
import functools
import math

import jax
import jax.numpy as jnp
from jax import lax
from jax.experimental import pallas as pl
from jax.experimental.pallas import tpu as pltpu

F32 = jnp.float32
BF16 = jnp.bfloat16
HI = lax.Precision.HIGHEST

D = 1024
B = 8
S = 2048
LC = 256
GW = 64
HD = 64
NL = B * S
NC = B * LC
ROWS = NL + NC
TM = 256
NT_LAT = NL // TM
NT_ALL = ROWS // TM
CTX_BLK0 = NL // LC

HY_CH = 256
HY_BANDS = 16
HY_DECAY_MIN = -math.log(1e-2) / 1.5
HY_DECAY_MAX = -math.log(1e-2) / 0.3
WINDOW = 128
NA_KR = 8
NA_KC = 16
CHUNK = 64
N_EXP = 32
TOP_K = 4
D_EXP = 256
ROUTED_SCALE = 2.5
MOE_BLK = 256
EPS = 1e-6
NEG = -1e30
SCALE = HD ** -0.5
VMEM_LIMIT = 56 * 1024 * 1024


def _cp(*sem):
    return pltpu.CompilerParams(dimension_semantics=tuple(sem), vmem_limit_bytes=VMEM_LIMIT)


def _dot(a, b, precision=None):
    return jnp.dot(a, b, preferred_element_type=F32, precision=precision)


def _dot_nt(a, b, precision=None):
    return lax.dot_general(a, b, (((1,), (1,)), ((), ())), preferred_element_type=F32, precision=precision)


def _dot_tn(a, b, precision=None):
    return lax.dot_general(a, b, (((0,), (0,)), ((), ())), preferred_element_type=F32, precision=precision)


def _sigmoid(x):
    return 1.0 / (1.0 + jnp.exp(-x))


def _silu(x):
    return x * _sigmoid(x)


def _softplus(x):
    return jnp.maximum(x, 0.0) + jnp.log(1.0 + jnp.exp(-jnp.abs(x)))


def _shift_rows(x, d):
    n = x.shape[0]
    if d == 0:
        return x
    y = pltpu.roll(x, (-d) % n, axis=0)
    t = lax.broadcasted_iota(jnp.int32, x.shape, 0)
    ok = (t + d >= 0) & (t + d < n)
    return jnp.where(ok, y, 0.0)


def _ada_kernel(c_ref, w_ref, b_ref, o_ref):
    s = _silu(c_ref[...])
    o_ref[0] = _dot(s.astype(BF16), w_ref[0].astype(BF16)) + b_ref[0]


def _ada(cvec, w_ada, b_ada):
    nl = w_ada.shape[0]
    tn = 1536
    return pl.pallas_call(
        _ada_kernel,
        grid=(nl, 6 * D // tn),
        in_specs=[
            pl.BlockSpec((16, D), lambda l, j: (0, 0)),
            pl.BlockSpec((1, D, tn), lambda l, j: (l, 0, j)),
            pl.BlockSpec((1, 1, tn), lambda l, j: (l, 0, j)),
        ],
        out_specs=pl.BlockSpec((1, 16, tn), lambda l, j: (l, 0, j)),
        out_shape=jax.ShapeDtypeStruct((nl, 16, 6 * D), F32),
        compiler_params=_cp("arbitrary", "arbitrary"),
        name="adaln",
    )(cvec, w_ada, b_ada.reshape(nl, 1, 6 * D))


def _modulated_norm(x, g, shift, scale):
    ms = jnp.mean(x * x, axis=-1, keepdims=True)
    y = x * lax.rsqrt(ms + EPS) * g
    return y * (1.0 + scale) + shift


def _inproj_kernel(x_ref, mod_ref, g_ref, why_ref, wwa_ref, wna_ref, wgd_ref, wab_ref,
                   cos_ref, sa_ref, sb_ref, ohy, owa, ona, ogd, oab):
    m = mod_ref[0]
    h = _modulated_norm(x_ref[...], g_ref[...], m[0:1], m[1:2]).astype(BF16)
    ohy[...] = _dot(h, why_ref[...]).astype(ohy.dtype)
    ona[...] = _dot(h, wna_ref[...]).astype(ona.dtype)
    ogd[...] = _dot(h, wgd_ref[...]).astype(ogd.dtype)
    oab[...] = _dot(h, wab_ref[...])
    a = _dot(h, wwa_ref[...])
    for c in range(4):
        sl = slice(c * 128, (c + 1) * 128)
        ac = a[:, sl]
        r = (ac * cos_ref[:, sl] + pltpu.roll(ac, 112, axis=1) * sa_ref[:, sl]
             + pltpu.roll(ac, 16, axis=1) * sb_ref[:, sl])
        owa[:, sl] = r.astype(owa.dtype)


def _inproj(xa, mod, l, g, ws, tabs):
    why, wwa, wna, wgd, wab = ws
    cos_t, sa_t, sb_t = tabs
    row = lambda i: (i, 0)
    const = lambda i: (0, 0)
    tab = lambda i: (jnp.where(i < NT_LAT, i % (S // TM), S // TM), 0)
    return pl.pallas_call(
        _inproj_kernel,
        grid=(NT_ALL,),
        in_specs=[
            pl.BlockSpec((TM, D), row),
            pl.BlockSpec((1, 6, D), lambda i: (l * 16 + (i * TM) // S, 0, 0)),
            pl.BlockSpec((1, D), const),
            pl.BlockSpec((D, 768), const),
            pl.BlockSpec((D, 512), const),
            pl.BlockSpec((D, 768), const),
            pl.BlockSpec((D, 1024), const),
            pl.BlockSpec((D, 128), const),
            pl.BlockSpec((TM, 512), tab),
            pl.BlockSpec((TM, 512), tab),
            pl.BlockSpec((TM, 512), tab),
        ],
        out_specs=[
            pl.BlockSpec((TM, 768), row),
            pl.BlockSpec((TM, 512), row),
            pl.BlockSpec((TM, 768), row),
            pl.BlockSpec((TM, 1024), row),
            pl.BlockSpec((TM, 128), row),
        ],
        out_shape=[
            jax.ShapeDtypeStruct((ROWS, 768), BF16),
            jax.ShapeDtypeStruct((ROWS, 512), BF16),
            jax.ShapeDtypeStruct((ROWS, 768), BF16),
            jax.ShapeDtypeStruct((ROWS, 1024), BF16),
            jax.ShapeDtypeStruct((ROWS, 128), F32),
        ],
        compiler_params=_cp("arbitrary"),
        name="inproj",
    )(xa, mod, g, why, wwa, wna, wgd, wab, cos_t, sa_t, sb_t)


def _rope_tables():
    quarter = HD // 4
    pos = jnp.arange(S)
    inv = 10000.0 ** (-jnp.arange(quarter, dtype=F32) / quarter)
    ang_r = (pos // GW).astype(F32)[:, None] * inv[None, :]
    ang_c = (pos % GW).astype(F32)[:, None] * inv[None, :]
    z = jnp.zeros_like(ang_r)
    cos_h = jnp.concatenate([jnp.cos(ang_r)] * 2 + [jnp.cos(ang_c)] * 2, axis=1)
    sa_h = jnp.concatenate([-jnp.sin(ang_r), z, -jnp.sin(ang_c), z], axis=1)
    sb_h = jnp.concatenate([z, jnp.sin(ang_r), z, jnp.sin(ang_c)], axis=1)
    ones = jnp.ones((S, 128), F32)
    zeros = jnp.zeros((S, 128), F32)
    cos_t = jnp.concatenate([jnp.tile(cos_h, (1, 6)), ones], axis=1)
    sa_t = jnp.concatenate([jnp.tile(sa_h, (1, 6)), zeros], axis=1)
    sb_t = jnp.concatenate([jnp.tile(sb_h, (1, 6)), zeros], axis=1)
    ident = jnp.ones((TM, 512), F32)
    none = jnp.zeros((TM, 512), F32)
    return (jnp.concatenate([cos_t, ident], axis=0), jnp.concatenate([sa_t, none], axis=0),
            jnp.concatenate([sb_t, none], axis=0))


def _dft_tables(length):
    n = 2 * length
    f = jnp.arange(length, dtype=jnp.int32)
    m = ((2 * f[:, None] + 1) * (2 * f[None, :] + 1)) % (4 * n)
    th = m.astype(F32) * (2.0 * math.pi / (4 * n))
    phi = (2 * f + 1).astype(F32) * (math.pi / (2 * n))
    return (jnp.cos(th).astype(BF16), jnp.sin(th).astype(BF16),
            jnp.cos(phi)[:, None], jnp.sin(phi)[:, None])


def _hyena_features(length):
    t = jnp.arange(length, dtype=F32)
    t_norm = t / max(length - 1, 1)
    bands = jnp.linspace(1e-4, HY_BANDS - 1, HY_BANDS, dtype=F32)
    ang = (2.0 * math.pi / length) * t[:, None] * bands[None, :]
    z = jnp.concatenate([t_norm[:, None], jnp.cos(ang), -jnp.sin(ang)], axis=-1)
    z = jnp.pad(z, ((0, 0), (0, 128 - z.shape[1])))
    decay = jnp.tile(jnp.linspace(HY_DECAY_MIN, HY_DECAY_MAX, HY_CH, dtype=F32), 2)
    return z, jnp.exp(-t_norm[:, None] * decay[None, :])


def _hyfilt_kernel(z_ref, w1_ref, b1_ref, w2_ref, b2_ref, w3_ref, fr_ref, dec_ref,
                   c_ref, s_ref, cp_ref, sp_ref, kre_ref, kim_ref, pq_ref, *, length):
    @pl.when(pl.program_id(0) == 0)
    def _():
        fr = fr_ref[...]
        h = jnp.sin(fr * (_dot(z_ref[...], w1_ref[...], HI) + b1_ref[...]))
        h = jnp.sin(fr * (_dot(h, w2_ref[...], HI) + b2_ref[...]))
        h = _dot(h, w3_ref[...], HI) * dec_ref[...]
        hf = h[:, :HY_CH]
        t = lax.broadcasted_iota(jnp.int32, (length, HY_CH), 0)
        hb = jnp.where(t == 0, 0.0, h[:, HY_CH:])
        pq_ref[:, :HY_CH] = (hf + hb).astype(BF16)
        pq_ref[:, HY_CH:] = (hb - hf).astype(BF16)

    pq = pq_ref[...]
    cpq = _dot(c_ref[...], pq)
    spq = _dot(s_ref[...], pq)
    cphi, sphi = cp_ref[...], sp_ref[...]
    norm = 1.0 / length
    kre_ref[...] = (cphi * cpq[:, :HY_CH] + sphi * spq[:, :HY_CH]) * norm
    kim_ref[...] = (cphi * spq[:, HY_CH:] - sphi * cpq[:, HY_CH:]) * norm


def _hyena_filter_spectrum(length, w1, b1, w2, b2, w3, freq, dft):
    z, dec = _hyena_features(length)
    c_m, s_m, cphi, sphi = dft
    w1p = jnp.pad(w1, ((0, 128 - w1.shape[0]), (0, 0)))
    tf = min(512, length)
    const = lambda j: (0, 0)
    blk = lambda j: (j, 0)
    return pl.pallas_call(
        functools.partial(_hyfilt_kernel, length=length),
        grid=(length // tf,),
        in_specs=[
            pl.BlockSpec((length, 128), const),
            pl.BlockSpec((128, 64), const),
            pl.BlockSpec((1, 64), const),
            pl.BlockSpec((64, 64), const),
            pl.BlockSpec((1, 64), const),
            pl.BlockSpec((64, 2 * HY_CH), const),
            pl.BlockSpec((1, 64), const),
            pl.BlockSpec((length, 2 * HY_CH), const),
            pl.BlockSpec((tf, length), blk),
            pl.BlockSpec((tf, length), blk),
            pl.BlockSpec((tf, 1), blk),
            pl.BlockSpec((tf, 1), blk),
        ],
        out_specs=[pl.BlockSpec((tf, HY_CH), blk)] * 2,
        out_shape=[jax.ShapeDtypeStruct((length, HY_CH), F32)] * 2,
        scratch_shapes=[pltpu.VMEM((length, 2 * HY_CH), BF16)],
        compiler_params=_cp("arbitrary"),
        name=f"hyena_filter_{length}",
    )(z, w1p, b1[None, :], w2, b2[None, :], w3, freq[None, :], dec, c_m, s_m, cphi, sphi)


def _short_conv3(x_ref, cw):
    x = x_ref[...].astype(F32)
    return _shift_rows(x, -1) * cw[0:1] + x * cw[1:2] + _shift_rows(x, 1) * cw[2:3]


def _hyena_kernel(p0_ref, p1_ref, p2_ref, cw_ref, bias_ref, c_ref, s_ref, kre_ref, kim_ref, *rest):
    o_ref = rest[-1]
    cw = cw_ref[...]
    x0 = _short_conv3(p0_ref, cw[:, :HY_CH])
    k = _short_conv3(p1_ref, cw[:, HY_CH:2 * HY_CH]) * _short_conv3(p2_ref, cw[:, 2 * HY_CH:])
    kb = k.astype(BF16)
    a = _dot(c_ref[...], kb)
    b = _dot(s_ref[...], kb)
    kre, kim = kre_ref[...], kim_ref[...]
    yre = (a * kre + b * kim).astype(BF16)
    yim = (b * kre - a * kim).astype(BF16)
    y = _dot(c_ref[...], yre) + _dot(s_ref[...], yim)
    o_ref[...] = (x0 * (y + k * bias_ref[...])).astype(o_ref.dtype)


def _hyena(p_hy, y_prev, length, blk0, conv_w, bias, dft, kre, kim):
    c_m, s_m = dft[0], dft[1]
    const = lambda b: (0, 0)
    once = pl.Buffered(1)
    in_specs = [
        pl.BlockSpec((length, HY_CH), lambda b: (blk0 + b, 0)),
        pl.BlockSpec((length, HY_CH), lambda b: (blk0 + b, 1)),
        pl.BlockSpec((length, HY_CH), lambda b: (blk0 + b, 2)),
        pl.BlockSpec((3, 768), const),
        pl.BlockSpec((1, HY_CH), const),
        pl.BlockSpec((length, length), const, pipeline_mode=once),
        pl.BlockSpec((length, length), const, pipeline_mode=once),
        pl.BlockSpec((length, HY_CH), const, pipeline_mode=once),
        pl.BlockSpec((length, HY_CH), const, pipeline_mode=once),
    ]
    args = [p_hy, p_hy, p_hy, conv_w, bias[None, :], c_m, s_m, kre, kim]
    aliases = {}
    if y_prev is not None:
        in_specs.append(pl.BlockSpec(memory_space=pl.ANY))
        args.append(y_prev)
        aliases = {9: 0}
    return pl.pallas_call(
        _hyena_kernel,
        grid=(B,),
        in_specs=in_specs,
        out_specs=pl.BlockSpec((length, HY_CH), lambda b: (blk0 + b, 0)),
        out_shape=jax.ShapeDtypeStruct((ROWS, HY_CH), BF16),
        input_output_aliases=aliases,
        compiler_params=_cp("arbitrary"),
        name=f"hyena_{length}",
    )(*args)


def _softmax_pv(parts, extra_logit=None):
    m = None
    for s, _ in parts:
        mi = jnp.max(s, axis=-1, keepdims=True)
        m = mi if m is None else jnp.maximum(m, mi)
    if extra_logit is not None:
        m = jnp.maximum(m, extra_logit)
    den = 0.0 if extra_logit is None else jnp.exp(extra_logit - m)
    acc = None
    for s, v in parts:
        e = jnp.exp(s - m)
        den = den + jnp.sum(e, axis=-1, keepdims=True)
        o = _dot(e.astype(BF16), v)
        acc = o if acc is None else acc + o
    return acc / den


def _wa_kernel(sink_ref, q_ref, k_ref, v_ref, kc_ref, vc_ref, o_ref):
    n = pl.program_id(1)
    start = pl.multiple_of(jnp.clip((n - 1) * 128, 0, S - 384), 128)
    kw = k_ref[pl.ds(start, 384), :]
    vw = v_ref[pl.ds(start, 384), :]
    kc, vc = kc_ref[...], vc_ref[...]
    q = q_ref[...]
    rr = lax.broadcasted_iota(jnp.int32, (256, 384), 0)
    qpos = n * 128 + jnp.where(rr >= 128, rr - 128, rr)
    kpos = start + lax.broadcasted_iota(jnp.int32, (256, 384), 1)
    valid = jnp.abs(qpos - kpos) <= WINDOW
    r1 = lax.broadcasted_iota(jnp.int32, (256, 1), 0)
    for hk in range(2):
        q2 = jnp.concatenate([q[:, (2 * hk) * HD:(2 * hk + 1) * HD],
                              q[:, (2 * hk + 1) * HD:(2 * hk + 2) * HD]], axis=0)
        hs = slice(hk * HD, (hk + 1) * HD)
        s_loc = jnp.where(valid, _dot_nt(q2, kw[:, hs]) * SCALE, NEG)
        s_ctx = _dot_nt(q2, kc[:, hs]) * SCALE
        sink = jnp.where(r1 >= 128, sink_ref[2 * hk + 1], sink_ref[2 * hk])
        o = _softmax_pv([(s_ctx, vc[:, hs]), (s_loc, vw[:, hs])], sink)
        o_ref[:, (2 * hk) * HD:(2 * hk + 1) * HD] = o[:128].astype(o_ref.dtype)
        o_ref[:, (2 * hk + 1) * HD:(2 * hk + 2) * HD] = o[128:].astype(o_ref.dtype)


def _window_attention(p_wa, sink):
    nb = S // 128
    return pl.pallas_call(
        _wa_kernel,
        grid=(B, nb),
        in_specs=[
            pl.BlockSpec(memory_space=pltpu.SMEM),
            pl.BlockSpec((128, 256), lambda b, n: (b * nb + n, 0)),
            pl.BlockSpec((S, 128), lambda b, n: (b, 2)),
            pl.BlockSpec((S, 128), lambda b, n: (b, 3)),
            pl.BlockSpec((LC, 128), lambda b, n: (CTX_BLK0 + b, 2)),
            pl.BlockSpec((LC, 128), lambda b, n: (CTX_BLK0 + b, 3)),
        ],
        out_specs=pl.BlockSpec((128, 256), lambda b, n: (b * nb + n, 0)),
        out_shape=jax.ShapeDtypeStruct((ROWS, 256), BF16),
        compiler_params=_cp("arbitrary", "arbitrary"),
        name="window_attn",
    )(sink, p_wa, p_wa, p_wa, p_wa, p_wa)


def _na_kernel(q_ref, k_ref, v_ref, kc_ref, vc_ref, bias_ref, o_ref):
    r = pl.program_id(1)
    start = pl.multiple_of(jnp.clip(r - NA_KR // 2, 0, S // GW - NA_KR) * GW, GW)
    kw = k_ref[pl.ds(start, NA_KR * GW), :]
    vw = v_ref[pl.ds(start, NA_KR * GW), :]
    kc, vc = kc_ref[...], vc_ref[...]
    q = q_ref[...]
    for h in range(4):
        hs = slice(h * HD, (h + 1) * HD)
        s_loc = _dot_nt(q[:, hs], kw[:, hs]) * SCALE + bias_ref[h, 0]
        s_ctx = _dot_nt(q[:, hs], kc[:, hs]) * SCALE
        o = _softmax_pv([(s_ctx, vc[:, hs]), (s_loc, vw[:, hs])])
        o_ref[:, hs] = o.astype(o_ref.dtype)


def _na_bias_table(rpb):
    o = jnp.arange(NA_KR)[:, None, None, None]
    j = jnp.arange(NA_KR)[None, None, :, None]
    qc = jnp.arange(GW)[None, :, None, None]
    kcol = jnp.arange(GW)[None, None, None, :]
    ri = jnp.broadcast_to(o + j, (NA_KR, GW, NA_KR, GW))
    ci = jnp.broadcast_to(jnp.clip(kcol - qc + NA_KC - 1, 0, 2 * NA_KC - 2), (NA_KR, GW, NA_KR, GW))
    c_start = jnp.clip(qc - NA_KC // 2, 0, GW - NA_KC)
    ok = jnp.broadcast_to((kcol >= c_start) & (kcol < c_start + NA_KC), (NA_KR, GW, NA_KR, GW))
    tbl = jnp.where(ok[None], rpb.astype(F32)[:, ri, ci], NEG)
    return tbl.reshape(4, NA_KR, GW, NA_KR * GW)


def _neighborhood_attention(p_na, bias_tbl):
    rows = S // GW

    def off(r):
        return jnp.clip(r - NA_KR // 2, 0, rows - NA_KR) - r + NA_KR - 1

    return pl.pallas_call(
        _na_kernel,
        grid=(B, rows),
        in_specs=[
            pl.BlockSpec((GW, 256), lambda b, r: (b * rows + r, 0)),
            pl.BlockSpec((S, 256), lambda b, r: (b, 1)),
            pl.BlockSpec((S, 256), lambda b, r: (b, 2)),
            pl.BlockSpec((LC, 256), lambda b, r: (CTX_BLK0 + b, 1)),
            pl.BlockSpec((LC, 256), lambda b, r: (CTX_BLK0 + b, 2)),
            pl.BlockSpec((4, 1, GW, NA_KR * GW), lambda b, r: (0, off(r), 0, 0)),
        ],
        out_specs=pl.BlockSpec((GW, 256), lambda b, r: (b * rows + r, 0)),
        out_shape=jax.ShapeDtypeStruct((ROWS, 256), BF16),
        compiler_params=_cp("arbitrary", "arbitrary"),
        name="neighborhood_attn",
    )(p_na, p_na, p_na, p_na, p_na, bias_tbl)


def _ctx_attn_kernel(sink_ref, qw_ref, kw_ref, vw_ref, qn_ref, kn_ref, vn_ref, ywa_in, yna_in, owa_ref, ona_ref):
    del ywa_in, yna_in
    q, k, v = qw_ref[...], kw_ref[...], vw_ref[...]
    r1 = lax.broadcasted_iota(jnp.int32, (2 * LC, 1), 0)
    for hk in range(2):
        q2 = jnp.concatenate([q[:, (2 * hk) * HD:(2 * hk + 1) * HD],
                              q[:, (2 * hk + 1) * HD:(2 * hk + 2) * HD]], axis=0)
        hs = slice(hk * HD, (hk + 1) * HD)
        sink = jnp.where(r1 >= LC, sink_ref[2 * hk + 1], sink_ref[2 * hk])
        o = _softmax_pv([(_dot_nt(q2, k[:, hs]) * SCALE, v[:, hs])], sink)
        owa_ref[:, (2 * hk) * HD:(2 * hk + 1) * HD] = o[:LC].astype(owa_ref.dtype)
        owa_ref[:, (2 * hk + 1) * HD:(2 * hk + 2) * HD] = o[LC:].astype(owa_ref.dtype)
    q, k, v = qn_ref[...], kn_ref[...], vn_ref[...]
    for h in range(4):
        hs = slice(h * HD, (h + 1) * HD)
        o = _softmax_pv([(_dot_nt(q[:, hs], k[:, hs]) * SCALE, v[:, hs])])
        ona_ref[:, hs] = o.astype(ona_ref.dtype)


def _ctx_attention(p_wa, p_na, sink, y_wa, y_na):
    blk = lambda c: (lambda b: (CTX_BLK0 + b, c))
    return pl.pallas_call(
        _ctx_attn_kernel,
        grid=(B,),
        in_specs=[
            pl.BlockSpec(memory_space=pltpu.SMEM),
            pl.BlockSpec((LC, 256), blk(0)),
            pl.BlockSpec((LC, 128), blk(2)),
            pl.BlockSpec((LC, 128), blk(3)),
            pl.BlockSpec((LC, 256), blk(0)),
            pl.BlockSpec((LC, 256), blk(1)),
            pl.BlockSpec((LC, 256), blk(2)),
            pl.BlockSpec(memory_space=pl.ANY),
            pl.BlockSpec(memory_space=pl.ANY),
        ],
        out_specs=[pl.BlockSpec((LC, 256), blk(0)), pl.BlockSpec((LC, 256), blk(0))],
        out_shape=[jax.ShapeDtypeStruct((ROWS, 256), BF16)] * 2,
        input_output_aliases={7: 0, 8: 1},
        compiler_params=_cp("arbitrary"),
        name="ctx_attn",
    )(sink, p_wa, p_wa, p_wa, p_na, p_na, p_na, y_wa, y_na)


def _gdn_prep_kernel(p_ref, ab_ref, cw_ref, alog_ref, dtb_ref, qkv_ref, gb_ref):
    x = p_ref[:, :768].astype(F32)
    cw = cw_ref[...]
    u = (_shift_rows(x, -2) * cw[0:1] + _shift_rows(x, -1) * cw[1:2] + x * cw[2:3]
         + _shift_rows(x, 1) * cw[3:4])
    u = _silu(u)
    for j in range(8):
        sl = slice(j * HD, (j + 1) * HD)
        xs = u[:, sl]
        nrm = lax.rsqrt(jnp.sum(xs * xs, axis=-1, keepdims=True) + EPS)
        qkv_ref[0, :, sl] = xs * (nrm * SCALE if j < 4 else nrm)
    qkv_ref[0, :, 512:768] = u[:, 512:768]
    ab = ab_ref[...]
    g = -jnp.exp(alog_ref[...]) * _softplus(ab + dtb_ref[...])
    lane = lax.broadcasted_iota(jnp.int32, ab.shape, 1)
    gb_ref[0] = jnp.where(lane < 8, g, _sigmoid(ab))


def _gdn_prep(p_gd, p_ab, length, blk0, conv_w, a_log, dt_bias, prev):
    const = lambda b: (0, 0)
    seq_blk = 0 if length == S else S // LC
    alog = jnp.pad(a_log.reshape(1, 8), ((0, 0), (0, 120)))
    dtb = jnp.pad(dt_bias.reshape(1, 8), ((0, 0), (0, 120)))
    in_specs = [
        pl.BlockSpec((length, 1024), lambda b: (blk0 + b, 0)),
        pl.BlockSpec((length, 128), lambda b: (blk0 + b, 0)),
        pl.BlockSpec((4, 768), const),
        pl.BlockSpec((1, 128), const),
        pl.BlockSpec((1, 128), const),
    ]
    args = [p_gd, p_ab, conv_w, alog, dtb]
    aliases = {}
    if prev is not None:
        in_specs += [pl.BlockSpec(memory_space=pl.ANY)] * 2
        args += list(prev)
        aliases = {5: 0, 6: 1}

    def body(p_ref, ab_ref, cw_ref, alog_ref, dtb_ref, *rest):
        _gdn_prep_kernel(p_ref, ab_ref, cw_ref, alog_ref, dtb_ref, rest[-2], rest[-1])

    return pl.pallas_call(
        body,
        grid=(B,),
        in_specs=in_specs,
        out_specs=[pl.BlockSpec((1, length, 768), lambda b: (b, seq_blk, 0)),
                   pl.BlockSpec((1, length, 128), lambda b: (b, seq_blk, 0))],
        out_shape=[jax.ShapeDtypeStruct((B, S + LC, 768), F32), jax.ShapeDtypeStruct((B, S + LC, 128), F32)],
        input_output_aliases=aliases,
        compiler_params=_cp("arbitrary"),
        name=f"gdn_prep_{length}",
    )(*args)


def _bdot(a, b):
    return _dot(a.astype(BF16), b.astype(BF16))


def _gdn_chain(x, gb, cum, cum_t, st, dirn, h, masks):
    tri, strict = masks
    col = dirn * 4 + h
    hs = slice(h * HD, (h + 1) * HD)
    qh, kh, vh = x[:, hs], x[:, 256 + h * HD:256 + (h + 1) * HD], x[:, 512 + h * HD:512 + (h + 1) * HD]
    gc = cum[:, col:col + 1]
    gc_row = cum_t[col:col + 1, :]
    beta = gb[:, 8 + col:9 + col]
    decay = jnp.exp(jnp.where(tri, gc - gc_row, NEG))
    kb = kh * beta
    kbf, khf = kb.astype(BF16), kh.astype(BF16)
    pw = jnp.where(strict, -(_dot_nt(kbf, khf) * decay), 0.0)
    eg = jnp.exp(gc)
    xw = jnp.concatenate([vh * beta, kb * eg], axis=1)
    for i in range(6):
        xw = xw + _bdot(pw, xw)
        if i < 5:
            pw = _bdot(pw, pw)
    u, w = xw[:, :HD], xw[:, HD:]
    g_last = gc[0:1, :] if dirn else gc[CHUNK - 1:CHUNK, :]
    k_dec = kh * jnp.exp(g_last - gc)
    attn = _dot_nt(qh.astype(BF16), khf) * decay
    u_new = u - _bdot(w, st)
    o = _bdot(qh * eg, st) + _bdot(attn, u_new)
    st_new = st * jnp.exp(g_last) + _dot_tn(k_dec.astype(BF16), u_new.astype(BF16))
    return o, st_new


def _gdn_kernel(x0_ref, gb0_ref, x1_ref, gb1_ref, o0_ref, o1_ref, state_ref):
    @pl.when(pl.program_id(0) == 0)
    def _():
        state_ref[...] = jnp.zeros_like(state_ref)

    ii = lax.broadcasted_iota(jnp.int32, (CHUNK, CHUNK), 0)
    jj = lax.broadcasted_iota(jnp.int32, (CHUNK, CHUNK), 1)
    masks = ((jj <= ii, jj < ii), (jj >= ii, jj > ii))
    ones_tri = ((jj <= ii).astype(F32), (jj >= ii).astype(F32))

    def body(b, carry):
        states = [[state_ref[b, dirn, h] for h in range(4)] for dirn in range(2)]
        xs = (x0_ref[b], x1_ref[b])
        gbs = (gb0_ref[b], gb1_ref[b])
        outs, new_states = [], []
        for dirn in range(2):
            cum = _dot(ones_tri[dirn], gbs[dirn], HI)
            cum_t = cum.T
            res = [_gdn_chain(xs[dirn], gbs[dirn], cum, cum_t, states[dirn][h], dirn, h, masks[dirn])
                   for h in range(4)]
            outs.append(jnp.concatenate([r[0] for r in res], axis=1))
            new_states.append([r[1] for r in res])
        for dirn in range(2):
            for h in range(4):
                state_ref[b, dirn, h] = new_states[dirn][h]
        o0_ref[b] = outs[0]
        o1_ref[b] = outs[1]
        return carry

    lax.fori_loop(0, B, body, 0)


def _gdn_scan(qkv, gb):
    nc = (S + LC) // CHUNK
    fwd = lambda c: (0, (c + S // CHUNK) % nc, 0)
    bwd = lambda c: (0, nc - 1 - c, 0)
    return pl.pallas_call(
        _gdn_kernel,
        grid=(nc,),
        in_specs=[
            pl.BlockSpec((B, CHUNK, 768), fwd),
            pl.BlockSpec((B, CHUNK, 128), fwd),
            pl.BlockSpec((B, CHUNK, 768), bwd),
            pl.BlockSpec((B, CHUNK, 128), bwd),
        ],
        out_specs=[pl.BlockSpec((B, CHUNK, 256), fwd), pl.BlockSpec((B, CHUNK, 256), bwd)],
        out_shape=[jax.ShapeDtypeStruct((B, S + LC, 256), F32)] * 2,
        scratch_shapes=[pltpu.VMEM((B, 2, 4, HD, HD), F32)],
        compiler_params=_cp("arbitrary"),
        name="gdn_scan",
    )(qkv, gb, qkv, gb)


def _gdn_finish_kernel(o0_ref, o1_ref, gate_ref, ng_ref, y_ref):
    o = o0_ref[0] + o1_ref[0]
    gi = lax.broadcasted_iota(jnp.int32, (256, 256), 0) // HD
    gj = lax.broadcasted_iota(jnp.int32, (256, 256), 1) // HD
    group_mean = jnp.where(gi == gj, 1.0 / HD, 0.0)
    ms = _dot(o * o, group_mean, HI)
    y = o * lax.rsqrt(ms + EPS) * ng_ref[...]
    y_ref[...] = (y * _silu(gate_ref[...].astype(F32))).astype(y_ref.dtype)


def _gdn_finish(o0, o1, p_gd, norm_g, need_ctx):
    lat_tiles = S // TM
    row_blk = lambda b, j: jnp.where(j < lat_tiles, b * lat_tiles + j, CTX_BLK0 + b)
    return pl.pallas_call(
        _gdn_finish_kernel,
        grid=(B, lat_tiles + (1 if need_ctx else 0)),
        in_specs=[
            pl.BlockSpec((1, TM, 256), lambda b, j: (b, j, 0)),
            pl.BlockSpec((1, TM, 256), lambda b, j: (b, j, 0)),
            pl.BlockSpec((TM, 256), lambda b, j: (row_blk(b, j), 3)),
            pl.BlockSpec((1, 256), lambda b, j: (0, 0)),
        ],
        out_specs=pl.BlockSpec((TM, 256), lambda b, j: (row_blk(b, j), 0)),
        out_shape=jax.ShapeDtypeStruct((ROWS, 256), BF16),
        compiler_params=_cp("arbitrary", "arbitrary"),
        name="gdn_finish",
    )(o0, o1, p_gd, jnp.tile(norm_g, 4)[None, :])


def _outproj_kernel(x_ref, mod_ref, g2_ref, yh_ref, yw_ref, yn_ref, yg_ref, wo_ref, wr_ref, eb_ref,
                    xo_ref, h2_ref, idx_ref, tw_ref):
    m = mod_ref[0]
    acc = (_dot(yh_ref[...], wo_ref[0:256, :]) + _dot(yw_ref[...], wo_ref[256:512, :])
           + _dot(yn_ref[...], wo_ref[512:768, :]) + _dot(yg_ref[...], wo_ref[768:1024, :]))
    x = x_ref[...] + m[2:3] * acc
    xo_ref[...] = x
    h2 = _modulated_norm(x, g2_ref[...], m[3:4], m[4:5])
    h2_ref[...] = h2.astype(h2_ref.dtype)
    scores = _sigmoid(_dot(h2, wr_ref[...], HI))
    lane = lax.broadcasted_iota(jnp.int32, scores.shape, 1)
    sel = jnp.where(lane < N_EXP, scores + eb_ref[...], -jnp.inf)
    lane_f = lane.astype(F32)
    idx_out = jnp.zeros(scores.shape, F32)
    s_out = jnp.zeros(scores.shape, F32)
    for kk in range(TOP_K):
        mx = jnp.max(sel, axis=-1, keepdims=True)
        idx = jnp.min(jnp.where(sel == mx, lane_f, 128.0), axis=-1, keepdims=True)
        hit = lane_f == idx
        sk = jnp.sum(jnp.where(hit, scores, 0.0), axis=-1, keepdims=True)
        sel = jnp.where(hit, -jnp.inf, sel)
        idx_out = jnp.where(lane == kk, idx, idx_out)
        s_out = jnp.where(lane == kk, sk, s_out)
    tot = jnp.sum(s_out, axis=-1, keepdims=True)
    idx_ref[...] = idx_out.astype(jnp.int32)
    tw_ref[...] = s_out / tot * ROUTED_SCALE


def _outproj(xa, mod, l, g2, ys, wo, wr, eb, n_tiles):
    row = lambda i: (i, 0)
    const = lambda i: (0, 0)
    rows = n_tiles * TM
    return pl.pallas_call(
        _outproj_kernel,
        grid=(n_tiles,),
        in_specs=[
            pl.BlockSpec((TM, D), row),
            pl.BlockSpec((1, 6, D), lambda i: (l * 16 + (i * TM) // S, 0, 0)),
            pl.BlockSpec((1, D), const),
            pl.BlockSpec((TM, 256), row),
            pl.BlockSpec((TM, 256), row),
            pl.BlockSpec((TM, 256), row),
            pl.BlockSpec((TM, 256), row),
            pl.BlockSpec((D, D), const),
            pl.BlockSpec((D, 128), const),
            pl.BlockSpec((1, 128), const),
        ],
        out_specs=[
            pl.BlockSpec((TM, D), row),
            pl.BlockSpec((TM, D), row),
            pl.BlockSpec((TM, 128), row),
            pl.BlockSpec((TM, 128), row),
        ],
        out_shape=[
            jax.ShapeDtypeStruct((rows, D), F32),
            jax.ShapeDtypeStruct((rows, D), BF16),
            jax.ShapeDtypeStruct((rows, 128), jnp.int32),
            jax.ShapeDtypeStruct((rows, 128), F32),
        ],
        compiler_params=_cp("arbitrary"),
        name="outproj_router",
    )(xa, mod, g2, *ys, wo, wr, eb)


def _moe_kernel(be_ref, x_ref, rw_ref, wg_ref, wu_ref, wd_ref, o_ref):
    del be_ref
    x = x_ref[...]
    g = _dot(x, wg_ref[0].astype(BF16))
    u = _dot(x, wu_ref[0].astype(BF16))
    hid = (_silu(g) * u).astype(BF16)
    y = _dot(hid, wd_ref[0].astype(BF16))
    o_ref[...] = (y * rw_ref[...]).astype(o_ref.dtype)


def _moe_experts(blk_e, xb, row_w, wg, wu, wd):
    n_rows = xb.shape[0]
    n_blocks = n_rows // MOE_BLK
    grid_spec = pltpu.PrefetchScalarGridSpec(
        num_scalar_prefetch=1,
        grid=(n_blocks,),
        in_specs=[
            pl.BlockSpec((MOE_BLK, D), lambda i, be: (i, 0)),
            pl.BlockSpec((MOE_BLK, 1), lambda i, be: (i, 0)),
            pl.BlockSpec((1, D, D_EXP), lambda i, be: (be[i], 0, 0)),
            pl.BlockSpec((1, D, D_EXP), lambda i, be: (be[i], 0, 0)),
            pl.BlockSpec((1, D_EXP, D), lambda i, be: (be[i], 0, 0)),
        ],
        out_specs=pl.BlockSpec((MOE_BLK, D), lambda i, be: (i, 0)),
    )
    return pl.pallas_call(
        _moe_kernel,
        grid_spec=grid_spec,
        out_shape=jax.ShapeDtypeStruct((n_rows, D), F32),
        compiler_params=_cp("arbitrary"),
        name="moe_experts",
    )(blk_e, xb, row_w, wg, wu, wd)


def _dispatch(top_idx, top_w):
    n_tok = top_idx.shape[0]
    n_asg = n_tok * TOP_K
    flat_e = top_idx.reshape(n_asg)
    flat_w = top_w.reshape(n_asg)
    counts = jnp.sum((flat_e[:, None] == jnp.arange(N_EXP)[None, :]).astype(jnp.int32), axis=0)
    padded = (counts + MOE_BLK - 1) // MOE_BLK * MOE_BLK
    pad_end = jnp.cumsum(padded)
    pad_start = pad_end - padded
    raw_start = jnp.cumsum(counts) - counts
    order = jnp.argsort(flat_e, stable=True)
    sorted_e = flat_e[order]
    dest = (pad_start[sorted_e] + jnp.arange(n_asg) - raw_start[sorted_e]).astype(jnp.int32)
    n_blocks = n_asg // MOE_BLK + N_EXP
    n_rows = n_blocks * MOE_BLK
    row_tok = jnp.zeros((n_rows,), jnp.int32).at[dest].set((order // TOP_K).astype(jnp.int32))
    row_w = jnp.zeros((n_rows,), F32).at[dest].set(flat_w[order])
    blk_e = jnp.minimum(jnp.searchsorted(pad_end, jnp.arange(n_blocks) * MOE_BLK, side='right'),
                        N_EXP - 1).astype(jnp.int32)
    pos = jnp.zeros((n_asg,), jnp.int32).at[order].set(dest)
    return row_tok, row_w, blk_e, pos.reshape(n_tok, TOP_K)


def _ffn_out_kernel(x_ref, mod_ref, h2_ref, r_ref, wg_ref, wu_ref, wd_ref, nf_ref, o_ref, *, final):
    m = mod_ref[0]
    h2 = h2_ref[...]
    hid = (_silu(_dot(h2, wg_ref[...])) * _dot(h2, wu_ref[...])).astype(BF16)
    x = x_ref[...] + m[5:6] * (r_ref[...] + _dot(hid, wd_ref[...]))
    if final:
        ms = jnp.mean(x * x, axis=-1, keepdims=True)
        x = x * lax.rsqrt(ms + EPS) * nf_ref[...]
    o_ref[...] = x


def _ffn_out(xn, mod, l, h2, routed, wg, wu, wd, norm_f, n_tiles, final):
    row = lambda i: (i, 0)
    const = lambda i: (0, 0)
    return pl.pallas_call(
        functools.partial(_ffn_out_kernel, final=final),
        grid=(n_tiles,),
        in_specs=[
            pl.BlockSpec((TM, D), row),
            pl.BlockSpec((1, 6, D), lambda i: (l * 16 + (i * TM) // S, 0, 0)),
            pl.BlockSpec((TM, D), row),
            pl.BlockSpec((TM, D), row),
            pl.BlockSpec((D, D_EXP), const),
            pl.BlockSpec((D, D_EXP), const),
            pl.BlockSpec((D_EXP, D), const),
            pl.BlockSpec((1, D), const),
        ],
        out_specs=pl.BlockSpec((TM, D), row),
        out_shape=jax.ShapeDtypeStruct((n_tiles * TM, D), F32),
        compiler_params=_cp("arbitrary"),
        name="shared_ffn_residual",
    )(xn, mod, h2, routed, wg, wu, wd, norm_f)


def kernel(x, c, ctx, c_ctx, w_ada, b_ada, norm1, norm2, norm_f, w_in, w_out, hy_conv, hy_w1, hy_b1, hy_w2, hy_b2, hy_w3, hy_freq, hy_bias, wa_sink, na_rpb, gdn_conv, gdn_a_log, gdn_dt_bias, gdn_norm, moe_router, moe_bias, moe_gate, moe_up, moe_down, sh_gate, sh_up, sh_down):
    depth = w_ada.shape[0]
    xa = jnp.concatenate([x.reshape(NL, D), ctx.reshape(NC, D)], axis=0)
    cvec = jnp.concatenate([c, c_ctx[None, :], jnp.zeros((16 - B - 1, D), F32)], axis=0)
    mod = _ada(cvec, w_ada, b_ada).reshape(depth * 16, 6, D)
    rope = _rope_tables()
    dft_lat = _dft_tables(S)
    dft_ctx = _dft_tables(LC)
    o1, o2, o3 = 768, 768 + 512, 768 + 512 + 768

    for l in range(depth):
        need_ctx = l < depth - 1
        n_tiles = NT_ALL if need_ctx else NT_LAT
        wl = w_in[l].astype(BF16)
        ws = (wl[:, :o1], wl[:, o1:o2], wl[:, o2:o3], wl[:, o3:o3 + 1024],
              jnp.pad(wl[:, o3 + 1024:], ((0, 0), (0, 128 - 16))))
        p_hy, p_wa, p_na, p_gd, p_ab = _inproj(xa, mod, l, norm1[l][None, :], ws, rope)

        filt = (hy_w1[l], hy_b1[l], hy_w2[l], hy_b2[l], hy_w3[l], hy_freq[l])
        kre, kim = _hyena_filter_spectrum(S, *filt, dft_lat)
        y_hy = _hyena(p_hy, None, S, 0, hy_conv[l], hy_bias[l], dft_lat, kre, kim)
        y_wa = _window_attention(p_wa, wa_sink[l])
        y_na = _neighborhood_attention(p_na, _na_bias_table(na_rpb[l]))
        if need_ctx:
            kre_c, kim_c = _hyena_filter_spectrum(LC, *filt, dft_ctx)
            y_hy = _hyena(p_hy, y_hy, LC, CTX_BLK0, hy_conv[l], hy_bias[l], dft_ctx, kre_c, kim_c)
            y_wa, y_na = _ctx_attention(p_wa, p_na, wa_sink[l], y_wa, y_na)
        prep = _gdn_prep(p_gd, p_ab, S, 0, gdn_conv[l], gdn_a_log[l], gdn_dt_bias[l], None)
        qkv, gb = _gdn_prep(p_gd, p_ab, LC, CTX_BLK0, gdn_conv[l], gdn_a_log[l], gdn_dt_bias[l], prep)
        y_gd = _gdn_finish(*_gdn_scan(qkv, gb), p_gd, gdn_norm[l], need_ctx)

        wr = jnp.pad(moe_router[l], ((0, 0), (0, 128 - N_EXP)))
        eb = jnp.pad(moe_bias[l], (0, 128 - N_EXP))[None, :]
        xn, h2, top_idx, top_w = _outproj(xa, mod, l, norm2[l][None, :], (y_hy, y_wa, y_na, y_gd),
                                          w_out[l].astype(BF16), wr, eb, n_tiles)
        row_tok, row_w, blk_e, pos = _dispatch(top_idx[:, :TOP_K], top_w[:, :TOP_K])
        yb = _moe_experts(blk_e, h2[row_tok], row_w[:, None], moe_gate[l], moe_up[l], moe_down[l])
        routed = jnp.sum(yb[pos], axis=1)
        xa_new = _ffn_out(xn, mod, l, h2, routed, sh_gate[l].astype(BF16), sh_up[l].astype(BF16),
                          sh_down[l].astype(BF16), norm_f[None, :], n_tiles, final=not need_ctx)
        if need_ctx:
            xa = xa_new
        else:
            return xa_new.reshape(B, S, D)
```

```python
import functools
import math

import jax
import jax.numpy as jnp
from jax import lax
from jax.experimental import pallas as pl
from jax.experimental.pallas import tpu as pltpu

F32 = jnp.float32
BF16 = jnp.bfloat16
HI = lax.Precision.HIGHEST

D = 1024
B = 8
S = 2048
LC = 256
GW = 64
HD = 64
NL = B * S
NC = B * LC
ROWS = NL + NC
TM = 256
NT_LAT = NL // TM
NT_ALL = ROWS // TM
CTX_BLK0 = NL // LC

HY_CH = 256
HY_BANDS = 16
HY_DECAY_MIN = -math.log(1e-2) / 1.5
HY_DECAY_MAX = -math.log(1e-2) / 0.3
WINDOW = 128
NA_KR = 8
NA_KC = 16
CHUNK = 64
N_EXP = 32
TOP_K = 4
D_EXP = 256
ROUTED_SCALE = 2.5
MOE_BLK = 256
EPS = 1e-6
NEG = -1e30
SCALE = HD ** -0.5
VMEM_LIMIT = 56 * 1024 * 1024


def _cp(*sem):
    return pltpu.CompilerParams(dimension_semantics=tuple(sem), vmem_limit_bytes=VMEM_LIMIT)


def _dot(a, b, precision=None):
    return jnp.dot(a, b, preferred_element_type=F32, precision=precision)


def _dot_nt(a, b, precision=None):
    return lax.dot_general(a, b, (((1,), (1,)), ((), ())), preferred_element_type=F32, precision=precision)


def _dot_tn(a, b, precision=None):
    return lax.dot_general(a, b, (((0,), (0,)), ((), ())), preferred_element_type=F32, precision=precision)


def _sigmoid(x):
    return 1.0 / (1.0 + jnp.exp(-x))


def _silu(x):
    return x * _sigmoid(x)


def _softplus(x):
    return jnp.maximum(x, 0.0) + jnp.log(1.0 + jnp.exp(-jnp.abs(x)))


def _shift_rows(x, d):
    n = x.shape[0]
    if d == 0:
        return x
    y = pltpu.roll(x, (-d) % n, axis=0)
    t = lax.broadcasted_iota(jnp.int32, x.shape, 0)
    ok = (t + d >= 0) & (t + d < n)
    return jnp.where(ok, y, 0.0)


def _ada_kernel(c_ref, w_ref, b_ref, o_ref):
    s = _silu(c_ref[...])
    o_ref[0] = _dot(s.astype(BF16), w_ref[0].astype(BF16)) + b_ref[0]


def _ada(cvec, w_ada, b_ada):
    nl = w_ada.shape[0]
    tn = 1536
    return pl.pallas_call(
        _ada_kernel,
        grid=(nl, 6 * D // tn),
        in_specs=[
            pl.BlockSpec((16, D), lambda l, j: (0, 0)),
            pl.BlockSpec((1, D, tn), lambda l, j: (l, 0, j)),
            pl.BlockSpec((1, 1, tn), lambda l, j: (l, 0, j)),
        ],
        out_specs=pl.BlockSpec((1, 16, tn), lambda l, j: (l, 0, j)),
        out_shape=jax.ShapeDtypeStruct((nl, 16, 6 * D), F32),
        compiler_params=_cp("arbitrary", "arbitrary"),
        name="adaln",
    )(cvec, w_ada, b_ada.reshape(nl, 1, 6 * D))


def _modulated_norm(x, g, shift, scale):
    ms = jnp.mean(x * x, axis=-1, keepdims=True)
    y = x * lax.rsqrt(ms + EPS) * g
    return y * (1.0 + scale) + shift


def _inproj_kernel(x_ref, mod_ref, g_ref, why_ref, wwa_ref, wna_ref, wgd_ref, wab_ref,
                   cos_ref, sa_ref, sb_ref, ohy, owa, ona, ogd, oab):
    m = mod_ref[0]
    h = _modulated_norm(x_ref[...], g_ref[...], m[0:1], m[1:2]).astype(BF16)
    ohy[...] = _dot(h, why_ref[...]).astype(ohy.dtype)
    ona[...] = _dot(h, wna_ref[...]).astype(ona.dtype)
    ogd[...] = _dot(h, wgd_ref[...]).astype(ogd.dtype)
    oab[...] = _dot(h, wab_ref[...])
    a = _dot(h, wwa_ref[...])
    for c in range(4):
        sl = slice(c * 128, (c + 1) * 128)
        ac = a[:, sl]
        r = (ac * cos_ref[:, sl] + pltpu.roll(ac, 112, axis=1) * sa_ref[:, sl]
             + pltpu.roll(ac, 16, axis=1) * sb_ref[:, sl])
        owa[:, sl] = r.astype(owa.dtype)


def _inproj(xa, mod, l, g, ws, tabs):
    why, wwa, wna, wgd, wab = ws
    cos_t, sa_t, sb_t = tabs
    row = lambda i: (i, 0)
    const = lambda i: (0, 0)
    tab = lambda i: (jnp.where(i < NT_LAT, i % (S // TM), S // TM), 0)
    return pl.pallas_call(
        _inproj_kernel,
        grid=(NT_ALL,),
        in_specs=[
            pl.BlockSpec((TM, D), row),
            pl.BlockSpec((1, 6, D), lambda i: (l * 16 + (i * TM) // S, 0, 0)),
            pl.BlockSpec((1, D), const),
            pl.BlockSpec((D, 768), const),
            pl.BlockSpec((D, 512), const),
            pl.BlockSpec((D, 768), const),
            pl.BlockSpec((D, 1024), const),
            pl.BlockSpec((D, 128), const),
            pl.BlockSpec((TM, 512), tab),
            pl.BlockSpec((TM, 512), tab),
            pl.BlockSpec((TM, 512), tab),
        ],
        out_specs=[
            pl.BlockSpec((TM, 768), row),
            pl.BlockSpec((TM, 512), row),
            pl.BlockSpec((TM, 768), row),
            pl.BlockSpec((TM, 1024), row),
            pl.BlockSpec((TM, 128), row),
        ],
        out_shape=[
            jax.ShapeDtypeStruct((ROWS, 768), BF16),
            jax.ShapeDtypeStruct((ROWS, 512), BF16),
            jax.ShapeDtypeStruct((ROWS, 768), BF16),
            jax.ShapeDtypeStruct((ROWS, 1024), BF16),
            jax.ShapeDtypeStruct((ROWS, 128), F32),
        ],
        compiler_params=_cp("arbitrary"),
        name="inproj",
    )(xa, mod, g, why, wwa, wna, wgd, wab, cos_t, sa_t, sb_t)


def _rope_tables():
    quarter = HD // 4
    pos = jnp.arange(S)
    inv = 10000.0 ** (-jnp.arange(quarter, dtype=F32) / quarter)
    ang_r = (pos // GW).astype(F32)[:, None] * inv[None, :]
    ang_c = (pos % GW).astype(F32)[:, None] * inv[None, :]
    z = jnp.zeros_like(ang_r)
    cos_h = jnp.concatenate([jnp.cos(ang_r)] * 2 + [jnp.cos(ang_c)] * 2, axis=1)
    sa_h = jnp.concatenate([-jnp.sin(ang_r), z, -jnp.sin(ang_c), z], axis=1)
    sb_h = jnp.concatenate([z, jnp.sin(ang_r), z, jnp.sin(ang_c)], axis=1)
    ones = jnp.ones((S, 128), F32)
    zeros = jnp.zeros((S, 128), F32)
    cos_t = jnp.concatenate([jnp.tile(cos_h, (1, 6)), ones], axis=1)
    sa_t = jnp.concatenate([jnp.tile(sa_h, (1, 6)), zeros], axis=1)
    sb_t = jnp.concatenate([jnp.tile(sb_h, (1, 6)), zeros], axis=1)
    ident = jnp.ones((TM, 512), F32)
    none = jnp.zeros((TM, 512), F32)
    return (jnp.concatenate([cos_t, ident], axis=0), jnp.concatenate([sa_t, none], axis=0),
            jnp.concatenate([sb_t, none], axis=0))


def _dft_tables(length):
    n = 2 * length
    f = jnp.arange(length, dtype=jnp.int32)
    m = ((2 * f[:, None] + 1) * (2 * f[None, :] + 1)) % (4 * n)
    th = m.astype(F32) * (2.0 * math.pi / (4 * n))
    phi = (2 * f + 1).astype(F32) * (math.pi / (2 * n))
    return (jnp.cos(th).astype(BF16), jnp.sin(th).astype(BF16),
            jnp.cos(phi)[:, None], jnp.sin(phi)[:, None])


def _hyena_features(length):
    t = jnp.arange(length, dtype=F32)
    t_norm = t / max(length - 1, 1)
    bands = jnp.linspace(1e-4, HY_BANDS - 1, HY_BANDS, dtype=F32)
    ang = (2.0 * math.pi / length) * t[:, None] * bands[None, :]
    z = jnp.concatenate([t_norm[:, None], jnp.cos(ang), -jnp.sin(ang)], axis=-1)
    z = jnp.pad(z, ((0, 0), (0, 128 - z.shape[1])))
    decay = jnp.tile(jnp.linspace(HY_DECAY_MIN, HY_DECAY_MAX, HY_CH, dtype=F32), 2)
    return z, jnp.exp(-t_norm[:, None] * decay[None, :])


def _hyfilt_kernel(z_ref, w1_ref, b1_ref, w2_ref, b2_ref, w3_ref, fr_ref, dec_ref,
                   c_ref, s_ref, cp_ref, sp_ref, kre_ref, kim_ref, pq_ref, *, length):
    @pl.when(pl.program_id(0) == 0)
    def _():
        fr = fr_ref[...]
        h = jnp.sin(fr * (_dot(z_ref[...], w1_ref[...], HI) + b1_ref[...]))
        h = jnp.sin(fr * (_dot(h, w2_ref[...], HI) + b2_ref[...]))
        h = _dot(h, w3_ref[...], HI) * dec_ref[...]
        hf = h[:, :HY_CH]
        t = lax.broadcasted_iota(jnp.int32, (length, HY_CH), 0)
        hb = jnp.where(t == 0, 0.0, h[:, HY_CH:])
        pq_ref[:, :HY_CH] = (hf + hb).astype(BF16)
        pq_ref[:, HY_CH:] = (hb - hf).astype(BF16)

    pq = pq_ref[...]
    cpq = _dot(c_ref[...], pq)
    spq = _dot(s_ref[...], pq)
    cphi, sphi = cp_ref[...], sp_ref[...]
    norm = 1.0 / length
    kre_ref[...] = (cphi * cpq[:, :HY_CH] + sphi * spq[:, :HY_CH]) * norm
    kim_ref[...] = (cphi * spq[:, HY_CH:] - sphi * cpq[:, HY_CH:]) * norm


def _hyena_filter_spectrum(length, w1, b1, w2, b2, w3, freq, dft):
    z, dec = _hyena_features(length)
    c_m, s_m, cphi, sphi = dft
    w1p = jnp.pad(w1, ((0, 128 - w1.shape[0]), (0, 0)))
    tf = min(512, length)
    const = lambda j: (0, 0)
    blk = lambda j: (j, 0)
    return pl.pallas_call(
        functools.partial(_hyfilt_kernel, length=length),
        grid=(length // tf,),
        in_specs=[
            pl.BlockSpec((length, 128), const),
            pl.BlockSpec((128, 64), const),
            pl.BlockSpec((1, 64), const),
            pl.BlockSpec((64, 64), const),
            pl.BlockSpec((1, 64), const),
            pl.BlockSpec((64, 2 * HY_CH), const),
            pl.BlockSpec((1, 64), const),
            pl.BlockSpec((length, 2 * HY_CH), const),
            pl.BlockSpec((tf, length), blk),
            pl.BlockSpec((tf, length), blk),
            pl.BlockSpec((tf, 1), blk),
            pl.BlockSpec((tf, 1), blk),
        ],
        out_specs=[pl.BlockSpec((tf, HY_CH), blk)] * 2,
        out_shape=[jax.ShapeDtypeStruct((length, HY_CH), F32)] * 2,
        scratch_shapes=[pltpu.VMEM((length, 2 * HY_CH), BF16)],
        compiler_params=_cp("arbitrary"),
        name=f"hyena_filter_{length}",
    )(z, w1p, b1[None, :], w2, b2[None, :], w3, freq[None, :], dec, c_m, s_m, cphi, sphi)


def _short_conv3(x_ref, cw):
    x = x_ref[...].astype(F32)
    return _shift_rows(x, -1) * cw[0:1] + x * cw[1:2] + _shift_rows(x, 1) * cw[2:3]


def _hyena_kernel(p0_ref, p1_ref, p2_ref, cw_ref, bias_ref, c_ref, s_ref, kre_ref, kim_ref, o_ref):
    cw = cw_ref[...]
    x0 = _short_conv3(p0_ref, cw[:, :HY_CH])
    k = _short_conv3(p1_ref, cw[:, HY_CH:2 * HY_CH]) * _short_conv3(p2_ref, cw[:, 2 * HY_CH:])
    kb = k.astype(BF16)
    a = _dot(c_ref[...], kb)
    b = _dot(s_ref[...], kb)
    kre, kim = kre_ref[...], kim_ref[...]
    yre = (a * kre + b * kim).astype(BF16)
    yim = (b * kre - a * kim).astype(BF16)
    y = _dot(c_ref[...], yre) + _dot(s_ref[...], yim)
    o_ref[...] = (x0 * (y + k * bias_ref[...])).astype(o_ref.dtype)


def _hyena(p_hy, length, blk0, conv_w, bias, dft, kre, kim):
    c_m, s_m = dft[0], dft[1]
    const = lambda b: (0, 0)
    once = pl.Buffered(1)
    in_specs = [
        pl.BlockSpec((length, HY_CH), lambda b: (blk0 + b, 0)),
        pl.BlockSpec((length, HY_CH), lambda b: (blk0 + b, 1)),
        pl.BlockSpec((length, HY_CH), lambda b: (blk0 + b, 2)),
        pl.BlockSpec((3, 768), const),
        pl.BlockSpec((1, HY_CH), const),
        pl.BlockSpec((length, length), const, pipeline_mode=once),
        pl.BlockSpec((length, length), const, pipeline_mode=once),
        pl.BlockSpec((length, HY_CH), const, pipeline_mode=once),
        pl.BlockSpec((length, HY_CH), const, pipeline_mode=once),
    ]
    return pl.pallas_call(
        _hyena_kernel,
        grid=(B,),
        in_specs=in_specs,
        out_specs=pl.BlockSpec((length, HY_CH), lambda b: (b, 0)),
        out_shape=jax.ShapeDtypeStruct((B * length, HY_CH), BF16),
        compiler_params=_cp("arbitrary"),
        name=f"hyena_{length}",
    )(p_hy, p_hy, p_hy, conv_w, bias[None, :], c_m, s_m, kre, kim)


def _softmax_pv(parts, extra_logit=None):
    m = None
    for s, _ in parts:
        mi = jnp.max(s, axis=-1, keepdims=True)
        m = mi if m is None else jnp.maximum(m, mi)
    if extra_logit is not None:
        m = jnp.maximum(m, extra_logit)
    den = 0.0 if extra_logit is None else jnp.exp(extra_logit - m)
    acc = None
    for s, v in parts:
        e = jnp.exp(s - m)
        den = den + jnp.sum(e, axis=-1, keepdims=True)
        o = _dot(e.astype(BF16), v)
        acc = o if acc is None else acc + o
    return acc / den


def _wa_kernel(sink_ref, q_ref, k_ref, v_ref, kc_ref, vc_ref, o_ref):
    n = pl.program_id(1)
    start = pl.multiple_of(jnp.clip((n - 1) * 128, 0, S - 384), 128)
    kw = k_ref[pl.ds(start, 384), :]
    vw = v_ref[pl.ds(start, 384), :]
    kc, vc = kc_ref[...], vc_ref[...]
    q = q_ref[...]
    rr = lax.broadcasted_iota(jnp.int32, (256, 384), 0)
    qpos = n * 128 + jnp.where(rr >= 128, rr - 128, rr)
    kpos = start + lax.broadcasted_iota(jnp.int32, (256, 384), 1)
    valid = jnp.abs(qpos - kpos) <= WINDOW
    r1 = lax.broadcasted_iota(jnp.int32, (256, 1), 0)
    for hk in range(2):
        q2 = jnp.concatenate([q[:, (2 * hk) * HD:(2 * hk + 1) * HD],
                              q[:, (2 * hk + 1) * HD:(2 * hk + 2) * HD]], axis=0)
        hs = slice(hk * HD, (hk + 1) * HD)
        s_loc = jnp.where(valid, _dot_nt(q2, kw[:, hs]) * SCALE, NEG)
        s_ctx = _dot_nt(q2, kc[:, hs]) * SCALE
        sink = jnp.where(r1 >= 128, sink_ref[2 * hk + 1], sink_ref[2 * hk])
        o = _softmax_pv([(s_ctx, vc[:, hs]), (s_loc, vw[:, hs])], sink)
        o_ref[:, (2 * hk) * HD:(2 * hk + 1) * HD] = o[:128].astype(o_ref.dtype)
        o_ref[:, (2 * hk + 1) * HD:(2 * hk + 2) * HD] = o[128:].astype(o_ref.dtype)


def _window_attention(p_wa, sink):
    nb = S // 128
    return pl.pallas_call(
        _wa_kernel,
        grid=(B, nb),
        in_specs=[
            pl.BlockSpec(memory_space=pltpu.SMEM),
            pl.BlockSpec((128, 256), lambda b, n: (b * nb + n, 0)),
            pl.BlockSpec((S, 128), lambda b, n: (b, 2)),
            pl.BlockSpec((S, 128), lambda b, n: (b, 3)),
            pl.BlockSpec((LC, 128), lambda b, n: (CTX_BLK0 + b, 2)),
            pl.BlockSpec((LC, 128), lambda b, n: (CTX_BLK0 + b, 3)),
        ],
        out_specs=pl.BlockSpec((128, 256), lambda b, n: (b * nb + n, 0)),
        out_shape=jax.ShapeDtypeStruct((NL, 256), BF16),
        compiler_params=_cp("arbitrary", "arbitrary"),
        name="window_attn",
    )(sink, p_wa, p_wa, p_wa, p_wa, p_wa)


def _na_kernel(q_ref, k_ref, v_ref, kc_ref, vc_ref, bias_ref, o_ref):
    r = pl.program_id(1)
    start = pl.multiple_of(jnp.clip(r - NA_KR // 2, 0, S // GW - NA_KR) * GW, GW)
    kw = k_ref[pl.ds(start, NA_KR * GW), :]
    vw = v_ref[pl.ds(start, NA_KR * GW), :]
    kc, vc = kc_ref[...], vc_ref[...]
    q = q_ref[...]
    for h in range(4):
        hs = slice(h * HD, (h + 1) * HD)
        s_loc = _dot_nt(q[:, hs], kw[:, hs]) * SCALE + bias_ref[h, 0]
        s_ctx = _dot_nt(q[:, hs], kc[:, hs]) * SCALE
        o = _softmax_pv([(s_ctx, vc[:, hs]), (s_loc, vw[:, hs])])
        o_ref[:, hs] = o.astype(o_ref.dtype)


def _na_bias_table(rpb):
    o = jnp.arange(NA_KR)[:, None, None, None]
    j = jnp.arange(NA_KR)[None, None, :, None]
    qc = jnp.arange(GW)[None, :, None, None]
    kcol = jnp.arange(GW)[None, None, None, :]
    ri = jnp.broadcast_to(o + j, (NA_KR, GW, NA_KR, GW))
    ci = jnp.broadcast_to(jnp.clip(kcol - qc + NA_KC - 1, 0, 2 * NA_KC - 2), (NA_KR, GW, NA_KR, GW))
    c_start = jnp.clip(qc - NA_KC // 2, 0, GW - NA_KC)
    ok = jnp.broadcast_to((kcol >= c_start) & (kcol < c_start + NA_KC), (NA_KR, GW, NA_KR, GW))
    tbl = jnp.where(ok[None], rpb.astype(F32)[:, ri, ci], NEG)
    return tbl.reshape(4, NA_KR, GW, NA_KR * GW)


def _neighborhood_attention(p_na, bias_tbl):
    rows = S // GW

    def off(r):
        return jnp.clip(r - NA_KR // 2, 0, rows - NA_KR) - r + NA_KR - 1

    return pl.pallas_call(
        _na_kernel,
        grid=(B, rows),
        in_specs=[
            pl.BlockSpec((GW, 256), lambda b, r: (b * rows + r, 0)),
            pl.BlockSpec((S, 256), lambda b, r: (b, 1)),
            pl.BlockSpec((S, 256), lambda b, r: (b, 2)),
            pl.BlockSpec((LC, 256), lambda b, r: (CTX_BLK0 + b, 1)),
            pl.BlockSpec((LC, 256), lambda b, r: (CTX_BLK0 + b, 2)),
            pl.BlockSpec((4, 1, GW, NA_KR * GW), lambda b, r: (0, off(r), 0, 0)),
        ],
        out_specs=pl.BlockSpec((GW, 256), lambda b, r: (b * rows + r, 0)),
        out_shape=jax.ShapeDtypeStruct((NL, 256), BF16),
        compiler_params=_cp("arbitrary", "arbitrary"),
        name="neighborhood_attn",
    )(p_na, p_na, p_na, p_na, p_na, bias_tbl)


def _ctx_attn_kernel(sink_ref, qw_ref, kw_ref, vw_ref, qn_ref, kn_ref, vn_ref, owa_ref, ona_ref):
    q, k, v = qw_ref[...], kw_ref[...], vw_ref[...]
    r1 = lax.broadcasted_iota(jnp.int32, (2 * LC, 1), 0)
    for hk in range(2):
        q2 = jnp.concatenate([q[:, (2 * hk) * HD:(2 * hk + 1) * HD],
                              q[:, (2 * hk + 1) * HD:(2 * hk + 2) * HD]], axis=0)
        hs = slice(hk * HD, (hk + 1) * HD)
        sink = jnp.where(r1 >= LC, sink_ref[2 * hk + 1], sink_ref[2 * hk])
        o = _softmax_pv([(_dot_nt(q2, k[:, hs]) * SCALE, v[:, hs])], sink)
        owa_ref[:, (2 * hk) * HD:(2 * hk + 1) * HD] = o[:LC].astype(owa_ref.dtype)
        owa_ref[:, (2 * hk + 1) * HD:(2 * hk + 2) * HD] = o[LC:].astype(owa_ref.dtype)
    q, k, v = qn_ref[...], kn_ref[...], vn_ref[...]
    for h in range(4):
        hs = slice(h * HD, (h + 1) * HD)
        o = _softmax_pv([(_dot_nt(q[:, hs], k[:, hs]) * SCALE, v[:, hs])])
        ona_ref[:, hs] = o.astype(ona_ref.dtype)


def _ctx_attention(p_wa, p_na, sink):
    blk = lambda c: (lambda b: (CTX_BLK0 + b, c))
    return pl.pallas_call(
        _ctx_attn_kernel,
        grid=(B,),
        in_specs=[
            pl.BlockSpec(memory_space=pltpu.SMEM),
            pl.BlockSpec((LC, 256), blk(0)),
            pl.BlockSpec((LC, 128), blk(2)),
            pl.BlockSpec((LC, 128), blk(3)),
            pl.BlockSpec((LC, 256), blk(0)),
            pl.BlockSpec((LC, 256), blk(1)),
            pl.BlockSpec((LC, 256), blk(2)),
        ],
        out_specs=[pl.BlockSpec((LC, 256), lambda b: (b, 0))] * 2,
        out_shape=[jax.ShapeDtypeStruct((NC, 256), BF16)] * 2,
        compiler_params=_cp("arbitrary"),
        name="ctx_attn",
    )(sink, p_wa, p_wa, p_wa, p_na, p_na, p_na)


def _gdn_prep_seq(p_ref, ab_ref, cw, alog, dtb, qkv_ref, gb_ref, row0, length):
    x = p_ref[:, :768].astype(F32)
    u = (_shift_rows(x, -2) * cw[0:1] + _shift_rows(x, -1) * cw[1:2] + x * cw[2:3]
         + _shift_rows(x, 1) * cw[3:4])
    u = _silu(u)
    rows = slice(row0, row0 + length)
    for j in range(8):
        sl = slice(j * HD, (j + 1) * HD)
        xs = u[:, sl]
        nrm = lax.rsqrt(jnp.sum(xs * xs, axis=-1, keepdims=True) + EPS)
        qkv_ref[0, rows, sl] = xs * (nrm * SCALE if j < 4 else nrm)
    qkv_ref[0, rows, 512:768] = u[:, 512:768]
    ab = ab_ref[...]
    g = -jnp.exp(alog) * _softplus(ab + dtb)
    lane = lax.broadcasted_iota(jnp.int32, ab.shape, 1)
    gb_ref[0, rows, :] = jnp.where(lane < 8, g, _sigmoid(ab))


def _gdn_prep_kernel(pl_ref, abl_ref, pc_ref, abc_ref, cw_ref, alog_ref, dtb_ref, qkv_ref, gb_ref):
    cw, alog, dtb = cw_ref[...], alog_ref[...], dtb_ref[...]
    _gdn_prep_seq(pl_ref, abl_ref, cw, alog, dtb, qkv_ref, gb_ref, 0, S)
    _gdn_prep_seq(pc_ref, abc_ref, cw, alog, dtb, qkv_ref, gb_ref, S, LC)


def _gdn_prep(p_gd, p_ab, conv_w, a_log, dt_bias):
    const = lambda b: (0, 0)
    alog = jnp.pad(a_log.reshape(1, 8), ((0, 0), (0, 120)))
    dtb = jnp.pad(dt_bias.reshape(1, 8), ((0, 0), (0, 120)))
    return pl.pallas_call(
        _gdn_prep_kernel,
        grid=(B,),
        in_specs=[
            pl.BlockSpec((S, 1024), lambda b: (b, 0)),
            pl.BlockSpec((S, 128), lambda b: (b, 0)),
            pl.BlockSpec((LC, 1024), lambda b: (CTX_BLK0 + b, 0)),
            pl.BlockSpec((LC, 128), lambda b: (CTX_BLK0 + b, 0)),
            pl.BlockSpec((4, 768), const),
            pl.BlockSpec((1, 128), const),
            pl.BlockSpec((1, 128), const),
        ],
        out_specs=[pl.BlockSpec((1, S + LC, 768), lambda b: (b, 0, 0)),
                   pl.BlockSpec((1, S + LC, 128), lambda b: (b, 0, 0))],
        out_shape=[jax.ShapeDtypeStruct((B, S + LC, 768), F32), jax.ShapeDtypeStruct((B, S + LC, 128), F32)],
        compiler_params=_cp("arbitrary"),
        name="gdn_prep",
    )(p_gd, p_ab, p_gd, p_ab, conv_w, alog, dtb)


def _gdn_chains(xs, gbs, cums, cum_ts, states, masks):
    chains = [(dirn, h) for dirn in range(2) for h in range(4)]
    pre = []
    for dirn, h in chains:
        x, gb = xs[dirn], gbs[dirn]
        tri, strict = masks[dirn]
        col = dirn * 4 + h
        qh = x[:, h * HD:(h + 1) * HD]
        kh = x[:, 256 + h * HD:256 + (h + 1) * HD]
        vh = x[:, 512 + h * HD:512 + (h + 1) * HD]
        gc = cums[dirn][:, col:col + 1]
        gc_row = cum_ts[dirn][col:col + 1, :]
        beta = gb[:, 8 + col:9 + col]
        decay = jnp.exp(jnp.where(tri, gc - gc_row, NEG))
        kb = kh * beta
        eg = jnp.exp(gc)
        g_last = gc[0:1, :] if dirn else gc[CHUNK - 1:CHUNK, :]
        pre.append(dict(strict=strict, decay=decay, kbf=kb.astype(BF16), khf=kh.astype(BF16), qf=qh.astype(BF16),
                        xw=jnp.concatenate([vh * beta, kb * eg], axis=1), qdec=(qh * eg).astype(BF16),
                        kdec=(kh * jnp.exp(g_last - gc)).astype(BF16), gl=jnp.exp(g_last)))
    n = len(chains)
    gram = [_dot_nt(p["kbf"], p["khf"]) for p in pre]
    attn = [_dot_nt(p["qf"], p["khf"]) for p in pre]
    lmat = [jnp.where(p["strict"], g * p["decay"], 0.0) for p, g in zip(pre, gram)]
    attn = [(a * p["decay"]).astype(BF16) for p, a in zip(pre, attn)]
    ii = lax.broadcasted_iota(jnp.int32, (CHUNK, CHUNK), 0)
    jj = lax.broadcasted_iota(jnp.int32, (CHUNK, CHUNK), 1)

    def merged_off_blocks(log_s):
        return ((ii >> (log_s + 1)) == (jj >> (log_s + 1))) & ((ii >> log_s) != (jj >> log_s))

    eye = (ii == jj).astype(F32)
    dinv = [eye - jnp.where(merged_off_blocks(0), lm, 0.0) for lm in lmat]
    for log_s in range(1, 6):
        m = merged_off_blocks(log_s)
        dinv_b = [d.astype(BF16) for d in dinv]
        ld = [_dot(jnp.where(m, lmat[c], 0.0).astype(BF16), dinv_b[c]) for c in range(n)]
        upd = [_dot(dinv_b[c], ld[c].astype(BF16)) for c in range(n)]
        dinv = [dinv[c] - upd[c] for c in range(n)]
    xw = [_dot(dinv[c].astype(BF16), pre[c]["xw"].astype(BF16)) for c in range(n)]
    sts = [states[d][h] for d, h in chains]
    stb = [s.astype(BF16) for s in sts]
    ws = [_dot(xw[c][:, HD:].astype(BF16), stb[c]) for c in range(n)]
    qs = [_dot(pre[c]["qdec"], stb[c]) for c in range(n)]
    u_new = [(xw[c][:, :HD] - ws[c]).astype(BF16) for c in range(n)]
    au = [_dot(attn[c], u_new[c]) for c in range(n)]
    ku = [_dot_tn(pre[c]["kdec"], u_new[c]) for c in range(n)]
    outs = [qs[c] + au[c] for c in range(n)]
    new_states = [sts[c] * pre[c]["gl"] + ku[c] for c in range(n)]
    return outs, new_states


def _gdn_kernel(x0_ref, gb0_ref, x1_ref, gb1_ref, o0_ref, o1_ref, state_ref):
    @pl.when(pl.program_id(0) == 0)
    def _():
        state_ref[...] = jnp.zeros_like(state_ref)

    ii = lax.broadcasted_iota(jnp.int32, (CHUNK, CHUNK), 0)
    jj = lax.broadcasted_iota(jnp.int32, (CHUNK, CHUNK), 1)
    masks = ((jj <= ii, jj < ii), (jj >= ii, jj > ii))
    ones_tri = ((jj <= ii).astype(F32), (jj >= ii).astype(F32))

    def body(b, carry):
        states = [[state_ref[b, dirn, h] for h in range(4)] for dirn in range(2)]
        xs = (x0_ref[b], x1_ref[b])
        gbs = (gb0_ref[b], gb1_ref[b])
        cums = [_dot(ones_tri[d], gbs[d], HI) for d in range(2)]
        cum_ts = [cm.T for cm in cums]
        outs, new_states = _gdn_chains(xs, gbs, cums, cum_ts, states, masks)
        for c in range(8):
            state_ref[b, c // 4, c % 4] = new_states[c]
        o0_ref[b] = jnp.concatenate(outs[:4], axis=1)
        o1_ref[b] = jnp.concatenate(outs[4:], axis=1)
        return carry

    lax.fori_loop(0, B, body, 0)


def _gdn_scan(qkv, gb):
    nc = (S + LC) // CHUNK
    fwd = lambda c: (0, (c + S // CHUNK) % nc, 0)
    bwd = lambda c: (0, nc - 1 - c, 0)
    return pl.pallas_call(
        _gdn_kernel,
        grid=(nc,),
        in_specs=[
            pl.BlockSpec((B, CHUNK, 768), fwd),
            pl.BlockSpec((B, CHUNK, 128), fwd),
            pl.BlockSpec((B, CHUNK, 768), bwd),
            pl.BlockSpec((B, CHUNK, 128), bwd),
        ],
        out_specs=[pl.BlockSpec((B, CHUNK, 256), fwd), pl.BlockSpec((B, CHUNK, 256), bwd)],
        out_shape=[jax.ShapeDtypeStruct((B, S + LC, 256), F32)] * 2,
        scratch_shapes=[pltpu.VMEM((B, 2, 4, HD, HD), F32)],
        compiler_params=_cp("arbitrary"),
        name="gdn_scan",
    )(qkv, gb, qkv, gb)


def _gdn_finish_kernel(o0_ref, o1_ref, gate_ref, ng_ref, y_ref):
    o = o0_ref[0] + o1_ref[0]
    gi = lax.broadcasted_iota(jnp.int32, (256, 256), 0) // HD
    gj = lax.broadcasted_iota(jnp.int32, (256, 256), 1) // HD
    group_mean = jnp.where(gi == gj, 1.0 / HD, 0.0)
    ms = _dot(o * o, group_mean, HI)
    y = o * lax.rsqrt(ms + EPS) * ng_ref[...]
    y_ref[...] = (y * _silu(gate_ref[...].astype(F32))).astype(y_ref.dtype)


def _gdn_finish(o0, o1, p_gd, norm_g, need_ctx):
    lat_tiles = S // TM
    row_blk = lambda b, j: jnp.where(j < lat_tiles, b * lat_tiles + j, CTX_BLK0 + b)
    return pl.pallas_call(
        _gdn_finish_kernel,
        grid=(B, lat_tiles + (1 if need_ctx else 0)),
        in_specs=[
            pl.BlockSpec((1, TM, 256), lambda b, j: (b, j, 0)),
            pl.BlockSpec((1, TM, 256), lambda b, j: (b, j, 0)),
            pl.BlockSpec((TM, 256), lambda b, j: (row_blk(b, j), 3)),
            pl.BlockSpec((1, 256), lambda b, j: (0, 0)),
        ],
        out_specs=pl.BlockSpec((TM, 256), lambda b, j: (row_blk(b, j), 0)),
        out_shape=jax.ShapeDtypeStruct((ROWS if need_ctx else NL, 256), BF16),
        compiler_params=_cp("arbitrary", "arbitrary"),
        name="gdn_finish",
    )(o0, o1, p_gd, jnp.tile(norm_g, 4)[None, :])


def _outproj_kernel(x_ref, mod_ref, g2_ref, yh_ref, yw_ref, yn_ref, yg_ref, wo_ref, wr_ref, eb_ref,
                    xo_ref, h2_ref, idx_ref, tw_ref):
    m = mod_ref[0]
    acc = (_dot(yh_ref[...], wo_ref[0:256, :]) + _dot(yw_ref[...], wo_ref[256:512, :])
           + _dot(yn_ref[...], wo_ref[512:768, :]) + _dot(yg_ref[...], wo_ref[768:1024, :]))
    x = x_ref[...] + m[2:3] * acc
    xo_ref[...] = x
    h2 = _modulated_norm(x, g2_ref[...], m[3:4], m[4:5])
    h2_ref[...] = h2.astype(h2_ref.dtype)
    scores = _sigmoid(_dot(h2, wr_ref[...], HI))
    lane = lax.broadcasted_iota(jnp.int32, scores.shape, 1)
    sel = jnp.where(lane < N_EXP, scores + eb_ref[...], -jnp.inf)
    lane_f = lane.astype(F32)
    idx_out = jnp.zeros(scores.shape, F32)
    s_out = jnp.zeros(scores.shape, F32)
    for kk in range(TOP_K):
        mx = jnp.max(sel, axis=-1, keepdims=True)
        idx = jnp.min(jnp.where(sel == mx, lane_f, 128.0), axis=-1, keepdims=True)
        hit = lane_f == idx
        sk = jnp.sum(jnp.where(hit, scores, 0.0), axis=-1, keepdims=True)
        sel = jnp.where(hit, -jnp.inf, sel)
        idx_out = jnp.where(lane == kk, idx, idx_out)
        s_out = jnp.where(lane == kk, sk, s_out)
    tot = jnp.sum(s_out, axis=-1, keepdims=True)
    idx_ref[...] = idx_out.astype(jnp.int32)
    tw_ref[...] = s_out / tot * ROUTED_SCALE


def _outproj(xa, mod, l, g2, ys, wo, wr, eb, n_tiles):
    row = lambda i: (i, 0)
    const = lambda i: (0, 0)
    rows = n_tiles * TM
    return pl.pallas_call(
        _outproj_kernel,
        grid=(n_tiles,),
        in_specs=[
            pl.BlockSpec((TM, D), row),
            pl.BlockSpec((1, 6, D), lambda i: (l * 16 + (i * TM) // S, 0, 0)),
            pl.BlockSpec((1, D), const),
            pl.BlockSpec((TM, 256), row),
            pl.BlockSpec((TM, 256), row),
            pl.BlockSpec((TM, 256), row),
            pl.BlockSpec((TM, 256), row),
            pl.BlockSpec((D, D), const),
            pl.BlockSpec((D, 128), const),
            pl.BlockSpec((1, 128), const),
        ],
        out_specs=[
            pl.BlockSpec((TM, D), row),
            pl.BlockSpec((TM, D), row),
            pl.BlockSpec((TM, 128), row),
            pl.BlockSpec((TM, 128), row),
        ],
        out_shape=[
            jax.ShapeDtypeStruct((rows, D), F32),
            jax.ShapeDtypeStruct((rows, D), BF16),
            jax.ShapeDtypeStruct((rows, 128), jnp.int32),
            jax.ShapeDtypeStruct((rows, 128), F32),
        ],
        compiler_params=_cp("arbitrary"),
        name="outproj_router",
    )(xa, mod, g2, *ys, wo, wr, eb)


def _moe_kernel(be_ref, x_ref, rw_ref, wg_ref, wu_ref, wd_ref, o_ref):
    del be_ref
    x = x_ref[...]
    g = _dot(x, wg_ref[0].astype(BF16))
    u = _dot(x, wu_ref[0].astype(BF16))
    hid = (_silu(g) * u).astype(BF16)
    y = _dot(hid, wd_ref[0].astype(BF16))
    o_ref[...] = (y * rw_ref[...]).astype(o_ref.dtype)


def _moe_experts(blk_e, xb, row_w, wg, wu, wd):
    n_rows = xb.shape[0]
    n_blocks = n_rows // MOE_BLK
    grid_spec = pltpu.PrefetchScalarGridSpec(
        num_scalar_prefetch=1,
        grid=(n_blocks,),
        in_specs=[
            pl.BlockSpec((MOE_BLK, D), lambda i, be: (i, 0)),
            pl.BlockSpec((MOE_BLK, 1), lambda i, be: (i, 0)),
            pl.BlockSpec((1, D, D_EXP), lambda i, be: (be[i], 0, 0)),
            pl.BlockSpec((1, D, D_EXP), lambda i, be: (be[i], 0, 0)),
            pl.BlockSpec((1, D_EXP, D), lambda i, be: (be[i], 0, 0)),
        ],
        out_specs=pl.BlockSpec((MOE_BLK, D), lambda i, be: (i, 0)),
    )
    return pl.pallas_call(
        _moe_kernel,
        grid_spec=grid_spec,
        out_shape=jax.ShapeDtypeStruct((n_rows, D), F32),
        compiler_params=_cp("arbitrary"),
        name="moe_experts",
    )(blk_e, xb, row_w, wg, wu, wd)


def _dispatch(top_idx, top_w):
    n_tok = top_idx.shape[0]
    n_asg = n_tok * TOP_K
    flat_e = top_idx.reshape(n_asg)
    flat_w = top_w.reshape(n_asg)
    counts = jnp.sum((flat_e[:, None] == jnp.arange(N_EXP)[None, :]).astype(jnp.int32), axis=0)
    padded = (counts + MOE_BLK - 1) // MOE_BLK * MOE_BLK
    pad_end = jnp.cumsum(padded)
    pad_start = pad_end - padded
    raw_start = jnp.cumsum(counts) - counts
    order = jnp.argsort(flat_e, stable=True)
    sorted_e = flat_e[order]
    dest = (pad_start[sorted_e] + jnp.arange(n_asg) - raw_start[sorted_e]).astype(jnp.int32)
    n_blocks = n_asg // MOE_BLK + N_EXP
    n_rows = n_blocks * MOE_BLK
    row_tok = jnp.zeros((n_rows,), jnp.int32).at[dest].set((order // TOP_K).astype(jnp.int32))
    row_w = jnp.zeros((n_rows,), F32).at[dest].set(flat_w[order])
    blk_e = jnp.minimum(jnp.searchsorted(pad_end, jnp.arange(n_blocks) * MOE_BLK, side='right'),
                        N_EXP - 1).astype(jnp.int32)
    pos = jnp.zeros((n_asg,), jnp.int32).at[order].set(dest)
    return row_tok, row_w, blk_e, pos.reshape(n_tok, TOP_K)


def _ffn_out_kernel(x_ref, mod_ref, h2_ref, r_ref, wg_ref, wu_ref, wd_ref, nf_ref, o_ref, *, final):
    m = mod_ref[0]
    h2 = h2_ref[...]
    hid = (_silu(_dot(h2, wg_ref[...])) * _dot(h2, wu_ref[...])).astype(BF16)
    x = x_ref[...] + m[5:6] * (r_ref[...] + _dot(hid, wd_ref[...]))
    if final:
        ms = jnp.mean(x * x, axis=-1, keepdims=True)
        x = x * lax.rsqrt(ms + EPS) * nf_ref[...]
    o_ref[...] = x


def _ffn_out(xn, mod, l, h2, routed, wg, wu, wd, norm_f, n_tiles, final):
    row = lambda i: (i, 0)
    const = lambda i: (0, 0)
    return pl.pallas_call(
        functools.partial(_ffn_out_kernel, final=final),
        grid=(n_tiles,),
        in_specs=[
            pl.BlockSpec((TM, D), row),
            pl.BlockSpec((1, 6, D), lambda i: (l * 16 + (i * TM) // S, 0, 0)),
            pl.BlockSpec((TM, D), row),
            pl.BlockSpec((TM, D), row),
            pl.BlockSpec((D, D_EXP), const),
            pl.BlockSpec((D, D_EXP), const),
            pl.BlockSpec((D_EXP, D), const),
            pl.BlockSpec((1, D), const),
        ],
        out_specs=pl.BlockSpec((TM, D), row),
        out_shape=jax.ShapeDtypeStruct((n_tiles * TM, D), F32),
        compiler_params=_cp("arbitrary"),
        name="shared_ffn_residual",
    )(xn, mod, h2, routed, wg, wu, wd, norm_f)


def kernel(x, c, ctx, c_ctx, w_ada, b_ada, norm1, norm2, norm_f, w_in, w_out, hy_conv, hy_w1, hy_b1, hy_w2, hy_b2, hy_w3, hy_freq, hy_bias, wa_sink, na_rpb, gdn_conv, gdn_a_log, gdn_dt_bias, gdn_norm, moe_router, moe_bias, moe_gate, moe_up, moe_down, sh_gate, sh_up, sh_down):
    depth = w_ada.shape[0]
    xa = jnp.concatenate([x.reshape(NL, D), ctx.reshape(NC, D)], axis=0)
    cvec = jnp.concatenate([c, c_ctx[None, :], jnp.zeros((16 - B - 1, D), F32)], axis=0)
    mod = _ada(cvec, w_ada, b_ada).reshape(depth * 16, 6, D)
    rope = _rope_tables()
    dft_lat = _dft_tables(S)
    dft_ctx = _dft_tables(LC)
    o1, o2, o3 = 768, 768 + 512, 768 + 512 + 768

    for l in range(depth):
        need_ctx = l < depth - 1
        n_tiles = NT_ALL if need_ctx else NT_LAT
        wl = w_in[l].astype(BF16)
        ws = (wl[:, :o1], wl[:, o1:o2], wl[:, o2:o3], wl[:, o3:o3 + 1024],
              jnp.pad(wl[:, o3 + 1024:], ((0, 0), (0, 128 - 16))))
        p_hy, p_wa, p_na, p_gd, p_ab = _inproj(xa, mod, l, norm1[l][None, :], ws, rope)

        filt = (hy_w1[l], hy_b1[l], hy_w2[l], hy_b2[l], hy_w3[l], hy_freq[l])
        kre, kim = _hyena_filter_spectrum(S, *filt, dft_lat)
        y_hy = _hyena(p_hy, S, 0, hy_conv[l], hy_bias[l], dft_lat, kre, kim)
        y_wa = _window_attention(p_wa, wa_sink[l])
        y_na = _neighborhood_attention(p_na, _na_bias_table(na_rpb[l]))
        if need_ctx:
            kre_c, kim_c = _hyena_filter_spectrum(LC, *filt, dft_ctx)
            yc_hy = _hyena(p_hy, LC, CTX_BLK0, hy_conv[l], hy_bias[l], dft_ctx, kre_c, kim_c)
            yc_wa, yc_na = _ctx_attention(p_wa, p_na, wa_sink[l])
            y_hy = jnp.concatenate([y_hy, yc_hy], axis=0)
            y_wa = jnp.concatenate([y_wa, yc_wa], axis=0)
            y_na = jnp.concatenate([y_na, yc_na], axis=0)
        qkv, gb = _gdn_prep(p_gd, p_ab, gdn_conv[l], gdn_a_log[l], gdn_dt_bias[l])
        y_gd = _gdn_finish(*_gdn_scan(qkv, gb), p_gd, gdn_norm[l], need_ctx)

        wr = jnp.pad(moe_router[l], ((0, 0), (0, 128 - N_EXP)))
        eb = jnp.pad(moe_bias[l], (0, 128 - N_EXP))[None, :]
        xn, h2, top_idx, top_w = _outproj(xa, mod, l, norm2[l][None, :], (y_hy, y_wa, y_na, y_gd),
                                          w_out[l].astype(BF16), wr, eb, n_tiles)
        row_tok, row_w, blk_e, pos = _dispatch(top_idx[:, :TOP_K], top_w[:, :TOP_K])
        yb = _moe_experts(blk_e, h2[row_tok], row_w[:, None], moe_gate[l], moe_up[l], moe_down[l])
        routed = jnp.sum(yb[pos], axis=1)
        xa_new = _ffn_out(xn, mod, l, h2, routed, sh_gate[l].astype(BF16), sh_up[l].astype(BF16),
                          sh_down[l].astype(BF16), norm_f[None, :], n_tiles, final=not need_ctx)
        if need_ctx:
            xa = xa_new
        else:
            return xa_new.reshape(B, S, D)
```

```python
import functools
import math

import jax
import jax.numpy as jnp
from jax import lax
from jax.experimental import pallas as pl
from jax.experimental.pallas import tpu as pltpu

F32 = jnp.float32
BF16 = jnp.bfloat16
HI = lax.Precision.HIGHEST

D = 1024
B = 8
S = 2048
LC = 256
GW = 64
HD = 64
NL = B * S
NC = B * LC
ROWS = NL + NC
TM = 256
NT_LAT = NL // TM
NT_ALL = ROWS // TM
CTX_BLK0 = NL // LC

HY_CH = 256
HY_BANDS = 16
HY_DECAY_MIN = -math.log(1e-2) / 1.5
HY_DECAY_MAX = -math.log(1e-2) / 0.3
WINDOW = 128
NA_KR = 8
NA_KC = 16
CHUNK = 64
N_EXP = 32
TOP_K = 4
D_EXP = 256
ROUTED_SCALE = 2.5
MOE_BLK = 256
EPS = 1e-6
NEG = -1e30
SCALE = HD ** -0.5
VMEM_LIMIT = 56 * 1024 * 1024


def _cp(*sem):
    return pltpu.CompilerParams(dimension_semantics=tuple(sem), vmem_limit_bytes=VMEM_LIMIT)


def _dot(a, b, precision=None):
    return jnp.dot(a, b, preferred_element_type=F32, precision=precision)


def _dot_nt(a, b, precision=None):
    return lax.dot_general(a, b, (((1,), (1,)), ((), ())), preferred_element_type=F32, precision=precision)


def _dot_tn(a, b, precision=None):
    return lax.dot_general(a, b, (((0,), (0,)), ((), ())), preferred_element_type=F32, precision=precision)


def _sigmoid(x):
    return 1.0 / (1.0 + jnp.exp(-x))


def _silu(x):
    return x * _sigmoid(x)


def _softplus(x):
    return jnp.maximum(x, 0.0) + jnp.log(1.0 + jnp.exp(-jnp.abs(x)))


def _shift_rows(x, d):
    n = x.shape[0]
    if d == 0:
        return x
    y = pltpu.roll(x, (-d) % n, axis=0)
    t = lax.broadcasted_iota(jnp.int32, x.shape, 0)
    ok = (t + d >= 0) & (t + d < n)
    return jnp.where(ok, y, 0.0)


def _ada_kernel(c_ref, w_ref, b_ref, o_ref):
    s = _silu(c_ref[...])
    o_ref[0] = _dot(s.astype(BF16), w_ref[0].astype(BF16)) + b_ref[0]


def _ada(cvec, w_ada, b_ada):
    nl = w_ada.shape[0]
    tn = 1536
    return pl.pallas_call(
        _ada_kernel,
        grid=(nl, 6 * D // tn),
        in_specs=[
            pl.BlockSpec((16, D), lambda l, j: (0, 0)),
            pl.BlockSpec((1, D, tn), lambda l, j: (l, 0, j)),
            pl.BlockSpec((1, 1, tn), lambda l, j: (l, 0, j)),
        ],
        out_specs=pl.BlockSpec((1, 16, tn), lambda l, j: (l, 0, j)),
        out_shape=jax.ShapeDtypeStruct((nl, 16, 6 * D), F32),
        compiler_params=_cp("arbitrary", "arbitrary"),
        name="adaln",
    )(cvec, w_ada, b_ada.reshape(nl, 1, 6 * D))


def _modulated_norm(x, g, shift, scale):
    ms = jnp.mean(x * x, axis=-1, keepdims=True)
    y = x * lax.rsqrt(ms + EPS) * g
    return y * (1.0 + scale) + shift


def _inproj_kernel(x_ref, mod_ref, g_ref, why_ref, wwa_ref, wna_ref, wgd_ref, wab_ref,
                   cos_ref, sa_ref, sb_ref, ohy, owa, ona, ogd, oab):
    m = mod_ref[0]
    h = _modulated_norm(x_ref[...], g_ref[...], m[0:1], m[1:2]).astype(BF16)
    ohy[...] = _dot(h, why_ref[...]).astype(ohy.dtype)
    ona[...] = _dot(h, wna_ref[...]).astype(ona.dtype)
    ogd[...] = _dot(h, wgd_ref[...]).astype(ogd.dtype)
    oab[...] = _dot(h, wab_ref[...])
    a = _dot(h, wwa_ref[...])
    for c in range(4):
        sl = slice(c * 128, (c + 1) * 128)
        ac = a[:, sl]
        r = (ac * cos_ref[:, sl] + pltpu.roll(ac, 112, axis=1) * sa_ref[:, sl]
             + pltpu.roll(ac, 16, axis=1) * sb_ref[:, sl])
        owa[:, sl] = r.astype(owa.dtype)


def _inproj(xa, mod, l, g, ws, tabs):
    why, wwa, wna, wgd, wab = ws
    cos_t, sa_t, sb_t = tabs
    row = lambda i: (i, 0)
    const = lambda i: (0, 0)
    tab = lambda i: (jnp.where(i < NT_LAT, i % (S // TM), S // TM), 0)
    return pl.pallas_call(
        _inproj_kernel,
        grid=(NT_ALL,),
        in_specs=[
            pl.BlockSpec((TM, D), row),
            pl.BlockSpec((1, 6, D), lambda i: (l * 16 + (i * TM) // S, 0, 0)),
            pl.BlockSpec((1, D), const),
            pl.BlockSpec((D, 768), const),
            pl.BlockSpec((D, 512), const),
            pl.BlockSpec((D, 768), const),
            pl.BlockSpec((D, 1024), const),
            pl.BlockSpec((D, 128), const),
            pl.BlockSpec((TM, 512), tab),
            pl.BlockSpec((TM, 512), tab),
            pl.BlockSpec((TM, 512), tab),
        ],
        out_specs=[
            pl.BlockSpec((TM, 768), row),
            pl.BlockSpec((TM, 512), row),
            pl.BlockSpec((TM, 768), row),
            pl.BlockSpec((TM, 1024), row),
            pl.BlockSpec((TM, 128), row),
        ],
        out_shape=[
            jax.ShapeDtypeStruct((ROWS, 768), BF16),
            jax.ShapeDtypeStruct((ROWS, 512), BF16),
            jax.ShapeDtypeStruct((ROWS, 768), BF16),
            jax.ShapeDtypeStruct((ROWS, 1024), BF16),
            jax.ShapeDtypeStruct((ROWS, 128), F32),
        ],
        compiler_params=_cp("arbitrary"),
        name="inproj",
    )(xa, mod, g, why, wwa, wna, wgd, wab, cos_t, sa_t, sb_t)


def _rope_tables():
    quarter = HD // 4
    pos = jnp.arange(S)
    inv = 10000.0 ** (-jnp.arange(quarter, dtype=F32) / quarter)
    ang_r = (pos // GW).astype(F32)[:, None] * inv[None, :]
    ang_c = (pos % GW).astype(F32)[:, None] * inv[None, :]
    z = jnp.zeros_like(ang_r)
    cos_h = jnp.concatenate([jnp.cos(ang_r)] * 2 + [jnp.cos(ang_c)] * 2, axis=1)
    sa_h = jnp.concatenate([-jnp.sin(ang_r), z, -jnp.sin(ang_c), z], axis=1)
    sb_h = jnp.concatenate([z, jnp.sin(ang_r), z, jnp.sin(ang_c)], axis=1)
    ones = jnp.ones((S, 128), F32)
    zeros = jnp.zeros((S, 128), F32)
    cos_t = jnp.concatenate([jnp.tile(cos_h, (1, 6)), ones], axis=1)
    sa_t = jnp.concatenate([jnp.tile(sa_h, (1, 6)), zeros], axis=1)
    sb_t = jnp.concatenate([jnp.tile(sb_h, (1, 6)), zeros], axis=1)
    ident = jnp.ones((TM, 512), F32)
    none = jnp.zeros((TM, 512), F32)
    return (jnp.concatenate([cos_t, ident], axis=0), jnp.concatenate([sa_t, none], axis=0),
            jnp.concatenate([sb_t, none], axis=0))


def _dft_tables(length):
    n = 2 * length
    f = jnp.arange(length, dtype=jnp.int32)
    m = ((2 * f[:, None] + 1) * (2 * f[None, :] + 1)) % (4 * n)
    th = m.astype(F32) * (2.0 * math.pi / (4 * n))
    phi = (2 * f + 1).astype(F32) * (math.pi / (2 * n))
    return (jnp.cos(th).astype(BF16), jnp.sin(th).astype(BF16),
            jnp.cos(phi)[:, None], jnp.sin(phi)[:, None])


def _hyena_features(length):
    t = jnp.arange(length, dtype=F32)
    t_norm = t / max(length - 1, 1)
    bands = jnp.linspace(1e-4, HY_BANDS - 1, HY_BANDS, dtype=F32)
    ang = (2.0 * math.pi / length) * t[:, None] * bands[None, :]
    z = jnp.concatenate([t_norm[:, None], jnp.cos(ang), -jnp.sin(ang)], axis=-1)
    z = jnp.pad(z, ((0, 0), (0, 128 - z.shape[1])))
    decay = jnp.tile(jnp.linspace(HY_DECAY_MIN, HY_DECAY_MAX, HY_CH, dtype=F32), 2)
    return z, jnp.exp(-t_norm[:, None] * decay[None, :])


def _hyfilt_kernel(z_ref, w1_ref, b1_ref, w2_ref, b2_ref, w3_ref, fr_ref, dec_ref,
                   c_ref, s_ref, cp_ref, sp_ref, kre_ref, kim_ref, pq_ref, *, length):
    @pl.when(pl.program_id(0) == 0)
    def _():
        fr = fr_ref[...]
        h = jnp.sin(fr * (_dot(z_ref[...], w1_ref[...], HI) + b1_ref[...]))
        h = jnp.sin(fr * (_dot(h, w2_ref[...], HI) + b2_ref[...]))
        h = _dot(h, w3_ref[...], HI) * dec_ref[...]
        hf = h[:, :HY_CH]
        t = lax.broadcasted_iota(jnp.int32, (length, HY_CH), 0)
        hb = jnp.where(t == 0, 0.0, h[:, HY_CH:])
        pq_ref[:, :HY_CH] = (hf + hb).astype(BF16)
        pq_ref[:, HY_CH:] = (hb - hf).astype(BF16)

    pq = pq_ref[...]
    cpq = _dot(c_ref[...], pq)
    spq = _dot(s_ref[...], pq)
    cphi, sphi = cp_ref[...], sp_ref[...]
    norm = 1.0 / length
    kre_ref[...] = (cphi * cpq[:, :HY_CH] + sphi * spq[:, :HY_CH]) * norm
    kim_ref[...] = (cphi * spq[:, HY_CH:] - sphi * cpq[:, HY_CH:]) * norm


def _hyena_filter_spectrum(length, w1, b1, w2, b2, w3, freq, dft):
    z, dec = _hyena_features(length)
    c_m, s_m, cphi, sphi = dft
    w1p = jnp.pad(w1, ((0, 128 - w1.shape[0]), (0, 0)))
    tf = min(512, length)
    const = lambda j: (0, 0)
    blk = lambda j: (j, 0)
    return pl.pallas_call(
        functools.partial(_hyfilt_kernel, length=length),
        grid=(length // tf,),
        in_specs=[
            pl.BlockSpec((length, 128), const),
            pl.BlockSpec((128, 64), const),
            pl.BlockSpec((1, 64), const),
            pl.BlockSpec((64, 64), const),
            pl.BlockSpec((1, 64), const),
            pl.BlockSpec((64, 2 * HY_CH), const),
            pl.BlockSpec((1, 64), const),
            pl.BlockSpec((length, 2 * HY_CH), const),
            pl.BlockSpec((tf, length), blk),
            pl.BlockSpec((tf, length), blk),
            pl.BlockSpec((tf, 1), blk),
            pl.BlockSpec((tf, 1), blk),
        ],
        out_specs=[pl.BlockSpec((tf, HY_CH), blk)] * 2,
        out_shape=[jax.ShapeDtypeStruct((length, HY_CH), F32)] * 2,
        scratch_shapes=[pltpu.VMEM((length, 2 * HY_CH), BF16)],
        compiler_params=_cp("arbitrary"),
        name=f"hyena_filter_{length}",
    )(z, w1p, b1[None, :], w2, b2[None, :], w3, freq[None, :], dec, c_m, s_m, cphi, sphi)


def _short_conv3(x_ref, cw):
    x = x_ref[...].astype(F32)
    return _shift_rows(x, -1) * cw[0:1] + x * cw[1:2] + _shift_rows(x, 1) * cw[2:3]


def _hyena_kernel(p0_ref, p1_ref, p2_ref, cw_ref, bias_ref, c_ref, s_ref, kre_ref, kim_ref, o_ref):
    cw = cw_ref[...]
    x0 = _short_conv3(p0_ref, cw[:, :HY_CH])
    k = _short_conv3(p1_ref, cw[:, HY_CH:2 * HY_CH]) * _short_conv3(p2_ref, cw[:, 2 * HY_CH:])
    kb = k.astype(BF16)
    a = _dot(c_ref[...], kb)
    b = _dot(s_ref[...], kb)
    kre, kim = kre_ref[...], kim_ref[...]
    yre = (a * kre + b * kim).astype(BF16)
    yim = (b * kre - a * kim).astype(BF16)
    y = _dot(c_ref[...], yre) + _dot(s_ref[...], yim)
    o_ref[...] = (x0 * (y + k * bias_ref[...])).astype(o_ref.dtype)


def _hyena(p_hy, length, blk0, conv_w, bias, dft, kre, kim):
    c_m, s_m = dft[0], dft[1]
    const = lambda b: (0, 0)
    once = pl.Buffered(1)
    in_specs = [
        pl.BlockSpec((length, HY_CH), lambda b: (blk0 + b, 0)),
        pl.BlockSpec((length, HY_CH), lambda b: (blk0 + b, 1)),
        pl.BlockSpec((length, HY_CH), lambda b: (blk0 + b, 2)),
        pl.BlockSpec((3, 768), const),
        pl.BlockSpec((1, HY_CH), const),
        pl.BlockSpec((length, length), const, pipeline_mode=once),
        pl.BlockSpec((length, length), const, pipeline_mode=once),
        pl.BlockSpec((length, HY_CH), const, pipeline_mode=once),
        pl.BlockSpec((length, HY_CH), const, pipeline_mode=once),
    ]
    return pl.pallas_call(
        _hyena_kernel,
        grid=(B,),
        in_specs=in_specs,
        out_specs=pl.BlockSpec((length, HY_CH), lambda b: (b, 0)),
        out_shape=jax.ShapeDtypeStruct((B * length, HY_CH), BF16),
        compiler_params=_cp("arbitrary"),
        name=f"hyena_{length}",
    )(p_hy, p_hy, p_hy, conv_w, bias[None, :], c_m, s_m, kre, kim)


def _softmax_pv(parts, extra_logit=None):
    m = None
    for s, _ in parts:
        mi = jnp.max(s, axis=-1, keepdims=True)
        m = mi if m is None else jnp.maximum(m, mi)
    if extra_logit is not None:
        m = jnp.maximum(m, extra_logit)
    den = 0.0 if extra_logit is None else jnp.exp(extra_logit - m)
    acc = None
    for s, v in parts:
        e = jnp.exp(s - m)
        den = den + jnp.sum(e, axis=-1, keepdims=True)
        o = _dot(e.astype(BF16), v)
        acc = o if acc is None else acc + o
    return acc / den


def _wa_kernel(sink_ref, q_ref, k_ref, v_ref, kc_ref, vc_ref, o_ref):
    n = pl.program_id(1)
    start = pl.multiple_of(jnp.clip((n - 1) * 128, 0, S - 384), 128)
    kw = k_ref[pl.ds(start, 384), :]
    vw = v_ref[pl.ds(start, 384), :]
    kc, vc = kc_ref[...], vc_ref[...]
    q = q_ref[...]
    rr = lax.broadcasted_iota(jnp.int32, (256, 384), 0)
    qpos = n * 128 + jnp.where(rr >= 128, rr - 128, rr)
    kpos = start + lax.broadcasted_iota(jnp.int32, (256, 384), 1)
    valid = jnp.abs(qpos - kpos) <= WINDOW
    r1 = lax.broadcasted_iota(jnp.int32, (256, 1), 0)
    for hk in range(2):
        q2 = jnp.concatenate([q[:, (2 * hk) * HD:(2 * hk + 1) * HD],
                              q[:, (2 * hk + 1) * HD:(2 * hk + 2) * HD]], axis=0)
        hs = slice(hk * HD, (hk + 1) * HD)
        s_loc = jnp.where(valid, _dot_nt(q2, kw[:, hs]) * SCALE, NEG)
        s_ctx = _dot_nt(q2, kc[:, hs]) * SCALE
        sink = jnp.where(r1 >= 128, sink_ref[2 * hk + 1], sink_ref[2 * hk])
        o = _softmax_pv([(s_ctx, vc[:, hs]), (s_loc, vw[:, hs])], sink)
        o_ref[:, (2 * hk) * HD:(2 * hk + 1) * HD] = o[:128].astype(o_ref.dtype)
        o_ref[:, (2 * hk + 1) * HD:(2 * hk + 2) * HD] = o[128:].astype(o_ref.dtype)


def _window_attention(p_wa, sink):
    nb = S // 128
    return pl.pallas_call(
        _wa_kernel,
        grid=(B, nb),
        in_specs=[
            pl.BlockSpec(memory_space=pltpu.SMEM),
            pl.BlockSpec((128, 256), lambda b, n: (b * nb + n, 0)),
            pl.BlockSpec((S, 128), lambda b, n: (b, 2)),
            pl.BlockSpec((S, 128), lambda b, n: (b, 3)),
            pl.BlockSpec((LC, 128), lambda b, n: (CTX_BLK0 + b, 2)),
            pl.BlockSpec((LC, 128), lambda b, n: (CTX_BLK0 + b, 3)),
        ],
        out_specs=pl.BlockSpec((128, 256), lambda b, n: (b * nb + n, 0)),
        out_shape=jax.ShapeDtypeStruct((NL, 256), BF16),
        compiler_params=_cp("arbitrary", "arbitrary"),
        name="window_attn",
    )(sink, p_wa, p_wa, p_wa, p_wa, p_wa)


def _na_kernel(q_ref, k_ref, v_ref, kc_ref, vc_ref, bias_ref, o_ref):
    r = pl.program_id(1)
    start = pl.multiple_of(jnp.clip(r - NA_KR // 2, 0, S // GW - NA_KR) * GW, GW)
    kw = k_ref[pl.ds(start, NA_KR * GW), :]
    vw = v_ref[pl.ds(start, NA_KR * GW), :]
    kc, vc = kc_ref[...], vc_ref[...]
    q = q_ref[...]
    for h in range(4):
        hs = slice(h * HD, (h + 1) * HD)
        s_loc = _dot_nt(q[:, hs], kw[:, hs]) * SCALE + bias_ref[h, 0]
        s_ctx = _dot_nt(q[:, hs], kc[:, hs]) * SCALE
        o = _softmax_pv([(s_ctx, vc[:, hs]), (s_loc, vw[:, hs])])
        o_ref[:, hs] = o.astype(o_ref.dtype)


def _na_bias_table(rpb):
    o = jnp.arange(NA_KR)[:, None, None, None]
    j = jnp.arange(NA_KR)[None, None, :, None]
    qc = jnp.arange(GW)[None, :, None, None]
    kcol = jnp.arange(GW)[None, None, None, :]
    ri = jnp.broadcast_to(o + j, (NA_KR, GW, NA_KR, GW))
    ci = jnp.broadcast_to(jnp.clip(kcol - qc + NA_KC - 1, 0, 2 * NA_KC - 2), (NA_KR, GW, NA_KR, GW))
    c_start = jnp.clip(qc - NA_KC // 2, 0, GW - NA_KC)
    ok = jnp.broadcast_to((kcol >= c_start) & (kcol < c_start + NA_KC), (NA_KR, GW, NA_KR, GW))
    tbl = jnp.where(ok[None], rpb.astype(F32)[:, ri, ci], NEG)
    return tbl.reshape(4, NA_KR, GW, NA_KR * GW)


def _neighborhood_attention(p_na, bias_tbl):
    rows = S // GW

    def off(r):
        return jnp.clip(r - NA_KR // 2, 0, rows - NA_KR) - r + NA_KR - 1

    return pl.pallas_call(
        _na_kernel,
        grid=(B, rows),
        in_specs=[
            pl.BlockSpec((GW, 256), lambda b, r: (b * rows + r, 0)),
            pl.BlockSpec((S, 256), lambda b, r: (b, 1)),
            pl.BlockSpec((S, 256), lambda b, r: (b, 2)),
            pl.BlockSpec((LC, 256), lambda b, r: (CTX_BLK0 + b, 1)),
            pl.BlockSpec((LC, 256), lambda b, r: (CTX_BLK0 + b, 2)),
            pl.BlockSpec((4, 1, GW, NA_KR * GW), lambda b, r: (0, off(r), 0, 0)),
        ],
        out_specs=pl.BlockSpec((GW, 256), lambda b, r: (b * rows + r, 0)),
        out_shape=jax.ShapeDtypeStruct((NL, 256), BF16),
        compiler_params=_cp("arbitrary", "arbitrary"),
        name="neighborhood_attn",
    )(p_na, p_na, p_na, p_na, p_na, bias_tbl)


def _ctx_attn_kernel(sink_ref, qw_ref, kw_ref, vw_ref, qn_ref, kn_ref, vn_ref, owa_ref, ona_ref):
    q, k, v = qw_ref[...], kw_ref[...], vw_ref[...]
    r1 = lax.broadcasted_iota(jnp.int32, (2 * LC, 1), 0)
    for hk in range(2):
        q2 = jnp.concatenate([q[:, (2 * hk) * HD:(2 * hk + 1) * HD],
                              q[:, (2 * hk + 1) * HD:(2 * hk + 2) * HD]], axis=0)
        hs = slice(hk * HD, (hk + 1) * HD)
        sink = jnp.where(r1 >= LC, sink_ref[2 * hk + 1], sink_ref[2 * hk])
        o = _softmax_pv([(_dot_nt(q2, k[:, hs]) * SCALE, v[:, hs])], sink)
        owa_ref[:, (2 * hk) * HD:(2 * hk + 1) * HD] = o[:LC].astype(owa_ref.dtype)
        owa_ref[:, (2 * hk + 1) * HD:(2 * hk + 2) * HD] = o[LC:].astype(owa_ref.dtype)
    q, k, v = qn_ref[...], kn_ref[...], vn_ref[...]
    for h in range(4):
        hs = slice(h * HD, (h + 1) * HD)
        o = _softmax_pv([(_dot_nt(q[:, hs], k[:, hs]) * SCALE, v[:, hs])])
        ona_ref[:, hs] = o.astype(ona_ref.dtype)


def _ctx_attention(p_wa, p_na, sink):
    blk = lambda c: (lambda b: (CTX_BLK0 + b, c))
    return pl.pallas_call(
        _ctx_attn_kernel,
        grid=(B,),
        in_specs=[
            pl.BlockSpec(memory_space=pltpu.SMEM),
            pl.BlockSpec((LC, 256), blk(0)),
            pl.BlockSpec((LC, 128), blk(2)),
            pl.BlockSpec((LC, 128), blk(3)),
            pl.BlockSpec((LC, 256), blk(0)),
            pl.BlockSpec((LC, 256), blk(1)),
            pl.BlockSpec((LC, 256), blk(2)),
        ],
        out_specs=[pl.BlockSpec((LC, 256), lambda b: (b, 0))] * 2,
        out_shape=[jax.ShapeDtypeStruct((NC, 256), BF16)] * 2,
        compiler_params=_cp("arbitrary"),
        name="ctx_attn",
    )(sink, p_wa, p_wa, p_wa, p_na, p_na, p_na)


def _gdn_prep_seq(p_ref, ab_ref, cw, alog, dtb, qkv_ref, gb_ref, row0, length):
    x = p_ref[:, :768].astype(F32)
    u = (_shift_rows(x, -2) * cw[0:1] + _shift_rows(x, -1) * cw[1:2] + x * cw[2:3]
         + _shift_rows(x, 1) * cw[3:4])
    u = _silu(u)
    rows = slice(row0, row0 + length)
    for j in range(8):
        sl = slice(j * HD, (j + 1) * HD)
        xs = u[:, sl]
        nrm = lax.rsqrt(jnp.sum(xs * xs, axis=-1, keepdims=True) + EPS)
        qkv_ref[0, rows, sl] = xs * (nrm * SCALE if j < 4 else nrm)
    qkv_ref[0, rows, 512:768] = u[:, 512:768]
    ab = ab_ref[...]
    g = -jnp.exp(alog) * _softplus(ab + dtb)
    lane = lax.broadcasted_iota(jnp.int32, ab.shape, 1)
    gb_ref[0, rows, :] = jnp.where(lane < 8, g, _sigmoid(ab))


def _gdn_prep_kernel(pl_ref, abl_ref, pc_ref, abc_ref, cw_ref, alog_ref, dtb_ref, qkv_ref, gb_ref):
    cw, alog, dtb = cw_ref[...], alog_ref[...], dtb_ref[...]
    _gdn_prep_seq(pl_ref, abl_ref, cw, alog, dtb, qkv_ref, gb_ref, 0, S)
    _gdn_prep_seq(pc_ref, abc_ref, cw, alog, dtb, qkv_ref, gb_ref, S, LC)


def _gdn_prep(p_gd, p_ab, conv_w, a_log, dt_bias):
    const = lambda b: (0, 0)
    alog = jnp.pad(a_log.reshape(1, 8), ((0, 0), (0, 120)))
    dtb = jnp.pad(dt_bias.reshape(1, 8), ((0, 0), (0, 120)))
    return pl.pallas_call(
        _gdn_prep_kernel,
        grid=(B,),
        in_specs=[
            pl.BlockSpec((S, 1024), lambda b: (b, 0)),
            pl.BlockSpec((S, 128), lambda b: (b, 0)),
            pl.BlockSpec((LC, 1024), lambda b: (CTX_BLK0 + b, 0)),
            pl.BlockSpec((LC, 128), lambda b: (CTX_BLK0 + b, 0)),
            pl.BlockSpec((4, 768), const),
            pl.BlockSpec((1, 128), const),
            pl.BlockSpec((1, 128), const),
        ],
        out_specs=[pl.BlockSpec((1, S + LC, 768), lambda b: (b, 0, 0)),
                   pl.BlockSpec((1, S + LC, 128), lambda b: (b, 0, 0))],
        out_shape=[jax.ShapeDtypeStruct((B, S + LC, 768), F32), jax.ShapeDtypeStruct((B, S + LC, 128), F32)],
        compiler_params=_cp("arbitrary"),
        name="gdn_prep",
    )(p_gd, p_ab, p_gd, p_ab, conv_w, alog, dtb)


def _gdn_chains(xs, gbs, cums, cum_ts, states, masks):
    chains = [(dirn, h) for dirn in range(2) for h in range(4)]
    pre = []
    for dirn, h in chains:
        x, gb = xs[dirn], gbs[dirn]
        tri, strict = masks[dirn]
        col = dirn * 4 + h
        qh = x[:, h * HD:(h + 1) * HD]
        kh = x[:, 256 + h * HD:256 + (h + 1) * HD]
        vh = x[:, 512 + h * HD:512 + (h + 1) * HD]
        gc = cums[dirn][:, col:col + 1]
        gc_row = cum_ts[dirn][col:col + 1, :]
        beta = gb[:, 8 + col:9 + col]
        decay = jnp.exp(jnp.where(tri, gc - gc_row, NEG))
        kb = kh * beta
        eg = jnp.exp(gc)
        g_last = gc[0:1, :] if dirn else gc[CHUNK - 1:CHUNK, :]
        pre.append(dict(strict=strict, decay=decay, kbf=kb.astype(BF16), khf=kh.astype(BF16), qf=qh.astype(BF16),
                        xw=jnp.concatenate([vh * beta, kb * eg], axis=1), qdec=(qh * eg).astype(BF16),
                        kdec=(kh * jnp.exp(g_last - gc)).astype(BF16), gl=jnp.exp(g_last)))
    n = len(chains)
    gram = [_dot_nt(p["kbf"], p["khf"]) for p in pre]
    attn = [_dot_nt(p["qf"], p["khf"]) for p in pre]
    lmat = [jnp.where(p["strict"], g * p["decay"], 0.0) for p, g in zip(pre, gram)]
    attn = [(a * p["decay"]).astype(BF16) for p, a in zip(pre, attn)]
    ii = lax.broadcasted_iota(jnp.int32, (CHUNK, CHUNK), 0)
    jj = lax.broadcasted_iota(jnp.int32, (CHUNK, CHUNK), 1)

    def merged_off_blocks(log_s):
        return ((ii >> (log_s + 1)) == (jj >> (log_s + 1))) & ((ii >> log_s) != (jj >> log_s))

    eye = (ii == jj).astype(F32)
    dinv = [eye - jnp.where(merged_off_blocks(0), lm, 0.0) for lm in lmat]
    for log_s in range(1, 6):
        m = merged_off_blocks(log_s)
        dinv_b = [d.astype(BF16) for d in dinv]
        ld = [_dot(jnp.where(m, lmat[c], 0.0).astype(BF16), dinv_b[c]) for c in range(n)]
        upd = [_dot(dinv_b[c], ld[c].astype(BF16)) for c in range(n)]
        dinv = [dinv[c] - upd[c] for c in range(n)]
    xw = [_dot(dinv[c].astype(BF16), pre[c]["xw"].astype(BF16)) for c in range(n)]
    sts = [states[d][h] for d, h in chains]
    stb = [s.astype(BF16) for s in sts]
    ws = [_dot(xw[c][:, HD:].astype(BF16), stb[c]) for c in range(n)]
    qs = [_dot(pre[c]["qdec"], stb[c]) for c in range(n)]
    u_new = [(xw[c][:, :HD] - ws[c]).astype(BF16) for c in range(n)]
    au = [_dot(attn[c], u_new[c]) for c in range(n)]
    ku = [_dot_tn(pre[c]["kdec"], u_new[c]) for c in range(n)]
    outs = [qs[c] + au[c] for c in range(n)]
    new_states = [sts[c] * pre[c]["gl"] + ku[c] for c in range(n)]
    return outs, new_states


def _gdn_kernel(x0_ref, gb0_ref, x1_ref, gb1_ref, o0_ref, o1_ref, state_ref):
    @pl.when(pl.program_id(0) == 0)
    def _():
        state_ref[...] = jnp.zeros_like(state_ref)

    ii = lax.broadcasted_iota(jnp.int32, (CHUNK, CHUNK), 0)
    jj = lax.broadcasted_iota(jnp.int32, (CHUNK, CHUNK), 1)
    masks = ((jj <= ii, jj < ii), (jj >= ii, jj > ii))
    ones_tri = ((jj <= ii).astype(F32), (jj >= ii).astype(F32))

    def body(b, carry):
        states = [[state_ref[b, dirn, h] for h in range(4)] for dirn in range(2)]
        xs = (x0_ref[b], x1_ref[b])
        gbs = (gb0_ref[b], gb1_ref[b])
        cums = [_dot(ones_tri[d], gbs[d], HI) for d in range(2)]
        cum_ts = [cm.T for cm in cums]
        outs, new_states = _gdn_chains(xs, gbs, cums, cum_ts, states, masks)
        for c in range(8):
            state_ref[b, c // 4, c % 4] = new_states[c]
        o0_ref[b] = jnp.concatenate(outs[:4], axis=1)
        o1_ref[b] = jnp.concatenate(outs[4:], axis=1)
        return carry

    lax.fori_loop(0, B, body, 0)


def _gdn_scan(qkv, gb):
    nc = (S + LC) // CHUNK
    fwd = lambda c: (0, (c + S // CHUNK) % nc, 0)
    bwd = lambda c: (0, nc - 1 - c, 0)
    return pl.pallas_call(
        _gdn_kernel,
        grid=(nc,),
        in_specs=[
            pl.BlockSpec((B, CHUNK, 768), fwd),
            pl.BlockSpec((B, CHUNK, 128), fwd),
            pl.BlockSpec((B, CHUNK, 768), bwd),
            pl.BlockSpec((B, CHUNK, 128), bwd),
        ],
        out_specs=[pl.BlockSpec((B, CHUNK, 256), fwd), pl.BlockSpec((B, CHUNK, 256), bwd)],
        out_shape=[jax.ShapeDtypeStruct((B, S + LC, 256), F32)] * 2,
        scratch_shapes=[pltpu.VMEM((B, 2, 4, HD, HD), F32)],
        compiler_params=_cp("arbitrary"),
        name="gdn_scan",
    )(qkv, gb, qkv, gb)


def _gdn_finish_kernel(o0_ref, o1_ref, gate_ref, ng_ref, y_ref):
    o = o0_ref[0] + o1_ref[0]
    gi = lax.broadcasted_iota(jnp.int32, (256, 256), 0) // HD
    gj = lax.broadcasted_iota(jnp.int32, (256, 256), 1) // HD
    group_mean = jnp.where(gi == gj, 1.0 / HD, 0.0)
    ms = _dot(o * o, group_mean, HI)
    y = o * lax.rsqrt(ms + EPS) * ng_ref[...]
    y_ref[...] = (y * _silu(gate_ref[...].astype(F32))).astype(y_ref.dtype)


def _gdn_finish(o0, o1, p_gd, norm_g, need_ctx):
    lat_tiles = S // TM
    row_blk = lambda b, j: jnp.where(j < lat_tiles, b * lat_tiles + j, CTX_BLK0 + b)
    return pl.pallas_call(
        _gdn_finish_kernel,
        grid=(B, lat_tiles + (1 if need_ctx else 0)),
        in_specs=[
            pl.BlockSpec((1, TM, 256), lambda b, j: (b, j, 0)),
            pl.BlockSpec((1, TM, 256), lambda b, j: (b, j, 0)),
            pl.BlockSpec((TM, 256), lambda b, j: (row_blk(b, j), 3)),
            pl.BlockSpec((1, 256), lambda b, j: (0, 0)),
        ],
        out_specs=pl.BlockSpec((TM, 256), lambda b, j: (row_blk(b, j), 0)),
        out_shape=jax.ShapeDtypeStruct((ROWS if need_ctx else NL, 256), BF16),
        compiler_params=_cp("arbitrary", "arbitrary"),
        name="gdn_finish",
    )(o0, o1, p_gd, jnp.tile(norm_g, 4)[None, :])


def _pack_bf16_pairs(x):
    n = x.shape[1] // 2
    lo = pltpu.bitcast(x[:, :n].astype(BF16).astype(F32), jnp.uint32)
    hi = pltpu.bitcast(x[:, n:].astype(BF16).astype(F32), jnp.uint32)
    return hi | (lo >> 16)


def _unpack_bf16_pairs(w):
    lo = pltpu.bitcast(w << 16, F32)
    hi = pltpu.bitcast(w & jnp.uint32(0xFFFF0000), F32)
    return jnp.concatenate([lo, hi], axis=1)


def _outproj_kernel(x_ref, mod_ref, g2_ref, yh_ref, yw_ref, yn_ref, yg_ref, wo_ref, wr_ref, eb_ref,
                    xo_ref, h2_ref, h2p_ref, idx_ref, tw_ref, rank_ref, cnt_ref, carry_ref):
    @pl.when(pl.program_id(0) == 0)
    def _():
        carry_ref[...] = jnp.zeros_like(carry_ref)

    m = mod_ref[0]
    acc = (_dot(yh_ref[...], wo_ref[0:256, :]) + _dot(yw_ref[...], wo_ref[256:512, :])
           + _dot(yn_ref[...], wo_ref[512:768, :]) + _dot(yg_ref[...], wo_ref[768:1024, :]))
    x = x_ref[...] + m[2:3] * acc
    xo_ref[...] = x
    h2 = _modulated_norm(x, g2_ref[...], m[3:4], m[4:5])
    h2_ref[...] = h2.astype(h2_ref.dtype)
    h2p_ref[...] = _pack_bf16_pairs(h2)
    scores = _sigmoid(_dot(h2, wr_ref[...], HI))
    lane = lax.broadcasted_iota(jnp.int32, scores.shape, 1)
    sel = jnp.where(lane < N_EXP, scores + eb_ref[...], -jnp.inf)
    lane_f = lane.astype(F32)
    idx_out = jnp.zeros(scores.shape, F32)
    s_out = jnp.zeros(scores.shape, F32)
    hits = []
    for kk in range(TOP_K):
        mx = jnp.max(sel, axis=-1, keepdims=True)
        idx = jnp.min(jnp.where(sel == mx, lane_f, 128.0), axis=-1, keepdims=True)
        hit = lane_f == idx
        sk = jnp.sum(jnp.where(hit, scores, 0.0), axis=-1, keepdims=True)
        sel = jnp.where(hit, -jnp.inf, sel)
        idx_out = jnp.where(lane == kk, idx, idx_out)
        s_out = jnp.where(lane == kk, sk, s_out)
        hits.append(hit)
    tot = jnp.sum(s_out, axis=-1, keepdims=True)
    idx_ref[...] = idx_out.astype(jnp.int32)
    tw_ref[...] = s_out / tot * ROUTED_SCALE
    cnt = jnp.zeros(scores.shape, F32)
    for hit in hits:
        cnt = cnt + jnp.where(hit, 1.0, 0.0)
    ri = lax.broadcasted_iota(jnp.int32, (TM, TM), 0)
    rj = lax.broadcasted_iota(jnp.int32, (TM, TM), 1)
    before = _dot(jnp.where(rj < ri, 1.0, 0.0).astype(BF16), cnt.astype(BF16)) + carry_ref[...]
    rank_out = jnp.zeros(scores.shape, F32)
    for kk, hit in enumerate(hits):
        rk = jnp.sum(jnp.where(hit, before, 0.0), axis=-1, keepdims=True)
        rank_out = jnp.where(lane == kk, rk, rank_out)
    rank_ref[...] = rank_out.astype(jnp.int32)
    carry_ref[...] = carry_ref[...] + jnp.sum(cnt, axis=0, keepdims=True)
    cnt_ref[...] = carry_ref[...].astype(jnp.int32)


def _outproj(xa, mod, l, g2, ys, wo, wr, eb, n_tiles):
    row = lambda i: (i, 0)
    const = lambda i: (0, 0)
    rows = n_tiles * TM
    return pl.pallas_call(
        _outproj_kernel,
        grid=(n_tiles,),
        in_specs=[
            pl.BlockSpec((TM, D), row),
            pl.BlockSpec((1, 6, D), lambda i: (l * 16 + (i * TM) // S, 0, 0)),
            pl.BlockSpec((1, D), const),
            pl.BlockSpec((TM, 256), row),
            pl.BlockSpec((TM, 256), row),
            pl.BlockSpec((TM, 256), row),
            pl.BlockSpec((TM, 256), row),
            pl.BlockSpec((D, D), const),
            pl.BlockSpec((D, 128), const),
            pl.BlockSpec((1, 128), const),
        ],
        out_specs=[
            pl.BlockSpec((TM, D), row),
            pl.BlockSpec((TM, D), row),
            pl.BlockSpec((TM, D // 2), row),
            pl.BlockSpec((TM, 128), row),
            pl.BlockSpec((TM, 128), row),
            pl.BlockSpec((TM, 128), row),
            pl.BlockSpec((1, 128), const),
        ],
        out_shape=[
            jax.ShapeDtypeStruct((rows, D), F32),
            jax.ShapeDtypeStruct((rows, D), BF16),
            jax.ShapeDtypeStruct((rows, D // 2), jnp.uint32),
            jax.ShapeDtypeStruct((rows, 128), jnp.int32),
            jax.ShapeDtypeStruct((rows, 128), F32),
            jax.ShapeDtypeStruct((rows, 128), jnp.int32),
            jax.ShapeDtypeStruct((1, 128), jnp.int32),
        ],
        scratch_shapes=[pltpu.VMEM((1, 128), F32)],
        compiler_params=_cp("arbitrary"),
        name="outproj_router",
    )(xa, mod, g2, *ys, wo, wr, eb)


ASG_TILE = TM * TOP_K
ASG_ROWS = ASG_TILE // 128


def _segment_tables(counts, n_blocks):
    counts = counts[0, :N_EXP]
    padded = (counts + MOE_BLK - 1) // MOE_BLK * MOE_BLK
    pad_end = jnp.cumsum(padded)
    starts = jnp.arange(n_blocks, dtype=jnp.int32) * MOE_BLK
    blk_e = jnp.minimum(jnp.sum((pad_end[None, :] <= starts[:, None]).astype(jnp.int32), axis=1), N_EXP - 1)
    return (pad_end - padded).astype(jnp.int32), blk_e.astype(jnp.int32)


def _row_copy_wait(shape_ref, dst_ref, sem):
    pltpu.make_async_copy(shape_ref, dst_ref, sem).wait()


def _dispatch_kernel(seg_ref, idx_ref, rank_ref, h2p_ref, xb_in_ref, xb_ref, sem):
    del xb_in_ref

    def issue(a, carry):
        r, c = a >> 7, a & 127
        dst = seg_ref[idx_ref[r, c]] + rank_ref[r, c]
        pltpu.make_async_copy(h2p_ref.at[pl.ds(a >> 2, 1)], xb_ref.at[pl.ds(dst, 1)], sem).start()
        return carry

    lax.fori_loop(0, ASG_TILE, issue, 0, unroll=8)
    for _ in range(TOP_K):
        _row_copy_wait(h2p_ref, xb_ref.at[pl.ds(0, TM)], sem)


def _dispatch_rows(seg_start, idx_c, rank_c, h2p, n_rows):
    n_tiles = h2p.shape[0] // TM
    grid_spec = pltpu.PrefetchScalarGridSpec(
        num_scalar_prefetch=1,
        grid=(n_tiles,),
        in_specs=[
            pl.BlockSpec((ASG_ROWS, 128), lambda i, seg: (i, 0), memory_space=pltpu.SMEM),
            pl.BlockSpec((ASG_ROWS, 128), lambda i, seg: (i, 0), memory_space=pltpu.SMEM),
            pl.BlockSpec((TM, D // 2), lambda i, seg: (i, 0)),
            pl.BlockSpec(memory_space=pl.ANY),
        ],
        out_specs=pl.BlockSpec(memory_space=pl.ANY),
        scratch_shapes=[pltpu.SemaphoreType.DMA(())],
    )
    return pl.pallas_call(
        _dispatch_kernel,
        grid_spec=grid_spec,
        out_shape=jax.ShapeDtypeStruct((n_rows, D // 2), jnp.uint32),
        input_output_aliases={4: 0},
        compiler_params=_cp("arbitrary"),
        name="moe_dispatch",
    )(seg_start, idx_c, rank_c, h2p, jnp.zeros((n_rows, D // 2), jnp.uint32))


def _moe_kernel(be_ref, x_ref, wg_ref, wu_ref, wd_ref, o_ref):
    del be_ref
    x = _unpack_bf16_pairs(x_ref[...]).astype(BF16)
    g = _dot(x, wg_ref[0].astype(BF16))
    u = _dot(x, wu_ref[0].astype(BF16))
    hid = (_silu(g) * u).astype(BF16)
    o_ref[...] = _pack_bf16_pairs(_dot(hid, wd_ref[0].astype(BF16)))


def _moe_experts(blk_e, xb, wg, wu, wd):
    n_rows = xb.shape[0]
    grid_spec = pltpu.PrefetchScalarGridSpec(
        num_scalar_prefetch=1,
        grid=(n_rows // MOE_BLK,),
        in_specs=[
            pl.BlockSpec((MOE_BLK, D // 2), lambda i, be: (i, 0)),
            pl.BlockSpec((1, D, D_EXP), lambda i, be: (be[i], 0, 0)),
            pl.BlockSpec((1, D, D_EXP), lambda i, be: (be[i], 0, 0)),
            pl.BlockSpec((1, D_EXP, D), lambda i, be: (be[i], 0, 0)),
        ],
        out_specs=pl.BlockSpec((MOE_BLK, D // 2), lambda i, be: (i, 0)),
    )
    return pl.pallas_call(
        _moe_kernel,
        grid_spec=grid_spec,
        out_shape=jax.ShapeDtypeStruct((n_rows, D // 2), jnp.uint32),
        compiler_params=_cp("arbitrary"),
        name="moe_experts",
    )(blk_e, xb, wg, wu, wd)


def _ffn_out_kernel(seg_ref, idx_ref, rank_ref, x_ref, mod_ref, h2_ref, tw_ref, wg_ref, wu_ref, wd_ref, nf_ref,
                    yb_ref, o_ref, gat_ref, sem, *, final):
    def issue(a, carry):
        r, c = a >> 7, a & 127
        src = seg_ref[idx_ref[r, c]] + rank_ref[r, c]
        pltpu.make_async_copy(yb_ref.at[pl.ds(src, 1)], gat_ref.at[pl.ds((a & 3) * TM + (a >> 2), 1)], sem).start()
        return carry

    lax.fori_loop(0, ASG_TILE, issue, 0, unroll=8)
    m = mod_ref[0]
    h2 = h2_ref[...]
    hid = (_silu(_dot(h2, wg_ref[...])) * _dot(h2, wu_ref[...])).astype(BF16)
    shared = _dot(hid, wd_ref[...])
    for _ in range(TOP_K):
        _row_copy_wait(yb_ref.at[pl.ds(0, TM)], gat_ref.at[pl.ds(0, TM)], sem)
    tw = tw_ref[...]
    routed = _unpack_bf16_pairs(gat_ref[0:TM, :]) * tw[:, 0:1]
    for k in range(1, TOP_K):
        routed = routed + _unpack_bf16_pairs(gat_ref[k * TM:(k + 1) * TM, :]) * tw[:, k:k + 1]
    x = x_ref[...] + m[5:6] * (routed + shared)
    if final:
        ms = jnp.mean(x * x, axis=-1, keepdims=True)
        x = x * lax.rsqrt(ms + EPS) * nf_ref[...]
    o_ref[...] = x


def _ffn_out(seg_start, idx_c, rank_c, xn, mod, l, h2, top_w, wg, wu, wd, norm_f, yb, n_tiles, final):
    row = lambda i, seg: (i, 0)
    const = lambda i, seg: (0, 0)
    grid_spec = pltpu.PrefetchScalarGridSpec(
        num_scalar_prefetch=1,
        grid=(n_tiles,),
        in_specs=[
            pl.BlockSpec((ASG_ROWS, 128), row, memory_space=pltpu.SMEM),
            pl.BlockSpec((ASG_ROWS, 128), row, memory_space=pltpu.SMEM),
            pl.BlockSpec((TM, D), row),
            pl.BlockSpec((1, 6, D), lambda i, seg: (l * 16 + (i * TM) // S, 0, 0)),
            pl.BlockSpec((TM, D), row),
            pl.BlockSpec((TM, 128), row),
            pl.BlockSpec((D, D_EXP), const),
            pl.BlockSpec((D, D_EXP), const),
            pl.BlockSpec((D_EXP, D), const),
            pl.BlockSpec((1, D), const),
            pl.BlockSpec(memory_space=pl.ANY),
        ],
        out_specs=pl.BlockSpec((TM, D), row),
        scratch_shapes=[pltpu.VMEM((ASG_TILE, D // 2), jnp.uint32), pltpu.SemaphoreType.DMA(())],
    )
    return pl.pallas_call(
        functools.partial(_ffn_out_kernel, final=final),
        grid_spec=grid_spec,
        out_shape=jax.ShapeDtypeStruct((n_tiles * TM, D), F32),
        compiler_params=_cp("arbitrary"),
        name="shared_ffn_residual",
    )(seg_start, idx_c, rank_c, xn, mod, h2, top_w, wg, wu, wd, norm_f, yb)


def kernel(x, c, ctx, c_ctx, w_ada, b_ada, norm1, norm2, norm_f, w_in, w_out, hy_conv, hy_w1, hy_b1, hy_w2, hy_b2, hy_w3, hy_freq, hy_bias, wa_sink, na_rpb, gdn_conv, gdn_a_log, gdn_dt_bias, gdn_norm, moe_router, moe_bias, moe_gate, moe_up, moe_down, sh_gate, sh_up, sh_down):
    depth = w_ada.shape[0]
    xa = jnp.concatenate([x.reshape(NL, D), ctx.reshape(NC, D)], axis=0)
    cvec = jnp.concatenate([c, c_ctx[None, :], jnp.zeros((16 - B - 1, D), F32)], axis=0)
    mod = _ada(cvec, w_ada, b_ada).reshape(depth * 16, 6, D)
    rope = _rope_tables()
    dft_lat = _dft_tables(S)
    dft_ctx = _dft_tables(LC)
    o1, o2, o3 = 768, 768 + 512, 768 + 512 + 768

    for l in range(depth):
        need_ctx = l < depth - 1
        n_tiles = NT_ALL if need_ctx else NT_LAT
        wl = w_in[l].astype(BF16)
        ws = (wl[:, :o1], wl[:, o1:o2], wl[:, o2:o3], wl[:, o3:o3 + 1024],
              jnp.pad(wl[:, o3 + 1024:], ((0, 0), (0, 128 - 16))))
        p_hy, p_wa, p_na, p_gd, p_ab = _inproj(xa, mod, l, norm1[l][None, :], ws, rope)

        filt = (hy_w1[l], hy_b1[l], hy_w2[l], hy_b2[l], hy_w3[l], hy_freq[l])
        kre, kim = _hyena_filter_spectrum(S, *filt, dft_lat)
        y_hy = _hyena(p_hy, S, 0, hy_conv[l], hy_bias[l], dft_lat, kre, kim)
        y_wa = _window_attention(p_wa, wa_sink[l])
        y_na = _neighborhood_attention(p_na, _na_bias_table(na_rpb[l]))
        if need_ctx:
            kre_c, kim_c = _hyena_filter_spectrum(LC, *filt, dft_ctx)
            yc_hy = _hyena(p_hy, LC, CTX_BLK0, hy_conv[l], hy_bias[l], dft_ctx, kre_c, kim_c)
            yc_wa, yc_na = _ctx_attention(p_wa, p_na, wa_sink[l])
            y_hy = jnp.concatenate([y_hy, yc_hy], axis=0)
            y_wa = jnp.concatenate([y_wa, yc_wa], axis=0)
            y_na = jnp.concatenate([y_na, yc_na], axis=0)
        qkv, gb = _gdn_prep(p_gd, p_ab, gdn_conv[l], gdn_a_log[l], gdn_dt_bias[l])
        y_gd = _gdn_finish(*_gdn_scan(qkv, gb), p_gd, gdn_norm[l], need_ctx)

        wr = jnp.pad(moe_router[l], ((0, 0), (0, 128 - N_EXP)))
        eb = jnp.pad(moe_bias[l], (0, 128 - N_EXP))[None, :]
        xn, h2, h2p, top_idx, top_w, rank, counts = _outproj(
            xa, mod, l, norm2[l][None, :], (y_hy, y_wa, y_na, y_gd), w_out[l].astype(BF16), wr, eb, n_tiles)
        n_asg = n_tiles * ASG_TILE
        n_blocks = n_asg // MOE_BLK + N_EXP
        seg_start, blk_e = _segment_tables(counts, n_blocks)
        idx_c = top_idx[:, :TOP_K].reshape(n_asg // 128, 128)
        rank_c = rank[:, :TOP_K].reshape(n_asg // 128, 128)
        xb = _dispatch_rows(seg_start, idx_c, rank_c, h2p, n_blocks * MOE_BLK)
        yb = _moe_experts(blk_e, xb, moe_gate[l], moe_up[l], moe_down[l])
        xa_new = _ffn_out(seg_start, idx_c, rank_c, xn, mod, l, h2, top_w, sh_gate[l].astype(BF16),
                          sh_up[l].astype(BF16), sh_down[l].astype(BF16), norm_f[None, :], yb, n_tiles,
                          final=not need_ctx)
        if need_ctx:
            xa = xa_new
        else:
            return xa_new.reshape(B, S, D)
```

```python
import functools
import math

import jax
import jax.numpy as jnp
from jax import lax
from jax.experimental import pallas as pl
from jax.experimental.pallas import tpu as pltpu

F32 = jnp.float32
BF16 = jnp.bfloat16
HI = lax.Precision.HIGHEST

D = 1024
B = 8
S = 2048
LC = 256
GW = 64
HD = 64
NL = B * S
NC = B * LC
ROWS = NL + NC
TM = 256
NT_LAT = NL // TM
NT_ALL = ROWS // TM
CTX_BLK0 = NL // LC

HY_CH = 256
HY_BANDS = 16
HY_DECAY_MIN = -math.log(1e-2) / 1.5
HY_DECAY_MAX = -math.log(1e-2) / 0.3
WINDOW = 128
NA_KR = 8
NA_KC = 16
CHUNK = 64
N_EXP = 32
TOP_K = 4
D_EXP = 256
ROUTED_SCALE = 2.5
MOE_BLK = 256
GDN_SEQS = 4
EPS = 1e-6
NEG = -1e30
SCALE = HD ** -0.5
VMEM_LIMIT = 56 * 1024 * 1024


def _cp(*sem):
    return pltpu.CompilerParams(dimension_semantics=tuple(sem), vmem_limit_bytes=VMEM_LIMIT)


def _dot(a, b, precision=None):
    return jnp.dot(a, b, preferred_element_type=F32, precision=precision)


def _dot_nt(a, b, precision=None):
    return lax.dot_general(a, b, (((1,), (1,)), ((), ())), preferred_element_type=F32, precision=precision)


def _dot_tn(a, b, precision=None):
    return lax.dot_general(a, b, (((0,), (0,)), ((), ())), preferred_element_type=F32, precision=precision)


def _sigmoid(x):
    return 1.0 / (1.0 + jnp.exp(-x))


def _silu(x):
    return x * _sigmoid(x)


def _softplus(x):
    return jnp.maximum(x, 0.0) + jnp.log(1.0 + jnp.exp(-jnp.abs(x)))


def _shift_rows(x, d):
    n = x.shape[0]
    if d == 0:
        return x
    y = pltpu.roll(x, (-d) % n, axis=0)
    t = lax.broadcasted_iota(jnp.int32, x.shape, 0)
    ok = (t + d >= 0) & (t + d < n)
    return jnp.where(ok, y, 0.0)


def _ada_kernel(c_ref, w_ref, b_ref, o_ref):
    s = _silu(c_ref[...])
    o_ref[0] = _dot(s.astype(BF16), w_ref[0].astype(BF16)) + b_ref[0]


def _ada(cvec, w_ada, b_ada):
    nl = w_ada.shape[0]
    tn = 1536
    return pl.pallas_call(
        _ada_kernel,
        grid=(nl, 6 * D // tn),
        in_specs=[
            pl.BlockSpec((16, D), lambda l, j: (0, 0)),
            pl.BlockSpec((1, D, tn), lambda l, j: (l, 0, j)),
            pl.BlockSpec((1, 1, tn), lambda l, j: (l, 0, j)),
        ],
        out_specs=pl.BlockSpec((1, 16, tn), lambda l, j: (l, 0, j)),
        out_shape=jax.ShapeDtypeStruct((nl, 16, 6 * D), F32),
        compiler_params=_cp("arbitrary", "arbitrary"),
        name="adaln",
    )(cvec, w_ada, b_ada.reshape(nl, 1, 6 * D))


def _modulated_norm(x, g, shift, scale):
    ms = jnp.mean(x * x, axis=-1, keepdims=True)
    y = x * lax.rsqrt(ms + EPS) * g
    return y * (1.0 + scale) + shift


def _inproj_kernel(x_ref, mod_ref, g_ref, why_ref, wwa_ref, wna_ref, wgd_ref, wab_ref,
                   cos_ref, sa_ref, sb_ref, ohy, owa, ona, ogd, oab):
    m = mod_ref[0]
    h = _modulated_norm(x_ref[...], g_ref[...], m[0:1], m[1:2]).astype(BF16)
    ohy[...] = _dot(h, why_ref[...]).astype(ohy.dtype)
    ona[...] = _dot(h, wna_ref[...]).astype(ona.dtype)
    ogd[...] = _dot(h, wgd_ref[...]).astype(ogd.dtype)
    oab[...] = _dot(h, wab_ref[...])
    a = _dot(h, wwa_ref[...])
    for c in range(4):
        sl = slice(c * 128, (c + 1) * 128)
        ac = a[:, sl]
        r = (ac * cos_ref[:, sl] + pltpu.roll(ac, 112, axis=1) * sa_ref[:, sl]
             + pltpu.roll(ac, 16, axis=1) * sb_ref[:, sl])
        owa[:, sl] = r.astype(owa.dtype)


def _inproj(xa, mod, l, g, ws, tabs):
    why, wwa, wna, wgd, wab = ws
    cos_t, sa_t, sb_t = tabs
    row = lambda i: (i, 0)
    const = lambda i: (0, 0)
    tab = lambda i: (jnp.where(i < NT_LAT, i % (S // TM), S // TM), 0)
    return pl.pallas_call(
        _inproj_kernel,
        grid=(NT_ALL,),
        in_specs=[
            pl.BlockSpec((TM, D), row),
            pl.BlockSpec((1, 6, D), lambda i: (l * 16 + (i * TM) // S, 0, 0)),
            pl.BlockSpec((1, D), const),
            pl.BlockSpec((D, 768), const),
            pl.BlockSpec((D, 512), const),
            pl.BlockSpec((D, 768), const),
            pl.BlockSpec((D, 1024), const),
            pl.BlockSpec((D, 128), const),
            pl.BlockSpec((TM, 512), tab),
            pl.BlockSpec((TM, 512), tab),
            pl.BlockSpec((TM, 512), tab),
        ],
        out_specs=[
            pl.BlockSpec((TM, 768), row),
            pl.BlockSpec((TM, 512), row),
            pl.BlockSpec((TM, 768), row),
            pl.BlockSpec((TM, 1024), row),
            pl.BlockSpec((TM, 128), row),
        ],
        out_shape=[
            jax.ShapeDtypeStruct((ROWS, 768), BF16),
            jax.ShapeDtypeStruct((ROWS, 512), BF16),
            jax.ShapeDtypeStruct((ROWS, 768), BF16),
            jax.ShapeDtypeStruct((ROWS, 1024), BF16),
            jax.ShapeDtypeStruct((ROWS, 128), F32),
        ],
        compiler_params=_cp("arbitrary"),
        name="inproj",
    )(xa, mod, g, why, wwa, wna, wgd, wab, cos_t, sa_t, sb_t)


def _rope_tables():
    quarter = HD // 4
    pos = jnp.arange(S)
    inv = 10000.0 ** (-jnp.arange(quarter, dtype=F32) / quarter)
    ang_r = (pos // GW).astype(F32)[:, None] * inv[None, :]
    ang_c = (pos % GW).astype(F32)[:, None] * inv[None, :]
    z = jnp.zeros_like(ang_r)
    cos_h = jnp.concatenate([jnp.cos(ang_r)] * 2 + [jnp.cos(ang_c)] * 2, axis=1)
    sa_h = jnp.concatenate([-jnp.sin(ang_r), z, -jnp.sin(ang_c), z], axis=1)
    sb_h = jnp.concatenate([z, jnp.sin(ang_r), z, jnp.sin(ang_c)], axis=1)
    ones = jnp.ones((S, 128), F32)
    zeros = jnp.zeros((S, 128), F32)
    cos_t = jnp.concatenate([jnp.tile(cos_h, (1, 6)), ones], axis=1)
    sa_t = jnp.concatenate([jnp.tile(sa_h, (1, 6)), zeros], axis=1)
    sb_t = jnp.concatenate([jnp.tile(sb_h, (1, 6)), zeros], axis=1)
    ident = jnp.ones((TM, 512), F32)
    none = jnp.zeros((TM, 512), F32)
    return (jnp.concatenate([cos_t, ident], axis=0), jnp.concatenate([sa_t, none], axis=0),
            jnp.concatenate([sb_t, none], axis=0))


def _dft_tables(length):
    n = 2 * length
    f = jnp.arange(length, dtype=jnp.int32)
    m = ((2 * f[:, None] + 1) * (2 * f[None, :] + 1)) % (4 * n)
    th = m.astype(F32) * (2.0 * math.pi / (4 * n))
    phi = (2 * f + 1).astype(F32) * (math.pi / (2 * n))
    return (jnp.cos(th).astype(BF16), jnp.sin(th).astype(BF16),
            jnp.cos(phi)[:, None], jnp.sin(phi)[:, None])


def _hyena_features(length):
    t = jnp.arange(length, dtype=F32)
    t_norm = t / max(length - 1, 1)
    bands = jnp.linspace(1e-4, HY_BANDS - 1, HY_BANDS, dtype=F32)
    ang = (2.0 * math.pi / length) * t[:, None] * bands[None, :]
    z = jnp.concatenate([t_norm[:, None], jnp.cos(ang), -jnp.sin(ang)], axis=-1)
    z = jnp.pad(z, ((0, 0), (0, 128 - z.shape[1])))
    decay = jnp.tile(jnp.linspace(HY_DECAY_MIN, HY_DECAY_MAX, HY_CH, dtype=F32), 2)
    return z, jnp.exp(-t_norm[:, None] * decay[None, :])


def _hyfilt_kernel(z_ref, w1_ref, b1_ref, w2_ref, b2_ref, w3_ref, fr_ref, dec_ref,
                   c_ref, s_ref, cp_ref, sp_ref, kre_ref, kim_ref, pq_ref, *, length):
    @pl.when(pl.program_id(0) == 0)
    def _():
        fr = fr_ref[...]
        h = jnp.sin(fr * (_dot(z_ref[...], w1_ref[...], HI) + b1_ref[...]))
        h = jnp.sin(fr * (_dot(h, w2_ref[...], HI) + b2_ref[...]))
        h = _dot(h, w3_ref[...], HI) * dec_ref[...]
        hf = h[:, :HY_CH]
        t = lax.broadcasted_iota(jnp.int32, (length, HY_CH), 0)
        hb = jnp.where(t == 0, 0.0, h[:, HY_CH:])
        pq_ref[:, :HY_CH] = (hf + hb).astype(BF16)
        pq_ref[:, HY_CH:] = (hb - hf).astype(BF16)

    pq = pq_ref[...]
    cpq = _dot(c_ref[...], pq)
    spq = _dot(s_ref[...], pq)
    cphi, sphi = cp_ref[...], sp_ref[...]
    norm = 1.0 / length
    kre_ref[...] = (cphi * cpq[:, :HY_CH] + sphi * spq[:, :HY_CH]) * norm
    kim_ref[...] = (cphi * spq[:, HY_CH:] - sphi * cpq[:, HY_CH:]) * norm


def _hyena_filter_spectrum(length, w1, b1, w2, b2, w3, freq, dft):
    z, dec = _hyena_features(length)
    c_m, s_m, cphi, sphi = dft
    w1p = jnp.pad(w1, ((0, 128 - w1.shape[0]), (0, 0)))
    tf = min(512, length)
    const = lambda j: (0, 0)
    blk = lambda j: (j, 0)
    return pl.pallas_call(
        functools.partial(_hyfilt_kernel, length=length),
        grid=(length // tf,),
        in_specs=[
            pl.BlockSpec((length, 128), const),
            pl.BlockSpec((128, 64), const),
            pl.BlockSpec((1, 64), const),
            pl.BlockSpec((64, 64), const),
            pl.BlockSpec((1, 64), const),
            pl.BlockSpec((64, 2 * HY_CH), const),
            pl.BlockSpec((1, 64), const),
            pl.BlockSpec((length, 2 * HY_CH), const),
            pl.BlockSpec((tf, length), blk),
            pl.BlockSpec((tf, length), blk),
            pl.BlockSpec((tf, 1), blk),
            pl.BlockSpec((tf, 1), blk),
        ],
        out_specs=[pl.BlockSpec((tf, HY_CH), blk)] * 2,
        out_shape=[jax.ShapeDtypeStruct((length, HY_CH), F32)] * 2,
        scratch_shapes=[pltpu.VMEM((length, 2 * HY_CH), BF16)],
        compiler_params=_cp("arbitrary"),
        name=f"hyena_filter_{length}",
    )(z, w1p, b1[None, :], w2, b2[None, :], w3, freq[None, :], dec, c_m, s_m, cphi, sphi)


def _short_conv3(x_ref, cw):
    x = x_ref[...].astype(F32)
    return _shift_rows(x, -1) * cw[0:1] + x * cw[1:2] + _shift_rows(x, 1) * cw[2:3]


def _hyena_kernel(p0_ref, p1_ref, p2_ref, cw_ref, bias_ref, c_ref, s_ref, kre_ref, kim_ref, o_ref):
    cw = cw_ref[...]
    x0 = _short_conv3(p0_ref, cw[:, :HY_CH])
    k = _short_conv3(p1_ref, cw[:, HY_CH:2 * HY_CH]) * _short_conv3(p2_ref, cw[:, 2 * HY_CH:])
    kb = k.astype(BF16)
    a = _dot(c_ref[...], kb)
    b = _dot(s_ref[...], kb)
    kre, kim = kre_ref[...], kim_ref[...]
    yre = (a * kre + b * kim).astype(BF16)
    yim = (b * kre - a * kim).astype(BF16)
    y = _dot(c_ref[...], yre) + _dot(s_ref[...], yim)
    o_ref[...] = (x0 * (y + k * bias_ref[...])).astype(o_ref.dtype)


def _hyena(p_hy, length, blk0, conv_w, bias, dft, kre, kim):
    c_m, s_m = dft[0], dft[1]
    const = lambda b: (0, 0)
    once = pl.Buffered(1)
    in_specs = [
        pl.BlockSpec((length, HY_CH), lambda b: (blk0 + b, 0)),
        pl.BlockSpec((length, HY_CH), lambda b: (blk0 + b, 1)),
        pl.BlockSpec((length, HY_CH), lambda b: (blk0 + b, 2)),
        pl.BlockSpec((3, 768), const),
        pl.BlockSpec((1, HY_CH), const),
        pl.BlockSpec((length, length), const, pipeline_mode=once),
        pl.BlockSpec((length, length), const, pipeline_mode=once),
        pl.BlockSpec((length, HY_CH), const, pipeline_mode=once),
        pl.BlockSpec((length, HY_CH), const, pipeline_mode=once),
    ]
    return pl.pallas_call(
        _hyena_kernel,
        grid=(B,),
        in_specs=in_specs,
        out_specs=pl.BlockSpec((length, HY_CH), lambda b: (b, 0)),
        out_shape=jax.ShapeDtypeStruct((B * length, HY_CH), BF16),
        compiler_params=_cp("arbitrary"),
        name=f"hyena_{length}",
    )(p_hy, p_hy, p_hy, conv_w, bias[None, :], c_m, s_m, kre, kim)


def _softmax_pv(heads):
    probs, dens = [], []
    for parts, sink in heads:
        m = None
        for s, _ in parts:
            mi = jnp.max(s, axis=-1, keepdims=True)
            m = mi if m is None else jnp.maximum(m, mi)
        if sink is not None:
            m = jnp.maximum(m, sink)
        den = 0.0 if sink is None else jnp.exp(sink - m)
        es = []
        for s, _ in parts:
            e = jnp.exp(s - m)
            den = den + jnp.sum(e, axis=-1, keepdims=True)
            es.append(e.astype(BF16))
        probs.append(es)
        dens.append(den)
    outs = []
    for (parts, _), es in zip(heads, probs):
        acc = None
        for e, (_, v) in zip(es, parts):
            o = _dot(e, v)
            acc = o if acc is None else acc + o
        outs.append(acc)
    return [o / den for o, den in zip(outs, dens)]


def _wa_kernel(sink_ref, q_ref, k_ref, v_ref, kc_ref, vc_ref, o_ref):
    n = pl.program_id(1)
    start = pl.multiple_of(jnp.clip((n - 1) * 128, 0, S - 384), 128)
    kw = k_ref[pl.ds(start, 384), :]
    vw = v_ref[pl.ds(start, 384), :]
    kc, vc = kc_ref[...], vc_ref[...]
    q = q_ref[...]
    rr = lax.broadcasted_iota(jnp.int32, (256, 384), 0)
    qpos = n * 128 + jnp.where(rr >= 128, rr - 128, rr)
    kpos = start + lax.broadcasted_iota(jnp.int32, (256, 384), 1)
    valid = jnp.abs(qpos - kpos) <= WINDOW
    r1 = lax.broadcasted_iota(jnp.int32, (256, 1), 0)
    heads = []
    for hk in range(2):
        q2 = jnp.concatenate([q[:, (2 * hk) * HD:(2 * hk + 1) * HD],
                              q[:, (2 * hk + 1) * HD:(2 * hk + 2) * HD]], axis=0)
        hs = slice(hk * HD, (hk + 1) * HD)
        s_loc = jnp.where(valid, _dot_nt(q2, kw[:, hs]) * SCALE, NEG)
        s_ctx = _dot_nt(q2, kc[:, hs]) * SCALE
        sink = jnp.where(r1 >= 128, sink_ref[2 * hk + 1], sink_ref[2 * hk])
        heads.append(([(s_ctx, vc[:, hs]), (s_loc, vw[:, hs])], sink))
    for hk, o in enumerate(_softmax_pv(heads)):
        o_ref[:, (2 * hk) * HD:(2 * hk + 1) * HD] = o[:128].astype(o_ref.dtype)
        o_ref[:, (2 * hk + 1) * HD:(2 * hk + 2) * HD] = o[128:].astype(o_ref.dtype)


def _window_attention(p_wa, sink):
    nb = S // 128
    return pl.pallas_call(
        _wa_kernel,
        grid=(B, nb),
        in_specs=[
            pl.BlockSpec(memory_space=pltpu.SMEM),
            pl.BlockSpec((128, 256), lambda b, n: (b * nb + n, 0)),
            pl.BlockSpec((S, 128), lambda b, n: (b, 2)),
            pl.BlockSpec((S, 128), lambda b, n: (b, 3)),
            pl.BlockSpec((LC, 128), lambda b, n: (CTX_BLK0 + b, 2)),
            pl.BlockSpec((LC, 128), lambda b, n: (CTX_BLK0 + b, 3)),
        ],
        out_specs=pl.BlockSpec((128, 256), lambda b, n: (b * nb + n, 0)),
        out_shape=jax.ShapeDtypeStruct((NL, 256), BF16),
        compiler_params=_cp("arbitrary", "arbitrary"),
        name="window_attn",
    )(sink, p_wa, p_wa, p_wa, p_wa, p_wa)


def _na_kernel(q_ref, k_ref, v_ref, kc_ref, vc_ref, bias_ref, o_ref):
    r = pl.program_id(1)
    start = pl.multiple_of(jnp.clip(r - NA_KR // 2, 0, S // GW - NA_KR) * GW, GW)
    kw = k_ref[pl.ds(start, NA_KR * GW), :]
    vw = v_ref[pl.ds(start, NA_KR * GW), :]
    kc, vc = kc_ref[...], vc_ref[...]
    q = q_ref[...]
    heads = []
    for h in range(4):
        hs = slice(h * HD, (h + 1) * HD)
        s_loc = _dot_nt(q[:, hs], kw[:, hs]) * SCALE + bias_ref[h, 0]
        s_ctx = _dot_nt(q[:, hs], kc[:, hs]) * SCALE
        heads.append(([(s_ctx, vc[:, hs]), (s_loc, vw[:, hs])], None))
    for h, o in enumerate(_softmax_pv(heads)):
        o_ref[:, h * HD:(h + 1) * HD] = o.astype(o_ref.dtype)


def _na_bias_kernel(rpb_ref, sel_ref, o_ref):
    o_ref[0] = _dot(rpb_ref[0], sel_ref[...], HI)


def _na_bias_table(rpb):
    qc = jnp.arange(GW)[:, None]
    kcol = jnp.arange(GW)[None, :]
    ci = jnp.clip(kcol - qc + NA_KC - 1, 0, 2 * NA_KC - 2).reshape(1, GW * GW)
    sel = (jnp.arange(128)[:, None] == ci).astype(F32)
    n_ri = 2 * NA_KR - 1
    rpb_p = jnp.pad(rpb.astype(F32), ((0, 0), (0, 16 - n_ri), (0, 128 - (2 * NA_KC - 1))))
    per_row = pl.pallas_call(
        _na_bias_kernel,
        grid=(4,),
        in_specs=[pl.BlockSpec((1, 16, 128), lambda h: (h, 0, 0)), pl.BlockSpec((128, GW * GW), lambda h: (0, 0))],
        out_specs=pl.BlockSpec((1, 16, GW * GW), lambda h: (h, 0, 0)),
        out_shape=jax.ShapeDtypeStruct((4, 16, GW * GW), F32),
        compiler_params=_cp("arbitrary"),
        name="na_bias",
    )(rpb_p, sel).reshape(4, 16, GW, GW)
    tbl = jnp.stack([per_row[:, o:o + NA_KR] for o in range(NA_KR)], axis=1)
    tbl = jnp.transpose(tbl, (0, 1, 3, 2, 4))
    c_start = jnp.clip(qc - NA_KC // 2, 0, GW - NA_KC)
    ok = (kcol >= c_start) & (kcol < c_start + NA_KC)
    tbl = jnp.where(ok[None, None, :, None, :], tbl, NEG)
    return tbl.reshape(4, NA_KR, GW, NA_KR * GW)


def _neighborhood_attention(p_na, bias_tbl):
    rows = S // GW

    def off(r):
        return jnp.clip(r - NA_KR // 2, 0, rows - NA_KR) - r + NA_KR - 1

    return pl.pallas_call(
        _na_kernel,
        grid=(B, rows),
        in_specs=[
            pl.BlockSpec((GW, 256), lambda b, r: (b * rows + r, 0)),
            pl.BlockSpec((S, 256), lambda b, r: (b, 1)),
            pl.BlockSpec((S, 256), lambda b, r: (b, 2)),
            pl.BlockSpec((LC, 256), lambda b, r: (CTX_BLK0 + b, 1)),
            pl.BlockSpec((LC, 256), lambda b, r: (CTX_BLK0 + b, 2)),
            pl.BlockSpec((4, 1, GW, NA_KR * GW), lambda b, r: (0, off(r), 0, 0)),
        ],
        out_specs=pl.BlockSpec((GW, 256), lambda b, r: (b * rows + r, 0)),
        out_shape=jax.ShapeDtypeStruct((NL, 256), BF16),
        compiler_params=_cp("arbitrary", "arbitrary"),
        name="neighborhood_attn",
    )(p_na, p_na, p_na, p_na, p_na, bias_tbl)


def _ctx_attn_kernel(sink_ref, qw_ref, kw_ref, vw_ref, qn_ref, kn_ref, vn_ref, owa_ref, ona_ref):
    q, k, v = qw_ref[...], kw_ref[...], vw_ref[...]
    r1 = lax.broadcasted_iota(jnp.int32, (2 * LC, 1), 0)
    heads = []
    for hk in range(2):
        q2 = jnp.concatenate([q[:, (2 * hk) * HD:(2 * hk + 1) * HD],
                              q[:, (2 * hk + 1) * HD:(2 * hk + 2) * HD]], axis=0)
        hs = slice(hk * HD, (hk + 1) * HD)
        sink = jnp.where(r1 >= LC, sink_ref[2 * hk + 1], sink_ref[2 * hk])
        heads.append(([(_dot_nt(q2, k[:, hs]) * SCALE, v[:, hs])], sink))
    for hk, o in enumerate(_softmax_pv(heads)):
        owa_ref[:, (2 * hk) * HD:(2 * hk + 1) * HD] = o[:LC].astype(owa_ref.dtype)
        owa_ref[:, (2 * hk + 1) * HD:(2 * hk + 2) * HD] = o[LC:].astype(owa_ref.dtype)
    q, k, v = qn_ref[...], kn_ref[...], vn_ref[...]
    heads = []
    for h in range(4):
        hs = slice(h * HD, (h + 1) * HD)
        heads.append(([(_dot_nt(q[:, hs], k[:, hs]) * SCALE, v[:, hs])], None))
    for h, o in enumerate(_softmax_pv(heads)):
        ona_ref[:, h * HD:(h + 1) * HD] = o.astype(ona_ref.dtype)


def _ctx_attention(p_wa, p_na, sink):
    blk = lambda c: (lambda b: (CTX_BLK0 + b, c))
    return pl.pallas_call(
        _ctx_attn_kernel,
        grid=(B,),
        in_specs=[
            pl.BlockSpec(memory_space=pltpu.SMEM),
            pl.BlockSpec((LC, 256), blk(0)),
            pl.BlockSpec((LC, 128), blk(2)),
            pl.BlockSpec((LC, 128), blk(3)),
            pl.BlockSpec((LC, 256), blk(0)),
            pl.BlockSpec((LC, 256), blk(1)),
            pl.BlockSpec((LC, 256), blk(2)),
        ],
        out_specs=[pl.BlockSpec((LC, 256), lambda b: (b, 0))] * 2,
        out_shape=[jax.ShapeDtypeStruct((NC, 256), BF16)] * 2,
        compiler_params=_cp("arbitrary"),
        name="ctx_attn",
    )(sink, p_wa, p_wa, p_wa, p_na, p_na, p_na)


def _gdn_prep_seq(p_ref, ab_ref, cw, alog, dtb, qkv_ref, gb_ref, row0, length):
    x = p_ref[:, :768].astype(F32)
    u = (_shift_rows(x, -2) * cw[0:1] + _shift_rows(x, -1) * cw[1:2] + x * cw[2:3]
         + _shift_rows(x, 1) * cw[3:4])
    u = _silu(u)
    rows = slice(row0, row0 + length)
    for j in range(8):
        sl = slice(j * HD, (j + 1) * HD)
        xs = u[:, sl]
        nrm = lax.rsqrt(jnp.sum(xs * xs, axis=-1, keepdims=True) + EPS)
        qkv_ref[0, rows, sl] = xs * (nrm * SCALE if j < 4 else nrm)
    qkv_ref[0, rows, 512:768] = u[:, 512:768]
    ab = ab_ref[...]
    g = -jnp.exp(alog) * _softplus(ab + dtb)
    lane = lax.broadcasted_iota(jnp.int32, ab.shape, 1)
    gb_ref[0, rows, :] = jnp.where(lane < 8, g, _sigmoid(ab))


def _gdn_prep_kernel(pl_ref, abl_ref, pc_ref, abc_ref, cw_ref, alog_ref, dtb_ref, qkv_ref, gb_ref):
    cw, alog, dtb = cw_ref[...], alog_ref[...], dtb_ref[...]
    _gdn_prep_seq(pl_ref, abl_ref, cw, alog, dtb, qkv_ref, gb_ref, 0, S)
    _gdn_prep_seq(pc_ref, abc_ref, cw, alog, dtb, qkv_ref, gb_ref, S, LC)


def _gdn_prep(p_gd, p_ab, conv_w, a_log, dt_bias):
    const = lambda b: (0, 0)
    alog = jnp.pad(a_log.reshape(1, 8), ((0, 0), (0, 120)))
    dtb = jnp.pad(dt_bias.reshape(1, 8), ((0, 0), (0, 120)))
    return pl.pallas_call(
        _gdn_prep_kernel,
        grid=(B,),
        in_specs=[
            pl.BlockSpec((S, 1024), lambda b: (b, 0)),
            pl.BlockSpec((S, 128), lambda b: (b, 0)),
            pl.BlockSpec((LC, 1024), lambda b: (CTX_BLK0 + b, 0)),
            pl.BlockSpec((LC, 128), lambda b: (CTX_BLK0 + b, 0)),
            pl.BlockSpec((4, 768), const),
            pl.BlockSpec((1, 128), const),
            pl.BlockSpec((1, 128), const),
        ],
        out_specs=[pl.BlockSpec((1, S + LC, 768), lambda b: (b, 0, 0)),
                   pl.BlockSpec((1, S + LC, 128), lambda b: (b, 0, 0))],
        out_shape=[jax.ShapeDtypeStruct((B, S + LC, 768), F32), jax.ShapeDtypeStruct((B, S + LC, 128), F32)],
        compiler_params=_cp("arbitrary"),
        name="gdn_prep",
    )(p_gd, p_ab, p_gd, p_ab, conv_w, alog, dtb)


def _gdn_chains(xs, gbs, cums, cum_ts, states, masks):
    chains = [(g, h) for g in range(len(xs)) for h in range(4)]
    pre = []
    for g, h in chains:
        dirn = g % 2
        x, gb = xs[g], gbs[g]
        tri, strict = masks[dirn]
        col = dirn * 4 + h
        qh = x[:, h * HD:(h + 1) * HD]
        kh = x[:, 256 + h * HD:256 + (h + 1) * HD]
        vh = x[:, 512 + h * HD:512 + (h + 1) * HD]
        gc = cums[g][:, col:col + 1]
        gc_row = cum_ts[g][col:col + 1, :]
        beta = gb[:, 8 + col:9 + col]
        decay = jnp.exp(jnp.where(tri, gc - gc_row, NEG))
        kb = kh * beta
        eg = jnp.exp(gc)
        g_last = gc[0:1, :] if dirn else gc[CHUNK - 1:CHUNK, :]
        pre.append(dict(strict=strict, decay=decay, kbf=kb.astype(BF16), khf=kh.astype(BF16), qf=qh.astype(BF16),
                        xw=jnp.concatenate([vh * beta, kb * eg], axis=1), qdec=(qh * eg).astype(BF16),
                        kdec=(kh * jnp.exp(g_last - gc)).astype(BF16), gl=jnp.exp(g_last)))
    n = len(chains)
    gram = [_dot_nt(p["kbf"], p["khf"]) for p in pre]
    attn = [_dot_nt(p["qf"], p["khf"]) for p in pre]
    lmat = [jnp.where(p["strict"], g * p["decay"], 0.0) for p, g in zip(pre, gram)]
    attn = [(a * p["decay"]).astype(BF16) for p, a in zip(pre, attn)]
    ii = lax.broadcasted_iota(jnp.int32, (CHUNK, CHUNK), 0)
    jj = lax.broadcasted_iota(jnp.int32, (CHUNK, CHUNK), 1)

    def merged_off_blocks(log_s):
        return ((ii >> (log_s + 1)) == (jj >> (log_s + 1))) & ((ii >> log_s) != (jj >> log_s))

    eye = (ii == jj).astype(F32)
    dinv = [eye - jnp.where(merged_off_blocks(0), lm, 0.0) for lm in lmat]
    for log_s in range(1, 6):
        m = merged_off_blocks(log_s)
        dinv_b = [d.astype(BF16) for d in dinv]
        ld = [_dot(jnp.where(m, lmat[c], 0.0).astype(BF16), dinv_b[c]) for c in range(n)]
        upd = [_dot(dinv_b[c], ld[c].astype(BF16)) for c in range(n)]
        dinv = [dinv[c] - upd[c] for c in range(n)]
    xw = [_dot(dinv[c].astype(BF16), pre[c]["xw"].astype(BF16)) for c in range(n)]
    sts = [states[g][h] for g, h in chains]
    stb = [s.astype(BF16) for s in sts]
    ws = [_dot(xw[c][:, HD:].astype(BF16), stb[c]) for c in range(n)]
    qs = [_dot(pre[c]["qdec"], stb[c]) for c in range(n)]
    u_new = [(xw[c][:, :HD] - ws[c]).astype(BF16) for c in range(n)]
    au = [_dot(attn[c], u_new[c]) for c in range(n)]
    ku = [_dot_tn(pre[c]["kdec"], u_new[c]) for c in range(n)]
    outs = [qs[c] + au[c] for c in range(n)]
    new_states = [sts[c] * pre[c]["gl"] + ku[c] for c in range(n)]
    return outs, new_states


def _gdn_kernel(x0_ref, gb0_ref, x1_ref, gb1_ref, o0_ref, o1_ref, state_ref):
    @pl.when(pl.program_id(0) == 0)
    def _():
        state_ref[...] = jnp.zeros_like(state_ref)

    ii = lax.broadcasted_iota(jnp.int32, (CHUNK, CHUNK), 0)
    jj = lax.broadcasted_iota(jnp.int32, (CHUNK, CHUNK), 1)
    masks = ((jj <= ii, jj < ii), (jj >= ii, jj > ii))
    ones_tri = ((jj <= ii).astype(F32), (jj >= ii).astype(F32))

    def body(bp, carry):
        bs = [bp * GDN_SEQS + i for i in range(GDN_SEQS)]
        states = [[state_ref[b, dirn, h] for h in range(4)] for b in bs for dirn in range(2)]
        xs = [r[b] for b in bs for r in (x0_ref, x1_ref)]
        gbs = [r[b] for b in bs for r in (gb0_ref, gb1_ref)]
        cums = [_dot(ones_tri[g % 2], gbs[g], HI) for g in range(2 * GDN_SEQS)]
        cum_ts = [cm.T for cm in cums]
        outs, new_states = _gdn_chains(xs, gbs, cums, cum_ts, states, masks)
        for c, st in enumerate(new_states):
            state_ref[bs[c // 8], (c // 4) % 2, c % 4] = st
        for i, b in enumerate(bs):
            o0_ref[b] = jnp.concatenate(outs[8 * i:8 * i + 4], axis=1)
            o1_ref[b] = jnp.concatenate(outs[8 * i + 4:8 * i + 8], axis=1)
        return carry

    lax.fori_loop(0, B // GDN_SEQS, body, 0)


def _gdn_scan(qkv, gb):
    nc = (S + LC) // CHUNK
    fwd = lambda c: (0, (c + S // CHUNK) % nc, 0)
    bwd = lambda c: (0, nc - 1 - c, 0)
    return pl.pallas_call(
        _gdn_kernel,
        grid=(nc,),
        in_specs=[
            pl.BlockSpec((B, CHUNK, 768), fwd),
            pl.BlockSpec((B, CHUNK, 128), fwd),
            pl.BlockSpec((B, CHUNK, 768), bwd),
            pl.BlockSpec((B, CHUNK, 128), bwd),
        ],
        out_specs=[pl.BlockSpec((B, CHUNK, 256), fwd), pl.BlockSpec((B, CHUNK, 256), bwd)],
        out_shape=[jax.ShapeDtypeStruct((B, S + LC, 256), F32)] * 2,
        scratch_shapes=[pltpu.VMEM((B, 2, 4, HD, HD), F32)],
        compiler_params=_cp("arbitrary"),
        name="gdn_scan",
    )(qkv, gb, qkv, gb)


def _gdn_finish_kernel(o0_ref, o1_ref, gate_ref, ng_ref, y_ref):
    o = o0_ref[0] + o1_ref[0]
    gi = lax.broadcasted_iota(jnp.int32, (256, 256), 0) // HD
    gj = lax.broadcasted_iota(jnp.int32, (256, 256), 1) // HD
    group_mean = jnp.where(gi == gj, 1.0 / HD, 0.0)
    ms = _dot(o * o, group_mean, HI)
    y = o * lax.rsqrt(ms + EPS) * ng_ref[...]
    y_ref[...] = (y * _silu(gate_ref[...].astype(F32))).astype(y_ref.dtype)


def _gdn_finish(o0, o1, p_gd, norm_g, need_ctx):
    lat_tiles = S // TM
    row_blk = lambda b, j: jnp.where(j < lat_tiles, b * lat_tiles + j, CTX_BLK0 + b)
    return pl.pallas_call(
        _gdn_finish_kernel,
        grid=(B, lat_tiles + (1 if need_ctx else 0)),
        in_specs=[
            pl.BlockSpec((1, TM, 256), lambda b, j: (b, j, 0)),
            pl.BlockSpec((1, TM, 256), lambda b, j: (b, j, 0)),
            pl.BlockSpec((TM, 256), lambda b, j: (row_blk(b, j), 3)),
            pl.BlockSpec((1, 256), lambda b, j: (0, 0)),
        ],
        out_specs=pl.BlockSpec((TM, 256), lambda b, j: (row_blk(b, j), 0)),
        out_shape=jax.ShapeDtypeStruct((ROWS if need_ctx else NL, 256), BF16),
        compiler_params=_cp("arbitrary", "arbitrary"),
        name="gdn_finish",
    )(o0, o1, p_gd, jnp.tile(norm_g, 4)[None, :])


def _pack_bf16_pairs(x):
    n = x.shape[1] // 2
    lo = pltpu.bitcast(x[:, :n].astype(BF16).astype(F32), jnp.uint32)
    hi = pltpu.bitcast(x[:, n:].astype(BF16).astype(F32), jnp.uint32)
    return hi | (lo >> 16)


def _unpack_bf16_pairs(w):
    lo = pltpu.bitcast(w << 16, F32)
    hi = pltpu.bitcast(w & jnp.uint32(0xFFFF0000), F32)
    return jnp.concatenate([lo, hi], axis=1)


def _outproj_kernel(x_ref, mod_ref, g2_ref, yh_ref, yw_ref, yn_ref, yg_ref, wo_ref, wr_ref, eb_ref,
                    xo_ref, h2_ref, h2p_ref, idx_ref, tw_ref, rank_ref, cnt_ref, carry_ref):
    @pl.when(pl.program_id(0) == 0)
    def _():
        carry_ref[...] = jnp.zeros_like(carry_ref)

    m = mod_ref[0]
    acc = (_dot(yh_ref[...], wo_ref[0:256, :]) + _dot(yw_ref[...], wo_ref[256:512, :])
           + _dot(yn_ref[...], wo_ref[512:768, :]) + _dot(yg_ref[...], wo_ref[768:1024, :]))
    x = x_ref[...] + m[2:3] * acc
    xo_ref[...] = x
    h2 = _modulated_norm(x, g2_ref[...], m[3:4], m[4:5])
    h2_ref[...] = h2.astype(h2_ref.dtype)
    h2p_ref[...] = _pack_bf16_pairs(h2)
    scores = _sigmoid(_dot(h2, wr_ref[...], HI))
    lane = lax.broadcasted_iota(jnp.int32, scores.shape, 1)
    sel = jnp.where(lane < N_EXP, scores + eb_ref[...], -jnp.inf)
    lane_f = lane.astype(F32)
    idx_out = jnp.zeros(scores.shape, F32)
    s_out = jnp.zeros(scores.shape, F32)
    hits = []
    for kk in range(TOP_K):
        mx = jnp.max(sel, axis=-1, keepdims=True)
        idx = jnp.min(jnp.where(sel == mx, lane_f, 128.0), axis=-1, keepdims=True)
        hit = lane_f == idx
        sk = jnp.sum(jnp.where(hit, scores, 0.0), axis=-1, keepdims=True)
        sel = jnp.where(hit, -jnp.inf, sel)
        idx_out = jnp.where(lane == kk, idx, idx_out)
        s_out = jnp.where(lane == kk, sk, s_out)
        hits.append(hit)
    tot = jnp.sum(s_out, axis=-1, keepdims=True)
    idx_ref[...] = idx_out.astype(jnp.int32)
    tw_ref[...] = s_out / tot * ROUTED_SCALE
    cnt = jnp.zeros(scores.shape, F32)
    for hit in hits:
        cnt = cnt + jnp.where(hit, 1.0, 0.0)
    ri = lax.broadcasted_iota(jnp.int32, (TM, TM), 0)
    rj = lax.broadcasted_iota(jnp.int32, (TM, TM), 1)
    before = _dot(jnp.where(rj < ri, 1.0, 0.0).astype(BF16), cnt.astype(BF16)) + carry_ref[...]
    rank_out = jnp.zeros(scores.shape, F32)
    for kk, hit in enumerate(hits):
        rk = jnp.sum(jnp.where(hit, before, 0.0), axis=-1, keepdims=True)
        rank_out = jnp.where(lane == kk, rk, rank_out)
    rank_ref[...] = rank_out.astype(jnp.int32)
    carry_ref[...] = carry_ref[...] + jnp.sum(cnt, axis=0, keepdims=True)
    cnt_ref[...] = carry_ref[...].astype(jnp.int32)


def _outproj(xa, mod, l, g2, ys, wo, wr, eb, n_tiles):
    row = lambda i: (i, 0)
    const = lambda i: (0, 0)
    rows = n_tiles * TM
    return pl.pallas_call(
        _outproj_kernel,
        grid=(n_tiles,),
        in_specs=[
            pl.BlockSpec((TM, D), row),
            pl.BlockSpec((1, 6, D), lambda i: (l * 16 + (i * TM) // S, 0, 0)),
            pl.BlockSpec((1, D), const),
            pl.BlockSpec((TM, 256), row),
            pl.BlockSpec((TM, 256), row),
            pl.BlockSpec((TM, 256), row),
            pl.BlockSpec((TM, 256), row),
            pl.BlockSpec((D, D), const),
            pl.BlockSpec((D, 128), const),
            pl.BlockSpec((1, 128), const),
        ],
        out_specs=[
            pl.BlockSpec((TM, D), row),
            pl.BlockSpec((TM, D), row),
            pl.BlockSpec((TM, D // 2), row),
            pl.BlockSpec((TM, 128), row),
            pl.BlockSpec((TM, 128), row),
            pl.BlockSpec((TM, 128), row),
            pl.BlockSpec((1, 128), const),
        ],
        out_shape=[
            jax.ShapeDtypeStruct((rows, D), F32),
            jax.ShapeDtypeStruct((rows, D), BF16),
            jax.ShapeDtypeStruct((rows, D // 2), jnp.uint32),
            jax.ShapeDtypeStruct((rows, 128), jnp.int32),
            jax.ShapeDtypeStruct((rows, 128), F32),
            jax.ShapeDtypeStruct((rows, 128), jnp.int32),
            jax.ShapeDtypeStruct((1, 128), jnp.int32),
        ],
        scratch_shapes=[pltpu.VMEM((1, 128), F32)],
        compiler_params=_cp("arbitrary"),
        name="outproj_router",
    )(xa, mod, g2, *ys, wo, wr, eb)


ASG_TILE = TM * TOP_K


def _segment_tables(counts, n_blocks):
    counts = counts[0, :N_EXP]
    padded = (counts + MOE_BLK - 1) // MOE_BLK * MOE_BLK
    pad_end = jnp.cumsum(padded)
    starts = jnp.arange(n_blocks, dtype=jnp.int32) * MOE_BLK
    blk_e = jnp.minimum(jnp.sum((pad_end[None, :] <= starts[:, None]).astype(jnp.int32), axis=1), N_EXP - 1)
    return (pad_end - padded).astype(jnp.int32), blk_e.astype(jnp.int32)


def _row_copy_wait(shape_ref, dst_ref, sem):
    pltpu.make_async_copy(shape_ref, dst_ref, sem).wait()


def _dispatch_kernel(seg_ref, idx_ref, rank_ref, h2p_ref, xb_in_ref, xb_ref, sem):
    del xb_in_ref

    def issue(t, carry):
        src = h2p_ref.at[pl.ds(t, 1)]
        for k in range(TOP_K):
            a = t * TOP_K + k
            dst = seg_ref[idx_ref[a]] + rank_ref[a]
            pltpu.make_async_copy(src, xb_ref.at[pl.ds(dst, 1)], sem).start()
        return carry

    lax.fori_loop(0, TM, issue, 0, unroll=8)
    for _ in range(TOP_K):
        _row_copy_wait(h2p_ref, xb_ref.at[pl.ds(0, TM)], sem)


def _dispatch_rows(seg_start, idx_c, rank_c, h2p, n_rows):
    n_tiles = h2p.shape[0] // TM
    grid_spec = pltpu.PrefetchScalarGridSpec(
        num_scalar_prefetch=1,
        grid=(n_tiles,),
        in_specs=[
            pl.BlockSpec((ASG_TILE,), lambda i, seg: (i,), memory_space=pltpu.SMEM),
            pl.BlockSpec((ASG_TILE,), lambda i, seg: (i,), memory_space=pltpu.SMEM),
            pl.BlockSpec((TM, D // 2), lambda i, seg: (i, 0)),
            pl.BlockSpec(memory_space=pl.ANY),
        ],
        out_specs=pl.BlockSpec(memory_space=pl.ANY),
        scratch_shapes=[pltpu.SemaphoreType.DMA(())],
    )
    return pl.pallas_call(
        _dispatch_kernel,
        grid_spec=grid_spec,
        out_shape=jax.ShapeDtypeStruct((n_rows, D // 2), jnp.uint32),
        input_output_aliases={4: 0},
        compiler_params=_cp("arbitrary"),
        name="moe_dispatch",
    )(seg_start, idx_c, rank_c, h2p, jnp.zeros((n_rows, D // 2), jnp.uint32))


def _moe_kernel(be_ref, x_ref, wg_ref, wu_ref, wd_ref, o_ref):
    del be_ref
    x = _unpack_bf16_pairs(x_ref[...]).astype(BF16)
    g = _dot(x, wg_ref[0].astype(BF16))
    u = _dot(x, wu_ref[0].astype(BF16))
    hid = (_silu(g) * u).astype(BF16)
    o_ref[...] = _pack_bf16_pairs(_dot(hid, wd_ref[0].astype(BF16)))


def _moe_experts(blk_e, xb, wg, wu, wd):
    n_rows = xb.shape[0]
    grid_spec = pltpu.PrefetchScalarGridSpec(
        num_scalar_prefetch=1,
        grid=(n_rows // MOE_BLK,),
        in_specs=[
            pl.BlockSpec((MOE_BLK, D // 2), lambda i, be: (i, 0)),
            pl.BlockSpec((1, D, D_EXP), lambda i, be: (be[i], 0, 0)),
            pl.BlockSpec((1, D, D_EXP), lambda i, be: (be[i], 0, 0)),
            pl.BlockSpec((1, D_EXP, D), lambda i, be: (be[i], 0, 0)),
        ],
        out_specs=pl.BlockSpec((MOE_BLK, D // 2), lambda i, be: (i, 0)),
    )
    return pl.pallas_call(
        _moe_kernel,
        grid_spec=grid_spec,
        out_shape=jax.ShapeDtypeStruct((n_rows, D // 2), jnp.uint32),
        compiler_params=_cp("arbitrary"),
        name="moe_experts",
    )(blk_e, xb, wg, wu, wd)


def _ffn_out_kernel(seg_ref, idx_ref, rank_ref, x_ref, mod_ref, h2_ref, tw_ref, wg_ref, wu_ref, wd_ref, nf_ref,
                    yb_ref, o_ref, gat_ref, sem, *, final):
    def issue(t, carry):
        for k in range(TOP_K):
            a = t * TOP_K + k
            src = seg_ref[idx_ref[a]] + rank_ref[a]
            pltpu.make_async_copy(yb_ref.at[pl.ds(src, 1)], gat_ref.at[pl.ds(k * TM + t, 1)], sem).start()
        return carry

    lax.fori_loop(0, TM, issue, 0, unroll=8)
    m = mod_ref[0]
    h2 = h2_ref[...]
    hid = (_silu(_dot(h2, wg_ref[...])) * _dot(h2, wu_ref[...])).astype(BF16)
    shared = _dot(hid, wd_ref[...])
    for _ in range(TOP_K):
        _row_copy_wait(yb_ref.at[pl.ds(0, TM)], gat_ref.at[pl.ds(0, TM)], sem)
    tw = tw_ref[...]
    routed = _unpack_bf16_pairs(gat_ref[0:TM, :]) * tw[:, 0:1]
    for k in range(1, TOP_K):
        routed = routed + _unpack_bf16_pairs(gat_ref[k * TM:(k + 1) * TM, :]) * tw[:, k:k + 1]
    x = x_ref[...] + m[5:6] * (routed + shared)
    if final:
        ms = jnp.mean(x * x, axis=-1, keepdims=True)
        x = x * lax.rsqrt(ms + EPS) * nf_ref[...]
    o_ref[...] = x


def _ffn_out(seg_start, idx_c, rank_c, xn, mod, l, h2, top_w, wg, wu, wd, norm_f, yb, n_tiles, final):
    row = lambda i, seg: (i, 0)
    const = lambda i, seg: (0, 0)
    grid_spec = pltpu.PrefetchScalarGridSpec(
        num_scalar_prefetch=1,
        grid=(n_tiles,),
        in_specs=[
            pl.BlockSpec((ASG_TILE,), lambda i, seg: (i,), memory_space=pltpu.SMEM),
            pl.BlockSpec((ASG_TILE,), lambda i, seg: (i,), memory_space=pltpu.SMEM),
            pl.BlockSpec((TM, D), row),
            pl.BlockSpec((1, 6, D), lambda i, seg: (l * 16 + (i * TM) // S, 0, 0)),
            pl.BlockSpec((TM, D), row),
            pl.BlockSpec((TM, 128), row),
            pl.BlockSpec((D, D_EXP), const),
            pl.BlockSpec((D, D_EXP), const),
            pl.BlockSpec((D_EXP, D), const),
            pl.BlockSpec((1, D), const),
            pl.BlockSpec(memory_space=pl.ANY),
        ],
        out_specs=pl.BlockSpec((TM, D), row),
        scratch_shapes=[pltpu.VMEM((ASG_TILE, D // 2), jnp.uint32), pltpu.SemaphoreType.DMA(())],
    )
    return pl.pallas_call(
        functools.partial(_ffn_out_kernel, final=final),
        grid_spec=grid_spec,
        out_shape=jax.ShapeDtypeStruct((n_tiles * TM, D), F32),
        compiler_params=_cp("arbitrary"),
        name="shared_ffn_residual",
    )(seg_start, idx_c, rank_c, xn, mod, h2, top_w, wg, wu, wd, norm_f, yb)


def kernel(x, c, ctx, c_ctx, w_ada, b_ada, norm1, norm2, norm_f, w_in, w_out, hy_conv, hy_w1, hy_b1, hy_w2, hy_b2, hy_w3, hy_freq, hy_bias, wa_sink, na_rpb, gdn_conv, gdn_a_log, gdn_dt_bias, gdn_norm, moe_router, moe_bias, moe_gate, moe_up, moe_down, sh_gate, sh_up, sh_down):
    depth = w_ada.shape[0]
    xa = jnp.concatenate([x.reshape(NL, D), ctx.reshape(NC, D)], axis=0)
    cvec = jnp.concatenate([c, c_ctx[None, :], jnp.zeros((16 - B - 1, D), F32)], axis=0)
    mod = _ada(cvec, w_ada, b_ada).reshape(depth * 16, 6, D)
    rope = _rope_tables()
    dft_lat = _dft_tables(S)
    dft_ctx = _dft_tables(LC)
    o1, o2, o3 = 768, 768 + 512, 768 + 512 + 768

    for l in range(depth):
        need_ctx = l < depth - 1
        n_tiles = NT_ALL if need_ctx else NT_LAT
        wl = w_in[l].astype(BF16)
        ws = (wl[:, :o1], wl[:, o1:o2], wl[:, o2:o3], wl[:, o3:o3 + 1024],
              jnp.pad(wl[:, o3 + 1024:], ((0, 0), (0, 128 - 16))))
        p_hy, p_wa, p_na, p_gd, p_ab = _inproj(xa, mod, l, norm1[l][None, :], ws, rope)

        filt = (hy_w1[l], hy_b1[l], hy_w2[l], hy_b2[l], hy_w3[l], hy_freq[l])
        kre, kim = _hyena_filter_spectrum(S, *filt, dft_lat)
        y_hy = _hyena(p_hy, S, 0, hy_conv[l], hy_bias[l], dft_lat, kre, kim)
        y_wa = _window_attention(p_wa, wa_sink[l])
        y_na = _neighborhood_attention(p_na, _na_bias_table(na_rpb[l]))
        if need_ctx:
            kre_c, kim_c = _hyena_filter_spectrum(LC, *filt, dft_ctx)
            yc_hy = _hyena(p_hy, LC, CTX_BLK0, hy_conv[l], hy_bias[l], dft_ctx, kre_c, kim_c)
            yc_wa, yc_na = _ctx_attention(p_wa, p_na, wa_sink[l])
            y_hy = jnp.concatenate([y_hy, yc_hy], axis=0)
            y_wa = jnp.concatenate([y_wa, yc_wa], axis=0)
            y_na = jnp.concatenate([y_na, yc_na], axis=0)
        qkv, gb = _gdn_prep(p_gd, p_ab, gdn_conv[l], gdn_a_log[l], gdn_dt_bias[l])
        y_gd = _gdn_finish(*_gdn_scan(qkv, gb), p_gd, gdn_norm[l], need_ctx)

        wr = jnp.pad(moe_router[l], ((0, 0), (0, 128 - N_EXP)))
        eb = jnp.pad(moe_bias[l], (0, 128 - N_EXP))[None, :]
        xn, h2, h2p, top_idx, top_w, rank, counts = _outproj(
            xa, mod, l, norm2[l][None, :], (y_hy, y_wa, y_na, y_gd), w_out[l].astype(BF16), wr, eb, n_tiles)
        n_asg = n_tiles * ASG_TILE
        n_blocks = n_asg // MOE_BLK + N_EXP
        seg_start, blk_e = _segment_tables(counts, n_blocks)
        idx_c = top_idx[:, :TOP_K].reshape(n_asg)
        rank_c = rank[:, :TOP_K].reshape(n_asg)
        xb = _dispatch_rows(seg_start, idx_c, rank_c, h2p, n_blocks * MOE_BLK)
        yb = _moe_experts(blk_e, xb, moe_gate[l], moe_up[l], moe_down[l])
        xa_new = _ffn_out(seg_start, idx_c, rank_c, xn, mod, l, h2, top_w, sh_gate[l].astype(BF16),
                          sh_up[l].astype(BF16), sh_down[l].astype(BF16), norm_f[None, :], yb, n_tiles,
                          final=not need_ctx)
        if need_ctx:
            xa = xa_new
        else:
            return xa_new.reshape(B, S, D)
```

```python
import functools
import math

import jax
import jax.numpy as jnp
from jax import lax
from jax.experimental import pallas as pl
from jax.experimental.pallas import tpu as pltpu

F32 = jnp.float32
BF16 = jnp.bfloat16
HI = lax.Precision.HIGHEST

D = 1024
B = 8
S = 2048
LC = 256
GW = 64
HD = 64
NL = B * S
NC = B * LC
ROWS = NL + NC
TM = 256
NT_LAT = NL // TM
NT_ALL = ROWS // TM
CTX_BLK0 = NL // LC

HY_CH = 256
HY_BANDS = 16
HY_DECAY_MIN = -math.log(1e-2) / 1.5
HY_DECAY_MAX = -math.log(1e-2) / 0.3
WINDOW = 128
NA_KR = 8
NA_KC = 16
CHUNK = 64
N_EXP = 32
TOP_K = 4
D_EXP = 256
ROUTED_SCALE = 2.5
MOE_BLK = 512
NA_ROWS = 4
GDN_SEQS = 4
EPS = 1e-6
NEG = -1e30
SCALE = HD ** -0.5
VMEM_LIMIT = 56 * 1024 * 1024


def _cp(*sem):
    return pltpu.CompilerParams(dimension_semantics=tuple(sem), vmem_limit_bytes=VMEM_LIMIT)


def _dot(a, b, precision=None):
    return jnp.dot(a, b, preferred_element_type=F32, precision=precision)


def _dot_nt(a, b, precision=None):
    return lax.dot_general(a, b, (((1,), (1,)), ((), ())), preferred_element_type=F32, precision=precision)


def _dot_tn(a, b, precision=None):
    return lax.dot_general(a, b, (((0,), (0,)), ((), ())), preferred_element_type=F32, precision=precision)


def _sigmoid(x):
    return 1.0 / (1.0 + jnp.exp(-x))


def _silu(x):
    return x * _sigmoid(x)


def _softplus(x):
    return jnp.maximum(x, 0.0) + jnp.log(1.0 + jnp.exp(-jnp.abs(x)))


def _shift_rows(x, d):
    n = x.shape[0]
    if d == 0:
        return x
    y = pltpu.roll(x, (-d) % n, axis=0)
    t = lax.broadcasted_iota(jnp.int32, x.shape, 0)
    ok = (t + d >= 0) & (t + d < n)
    return jnp.where(ok, y, 0.0)


def _ada_kernel(c_ref, w_ref, b_ref, o_ref):
    s = _silu(c_ref[...])
    o_ref[0] = _dot(s.astype(BF16), w_ref[0].astype(BF16)) + b_ref[0]


def _ada(cvec, w_ada, b_ada):
    nl = w_ada.shape[0]
    tn = 1536
    return pl.pallas_call(
        _ada_kernel,
        grid=(nl, 6 * D // tn),
        in_specs=[
            pl.BlockSpec((16, D), lambda l, j: (0, 0)),
            pl.BlockSpec((1, D, tn), lambda l, j: (l, 0, j)),
            pl.BlockSpec((1, 1, tn), lambda l, j: (l, 0, j)),
        ],
        out_specs=pl.BlockSpec((1, 16, tn), lambda l, j: (l, 0, j)),
        out_shape=jax.ShapeDtypeStruct((nl, 16, 6 * D), F32),
        compiler_params=_cp("arbitrary", "arbitrary"),
        name="adaln",
    )(cvec, w_ada, b_ada.reshape(nl, 1, 6 * D))


def _modulated_norm(x, g, shift, scale):
    ms = jnp.mean(x * x, axis=-1, keepdims=True)
    y = x * lax.rsqrt(ms + EPS) * g
    return y * (1.0 + scale) + shift


def _inproj_kernel(x_ref, mod_ref, g_ref, why_ref, wwa_ref, wna_ref, wgd_ref, wab_ref,
                   cos_ref, sa_ref, sb_ref, ohy, owa, ona, ogd, oab):
    m = mod_ref[0]
    h = _modulated_norm(x_ref[...], g_ref[...], m[0:1], m[1:2]).astype(BF16)
    ohy[...] = _dot(h, why_ref[...]).astype(ohy.dtype)
    ona[...] = _dot(h, wna_ref[...]).astype(ona.dtype)
    ogd[...] = _dot(h, wgd_ref[...]).astype(ogd.dtype)
    oab[...] = _dot(h, wab_ref[...])
    a = _dot(h, wwa_ref[...])
    for c in range(4):
        sl = slice(c * 128, (c + 1) * 128)
        ac = a[:, sl]
        r = (ac * cos_ref[:, sl] + pltpu.roll(ac, 112, axis=1) * sa_ref[:, sl]
             + pltpu.roll(ac, 16, axis=1) * sb_ref[:, sl])
        owa[:, sl] = r.astype(owa.dtype)


def _inproj(xa, mod, l, g, ws, tabs):
    why, wwa, wna, wgd, wab = ws
    cos_t, sa_t, sb_t = tabs
    row = lambda i: (i, 0)
    const = lambda i: (0, 0)
    tab = lambda i: (jnp.where(i < NT_LAT, i % (S // TM), S // TM), 0)
    return pl.pallas_call(
        _inproj_kernel,
        grid=(NT_ALL,),
        in_specs=[
            pl.BlockSpec((TM, D), row),
            pl.BlockSpec((1, 6, D), lambda i: (l * 16 + (i * TM) // S, 0, 0)),
            pl.BlockSpec((1, D), const),
            pl.BlockSpec((D, 768), const),
            pl.BlockSpec((D, 512), const),
            pl.BlockSpec((D, 768), const),
            pl.BlockSpec((D, 1024), const),
            pl.BlockSpec((D, 128), const),
            pl.BlockSpec((TM, 512), tab),
            pl.BlockSpec((TM, 512), tab),
            pl.BlockSpec((TM, 512), tab),
        ],
        out_specs=[
            pl.BlockSpec((TM, 768), row),
            pl.BlockSpec((TM, 512), row),
            pl.BlockSpec((TM, 768), row),
            pl.BlockSpec((TM, 1024), row),
            pl.BlockSpec((TM, 128), row),
        ],
        out_shape=[
            jax.ShapeDtypeStruct((ROWS, 768), BF16),
            jax.ShapeDtypeStruct((ROWS, 512), BF16),
            jax.ShapeDtypeStruct((ROWS, 768), BF16),
            jax.ShapeDtypeStruct((ROWS, 1024), BF16),
            jax.ShapeDtypeStruct((ROWS, 128), F32),
        ],
        compiler_params=_cp("arbitrary"),
        name="inproj",
    )(xa, mod, g, why, wwa, wna, wgd, wab, cos_t, sa_t, sb_t)


def _rope_tables():
    quarter = HD // 4
    pos = jnp.arange(S)
    inv = 10000.0 ** (-jnp.arange(quarter, dtype=F32) / quarter)
    ang_r = (pos // GW).astype(F32)[:, None] * inv[None, :]
    ang_c = (pos % GW).astype(F32)[:, None] * inv[None, :]
    z = jnp.zeros_like(ang_r)
    cos_h = jnp.concatenate([jnp.cos(ang_r)] * 2 + [jnp.cos(ang_c)] * 2, axis=1)
    sa_h = jnp.concatenate([-jnp.sin(ang_r), z, -jnp.sin(ang_c), z], axis=1)
    sb_h = jnp.concatenate([z, jnp.sin(ang_r), z, jnp.sin(ang_c)], axis=1)
    ones = jnp.ones((S, 128), F32)
    zeros = jnp.zeros((S, 128), F32)
    cos_t = jnp.concatenate([jnp.tile(cos_h, (1, 6)), ones], axis=1)
    sa_t = jnp.concatenate([jnp.tile(sa_h, (1, 6)), zeros], axis=1)
    sb_t = jnp.concatenate([jnp.tile(sb_h, (1, 6)), zeros], axis=1)
    ident = jnp.ones((TM, 512), F32)
    none = jnp.zeros((TM, 512), F32)
    return (jnp.concatenate([cos_t, ident], axis=0), jnp.concatenate([sa_t, none], axis=0),
            jnp.concatenate([sb_t, none], axis=0))


def _dft_tables(length):
    n = 2 * length
    f = jnp.arange(length, dtype=jnp.int32)
    m = ((2 * f[:, None] + 1) * (2 * f[None, :] + 1)) % (4 * n)
    th = m.astype(F32) * (2.0 * math.pi / (4 * n))
    phi = (2 * f + 1).astype(F32) * (math.pi / (2 * n))
    return (jnp.cos(th).astype(BF16), jnp.sin(th).astype(BF16),
            jnp.cos(phi)[:, None], jnp.sin(phi)[:, None])


def _hyena_features(length):
    t = jnp.arange(length, dtype=F32)
    t_norm = t / max(length - 1, 1)
    bands = jnp.linspace(1e-4, HY_BANDS - 1, HY_BANDS, dtype=F32)
    ang = (2.0 * math.pi / length) * t[:, None] * bands[None, :]
    z = jnp.concatenate([t_norm[:, None], jnp.cos(ang), -jnp.sin(ang)], axis=-1)
    z = jnp.pad(z, ((0, 0), (0, 128 - z.shape[1])))
    decay = jnp.tile(jnp.linspace(HY_DECAY_MIN, HY_DECAY_MAX, HY_CH, dtype=F32), 2)
    return z, jnp.exp(-t_norm[:, None] * decay[None, :])


def _hyfilt_kernel(z_ref, w1_ref, b1_ref, w2_ref, b2_ref, w3_ref, fr_ref, dec_ref,
                   c_ref, s_ref, cp_ref, sp_ref, kre_ref, kim_ref, pq_ref, *, length):
    @pl.when(pl.program_id(0) == 0)
    def _():
        fr = fr_ref[...]
        h = jnp.sin(fr * (_dot(z_ref[...], w1_ref[...], HI) + b1_ref[...]))
        h = jnp.sin(fr * (_dot(h, w2_ref[...], HI) + b2_ref[...]))
        h = _dot(h, w3_ref[...], HI) * dec_ref[...]
        hf = h[:, :HY_CH]
        t = lax.broadcasted_iota(jnp.int32, (length, HY_CH), 0)
        hb = jnp.where(t == 0, 0.0, h[:, HY_CH:])
        pq_ref[:, :HY_CH] = (hf + hb).astype(BF16)
        pq_ref[:, HY_CH:] = (hb - hf).astype(BF16)

    pq = pq_ref[...]
    cpq = _dot(c_ref[...], pq)
    spq = _dot(s_ref[...], pq)
    cphi, sphi = cp_ref[...], sp_ref[...]
    norm = 1.0 / length
    kre_ref[...] = (cphi * cpq[:, :HY_CH] + sphi * spq[:, :HY_CH]) * norm
    kim_ref[...] = (cphi * spq[:, HY_CH:] - sphi * cpq[:, HY_CH:]) * norm


def _hyena_filter_spectrum(length, w1, b1, w2, b2, w3, freq, dft):
    z, dec = _hyena_features(length)
    c_m, s_m, cphi, sphi = dft
    w1p = jnp.pad(w1, ((0, 128 - w1.shape[0]), (0, 0)))
    tf = min(512, length)
    const = lambda j: (0, 0)
    blk = lambda j: (j, 0)
    return pl.pallas_call(
        functools.partial(_hyfilt_kernel, length=length),
        grid=(length // tf,),
        in_specs=[
            pl.BlockSpec((length, 128), const),
            pl.BlockSpec((128, 64), const),
            pl.BlockSpec((1, 64), const),
            pl.BlockSpec((64, 64), const),
            pl.BlockSpec((1, 64), const),
            pl.BlockSpec((64, 2 * HY_CH), const),
            pl.BlockSpec((1, 64), const),
            pl.BlockSpec((length, 2 * HY_CH), const),
            pl.BlockSpec((tf, length), blk),
            pl.BlockSpec((tf, length), blk),
            pl.BlockSpec((tf, 1), blk),
            pl.BlockSpec((tf, 1), blk),
        ],
        out_specs=[pl.BlockSpec((tf, HY_CH), blk)] * 2,
        out_shape=[jax.ShapeDtypeStruct((length, HY_CH), F32)] * 2,
        scratch_shapes=[pltpu.VMEM((length, 2 * HY_CH), BF16)],
        compiler_params=_cp("arbitrary"),
        name=f"hyena_filter_{length}",
    )(z, w1p, b1[None, :], w2, b2[None, :], w3, freq[None, :], dec, c_m, s_m, cphi, sphi)


def _short_conv3(x_ref, cw):
    x = x_ref[...].astype(F32)
    return _shift_rows(x, -1) * cw[0:1] + x * cw[1:2] + _shift_rows(x, 1) * cw[2:3]


def _hyena_kernel(p0_ref, p1_ref, p2_ref, cw_ref, bias_ref, c_ref, s_ref, kre_ref, kim_ref, o_ref):
    cw = cw_ref[...]
    x0 = _short_conv3(p0_ref, cw[:, :HY_CH])
    k = _short_conv3(p1_ref, cw[:, HY_CH:2 * HY_CH]) * _short_conv3(p2_ref, cw[:, 2 * HY_CH:])
    kb = k.astype(BF16)
    a = _dot(c_ref[...], kb)
    b = _dot(s_ref[...], kb)
    kre, kim = kre_ref[...], kim_ref[...]
    yre = (a * kre + b * kim).astype(BF16)
    yim = (b * kre - a * kim).astype(BF16)
    y = _dot(c_ref[...], yre) + _dot(s_ref[...], yim)
    o_ref[...] = (x0 * (y + k * bias_ref[...])).astype(o_ref.dtype)


def _hyena(p_hy, length, blk0, conv_w, bias, dft, kre, kim):
    c_m, s_m = dft[0], dft[1]
    const = lambda b: (0, 0)
    once = pl.Buffered(1)
    in_specs = [
        pl.BlockSpec((length, HY_CH), lambda b: (blk0 + b, 0)),
        pl.BlockSpec((length, HY_CH), lambda b: (blk0 + b, 1)),
        pl.BlockSpec((length, HY_CH), lambda b: (blk0 + b, 2)),
        pl.BlockSpec((3, 768), const),
        pl.BlockSpec((1, HY_CH), const),
        pl.BlockSpec((length, length), const, pipeline_mode=once),
        pl.BlockSpec((length, length), const, pipeline_mode=once),
        pl.BlockSpec((length, HY_CH), const, pipeline_mode=once),
        pl.BlockSpec((length, HY_CH), const, pipeline_mode=once),
    ]
    return pl.pallas_call(
        _hyena_kernel,
        grid=(B,),
        in_specs=in_specs,
        out_specs=pl.BlockSpec((length, HY_CH), lambda b: (b, 0)),
        out_shape=jax.ShapeDtypeStruct((B * length, HY_CH), BF16),
        compiler_params=_cp("arbitrary"),
        name=f"hyena_{length}",
    )(p_hy, p_hy, p_hy, conv_w, bias[None, :], c_m, s_m, kre, kim)


def _softmax_pv(heads):
    probs, dens = [], []
    for parts, sink in heads:
        m = None
        for s, _ in parts:
            mi = jnp.max(s, axis=-1, keepdims=True)
            m = mi if m is None else jnp.maximum(m, mi)
        if sink is not None:
            m = jnp.maximum(m, sink)
        den = 0.0 if sink is None else jnp.exp(sink - m)
        es = []
        for s, _ in parts:
            e = jnp.exp(s - m)
            den = den + jnp.sum(e, axis=-1, keepdims=True)
            es.append(e.astype(BF16))
        probs.append(es)
        dens.append(den)
    outs = []
    for (parts, _), es in zip(heads, probs):
        acc = None
        for e, (_, v) in zip(es, parts):
            o = _dot(e, v)
            acc = o if acc is None else acc + o
        outs.append(acc)
    return [o / den for o, den in zip(outs, dens)]


def _wa_kernel(sink_ref, q_ref, k_ref, v_ref, kc_ref, vc_ref, o_ref):
    n = pl.program_id(1)
    start = pl.multiple_of(jnp.clip((n - 1) * 128, 0, S - 384), 128)
    kw = k_ref[pl.ds(start, 384), :]
    vw = v_ref[pl.ds(start, 384), :]
    kc, vc = kc_ref[...], vc_ref[...]
    q = q_ref[...]
    rr = lax.broadcasted_iota(jnp.int32, (256, 384), 0)
    qpos = n * 128 + jnp.where(rr >= 128, rr - 128, rr)
    kpos = start + lax.broadcasted_iota(jnp.int32, (256, 384), 1)
    valid = jnp.abs(qpos - kpos) <= WINDOW
    r1 = lax.broadcasted_iota(jnp.int32, (256, 1), 0)
    heads = []
    for hk in range(2):
        q2 = jnp.concatenate([q[:, (2 * hk) * HD:(2 * hk + 1) * HD],
                              q[:, (2 * hk + 1) * HD:(2 * hk + 2) * HD]], axis=0)
        hs = slice(hk * HD, (hk + 1) * HD)
        s_loc = jnp.where(valid, _dot_nt(q2, kw[:, hs]) * SCALE, NEG)
        s_ctx = _dot_nt(q2, kc[:, hs]) * SCALE
        sink = jnp.where(r1 >= 128, sink_ref[2 * hk + 1], sink_ref[2 * hk])
        heads.append(([(s_ctx, vc[:, hs]), (s_loc, vw[:, hs])], sink))
    for hk, o in enumerate(_softmax_pv(heads)):
        o_ref[:, (2 * hk) * HD:(2 * hk + 1) * HD] = o[:128].astype(o_ref.dtype)
        o_ref[:, (2 * hk + 1) * HD:(2 * hk + 2) * HD] = o[128:].astype(o_ref.dtype)


def _window_attention(p_wa, sink):
    nb = S // 128
    return pl.pallas_call(
        _wa_kernel,
        grid=(B, nb),
        in_specs=[
            pl.BlockSpec(memory_space=pltpu.SMEM),
            pl.BlockSpec((128, 256), lambda b, n: (b * nb + n, 0)),
            pl.BlockSpec((S, 128), lambda b, n: (b, 2)),
            pl.BlockSpec((S, 128), lambda b, n: (b, 3)),
            pl.BlockSpec((LC, 128), lambda b, n: (CTX_BLK0 + b, 2)),
            pl.BlockSpec((LC, 128), lambda b, n: (CTX_BLK0 + b, 3)),
        ],
        out_specs=pl.BlockSpec((128, 256), lambda b, n: (b * nb + n, 0)),
        out_shape=jax.ShapeDtypeStruct((NL, 256), BF16),
        compiler_params=_cp("arbitrary", "arbitrary"),
        name="window_attn",
    )(sink, p_wa, p_wa, p_wa, p_wa, p_wa)


def _na_kernel(q_ref, k_ref, v_ref, kc_ref, vc_ref, bias_ref, o_ref):
    kc, vc = kc_ref[...], vc_ref[...]
    heads = []
    for j in range(NA_ROWS):
        r = pl.program_id(1) * NA_ROWS + j
        first = jnp.clip(r - NA_KR // 2, 0, S // GW - NA_KR)
        start = pl.multiple_of(first * GW, GW)
        off = first - r + NA_KR - 1
        kw = k_ref[pl.ds(start, NA_KR * GW), :]
        vw = v_ref[pl.ds(start, NA_KR * GW), :]
        q = q_ref[j * GW:(j + 1) * GW, :]
        for h in range(4):
            hs = slice(h * HD, (h + 1) * HD)
            s_loc = _dot_nt(q[:, hs], kw[:, hs]) * SCALE + bias_ref[h, off]
            s_ctx = _dot_nt(q[:, hs], kc[:, hs]) * SCALE
            heads.append(([(s_ctx, vc[:, hs]), (s_loc, vw[:, hs])], None))
    for i, o in enumerate(_softmax_pv(heads)):
        j, h = divmod(i, 4)
        o_ref[j * GW:(j + 1) * GW, h * HD:(h + 1) * HD] = o.astype(o_ref.dtype)


def _na_bias_kernel(rpb_ref, sel_ref, o_ref):
    o_ref[0] = _dot(rpb_ref[0], sel_ref[...], HI)


def _na_bias_table(rpb):
    qc = jnp.arange(GW)[:, None]
    kcol = jnp.arange(GW)[None, :]
    ci = jnp.clip(kcol - qc + NA_KC - 1, 0, 2 * NA_KC - 2).reshape(1, GW * GW)
    sel = (jnp.arange(128)[:, None] == ci).astype(F32)
    n_ri = 2 * NA_KR - 1
    rpb_p = jnp.pad(rpb.astype(F32), ((0, 0), (0, 16 - n_ri), (0, 128 - (2 * NA_KC - 1))))
    per_row = pl.pallas_call(
        _na_bias_kernel,
        grid=(4,),
        in_specs=[pl.BlockSpec((1, 16, 128), lambda h: (h, 0, 0)), pl.BlockSpec((128, GW * GW), lambda h: (0, 0))],
        out_specs=pl.BlockSpec((1, 16, GW * GW), lambda h: (h, 0, 0)),
        out_shape=jax.ShapeDtypeStruct((4, 16, GW * GW), F32),
        compiler_params=_cp("arbitrary"),
        name="na_bias",
    )(rpb_p, sel).reshape(4, 16, GW, GW)
    tbl = jnp.stack([per_row[:, o:o + NA_KR] for o in range(NA_KR)], axis=1)
    tbl = jnp.transpose(tbl, (0, 1, 3, 2, 4))
    c_start = jnp.clip(qc - NA_KC // 2, 0, GW - NA_KC)
    ok = (kcol >= c_start) & (kcol < c_start + NA_KC)
    tbl = jnp.where(ok[None, None, :, None, :], tbl, NEG)
    return tbl.reshape(4, NA_KR, GW, NA_KR * GW)


def _neighborhood_attention(p_na, bias_tbl):
    steps = S // GW // NA_ROWS
    return pl.pallas_call(
        _na_kernel,
        grid=(B, steps),
        in_specs=[
            pl.BlockSpec((NA_ROWS * GW, 256), lambda b, r: (b * steps + r, 0)),
            pl.BlockSpec((S, 256), lambda b, r: (b, 1)),
            pl.BlockSpec((S, 256), lambda b, r: (b, 2)),
            pl.BlockSpec((LC, 256), lambda b, r: (CTX_BLK0 + b, 1)),
            pl.BlockSpec((LC, 256), lambda b, r: (CTX_BLK0 + b, 2)),
            pl.BlockSpec((4, NA_KR, GW, NA_KR * GW), lambda b, r: (0, 0, 0, 0)),
        ],
        out_specs=pl.BlockSpec((NA_ROWS * GW, 256), lambda b, r: (b * steps + r, 0)),
        out_shape=jax.ShapeDtypeStruct((NL, 256), BF16),
        compiler_params=_cp("arbitrary", "arbitrary"),
        name="neighborhood_attn",
    )(p_na, p_na, p_na, p_na, p_na, bias_tbl)


def _ctx_attn_kernel(sink_ref, qw_ref, kw_ref, vw_ref, qn_ref, kn_ref, vn_ref, owa_ref, ona_ref):
    q, k, v = qw_ref[...], kw_ref[...], vw_ref[...]
    r1 = lax.broadcasted_iota(jnp.int32, (2 * LC, 1), 0)
    heads = []
    for hk in range(2):
        q2 = jnp.concatenate([q[:, (2 * hk) * HD:(2 * hk + 1) * HD],
                              q[:, (2 * hk + 1) * HD:(2 * hk + 2) * HD]], axis=0)
        hs = slice(hk * HD, (hk + 1) * HD)
        sink = jnp.where(r1 >= LC, sink_ref[2 * hk + 1], sink_ref[2 * hk])
        heads.append(([(_dot_nt(q2, k[:, hs]) * SCALE, v[:, hs])], sink))
    for hk, o in enumerate(_softmax_pv(heads)):
        owa_ref[:, (2 * hk) * HD:(2 * hk + 1) * HD] = o[:LC].astype(owa_ref.dtype)
        owa_ref[:, (2 * hk + 1) * HD:(2 * hk + 2) * HD] = o[LC:].astype(owa_ref.dtype)
    q, k, v = qn_ref[...], kn_ref[...], vn_ref[...]
    heads = []
    for h in range(4):
        hs = slice(h * HD, (h + 1) * HD)
        heads.append(([(_dot_nt(q[:, hs], k[:, hs]) * SCALE, v[:, hs])], None))
    for h, o in enumerate(_softmax_pv(heads)):
        ona_ref[:, h * HD:(h + 1) * HD] = o.astype(ona_ref.dtype)


def _ctx_attention(p_wa, p_na, sink):
    blk = lambda c: (lambda b: (CTX_BLK0 + b, c))
    return pl.pallas_call(
        _ctx_attn_kernel,
        grid=(B,),
        in_specs=[
            pl.BlockSpec(memory_space=pltpu.SMEM),
            pl.BlockSpec((LC, 256), blk(0)),
            pl.BlockSpec((LC, 128), blk(2)),
            pl.BlockSpec((LC, 128), blk(3)),
            pl.BlockSpec((LC, 256), blk(0)),
            pl.BlockSpec((LC, 256), blk(1)),
            pl.BlockSpec((LC, 256), blk(2)),
        ],
        out_specs=[pl.BlockSpec((LC, 256), lambda b: (b, 0))] * 2,
        out_shape=[jax.ShapeDtypeStruct((NC, 256), BF16)] * 2,
        compiler_params=_cp("arbitrary"),
        name="ctx_attn",
    )(sink, p_wa, p_wa, p_wa, p_na, p_na, p_na)


def _gdn_prep_seq(p_ref, ab_ref, cw, alog, dtb, qkv_ref, gb_ref, row0, length):
    x = p_ref[:, :768].astype(F32)
    u = (_shift_rows(x, -2) * cw[0:1] + _shift_rows(x, -1) * cw[1:2] + x * cw[2:3]
         + _shift_rows(x, 1) * cw[3:4])
    u = _silu(u)
    rows = slice(row0, row0 + length)
    for j in range(8):
        sl = slice(j * HD, (j + 1) * HD)
        xs = u[:, sl]
        nrm = lax.rsqrt(jnp.sum(xs * xs, axis=-1, keepdims=True) + EPS)
        qkv_ref[0, rows, sl] = xs * (nrm * SCALE if j < 4 else nrm)
    qkv_ref[0, rows, 512:768] = u[:, 512:768]
    ab = ab_ref[...]
    g = -jnp.exp(alog) * _softplus(ab + dtb)
    lane = lax.broadcasted_iota(jnp.int32, ab.shape, 1)
    gb_ref[0, rows, :] = jnp.where(lane < 8, g, _sigmoid(ab))


def _gdn_prep_kernel(pl_ref, abl_ref, pc_ref, abc_ref, cw_ref, alog_ref, dtb_ref, qkv_ref, gb_ref):
    cw, alog, dtb = cw_ref[...], alog_ref[...], dtb_ref[...]
    _gdn_prep_seq(pl_ref, abl_ref, cw, alog, dtb, qkv_ref, gb_ref, 0, S)
    _gdn_prep_seq(pc_ref, abc_ref, cw, alog, dtb, qkv_ref, gb_ref, S, LC)


def _gdn_prep(p_gd, p_ab, conv_w, a_log, dt_bias):
    const = lambda b: (0, 0)
    alog = jnp.pad(a_log.reshape(1, 8), ((0, 0), (0, 120)))
    dtb = jnp.pad(dt_bias.reshape(1, 8), ((0, 0), (0, 120)))
    return pl.pallas_call(
        _gdn_prep_kernel,
        grid=(B,),
        in_specs=[
            pl.BlockSpec((S, 1024), lambda b: (b, 0)),
            pl.BlockSpec((S, 128), lambda b: (b, 0)),
            pl.BlockSpec((LC, 1024), lambda b: (CTX_BLK0 + b, 0)),
            pl.BlockSpec((LC, 128), lambda b: (CTX_BLK0 + b, 0)),
            pl.BlockSpec((4, 768), const),
            pl.BlockSpec((1, 128), const),
            pl.BlockSpec((1, 128), const),
        ],
        out_specs=[pl.BlockSpec((1, S + LC, 768), lambda b: (b, 0, 0)),
                   pl.BlockSpec((1, S + LC, 128), lambda b: (b, 0, 0))],
        out_shape=[jax.ShapeDtypeStruct((B, S + LC, 768), F32), jax.ShapeDtypeStruct((B, S + LC, 128), F32)],
        compiler_params=_cp("arbitrary"),
        name="gdn_prep",
    )(p_gd, p_ab, p_gd, p_ab, conv_w, alog, dtb)


def _gdn_chains(xs, gbs, cums, cum_ts, states, masks):
    chains = [(g, h) for g in range(len(xs)) for h in range(4)]
    pre = []
    for g, h in chains:
        dirn = g % 2
        x, gb = xs[g], gbs[g]
        tri, strict = masks[dirn]
        col = dirn * 4 + h
        qh = x[:, h * HD:(h + 1) * HD]
        kh = x[:, 256 + h * HD:256 + (h + 1) * HD]
        vh = x[:, 512 + h * HD:512 + (h + 1) * HD]
        gc = cums[g][:, col:col + 1]
        gc_row = cum_ts[g][col:col + 1, :]
        beta = gb[:, 8 + col:9 + col]
        decay = jnp.exp(jnp.where(tri, gc - gc_row, NEG))
        kb = kh * beta
        eg = jnp.exp(gc)
        g_last = gc[0:1, :] if dirn else gc[CHUNK - 1:CHUNK, :]
        pre.append(dict(strict=strict, decay=decay, kbf=kb.astype(BF16), khf=kh.astype(BF16), qf=qh.astype(BF16),
                        xw=jnp.concatenate([vh * beta, kb * eg], axis=1), qdec=(qh * eg).astype(BF16),
                        kdec=(kh * jnp.exp(g_last - gc)).astype(BF16), gl=jnp.exp(g_last)))
    n = len(chains)
    gram = [_dot_nt(p["kbf"], p["khf"]) for p in pre]
    attn = [_dot_nt(p["qf"], p["khf"]) for p in pre]
    lmat = [jnp.where(p["strict"], g * p["decay"], 0.0) for p, g in zip(pre, gram)]
    attn = [(a * p["decay"]).astype(BF16) for p, a in zip(pre, attn)]
    ii = lax.broadcasted_iota(jnp.int32, (CHUNK, CHUNK), 0)
    jj = lax.broadcasted_iota(jnp.int32, (CHUNK, CHUNK), 1)

    def merged_off_blocks(log_s):
        return ((ii >> (log_s + 1)) == (jj >> (log_s + 1))) & ((ii >> log_s) != (jj >> log_s))

    eye = (ii == jj).astype(F32)
    dinv = [eye - jnp.where(merged_off_blocks(0), lm, 0.0) for lm in lmat]
    for log_s in range(1, 6):
        m = merged_off_blocks(log_s)
        dinv_b = [d.astype(BF16) for d in dinv]
        ld = [_dot(jnp.where(m, lmat[c], 0.0).astype(BF16), dinv_b[c]) for c in range(n)]
        upd = [_dot(dinv_b[c], ld[c].astype(BF16)) for c in range(n)]
        dinv = [dinv[c] - upd[c] for c in range(n)]
    xw = [_dot(dinv[c].astype(BF16), pre[c]["xw"].astype(BF16)) for c in range(n)]
    sts = [states[g][h] for g, h in chains]
    stb = [s.astype(BF16) for s in sts]
    ws = [_dot(xw[c][:, HD:].astype(BF16), stb[c]) for c in range(n)]
    qs = [_dot(pre[c]["qdec"], stb[c]) for c in range(n)]
    u_new = [(xw[c][:, :HD] - ws[c]).astype(BF16) for c in range(n)]
    au = [_dot(attn[c], u_new[c]) for c in range(n)]
    ku = [_dot_tn(pre[c]["kdec"], u_new[c]) for c in range(n)]
    outs = [qs[c] + au[c] for c in range(n)]
    new_states = [sts[c] * pre[c]["gl"] + ku[c] for c in range(n)]
    return outs, new_states


def _gdn_kernel(x0_ref, gb0_ref, x1_ref, gb1_ref, o0_ref, o1_ref, state_ref):
    @pl.when(pl.program_id(0) == 0)
    def _():
        state_ref[...] = jnp.zeros_like(state_ref)

    ii = lax.broadcasted_iota(jnp.int32, (CHUNK, CHUNK), 0)
    jj = lax.broadcasted_iota(jnp.int32, (CHUNK, CHUNK), 1)
    masks = ((jj <= ii, jj < ii), (jj >= ii, jj > ii))
    ones_tri = ((jj <= ii).astype(F32), (jj >= ii).astype(F32))

    def body(bp, carry):
        bs = [bp * GDN_SEQS + i for i in range(GDN_SEQS)]
        states = [[state_ref[b, dirn, h] for h in range(4)] for b in bs for dirn in range(2)]
        xs = [r[b] for b in bs for r in (x0_ref, x1_ref)]
        gbs = [r[b] for b in bs for r in (gb0_ref, gb1_ref)]
        cums = [_dot(ones_tri[g % 2], gbs[g], HI) for g in range(2 * GDN_SEQS)]
        cum_ts = [cm.T for cm in cums]
        outs, new_states = _gdn_chains(xs, gbs, cums, cum_ts, states, masks)
        for c, st in enumerate(new_states):
            state_ref[bs[c // 8], (c // 4) % 2, c % 4] = st
        for i, b in enumerate(bs):
            o0_ref[b] = jnp.concatenate(outs[8 * i:8 * i + 4], axis=1)
            o1_ref[b] = jnp.concatenate(outs[8 * i + 4:8 * i + 8], axis=1)
        return carry

    lax.fori_loop(0, B // GDN_SEQS, body, 0)


def _gdn_scan(qkv, gb):
    nc = (S + LC) // CHUNK
    fwd = lambda c: (0, (c + S // CHUNK) % nc, 0)
    bwd = lambda c: (0, nc - 1 - c, 0)
    return pl.pallas_call(
        _gdn_kernel,
        grid=(nc,),
        in_specs=[
            pl.BlockSpec((B, CHUNK, 768), fwd),
            pl.BlockSpec((B, CHUNK, 128), fwd),
            pl.BlockSpec((B, CHUNK, 768), bwd),
            pl.BlockSpec((B, CHUNK, 128), bwd),
        ],
        out_specs=[pl.BlockSpec((B, CHUNK, 256), fwd), pl.BlockSpec((B, CHUNK, 256), bwd)],
        out_shape=[jax.ShapeDtypeStruct((B, S + LC, 256), F32)] * 2,
        scratch_shapes=[pltpu.VMEM((B, 2, 4, HD, HD), F32)],
        compiler_params=_cp("arbitrary"),
        name="gdn_scan",
    )(qkv, gb, qkv, gb)


def _gdn_finish_kernel(o0_ref, o1_ref, gate_ref, ng_ref, y_ref):
    o = o0_ref[0] + o1_ref[0]
    gi = lax.broadcasted_iota(jnp.int32, (256, 256), 0) // HD
    gj = lax.broadcasted_iota(jnp.int32, (256, 256), 1) // HD
    group_mean = jnp.where(gi == gj, 1.0 / HD, 0.0)
    ms = _dot(o * o, group_mean, HI)
    y = o * lax.rsqrt(ms + EPS) * ng_ref[...]
    y_ref[...] = (y * _silu(gate_ref[...].astype(F32))).astype(y_ref.dtype)


def _gdn_finish(o0, o1, p_gd, norm_g, need_ctx):
    lat_tiles = S // TM
    row_blk = lambda b, j: jnp.where(j < lat_tiles, b * lat_tiles + j, CTX_BLK0 + b)
    return pl.pallas_call(
        _gdn_finish_kernel,
        grid=(B, lat_tiles + (1 if need_ctx else 0)),
        in_specs=[
            pl.BlockSpec((1, TM, 256), lambda b, j: (b, j, 0)),
            pl.BlockSpec((1, TM, 256), lambda b, j: (b, j, 0)),
            pl.BlockSpec((TM, 256), lambda b, j: (row_blk(b, j), 3)),
            pl.BlockSpec((1, 256), lambda b, j: (0, 0)),
        ],
        out_specs=pl.BlockSpec((TM, 256), lambda b, j: (row_blk(b, j), 0)),
        out_shape=jax.ShapeDtypeStruct((ROWS if need_ctx else NL, 256), BF16),
        compiler_params=_cp("arbitrary", "arbitrary"),
        name="gdn_finish",
    )(o0, o1, p_gd, jnp.tile(norm_g, 4)[None, :])


def _pack_bf16_pairs(x):
    n = x.shape[1] // 2
    lo = pltpu.bitcast(x[:, :n].astype(BF16).astype(F32), jnp.uint32)
    hi = pltpu.bitcast(x[:, n:].astype(BF16).astype(F32), jnp.uint32)
    return hi | (lo >> 16)


def _unpack_bf16_pairs(w):
    lo = pltpu.bitcast(w << 16, F32)
    hi = pltpu.bitcast(w & jnp.uint32(0xFFFF0000), F32)
    return jnp.concatenate([lo, hi], axis=1)


def _outproj_kernel(x_ref, mod_ref, g2_ref, yh_ref, yw_ref, yn_ref, yg_ref, wo_ref, wr_ref, eb_ref,
                    xo_ref, h2_ref, h2p_ref, idx_ref, tw_ref, rank_ref, cnt_ref, carry_ref):
    @pl.when(pl.program_id(0) == 0)
    def _():
        carry_ref[...] = jnp.zeros_like(carry_ref)

    m = mod_ref[0]
    acc = (_dot(yh_ref[...], wo_ref[0:256, :]) + _dot(yw_ref[...], wo_ref[256:512, :])
           + _dot(yn_ref[...], wo_ref[512:768, :]) + _dot(yg_ref[...], wo_ref[768:1024, :]))
    x = x_ref[...] + m[2:3] * acc
    xo_ref[...] = x
    h2 = _modulated_norm(x, g2_ref[...], m[3:4], m[4:5])
    h2_ref[...] = h2.astype(h2_ref.dtype)
    h2p_ref[...] = _pack_bf16_pairs(h2)
    scores = _sigmoid(_dot(h2, wr_ref[...], HI))
    lane = lax.broadcasted_iota(jnp.int32, scores.shape, 1)
    sel = jnp.where(lane < N_EXP, scores + eb_ref[...], -jnp.inf)
    lane_f = lane.astype(F32)
    idx_out = jnp.zeros(scores.shape, F32)
    s_out = jnp.zeros(scores.shape, F32)
    hits = []
    for kk in range(TOP_K):
        mx = jnp.max(sel, axis=-1, keepdims=True)
        idx = jnp.min(jnp.where(sel == mx, lane_f, 128.0), axis=-1, keepdims=True)
        hit = lane_f == idx
        sk = jnp.sum(jnp.where(hit, scores, 0.0), axis=-1, keepdims=True)
        sel = jnp.where(hit, -jnp.inf, sel)
        idx_out = jnp.where(lane == kk, idx, idx_out)
        s_out = jnp.where(lane == kk, sk, s_out)
        hits.append(hit)
    tot = jnp.sum(s_out, axis=-1, keepdims=True)
    idx_ref[...] = idx_out.astype(jnp.int32)
    tw_ref[...] = s_out / tot * ROUTED_SCALE
    cnt = jnp.zeros(scores.shape, F32)
    for hit in hits:
        cnt = cnt + jnp.where(hit, 1.0, 0.0)
    ri = lax.broadcasted_iota(jnp.int32, (TM, TM), 0)
    rj = lax.broadcasted_iota(jnp.int32, (TM, TM), 1)
    before = _dot(jnp.where(rj < ri, 1.0, 0.0).astype(BF16), cnt.astype(BF16)) + carry_ref[...]
    rank_out = jnp.zeros(scores.shape, F32)
    for kk, hit in enumerate(hits):
        rk = jnp.sum(jnp.where(hit, before, 0.0), axis=-1, keepdims=True)
        rank_out = jnp.where(lane == kk, rk, rank_out)
    rank_ref[...] = rank_out.astype(jnp.int32)
    carry_ref[...] = carry_ref[...] + jnp.sum(cnt, axis=0, keepdims=True)
    cnt_ref[...] = carry_ref[...].astype(jnp.int32)


def _outproj(xa, mod, l, g2, ys, wo, wr, eb, n_tiles):
    row = lambda i: (i, 0)
    const = lambda i: (0, 0)
    rows = n_tiles * TM
    return pl.pallas_call(
        _outproj_kernel,
        grid=(n_tiles,),
        in_specs=[
            pl.BlockSpec((TM, D), row),
            pl.BlockSpec((1, 6, D), lambda i: (l * 16 + (i * TM) // S, 0, 0)),
            pl.BlockSpec((1, D), const),
            pl.BlockSpec((TM, 256), row),
            pl.BlockSpec((TM, 256), row),
            pl.BlockSpec((TM, 256), row),
            pl.BlockSpec((TM, 256), row),
            pl.BlockSpec((D, D), const),
            pl.BlockSpec((D, 128), const),
            pl.BlockSpec((1, 128), const),
        ],
        out_specs=[
            pl.BlockSpec((TM, D), row),
            pl.BlockSpec((TM, D), row),
            pl.BlockSpec((TM, D // 2), row),
            pl.BlockSpec((TM, 128), row),
            pl.BlockSpec((TM, 128), row),
            pl.BlockSpec((TM, 128), row),
            pl.BlockSpec((1, 128), const),
        ],
        out_shape=[
            jax.ShapeDtypeStruct((rows, D), F32),
            jax.ShapeDtypeStruct((rows, D), BF16),
            jax.ShapeDtypeStruct((rows, D // 2), jnp.uint32),
            jax.ShapeDtypeStruct((rows, 128), jnp.int32),
            jax.ShapeDtypeStruct((rows, 128), F32),
            jax.ShapeDtypeStruct((rows, 128), jnp.int32),
            jax.ShapeDtypeStruct((1, 128), jnp.int32),
        ],
        scratch_shapes=[pltpu.VMEM((1, 128), F32)],
        compiler_params=_cp("arbitrary"),
        name="outproj_router",
    )(xa, mod, g2, *ys, wo, wr, eb)


ASG_TILE = TM * TOP_K


def _segment_tables(counts, n_blocks):
    counts = counts[0, :N_EXP]
    padded = (counts + MOE_BLK - 1) // MOE_BLK * MOE_BLK
    pad_end = jnp.cumsum(padded)
    starts = jnp.arange(n_blocks, dtype=jnp.int32) * MOE_BLK
    blk_e = jnp.minimum(jnp.sum((pad_end[None, :] <= starts[:, None]).astype(jnp.int32), axis=1), N_EXP - 1)
    blk_e = jnp.concatenate([blk_e, pad_end[-1:] // MOE_BLK])
    return (pad_end - padded).astype(jnp.int32), blk_e.astype(jnp.int32)


def _row_copy_wait(shape_ref, dst_ref, sem):
    pltpu.make_async_copy(shape_ref, dst_ref, sem).wait()


def _dispatch_kernel(seg_ref, idx_ref, rank_ref, h2p_ref, xb_in_ref, xb_ref, sem):
    del xb_in_ref

    def issue(t, carry):
        src = h2p_ref.at[pl.ds(t, 1)]
        for k in range(TOP_K):
            a = t * TOP_K + k
            dst = seg_ref[idx_ref[a]] + rank_ref[a]
            pltpu.make_async_copy(src, xb_ref.at[pl.ds(dst, 1)], sem).start()
        return carry

    lax.fori_loop(0, TM, issue, 0, unroll=True)
    for _ in range(TOP_K):
        _row_copy_wait(h2p_ref, xb_ref.at[pl.ds(0, TM)], sem)


def _dispatch_rows(seg_start, idx_c, rank_c, h2p, n_rows):
    n_tiles = h2p.shape[0] // TM
    grid_spec = pltpu.PrefetchScalarGridSpec(
        num_scalar_prefetch=1,
        grid=(n_tiles,),
        in_specs=[
            pl.BlockSpec((ASG_TILE,), lambda i, seg: (i,), memory_space=pltpu.SMEM),
            pl.BlockSpec((ASG_TILE,), lambda i, seg: (i,), memory_space=pltpu.SMEM),
            pl.BlockSpec((TM, D // 2), lambda i, seg: (i, 0)),
            pl.BlockSpec(memory_space=pl.ANY),
        ],
        out_specs=pl.BlockSpec(memory_space=pl.ANY),
        scratch_shapes=[pltpu.SemaphoreType.DMA(())],
    )
    return pl.pallas_call(
        _dispatch_kernel,
        grid_spec=grid_spec,
        out_shape=jax.ShapeDtypeStruct((n_rows, D // 2), jnp.uint32),
        input_output_aliases={4: 0},
        compiler_params=_cp("arbitrary"),
        name="moe_dispatch",
    )(seg_start, idx_c, rank_c, h2p, jnp.zeros((n_rows, D // 2), jnp.uint32))


def _moe_kernel(be_ref, x_ref, wg_ref, wu_ref, wd_ref, o_ref, wgb_ref, wub_ref, wdb_ref):
    i = pl.program_id(0)
    in_use = i < be_ref[pl.num_programs(0)]

    @pl.when(in_use & ((i == 0) | (be_ref[i] != be_ref[jnp.maximum(i - 1, 0)])))
    def _():
        wgb_ref[...] = wg_ref[0].astype(BF16)
        wub_ref[...] = wu_ref[0].astype(BF16)
        wdb_ref[...] = wd_ref[0].astype(BF16)

    @pl.when(in_use)
    def _():
        x = _unpack_bf16_pairs(x_ref[...]).astype(BF16)
        g = _dot(x, wgb_ref[...])
        u = _dot(x, wub_ref[...])
        hid = (_silu(g) * u).astype(BF16)
        o_ref[...] = _pack_bf16_pairs(_dot(hid, wdb_ref[...]))

    @pl.when(jnp.logical_not(in_use))
    def _():
        o_ref[...] = jnp.zeros_like(o_ref)


def _moe_experts(blk_e, xb, wg, wu, wd):
    n_rows = xb.shape[0]
    grid_spec = pltpu.PrefetchScalarGridSpec(
        num_scalar_prefetch=1,
        grid=(n_rows // MOE_BLK,),
        in_specs=[
            pl.BlockSpec((MOE_BLK, D // 2), lambda i, be: (i, 0)),
            pl.BlockSpec((1, D, D_EXP), lambda i, be: (be[i], 0, 0)),
            pl.BlockSpec((1, D, D_EXP), lambda i, be: (be[i], 0, 0)),
            pl.BlockSpec((1, D_EXP, D), lambda i, be: (be[i], 0, 0)),
        ],
        out_specs=pl.BlockSpec((MOE_BLK, D // 2), lambda i, be: (i, 0)),
        scratch_shapes=[pltpu.VMEM((D, D_EXP), BF16), pltpu.VMEM((D, D_EXP), BF16), pltpu.VMEM((D_EXP, D), BF16)],
    )
    return pl.pallas_call(
        _moe_kernel,
        grid_spec=grid_spec,
        out_shape=jax.ShapeDtypeStruct((n_rows, D // 2), jnp.uint32),
        compiler_params=_cp("arbitrary"),
        name="moe_experts",
    )(blk_e, xb, wg, wu, wd)


def _ffn_out_kernel(seg_ref, idx_ref, rank_ref, x_ref, mod_ref, h2_ref, tw_ref, wg_ref, wu_ref, wd_ref, nf_ref,
                    yb_ref, o_ref, gat_ref, sem, *, final):
    def issue(t, carry):
        for k in range(TOP_K):
            a = t * TOP_K + k
            src = seg_ref[idx_ref[a]] + rank_ref[a]
            pltpu.make_async_copy(yb_ref.at[pl.ds(src, 1)], gat_ref.at[pl.ds(k * TM + t, 1)], sem).start()
        return carry

    lax.fori_loop(0, TM, issue, 0, unroll=True)
    m = mod_ref[0]
    h2 = h2_ref[...]
    hid = (_silu(_dot(h2, wg_ref[...])) * _dot(h2, wu_ref[...])).astype(BF16)
    shared = _dot(hid, wd_ref[...])
    for _ in range(TOP_K):
        _row_copy_wait(yb_ref.at[pl.ds(0, TM)], gat_ref.at[pl.ds(0, TM)], sem)
    tw = tw_ref[...]
    routed = _unpack_bf16_pairs(gat_ref[0:TM, :]) * tw[:, 0:1]
    for k in range(1, TOP_K):
        routed = routed + _unpack_bf16_pairs(gat_ref[k * TM:(k + 1) * TM, :]) * tw[:, k:k + 1]
    x = x_ref[...] + m[5:6] * (routed + shared)
    if final:
        ms = jnp.mean(x * x, axis=-1, keepdims=True)
        x = x * lax.rsqrt(ms + EPS) * nf_ref[...]
    o_ref[...] = x


def _ffn_out(seg_start, idx_c, rank_c, xn, mod, l, h2, top_w, wg, wu, wd, norm_f, yb, n_tiles, final):
    row = lambda i, seg: (i, 0)
    const = lambda i, seg: (0, 0)
    grid_spec = pltpu.PrefetchScalarGridSpec(
        num_scalar_prefetch=1,
        grid=(n_tiles,),
        in_specs=[
            pl.BlockSpec((ASG_TILE,), lambda i, seg: (i,), memory_space=pltpu.SMEM),
            pl.BlockSpec((ASG_TILE,), lambda i, seg: (i,), memory_space=pltpu.SMEM),
            pl.BlockSpec((TM, D), row),
            pl.BlockSpec((1, 6, D), lambda i, seg: (l * 16 + (i * TM) // S, 0, 0)),
            pl.BlockSpec((TM, D), row),
            pl.BlockSpec((TM, 128), row),
            pl.BlockSpec((D, D_EXP), const),
            pl.BlockSpec((D, D_EXP), const),
            pl.BlockSpec((D_EXP, D), const),
            pl.BlockSpec((1, D), const),
            pl.BlockSpec(memory_space=pl.ANY),
        ],
        out_specs=pl.BlockSpec((TM, D), row),
        scratch_shapes=[pltpu.VMEM((ASG_TILE, D // 2), jnp.uint32), pltpu.SemaphoreType.DMA(())],
    )
    return pl.pallas_call(
        functools.partial(_ffn_out_kernel, final=final),
        grid_spec=grid_spec,
        out_shape=jax.ShapeDtypeStruct((n_tiles * TM, D), F32),
        compiler_params=_cp("arbitrary"),
        name="shared_ffn_residual",
    )(seg_start, idx_c, rank_c, xn, mod, h2, top_w, wg, wu, wd, norm_f, yb)


def kernel(x, c, ctx, c_ctx, w_ada, b_ada, norm1, norm2, norm_f, w_in, w_out, hy_conv, hy_w1, hy_b1, hy_w2, hy_b2, hy_w3, hy_freq, hy_bias, wa_sink, na_rpb, gdn_conv, gdn_a_log, gdn_dt_bias, gdn_norm, moe_router, moe_bias, moe_gate, moe_up, moe_down, sh_gate, sh_up, sh_down):
    depth = w_ada.shape[0]
    xa = jnp.concatenate([x.reshape(NL, D), ctx.reshape(NC, D)], axis=0)
    cvec = jnp.concatenate([c, c_ctx[None, :], jnp.zeros((16 - B - 1, D), F32)], axis=0)
    mod = _ada(cvec, w_ada, b_ada).reshape(depth * 16, 6, D)
    rope = _rope_tables()
    dft_lat = _dft_tables(S)
    dft_ctx = _dft_tables(LC)
    o1, o2, o3 = 768, 768 + 512, 768 + 512 + 768

    for l in range(depth):
        need_ctx = l < depth - 1
        n_tiles = NT_ALL if need_ctx else NT_LAT
        wl = w_in[l].astype(BF16)
        ws = (wl[:, :o1], wl[:, o1:o2], wl[:, o2:o3], wl[:, o3:o3 + 1024],
              jnp.pad(wl[:, o3 + 1024:], ((0, 0), (0, 128 - 16))))
        p_hy, p_wa, p_na, p_gd, p_ab = _inproj(xa, mod, l, norm1[l][None, :], ws, rope)

        filt = (hy_w1[l], hy_b1[l], hy_w2[l], hy_b2[l], hy_w3[l], hy_freq[l])
        kre, kim = _hyena_filter_spectrum(S, *filt, dft_lat)
        y_hy = _hyena(p_hy, S, 0, hy_conv[l], hy_bias[l], dft_lat, kre, kim)
        y_wa = _window_attention(p_wa, wa_sink[l])
        y_na = _neighborhood_attention(p_na, _na_bias_table(na_rpb[l]))
        if need_ctx:
            kre_c, kim_c = _hyena_filter_spectrum(LC, *filt, dft_ctx)
            yc_hy = _hyena(p_hy, LC, CTX_BLK0, hy_conv[l], hy_bias[l], dft_ctx, kre_c, kim_c)
            yc_wa, yc_na = _ctx_attention(p_wa, p_na, wa_sink[l])
            y_hy = jnp.concatenate([y_hy, yc_hy], axis=0)
            y_wa = jnp.concatenate([y_wa, yc_wa], axis=0)
            y_na = jnp.concatenate([y_na, yc_na], axis=0)
        qkv, gb = _gdn_prep(p_gd, p_ab, gdn_conv[l], gdn_a_log[l], gdn_dt_bias[l])
        y_gd = _gdn_finish(*_gdn_scan(qkv, gb), p_gd, gdn_norm[l], need_ctx)

        wr = jnp.pad(moe_router[l], ((0, 0), (0, 128 - N_EXP)))
        eb = jnp.pad(moe_bias[l], (0, 128 - N_EXP))[None, :]
        xn, h2, h2p, top_idx, top_w, rank, counts = _outproj(
            xa, mod, l, norm2[l][None, :], (y_hy, y_wa, y_na, y_gd), w_out[l].astype(BF16), wr, eb, n_tiles)
        n_asg = n_tiles * ASG_TILE
        n_blocks = n_asg // MOE_BLK + N_EXP
        seg_start, blk_e = _segment_tables(counts, n_blocks)
        idx_c = top_idx[:, :TOP_K].reshape(n_asg)
        rank_c = rank[:, :TOP_K].reshape(n_asg)
        xb = _dispatch_rows(seg_start, idx_c, rank_c, h2p, n_blocks * MOE_BLK)
        yb = _moe_experts(blk_e, xb, moe_gate[l], moe_up[l], moe_down[l])
        xa_new = _ffn_out(seg_start, idx_c, rank_c, xn, mod, l, h2, top_w, sh_gate[l].astype(BF16),
                          sh_up[l].astype(BF16), sh_down[l].astype(BF16), norm_f[None, :], yb, n_tiles,
                          final=not need_ctx)
        if need_ctx:
            xa = xa_new
        else:
            return xa_new.reshape(B, S, D)
```

```python
import functools
import math

import jax
import jax.numpy as jnp
from jax import lax
from jax.experimental import pallas as pl
from jax.experimental.pallas import tpu as pltpu

F32 = jnp.float32
BF16 = jnp.bfloat16
HI = lax.Precision.HIGHEST

D = 1024
B = 8
S = 2048
LC = 256
GW = 64
HD = 64
NL = B * S
NC = B * LC
ROWS = NL + NC
TM = 256
NT_LAT = NL // TM
NT_ALL = ROWS // TM
CTX_BLK0 = NL // LC

HY_CH = 256
HY_BANDS = 16
HY_DECAY_MIN = -math.log(1e-2) / 1.5
HY_DECAY_MAX = -math.log(1e-2) / 0.3
WINDOW = 128
NA_KR = 8
NA_KC = 16
CHUNK = 64
N_EXP = 32
TOP_K = 4
D_EXP = 256
ROUTED_SCALE = 2.5
MOE_BLK = 512
WA_BLKS = 2
NA_ROWS = 4
GDN_SEQS = 4
EPS = 1e-6
NEG = -1e30
SCALE = HD ** -0.5
VMEM_LIMIT = 56 * 1024 * 1024


def _cp(*sem):
    return pltpu.CompilerParams(dimension_semantics=tuple(sem), vmem_limit_bytes=VMEM_LIMIT)


def _dot(a, b, precision=None):
    return jnp.dot(a, b, preferred_element_type=F32, precision=precision)


def _dot_nt(a, b, precision=None):
    return lax.dot_general(a, b, (((1,), (1,)), ((), ())), preferred_element_type=F32, precision=precision)


def _dot_tn(a, b, precision=None):
    return lax.dot_general(a, b, (((0,), (0,)), ((), ())), preferred_element_type=F32, precision=precision)


def _sigmoid(x):
    return 1.0 / (1.0 + jnp.exp(-x))


def _silu(x):
    return x * _sigmoid(x)


def _softplus(x):
    return jnp.maximum(x, 0.0) + jnp.log(1.0 + jnp.exp(-jnp.abs(x)))


def _shift_rows(x, d):
    n = x.shape[0]
    if d == 0:
        return x
    y = pltpu.roll(x, (-d) % n, axis=0)
    t = lax.broadcasted_iota(jnp.int32, x.shape, 0)
    ok = (t + d >= 0) & (t + d < n)
    return jnp.where(ok, y, 0.0)


def _ada_kernel(c_ref, w_ref, b_ref, o_ref):
    s = _silu(c_ref[...])
    o_ref[0] = _dot(s.astype(BF16), w_ref[0].astype(BF16)) + b_ref[0]


def _ada(cvec, w_ada, b_ada):
    nl = w_ada.shape[0]
    tn = 1536
    return pl.pallas_call(
        _ada_kernel,
        grid=(nl, 6 * D // tn),
        in_specs=[
            pl.BlockSpec((16, D), lambda l, j: (0, 0)),
            pl.BlockSpec((1, D, tn), lambda l, j: (l, 0, j)),
            pl.BlockSpec((1, 1, tn), lambda l, j: (l, 0, j)),
        ],
        out_specs=pl.BlockSpec((1, 16, tn), lambda l, j: (l, 0, j)),
        out_shape=jax.ShapeDtypeStruct((nl, 16, 6 * D), F32),
        compiler_params=_cp("arbitrary", "arbitrary"),
        name="adaln",
    )(cvec, w_ada, b_ada.reshape(nl, 1, 6 * D))


def _modulated_norm(x, g, shift, scale):
    ms = jnp.mean(x * x, axis=-1, keepdims=True)
    y = x * lax.rsqrt(ms + EPS) * g
    return y * (1.0 + scale) + shift


def _inproj_kernel(x_ref, mod_ref, g_ref, why_ref, wwa_ref, wna_ref, wgd_ref, wab_ref,
                   cos_ref, sa_ref, sb_ref, ohy, owa, ona, ogd, oab):
    m = mod_ref[0]
    h = _modulated_norm(x_ref[...], g_ref[...], m[0:1], m[1:2]).astype(BF16)
    ohy[...] = _dot(h, why_ref[...]).astype(ohy.dtype)
    ona[...] = _dot(h, wna_ref[...]).astype(ona.dtype)
    ogd[...] = _dot(h, wgd_ref[...]).astype(ogd.dtype)
    oab[...] = _dot(h, wab_ref[...])
    a = _dot(h, wwa_ref[...])
    for c in range(4):
        sl = slice(c * 128, (c + 1) * 128)
        ac = a[:, sl]
        r = (ac * cos_ref[:, sl] + pltpu.roll(ac, 112, axis=1) * sa_ref[:, sl]
             + pltpu.roll(ac, 16, axis=1) * sb_ref[:, sl])
        owa[:, sl] = r.astype(owa.dtype)


def _inproj(xa, mod, l, g, ws, tabs):
    why, wwa, wna, wgd, wab = ws
    cos_t, sa_t, sb_t = tabs
    row = lambda i: (i, 0)
    const = lambda i: (0, 0)
    tab = lambda i: (jnp.where(i < NT_LAT, i % (S // TM), S // TM), 0)
    return pl.pallas_call(
        _inproj_kernel,
        grid=(NT_ALL,),
        in_specs=[
            pl.BlockSpec((TM, D), row),
            pl.BlockSpec((1, 6, D), lambda i: (l * 16 + (i * TM) // S, 0, 0)),
            pl.BlockSpec((1, D), const),
            pl.BlockSpec((D, 768), const),
            pl.BlockSpec((D, 512), const),
            pl.BlockSpec((D, 768), const),
            pl.BlockSpec((D, 1024), const),
            pl.BlockSpec((D, 128), const),
            pl.BlockSpec((TM, 512), tab),
            pl.BlockSpec((TM, 512), tab),
            pl.BlockSpec((TM, 512), tab),
        ],
        out_specs=[
            pl.BlockSpec((TM, 768), row),
            pl.BlockSpec((TM, 512), row),
            pl.BlockSpec((TM, 768), row),
            pl.BlockSpec((TM, 1024), row),
            pl.BlockSpec((TM, 128), row),
        ],
        out_shape=[
            jax.ShapeDtypeStruct((ROWS, 768), BF16),
            jax.ShapeDtypeStruct((ROWS, 512), BF16),
            jax.ShapeDtypeStruct((ROWS, 768), BF16),
            jax.ShapeDtypeStruct((ROWS, 1024), BF16),
            jax.ShapeDtypeStruct((ROWS, 128), F32),
        ],
        compiler_params=_cp("arbitrary"),
        name="inproj",
    )(xa, mod, g, why, wwa, wna, wgd, wab, cos_t, sa_t, sb_t)


def _rope_tables():
    quarter = HD // 4
    pos = jnp.arange(S)
    inv = 10000.0 ** (-jnp.arange(quarter, dtype=F32) / quarter)
    ang_r = (pos // GW).astype(F32)[:, None] * inv[None, :]
    ang_c = (pos % GW).astype(F32)[:, None] * inv[None, :]
    z = jnp.zeros_like(ang_r)
    cos_h = jnp.concatenate([jnp.cos(ang_r)] * 2 + [jnp.cos(ang_c)] * 2, axis=1)
    sa_h = jnp.concatenate([-jnp.sin(ang_r), z, -jnp.sin(ang_c), z], axis=1)
    sb_h = jnp.concatenate([z, jnp.sin(ang_r), z, jnp.sin(ang_c)], axis=1)
    ones = jnp.ones((S, 128), F32)
    zeros = jnp.zeros((S, 128), F32)
    cos_t = jnp.concatenate([jnp.tile(cos_h, (1, 6)), ones], axis=1)
    sa_t = jnp.concatenate([jnp.tile(sa_h, (1, 6)), zeros], axis=1)
    sb_t = jnp.concatenate([jnp.tile(sb_h, (1, 6)), zeros], axis=1)
    ident = jnp.ones((TM, 512), F32)
    none = jnp.zeros((TM, 512), F32)
    return (jnp.concatenate([cos_t, ident], axis=0), jnp.concatenate([sa_t, none], axis=0),
            jnp.concatenate([sb_t, none], axis=0))


def _dft_tables(length):
    n = 2 * length
    f = jnp.arange(length, dtype=jnp.int32)
    m = ((2 * f[:, None] + 1) * (2 * f[None, :] + 1)) % (4 * n)
    th = m.astype(F32) * (2.0 * math.pi / (4 * n))
    phi = (2 * f + 1).astype(F32) * (math.pi / (2 * n))
    return (jnp.cos(th).astype(BF16), jnp.sin(th).astype(BF16),
            jnp.cos(phi)[:, None], jnp.sin(phi)[:, None])


def _hyena_features(length):
    t = jnp.arange(length, dtype=F32)
    t_norm = t / max(length - 1, 1)
    bands = jnp.linspace(1e-4, HY_BANDS - 1, HY_BANDS, dtype=F32)
    ang = (2.0 * math.pi / length) * t[:, None] * bands[None, :]
    z = jnp.concatenate([t_norm[:, None], jnp.cos(ang), -jnp.sin(ang)], axis=-1)
    z = jnp.pad(z, ((0, 0), (0, 128 - z.shape[1])))
    decay = jnp.tile(jnp.linspace(HY_DECAY_MIN, HY_DECAY_MAX, HY_CH, dtype=F32), 2)
    return z, jnp.exp(-t_norm[:, None] * decay[None, :])


def _hyfilt_kernel(z_ref, w1_ref, b1_ref, w2_ref, b2_ref, w3_ref, fr_ref, dec_ref,
                   c_ref, s_ref, cp_ref, sp_ref, kre_ref, kim_ref, pq_ref, *, length):
    @pl.when(pl.program_id(0) == 0)
    def _():
        fr = fr_ref[...]
        h = jnp.sin(fr * (_dot(z_ref[...], w1_ref[...], HI) + b1_ref[...]))
        h = jnp.sin(fr * (_dot(h, w2_ref[...], HI) + b2_ref[...]))
        h = _dot(h, w3_ref[...], HI) * dec_ref[...]
        hf = h[:, :HY_CH]
        t = lax.broadcasted_iota(jnp.int32, (length, HY_CH), 0)
        hb = jnp.where(t == 0, 0.0, h[:, HY_CH:])
        pq_ref[:, :HY_CH] = (hf + hb).astype(BF16)
        pq_ref[:, HY_CH:] = (hb - hf).astype(BF16)

    pq = pq_ref[...]
    cpq = _dot(c_ref[...], pq)
    spq = _dot(s_ref[...], pq)
    cphi, sphi = cp_ref[...], sp_ref[...]
    norm = 1.0 / length
    kre_ref[...] = (cphi * cpq[:, :HY_CH] + sphi * spq[:, :HY_CH]) * norm
    kim_ref[...] = (cphi * spq[:, HY_CH:] - sphi * cpq[:, HY_CH:]) * norm


def _hyena_filter_spectrum(length, w1, b1, w2, b2, w3, freq, dft):
    z, dec = _hyena_features(length)
    c_m, s_m, cphi, sphi = dft
    w1p = jnp.pad(w1, ((0, 128 - w1.shape[0]), (0, 0)))
    tf = min(512, length)
    const = lambda j: (0, 0)
    blk = lambda j: (j, 0)
    return pl.pallas_call(
        functools.partial(_hyfilt_kernel, length=length),
        grid=(length // tf,),
        in_specs=[
            pl.BlockSpec((length, 128), const),
            pl.BlockSpec((128, 64), const),
            pl.BlockSpec((1, 64), const),
            pl.BlockSpec((64, 64), const),
            pl.BlockSpec((1, 64), const),
            pl.BlockSpec((64, 2 * HY_CH), const),
            pl.BlockSpec((1, 64), const),
            pl.BlockSpec((length, 2 * HY_CH), const),
            pl.BlockSpec((tf, length), blk),
            pl.BlockSpec((tf, length), blk),
            pl.BlockSpec((tf, 1), blk),
            pl.BlockSpec((tf, 1), blk),
        ],
        out_specs=[pl.BlockSpec((tf, HY_CH), blk)] * 2,
        out_shape=[jax.ShapeDtypeStruct((length, HY_CH), F32)] * 2,
        scratch_shapes=[pltpu.VMEM((length, 2 * HY_CH), BF16)],
        compiler_params=_cp("arbitrary"),
        name=f"hyena_filter_{length}",
    )(z, w1p, b1[None, :], w2, b2[None, :], w3, freq[None, :], dec, c_m, s_m, cphi, sphi)


def _short_conv3(x_ref, cw):
    x = x_ref[...].astype(F32)
    return _shift_rows(x, -1) * cw[0:1] + x * cw[1:2] + _shift_rows(x, 1) * cw[2:3]


def _hyena_kernel(p0_ref, p1_ref, p2_ref, cw_ref, bias_ref, c_ref, s_ref, kre_ref, kim_ref, o_ref):
    cw = cw_ref[...]
    x0 = _short_conv3(p0_ref, cw[:, :HY_CH])
    k = _short_conv3(p1_ref, cw[:, HY_CH:2 * HY_CH]) * _short_conv3(p2_ref, cw[:, 2 * HY_CH:])
    kb = k.astype(BF16)
    a = _dot(c_ref[...], kb)
    b = _dot(s_ref[...], kb)
    kre, kim = kre_ref[...], kim_ref[...]
    yre = (a * kre + b * kim).astype(BF16)
    yim = (b * kre - a * kim).astype(BF16)
    y = _dot(c_ref[...], yre) + _dot(s_ref[...], yim)
    o_ref[...] = (x0 * (y + k * bias_ref[...])).astype(o_ref.dtype)


def _hyena(p_hy, length, blk0, conv_w, bias, dft, kre, kim):
    c_m, s_m = dft[0], dft[1]
    const = lambda b: (0, 0)
    once = pl.Buffered(1)
    in_specs = [
        pl.BlockSpec((length, HY_CH), lambda b: (blk0 + b, 0)),
        pl.BlockSpec((length, HY_CH), lambda b: (blk0 + b, 1)),
        pl.BlockSpec((length, HY_CH), lambda b: (blk0 + b, 2)),
        pl.BlockSpec((3, 768), const),
        pl.BlockSpec((1, HY_CH), const),
        pl.BlockSpec((length, length), const, pipeline_mode=once),
        pl.BlockSpec((length, length), const, pipeline_mode=once),
        pl.BlockSpec((length, HY_CH), const, pipeline_mode=once),
        pl.BlockSpec((length, HY_CH), const, pipeline_mode=once),
    ]
    return pl.pallas_call(
        _hyena_kernel,
        grid=(B,),
        in_specs=in_specs,
        out_specs=pl.BlockSpec((length, HY_CH), lambda b: (b, 0)),
        out_shape=jax.ShapeDtypeStruct((B * length, HY_CH), BF16),
        compiler_params=_cp("arbitrary"),
        name=f"hyena_{length}",
    )(p_hy, p_hy, p_hy, conv_w, bias[None, :], c_m, s_m, kre, kim)


def _softmax_pv(heads):
    probs, dens = [], []
    for parts, sink in heads:
        m = None
        for s, _ in parts:
            mi = jnp.max(s, axis=-1, keepdims=True)
            m = mi if m is None else jnp.maximum(m, mi)
        if sink is not None:
            m = jnp.maximum(m, sink)
        den = 0.0 if sink is None else jnp.exp(sink - m)
        es = []
        for s, _ in parts:
            e = jnp.exp(s - m)
            den = den + jnp.sum(e, axis=-1, keepdims=True)
            es.append(e.astype(BF16))
        probs.append(es)
        dens.append(den)
    outs = []
    for (parts, _), es in zip(heads, probs):
        acc = None
        for e, (_, v) in zip(es, parts):
            o = _dot(e, v)
            acc = o if acc is None else acc + o
        outs.append(acc)
    return [o / den for o, den in zip(outs, dens)]


def _wa_kernel(sink_ref, q_ref, k_ref, v_ref, kc_ref, vc_ref, o_ref):
    kc, vc = kc_ref[...], vc_ref[...]
    rr = lax.broadcasted_iota(jnp.int32, (256, 384), 0)
    r1 = lax.broadcasted_iota(jnp.int32, (256, 1), 0)
    heads = []
    for j in range(WA_BLKS):
        n = pl.program_id(1) * WA_BLKS + j
        start = pl.multiple_of(jnp.clip((n - 1) * 128, 0, S - 384), 128)
        kw = k_ref[pl.ds(start, 384), :]
        vw = v_ref[pl.ds(start, 384), :]
        q = q_ref[j * 128:(j + 1) * 128, :]
        qpos = n * 128 + jnp.where(rr >= 128, rr - 128, rr)
        kpos = start + lax.broadcasted_iota(jnp.int32, (256, 384), 1)
        valid = jnp.abs(qpos - kpos) <= WINDOW
        for hk in range(2):
            q2 = jnp.concatenate([q[:, (2 * hk) * HD:(2 * hk + 1) * HD],
                                  q[:, (2 * hk + 1) * HD:(2 * hk + 2) * HD]], axis=0)
            hs = slice(hk * HD, (hk + 1) * HD)
            s_loc = jnp.where(valid, _dot_nt(q2, kw[:, hs]) * SCALE, NEG)
            s_ctx = _dot_nt(q2, kc[:, hs]) * SCALE
            sink = jnp.where(r1 >= 128, sink_ref[2 * hk + 1], sink_ref[2 * hk])
            heads.append(([(s_ctx, vc[:, hs]), (s_loc, vw[:, hs])], sink))
    for i, o in enumerate(_softmax_pv(heads)):
        j, hk = divmod(i, 2)
        rows = slice(j * 128, (j + 1) * 128)
        o_ref[rows, (2 * hk) * HD:(2 * hk + 1) * HD] = o[:128].astype(o_ref.dtype)
        o_ref[rows, (2 * hk + 1) * HD:(2 * hk + 2) * HD] = o[128:].astype(o_ref.dtype)


def _window_attention(p_wa, sink):
    nb = S // (128 * WA_BLKS)
    return pl.pallas_call(
        _wa_kernel,
        grid=(B, nb),
        in_specs=[
            pl.BlockSpec(memory_space=pltpu.SMEM),
            pl.BlockSpec((128 * WA_BLKS, 256), lambda b, n: (b * nb + n, 0)),
            pl.BlockSpec((S, 128), lambda b, n: (b, 2)),
            pl.BlockSpec((S, 128), lambda b, n: (b, 3)),
            pl.BlockSpec((LC, 128), lambda b, n: (CTX_BLK0 + b, 2)),
            pl.BlockSpec((LC, 128), lambda b, n: (CTX_BLK0 + b, 3)),
        ],
        out_specs=pl.BlockSpec((128 * WA_BLKS, 256), lambda b, n: (b * nb + n, 0)),
        out_shape=jax.ShapeDtypeStruct((NL, 256), BF16),
        compiler_params=_cp("arbitrary", "arbitrary"),
        name="window_attn",
    )(sink, p_wa, p_wa, p_wa, p_wa, p_wa)


def _na_kernel(q_ref, k_ref, v_ref, kc_ref, vc_ref, bias_ref, o_ref):
    kc, vc = kc_ref[...], vc_ref[...]
    heads = []
    for j in range(NA_ROWS):
        r = pl.program_id(1) * NA_ROWS + j
        first = jnp.clip(r - NA_KR // 2, 0, S // GW - NA_KR)
        start = pl.multiple_of(first * GW, GW)
        off = first - r + NA_KR - 1
        kw = k_ref[pl.ds(start, NA_KR * GW), :]
        vw = v_ref[pl.ds(start, NA_KR * GW), :]
        q = q_ref[j * GW:(j + 1) * GW, :]
        for h in range(4):
            hs = slice(h * HD, (h + 1) * HD)
            s_loc = _dot_nt(q[:, hs], kw[:, hs]) * SCALE + bias_ref[h, off]
            s_ctx = _dot_nt(q[:, hs], kc[:, hs]) * SCALE
            heads.append(([(s_ctx, vc[:, hs]), (s_loc, vw[:, hs])], None))
    for i, o in enumerate(_softmax_pv(heads)):
        j, h = divmod(i, 4)
        o_ref[j * GW:(j + 1) * GW, h * HD:(h + 1) * HD] = o.astype(o_ref.dtype)


def _na_bias_kernel(rpb_ref, sel_ref, o_ref):
    o_ref[0] = _dot(rpb_ref[0], sel_ref[...], HI)


def _na_bias_table(rpb):
    qc = jnp.arange(GW)[:, None]
    kcol = jnp.arange(GW)[None, :]
    ci = jnp.clip(kcol - qc + NA_KC - 1, 0, 2 * NA_KC - 2).reshape(1, GW * GW)
    sel = (jnp.arange(128)[:, None] == ci).astype(F32)
    n_ri = 2 * NA_KR - 1
    rpb_p = jnp.pad(rpb.astype(F32), ((0, 0), (0, 16 - n_ri), (0, 128 - (2 * NA_KC - 1))))
    per_row = pl.pallas_call(
        _na_bias_kernel,
        grid=(4,),
        in_specs=[pl.BlockSpec((1, 16, 128), lambda h: (h, 0, 0)), pl.BlockSpec((128, GW * GW), lambda h: (0, 0))],
        out_specs=pl.BlockSpec((1, 16, GW * GW), lambda h: (h, 0, 0)),
        out_shape=jax.ShapeDtypeStruct((4, 16, GW * GW), F32),
        compiler_params=_cp("arbitrary"),
        name="na_bias",
    )(rpb_p, sel).reshape(4, 16, GW, GW)
    tbl = jnp.stack([per_row[:, o:o + NA_KR] for o in range(NA_KR)], axis=1)
    tbl = jnp.transpose(tbl, (0, 1, 3, 2, 4))
    c_start = jnp.clip(qc - NA_KC // 2, 0, GW - NA_KC)
    ok = (kcol >= c_start) & (kcol < c_start + NA_KC)
    tbl = jnp.where(ok[None, None, :, None, :], tbl, NEG)
    return tbl.reshape(4, NA_KR, GW, NA_KR * GW)


def _neighborhood_attention(p_na, bias_tbl):
    steps = S // GW // NA_ROWS
    return pl.pallas_call(
        _na_kernel,
        grid=(B, steps),
        in_specs=[
            pl.BlockSpec((NA_ROWS * GW, 256), lambda b, r: (b * steps + r, 0)),
            pl.BlockSpec((S, 256), lambda b, r: (b, 1)),
            pl.BlockSpec((S, 256), lambda b, r: (b, 2)),
            pl.BlockSpec((LC, 256), lambda b, r: (CTX_BLK0 + b, 1)),
            pl.BlockSpec((LC, 256), lambda b, r: (CTX_BLK0 + b, 2)),
            pl.BlockSpec((4, NA_KR, GW, NA_KR * GW), lambda b, r: (0, 0, 0, 0)),
        ],
        out_specs=pl.BlockSpec((NA_ROWS * GW, 256), lambda b, r: (b * steps + r, 0)),
        out_shape=jax.ShapeDtypeStruct((NL, 256), BF16),
        compiler_params=_cp("arbitrary", "arbitrary"),
        name="neighborhood_attn",
    )(p_na, p_na, p_na, p_na, p_na, bias_tbl)


def _ctx_attn_kernel(sink_ref, qw_ref, kw_ref, vw_ref, qn_ref, kn_ref, vn_ref, owa_ref, ona_ref):
    q, k, v = qw_ref[...], kw_ref[...], vw_ref[...]
    r1 = lax.broadcasted_iota(jnp.int32, (2 * LC, 1), 0)
    heads = []
    for hk in range(2):
        q2 = jnp.concatenate([q[:, (2 * hk) * HD:(2 * hk + 1) * HD],
                              q[:, (2 * hk + 1) * HD:(2 * hk + 2) * HD]], axis=0)
        hs = slice(hk * HD, (hk + 1) * HD)
        sink = jnp.where(r1 >= LC, sink_ref[2 * hk + 1], sink_ref[2 * hk])
        heads.append(([(_dot_nt(q2, k[:, hs]) * SCALE, v[:, hs])], sink))
    for hk, o in enumerate(_softmax_pv(heads)):
        owa_ref[:, (2 * hk) * HD:(2 * hk + 1) * HD] = o[:LC].astype(owa_ref.dtype)
        owa_ref[:, (2 * hk + 1) * HD:(2 * hk + 2) * HD] = o[LC:].astype(owa_ref.dtype)
    q, k, v = qn_ref[...], kn_ref[...], vn_ref[...]
    heads = []
    for h in range(4):
        hs = slice(h * HD, (h + 1) * HD)
        heads.append(([(_dot_nt(q[:, hs], k[:, hs]) * SCALE, v[:, hs])], None))
    for h, o in enumerate(_softmax_pv(heads)):
        ona_ref[:, h * HD:(h + 1) * HD] = o.astype(ona_ref.dtype)


def _ctx_attention(p_wa, p_na, sink):
    blk = lambda c: (lambda b: (CTX_BLK0 + b, c))
    return pl.pallas_call(
        _ctx_attn_kernel,
        grid=(B,),
        in_specs=[
            pl.BlockSpec(memory_space=pltpu.SMEM),
            pl.BlockSpec((LC, 256), blk(0)),
            pl.BlockSpec((LC, 128), blk(2)),
            pl.BlockSpec((LC, 128), blk(3)),
            pl.BlockSpec((LC, 256), blk(0)),
            pl.BlockSpec((LC, 256), blk(1)),
            pl.BlockSpec((LC, 256), blk(2)),
        ],
        out_specs=[pl.BlockSpec((LC, 256), lambda b: (b, 0))] * 2,
        out_shape=[jax.ShapeDtypeStruct((NC, 256), BF16)] * 2,
        compiler_params=_cp("arbitrary"),
        name="ctx_attn",
    )(sink, p_wa, p_wa, p_wa, p_na, p_na, p_na)


def _gdn_prep_seq(p_ref, ab_ref, cw, alog, dtb, qkv_ref, gb_ref, row0, length):
    x = p_ref[:, :768].astype(F32)
    u = (_shift_rows(x, -2) * cw[0:1] + _shift_rows(x, -1) * cw[1:2] + x * cw[2:3]
         + _shift_rows(x, 1) * cw[3:4])
    u = _silu(u)
    rows = slice(row0, row0 + length)
    for j in range(8):
        sl = slice(j * HD, (j + 1) * HD)
        xs = u[:, sl]
        nrm = lax.rsqrt(jnp.sum(xs * xs, axis=-1, keepdims=True) + EPS)
        qkv_ref[0, rows, sl] = xs * (nrm * SCALE if j < 4 else nrm)
    qkv_ref[0, rows, 512:768] = u[:, 512:768]
    ab = ab_ref[...]
    g = -jnp.exp(alog) * _softplus(ab + dtb)
    lane = lax.broadcasted_iota(jnp.int32, ab.shape, 1)
    gb_ref[0, rows, :] = jnp.where(lane < 8, g, _sigmoid(ab))


def _gdn_prep_kernel(pl_ref, abl_ref, pc_ref, abc_ref, cw_ref, alog_ref, dtb_ref, qkv_ref, gb_ref):
    cw, alog, dtb = cw_ref[...], alog_ref[...], dtb_ref[...]
    _gdn_prep_seq(pl_ref, abl_ref, cw, alog, dtb, qkv_ref, gb_ref, 0, S)
    _gdn_prep_seq(pc_ref, abc_ref, cw, alog, dtb, qkv_ref, gb_ref, S, LC)


def _gdn_prep(p_gd, p_ab, conv_w, a_log, dt_bias):
    const = lambda b: (0, 0)
    alog = jnp.pad(a_log.reshape(1, 8), ((0, 0), (0, 120)))
    dtb = jnp.pad(dt_bias.reshape(1, 8), ((0, 0), (0, 120)))
    return pl.pallas_call(
        _gdn_prep_kernel,
        grid=(B,),
        in_specs=[
            pl.BlockSpec((S, 1024), lambda b: (b, 0)),
            pl.BlockSpec((S, 128), lambda b: (b, 0)),
            pl.BlockSpec((LC, 1024), lambda b: (CTX_BLK0 + b, 0)),
            pl.BlockSpec((LC, 128), lambda b: (CTX_BLK0 + b, 0)),
            pl.BlockSpec((4, 768), const),
            pl.BlockSpec((1, 128), const),
            pl.BlockSpec((1, 128), const),
        ],
        out_specs=[pl.BlockSpec((1, S + LC, 768), lambda b: (b, 0, 0)),
                   pl.BlockSpec((1, S + LC, 128), lambda b: (b, 0, 0))],
        out_shape=[jax.ShapeDtypeStruct((B, S + LC, 768), F32), jax.ShapeDtypeStruct((B, S + LC, 128), F32)],
        compiler_params=_cp("arbitrary"),
        name="gdn_prep",
    )(p_gd, p_ab, p_gd, p_ab, conv_w, alog, dtb)


def _gdn_chains(xs, gbs, cums, cum_ts, states, masks):
    chains = [(g, h) for g in range(len(xs)) for h in range(4)]
    pre = []
    for g, h in chains:
        dirn = g % 2
        x, gb = xs[g], gbs[g]
        tri, strict = masks[dirn]
        col = dirn * 4 + h
        qh = x[:, h * HD:(h + 1) * HD]
        kh = x[:, 256 + h * HD:256 + (h + 1) * HD]
        vh = x[:, 512 + h * HD:512 + (h + 1) * HD]
        gc = cums[g][:, col:col + 1]
        gc_row = cum_ts[g][col:col + 1, :]
        beta = gb[:, 8 + col:9 + col]
        decay = jnp.exp(jnp.where(tri, gc - gc_row, NEG))
        kb = kh * beta
        eg = jnp.exp(gc)
        g_last = gc[0:1, :] if dirn else gc[CHUNK - 1:CHUNK, :]
        pre.append(dict(strict=strict, decay=decay, kbf=kb.astype(BF16), khf=kh.astype(BF16), qf=qh.astype(BF16),
                        xw=jnp.concatenate([vh * beta, kb * eg], axis=1), qdec=(qh * eg).astype(BF16),
                        kdec=(kh * jnp.exp(g_last - gc)).astype(BF16), gl=jnp.exp(g_last)))
    n = len(chains)
    gram = [_dot_nt(p["kbf"], p["khf"]) for p in pre]
    attn = [_dot_nt(p["qf"], p["khf"]) for p in pre]
    lmat = [jnp.where(p["strict"], g * p["decay"], 0.0) for p, g in zip(pre, gram)]
    attn = [(a * p["decay"]).astype(BF16) for p, a in zip(pre, attn)]
    ii = lax.broadcasted_iota(jnp.int32, (CHUNK, CHUNK), 0)
    jj = lax.broadcasted_iota(jnp.int32, (CHUNK, CHUNK), 1)

    def merged_off_blocks(log_s):
        return ((ii >> (log_s + 1)) == (jj >> (log_s + 1))) & ((ii >> log_s) != (jj >> log_s))

    eye = (ii == jj).astype(F32)
    dinv = [eye - jnp.where(merged_off_blocks(0), lm, 0.0) for lm in lmat]
    for log_s in range(1, 6):
        m = merged_off_blocks(log_s)
        dinv_b = [d.astype(BF16) for d in dinv]
        ld = [_dot(jnp.where(m, lmat[c], 0.0).astype(BF16), dinv_b[c]) for c in range(n)]
        upd = [_dot(dinv_b[c], ld[c].astype(BF16)) for c in range(n)]
        dinv = [dinv[c] - upd[c] for c in range(n)]
    xw = [_dot(dinv[c].astype(BF16), pre[c]["xw"].astype(BF16)) for c in range(n)]
    sts = [states[g][h] for g, h in chains]
    stb = [s.astype(BF16) for s in sts]
    ws = [_dot(xw[c][:, HD:].astype(BF16), stb[c]) for c in range(n)]
    qs = [_dot(pre[c]["qdec"], stb[c]) for c in range(n)]
    u_new = [(xw[c][:, :HD] - ws[c]).astype(BF16) for c in range(n)]
    au = [_dot(attn[c], u_new[c]) for c in range(n)]
    ku = [_dot_tn(pre[c]["kdec"], u_new[c]) for c in range(n)]
    outs = [qs[c] + au[c] for c in range(n)]
    new_states = [sts[c] * pre[c]["gl"] + ku[c] for c in range(n)]
    return outs, new_states


def _gdn_kernel(x0_ref, gb0_ref, x1_ref, gb1_ref, o0_ref, o1_ref, state_ref):
    @pl.when(pl.program_id(0) == 0)
    def _():
        state_ref[...] = jnp.zeros_like(state_ref)

    ii = lax.broadcasted_iota(jnp.int32, (CHUNK, CHUNK), 0)
    jj = lax.broadcasted_iota(jnp.int32, (CHUNK, CHUNK), 1)
    masks = ((jj <= ii, jj < ii), (jj >= ii, jj > ii))
    ones_tri = ((jj <= ii).astype(F32), (jj >= ii).astype(F32))

    def body(bp, carry):
        bs = [bp * GDN_SEQS + i for i in range(GDN_SEQS)]
        states = [[state_ref[b, dirn, h] for h in range(4)] for b in bs for dirn in range(2)]
        xs = [r[b] for b in bs for r in (x0_ref, x1_ref)]
        gbs = [r[b] for b in bs for r in (gb0_ref, gb1_ref)]
        cums = [_dot(ones_tri[g % 2], gbs[g], HI) for g in range(2 * GDN_SEQS)]
        cum_ts = [cm.T for cm in cums]
        outs, new_states = _gdn_chains(xs, gbs, cums, cum_ts, states, masks)
        for c, st in enumerate(new_states):
            state_ref[bs[c // 8], (c // 4) % 2, c % 4] = st
        for i, b in enumerate(bs):
            o0_ref[b] = jnp.concatenate(outs[8 * i:8 * i + 4], axis=1)
            o1_ref[b] = jnp.concatenate(outs[8 * i + 4:8 * i + 8], axis=1)
        return carry

    lax.fori_loop(0, B // GDN_SEQS, body, 0)


def _gdn_scan(qkv, gb):
    nc = (S + LC) // CHUNK
    fwd = lambda c: (0, (c + S // CHUNK) % nc, 0)
    bwd = lambda c: (0, nc - 1 - c, 0)
    return pl.pallas_call(
        _gdn_kernel,
        grid=(nc,),
        in_specs=[
            pl.BlockSpec((B, CHUNK, 768), fwd),
            pl.BlockSpec((B, CHUNK, 128), fwd),
            pl.BlockSpec((B, CHUNK, 768), bwd),
            pl.BlockSpec((B, CHUNK, 128), bwd),
        ],
        out_specs=[pl.BlockSpec((B, CHUNK, 256), fwd), pl.BlockSpec((B, CHUNK, 256), bwd)],
        out_shape=[jax.ShapeDtypeStruct((B, S + LC, 256), F32)] * 2,
        scratch_shapes=[pltpu.VMEM((B, 2, 4, HD, HD), F32)],
        compiler_params=_cp("arbitrary"),
        name="gdn_scan",
    )(qkv, gb, qkv, gb)


def _gdn_finish_kernel(o0_ref, o1_ref, gate_ref, ng_ref, y_ref):
    o = o0_ref[0] + o1_ref[0]
    gi = lax.broadcasted_iota(jnp.int32, (256, 256), 0) // HD
    gj = lax.broadcasted_iota(jnp.int32, (256, 256), 1) // HD
    group_mean = jnp.where(gi == gj, 1.0 / HD, 0.0)
    ms = _dot(o * o, group_mean, HI)
    y = o * lax.rsqrt(ms + EPS) * ng_ref[...]
    y_ref[...] = (y * _silu(gate_ref[...].astype(F32))).astype(y_ref.dtype)


def _gdn_finish(o0, o1, p_gd, norm_g, need_ctx):
    lat_tiles = S // TM
    row_blk = lambda b, j: jnp.where(j < lat_tiles, b * lat_tiles + j, CTX_BLK0 + b)
    return pl.pallas_call(
        _gdn_finish_kernel,
        grid=(B, lat_tiles + (1 if need_ctx else 0)),
        in_specs=[
            pl.BlockSpec((1, TM, 256), lambda b, j: (b, j, 0)),
            pl.BlockSpec((1, TM, 256), lambda b, j: (b, j, 0)),
            pl.BlockSpec((TM, 256), lambda b, j: (row_blk(b, j), 3)),
            pl.BlockSpec((1, 256), lambda b, j: (0, 0)),
        ],
        out_specs=pl.BlockSpec((TM, 256), lambda b, j: (row_blk(b, j), 0)),
        out_shape=jax.ShapeDtypeStruct((ROWS if need_ctx else NL, 256), BF16),
        compiler_params=_cp("arbitrary", "arbitrary"),
        name="gdn_finish",
    )(o0, o1, p_gd, jnp.tile(norm_g, 4)[None, :])


def _pack_bf16_pairs(x):
    n = x.shape[1] // 2
    lo = pltpu.bitcast(x[:, :n].astype(BF16).astype(F32), jnp.uint32)
    hi = pltpu.bitcast(x[:, n:].astype(BF16).astype(F32), jnp.uint32)
    return hi | (lo >> 16)


def _unpack_bf16_pairs(w):
    lo = pltpu.bitcast(w << 16, F32)
    hi = pltpu.bitcast(w & jnp.uint32(0xFFFF0000), F32)
    return jnp.concatenate([lo, hi], axis=1)


def _outproj_kernel(x_ref, mod_ref, g2_ref, yh_ref, yw_ref, yn_ref, yg_ref, wo_ref, wr_ref, eb_ref,
                    xo_ref, h2_ref, h2p_ref, idx_ref, tw_ref, rank_ref, cnt_ref, carry_ref):
    @pl.when(pl.program_id(0) == 0)
    def _():
        carry_ref[...] = jnp.zeros_like(carry_ref)

    m = mod_ref[0]
    acc = (_dot(yh_ref[...], wo_ref[0:256, :]) + _dot(yw_ref[...], wo_ref[256:512, :])
           + _dot(yn_ref[...], wo_ref[512:768, :]) + _dot(yg_ref[...], wo_ref[768:1024, :]))
    x = x_ref[...] + m[2:3] * acc
    xo_ref[...] = x
    h2 = _modulated_norm(x, g2_ref[...], m[3:4], m[4:5])
    h2_ref[...] = h2.astype(h2_ref.dtype)
    h2p_ref[...] = _pack_bf16_pairs(h2)
    scores = _sigmoid(_dot(h2, wr_ref[...], HI))
    lane = lax.broadcasted_iota(jnp.int32, scores.shape, 1)
    sel = jnp.where(lane < N_EXP, scores + eb_ref[...], -jnp.inf)
    lane_f = lane.astype(F32)
    idx_out = jnp.zeros(scores.shape, F32)
    s_out = jnp.zeros(scores.shape, F32)
    hits = []
    for kk in range(TOP_K):
        mx = jnp.max(sel, axis=-1, keepdims=True)
        idx = jnp.min(jnp.where(sel == mx, lane_f, 128.0), axis=-1, keepdims=True)
        hit = lane_f == idx
        sk = jnp.sum(jnp.where(hit, scores, 0.0), axis=-1, keepdims=True)
        sel = jnp.where(hit, -jnp.inf, sel)
        idx_out = jnp.where(lane == kk, idx, idx_out)
        s_out = jnp.where(lane == kk, sk, s_out)
        hits.append(hit)
    tot = jnp.sum(s_out, axis=-1, keepdims=True)
    tw_ref[...] = s_out / tot * ROUTED_SCALE
    cnt = jnp.zeros(scores.shape, F32)
    for hit in hits:
        cnt = cnt + jnp.where(hit, 1.0, 0.0)
    ri = lax.broadcasted_iota(jnp.int32, (TM, TM), 0)
    rj = lax.broadcasted_iota(jnp.int32, (TM, TM), 1)
    before = _dot(jnp.where(rj < ri, 1.0, 0.0).astype(BF16), cnt.astype(BF16)) + carry_ref[...]
    rank_out = jnp.zeros(scores.shape, F32)
    for kk, hit in enumerate(hits):
        rk = jnp.sum(jnp.where(hit, before, 0.0), axis=-1, keepdims=True)
        rank_out = jnp.where(lane == kk, rk, rank_out)
    idx_ref[...] = idx_out.T[:8].astype(jnp.int32)
    rank_ref[...] = rank_out.T[:8].astype(jnp.int32)
    carry_ref[...] = carry_ref[...] + jnp.sum(cnt, axis=0, keepdims=True)
    cnt_ref[...] = carry_ref[...].astype(jnp.int32)


def _outproj(xa, mod, l, g2, ys, wo, wr, eb, n_tiles):
    row = lambda i: (i, 0)
    const = lambda i: (0, 0)
    rows = n_tiles * TM
    return pl.pallas_call(
        _outproj_kernel,
        grid=(n_tiles,),
        in_specs=[
            pl.BlockSpec((TM, D), row),
            pl.BlockSpec((1, 6, D), lambda i: (l * 16 + (i * TM) // S, 0, 0)),
            pl.BlockSpec((1, D), const),
            pl.BlockSpec((TM, 256), row),
            pl.BlockSpec((TM, 256), row),
            pl.BlockSpec((TM, 256), row),
            pl.BlockSpec((TM, 256), row),
            pl.BlockSpec((D, D), const),
            pl.BlockSpec((D, 128), const),
            pl.BlockSpec((1, 128), const),
        ],
        out_specs=[
            pl.BlockSpec((TM, D), row),
            pl.BlockSpec((TM, D), row),
            pl.BlockSpec((TM, D // 2), row),
            pl.BlockSpec((8, TM), row),
            pl.BlockSpec((TM, 128), row),
            pl.BlockSpec((8, TM), row),
            pl.BlockSpec((1, 128), const),
        ],
        out_shape=[
            jax.ShapeDtypeStruct((rows, D), F32),
            jax.ShapeDtypeStruct((rows, D), BF16),
            jax.ShapeDtypeStruct((rows, D // 2), jnp.uint32),
            jax.ShapeDtypeStruct((n_tiles * 8, TM), jnp.int32),
            jax.ShapeDtypeStruct((rows, 128), F32),
            jax.ShapeDtypeStruct((n_tiles * 8, TM), jnp.int32),
            jax.ShapeDtypeStruct((1, 128), jnp.int32),
        ],
        scratch_shapes=[pltpu.VMEM((1, 128), F32)],
        compiler_params=_cp("arbitrary"),
        name="outproj_router",
    )(xa, mod, g2, *ys, wo, wr, eb)


ASG_TILE = TM * TOP_K


def _segment_tables(counts, n_blocks):
    counts = counts[0, :N_EXP]
    padded = (counts + MOE_BLK - 1) // MOE_BLK * MOE_BLK
    pad_end = jnp.cumsum(padded)
    starts = jnp.arange(n_blocks, dtype=jnp.int32) * MOE_BLK
    blk_e = jnp.minimum(jnp.sum((pad_end[None, :] <= starts[:, None]).astype(jnp.int32), axis=1), N_EXP - 1)
    blk_e = jnp.concatenate([blk_e, pad_end[-1:] // MOE_BLK])
    return (pad_end - padded).astype(jnp.int32), blk_e.astype(jnp.int32)


def _row_copy_wait(shape_ref, dst_ref, sem):
    pltpu.make_async_copy(shape_ref, dst_ref, sem).wait()


def _dispatch_kernel(seg_ref, idx_ref, rank_ref, h2p_ref, xb_in_ref, xb_ref, sem):
    del xb_in_ref

    def issue(t, carry):
        src = h2p_ref.at[pl.ds(t, 1)]
        for k in range(TOP_K):
            dst = seg_ref[idx_ref[k, t]] + rank_ref[k, t]
            pltpu.make_async_copy(src, xb_ref.at[pl.ds(dst, 1)], sem).start()
        return carry

    lax.fori_loop(0, TM, issue, 0, unroll=True)
    for _ in range(TOP_K):
        _row_copy_wait(h2p_ref, xb_ref.at[pl.ds(0, TM)], sem)


def _dispatch_rows(seg_start, idx_c, rank_c, h2p, n_rows):
    n_tiles = h2p.shape[0] // TM
    grid_spec = pltpu.PrefetchScalarGridSpec(
        num_scalar_prefetch=1,
        grid=(n_tiles,),
        in_specs=[
            pl.BlockSpec((8, TM), lambda i, seg: (i, 0), memory_space=pltpu.SMEM),
            pl.BlockSpec((8, TM), lambda i, seg: (i, 0), memory_space=pltpu.SMEM),
            pl.BlockSpec((TM, D // 2), lambda i, seg: (i, 0)),
            pl.BlockSpec(memory_space=pl.ANY),
        ],
        out_specs=pl.BlockSpec(memory_space=pl.ANY),
        scratch_shapes=[pltpu.SemaphoreType.DMA(())],
    )
    return pl.pallas_call(
        _dispatch_kernel,
        grid_spec=grid_spec,
        out_shape=jax.ShapeDtypeStruct((n_rows, D // 2), jnp.uint32),
        input_output_aliases={4: 0},
        compiler_params=_cp("arbitrary"),
        name="moe_dispatch",
    )(seg_start, idx_c, rank_c, h2p, jnp.zeros((n_rows, D // 2), jnp.uint32))


def _moe_kernel(be_ref, x_ref, wg_ref, wu_ref, wd_ref, o_ref, wgb_ref, wub_ref, wdb_ref):
    i = pl.program_id(0)
    in_use = i < be_ref[pl.num_programs(0)]

    @pl.when(in_use & ((i == 0) | (be_ref[i] != be_ref[jnp.maximum(i - 1, 0)])))
    def _():
        wgb_ref[...] = wg_ref[0].astype(BF16)
        wub_ref[...] = wu_ref[0].astype(BF16)
        wdb_ref[...] = wd_ref[0].astype(BF16)

    @pl.when(in_use)
    def _():
        x = _unpack_bf16_pairs(x_ref[...]).astype(BF16)
        g = _dot(x, wgb_ref[...])
        u = _dot(x, wub_ref[...])
        hid = (_silu(g) * u).astype(BF16)
        o_ref[...] = _pack_bf16_pairs(_dot(hid, wdb_ref[...]))

    @pl.when(jnp.logical_not(in_use))
    def _():
        o_ref[...] = jnp.zeros_like(o_ref)


def _moe_experts(blk_e, xb, wg, wu, wd):
    n_rows = xb.shape[0]
    grid_spec = pltpu.PrefetchScalarGridSpec(
        num_scalar_prefetch=1,
        grid=(n_rows // MOE_BLK,),
        in_specs=[
            pl.BlockSpec((MOE_BLK, D // 2), lambda i, be: (i, 0)),
            pl.BlockSpec((1, D, D_EXP), lambda i, be: (be[i], 0, 0)),
            pl.BlockSpec((1, D, D_EXP), lambda i, be: (be[i], 0, 0)),
            pl.BlockSpec((1, D_EXP, D), lambda i, be: (be[i], 0, 0)),
        ],
        out_specs=pl.BlockSpec((MOE_BLK, D // 2), lambda i, be: (i, 0)),
        scratch_shapes=[pltpu.VMEM((D, D_EXP), BF16), pltpu.VMEM((D, D_EXP), BF16), pltpu.VMEM((D_EXP, D), BF16)],
    )
    return pl.pallas_call(
        _moe_kernel,
        grid_spec=grid_spec,
        out_shape=jax.ShapeDtypeStruct((n_rows, D // 2), jnp.uint32),
        compiler_params=_cp("arbitrary"),
        name="moe_experts",
    )(blk_e, xb, wg, wu, wd)


def _ffn_out_kernel(seg_ref, idx_ref, rank_ref, x_ref, mod_ref, h2_ref, tw_ref, wg_ref, wu_ref, wd_ref, nf_ref,
                    yb_ref, o_ref, gat_ref, sem, *, final):
    def issue(t, carry):
        for k in range(TOP_K):
            src = seg_ref[idx_ref[k, t]] + rank_ref[k, t]
            pltpu.make_async_copy(yb_ref.at[pl.ds(src, 1)], gat_ref.at[pl.ds(k * TM + t, 1)], sem).start()
        return carry

    lax.fori_loop(0, TM, issue, 0, unroll=True)
    m = mod_ref[0]
    h2 = h2_ref[...]
    hid = (_silu(_dot(h2, wg_ref[...])) * _dot(h2, wu_ref[...])).astype(BF16)
    shared = _dot(hid, wd_ref[...])
    for _ in range(TOP_K):
        _row_copy_wait(yb_ref.at[pl.ds(0, TM)], gat_ref.at[pl.ds(0, TM)], sem)
    tw = tw_ref[...]
    routed = _unpack_bf16_pairs(gat_ref[0:TM, :]) * tw[:, 0:1]
    for k in range(1, TOP_K):
        routed = routed + _unpack_bf16_pairs(gat_ref[k * TM:(k + 1) * TM, :]) * tw[:, k:k + 1]
    x = x_ref[...] + m[5:6] * (routed + shared)
    if final:
        ms = jnp.mean(x * x, axis=-1, keepdims=True)
        x = x * lax.rsqrt(ms + EPS) * nf_ref[...]
    o_ref[...] = x


def _ffn_out(seg_start, idx_c, rank_c, xn, mod, l, h2, top_w, wg, wu, wd, norm_f, yb, n_tiles, final):
    row = lambda i, seg: (i, 0)
    const = lambda i, seg: (0, 0)
    grid_spec = pltpu.PrefetchScalarGridSpec(
        num_scalar_prefetch=1,
        grid=(n_tiles,),
        in_specs=[
            pl.BlockSpec((8, TM), lambda i, seg: (i, 0), memory_space=pltpu.SMEM),
            pl.BlockSpec((8, TM), lambda i, seg: (i, 0), memory_space=pltpu.SMEM),
            pl.BlockSpec((TM, D), row),
            pl.BlockSpec((1, 6, D), lambda i, seg: (l * 16 + (i * TM) // S, 0, 0)),
            pl.BlockSpec((TM, D), row),
            pl.BlockSpec((TM, 128), row),
            pl.BlockSpec((D, D_EXP), const),
            pl.BlockSpec((D, D_EXP), const),
            pl.BlockSpec((D_EXP, D), const),
            pl.BlockSpec((1, D), const),
            pl.BlockSpec(memory_space=pl.ANY),
        ],
        out_specs=pl.BlockSpec((TM, D), row),
        scratch_shapes=[pltpu.VMEM((ASG_TILE, D // 2), jnp.uint32), pltpu.SemaphoreType.DMA(())],
    )
    return pl.pallas_call(
        functools.partial(_ffn_out_kernel, final=final),
        grid_spec=grid_spec,
        out_shape=jax.ShapeDtypeStruct((n_tiles * TM, D), F32),
        compiler_params=_cp("arbitrary"),
        name="shared_ffn_residual",
    )(seg_start, idx_c, rank_c, xn, mod, h2, top_w, wg, wu, wd, norm_f, yb)


def kernel(x, c, ctx, c_ctx, w_ada, b_ada, norm1, norm2, norm_f, w_in, w_out, hy_conv, hy_w1, hy_b1, hy_w2, hy_b2, hy_w3, hy_freq, hy_bias, wa_sink, na_rpb, gdn_conv, gdn_a_log, gdn_dt_bias, gdn_norm, moe_router, moe_bias, moe_gate, moe_up, moe_down, sh_gate, sh_up, sh_down):
    depth = w_ada.shape[0]
    xa = jnp.concatenate([x.reshape(NL, D), ctx.reshape(NC, D)], axis=0)
    cvec = jnp.concatenate([c, c_ctx[None, :], jnp.zeros((16 - B - 1, D), F32)], axis=0)
    mod = _ada(cvec, w_ada, b_ada).reshape(depth * 16, 6, D)
    rope = _rope_tables()
    dft_lat = _dft_tables(S)
    dft_ctx = _dft_tables(LC)
    o1, o2, o3 = 768, 768 + 512, 768 + 512 + 768

    for l in range(depth):
        need_ctx = l < depth - 1
        n_tiles = NT_ALL if need_ctx else NT_LAT
        wl = w_in[l].astype(BF16)
        ws = (wl[:, :o1], wl[:, o1:o2], wl[:, o2:o3], wl[:, o3:o3 + 1024],
              jnp.pad(wl[:, o3 + 1024:], ((0, 0), (0, 128 - 16))))
        p_hy, p_wa, p_na, p_gd, p_ab = _inproj(xa, mod, l, norm1[l][None, :], ws, rope)

        filt = (hy_w1[l], hy_b1[l], hy_w2[l], hy_b2[l], hy_w3[l], hy_freq[l])
        kre, kim = _hyena_filter_spectrum(S, *filt, dft_lat)
        y_hy = _hyena(p_hy, S, 0, hy_conv[l], hy_bias[l], dft_lat, kre, kim)
        y_wa = _window_attention(p_wa, wa_sink[l])
        y_na = _neighborhood_attention(p_na, _na_bias_table(na_rpb[l]))
        if need_ctx:
            kre_c, kim_c = _hyena_filter_spectrum(LC, *filt, dft_ctx)
            yc_hy = _hyena(p_hy, LC, CTX_BLK0, hy_conv[l], hy_bias[l], dft_ctx, kre_c, kim_c)
            yc_wa, yc_na = _ctx_attention(p_wa, p_na, wa_sink[l])
            y_hy = jnp.concatenate([y_hy, yc_hy], axis=0)
            y_wa = jnp.concatenate([y_wa, yc_wa], axis=0)
            y_na = jnp.concatenate([y_na, yc_na], axis=0)
        qkv, gb = _gdn_prep(p_gd, p_ab, gdn_conv[l], gdn_a_log[l], gdn_dt_bias[l])
        y_gd = _gdn_finish(*_gdn_scan(qkv, gb), p_gd, gdn_norm[l], need_ctx)

        wr = jnp.pad(moe_router[l], ((0, 0), (0, 128 - N_EXP)))
        eb = jnp.pad(moe_bias[l], (0, 128 - N_EXP))[None, :]
        xn, h2, h2p, top_idx, top_w, rank, counts = _outproj(
            xa, mod, l, norm2[l][None, :], (y_hy, y_wa, y_na, y_gd), w_out[l].astype(BF16), wr, eb, n_tiles)
        n_asg = n_tiles * ASG_TILE
        n_blocks = n_asg // MOE_BLK + N_EXP
        seg_start, blk_e = _segment_tables(counts, n_blocks)
        xb = _dispatch_rows(seg_start, top_idx, rank, h2p, n_blocks * MOE_BLK)
        yb = _moe_experts(blk_e, xb, moe_gate[l], moe_up[l], moe_down[l])
        xa_new = _ffn_out(seg_start, top_idx, rank, xn, mod, l, h2, top_w, sh_gate[l].astype(BF16),
                          sh_up[l].astype(BF16), sh_down[l].astype(BF16), norm_f[None, :], yb, n_tiles,
                          final=not need_ctx)
        if need_ctx:
            xa = xa_new
        else:
            return xa_new.reshape(B, S, D)
```

```python
import functools
import math

import jax
import jax.numpy as jnp
from jax import lax
from jax.experimental import pallas as pl
from jax.experimental.pallas import tpu as pltpu

F32 = jnp.float32
BF16 = jnp.bfloat16
HI = lax.Precision.HIGHEST

D = 1024
B = 8
S = 2048
LC = 256
GW = 64
HD = 64
NL = B * S
NC = B * LC
ROWS = NL + NC
TM = 256
NT_LAT = NL // TM
NT_ALL = ROWS // TM
CTX_BLK0 = NL // LC

HY_CH = 256
HY_BANDS = 16
HY_DECAY_MIN = -math.log(1e-2) / 1.5
HY_DECAY_MAX = -math.log(1e-2) / 0.3
WINDOW = 128
NA_KR = 8
NA_KC = 16
CHUNK = 64
N_EXP = 32
TOP_K = 4
D_EXP = 256
ROUTED_SCALE = 2.5
MOE_BLK = 512
WA_BLKS = 2
NA_ROWS = 4
GDN_SEQS = 4
EPS = 1e-6
NEG = -1e30
SCALE = HD ** -0.5
VMEM_LIMIT = 56 * 1024 * 1024


def _cp(*sem):
    return pltpu.CompilerParams(dimension_semantics=tuple(sem), vmem_limit_bytes=VMEM_LIMIT)


def _dot(a, b, precision=None):
    return jnp.dot(a, b, preferred_element_type=F32, precision=precision)


def _dot_nt(a, b, precision=None):
    return lax.dot_general(a, b, (((1,), (1,)), ((), ())), preferred_element_type=F32, precision=precision)


def _dot_tn(a, b, precision=None):
    return lax.dot_general(a, b, (((0,), (0,)), ((), ())), preferred_element_type=F32, precision=precision)


def _sigmoid(x):
    return 1.0 / (1.0 + jnp.exp(-x))


def _silu(x):
    return x * _sigmoid(x)


def _softplus(x):
    return jnp.maximum(x, 0.0) + jnp.log(1.0 + jnp.exp(-jnp.abs(x)))


def _shift_rows(x, d):
    n = x.shape[0]
    if d == 0:
        return x
    y = pltpu.roll(x, (-d) % n, axis=0)
    t = lax.broadcasted_iota(jnp.int32, x.shape, 0)
    ok = (t + d >= 0) & (t + d < n)
    return jnp.where(ok, y, 0.0)


def _ada_kernel(c_ref, w_ref, b_ref, o_ref):
    s = _silu(c_ref[...])
    o_ref[0] = _dot(s.astype(BF16), w_ref[0].astype(BF16)) + b_ref[0]


def _ada(cvec, w_ada, b_ada):
    nl = w_ada.shape[0]
    tn = 1536
    return pl.pallas_call(
        _ada_kernel,
        grid=(nl, 6 * D // tn),
        in_specs=[
            pl.BlockSpec((16, D), lambda l, j: (0, 0)),
            pl.BlockSpec((1, D, tn), lambda l, j: (l, 0, j)),
            pl.BlockSpec((1, 1, tn), lambda l, j: (l, 0, j)),
        ],
        out_specs=pl.BlockSpec((1, 16, tn), lambda l, j: (l, 0, j)),
        out_shape=jax.ShapeDtypeStruct((nl, 16, 6 * D), F32),
        compiler_params=_cp("arbitrary", "arbitrary"),
        name="adaln",
    )(cvec, w_ada, b_ada.reshape(nl, 1, 6 * D))


def _modulated_norm(x, g, shift, scale):
    ms = jnp.mean(x * x, axis=-1, keepdims=True)
    y = x * lax.rsqrt(ms + EPS) * g
    return y * (1.0 + scale) + shift


def _inproj_kernel(x_ref, mod_ref, g_ref, why_ref, wwa_ref, wna_ref, wgd_ref, wab_ref,
                   cos_ref, sa_ref, sb_ref, ohy, owa, ona, ogd, oab):
    m = mod_ref[0]
    h = _modulated_norm(x_ref[...], g_ref[...], m[0:1], m[1:2]).astype(BF16)
    ohy[...] = _dot(h, why_ref[...]).astype(ohy.dtype)
    ona[...] = _dot(h, wna_ref[...]).astype(ona.dtype)
    ogd[...] = _dot(h, wgd_ref[...]).astype(ogd.dtype)
    oab[...] = _dot(h, wab_ref[...])
    a = _dot(h, wwa_ref[...])
    for c in range(4):
        sl = slice(c * 128, (c + 1) * 128)
        ac = a[:, sl]
        r = (ac * cos_ref[:, sl] + pltpu.roll(ac, 112, axis=1) * sa_ref[:, sl]
             + pltpu.roll(ac, 16, axis=1) * sb_ref[:, sl])
        owa[:, sl] = r.astype(owa.dtype)


def _inproj(xa, mod, l, g, ws, tabs):
    why, wwa, wna, wgd, wab = ws
    cos_t, sa_t, sb_t = tabs
    row = lambda i: (i, 0)
    const = lambda i: (0, 0)
    tab = lambda i: (jnp.where(i < NT_LAT, i % (S // TM), S // TM), 0)
    return pl.pallas_call(
        _inproj_kernel,
        grid=(NT_ALL,),
        in_specs=[
            pl.BlockSpec((TM, D), row),
            pl.BlockSpec((1, 6, D), lambda i: (l * 16 + (i * TM) // S, 0, 0)),
            pl.BlockSpec((1, D), const),
            pl.BlockSpec((D, 768), const),
            pl.BlockSpec((D, 512), const),
            pl.BlockSpec((D, 768), const),
            pl.BlockSpec((D, 1024), const),
            pl.BlockSpec((D, 128), const),
            pl.BlockSpec((TM, 512), tab),
            pl.BlockSpec((TM, 512), tab),
            pl.BlockSpec((TM, 512), tab),
        ],
        out_specs=[
            pl.BlockSpec((TM, 768), row),
            pl.BlockSpec((TM, 512), row),
            pl.BlockSpec((TM, 768), row),
            pl.BlockSpec((TM, 1024), row),
            pl.BlockSpec((TM, 128), row),
        ],
        out_shape=[
            jax.ShapeDtypeStruct((ROWS, 768), BF16),
            jax.ShapeDtypeStruct((ROWS, 512), BF16),
            jax.ShapeDtypeStruct((ROWS, 768), BF16),
            jax.ShapeDtypeStruct((ROWS, 1024), BF16),
            jax.ShapeDtypeStruct((ROWS, 128), F32),
        ],
        compiler_params=_cp("arbitrary"),
        name="inproj",
    )(xa, mod, g, why, wwa, wna, wgd, wab, cos_t, sa_t, sb_t)


def _rope_tables():
    quarter = HD // 4
    pos = jnp.arange(S)
    inv = 10000.0 ** (-jnp.arange(quarter, dtype=F32) / quarter)
    ang_r = (pos // GW).astype(F32)[:, None] * inv[None, :]
    ang_c = (pos % GW).astype(F32)[:, None] * inv[None, :]
    z = jnp.zeros_like(ang_r)
    cos_h = jnp.concatenate([jnp.cos(ang_r)] * 2 + [jnp.cos(ang_c)] * 2, axis=1)
    sa_h = jnp.concatenate([-jnp.sin(ang_r), z, -jnp.sin(ang_c), z], axis=1)
    sb_h = jnp.concatenate([z, jnp.sin(ang_r), z, jnp.sin(ang_c)], axis=1)
    ones = jnp.ones((S, 128), F32)
    zeros = jnp.zeros((S, 128), F32)
    cos_t = jnp.concatenate([jnp.tile(cos_h, (1, 6)), ones], axis=1)
    sa_t = jnp.concatenate([jnp.tile(sa_h, (1, 6)), zeros], axis=1)
    sb_t = jnp.concatenate([jnp.tile(sb_h, (1, 6)), zeros], axis=1)
    ident = jnp.ones((TM, 512), F32)
    none = jnp.zeros((TM, 512), F32)
    return (jnp.concatenate([cos_t, ident], axis=0), jnp.concatenate([sa_t, none], axis=0),
            jnp.concatenate([sb_t, none], axis=0))


def _dft_tables(length):
    n = 2 * length
    f = jnp.arange(length, dtype=jnp.int32)
    m = ((2 * f[:, None] + 1) * (2 * f[None, :] + 1)) % (4 * n)
    th = m.astype(F32) * (2.0 * math.pi / (4 * n))
    phi = (2 * f + 1).astype(F32) * (math.pi / (2 * n))
    return (jnp.cos(th).astype(BF16), jnp.sin(th).astype(BF16),
            jnp.cos(phi)[:, None], jnp.sin(phi)[:, None])


def _hyena_features(length):
    t = jnp.arange(length, dtype=F32)
    t_norm = t / max(length - 1, 1)
    bands = jnp.linspace(1e-4, HY_BANDS - 1, HY_BANDS, dtype=F32)
    ang = (2.0 * math.pi / length) * t[:, None] * bands[None, :]
    z = jnp.concatenate([t_norm[:, None], jnp.cos(ang), -jnp.sin(ang)], axis=-1)
    z = jnp.pad(z, ((0, 0), (0, 128 - z.shape[1])))
    decay = jnp.tile(jnp.linspace(HY_DECAY_MIN, HY_DECAY_MAX, HY_CH, dtype=F32), 2)
    return z, jnp.exp(-t_norm[:, None] * decay[None, :])


def _hyfilt_kernel(z_ref, w1_ref, b1_ref, w2_ref, b2_ref, w3_ref, fr_ref, dec_ref,
                   c_ref, s_ref, cp_ref, sp_ref, kre_ref, kim_ref, pq_ref, *, length):
    @pl.when(pl.program_id(0) == 0)
    def _():
        fr = fr_ref[...]
        h = jnp.sin(fr * (_dot(z_ref[...], w1_ref[...], HI) + b1_ref[...]))
        h = jnp.sin(fr * (_dot(h, w2_ref[...], HI) + b2_ref[...]))
        h = _dot(h, w3_ref[...], HI) * dec_ref[...]
        hf = h[:, :HY_CH]
        t = lax.broadcasted_iota(jnp.int32, (length, HY_CH), 0)
        hb = jnp.where(t == 0, 0.0, h[:, HY_CH:])
        pq_ref[:, :HY_CH] = (hf + hb).astype(BF16)
        pq_ref[:, HY_CH:] = (hb - hf).astype(BF16)

    pq = pq_ref[...]
    cpq = _dot(c_ref[...], pq)
    spq = _dot(s_ref[...], pq)
    cphi, sphi = cp_ref[...], sp_ref[...]
    norm = 1.0 / length
    kre_ref[...] = (cphi * cpq[:, :HY_CH] + sphi * spq[:, :HY_CH]) * norm
    kim_ref[...] = (cphi * spq[:, HY_CH:] - sphi * cpq[:, HY_CH:]) * norm


def _hyena_filter_spectrum(length, w1, b1, w2, b2, w3, freq, dft):
    z, dec = _hyena_features(length)
    c_m, s_m, cphi, sphi = dft
    w1p = jnp.pad(w1, ((0, 128 - w1.shape[0]), (0, 0)))
    tf = min(512, length)
    const = lambda j: (0, 0)
    blk = lambda j: (j, 0)
    return pl.pallas_call(
        functools.partial(_hyfilt_kernel, length=length),
        grid=(length // tf,),
        in_specs=[
            pl.BlockSpec((length, 128), const),
            pl.BlockSpec((128, 64), const),
            pl.BlockSpec((1, 64), const),
            pl.BlockSpec((64, 64), const),
            pl.BlockSpec((1, 64), const),
            pl.BlockSpec((64, 2 * HY_CH), const),
            pl.BlockSpec((1, 64), const),
            pl.BlockSpec((length, 2 * HY_CH), const),
            pl.BlockSpec((tf, length), blk),
            pl.BlockSpec((tf, length), blk),
            pl.BlockSpec((tf, 1), blk),
            pl.BlockSpec((tf, 1), blk),
        ],
        out_specs=[pl.BlockSpec((tf, HY_CH), blk)] * 2,
        out_shape=[jax.ShapeDtypeStruct((length, HY_CH), F32)] * 2,
        scratch_shapes=[pltpu.VMEM((length, 2 * HY_CH), BF16)],
        compiler_params=_cp("arbitrary"),
        name=f"hyena_filter_{length}",
    )(z, w1p, b1[None, :], w2, b2[None, :], w3, freq[None, :], dec, c_m, s_m, cphi, sphi)


def _short_conv3(x_ref, cw):
    x = x_ref[...].astype(F32)
    return _shift_rows(x, -1) * cw[0:1] + x * cw[1:2] + _shift_rows(x, 1) * cw[2:3]


def _hyena_kernel(p0_ref, p1_ref, p2_ref, cw_ref, bias_ref, c_ref, s_ref, kre_ref, kim_ref, o_ref):
    cw = cw_ref[...]
    x0 = _short_conv3(p0_ref, cw[:, :HY_CH])
    k = _short_conv3(p1_ref, cw[:, HY_CH:2 * HY_CH]) * _short_conv3(p2_ref, cw[:, 2 * HY_CH:])
    kb = k.astype(BF16)
    a = _dot(c_ref[...], kb)
    b = _dot(s_ref[...], kb)
    kre, kim = kre_ref[...], kim_ref[...]
    yre = (a * kre + b * kim).astype(BF16)
    yim = (b * kre - a * kim).astype(BF16)
    y = _dot(c_ref[...], yre) + _dot(s_ref[...], yim)
    o_ref[...] = (x0 * (y + k * bias_ref[...])).astype(o_ref.dtype)


def _hyena(p_hy, length, blk0, conv_w, bias, dft, kre, kim):
    c_m, s_m = dft[0], dft[1]
    const = lambda b: (0, 0)
    once = pl.Buffered(1)
    in_specs = [
        pl.BlockSpec((length, HY_CH), lambda b: (blk0 + b, 0)),
        pl.BlockSpec((length, HY_CH), lambda b: (blk0 + b, 1)),
        pl.BlockSpec((length, HY_CH), lambda b: (blk0 + b, 2)),
        pl.BlockSpec((3, 768), const),
        pl.BlockSpec((1, HY_CH), const),
        pl.BlockSpec((length, length), const, pipeline_mode=once),
        pl.BlockSpec((length, length), const, pipeline_mode=once),
        pl.BlockSpec((length, HY_CH), const, pipeline_mode=once),
        pl.BlockSpec((length, HY_CH), const, pipeline_mode=once),
    ]
    return pl.pallas_call(
        _hyena_kernel,
        grid=(B,),
        in_specs=in_specs,
        out_specs=pl.BlockSpec((length, HY_CH), lambda b: (b, 0)),
        out_shape=jax.ShapeDtypeStruct((B * length, HY_CH), BF16),
        compiler_params=_cp("arbitrary"),
        name=f"hyena_{length}",
    )(p_hy, p_hy, p_hy, conv_w, bias[None, :], c_m, s_m, kre, kim)


def _softmax_pv(heads):
    probs, dens = [], []
    for parts, sink in heads:
        m = None
        for s, _ in parts:
            mi = jnp.max(s, axis=-1, keepdims=True)
            m = mi if m is None else jnp.maximum(m, mi)
        if sink is not None:
            m = jnp.maximum(m, sink)
        den = 0.0 if sink is None else jnp.exp(sink - m)
        es = []
        for s, _ in parts:
            e = jnp.exp(s - m)
            den = den + jnp.sum(e, axis=-1, keepdims=True)
            es.append(e.astype(BF16))
        probs.append(es)
        dens.append(den)
    outs = []
    for (parts, _), es in zip(heads, probs):
        acc = None
        for e, (_, v) in zip(es, parts):
            o = _dot(e, v)
            acc = o if acc is None else acc + o
        outs.append(acc)
    return [o / den for o, den in zip(outs, dens)]


def _wa_kernel(sink_ref, q_ref, k_ref, v_ref, kc_ref, vc_ref, o_ref):
    kc, vc = kc_ref[...], vc_ref[...]
    rr = lax.broadcasted_iota(jnp.int32, (256, 384), 0)
    r1 = lax.broadcasted_iota(jnp.int32, (256, 1), 0)
    heads = []
    for j in range(WA_BLKS):
        n = pl.program_id(1) * WA_BLKS + j
        start = pl.multiple_of(jnp.clip((n - 1) * 128, 0, S - 384), 128)
        kw = k_ref[pl.ds(start, 384), :]
        vw = v_ref[pl.ds(start, 384), :]
        q = q_ref[j * 128:(j + 1) * 128, :]
        qpos = n * 128 + jnp.where(rr >= 128, rr - 128, rr)
        kpos = start + lax.broadcasted_iota(jnp.int32, (256, 384), 1)
        valid = jnp.abs(qpos - kpos) <= WINDOW
        for hk in range(2):
            q2 = jnp.concatenate([q[:, (2 * hk) * HD:(2 * hk + 1) * HD],
                                  q[:, (2 * hk + 1) * HD:(2 * hk + 2) * HD]], axis=0)
            hs = slice(hk * HD, (hk + 1) * HD)
            s_loc = jnp.where(valid, _dot_nt(q2, kw[:, hs]) * SCALE, NEG)
            s_ctx = _dot_nt(q2, kc[:, hs]) * SCALE
            sink = jnp.where(r1 >= 128, sink_ref[2 * hk + 1], sink_ref[2 * hk])
            heads.append(([(s_ctx, vc[:, hs]), (s_loc, vw[:, hs])], sink))
    for i, o in enumerate(_softmax_pv(heads)):
        j, hk = divmod(i, 2)
        rows = slice(j * 128, (j + 1) * 128)
        o_ref[rows, (2 * hk) * HD:(2 * hk + 1) * HD] = o[:128].astype(o_ref.dtype)
        o_ref[rows, (2 * hk + 1) * HD:(2 * hk + 2) * HD] = o[128:].astype(o_ref.dtype)


def _window_attention(p_wa, sink):
    nb = S // (128 * WA_BLKS)
    return pl.pallas_call(
        _wa_kernel,
        grid=(B, nb),
        in_specs=[
            pl.BlockSpec(memory_space=pltpu.SMEM),
            pl.BlockSpec((128 * WA_BLKS, 256), lambda b, n: (b * nb + n, 0)),
            pl.BlockSpec((S, 128), lambda b, n: (b, 2)),
            pl.BlockSpec((S, 128), lambda b, n: (b, 3)),
            pl.BlockSpec((LC, 128), lambda b, n: (CTX_BLK0 + b, 2)),
            pl.BlockSpec((LC, 128), lambda b, n: (CTX_BLK0 + b, 3)),
        ],
        out_specs=pl.BlockSpec((128 * WA_BLKS, 256), lambda b, n: (b * nb + n, 0)),
        out_shape=jax.ShapeDtypeStruct((NL, 256), BF16),
        compiler_params=_cp("arbitrary", "arbitrary"),
        name="window_attn",
    )(sink, p_wa, p_wa, p_wa, p_wa, p_wa)


def _na_kernel(q_ref, k_ref, v_ref, kc_ref, vc_ref, bias_ref, o_ref):
    kc, vc = kc_ref[...], vc_ref[...]
    heads = []
    for j in range(NA_ROWS):
        r = pl.program_id(1) * NA_ROWS + j
        first = jnp.clip(r - NA_KR // 2, 0, S // GW - NA_KR)
        start = pl.multiple_of(first * GW, GW)
        off = first - r + NA_KR - 1
        kw = k_ref[pl.ds(start, NA_KR * GW), :]
        vw = v_ref[pl.ds(start, NA_KR * GW), :]
        q = q_ref[j * GW:(j + 1) * GW, :]
        for h in range(4):
            hs = slice(h * HD, (h + 1) * HD)
            s_loc = _dot_nt(q[:, hs], kw[:, hs]) * SCALE + bias_ref[h, off]
            s_ctx = _dot_nt(q[:, hs], kc[:, hs]) * SCALE
            heads.append(([(s_ctx, vc[:, hs]), (s_loc, vw[:, hs])], None))
    for i, o in enumerate(_softmax_pv(heads)):
        j, h = divmod(i, 4)
        o_ref[j * GW:(j + 1) * GW, h * HD:(h + 1) * HD] = o.astype(o_ref.dtype)


def _na_bias_kernel(rpb_ref, sel_ref, o_ref):
    o_ref[0] = _dot(rpb_ref[0], sel_ref[...], HI)


def _na_bias_table(rpb):
    qc = jnp.arange(GW)[:, None]
    kcol = jnp.arange(GW)[None, :]
    ci = jnp.clip(kcol - qc + NA_KC - 1, 0, 2 * NA_KC - 2).reshape(1, GW * GW)
    sel = (jnp.arange(128)[:, None] == ci).astype(F32)
    n_ri = 2 * NA_KR - 1
    rpb_p = jnp.pad(rpb.astype(F32), ((0, 0), (0, 16 - n_ri), (0, 128 - (2 * NA_KC - 1))))
    per_row = pl.pallas_call(
        _na_bias_kernel,
        grid=(4,),
        in_specs=[pl.BlockSpec((1, 16, 128), lambda h: (h, 0, 0)), pl.BlockSpec((128, GW * GW), lambda h: (0, 0))],
        out_specs=pl.BlockSpec((1, 16, GW * GW), lambda h: (h, 0, 0)),
        out_shape=jax.ShapeDtypeStruct((4, 16, GW * GW), F32),
        compiler_params=_cp("arbitrary"),
        name="na_bias",
    )(rpb_p, sel).reshape(4, 16, GW, GW)
    tbl = jnp.stack([per_row[:, o:o + NA_KR] for o in range(NA_KR)], axis=1)
    tbl = jnp.transpose(tbl, (0, 1, 3, 2, 4))
    c_start = jnp.clip(qc - NA_KC // 2, 0, GW - NA_KC)
    ok = (kcol >= c_start) & (kcol < c_start + NA_KC)
    tbl = jnp.where(ok[None, None, :, None, :], tbl, NEG)
    return tbl.reshape(4, NA_KR, GW, NA_KR * GW)


def _neighborhood_attention(p_na, bias_tbl):
    steps = S // GW // NA_ROWS
    return pl.pallas_call(
        _na_kernel,
        grid=(B, steps),
        in_specs=[
            pl.BlockSpec((NA_ROWS * GW, 256), lambda b, r: (b * steps + r, 0)),
            pl.BlockSpec((S, 256), lambda b, r: (b, 1)),
            pl.BlockSpec((S, 256), lambda b, r: (b, 2)),
            pl.BlockSpec((LC, 256), lambda b, r: (CTX_BLK0 + b, 1)),
            pl.BlockSpec((LC, 256), lambda b, r: (CTX_BLK0 + b, 2)),
            pl.BlockSpec((4, NA_KR, GW, NA_KR * GW), lambda b, r: (0, 0, 0, 0)),
        ],
        out_specs=pl.BlockSpec((NA_ROWS * GW, 256), lambda b, r: (b * steps + r, 0)),
        out_shape=jax.ShapeDtypeStruct((NL, 256), BF16),
        compiler_params=_cp("arbitrary", "arbitrary"),
        name="neighborhood_attn",
    )(p_na, p_na, p_na, p_na, p_na, bias_tbl)


def _ctx_attn_kernel(sink_ref, qw_ref, kw_ref, vw_ref, qn_ref, kn_ref, vn_ref, owa_ref, ona_ref):
    q, k, v = qw_ref[...], kw_ref[...], vw_ref[...]
    r1 = lax.broadcasted_iota(jnp.int32, (2 * LC, 1), 0)
    heads = []
    for hk in range(2):
        q2 = jnp.concatenate([q[:, (2 * hk) * HD:(2 * hk + 1) * HD],
                              q[:, (2 * hk + 1) * HD:(2 * hk + 2) * HD]], axis=0)
        hs = slice(hk * HD, (hk + 1) * HD)
        sink = jnp.where(r1 >= LC, sink_ref[2 * hk + 1], sink_ref[2 * hk])
        heads.append(([(_dot_nt(q2, k[:, hs]) * SCALE, v[:, hs])], sink))
    for hk, o in enumerate(_softmax_pv(heads)):
        owa_ref[:, (2 * hk) * HD:(2 * hk + 1) * HD] = o[:LC].astype(owa_ref.dtype)
        owa_ref[:, (2 * hk + 1) * HD:(2 * hk + 2) * HD] = o[LC:].astype(owa_ref.dtype)
    q, k, v = qn_ref[...], kn_ref[...], vn_ref[...]
    heads = []
    for h in range(4):
        hs = slice(h * HD, (h + 1) * HD)
        heads.append(([(_dot_nt(q[:, hs], k[:, hs]) * SCALE, v[:, hs])], None))
    for h, o in enumerate(_softmax_pv(heads)):
        ona_ref[:, h * HD:(h + 1) * HD] = o.astype(ona_ref.dtype)


def _ctx_attention(p_wa, p_na, sink):
    blk = lambda c: (lambda b: (CTX_BLK0 + b, c))
    return pl.pallas_call(
        _ctx_attn_kernel,
        grid=(B,),
        in_specs=[
            pl.BlockSpec(memory_space=pltpu.SMEM),
            pl.BlockSpec((LC, 256), blk(0)),
            pl.BlockSpec((LC, 128), blk(2)),
            pl.BlockSpec((LC, 128), blk(3)),
            pl.BlockSpec((LC, 256), blk(0)),
            pl.BlockSpec((LC, 256), blk(1)),
            pl.BlockSpec((LC, 256), blk(2)),
        ],
        out_specs=[pl.BlockSpec((LC, 256), lambda b: (b, 0))] * 2,
        out_shape=[jax.ShapeDtypeStruct((NC, 256), BF16)] * 2,
        compiler_params=_cp("arbitrary"),
        name="ctx_attn",
    )(sink, p_wa, p_wa, p_wa, p_na, p_na, p_na)


def _gdn_prep_seq(p_ref, ab_ref, cw, alog, dtb, qkv_ref, gb_ref, row0, length):
    x = p_ref[:, :768].astype(F32)
    u = (_shift_rows(x, -2) * cw[0:1] + _shift_rows(x, -1) * cw[1:2] + x * cw[2:3]
         + _shift_rows(x, 1) * cw[3:4])
    u = _silu(u)
    rows = slice(row0, row0 + length)
    for j in range(8):
        sl = slice(j * HD, (j + 1) * HD)
        xs = u[:, sl]
        nrm = lax.rsqrt(jnp.sum(xs * xs, axis=-1, keepdims=True) + EPS)
        qkv_ref[0, rows, sl] = xs * (nrm * SCALE if j < 4 else nrm)
    qkv_ref[0, rows, 512:768] = u[:, 512:768]
    ab = ab_ref[...]
    g = -jnp.exp(alog) * _softplus(ab + dtb)
    lane = lax.broadcasted_iota(jnp.int32, ab.shape, 1)
    gb_ref[0, rows, :] = jnp.where(lane < 8, g, _sigmoid(ab))


def _gdn_prep_kernel(pl_ref, abl_ref, pc_ref, abc_ref, cw_ref, alog_ref, dtb_ref, qkv_ref, gb_ref):
    cw, alog, dtb = cw_ref[...], alog_ref[...], dtb_ref[...]
    _gdn_prep_seq(pl_ref, abl_ref, cw, alog, dtb, qkv_ref, gb_ref, 0, S)
    _gdn_prep_seq(pc_ref, abc_ref, cw, alog, dtb, qkv_ref, gb_ref, S, LC)


def _gdn_prep(p_gd, p_ab, conv_w, a_log, dt_bias):
    const = lambda b: (0, 0)
    alog = jnp.pad(a_log.reshape(1, 8), ((0, 0), (0, 120)))
    dtb = jnp.pad(dt_bias.reshape(1, 8), ((0, 0), (0, 120)))
    return pl.pallas_call(
        _gdn_prep_kernel,
        grid=(B,),
        in_specs=[
            pl.BlockSpec((S, 1024), lambda b: (b, 0)),
            pl.BlockSpec((S, 128), lambda b: (b, 0)),
            pl.BlockSpec((LC, 1024), lambda b: (CTX_BLK0 + b, 0)),
            pl.BlockSpec((LC, 128), lambda b: (CTX_BLK0 + b, 0)),
            pl.BlockSpec((4, 768), const),
            pl.BlockSpec((1, 128), const),
            pl.BlockSpec((1, 128), const),
        ],
        out_specs=[pl.BlockSpec((1, S + LC, 768), lambda b: (b, 0, 0)),
                   pl.BlockSpec((1, S + LC, 128), lambda b: (b, 0, 0))],
        out_shape=[jax.ShapeDtypeStruct((B, S + LC, 768), F32), jax.ShapeDtypeStruct((B, S + LC, 128), F32)],
        compiler_params=_cp("arbitrary"),
        name="gdn_prep",
    )(p_gd, p_ab, p_gd, p_ab, conv_w, alog, dtb)


def _gdn_chains(xs, gbs, cums, cum_ts, states, masks):
    chains = [(g, h) for g in range(len(xs)) for h in range(4)]
    pre = []
    for g, h in chains:
        dirn = g % 2
        x, gb = xs[g], gbs[g]
        tri, strict = masks[dirn]
        col = dirn * 4 + h
        qh = x[:, h * HD:(h + 1) * HD]
        kh = x[:, 256 + h * HD:256 + (h + 1) * HD]
        vh = x[:, 512 + h * HD:512 + (h + 1) * HD]
        gc = cums[g][:, col:col + 1]
        gc_row = cum_ts[g][col:col + 1, :]
        beta = gb[:, 8 + col:9 + col]
        decay = jnp.exp(jnp.where(tri, gc - gc_row, NEG))
        kb = kh * beta
        eg = jnp.exp(gc)
        g_last = gc[0:1, :] if dirn else gc[CHUNK - 1:CHUNK, :]
        pre.append(dict(strict=strict, decay=decay, kbf=kb.astype(BF16), khf=kh.astype(BF16), qf=qh.astype(BF16),
                        xw=jnp.concatenate([vh * beta, kb * eg], axis=1), qdec=(qh * eg).astype(BF16),
                        kdec=(kh * jnp.exp(g_last - gc)).astype(BF16), gl=jnp.exp(g_last)))
    n = len(chains)
    gram = [_dot_nt(p["kbf"], p["khf"]) for p in pre]
    attn = [_dot_nt(p["qf"], p["khf"]) for p in pre]
    lmat = [jnp.where(p["strict"], g * p["decay"], 0.0) for p, g in zip(pre, gram)]
    attn = [(a * p["decay"]).astype(BF16) for p, a in zip(pre, attn)]
    ii = lax.broadcasted_iota(jnp.int32, (CHUNK, CHUNK), 0)
    jj = lax.broadcasted_iota(jnp.int32, (CHUNK, CHUNK), 1)

    def merged_off_blocks(log_s):
        return ((ii >> (log_s + 1)) == (jj >> (log_s + 1))) & ((ii >> log_s) != (jj >> log_s))

    eye = (ii == jj).astype(F32)
    dinv = [eye - jnp.where(merged_off_blocks(0), lm, 0.0) for lm in lmat]
    for log_s in range(1, 6):
        m = merged_off_blocks(log_s)
        dinv_b = [d.astype(BF16) for d in dinv]
        ld = [_dot(jnp.where(m, lmat[c], 0.0).astype(BF16), dinv_b[c]) for c in range(n)]
        upd = [_dot(dinv_b[c], ld[c].astype(BF16)) for c in range(n)]
        dinv = [dinv[c] - upd[c] for c in range(n)]
    xw = [_dot(dinv[c].astype(BF16), pre[c]["xw"].astype(BF16)) for c in range(n)]
    sts = [states[g][h] for g, h in chains]
    stb = [s.astype(BF16) for s in sts]
    ws = [_dot(xw[c][:, HD:].astype(BF16), stb[c]) for c in range(n)]
    qs = [_dot(pre[c]["qdec"], stb[c]) for c in range(n)]
    u_new = [(xw[c][:, :HD] - ws[c]).astype(BF16) for c in range(n)]
    au = [_dot(attn[c], u_new[c]) for c in range(n)]
    ku = [_dot_tn(pre[c]["kdec"], u_new[c]) for c in range(n)]
    outs = [qs[c] + au[c] for c in range(n)]
    new_states = [sts[c] * pre[c]["gl"] + ku[c] for c in range(n)]
    return outs, new_states


def _gdn_kernel(x0_ref, gb0_ref, x1_ref, gb1_ref, o0_ref, o1_ref, state_ref):
    @pl.when(pl.program_id(0) == 0)
    def _():
        state_ref[...] = jnp.zeros_like(state_ref)

    ii = lax.broadcasted_iota(jnp.int32, (CHUNK, CHUNK), 0)
    jj = lax.broadcasted_iota(jnp.int32, (CHUNK, CHUNK), 1)
    masks = ((jj <= ii, jj < ii), (jj >= ii, jj > ii))
    ones_tri = ((jj <= ii).astype(F32), (jj >= ii).astype(F32))

    def body(bp, carry):
        bs = [bp * GDN_SEQS + i for i in range(GDN_SEQS)]
        states = [[state_ref[b, dirn, h] for h in range(4)] for b in bs for dirn in range(2)]
        xs = [r[b] for b in bs for r in (x0_ref, x1_ref)]
        gbs = [r[b] for b in bs for r in (gb0_ref, gb1_ref)]
        cums = [_dot(ones_tri[g % 2], gbs[g], HI) for g in range(2 * GDN_SEQS)]
        cum_ts = [cm.T for cm in cums]
        outs, new_states = _gdn_chains(xs, gbs, cums, cum_ts, states, masks)
        for c, st in enumerate(new_states):
            state_ref[bs[c // 8], (c // 4) % 2, c % 4] = st
        for i, b in enumerate(bs):
            o0_ref[b] = jnp.concatenate(outs[8 * i:8 * i + 4], axis=1)
            o1_ref[b] = jnp.concatenate(outs[8 * i + 4:8 * i + 8], axis=1)
        return carry

    lax.fori_loop(0, B // GDN_SEQS, body, 0)


def _gdn_scan(qkv, gb):
    nc = (S + LC) // CHUNK
    fwd = lambda c: (0, (c + S // CHUNK) % nc, 0)
    bwd = lambda c: (0, nc - 1 - c, 0)
    return pl.pallas_call(
        _gdn_kernel,
        grid=(nc,),
        in_specs=[
            pl.BlockSpec((B, CHUNK, 768), fwd),
            pl.BlockSpec((B, CHUNK, 128), fwd),
            pl.BlockSpec((B, CHUNK, 768), bwd),
            pl.BlockSpec((B, CHUNK, 128), bwd),
        ],
        out_specs=[pl.BlockSpec((B, CHUNK, 256), fwd), pl.BlockSpec((B, CHUNK, 256), bwd)],
        out_shape=[jax.ShapeDtypeStruct((B, S + LC, 256), F32)] * 2,
        scratch_shapes=[pltpu.VMEM((B, 2, 4, HD, HD), F32)],
        compiler_params=_cp("arbitrary"),
        name="gdn_scan",
    )(qkv, gb, qkv, gb)


def _gdn_finish_kernel(o0_ref, o1_ref, gate_ref, ng_ref, y_ref):
    o = o0_ref[0] + o1_ref[0]
    gi = lax.broadcasted_iota(jnp.int32, (256, 256), 0) // HD
    gj = lax.broadcasted_iota(jnp.int32, (256, 256), 1) // HD
    group_mean = jnp.where(gi == gj, 1.0 / HD, 0.0)
    ms = _dot(o * o, group_mean, HI)
    y = o * lax.rsqrt(ms + EPS) * ng_ref[...]
    y_ref[...] = (y * _silu(gate_ref[...].astype(F32))).astype(y_ref.dtype)


def _gdn_finish(o0, o1, p_gd, norm_g, need_ctx):
    lat_tiles = S // TM
    row_blk = lambda b, j: jnp.where(j < lat_tiles, b * lat_tiles + j, CTX_BLK0 + b)
    return pl.pallas_call(
        _gdn_finish_kernel,
        grid=(B, lat_tiles + (1 if need_ctx else 0)),
        in_specs=[
            pl.BlockSpec((1, TM, 256), lambda b, j: (b, j, 0)),
            pl.BlockSpec((1, TM, 256), lambda b, j: (b, j, 0)),
            pl.BlockSpec((TM, 256), lambda b, j: (row_blk(b, j), 3)),
            pl.BlockSpec((1, 256), lambda b, j: (0, 0)),
        ],
        out_specs=pl.BlockSpec((TM, 256), lambda b, j: (row_blk(b, j), 0)),
        out_shape=jax.ShapeDtypeStruct((ROWS if need_ctx else NL, 256), BF16),
        compiler_params=_cp("arbitrary", "arbitrary"),
        name="gdn_finish",
    )(o0, o1, p_gd, jnp.tile(norm_g, 4)[None, :])


def _pack_bf16_pairs(x):
    n = x.shape[1] // 2
    lo = pltpu.bitcast(x[:, :n].astype(BF16).astype(F32), jnp.uint32)
    hi = pltpu.bitcast(x[:, n:].astype(BF16).astype(F32), jnp.uint32)
    return hi | (lo >> 16)


def _unpack_bf16_pairs(w):
    lo = pltpu.bitcast(w << 16, F32)
    hi = pltpu.bitcast(w & jnp.uint32(0xFFFF0000), F32)
    return jnp.concatenate([lo, hi], axis=1)


def _outproj_kernel(x_ref, mod_ref, g2_ref, yh_ref, yw_ref, yn_ref, yg_ref, wo_ref, wr_ref, eb_ref,
                    xo_ref, h2_ref, h2p_ref, idx_ref, tw_ref, rank_ref, cnt_ref, carry_ref):
    @pl.when(pl.program_id(0) == 0)
    def _():
        carry_ref[...] = jnp.zeros_like(carry_ref)

    m = mod_ref[0]
    acc = (_dot(yh_ref[...], wo_ref[0:256, :]) + _dot(yw_ref[...], wo_ref[256:512, :])
           + _dot(yn_ref[...], wo_ref[512:768, :]) + _dot(yg_ref[...], wo_ref[768:1024, :]))
    x = x_ref[...] + m[2:3] * acc
    xo_ref[...] = x
    h2 = _modulated_norm(x, g2_ref[...], m[3:4], m[4:5])
    h2_ref[...] = h2.astype(h2_ref.dtype)
    h2p_ref[...] = _pack_bf16_pairs(h2)
    scores = _sigmoid(_dot(h2, wr_ref[...], HI))
    lane = lax.broadcasted_iota(jnp.int32, scores.shape, 1)
    sel = jnp.where(lane < N_EXP, scores + eb_ref[...], -jnp.inf)
    lane_f = lane.astype(F32)
    idx_out = jnp.zeros(scores.shape, F32)
    s_out = jnp.zeros(scores.shape, F32)
    hits = []
    for kk in range(TOP_K):
        mx = jnp.max(sel, axis=-1, keepdims=True)
        idx = jnp.min(jnp.where(sel == mx, lane_f, 128.0), axis=-1, keepdims=True)
        hit = lane_f == idx
        sk = jnp.sum(jnp.where(hit, scores, 0.0), axis=-1, keepdims=True)
        sel = jnp.where(hit, -jnp.inf, sel)
        idx_out = jnp.where(lane == kk, idx, idx_out)
        s_out = jnp.where(lane == kk, sk, s_out)
        hits.append(hit)
    tot = jnp.sum(s_out, axis=-1, keepdims=True)
    tw_ref[...] = s_out / tot * ROUTED_SCALE
    cnt = jnp.zeros(scores.shape, F32)
    for hit in hits:
        cnt = cnt + jnp.where(hit, 1.0, 0.0)
    ri = lax.broadcasted_iota(jnp.int32, (TM, TM), 0)
    rj = lax.broadcasted_iota(jnp.int32, (TM, TM), 1)
    before = _dot(jnp.where(rj < ri, 1.0, 0.0).astype(BF16), cnt.astype(BF16)) + carry_ref[...]
    rank_out = jnp.zeros(scores.shape, F32)
    for kk, hit in enumerate(hits):
        rk = jnp.sum(jnp.where(hit, before, 0.0), axis=-1, keepdims=True)
        rank_out = jnp.where(lane == kk, rk, rank_out)
    idx_ref[...] = idx_out.T[:8].astype(jnp.int32)
    rank_ref[...] = rank_out.T[:8].astype(jnp.int32)
    carry_ref[...] = carry_ref[...] + jnp.sum(cnt, axis=0, keepdims=True)
    cnt_ref[...] = carry_ref[...].astype(jnp.int32)


def _outproj(xa, mod, l, g2, ys, wo, wr, eb, n_tiles):
    row = lambda i: (i, 0)
    const = lambda i: (0, 0)
    rows = n_tiles * TM
    return pl.pallas_call(
        _outproj_kernel,
        grid=(n_tiles,),
        in_specs=[
            pl.BlockSpec((TM, D), row),
            pl.BlockSpec((1, 6, D), lambda i: (l * 16 + (i * TM) // S, 0, 0)),
            pl.BlockSpec((1, D), const),
            pl.BlockSpec((TM, 256), row),
            pl.BlockSpec((TM, 256), row),
            pl.BlockSpec((TM, 256), row),
            pl.BlockSpec((TM, 256), row),
            pl.BlockSpec((D, D), const),
            pl.BlockSpec((D, 128), const),
            pl.BlockSpec((1, 128), const),
        ],
        out_specs=[
            pl.BlockSpec((TM, D), row),
            pl.BlockSpec((TM, D), row),
            pl.BlockSpec((TM, D // 2), row),
            pl.BlockSpec((8, TM), row),
            pl.BlockSpec((TM, 128), row),
            pl.BlockSpec((8, TM), row),
            pl.BlockSpec((1, 128), const),
        ],
        out_shape=[
            jax.ShapeDtypeStruct((rows, D), F32),
            jax.ShapeDtypeStruct((rows, D), BF16),
            jax.ShapeDtypeStruct((rows, D // 2), jnp.uint32),
            jax.ShapeDtypeStruct((n_tiles * 8, TM), jnp.int32),
            jax.ShapeDtypeStruct((rows, 128), F32),
            jax.ShapeDtypeStruct((n_tiles * 8, TM), jnp.int32),
            jax.ShapeDtypeStruct((1, 128), jnp.int32),
        ],
        scratch_shapes=[pltpu.VMEM((1, 128), F32)],
        compiler_params=_cp("arbitrary"),
        name="outproj_router",
    )(xa, mod, g2, *ys, wo, wr, eb)


ASG_TILE = TM * TOP_K


def _segment_tables(counts, n_blocks):
    counts = counts[0, :N_EXP]
    padded = (counts + MOE_BLK - 1) // MOE_BLK * MOE_BLK
    pad_end = jnp.cumsum(padded)
    seg_start = pad_end - padded
    starts = jnp.arange(n_blocks, dtype=jnp.int32) * MOE_BLK
    blk_e = jnp.minimum(jnp.sum((pad_end[None, :] <= starts[:, None]).astype(jnp.int32), axis=1), N_EXP - 1)
    seg = jnp.concatenate([seg_start, seg_start + counts, pad_end[-1:]])
    blk = jnp.concatenate([blk_e, pad_end[-1:] // MOE_BLK])
    return seg.astype(jnp.int32), blk.astype(jnp.int32)


def _row_copy_wait(shape_ref, dst_ref, sem):
    pltpu.make_async_copy(shape_ref, dst_ref, sem).wait()


def _zero_fill_rows(zero_ref, xb_ref, start, length, sem):
    plan = []
    aligned = (start + 7) & -8
    for r in range(7):
        plan.append(((start + r < aligned) & (r < length),
                     pltpu.make_async_copy(zero_ref.at[pl.ds(0, 1)], xb_ref.at[pl.ds(start + r, 1)], sem)))
    rest = start + length - aligned
    off = aligned
    for bit in reversed(range(3, MOE_BLK.bit_length() - 1)):
        size = 1 << bit
        take = (rest & size) != 0
        dst = xb_ref.at[pl.ds(pl.multiple_of(off, 8), size)]
        plan.append((take, pltpu.make_async_copy(zero_ref.at[pl.ds(0, size)], dst, sem)))
        off = off + jnp.where(take, size, 0)
    for take, copy in plan:
        pl.when(take)(copy.start)
    for take, copy in plan:
        pl.when(take)(copy.wait)


def _dispatch_kernel(seg_ref, idx_ref, rank_ref, h2p_ref, xb_ref, zero_ref, sem, zsem):
    i = pl.program_id(0)

    @pl.when(i == 0)
    def _():
        zero_ref[...] = jnp.zeros_like(zero_ref)

    @pl.when(i < N_EXP)
    def _():
        first_pad = seg_ref[N_EXP + i]
        next_start = seg_ref[jnp.where(i == N_EXP - 1, 2 * N_EXP, i + 1)]
        _zero_fill_rows(zero_ref, xb_ref, first_pad, next_start - first_pad, zsem)

    tail_row = seg_ref[2 * N_EXP] + (i - N_EXP) * MOE_BLK

    @pl.when((i >= N_EXP) & (tail_row < xb_ref.shape[0]))
    def _():
        copy = pltpu.make_async_copy(zero_ref, xb_ref.at[pl.ds(pl.multiple_of(tail_row, MOE_BLK), MOE_BLK)], zsem)
        copy.start()
        copy.wait()

    def issue(t, carry):
        src = h2p_ref.at[pl.ds(t, 1)]
        for k in range(TOP_K):
            dst = seg_ref[idx_ref[k, t]] + rank_ref[k, t]
            pltpu.make_async_copy(src, xb_ref.at[pl.ds(dst, 1)], sem).start()
        return carry

    lax.fori_loop(0, TM, issue, 0, unroll=True)
    for _ in range(TOP_K):
        _row_copy_wait(h2p_ref, xb_ref.at[pl.ds(0, TM)], sem)


def _dispatch_rows(seg_start, idx_c, rank_c, h2p, n_rows):
    n_tiles = h2p.shape[0] // TM
    assert n_tiles >= 2 * N_EXP
    grid_spec = pltpu.PrefetchScalarGridSpec(
        num_scalar_prefetch=1,
        grid=(n_tiles,),
        in_specs=[
            pl.BlockSpec((8, TM), lambda i, seg: (i, 0), memory_space=pltpu.SMEM),
            pl.BlockSpec((8, TM), lambda i, seg: (i, 0), memory_space=pltpu.SMEM),
            pl.BlockSpec((TM, D // 2), lambda i, seg: (i, 0)),
        ],
        out_specs=pl.BlockSpec(memory_space=pl.ANY),
        scratch_shapes=[pltpu.VMEM((MOE_BLK, D // 2), jnp.uint32), pltpu.SemaphoreType.DMA(()),
                        pltpu.SemaphoreType.DMA(())],
    )
    return pl.pallas_call(
        _dispatch_kernel,
        grid_spec=grid_spec,
        out_shape=jax.ShapeDtypeStruct((n_rows, D // 2), jnp.uint32),
        compiler_params=_cp("arbitrary"),
        name="moe_dispatch",
    )(seg_start, idx_c, rank_c, h2p)


def _moe_kernel(be_ref, x_ref, wg_ref, wu_ref, wd_ref, o_ref, wgb_ref, wub_ref, wdb_ref):
    i = pl.program_id(0)
    in_use = i < be_ref[pl.num_programs(0)]

    @pl.when(in_use & ((i == 0) | (be_ref[i] != be_ref[jnp.maximum(i - 1, 0)])))
    def _():
        wgb_ref[...] = wg_ref[0, 0].astype(BF16)
        wub_ref[...] = wu_ref[0, 0].astype(BF16)
        wdb_ref[...] = wd_ref[0, 0].astype(BF16)

    @pl.when(in_use)
    def _():
        x = _unpack_bf16_pairs(x_ref[...]).astype(BF16)
        g = _dot(x, wgb_ref[...])
        u = _dot(x, wub_ref[...])
        hid = (_silu(g) * u).astype(BF16)
        o_ref[...] = _pack_bf16_pairs(_dot(hid, wdb_ref[...]))

    @pl.when(jnp.logical_not(in_use))
    def _():
        o_ref[...] = jnp.zeros_like(o_ref)


def _moe_experts(blk_e, xb, l, wg, wu, wd):
    n_rows = xb.shape[0]
    grid_spec = pltpu.PrefetchScalarGridSpec(
        num_scalar_prefetch=1,
        grid=(n_rows // MOE_BLK,),
        in_specs=[
            pl.BlockSpec((MOE_BLK, D // 2), lambda i, be: (i, 0)),
            pl.BlockSpec((1, 1, D, D_EXP), lambda i, be: (l, be[i], 0, 0)),
            pl.BlockSpec((1, 1, D, D_EXP), lambda i, be: (l, be[i], 0, 0)),
            pl.BlockSpec((1, 1, D_EXP, D), lambda i, be: (l, be[i], 0, 0)),
        ],
        out_specs=pl.BlockSpec((MOE_BLK, D // 2), lambda i, be: (i, 0)),
        scratch_shapes=[pltpu.VMEM((D, D_EXP), BF16), pltpu.VMEM((D, D_EXP), BF16), pltpu.VMEM((D_EXP, D), BF16)],
    )
    return pl.pallas_call(
        _moe_kernel,
        grid_spec=grid_spec,
        out_shape=jax.ShapeDtypeStruct((n_rows, D // 2), jnp.uint32),
        compiler_params=_cp("arbitrary"),
        name="moe_experts",
    )(blk_e, xb, wg, wu, wd)


def _ffn_out_kernel(seg_ref, idx_ref, rank_ref, x_ref, mod_ref, h2_ref, tw_ref, wg_ref, wu_ref, wd_ref, nf_ref,
                    yb_ref, o_ref, gat_ref, sem, *, final):
    def issue(t, carry):
        for k in range(TOP_K):
            src = seg_ref[idx_ref[k, t]] + rank_ref[k, t]
            pltpu.make_async_copy(yb_ref.at[pl.ds(src, 1)], gat_ref.at[pl.ds(k * TM + t, 1)], sem).start()
        return carry

    lax.fori_loop(0, TM, issue, 0, unroll=True)
    m = mod_ref[0]
    h2 = h2_ref[...]
    hid = (_silu(_dot(h2, wg_ref[...])) * _dot(h2, wu_ref[...])).astype(BF16)
    shared = _dot(hid, wd_ref[...])
    for _ in range(TOP_K):
        _row_copy_wait(yb_ref.at[pl.ds(0, TM)], gat_ref.at[pl.ds(0, TM)], sem)
    tw = tw_ref[...]
    routed = _unpack_bf16_pairs(gat_ref[0:TM, :]) * tw[:, 0:1]
    for k in range(1, TOP_K):
        routed = routed + _unpack_bf16_pairs(gat_ref[k * TM:(k + 1) * TM, :]) * tw[:, k:k + 1]
    x = x_ref[...] + m[5:6] * (routed + shared)
    if final:
        ms = jnp.mean(x * x, axis=-1, keepdims=True)
        x = x * lax.rsqrt(ms + EPS) * nf_ref[...]
    o_ref[...] = x


def _ffn_out(seg_start, idx_c, rank_c, xn, mod, l, h2, top_w, wg, wu, wd, norm_f, yb, n_tiles, final):
    row = lambda i, seg: (i, 0)
    const = lambda i, seg: (0, 0)
    grid_spec = pltpu.PrefetchScalarGridSpec(
        num_scalar_prefetch=1,
        grid=(n_tiles,),
        in_specs=[
            pl.BlockSpec((8, TM), lambda i, seg: (i, 0), memory_space=pltpu.SMEM),
            pl.BlockSpec((8, TM), lambda i, seg: (i, 0), memory_space=pltpu.SMEM),
            pl.BlockSpec((TM, D), row),
            pl.BlockSpec((1, 6, D), lambda i, seg: (l * 16 + (i * TM) // S, 0, 0)),
            pl.BlockSpec((TM, D), row),
            pl.BlockSpec((TM, 128), row),
            pl.BlockSpec((D, D_EXP), const),
            pl.BlockSpec((D, D_EXP), const),
            pl.BlockSpec((D_EXP, D), const),
            pl.BlockSpec((1, D), const),
            pl.BlockSpec(memory_space=pl.ANY),
        ],
        out_specs=pl.BlockSpec((TM, D), row),
        scratch_shapes=[pltpu.VMEM((ASG_TILE, D // 2), jnp.uint32), pltpu.SemaphoreType.DMA(())],
    )
    return pl.pallas_call(
        functools.partial(_ffn_out_kernel, final=final),
        grid_spec=grid_spec,
        out_shape=jax.ShapeDtypeStruct((n_tiles * TM, D), F32),
        compiler_params=_cp("arbitrary"),
        name="shared_ffn_residual",
    )(seg_start, idx_c, rank_c, xn, mod, h2, top_w, wg, wu, wd, norm_f, yb)


def kernel(x, c, ctx, c_ctx, w_ada, b_ada, norm1, norm2, norm_f, w_in, w_out, hy_conv, hy_w1, hy_b1, hy_w2, hy_b2, hy_w3, hy_freq, hy_bias, wa_sink, na_rpb, gdn_conv, gdn_a_log, gdn_dt_bias, gdn_norm, moe_router, moe_bias, moe_gate, moe_up, moe_down, sh_gate, sh_up, sh_down):
    depth = w_ada.shape[0]
    xa = jnp.concatenate([x.reshape(NL, D), ctx.reshape(NC, D)], axis=0)
    cvec = jnp.concatenate([c, c_ctx[None, :], jnp.zeros((16 - B - 1, D), F32)], axis=0)
    mod = _ada(cvec, w_ada, b_ada).reshape(depth * 16, 6, D)
    rope = _rope_tables()
    dft_lat = _dft_tables(S)
    dft_ctx = _dft_tables(LC)
    o1, o2, o3 = 768, 768 + 512, 768 + 512 + 768

    for l in range(depth):
        need_ctx = l < depth - 1
        n_tiles = NT_ALL if need_ctx else NT_LAT
        wl = w_in[l].astype(BF16)
        ws = (wl[:, :o1], wl[:, o1:o2], wl[:, o2:o3], wl[:, o3:o3 + 1024],
              jnp.pad(wl[:, o3 + 1024:], ((0, 0), (0, 128 - 16))))
        p_hy, p_wa, p_na, p_gd, p_ab = _inproj(xa, mod, l, norm1[l][None, :], ws, rope)

        filt = (hy_w1[l], hy_b1[l], hy_w2[l], hy_b2[l], hy_w3[l], hy_freq[l])
        kre, kim = _hyena_filter_spectrum(S, *filt, dft_lat)
        y_hy = _hyena(p_hy, S, 0, hy_conv[l], hy_bias[l], dft_lat, kre, kim)
        y_wa = _window_attention(p_wa, wa_sink[l])
        y_na = _neighborhood_attention(p_na, _na_bias_table(na_rpb[l]))
        if need_ctx:
            kre_c, kim_c = _hyena_filter_spectrum(LC, *filt, dft_ctx)
            yc_hy = _hyena(p_hy, LC, CTX_BLK0, hy_conv[l], hy_bias[l], dft_ctx, kre_c, kim_c)
            yc_wa, yc_na = _ctx_attention(p_wa, p_na, wa_sink[l])
            y_hy = jnp.concatenate([y_hy, yc_hy], axis=0)
            y_wa = jnp.concatenate([y_wa, yc_wa], axis=0)
            y_na = jnp.concatenate([y_na, yc_na], axis=0)
        qkv, gb = _gdn_prep(p_gd, p_ab, gdn_conv[l], gdn_a_log[l], gdn_dt_bias[l])
        y_gd = _gdn_finish(*_gdn_scan(qkv, gb), p_gd, gdn_norm[l], need_ctx)

        wr = jnp.pad(moe_router[l], ((0, 0), (0, 128 - N_EXP)))
        eb = jnp.pad(moe_bias[l], (0, 128 - N_EXP))[None, :]
        xn, h2, h2p, top_idx, top_w, rank, counts = _outproj(
            xa, mod, l, norm2[l][None, :], (y_hy, y_wa, y_na, y_gd), w_out[l].astype(BF16), wr, eb, n_tiles)
        n_asg = n_tiles * ASG_TILE
        n_blocks = n_asg // MOE_BLK + N_EXP
        seg_start, blk_e = _segment_tables(counts, n_blocks)
        xb = _dispatch_rows(seg_start, top_idx, rank, h2p, n_blocks * MOE_BLK)
        yb = _moe_experts(blk_e, xb, l, moe_gate, moe_up, moe_down)
        xa_new = _ffn_out(seg_start, top_idx, rank, xn, mod, l, h2, top_w, sh_gate[l].astype(BF16),
                          sh_up[l].astype(BF16), sh_down[l].astype(BF16), norm_f[None, :], yb, n_tiles,
                          final=not need_ctx)
        if need_ctx:
            xa = xa_new
        else:
            return xa_new.reshape(B, S, D)
```

```python
import functools
import math

import jax
import jax.numpy as jnp
from jax import lax
from jax.experimental import pallas as pl
from jax.experimental.pallas import tpu as pltpu

F32 = jnp.float32
BF16 = jnp.bfloat16
HI = lax.Precision.HIGHEST

D = 1024
B = 8
S = 2048
LC = 256
GW = 64
HD = 64
NL = B * S
NC = B * LC
ROWS = NL + NC
TM = 256
NT_LAT = NL // TM
NT_ALL = ROWS // TM
CTX_BLK0 = NL // LC

HY_CH = 256
HY_BANDS = 16
HY_DECAY_MIN = -math.log(1e-2) / 1.5
HY_DECAY_MAX = -math.log(1e-2) / 0.3
WINDOW = 128
NA_KR = 8
NA_KC = 16
CHUNK = 64
N_EXP = 32
TOP_K = 4
D_EXP = 256
ROUTED_SCALE = 2.5
MOE_BLK = 512
WA_BLKS = 2
NA_ROWS = 4
GDN_SEQS = 4
EPS = 1e-6
NEG = -1e30
SCALE = HD ** -0.5
VMEM_LIMIT = 56 * 1024 * 1024


def _cp(*sem):
    return pltpu.CompilerParams(dimension_semantics=tuple(sem), vmem_limit_bytes=VMEM_LIMIT)


def _dot(a, b, precision=None):
    return jnp.dot(a, b, preferred_element_type=F32, precision=precision)


def _dot_nt(a, b, precision=None):
    return lax.dot_general(a, b, (((1,), (1,)), ((), ())), preferred_element_type=F32, precision=precision)


def _dot_tn(a, b, precision=None):
    return lax.dot_general(a, b, (((0,), (0,)), ((), ())), preferred_element_type=F32, precision=precision)


def _sigmoid(x):
    return 1.0 / (1.0 + jnp.exp(-x))


def _silu(x):
    return x * _sigmoid(x)


def _softplus(x):
    return jnp.maximum(x, 0.0) + jnp.log(1.0 + jnp.exp(-jnp.abs(x)))


def _shift_rows(x, d):
    n = x.shape[0]
    if d == 0:
        return x
    y = pltpu.roll(x, (-d) % n, axis=0)
    t = lax.broadcasted_iota(jnp.int32, x.shape, 0)
    ok = (t + d >= 0) & (t + d < n)
    return jnp.where(ok, y, 0.0)


def _ada_kernel(c_ref, w_ref, b_ref, o_ref):
    s = _silu(c_ref[...])
    o_ref[0] = _dot(s.astype(BF16), w_ref[0].astype(BF16)) + b_ref[0]


def _ada(cvec, w_ada, b_ada):
    nl = w_ada.shape[0]
    tn = 1536
    return pl.pallas_call(
        _ada_kernel,
        grid=(nl, 6 * D // tn),
        in_specs=[
            pl.BlockSpec((16, D), lambda l, j: (0, 0)),
            pl.BlockSpec((1, D, tn), lambda l, j: (l, 0, j)),
            pl.BlockSpec((1, 1, tn), lambda l, j: (l, 0, j)),
        ],
        out_specs=pl.BlockSpec((1, 16, tn), lambda l, j: (l, 0, j)),
        out_shape=jax.ShapeDtypeStruct((nl, 16, 6 * D), F32),
        compiler_params=_cp("arbitrary", "arbitrary"),
        name="adaln",
    )(cvec, w_ada, b_ada.reshape(nl, 1, 6 * D))


def _modulated_norm(x, g, shift, scale):
    ms = jnp.mean(x * x, axis=-1, keepdims=True)
    y = x * lax.rsqrt(ms + EPS) * g
    return y * (1.0 + scale) + shift


def _inproj_kernel(x_ref, mod_ref, g_ref, why_ref, wwa_ref, wna_ref, wgd_ref, wab_ref,
                   cos_ref, sa_ref, sb_ref, ohy, owa, ona, ogd, oab):
    m = mod_ref[0]
    h = _modulated_norm(x_ref[...], g_ref[...], m[0:1], m[1:2]).astype(BF16)
    ohy[...] = _dot(h, why_ref[...]).astype(ohy.dtype)
    ona[...] = _dot(h, wna_ref[...]).astype(ona.dtype)
    ogd[...] = _dot(h, wgd_ref[...]).astype(ogd.dtype)
    oab[...] = _dot(h, wab_ref[...])
    a = _dot(h, wwa_ref[...])
    for c in range(4):
        sl = slice(c * 128, (c + 1) * 128)
        ac = a[:, sl]
        r = (ac * cos_ref[:, sl] + pltpu.roll(ac, 112, axis=1) * sa_ref[:, sl]
             + pltpu.roll(ac, 16, axis=1) * sb_ref[:, sl])
        owa[:, sl] = r.astype(owa.dtype)


def _inproj(xa, mod, l, g, ws, tabs):
    why, wwa, wna, wgd, wab = ws
    cos_t, sa_t, sb_t = tabs
    row = lambda i: (i, 0)
    const = lambda i: (0, 0)
    tab = lambda i: (jnp.where(i < NT_LAT, i % (S // TM), S // TM), 0)
    return pl.pallas_call(
        _inproj_kernel,
        grid=(NT_ALL,),
        in_specs=[
            pl.BlockSpec((TM, D), row),
            pl.BlockSpec((1, 6, D), lambda i: (l * 16 + (i * TM) // S, 0, 0)),
            pl.BlockSpec((1, D), const),
            pl.BlockSpec((D, 768), const),
            pl.BlockSpec((D, 512), const),
            pl.BlockSpec((D, 768), const),
            pl.BlockSpec((D, 1024), const),
            pl.BlockSpec((D, 128), const),
            pl.BlockSpec((TM, 512), tab),
            pl.BlockSpec((TM, 512), tab),
            pl.BlockSpec((TM, 512), tab),
        ],
        out_specs=[
            pl.BlockSpec((TM, 768), row),
            pl.BlockSpec((TM, 512), row),
            pl.BlockSpec((TM, 768), row),
            pl.BlockSpec((TM, 1024), row),
            pl.BlockSpec((TM, 128), row),
        ],
        out_shape=[
            jax.ShapeDtypeStruct((ROWS, 768), BF16),
            jax.ShapeDtypeStruct((ROWS, 512), BF16),
            jax.ShapeDtypeStruct((ROWS, 768), BF16),
            jax.ShapeDtypeStruct((ROWS, 1024), BF16),
            jax.ShapeDtypeStruct((ROWS, 128), F32),
        ],
        compiler_params=_cp("arbitrary"),
        name="inproj",
    )(xa, mod, g, why, wwa, wna, wgd, wab, cos_t, sa_t, sb_t)


def _rope_tables():
    quarter = HD // 4
    pos = jnp.arange(S)
    inv = 10000.0 ** (-jnp.arange(quarter, dtype=F32) / quarter)
    ang_r = (pos // GW).astype(F32)[:, None] * inv[None, :]
    ang_c = (pos % GW).astype(F32)[:, None] * inv[None, :]
    z = jnp.zeros_like(ang_r)
    cos_h = jnp.concatenate([jnp.cos(ang_r)] * 2 + [jnp.cos(ang_c)] * 2, axis=1)
    sa_h = jnp.concatenate([-jnp.sin(ang_r), z, -jnp.sin(ang_c), z], axis=1)
    sb_h = jnp.concatenate([z, jnp.sin(ang_r), z, jnp.sin(ang_c)], axis=1)
    ones = jnp.ones((S, 128), F32)
    zeros = jnp.zeros((S, 128), F32)
    cos_t = jnp.concatenate([jnp.tile(cos_h, (1, 6)), ones], axis=1)
    sa_t = jnp.concatenate([jnp.tile(sa_h, (1, 6)), zeros], axis=1)
    sb_t = jnp.concatenate([jnp.tile(sb_h, (1, 6)), zeros], axis=1)
    ident = jnp.ones((TM, 512), F32)
    none = jnp.zeros((TM, 512), F32)
    return (jnp.concatenate([cos_t, ident], axis=0), jnp.concatenate([sa_t, none], axis=0),
            jnp.concatenate([sb_t, none], axis=0))


def _dft_tables(length):
    n = 2 * length
    f = jnp.arange(length, dtype=jnp.int32)
    m = ((2 * f[:, None] + 1) * (2 * f[None, :] + 1)) % (4 * n)
    th = m.astype(F32) * (2.0 * math.pi / (4 * n))
    phi = (2 * f + 1).astype(F32) * (math.pi / (2 * n))
    return (jnp.cos(th).astype(BF16), jnp.sin(th).astype(BF16),
            jnp.cos(phi)[:, None], jnp.sin(phi)[:, None])


def _hyena_features(length):
    t = jnp.arange(length, dtype=F32)
    t_norm = t / max(length - 1, 1)
    bands = jnp.linspace(1e-4, HY_BANDS - 1, HY_BANDS, dtype=F32)
    ang = (2.0 * math.pi / length) * t[:, None] * bands[None, :]
    z = jnp.concatenate([t_norm[:, None], jnp.cos(ang), -jnp.sin(ang)], axis=-1)
    z = jnp.pad(z, ((0, 0), (0, 128 - z.shape[1])))
    decay = jnp.tile(jnp.linspace(HY_DECAY_MIN, HY_DECAY_MAX, HY_CH, dtype=F32), 2)
    return z, jnp.exp(-t_norm[:, None] * decay[None, :])


def _hyfilt_kernel(z_ref, w1_ref, b1_ref, w2_ref, b2_ref, w3_ref, fr_ref, dec_ref,
                   c_ref, s_ref, cp_ref, sp_ref, kre_ref, kim_ref, pq_ref, *, length):
    @pl.when(pl.program_id(0) == 0)
    def _():
        fr = fr_ref[...]
        h = jnp.sin(fr * (_dot(z_ref[...], w1_ref[...], HI) + b1_ref[...]))
        h = jnp.sin(fr * (_dot(h, w2_ref[...], HI) + b2_ref[...]))
        h = _dot(h, w3_ref[...], HI) * dec_ref[...]
        hf = h[:, :HY_CH]
        t = lax.broadcasted_iota(jnp.int32, (length, HY_CH), 0)
        hb = jnp.where(t == 0, 0.0, h[:, HY_CH:])
        pq_ref[:, :HY_CH] = (hf + hb).astype(BF16)
        pq_ref[:, HY_CH:] = (hb - hf).astype(BF16)

    pq = pq_ref[...]
    cpq = _dot(c_ref[...], pq)
    spq = _dot(s_ref[...], pq)
    cphi, sphi = cp_ref[...], sp_ref[...]
    norm = 1.0 / length
    kre_ref[...] = (cphi * cpq[:, :HY_CH] + sphi * spq[:, :HY_CH]) * norm
    kim_ref[...] = (cphi * spq[:, HY_CH:] - sphi * cpq[:, HY_CH:]) * norm


def _hyena_filter_spectrum(length, w1, b1, w2, b2, w3, freq, dft):
    z, dec = _hyena_features(length)
    c_m, s_m, cphi, sphi = dft
    w1p = jnp.pad(w1, ((0, 128 - w1.shape[0]), (0, 0)))
    tf = min(512, length)
    const = lambda j: (0, 0)
    blk = lambda j: (j, 0)
    return pl.pallas_call(
        functools.partial(_hyfilt_kernel, length=length),
        grid=(length // tf,),
        in_specs=[
            pl.BlockSpec((length, 128), const),
            pl.BlockSpec((128, 64), const),
            pl.BlockSpec((1, 64), const),
            pl.BlockSpec((64, 64), const),
            pl.BlockSpec((1, 64), const),
            pl.BlockSpec((64, 2 * HY_CH), const),
            pl.BlockSpec((1, 64), const),
            pl.BlockSpec((length, 2 * HY_CH), const),
            pl.BlockSpec((tf, length), blk),
            pl.BlockSpec((tf, length), blk),
            pl.BlockSpec((tf, 1), blk),
            pl.BlockSpec((tf, 1), blk),
        ],
        out_specs=[pl.BlockSpec((tf, HY_CH), blk)] * 2,
        out_shape=[jax.ShapeDtypeStruct((length, HY_CH), F32)] * 2,
        scratch_shapes=[pltpu.VMEM((length, 2 * HY_CH), BF16)],
        compiler_params=_cp("arbitrary"),
        name=f"hyena_filter_{length}",
    )(z, w1p, b1[None, :], w2, b2[None, :], w3, freq[None, :], dec, c_m, s_m, cphi, sphi)


def _short_conv3(x_ref, cw):
    x = x_ref[...].astype(F32)
    return _shift_rows(x, -1) * cw[0:1] + x * cw[1:2] + _shift_rows(x, 1) * cw[2:3]


def _hyena_kernel(p0_ref, p1_ref, p2_ref, cw_ref, bias_ref, c_ref, s_ref, kre_ref, kim_ref, o_ref):
    cw = cw_ref[...]
    x0 = _short_conv3(p0_ref, cw[:, :HY_CH])
    k = _short_conv3(p1_ref, cw[:, HY_CH:2 * HY_CH]) * _short_conv3(p2_ref, cw[:, 2 * HY_CH:])
    kb = k.astype(BF16)
    a = _dot(c_ref[...], kb)
    b = _dot(s_ref[...], kb)
    kre, kim = kre_ref[...], kim_ref[...]
    yre = (a * kre + b * kim).astype(BF16)
    yim = (b * kre - a * kim).astype(BF16)
    y = _dot(c_ref[...], yre) + _dot(s_ref[...], yim)
    o_ref[...] = (x0 * (y + k * bias_ref[...])).astype(o_ref.dtype)


def _hyena(p_hy, length, blk0, conv_w, bias, dft, kre, kim):
    c_m, s_m = dft[0], dft[1]
    const = lambda b: (0, 0)
    once = pl.Buffered(1)
    in_specs = [
        pl.BlockSpec((length, HY_CH), lambda b: (blk0 + b, 0)),
        pl.BlockSpec((length, HY_CH), lambda b: (blk0 + b, 1)),
        pl.BlockSpec((length, HY_CH), lambda b: (blk0 + b, 2)),
        pl.BlockSpec((3, 768), const),
        pl.BlockSpec((1, HY_CH), const),
        pl.BlockSpec((length, length), const, pipeline_mode=once),
        pl.BlockSpec((length, length), const, pipeline_mode=once),
        pl.BlockSpec((length, HY_CH), const, pipeline_mode=once),
        pl.BlockSpec((length, HY_CH), const, pipeline_mode=once),
    ]
    return pl.pallas_call(
        _hyena_kernel,
        grid=(B,),
        in_specs=in_specs,
        out_specs=pl.BlockSpec((length, HY_CH), lambda b: (b, 0)),
        out_shape=jax.ShapeDtypeStruct((B * length, HY_CH), BF16),
        compiler_params=_cp("arbitrary"),
        name=f"hyena_{length}",
    )(p_hy, p_hy, p_hy, conv_w, bias[None, :], c_m, s_m, kre, kim)


def _ones_half(vwin, half):
    lane = lax.broadcasted_iota(jnp.int32, vwin.shape, 1)
    keep = (lane < HD) if half == 0 else (lane >= HD)
    return jnp.where(keep, vwin, jnp.ones_like(vwin))


def _softmax_pv(heads):
    probs, maxes = [], []
    for parts, sink, _ in heads:
        m = None
        for s, _ in parts:
            mi = jnp.max(s, axis=-1, keepdims=True)
            m = mi if m is None else jnp.maximum(m, mi)
        if sink is not None:
            m = jnp.maximum(m, sink)
        probs.append([jnp.exp((s - m).astype(BF16)) for s, _ in parts])
        maxes.append(m)
    outs = []
    for (parts, sink, half), es, m in zip(heads, probs, maxes):
        acc = None
        for e, (_, v) in zip(es, parts):
            o = _dot(e, v)
            acc = o if acc is None else acc + o
        den = acc[:, HD:HD + 1] if half == 0 else acc[:, 0:1]
        if sink is not None:
            den = den + jnp.exp(sink - m)
        outs.append(acc / den)
    return outs


def _wa_kernel(sink_ref, q_ref, k_ref, v_ref, kc_ref, vc_ref, o_ref):
    kc, vc = kc_ref[...], vc_ref[...]
    vcs = [_ones_half(vc, hk) for hk in range(2)]
    rr = lax.broadcasted_iota(jnp.int32, (256, 384), 0)
    r1 = lax.broadcasted_iota(jnp.int32, (256, 1), 0)
    heads = []
    for j in range(WA_BLKS):
        n = pl.program_id(1) * WA_BLKS + j
        start = pl.multiple_of(jnp.clip((n - 1) * 128, 0, S - 384), 128)
        kw = k_ref[pl.ds(start, 384), :]
        vw = v_ref[pl.ds(start, 384), :]
        q = q_ref[j * 128:(j + 1) * 128, :] * SCALE
        qpos = n * 128 + jnp.where(rr >= 128, rr - 128, rr)
        kpos = start + lax.broadcasted_iota(jnp.int32, (256, 384), 1)
        valid = jnp.abs(qpos - kpos) <= WINDOW
        for hk in range(2):
            q2 = jnp.concatenate([q[:, (2 * hk) * HD:(2 * hk + 1) * HD],
                                  q[:, (2 * hk + 1) * HD:(2 * hk + 2) * HD]], axis=0)
            hs = slice(hk * HD, (hk + 1) * HD)
            s_loc = jnp.where(valid, _dot_nt(q2, kw[:, hs]), NEG)
            s_ctx = _dot_nt(q2, kc[:, hs])
            sink = jnp.where(r1 >= 128, sink_ref[2 * hk + 1], sink_ref[2 * hk])
            heads.append(([(s_ctx, vcs[hk]), (s_loc, _ones_half(vw, hk))], sink, hk))
    for i, res in enumerate(_softmax_pv(heads)):
        j, hk = divmod(i, 2)
        o = res[:, hk * HD:(hk + 1) * HD]
        rows = slice(j * 128, (j + 1) * 128)
        o_ref[rows, (2 * hk) * HD:(2 * hk + 1) * HD] = o[:128].astype(o_ref.dtype)
        o_ref[rows, (2 * hk + 1) * HD:(2 * hk + 2) * HD] = o[128:].astype(o_ref.dtype)


def _window_attention(p_wa, sink):
    nb = S // (128 * WA_BLKS)
    return pl.pallas_call(
        _wa_kernel,
        grid=(B, nb),
        in_specs=[
            pl.BlockSpec(memory_space=pltpu.SMEM),
            pl.BlockSpec((128 * WA_BLKS, 256), lambda b, n: (b * nb + n, 0)),
            pl.BlockSpec((S, 128), lambda b, n: (b, 2)),
            pl.BlockSpec((S, 128), lambda b, n: (b, 3)),
            pl.BlockSpec((LC, 128), lambda b, n: (CTX_BLK0 + b, 2)),
            pl.BlockSpec((LC, 128), lambda b, n: (CTX_BLK0 + b, 3)),
        ],
        out_specs=pl.BlockSpec((128 * WA_BLKS, 256), lambda b, n: (b * nb + n, 0)),
        out_shape=jax.ShapeDtypeStruct((NL, 256), BF16),
        compiler_params=_cp("arbitrary", "arbitrary"),
        name="window_attn",
    )(sink, p_wa, p_wa, p_wa, p_wa, p_wa)


def _na_kernel(q_ref, k_ref, v_ref, kc_ref, vc_ref, bias_ref, o_ref):
    kc, vc = kc_ref[...], vc_ref[...]
    vcs = [_ones_half(vc[:, (h // 2) * 128:(h // 2 + 1) * 128], h % 2) for h in range(4)]
    heads = []
    for j in range(NA_ROWS):
        r = pl.program_id(1) * NA_ROWS + j
        first = jnp.clip(r - NA_KR // 2, 0, S // GW - NA_KR)
        start = pl.multiple_of(first * GW, GW)
        off = first - r + NA_KR - 1
        kw = k_ref[pl.ds(start, NA_KR * GW), :]
        vw = v_ref[pl.ds(start, NA_KR * GW), :]
        q = q_ref[j * GW:(j + 1) * GW, :] * SCALE
        for h in range(4):
            hs = slice(h * HD, (h + 1) * HD)
            win = slice((h // 2) * 128, (h // 2 + 1) * 128)
            s_loc = _dot_nt(q[:, hs], kw[:, hs]) + bias_ref[h, off]
            s_ctx = _dot_nt(q[:, hs], kc[:, hs])
            heads.append(([(s_ctx, vcs[h]), (s_loc, _ones_half(vw[:, win], h % 2))], None, h % 2))
    outs = _softmax_pv(heads)
    lane = lax.broadcasted_iota(jnp.int32, (GW, 128), 1)
    for j in range(NA_ROWS):
        for hp in range(2):
            pair = jnp.where(lane < HD, outs[4 * j + 2 * hp], outs[4 * j + 2 * hp + 1])
            o_ref[j * GW:(j + 1) * GW, hp * 128:(hp + 1) * 128] = pair.astype(o_ref.dtype)


def _na_bias_kernel(rpb_ref, sel_ref, o_ref):
    o_ref[0] = _dot(rpb_ref[0], sel_ref[...], HI)


def _na_bias_table(rpb):
    qc = jnp.arange(GW)[:, None]
    kcol = jnp.arange(GW)[None, :]
    ci = jnp.clip(kcol - qc + NA_KC - 1, 0, 2 * NA_KC - 2).reshape(1, GW * GW)
    sel = (jnp.arange(128)[:, None] == ci).astype(F32)
    n_ri = 2 * NA_KR - 1
    rpb_p = jnp.pad(rpb.astype(F32), ((0, 0), (0, 16 - n_ri), (0, 128 - (2 * NA_KC - 1))))
    per_row = pl.pallas_call(
        _na_bias_kernel,
        grid=(4,),
        in_specs=[pl.BlockSpec((1, 16, 128), lambda h: (h, 0, 0)), pl.BlockSpec((128, GW * GW), lambda h: (0, 0))],
        out_specs=pl.BlockSpec((1, 16, GW * GW), lambda h: (h, 0, 0)),
        out_shape=jax.ShapeDtypeStruct((4, 16, GW * GW), F32),
        compiler_params=_cp("arbitrary"),
        name="na_bias",
    )(rpb_p, sel).reshape(4, 16, GW, GW)
    tbl = jnp.stack([per_row[:, o:o + NA_KR] for o in range(NA_KR)], axis=1)
    tbl = jnp.transpose(tbl, (0, 1, 3, 2, 4))
    c_start = jnp.clip(qc - NA_KC // 2, 0, GW - NA_KC)
    ok = (kcol >= c_start) & (kcol < c_start + NA_KC)
    tbl = jnp.where(ok[None, None, :, None, :], tbl, NEG)
    return tbl.reshape(4, NA_KR, GW, NA_KR * GW)


def _neighborhood_attention(p_na, bias_tbl):
    steps = S // GW // NA_ROWS
    return pl.pallas_call(
        _na_kernel,
        grid=(B, steps),
        in_specs=[
            pl.BlockSpec((NA_ROWS * GW, 256), lambda b, r: (b * steps + r, 0)),
            pl.BlockSpec((S, 256), lambda b, r: (b, 1)),
            pl.BlockSpec((S, 256), lambda b, r: (b, 2)),
            pl.BlockSpec((LC, 256), lambda b, r: (CTX_BLK0 + b, 1)),
            pl.BlockSpec((LC, 256), lambda b, r: (CTX_BLK0 + b, 2)),
            pl.BlockSpec((4, NA_KR, GW, NA_KR * GW), lambda b, r: (0, 0, 0, 0)),
        ],
        out_specs=pl.BlockSpec((NA_ROWS * GW, 256), lambda b, r: (b * steps + r, 0)),
        out_shape=jax.ShapeDtypeStruct((NL, 256), BF16),
        compiler_params=_cp("arbitrary", "arbitrary"),
        name="neighborhood_attn",
    )(p_na, p_na, p_na, p_na, p_na, bias_tbl)


def _ctx_attn_kernel(sink_ref, qw_ref, kw_ref, vw_ref, qn_ref, kn_ref, vn_ref, owa_ref, ona_ref):
    q, k, v = qw_ref[...] * SCALE, kw_ref[...], vw_ref[...]
    r1 = lax.broadcasted_iota(jnp.int32, (2 * LC, 1), 0)
    heads = []
    for hk in range(2):
        q2 = jnp.concatenate([q[:, (2 * hk) * HD:(2 * hk + 1) * HD],
                              q[:, (2 * hk + 1) * HD:(2 * hk + 2) * HD]], axis=0)
        hs = slice(hk * HD, (hk + 1) * HD)
        sink = jnp.where(r1 >= LC, sink_ref[2 * hk + 1], sink_ref[2 * hk])
        heads.append(([(_dot_nt(q2, k[:, hs]), _ones_half(v, hk))], sink, hk))
    for hk, res in enumerate(_softmax_pv(heads)):
        o = res[:, hk * HD:(hk + 1) * HD]
        owa_ref[:, (2 * hk) * HD:(2 * hk + 1) * HD] = o[:LC].astype(owa_ref.dtype)
        owa_ref[:, (2 * hk + 1) * HD:(2 * hk + 2) * HD] = o[LC:].astype(owa_ref.dtype)
    q, k, v = qn_ref[...] * SCALE, kn_ref[...], vn_ref[...]
    heads = []
    for h in range(4):
        hs = slice(h * HD, (h + 1) * HD)
        win = slice((h // 2) * 128, (h // 2 + 1) * 128)
        heads.append(([(_dot_nt(q[:, hs], k[:, hs]), _ones_half(v[:, win], h % 2))], None, h % 2))
    outs = _softmax_pv(heads)
    lane = lax.broadcasted_iota(jnp.int32, (LC, 128), 1)
    for hp in range(2):
        pair = jnp.where(lane < HD, outs[2 * hp], outs[2 * hp + 1])
        ona_ref[:, hp * 128:(hp + 1) * 128] = pair.astype(ona_ref.dtype)


def _ctx_attention(p_wa, p_na, sink):
    blk = lambda c: (lambda b: (CTX_BLK0 + b, c))
    return pl.pallas_call(
        _ctx_attn_kernel,
        grid=(B,),
        in_specs=[
            pl.BlockSpec(memory_space=pltpu.SMEM),
            pl.BlockSpec((LC, 256), blk(0)),
            pl.BlockSpec((LC, 128), blk(2)),
            pl.BlockSpec((LC, 128), blk(3)),
            pl.BlockSpec((LC, 256), blk(0)),
            pl.BlockSpec((LC, 256), blk(1)),
            pl.BlockSpec((LC, 256), blk(2)),
        ],
        out_specs=[pl.BlockSpec((LC, 256), lambda b: (b, 0))] * 2,
        out_shape=[jax.ShapeDtypeStruct((NC, 256), BF16)] * 2,
        compiler_params=_cp("arbitrary"),
        name="ctx_attn",
    )(sink, p_wa, p_wa, p_wa, p_na, p_na, p_na)


def _gdn_prep_seq(p_ref, ab_ref, cw, alog, dtb, qkv_ref, gb_ref, row0, length):
    x = p_ref[:, :768].astype(F32)
    u = (_shift_rows(x, -2) * cw[0:1] + _shift_rows(x, -1) * cw[1:2] + x * cw[2:3]
         + _shift_rows(x, 1) * cw[3:4])
    u = _silu(u)
    rows = slice(row0, row0 + length)
    for j in range(8):
        sl = slice(j * HD, (j + 1) * HD)
        xs = u[:, sl]
        nrm = lax.rsqrt(jnp.sum(xs * xs, axis=-1, keepdims=True) + EPS)
        qkv_ref[0, rows, sl] = xs * (nrm * SCALE if j < 4 else nrm)
    qkv_ref[0, rows, 512:768] = u[:, 512:768]
    ab = ab_ref[...]
    g = -jnp.exp(alog) * _softplus(ab + dtb)
    lane = lax.broadcasted_iota(jnp.int32, ab.shape, 1)
    gb_ref[0, rows, :] = jnp.where(lane < 8, g, _sigmoid(ab))


def _gdn_prep_kernel(pl_ref, abl_ref, pc_ref, abc_ref, cw_ref, alog_ref, dtb_ref, qkv_ref, gb_ref):
    cw, alog, dtb = cw_ref[...], alog_ref[...], dtb_ref[...]
    _gdn_prep_seq(pl_ref, abl_ref, cw, alog, dtb, qkv_ref, gb_ref, 0, S)
    _gdn_prep_seq(pc_ref, abc_ref, cw, alog, dtb, qkv_ref, gb_ref, S, LC)


def _gdn_prep(p_gd, p_ab, conv_w, a_log, dt_bias):
    const = lambda b: (0, 0)
    alog = jnp.pad(a_log.reshape(1, 8), ((0, 0), (0, 120)))
    dtb = jnp.pad(dt_bias.reshape(1, 8), ((0, 0), (0, 120)))
    return pl.pallas_call(
        _gdn_prep_kernel,
        grid=(B,),
        in_specs=[
            pl.BlockSpec((S, 1024), lambda b: (b, 0)),
            pl.BlockSpec((S, 128), lambda b: (b, 0)),
            pl.BlockSpec((LC, 1024), lambda b: (CTX_BLK0 + b, 0)),
            pl.BlockSpec((LC, 128), lambda b: (CTX_BLK0 + b, 0)),
            pl.BlockSpec((4, 768), const),
            pl.BlockSpec((1, 128), const),
            pl.BlockSpec((1, 128), const),
        ],
        out_specs=[pl.BlockSpec((1, S + LC, 768), lambda b: (b, 0, 0)),
                   pl.BlockSpec((1, S + LC, 128), lambda b: (b, 0, 0))],
        out_shape=[jax.ShapeDtypeStruct((B, S + LC, 768), F32), jax.ShapeDtypeStruct((B, S + LC, 128), F32)],
        compiler_params=_cp("arbitrary"),
        name="gdn_prep",
    )(p_gd, p_ab, p_gd, p_ab, conv_w, alog, dtb)


def _gdn_chains(xs, gbs, cums, cum_ts, states, masks):
    chains = [(g, h) for g in range(len(xs)) for h in range(4)]
    pre = []
    for g, h in chains:
        dirn = g % 2
        x, gb = xs[g], gbs[g]
        tri, strict = masks[dirn]
        col = dirn * 4 + h
        qh = x[:, h * HD:(h + 1) * HD]
        kh = x[:, 256 + h * HD:256 + (h + 1) * HD]
        vh = x[:, 512 + h * HD:512 + (h + 1) * HD]
        gc = cums[g][:, col:col + 1]
        gc_row = cum_ts[g][col:col + 1, :]
        beta = gb[:, 8 + col:9 + col]
        decay = jnp.exp(jnp.where(tri, gc - gc_row, NEG))
        kb = kh * beta
        eg = jnp.exp(gc)
        g_last = gc[0:1, :] if dirn else gc[CHUNK - 1:CHUNK, :]
        pre.append(dict(strict=strict, decay=decay, kbf=kb.astype(BF16), khf=kh.astype(BF16), qf=qh.astype(BF16),
                        xw=jnp.concatenate([vh * beta, kb * eg], axis=1), qdec=(qh * eg).astype(BF16),
                        kdec=(kh * jnp.exp(g_last - gc)).astype(BF16), gl=jnp.exp(g_last)))
    n = len(chains)
    gram = [_dot_nt(p["kbf"], p["khf"]) for p in pre]
    attn = [_dot_nt(p["qf"], p["khf"]) for p in pre]
    lmat = [jnp.where(p["strict"], g * p["decay"], 0.0) for p, g in zip(pre, gram)]
    attn = [(a * p["decay"]).astype(BF16) for p, a in zip(pre, attn)]
    ii = lax.broadcasted_iota(jnp.int32, (CHUNK, CHUNK), 0)
    jj = lax.broadcasted_iota(jnp.int32, (CHUNK, CHUNK), 1)

    def merged_off_blocks(log_s):
        return ((ii >> (log_s + 1)) == (jj >> (log_s + 1))) & ((ii >> log_s) != (jj >> log_s))

    eye = (ii == jj).astype(F32)
    dinv = [eye - jnp.where(merged_off_blocks(0), lm, 0.0) for lm in lmat]
    for log_s in range(1, 6):
        m = merged_off_blocks(log_s)
        dinv_b = [d.astype(BF16) for d in dinv]
        ld = [_dot(jnp.where(m, lmat[c], 0.0).astype(BF16), dinv_b[c]) for c in range(n)]
        upd = [_dot(dinv_b[c], ld[c].astype(BF16)) for c in range(n)]
        dinv = [dinv[c] - upd[c] for c in range(n)]
    xw = [_dot(dinv[c].astype(BF16), pre[c]["xw"].astype(BF16)) for c in range(n)]
    sts = [states[g][h] for g, h in chains]
    stb = [s.astype(BF16) for s in sts]
    ws = [_dot(xw[c][:, HD:].astype(BF16), stb[c]) for c in range(n)]
    qs = [_dot(pre[c]["qdec"], stb[c]) for c in range(n)]
    u_new = [(xw[c][:, :HD] - ws[c]).astype(BF16) for c in range(n)]
    au = [_dot(attn[c], u_new[c]) for c in range(n)]
    ku = [_dot_tn(pre[c]["kdec"], u_new[c]) for c in range(n)]
    outs = [qs[c] + au[c] for c in range(n)]
    new_states = [sts[c] * pre[c]["gl"] + ku[c] for c in range(n)]
    return outs, new_states


def _gdn_kernel(x0_ref, gb0_ref, x1_ref, gb1_ref, o0_ref, o1_ref, state_ref):
    @pl.when(pl.program_id(0) == 0)
    def _():
        state_ref[...] = jnp.zeros_like(state_ref)

    ii = lax.broadcasted_iota(jnp.int32, (CHUNK, CHUNK), 0)
    jj = lax.broadcasted_iota(jnp.int32, (CHUNK, CHUNK), 1)
    masks = ((jj <= ii, jj < ii), (jj >= ii, jj > ii))
    ones_tri = ((jj <= ii).astype(F32), (jj >= ii).astype(F32))

    def body(bp, carry):
        bs = [bp * GDN_SEQS + i for i in range(GDN_SEQS)]
        states = [[state_ref[b, dirn, h] for h in range(4)] for b in bs for dirn in range(2)]
        xs = [r[b] for b in bs for r in (x0_ref, x1_ref)]
        gbs = [r[b] for b in bs for r in (gb0_ref, gb1_ref)]
        cums = [_dot(ones_tri[g % 2], gbs[g], HI) for g in range(2 * GDN_SEQS)]
        cum_ts = [cm.T for cm in cums]
        outs, new_states = _gdn_chains(xs, gbs, cums, cum_ts, states, masks)
        for c, st in enumerate(new_states):
            state_ref[bs[c // 8], (c // 4) % 2, c % 4] = st
        for i, b in enumerate(bs):
            o0_ref[b] = jnp.concatenate(outs[8 * i:8 * i + 4], axis=1)
            o1_ref[b] = jnp.concatenate(outs[8 * i + 4:8 * i + 8], axis=1)
        return carry

    lax.fori_loop(0, B // GDN_SEQS, body, 0)


def _gdn_scan(qkv, gb):
    nc = (S + LC) // CHUNK
    fwd = lambda c: (0, (c + S // CHUNK) % nc, 0)
    bwd = lambda c: (0, nc - 1 - c, 0)
    return pl.pallas_call(
        _gdn_kernel,
        grid=(nc,),
        in_specs=[
            pl.BlockSpec((B, CHUNK, 768), fwd),
            pl.BlockSpec((B, CHUNK, 128), fwd),
            pl.BlockSpec((B, CHUNK, 768), bwd),
            pl.BlockSpec((B, CHUNK, 128), bwd),
        ],
        out_specs=[pl.BlockSpec((B, CHUNK, 256), fwd), pl.BlockSpec((B, CHUNK, 256), bwd)],
        out_shape=[jax.ShapeDtypeStruct((B, S + LC, 256), F32)] * 2,
        scratch_shapes=[pltpu.VMEM((B, 2, 4, HD, HD), F32)],
        compiler_params=_cp("arbitrary"),
        name="gdn_scan",
    )(qkv, gb, qkv, gb)


def _gdn_finish_kernel(o0_ref, o1_ref, gate_ref, ng_ref, y_ref):
    o = o0_ref[0] + o1_ref[0]
    gi = lax.broadcasted_iota(jnp.int32, (256, 256), 0) // HD
    gj = lax.broadcasted_iota(jnp.int32, (256, 256), 1) // HD
    group_mean = jnp.where(gi == gj, 1.0 / HD, 0.0)
    ms = _dot(o * o, group_mean, HI)
    y = o * lax.rsqrt(ms + EPS) * ng_ref[...]
    y_ref[...] = (y * _silu(gate_ref[...].astype(F32))).astype(y_ref.dtype)


def _gdn_finish(o0, o1, p_gd, norm_g, need_ctx):
    lat_tiles = S // TM
    row_blk = lambda b, j: jnp.where(j < lat_tiles, b * lat_tiles + j, CTX_BLK0 + b)
    return pl.pallas_call(
        _gdn_finish_kernel,
        grid=(B, lat_tiles + (1 if need_ctx else 0)),
        in_specs=[
            pl.BlockSpec((1, TM, 256), lambda b, j: (b, j, 0)),
            pl.BlockSpec((1, TM, 256), lambda b, j: (b, j, 0)),
            pl.BlockSpec((TM, 256), lambda b, j: (row_blk(b, j), 3)),
            pl.BlockSpec((1, 256), lambda b, j: (0, 0)),
        ],
        out_specs=pl.BlockSpec((TM, 256), lambda b, j: (row_blk(b, j), 0)),
        out_shape=jax.ShapeDtypeStruct((ROWS if need_ctx else NL, 256), BF16),
        compiler_params=_cp("arbitrary", "arbitrary"),
        name="gdn_finish",
    )(o0, o1, p_gd, jnp.tile(norm_g, 4)[None, :])


def _pack_bf16_pairs(x):
    n = x.shape[1] // 2
    lo = pltpu.bitcast(x[:, :n].astype(BF16).astype(F32), jnp.uint32)
    hi = pltpu.bitcast(x[:, n:].astype(BF16).astype(F32), jnp.uint32)
    return hi | (lo >> 16)


def _unpack_bf16_pairs(w):
    lo = pltpu.bitcast(w << 16, F32)
    hi = pltpu.bitcast(w & jnp.uint32(0xFFFF0000), F32)
    return jnp.concatenate([lo, hi], axis=1)


def _outproj_kernel(x_ref, mod_ref, g2_ref, yh_ref, yw_ref, yn_ref, yg_ref, wo_ref, wr_ref, eb_ref,
                    xo_ref, h2_ref, h2p_ref, idx_ref, tw_ref, rank_ref, cnt_ref, carry_ref):
    @pl.when(pl.program_id(0) == 0)
    def _():
        carry_ref[...] = jnp.zeros_like(carry_ref)

    m = mod_ref[0]
    acc = (_dot(yh_ref[...], wo_ref[0:256, :]) + _dot(yw_ref[...], wo_ref[256:512, :])
           + _dot(yn_ref[...], wo_ref[512:768, :]) + _dot(yg_ref[...], wo_ref[768:1024, :]))
    x = x_ref[...] + m[2:3] * acc
    xo_ref[...] = x
    h2 = _modulated_norm(x, g2_ref[...], m[3:4], m[4:5])
    h2_ref[...] = h2.astype(h2_ref.dtype)
    h2p_ref[...] = _pack_bf16_pairs(h2)
    scores = _sigmoid(_dot(h2, wr_ref[...], HI))
    lane = lax.broadcasted_iota(jnp.int32, scores.shape, 1)
    sel = jnp.where(lane < N_EXP, scores + eb_ref[...], -jnp.inf)
    lane_f = lane.astype(F32)
    idx_out = jnp.zeros(scores.shape, F32)
    s_out = jnp.zeros(scores.shape, F32)
    hits = []
    for kk in range(TOP_K):
        mx = jnp.max(sel, axis=-1, keepdims=True)
        idx = jnp.min(jnp.where(sel == mx, lane_f, 128.0), axis=-1, keepdims=True)
        hit = lane_f == idx
        sk = jnp.sum(jnp.where(hit, scores, 0.0), axis=-1, keepdims=True)
        sel = jnp.where(hit, -jnp.inf, sel)
        idx_out = jnp.where(lane == kk, idx, idx_out)
        s_out = jnp.where(lane == kk, sk, s_out)
        hits.append(hit)
    tot = jnp.sum(s_out, axis=-1, keepdims=True)
    tw_ref[...] = s_out / tot * ROUTED_SCALE
    cnt = jnp.zeros(scores.shape, F32)
    for hit in hits:
        cnt = cnt + jnp.where(hit, 1.0, 0.0)
    ri = lax.broadcasted_iota(jnp.int32, (TM, TM), 0)
    rj = lax.broadcasted_iota(jnp.int32, (TM, TM), 1)
    before = _dot(jnp.where(rj < ri, 1.0, 0.0).astype(BF16), cnt.astype(BF16)) + carry_ref[...]
    rank_out = jnp.zeros(scores.shape, F32)
    for kk, hit in enumerate(hits):
        rk = jnp.sum(jnp.where(hit, before, 0.0), axis=-1, keepdims=True)
        rank_out = jnp.where(lane == kk, rk, rank_out)
    idx_ref[...] = idx_out.T[:8].astype(jnp.int32)
    rank_ref[...] = rank_out.T[:8].astype(jnp.int32)
    carry_ref[...] = carry_ref[...] + jnp.sum(cnt, axis=0, keepdims=True)
    cnt_ref[...] = carry_ref[...].astype(jnp.int32)


def _outproj(xa, mod, l, g2, ys, wo, wr, eb, n_tiles):
    row = lambda i: (i, 0)
    const = lambda i: (0, 0)
    rows = n_tiles * TM
    return pl.pallas_call(
        _outproj_kernel,
        grid=(n_tiles,),
        in_specs=[
            pl.BlockSpec((TM, D), row),
            pl.BlockSpec((1, 6, D), lambda i: (l * 16 + (i * TM) // S, 0, 0)),
            pl.BlockSpec((1, D), const),
            pl.BlockSpec((TM, 256), row),
            pl.BlockSpec((TM, 256), row),
            pl.BlockSpec((TM, 256), row),
            pl.BlockSpec((TM, 256), row),
            pl.BlockSpec((D, D), const),
            pl.BlockSpec((D, 128), const),
            pl.BlockSpec((1, 128), const),
        ],
        out_specs=[
            pl.BlockSpec((TM, D), row),
            pl.BlockSpec((TM, D), row),
            pl.BlockSpec((TM, D // 2), row),
            pl.BlockSpec((8, TM), row),
            pl.BlockSpec((TM, 128), row),
            pl.BlockSpec((8, TM), row),
            pl.BlockSpec((1, 128), const),
        ],
        out_shape=[
            jax.ShapeDtypeStruct((rows, D), F32),
            jax.ShapeDtypeStruct((rows, D), BF16),
            jax.ShapeDtypeStruct((rows, D // 2), jnp.uint32),
            jax.ShapeDtypeStruct((n_tiles * 8, TM), jnp.int32),
            jax.ShapeDtypeStruct((rows, 128), F32),
            jax.ShapeDtypeStruct((n_tiles * 8, TM), jnp.int32),
            jax.ShapeDtypeStruct((1, 128), jnp.int32),
        ],
        scratch_shapes=[pltpu.VMEM((1, 128), F32)],
        compiler_params=_cp("arbitrary"),
        name="outproj_router",
    )(xa, mod, g2, *ys, wo, wr, eb)


ASG_TILE = TM * TOP_K


def _segment_tables(counts, n_blocks):
    counts = counts[0, :N_EXP]
    padded = (counts + MOE_BLK - 1) // MOE_BLK * MOE_BLK
    pad_end = jnp.cumsum(padded)
    seg_start = pad_end - padded
    starts = jnp.arange(n_blocks, dtype=jnp.int32) * MOE_BLK
    blk_e = jnp.minimum(jnp.sum((pad_end[None, :] <= starts[:, None]).astype(jnp.int32), axis=1), N_EXP - 1)
    seg = jnp.concatenate([seg_start, seg_start + counts, pad_end[-1:]])
    blk = jnp.concatenate([blk_e, pad_end[-1:] // MOE_BLK])
    return seg.astype(jnp.int32), blk.astype(jnp.int32)


def _row_copy_wait(shape_ref, dst_ref, sem):
    pltpu.make_async_copy(shape_ref, dst_ref, sem).wait()


def _zero_fill_rows(zero_ref, xb_ref, start, length, sem):
    plan = []
    aligned = (start + 7) & -8
    for r in range(7):
        plan.append(((start + r < aligned) & (r < length),
                     pltpu.make_async_copy(zero_ref.at[pl.ds(0, 1)], xb_ref.at[pl.ds(start + r, 1)], sem)))
    rest = start + length - aligned
    off = aligned
    for bit in reversed(range(3, MOE_BLK.bit_length() - 1)):
        size = 1 << bit
        take = (rest & size) != 0
        dst = xb_ref.at[pl.ds(pl.multiple_of(off, 8), size)]
        plan.append((take, pltpu.make_async_copy(zero_ref.at[pl.ds(0, size)], dst, sem)))
        off = off + jnp.where(take, size, 0)
    for take, copy in plan:
        pl.when(take)(copy.start)
    for take, copy in plan:
        pl.when(take)(copy.wait)


def _dispatch_kernel(seg_ref, idx_ref, rank_ref, h2p_ref, xb_ref, zero_ref, ring_ref, sems, zsem):
    i = pl.program_id(0)
    sem = sems.at[i % 2]
    tile_ref = ring_ref.at[i % 2]
    tile_ref[...] = h2p_ref[...]

    @pl.when(i == 0)
    def _():
        zero_ref[...] = jnp.zeros_like(zero_ref)

    @pl.when(i < N_EXP)
    def _():
        first_pad = seg_ref[N_EXP + i]
        next_start = seg_ref[jnp.where(i == N_EXP - 1, 2 * N_EXP, i + 1)]
        _zero_fill_rows(zero_ref, xb_ref, first_pad, next_start - first_pad, zsem)

    tail_row = seg_ref[2 * N_EXP] + (i - N_EXP) * MOE_BLK

    @pl.when((i >= N_EXP) & (tail_row < xb_ref.shape[0]))
    def _():
        copy = pltpu.make_async_copy(zero_ref, xb_ref.at[pl.ds(pl.multiple_of(tail_row, MOE_BLK), MOE_BLK)], zsem)
        copy.start()
        copy.wait()

    def issue(t, carry):
        src = tile_ref.at[pl.ds(t, 1)]
        for k in range(TOP_K):
            dst = seg_ref[idx_ref[k, t]] + rank_ref[k, t]
            pltpu.make_async_copy(src, xb_ref.at[pl.ds(dst, 1)], sem).start()
        return carry

    lax.fori_loop(0, TM, issue, 0, unroll=True)

    @pl.when(i > 0)
    def _():
        for _ in range(TOP_K):
            _row_copy_wait(h2p_ref, xb_ref.at[pl.ds(0, TM)], sems.at[(i + 1) % 2])

    @pl.when(i == pl.num_programs(0) - 1)
    def _():
        for _ in range(TOP_K):
            _row_copy_wait(h2p_ref, xb_ref.at[pl.ds(0, TM)], sem)


def _dispatch_rows(seg_start, idx_c, rank_c, h2p, n_rows):
    n_tiles = h2p.shape[0] // TM
    assert n_tiles >= 2 * N_EXP
    grid_spec = pltpu.PrefetchScalarGridSpec(
        num_scalar_prefetch=1,
        grid=(n_tiles,),
        in_specs=[
            pl.BlockSpec((8, TM), lambda i, seg: (i, 0), memory_space=pltpu.SMEM),
            pl.BlockSpec((8, TM), lambda i, seg: (i, 0), memory_space=pltpu.SMEM),
            pl.BlockSpec((TM, D // 2), lambda i, seg: (i, 0)),
        ],
        out_specs=pl.BlockSpec(memory_space=pl.ANY),
        scratch_shapes=[pltpu.VMEM((MOE_BLK, D // 2), jnp.uint32), pltpu.VMEM((2, TM, D // 2), jnp.uint32),
                        pltpu.SemaphoreType.DMA((2,)), pltpu.SemaphoreType.DMA(())],
    )
    return pl.pallas_call(
        _dispatch_kernel,
        grid_spec=grid_spec,
        out_shape=jax.ShapeDtypeStruct((n_rows, D // 2), jnp.uint32),
        compiler_params=_cp("arbitrary"),
        name="moe_dispatch",
    )(seg_start, idx_c, rank_c, h2p)


def _moe_kernel(be_ref, x_ref, wg_ref, wu_ref, wd_ref, o_ref, wgb_ref, wub_ref, wdb_ref):
    i = pl.program_id(0)
    in_use = i < be_ref[pl.num_programs(0)]

    @pl.when(in_use & ((i == 0) | (be_ref[i] != be_ref[jnp.maximum(i - 1, 0)])))
    def _():
        wgb_ref[...] = wg_ref[0, 0].astype(BF16)
        wub_ref[...] = wu_ref[0, 0].astype(BF16)
        wdb_ref[...] = wd_ref[0, 0].astype(BF16)

    @pl.when(in_use)
    def _():
        x = _unpack_bf16_pairs(x_ref[...]).astype(BF16)
        g = _dot(x, wgb_ref[...])
        u = _dot(x, wub_ref[...])
        hid = (_silu(g) * u).astype(BF16)
        o_ref[...] = _pack_bf16_pairs(_dot(hid, wdb_ref[...]))

    @pl.when(jnp.logical_not(in_use))
    def _():
        o_ref[...] = jnp.zeros_like(o_ref)


def _moe_experts(blk_e, xb, l, wg, wu, wd):
    n_rows = xb.shape[0]
    grid_spec = pltpu.PrefetchScalarGridSpec(
        num_scalar_prefetch=1,
        grid=(n_rows // MOE_BLK,),
        in_specs=[
            pl.BlockSpec((MOE_BLK, D // 2), lambda i, be: (i, 0)),
            pl.BlockSpec((1, 1, D, D_EXP), lambda i, be: (l, be[i], 0, 0)),
            pl.BlockSpec((1, 1, D, D_EXP), lambda i, be: (l, be[i], 0, 0)),
            pl.BlockSpec((1, 1, D_EXP, D), lambda i, be: (l, be[i], 0, 0)),
        ],
        out_specs=pl.BlockSpec((MOE_BLK, D // 2), lambda i, be: (i, 0)),
        scratch_shapes=[pltpu.VMEM((D, D_EXP), BF16), pltpu.VMEM((D, D_EXP), BF16), pltpu.VMEM((D_EXP, D), BF16)],
    )
    return pl.pallas_call(
        _moe_kernel,
        grid_spec=grid_spec,
        out_shape=jax.ShapeDtypeStruct((n_rows, D // 2), jnp.uint32),
        compiler_params=_cp("arbitrary"),
        name="moe_experts",
    )(blk_e, xb, wg, wu, wd)


def _ffn_out_kernel(seg_ref, idx_ref, rank_ref, x_ref, mod_ref, h2_ref, tw_ref, wg_ref, wu_ref, wd_ref, nf_ref,
                    yb_ref, o_ref, gat_ref, sems, *, final):
    s = pl.program_id(0)

    def start_gathers(slot):
        buf = gat_ref.at[slot]

        def issue(t, carry):
            for k in range(TOP_K):
                src = seg_ref[idx_ref[k, t]] + rank_ref[k, t]
                pltpu.make_async_copy(yb_ref.at[pl.ds(src, 1)], buf.at[pl.ds(k * TM + t, 1)], sems.at[slot]).start()
            return carry

        lax.fori_loop(0, TM, issue, 0, unroll=True)

    def wait_gathers(slot):
        for _ in range(TOP_K):
            _row_copy_wait(yb_ref.at[pl.ds(0, TM)], gat_ref.at[slot, pl.ds(0, TM)], sems.at[slot])

    pl.when(s == 0)(lambda: start_gathers(1))
    start_gathers(s % 2)
    m = mod_ref[0]
    h2 = h2_ref[...]
    hid = (_silu(_dot(h2, wg_ref[...])) * _dot(h2, wu_ref[...])).astype(BF16)
    shared = _dot(hid, wd_ref[...])
    done = (s + 1) % 2
    wait_gathers(done)
    buf = gat_ref.at[done]
    tw = tw_ref[...]
    routed = _unpack_bf16_pairs(buf[0:TM, :]) * tw[:, 0:1]
    for k in range(1, TOP_K):
        routed = routed + _unpack_bf16_pairs(buf[k * TM:(k + 1) * TM, :]) * tw[:, k:k + 1]
    x = x_ref[...] + m[5:6] * (routed + shared)
    if final:
        ms = jnp.mean(x * x, axis=-1, keepdims=True)
        x = x * lax.rsqrt(ms + EPS) * nf_ref[...]
    o_ref[...] = x
    pl.when(s == pl.num_programs(0) - 1)(lambda: wait_gathers(s % 2))


def _ffn_out(seg_start, idx_c, rank_c, xn, mod, l, h2, top_w, wg, wu, wd, norm_f, yb, n_tiles, final):
    row = lambda i, seg: (jnp.maximum(i - 1, 0), 0)
    ahead = lambda i, seg: (jnp.minimum(i, n_tiles - 1), 0)
    const = lambda i, seg: (0, 0)
    grid_spec = pltpu.PrefetchScalarGridSpec(
        num_scalar_prefetch=1,
        grid=(n_tiles + 1,),
        in_specs=[
            pl.BlockSpec((8, TM), ahead, memory_space=pltpu.SMEM),
            pl.BlockSpec((8, TM), ahead, memory_space=pltpu.SMEM),
            pl.BlockSpec((TM, D), row),
            pl.BlockSpec((1, 6, D), lambda i, seg: (l * 16 + (jnp.maximum(i - 1, 0) * TM) // S, 0, 0)),
            pl.BlockSpec((TM, D), row),
            pl.BlockSpec((TM, 128), row),
            pl.BlockSpec((D, D_EXP), const),
            pl.BlockSpec((D, D_EXP), const),
            pl.BlockSpec((D_EXP, D), const),
            pl.BlockSpec((1, D), const),
            pl.BlockSpec(memory_space=pl.ANY),
        ],
        out_specs=pl.BlockSpec((TM, D), row),
        scratch_shapes=[pltpu.VMEM((2, ASG_TILE, D // 2), jnp.uint32), pltpu.SemaphoreType.DMA((2,))],
    )
    return pl.pallas_call(
        functools.partial(_ffn_out_kernel, final=final),
        grid_spec=grid_spec,
        out_shape=jax.ShapeDtypeStruct((n_tiles * TM, D), F32),
        compiler_params=_cp("arbitrary"),
        name="shared_ffn_residual",
    )(seg_start, idx_c, rank_c, xn, mod, h2, top_w, wg, wu, wd, norm_f, yb)


def kernel(x, c, ctx, c_ctx, w_ada, b_ada, norm1, norm2, norm_f, w_in, w_out, hy_conv, hy_w1, hy_b1, hy_w2, hy_b2, hy_w3, hy_freq, hy_bias, wa_sink, na_rpb, gdn_conv, gdn_a_log, gdn_dt_bias, gdn_norm, moe_router, moe_bias, moe_gate, moe_up, moe_down, sh_gate, sh_up, sh_down):
    depth = w_ada.shape[0]
    xa = jnp.concatenate([x.reshape(NL, D), ctx.reshape(NC, D)], axis=0)
    cvec = jnp.concatenate([c, c_ctx[None, :], jnp.zeros((16 - B - 1, D), F32)], axis=0)
    mod = _ada(cvec, w_ada, b_ada).reshape(depth * 16, 6, D)
    rope = _rope_tables()
    dft_lat = _dft_tables(S)
    dft_ctx = _dft_tables(LC)
    o1, o2, o3 = 768, 768 + 512, 768 + 512 + 768

    for l in range(depth):
        need_ctx = l < depth - 1
        n_tiles = NT_ALL if need_ctx else NT_LAT
        wl = w_in[l].astype(BF16)
        ws = (wl[:, :o1], wl[:, o1:o2], wl[:, o2:o3], wl[:, o3:o3 + 1024],
              jnp.pad(wl[:, o3 + 1024:], ((0, 0), (0, 128 - 16))))
        p_hy, p_wa, p_na, p_gd, p_ab = _inproj(xa, mod, l, norm1[l][None, :], ws, rope)

        filt = (hy_w1[l], hy_b1[l], hy_w2[l], hy_b2[l], hy_w3[l], hy_freq[l])
        kre, kim = _hyena_filter_spectrum(S, *filt, dft_lat)
        y_hy = _hyena(p_hy, S, 0, hy_conv[l], hy_bias[l], dft_lat, kre, kim)
        y_wa = _window_attention(p_wa, wa_sink[l])
        y_na = _neighborhood_attention(p_na, _na_bias_table(na_rpb[l]))
        if need_ctx:
            kre_c, kim_c = _hyena_filter_spectrum(LC, *filt, dft_ctx)
            yc_hy = _hyena(p_hy, LC, CTX_BLK0, hy_conv[l], hy_bias[l], dft_ctx, kre_c, kim_c)
            yc_wa, yc_na = _ctx_attention(p_wa, p_na, wa_sink[l])
            y_hy = jnp.concatenate([y_hy, yc_hy], axis=0)
            y_wa = jnp.concatenate([y_wa, yc_wa], axis=0)
            y_na = jnp.concatenate([y_na, yc_na], axis=0)
        qkv, gb = _gdn_prep(p_gd, p_ab, gdn_conv[l], gdn_a_log[l], gdn_dt_bias[l])
        y_gd = _gdn_finish(*_gdn_scan(qkv, gb), p_gd, gdn_norm[l], need_ctx)

        wr = jnp.pad(moe_router[l], ((0, 0), (0, 128 - N_EXP)))
        eb = jnp.pad(moe_bias[l], (0, 128 - N_EXP))[None, :]
        xn, h2, h2p, top_idx, top_w, rank, counts = _outproj(
            xa, mod, l, norm2[l][None, :], (y_hy, y_wa, y_na, y_gd), w_out[l].astype(BF16), wr, eb, n_tiles)
        n_asg = n_tiles * ASG_TILE
        n_blocks = n_asg // MOE_BLK + N_EXP
        seg_start, blk_e = _segment_tables(counts, n_blocks)
        xb = _dispatch_rows(seg_start, top_idx, rank, h2p, n_blocks * MOE_BLK)
        yb = _moe_experts(blk_e, xb, l, moe_gate, moe_up, moe_down)
        xa_new = _ffn_out(seg_start, top_idx, rank, xn, mod, l, h2, top_w, sh_gate[l].astype(BF16),
                          sh_up[l].astype(BF16), sh_down[l].astype(BF16), norm_f[None, :], yb, n_tiles,
                          final=not need_ctx)
        if need_ctx:
            xa = xa_new
        else:
            return xa_new.reshape(B, S, D)
```

```python
import functools
import math

import jax
import jax.numpy as jnp
from jax import lax
from jax.experimental import pallas as pl
from jax.experimental.pallas import tpu as pltpu

F32 = jnp.float32
BF16 = jnp.bfloat16
HI = lax.Precision.HIGHEST

D = 1024
B = 8
S = 2048
LC = 256
GW = 64
HD = 64
NL = B * S
NC = B * LC
ROWS = NL + NC
TM = 256
NT_LAT = NL // TM
NT_ALL = ROWS // TM
CTX_BLK0 = NL // LC

HY_CH = 256
HY_BANDS = 16
HY_DECAY_MIN = -math.log(1e-2) / 1.5
HY_DECAY_MAX = -math.log(1e-2) / 0.3
WINDOW = 128
NA_KR = 8
NA_KC = 16
CHUNK = 64
N_EXP = 32
TOP_K = 4
D_EXP = 256
ROUTED_SCALE = 2.5
MOE_BLK = 512
WA_BLKS = 2
NA_ROWS = 4
GDN_SEQS = 4
EPS = 1e-6
NEG = -1e30
SCALE = HD ** -0.5
VMEM_LIMIT = 56 * 1024 * 1024


def _cp(*sem):
    return pltpu.CompilerParams(dimension_semantics=tuple(sem), vmem_limit_bytes=VMEM_LIMIT)


def _dot(a, b, precision=None):
    return jnp.dot(a, b, preferred_element_type=F32, precision=precision)


def _dot_nt(a, b, precision=None):
    return lax.dot_general(a, b, (((1,), (1,)), ((), ())), preferred_element_type=F32, precision=precision)


def _dot_tn(a, b, precision=None):
    return lax.dot_general(a, b, (((0,), (0,)), ((), ())), preferred_element_type=F32, precision=precision)


def _sigmoid(x):
    return 1.0 / (1.0 + jnp.exp(-x))


def _silu(x):
    return x * _sigmoid(x)


def _softplus(x):
    return jnp.maximum(x, 0.0) + jnp.log(1.0 + jnp.exp(-jnp.abs(x)))


def _shift_rows(x, d):
    n = x.shape[0]
    if d == 0:
        return x
    y = pltpu.roll(x, (-d) % n, axis=0)
    t = lax.broadcasted_iota(jnp.int32, x.shape, 0)
    ok = (t + d >= 0) & (t + d < n)
    return jnp.where(ok, y, 0.0)


def _ada_kernel(c_ref, w_ref, b_ref, o_ref):
    s = _silu(c_ref[...])
    o_ref[0] = _dot(s.astype(BF16), w_ref[0].astype(BF16)) + b_ref[0]


def _ada(cvec, w_ada, b_ada):
    nl = w_ada.shape[0]
    tn = 1536
    return pl.pallas_call(
        _ada_kernel,
        grid=(nl, 6 * D // tn),
        in_specs=[
            pl.BlockSpec((16, D), lambda l, j: (0, 0)),
            pl.BlockSpec((1, D, tn), lambda l, j: (l, 0, j)),
            pl.BlockSpec((1, 1, tn), lambda l, j: (l, 0, j)),
        ],
        out_specs=pl.BlockSpec((1, 16, tn), lambda l, j: (l, 0, j)),
        out_shape=jax.ShapeDtypeStruct((nl, 16, 6 * D), F32),
        compiler_params=_cp("arbitrary", "arbitrary"),
        name="adaln",
    )(cvec, w_ada, b_ada.reshape(nl, 1, 6 * D))


def _modulated_norm(x, g, shift, scale):
    ms = jnp.mean(x * x, axis=-1, keepdims=True)
    y = x * lax.rsqrt(ms + EPS) * g
    return y * (1.0 + scale) + shift


def _inproj_kernel(x_ref, mod_ref, g_ref, why_ref, wwa_ref, wna_ref, wgd_ref, wab_ref,
                   cos_ref, sa_ref, sb_ref, ohy, owa, ona, ogd, oab):
    m = mod_ref[0]
    h = _modulated_norm(x_ref[...], g_ref[...], m[0:1], m[1:2]).astype(BF16)
    ohy[...] = _dot(h, why_ref[...]).astype(ohy.dtype)
    ona[...] = _dot(h, wna_ref[...]).astype(ona.dtype)
    ogd[...] = _dot(h, wgd_ref[...]).astype(ogd.dtype)
    oab[...] = _dot(h, wab_ref[...])
    a = _dot(h, wwa_ref[...])
    for c in range(4):
        sl = slice(c * 128, (c + 1) * 128)
        ac = a[:, sl]
        r = (ac * cos_ref[:, sl] + pltpu.roll(ac, 112, axis=1) * sa_ref[:, sl]
             + pltpu.roll(ac, 16, axis=1) * sb_ref[:, sl])
        owa[:, sl] = r.astype(owa.dtype)


def _inproj(xa, mod, l, g, ws, tabs):
    why, wwa, wna, wgd, wab = ws
    cos_t, sa_t, sb_t = tabs
    row = lambda i: (i, 0)
    const = lambda i: (0, 0)
    tab = lambda i: (jnp.where(i < NT_LAT, i % (S // TM), S // TM), 0)
    return pl.pallas_call(
        _inproj_kernel,
        grid=(NT_ALL,),
        in_specs=[
            pl.BlockSpec((TM, D), row),
            pl.BlockSpec((1, 6, D), lambda i: (l * 16 + (i * TM) // S, 0, 0)),
            pl.BlockSpec((1, D), const),
            pl.BlockSpec((D, 768), const),
            pl.BlockSpec((D, 512), const),
            pl.BlockSpec((D, 768), const),
            pl.BlockSpec((D, 1024), const),
            pl.BlockSpec((D, 128), const),
            pl.BlockSpec((TM, 512), tab),
            pl.BlockSpec((TM, 512), tab),
            pl.BlockSpec((TM, 512), tab),
        ],
        out_specs=[
            pl.BlockSpec((TM, 768), row),
            pl.BlockSpec((TM, 512), row),
            pl.BlockSpec((TM, 768), row),
            pl.BlockSpec((TM, 1024), row),
            pl.BlockSpec((TM, 128), row),
        ],
        out_shape=[
            jax.ShapeDtypeStruct((ROWS, 768), BF16),
            jax.ShapeDtypeStruct((ROWS, 512), BF16),
            jax.ShapeDtypeStruct((ROWS, 768), BF16),
            jax.ShapeDtypeStruct((ROWS, 1024), BF16),
            jax.ShapeDtypeStruct((ROWS, 128), F32),
        ],
        compiler_params=_cp("arbitrary"),
        name="inproj",
    )(xa, mod, g, why, wwa, wna, wgd, wab, cos_t, sa_t, sb_t)


def _rope_tables():
    quarter = HD // 4
    pos = jnp.arange(S)
    inv = 10000.0 ** (-jnp.arange(quarter, dtype=F32) / quarter)
    ang_r = (pos // GW).astype(F32)[:, None] * inv[None, :]
    ang_c = (pos % GW).astype(F32)[:, None] * inv[None, :]
    z = jnp.zeros_like(ang_r)
    cos_h = jnp.concatenate([jnp.cos(ang_r)] * 2 + [jnp.cos(ang_c)] * 2, axis=1)
    sa_h = jnp.concatenate([-jnp.sin(ang_r), z, -jnp.sin(ang_c), z], axis=1)
    sb_h = jnp.concatenate([z, jnp.sin(ang_r), z, jnp.sin(ang_c)], axis=1)
    ones = jnp.ones((S, 128), F32)
    zeros = jnp.zeros((S, 128), F32)
    cos_t = jnp.concatenate([jnp.tile(cos_h, (1, 6)), ones], axis=1)
    sa_t = jnp.concatenate([jnp.tile(sa_h, (1, 6)), zeros], axis=1)
    sb_t = jnp.concatenate([jnp.tile(sb_h, (1, 6)), zeros], axis=1)
    ident = jnp.ones((TM, 512), F32)
    none = jnp.zeros((TM, 512), F32)
    return (jnp.concatenate([cos_t, ident], axis=0), jnp.concatenate([sa_t, none], axis=0),
            jnp.concatenate([sb_t, none], axis=0))


def _dft_tables(length):
    n = 2 * length
    f = jnp.arange(length, dtype=jnp.int32)
    m = ((2 * f[:, None] + 1) * (2 * f[None, :] + 1)) % (4 * n)
    th = m.astype(F32) * (2.0 * math.pi / (4 * n))
    phi = (2 * f + 1).astype(F32) * (math.pi / (2 * n))
    return (jnp.cos(th).astype(BF16), jnp.sin(th).astype(BF16),
            jnp.cos(phi)[:, None], jnp.sin(phi)[:, None])


def _hyena_features(length):
    t = jnp.arange(length, dtype=F32)
    t_norm = t / max(length - 1, 1)
    bands = jnp.linspace(1e-4, HY_BANDS - 1, HY_BANDS, dtype=F32)
    ang = (2.0 * math.pi / length) * t[:, None] * bands[None, :]
    z = jnp.concatenate([t_norm[:, None], jnp.cos(ang), -jnp.sin(ang)], axis=-1)
    z = jnp.pad(z, ((0, 0), (0, 128 - z.shape[1])))
    decay = jnp.tile(jnp.linspace(HY_DECAY_MIN, HY_DECAY_MAX, HY_CH, dtype=F32), 2)
    return z, jnp.exp(-t_norm[:, None] * decay[None, :])


def _hyfilt_kernel(z_ref, w1_ref, b1_ref, w2_ref, b2_ref, w3_ref, fr_ref, dec_ref,
                   c_ref, s_ref, cp_ref, sp_ref, kre_ref, kim_ref, pq_ref, *, length):
    @pl.when(pl.program_id(0) == 0)
    def _():
        fr = fr_ref[...]
        h = jnp.sin(fr * (_dot(z_ref[...], w1_ref[...], HI) + b1_ref[...]))
        h = jnp.sin(fr * (_dot(h, w2_ref[...], HI) + b2_ref[...]))
        h = _dot(h, w3_ref[...], HI) * dec_ref[...]
        hf = h[:, :HY_CH]
        t = lax.broadcasted_iota(jnp.int32, (length, HY_CH), 0)
        hb = jnp.where(t == 0, 0.0, h[:, HY_CH:])
        pq_ref[:, :HY_CH] = (hf + hb).astype(BF16)
        pq_ref[:, HY_CH:] = (hb - hf).astype(BF16)

    pq = pq_ref[...]
    cpq = _dot(c_ref[...], pq)
    spq = _dot(s_ref[...], pq)
    cphi, sphi = cp_ref[...], sp_ref[...]
    norm = 1.0 / length
    kre_ref[...] = (cphi * cpq[:, :HY_CH] + sphi * spq[:, :HY_CH]) * norm
    kim_ref[...] = (cphi * spq[:, HY_CH:] - sphi * cpq[:, HY_CH:]) * norm


def _hyena_filter_spectrum(length, w1, b1, w2, b2, w3, freq, dft):
    z, dec = _hyena_features(length)
    c_m, s_m, cphi, sphi = dft
    w1p = jnp.pad(w1, ((0, 128 - w1.shape[0]), (0, 0)))
    tf = min(512, length)
    const = lambda j: (0, 0)
    blk = lambda j: (j, 0)
    return pl.pallas_call(
        functools.partial(_hyfilt_kernel, length=length),
        grid=(length // tf,),
        in_specs=[
            pl.BlockSpec((length, 128), const),
            pl.BlockSpec((128, 64), const),
            pl.BlockSpec((1, 64), const),
            pl.BlockSpec((64, 64), const),
            pl.BlockSpec((1, 64), const),
            pl.BlockSpec((64, 2 * HY_CH), const),
            pl.BlockSpec((1, 64), const),
            pl.BlockSpec((length, 2 * HY_CH), const),
            pl.BlockSpec((tf, length), blk),
            pl.BlockSpec((tf, length), blk),
            pl.BlockSpec((tf, 1), blk),
            pl.BlockSpec((tf, 1), blk),
        ],
        out_specs=[pl.BlockSpec((tf, HY_CH), blk)] * 2,
        out_shape=[jax.ShapeDtypeStruct((length, HY_CH), F32)] * 2,
        scratch_shapes=[pltpu.VMEM((length, 2 * HY_CH), BF16)],
        compiler_params=_cp("arbitrary"),
        name=f"hyena_filter_{length}",
    )(z, w1p, b1[None, :], w2, b2[None, :], w3, freq[None, :], dec, c_m, s_m, cphi, sphi)


def _short_conv3(x_ref, cw):
    x = x_ref[...].astype(F32)
    return _shift_rows(x, -1) * cw[0:1] + x * cw[1:2] + _shift_rows(x, 1) * cw[2:3]


def _hyena_kernel(p0_ref, p1_ref, p2_ref, cw_ref, bias_ref, c_ref, s_ref, kre_ref, kim_ref, o_ref):
    cw = cw_ref[...]
    x0 = _short_conv3(p0_ref, cw[:, :HY_CH])
    k = _short_conv3(p1_ref, cw[:, HY_CH:2 * HY_CH]) * _short_conv3(p2_ref, cw[:, 2 * HY_CH:])
    kb = k.astype(BF16)
    a = _dot(c_ref[...], kb)
    b = _dot(s_ref[...], kb)
    kre, kim = kre_ref[...], kim_ref[...]
    yre = (a * kre + b * kim).astype(BF16)
    yim = (b * kre - a * kim).astype(BF16)
    y = _dot(c_ref[...], yre) + _dot(s_ref[...], yim)
    o_ref[...] = (x0 * (y + k * bias_ref[...])).astype(o_ref.dtype)


def _hyena(p_hy, length, blk0, conv_w, bias, dft, kre, kim):
    c_m, s_m = dft[0], dft[1]
    const = lambda b: (0, 0)
    once = pl.Buffered(1)
    in_specs = [
        pl.BlockSpec((length, HY_CH), lambda b: (blk0 + b, 0)),
        pl.BlockSpec((length, HY_CH), lambda b: (blk0 + b, 1)),
        pl.BlockSpec((length, HY_CH), lambda b: (blk0 + b, 2)),
        pl.BlockSpec((3, 768), const),
        pl.BlockSpec((1, HY_CH), const),
        pl.BlockSpec((length, length), const, pipeline_mode=once),
        pl.BlockSpec((length, length), const, pipeline_mode=once),
        pl.BlockSpec((length, HY_CH), const, pipeline_mode=once),
        pl.BlockSpec((length, HY_CH), const, pipeline_mode=once),
    ]
    return pl.pallas_call(
        _hyena_kernel,
        grid=(B,),
        in_specs=in_specs,
        out_specs=pl.BlockSpec((length, HY_CH), lambda b: (b, 0)),
        out_shape=jax.ShapeDtypeStruct((B * length, HY_CH), BF16),
        compiler_params=_cp("arbitrary"),
        name=f"hyena_{length}",
    )(p_hy, p_hy, p_hy, conv_w, bias[None, :], c_m, s_m, kre, kim)


def _ones_half(vwin, half):
    lane = lax.broadcasted_iota(jnp.int32, vwin.shape, 1)
    keep = (lane < HD) if half == 0 else (lane >= HD)
    return jnp.where(keep, vwin, jnp.ones_like(vwin))


def _softmax_pv(heads):
    probs, maxes = [], []
    for parts, sink, _ in heads:
        m = None
        for s, _ in parts:
            mi = jnp.max(s, axis=-1, keepdims=True)
            m = mi if m is None else jnp.maximum(m, mi)
        if sink is not None:
            m = jnp.maximum(m, sink)
        probs.append([jnp.exp((s - m).astype(BF16)) for s, _ in parts])
        maxes.append(m)
    outs = []
    for (parts, sink, half), es, m in zip(heads, probs, maxes):
        acc = None
        for e, (_, v) in zip(es, parts):
            o = _dot(e, v)
            acc = o if acc is None else acc + o
        den = acc[:, HD:HD + 1] if half == 0 else acc[:, 0:1]
        if sink is not None:
            den = den + jnp.exp(sink - m)
        outs.append(acc / den)
    return outs


def _wa_kernel(sink_ref, q_ref, k_ref, v_ref, kc_ref, vc_ref, o_ref):
    kc, vc = kc_ref[...], vc_ref[...]
    vcs = [_ones_half(vc, hk) for hk in range(2)]
    rr = lax.broadcasted_iota(jnp.int32, (256, 384), 0)
    r1 = lax.broadcasted_iota(jnp.int32, (256, 1), 0)
    heads = []
    for j in range(WA_BLKS):
        n = pl.program_id(1) * WA_BLKS + j
        start = pl.multiple_of(jnp.clip((n - 1) * 128, 0, S - 384), 128)
        kw = k_ref[pl.ds(start, 384), :]
        vw = v_ref[pl.ds(start, 384), :]
        q = q_ref[j * 128:(j + 1) * 128, :] * SCALE
        qpos = n * 128 + jnp.where(rr >= 128, rr - 128, rr)
        kpos = start + lax.broadcasted_iota(jnp.int32, (256, 384), 1)
        valid = jnp.abs(qpos - kpos) <= WINDOW
        for hk in range(2):
            q2 = jnp.concatenate([q[:, (2 * hk) * HD:(2 * hk + 1) * HD],
                                  q[:, (2 * hk + 1) * HD:(2 * hk + 2) * HD]], axis=0)
            hs = slice(hk * HD, (hk + 1) * HD)
            s_loc = jnp.where(valid, _dot_nt(q2, kw[:, hs]), NEG)
            s_ctx = _dot_nt(q2, kc[:, hs])
            sink = jnp.where(r1 >= 128, sink_ref[2 * hk + 1], sink_ref[2 * hk])
            heads.append(([(s_ctx, vcs[hk]), (s_loc, _ones_half(vw, hk))], sink, hk))
    for i, res in enumerate(_softmax_pv(heads)):
        j, hk = divmod(i, 2)
        o = res[:, hk * HD:(hk + 1) * HD]
        rows = slice(j * 128, (j + 1) * 128)
        o_ref[rows, (2 * hk) * HD:(2 * hk + 1) * HD] = o[:128].astype(o_ref.dtype)
        o_ref[rows, (2 * hk + 1) * HD:(2 * hk + 2) * HD] = o[128:].astype(o_ref.dtype)


def _window_attention(p_wa, sink):
    nb = S // (128 * WA_BLKS)
    return pl.pallas_call(
        _wa_kernel,
        grid=(B, nb),
        in_specs=[
            pl.BlockSpec(memory_space=pltpu.SMEM),
            pl.BlockSpec((128 * WA_BLKS, 256), lambda b, n: (b * nb + n, 0)),
            pl.BlockSpec((S, 128), lambda b, n: (b, 2)),
            pl.BlockSpec((S, 128), lambda b, n: (b, 3)),
            pl.BlockSpec((LC, 128), lambda b, n: (CTX_BLK0 + b, 2)),
            pl.BlockSpec((LC, 128), lambda b, n: (CTX_BLK0 + b, 3)),
        ],
        out_specs=pl.BlockSpec((128 * WA_BLKS, 256), lambda b, n: (b * nb + n, 0)),
        out_shape=jax.ShapeDtypeStruct((NL, 256), BF16),
        compiler_params=_cp("arbitrary", "arbitrary"),
        name="window_attn",
    )(sink, p_wa, p_wa, p_wa, p_wa, p_wa)


def _na_kernel(q_ref, k_ref, v_ref, kc_ref, vc_ref, bias_ref, o_ref):
    kc, vc = kc_ref[...], vc_ref[...]
    vcs = [_ones_half(vc[:, (h // 2) * 128:(h // 2 + 1) * 128], h % 2) for h in range(4)]
    heads = []
    for j in range(NA_ROWS):
        r = pl.program_id(1) * NA_ROWS + j
        first = jnp.clip(r - NA_KR // 2, 0, S // GW - NA_KR)
        start = pl.multiple_of(first * GW, GW)
        off = first - r + NA_KR - 1
        kw = k_ref[pl.ds(start, NA_KR * GW), :]
        vw = v_ref[pl.ds(start, NA_KR * GW), :]
        q = q_ref[j * GW:(j + 1) * GW, :] * SCALE
        for h in range(4):
            hs = slice(h * HD, (h + 1) * HD)
            win = slice((h // 2) * 128, (h // 2 + 1) * 128)
            s_loc = _dot_nt(q[:, hs], kw[:, hs]) + bias_ref[h, off]
            s_ctx = _dot_nt(q[:, hs], kc[:, hs])
            heads.append(([(s_ctx, vcs[h]), (s_loc, _ones_half(vw[:, win], h % 2))], None, h % 2))
    outs = _softmax_pv(heads)
    lane = lax.broadcasted_iota(jnp.int32, (GW, 128), 1)
    for j in range(NA_ROWS):
        for hp in range(2):
            pair = jnp.where(lane < HD, outs[4 * j + 2 * hp], outs[4 * j + 2 * hp + 1])
            o_ref[j * GW:(j + 1) * GW, hp * 128:(hp + 1) * 128] = pair.astype(o_ref.dtype)


def _na_bias_kernel(rpb_ref, sel_ref, o_ref):
    o_ref[0] = _dot(rpb_ref[0], sel_ref[...], HI)


def _na_bias_table(rpb):
    qc = jnp.arange(GW)[:, None]
    kcol = jnp.arange(GW)[None, :]
    ci = jnp.clip(kcol - qc + NA_KC - 1, 0, 2 * NA_KC - 2).reshape(1, GW * GW)
    sel = (jnp.arange(128)[:, None] == ci).astype(F32)
    n_ri = 2 * NA_KR - 1
    rpb_p = jnp.pad(rpb.astype(F32), ((0, 0), (0, 16 - n_ri), (0, 128 - (2 * NA_KC - 1))))
    per_row = pl.pallas_call(
        _na_bias_kernel,
        grid=(4,),
        in_specs=[pl.BlockSpec((1, 16, 128), lambda h: (h, 0, 0)), pl.BlockSpec((128, GW * GW), lambda h: (0, 0))],
        out_specs=pl.BlockSpec((1, 16, GW * GW), lambda h: (h, 0, 0)),
        out_shape=jax.ShapeDtypeStruct((4, 16, GW * GW), F32),
        compiler_params=_cp("arbitrary"),
        name="na_bias",
    )(rpb_p, sel).reshape(4, 16, GW, GW)
    tbl = jnp.stack([per_row[:, o:o + NA_KR] for o in range(NA_KR)], axis=1)
    tbl = jnp.transpose(tbl, (0, 1, 3, 2, 4))
    c_start = jnp.clip(qc - NA_KC // 2, 0, GW - NA_KC)
    ok = (kcol >= c_start) & (kcol < c_start + NA_KC)
    tbl = jnp.where(ok[None, None, :, None, :], tbl, NEG)
    return tbl.reshape(4, NA_KR, GW, NA_KR * GW)


def _neighborhood_attention(p_na, bias_tbl):
    steps = S // GW // NA_ROWS
    return pl.pallas_call(
        _na_kernel,
        grid=(B, steps),
        in_specs=[
            pl.BlockSpec((NA_ROWS * GW, 256), lambda b, r: (b * steps + r, 0)),
            pl.BlockSpec((S, 256), lambda b, r: (b, 1)),
            pl.BlockSpec((S, 256), lambda b, r: (b, 2)),
            pl.BlockSpec((LC, 256), lambda b, r: (CTX_BLK0 + b, 1)),
            pl.BlockSpec((LC, 256), lambda b, r: (CTX_BLK0 + b, 2)),
            pl.BlockSpec((4, NA_KR, GW, NA_KR * GW), lambda b, r: (0, 0, 0, 0)),
        ],
        out_specs=pl.BlockSpec((NA_ROWS * GW, 256), lambda b, r: (b * steps + r, 0)),
        out_shape=jax.ShapeDtypeStruct((NL, 256), BF16),
        compiler_params=_cp("arbitrary", "arbitrary"),
        name="neighborhood_attn",
    )(p_na, p_na, p_na, p_na, p_na, bias_tbl)


def _ctx_attn_kernel(sink_ref, qw_ref, kw_ref, vw_ref, qn_ref, kn_ref, vn_ref, owa_ref, ona_ref):
    q, k, v = qw_ref[...] * SCALE, kw_ref[...], vw_ref[...]
    r1 = lax.broadcasted_iota(jnp.int32, (2 * LC, 1), 0)
    heads = []
    for hk in range(2):
        q2 = jnp.concatenate([q[:, (2 * hk) * HD:(2 * hk + 1) * HD],
                              q[:, (2 * hk + 1) * HD:(2 * hk + 2) * HD]], axis=0)
        hs = slice(hk * HD, (hk + 1) * HD)
        sink = jnp.where(r1 >= LC, sink_ref[2 * hk + 1], sink_ref[2 * hk])
        heads.append(([(_dot_nt(q2, k[:, hs]), _ones_half(v, hk))], sink, hk))
    for hk, res in enumerate(_softmax_pv(heads)):
        o = res[:, hk * HD:(hk + 1) * HD]
        owa_ref[:, (2 * hk) * HD:(2 * hk + 1) * HD] = o[:LC].astype(owa_ref.dtype)
        owa_ref[:, (2 * hk + 1) * HD:(2 * hk + 2) * HD] = o[LC:].astype(owa_ref.dtype)
    q, k, v = qn_ref[...] * SCALE, kn_ref[...], vn_ref[...]
    heads = []
    for h in range(4):
        hs = slice(h * HD, (h + 1) * HD)
        win = slice((h // 2) * 128, (h // 2 + 1) * 128)
        heads.append(([(_dot_nt(q[:, hs], k[:, hs]), _ones_half(v[:, win], h % 2))], None, h % 2))
    outs = _softmax_pv(heads)
    lane = lax.broadcasted_iota(jnp.int32, (LC, 128), 1)
    for hp in range(2):
        pair = jnp.where(lane < HD, outs[2 * hp], outs[2 * hp + 1])
        ona_ref[:, hp * 128:(hp + 1) * 128] = pair.astype(ona_ref.dtype)


def _ctx_attention(p_wa, p_na, sink):
    blk = lambda c: (lambda b: (CTX_BLK0 + b, c))
    return pl.pallas_call(
        _ctx_attn_kernel,
        grid=(B,),
        in_specs=[
            pl.BlockSpec(memory_space=pltpu.SMEM),
            pl.BlockSpec((LC, 256), blk(0)),
            pl.BlockSpec((LC, 128), blk(2)),
            pl.BlockSpec((LC, 128), blk(3)),
            pl.BlockSpec((LC, 256), blk(0)),
            pl.BlockSpec((LC, 256), blk(1)),
            pl.BlockSpec((LC, 256), blk(2)),
        ],
        out_specs=[pl.BlockSpec((LC, 256), lambda b: (b, 0))] * 2,
        out_shape=[jax.ShapeDtypeStruct((NC, 256), BF16)] * 2,
        compiler_params=_cp("arbitrary"),
        name="ctx_attn",
    )(sink, p_wa, p_wa, p_wa, p_na, p_na, p_na)


def _gdn_prep_seq(p_ref, ab_ref, cw, alog, dtb, qkv_ref, gb_ref, row0, length):
    x = p_ref[:, :768].astype(F32)
    u = (_shift_rows(x, -2) * cw[0:1] + _shift_rows(x, -1) * cw[1:2] + x * cw[2:3]
         + _shift_rows(x, 1) * cw[3:4])
    u = _silu(u)
    rows = slice(row0, row0 + length)
    for j in range(8):
        sl = slice(j * HD, (j + 1) * HD)
        xs = u[:, sl]
        nrm = lax.rsqrt(jnp.sum(xs * xs, axis=-1, keepdims=True) + EPS)
        qkv_ref[0, rows, sl] = xs * (nrm * SCALE if j < 4 else nrm)
    qkv_ref[0, rows, 512:768] = u[:, 512:768]
    ab = ab_ref[...]
    g = -jnp.exp(alog) * _softplus(ab + dtb)
    lane = lax.broadcasted_iota(jnp.int32, ab.shape, 1)
    gb_ref[0, rows, :] = jnp.where(lane < 8, g, _sigmoid(ab))


def _gdn_prep_kernel(pl_ref, abl_ref, pc_ref, abc_ref, cw_ref, alog_ref, dtb_ref, qkv_ref, gb_ref):
    cw, alog, dtb = cw_ref[...], alog_ref[...], dtb_ref[...]
    _gdn_prep_seq(pl_ref, abl_ref, cw, alog, dtb, qkv_ref, gb_ref, 0, S)
    _gdn_prep_seq(pc_ref, abc_ref, cw, alog, dtb, qkv_ref, gb_ref, S, LC)


def _gdn_prep(p_gd, p_ab, conv_w, a_log, dt_bias):
    const = lambda b: (0, 0)
    alog = jnp.pad(a_log.reshape(1, 8), ((0, 0), (0, 120)))
    dtb = jnp.pad(dt_bias.reshape(1, 8), ((0, 0), (0, 120)))
    return pl.pallas_call(
        _gdn_prep_kernel,
        grid=(B,),
        in_specs=[
            pl.BlockSpec((S, 1024), lambda b: (b, 0)),
            pl.BlockSpec((S, 128), lambda b: (b, 0)),
            pl.BlockSpec((LC, 1024), lambda b: (CTX_BLK0 + b, 0)),
            pl.BlockSpec((LC, 128), lambda b: (CTX_BLK0 + b, 0)),
            pl.BlockSpec((4, 768), const),
            pl.BlockSpec((1, 128), const),
            pl.BlockSpec((1, 128), const),
        ],
        out_specs=[pl.BlockSpec((1, S + LC, 768), lambda b: (b, 0, 0)),
                   pl.BlockSpec((1, S + LC, 128), lambda b: (b, 0, 0))],
        out_shape=[jax.ShapeDtypeStruct((B, S + LC, 768), F32), jax.ShapeDtypeStruct((B, S + LC, 128), F32)],
        compiler_params=_cp("arbitrary"),
        name="gdn_prep",
    )(p_gd, p_ab, p_gd, p_ab, conv_w, alog, dtb)


def _gdn_chains(xs, gbs, cums, cum_ts, states, masks):
    chains = [(g, h) for g in range(len(xs)) for h in range(4)]
    pre = []
    for g, h in chains:
        dirn = g % 2
        x, gb = xs[g], gbs[g]
        tri, strict = masks[dirn]
        col = dirn * 4 + h
        qh = x[:, h * HD:(h + 1) * HD]
        kh = x[:, 256 + h * HD:256 + (h + 1) * HD]
        vh = x[:, 512 + h * HD:512 + (h + 1) * HD]
        gc = cums[g][:, col:col + 1]
        gc_row = cum_ts[g][col:col + 1, :]
        beta = gb[:, 8 + col:9 + col]
        decay = jnp.exp(jnp.where(tri, gc - gc_row, NEG))
        kb = kh * beta
        eg = jnp.exp(gc)
        g_last = gc[0:1, :] if dirn else gc[CHUNK - 1:CHUNK, :]
        pre.append(dict(strict=strict, decay=decay, kbf=kb.astype(BF16), khf=kh.astype(BF16), qf=qh.astype(BF16),
                        xw=jnp.concatenate([vh * beta, kb * eg], axis=1), qdec=(qh * eg).astype(BF16),
                        kdec=(kh * jnp.exp(g_last - gc)).astype(BF16), gl=jnp.exp(g_last)))
    n = len(chains)
    gram = [_dot_nt(p["kbf"], p["khf"]) for p in pre]
    attn = [_dot_nt(p["qf"], p["khf"]) for p in pre]
    lmat = [jnp.where(p["strict"], g * p["decay"], 0.0) for p, g in zip(pre, gram)]
    attn = [(a * p["decay"]).astype(BF16) for p, a in zip(pre, attn)]
    ii = lax.broadcasted_iota(jnp.int32, (CHUNK, CHUNK), 0)
    jj = lax.broadcasted_iota(jnp.int32, (CHUNK, CHUNK), 1)

    def merged_off_blocks(log_s):
        return ((ii >> (log_s + 1)) == (jj >> (log_s + 1))) & ((ii >> log_s) != (jj >> log_s))

    eye = (ii == jj).astype(F32)
    dinv = [eye - jnp.where(merged_off_blocks(0), lm, 0.0) for lm in lmat]
    for log_s in range(1, 6):
        m = merged_off_blocks(log_s)
        dinv_b = [d.astype(BF16) for d in dinv]
        ld = [_dot(jnp.where(m, lmat[c], 0.0).astype(BF16), dinv_b[c]) for c in range(n)]
        upd = [_dot(dinv_b[c], ld[c].astype(BF16)) for c in range(n)]
        dinv = [dinv[c] - upd[c] for c in range(n)]
    xw = [_dot(dinv[c].astype(BF16), pre[c]["xw"].astype(BF16)) for c in range(n)]
    sts = [states[g][h] for g, h in chains]
    stb = [s.astype(BF16) for s in sts]
    ws = [_dot(xw[c][:, HD:].astype(BF16), stb[c]) for c in range(n)]
    qs = [_dot(pre[c]["qdec"], stb[c]) for c in range(n)]
    u_new = [(xw[c][:, :HD] - ws[c]).astype(BF16) for c in range(n)]
    au = [_dot(attn[c], u_new[c]) for c in range(n)]
    ku = [_dot_tn(pre[c]["kdec"], u_new[c]) for c in range(n)]
    outs = [qs[c] + au[c] for c in range(n)]
    new_states = [sts[c] * pre[c]["gl"] + ku[c] for c in range(n)]
    return outs, new_states


def _gdn_kernel(x0_ref, gb0_ref, x1_ref, gb1_ref, o0_ref, o1_ref, state_ref):
    @pl.when(pl.program_id(0) == 0)
    def _():
        state_ref[...] = jnp.zeros_like(state_ref)

    ii = lax.broadcasted_iota(jnp.int32, (CHUNK, CHUNK), 0)
    jj = lax.broadcasted_iota(jnp.int32, (CHUNK, CHUNK), 1)
    masks = ((jj <= ii, jj < ii), (jj >= ii, jj > ii))
    ones_tri = ((jj <= ii).astype(F32), (jj >= ii).astype(F32))

    def body(bp, carry):
        bs = [bp * GDN_SEQS + i for i in range(GDN_SEQS)]
        states = [[state_ref[b, dirn, h] for h in range(4)] for b in bs for dirn in range(2)]
        xs = [r[b] for b in bs for r in (x0_ref, x1_ref)]
        gbs = [r[b] for b in bs for r in (gb0_ref, gb1_ref)]
        cums = [_dot(ones_tri[g % 2], gbs[g], HI) for g in range(2 * GDN_SEQS)]
        cum_ts = [cm.T for cm in cums]
        outs, new_states = _gdn_chains(xs, gbs, cums, cum_ts, states, masks)
        for c, st in enumerate(new_states):
            state_ref[bs[c // 8], (c // 4) % 2, c % 4] = st
        for i, b in enumerate(bs):
            o0_ref[b] = jnp.concatenate(outs[8 * i:8 * i + 4], axis=1)
            o1_ref[b] = jnp.concatenate(outs[8 * i + 4:8 * i + 8], axis=1)
        return carry

    lax.fori_loop(0, B // GDN_SEQS, body, 0)


def _gdn_scan(qkv, gb):
    nc = (S + LC) // CHUNK
    fwd = lambda c: (0, (c + S // CHUNK) % nc, 0)
    bwd = lambda c: (0, nc - 1 - c, 0)
    return pl.pallas_call(
        _gdn_kernel,
        grid=(nc,),
        in_specs=[
            pl.BlockSpec((B, CHUNK, 768), fwd),
            pl.BlockSpec((B, CHUNK, 128), fwd),
            pl.BlockSpec((B, CHUNK, 768), bwd),
            pl.BlockSpec((B, CHUNK, 128), bwd),
        ],
        out_specs=[pl.BlockSpec((B, CHUNK, 256), fwd), pl.BlockSpec((B, CHUNK, 256), bwd)],
        out_shape=[jax.ShapeDtypeStruct((B, S + LC, 256), F32)] * 2,
        scratch_shapes=[pltpu.VMEM((B, 2, 4, HD, HD), F32)],
        compiler_params=_cp("arbitrary"),
        name="gdn_scan",
    )(qkv, gb, qkv, gb)


def _gdn_finish_kernel(o0_ref, o1_ref, gate_ref, ng_ref, y_ref):
    o = o0_ref[0] + o1_ref[0]
    gi = lax.broadcasted_iota(jnp.int32, (256, 256), 0) // HD
    gj = lax.broadcasted_iota(jnp.int32, (256, 256), 1) // HD
    group_mean = jnp.where(gi == gj, 1.0 / HD, 0.0)
    ms = _dot(o * o, group_mean, HI)
    y = o * lax.rsqrt(ms + EPS) * ng_ref[...]
    y_ref[...] = (y * _silu(gate_ref[...].astype(F32))).astype(y_ref.dtype)


def _gdn_finish(o0, o1, p_gd, norm_g, need_ctx):
    lat_tiles = S // TM
    row_blk = lambda b, j: jnp.where(j < lat_tiles, b * lat_tiles + j, CTX_BLK0 + b)
    return pl.pallas_call(
        _gdn_finish_kernel,
        grid=(B, lat_tiles + (1 if need_ctx else 0)),
        in_specs=[
            pl.BlockSpec((1, TM, 256), lambda b, j: (b, j, 0)),
            pl.BlockSpec((1, TM, 256), lambda b, j: (b, j, 0)),
            pl.BlockSpec((TM, 256), lambda b, j: (row_blk(b, j), 3)),
            pl.BlockSpec((1, 256), lambda b, j: (0, 0)),
        ],
        out_specs=pl.BlockSpec((TM, 256), lambda b, j: (row_blk(b, j), 0)),
        out_shape=jax.ShapeDtypeStruct((ROWS if need_ctx else NL, 256), BF16),
        compiler_params=_cp("arbitrary", "arbitrary"),
        name="gdn_finish",
    )(o0, o1, p_gd, jnp.tile(norm_g, 4)[None, :])


def _pack_bf16_pairs(x):
    n = x.shape[1] // 2
    lo = pltpu.bitcast(x[:, :n].astype(BF16).astype(F32), jnp.uint32)
    hi = pltpu.bitcast(x[:, n:].astype(BF16).astype(F32), jnp.uint32)
    return hi | (lo >> 16)


def _unpack_bf16_pairs(w):
    lo = pltpu.bitcast(w << 16, F32)
    hi = pltpu.bitcast(w & jnp.uint32(0xFFFF0000), F32)
    return jnp.concatenate([lo, hi], axis=1)


def _outproj_kernel(x_ref, mod_ref, g2_ref, yh_ref, yw_ref, yn_ref, yg_ref, wo_ref, wr_ref, eb_ref,
                    xo_ref, h2_ref, h2p_ref, idx_ref, tw_ref, rank_ref, cnt_ref, carry_ref):
    @pl.when(pl.program_id(0) == 0)
    def _():
        carry_ref[...] = jnp.zeros_like(carry_ref)

    m = mod_ref[0]
    acc = (_dot(yh_ref[...], wo_ref[0:256, :]) + _dot(yw_ref[...], wo_ref[256:512, :])
           + _dot(yn_ref[...], wo_ref[512:768, :]) + _dot(yg_ref[...], wo_ref[768:1024, :]))
    x = x_ref[...] + m[2:3] * acc
    xo_ref[...] = x
    h2 = _modulated_norm(x, g2_ref[...], m[3:4], m[4:5])
    h2_ref[...] = h2.astype(h2_ref.dtype)
    h2p_ref[...] = _pack_bf16_pairs(h2)
    scores = _sigmoid(_dot(h2, wr_ref[...], HI))
    lane = lax.broadcasted_iota(jnp.int32, scores.shape, 1)
    sel = jnp.where(lane < N_EXP, scores + eb_ref[...], -jnp.inf)
    lane_f = lane.astype(F32)
    idx_out = jnp.zeros(scores.shape, F32)
    s_out = jnp.zeros(scores.shape, F32)
    hits = []
    for kk in range(TOP_K):
        mx = jnp.max(sel, axis=-1, keepdims=True)
        idx = jnp.min(jnp.where(sel == mx, lane_f, 128.0), axis=-1, keepdims=True)
        hit = lane_f == idx
        sk = jnp.sum(jnp.where(hit, scores, 0.0), axis=-1, keepdims=True)
        sel = jnp.where(hit, -jnp.inf, sel)
        idx_out = jnp.where(lane == kk, idx, idx_out)
        s_out = jnp.where(lane == kk, sk, s_out)
        hits.append(hit)
    tot = jnp.sum(s_out, axis=-1, keepdims=True)
    tw_ref[...] = s_out / tot * ROUTED_SCALE
    cnt = jnp.zeros(scores.shape, F32)
    for hit in hits:
        cnt = cnt + jnp.where(hit, 1.0, 0.0)
    ri = lax.broadcasted_iota(jnp.int32, (TM, TM), 0)
    rj = lax.broadcasted_iota(jnp.int32, (TM, TM), 1)
    before = _dot(jnp.where(rj < ri, 1.0, 0.0).astype(BF16), cnt.astype(BF16)) + carry_ref[...]
    rank_out = jnp.zeros(scores.shape, F32)
    for kk, hit in enumerate(hits):
        rk = jnp.sum(jnp.where(hit, before, 0.0), axis=-1, keepdims=True)
        rank_out = jnp.where(lane == kk, rk, rank_out)
    idx_ref[...] = idx_out.T[:8].astype(jnp.int32)
    rank_ref[...] = rank_out.T[:8].astype(jnp.int32)
    carry_ref[...] = carry_ref[...] + jnp.sum(cnt, axis=0, keepdims=True)
    cnt_ref[...] = carry_ref[...].astype(jnp.int32)


def _outproj(xa, mod, l, g2, ys, wo, wr, eb, n_tiles):
    row = lambda i: (i, 0)
    const = lambda i: (0, 0)
    rows = n_tiles * TM
    return pl.pallas_call(
        _outproj_kernel,
        grid=(n_tiles,),
        in_specs=[
            pl.BlockSpec((TM, D), row),
            pl.BlockSpec((1, 6, D), lambda i: (l * 16 + (i * TM) // S, 0, 0)),
            pl.BlockSpec((1, D), const),
            pl.BlockSpec((TM, 256), row),
            pl.BlockSpec((TM, 256), row),
            pl.BlockSpec((TM, 256), row),
            pl.BlockSpec((TM, 256), row),
            pl.BlockSpec((D, D), const),
            pl.BlockSpec((D, 128), const),
            pl.BlockSpec((1, 128), const),
        ],
        out_specs=[
            pl.BlockSpec((TM, D), row),
            pl.BlockSpec((TM, D), row),
            pl.BlockSpec((TM, D // 2), row),
            pl.BlockSpec((8, TM), row),
            pl.BlockSpec((TM, 128), row),
            pl.BlockSpec((8, TM), row),
            pl.BlockSpec((1, 128), const),
        ],
        out_shape=[
            jax.ShapeDtypeStruct((rows, D), F32),
            jax.ShapeDtypeStruct((rows, D), BF16),
            jax.ShapeDtypeStruct((rows, D // 2), jnp.uint32),
            jax.ShapeDtypeStruct((n_tiles * 8, TM), jnp.int32),
            jax.ShapeDtypeStruct((rows, 128), F32),
            jax.ShapeDtypeStruct((n_tiles * 8, TM), jnp.int32),
            jax.ShapeDtypeStruct((1, 128), jnp.int32),
        ],
        scratch_shapes=[pltpu.VMEM((1, 128), F32)],
        compiler_params=_cp("arbitrary"),
        name="outproj_router",
    )(xa, mod, g2, *ys, wo, wr, eb)


ASG_TILE = TM * TOP_K


def _segment_tables(counts, n_blocks):
    counts = counts[0, :N_EXP]
    padded = (counts + MOE_BLK - 1) // MOE_BLK * MOE_BLK
    pad_end = jnp.cumsum(padded)
    seg_start = pad_end - padded
    starts = jnp.arange(n_blocks, dtype=jnp.int32) * MOE_BLK
    blk_e = jnp.minimum(jnp.sum((pad_end[None, :] <= starts[:, None]).astype(jnp.int32), axis=1), N_EXP - 1)
    seg = jnp.concatenate([seg_start, seg_start + counts, pad_end[-1:]])
    blk = jnp.concatenate([blk_e, pad_end[-1:] // MOE_BLK])
    return seg.astype(jnp.int32), blk.astype(jnp.int32)


def _row_copy_wait(shape_ref, dst_ref, sem):
    pltpu.make_async_copy(shape_ref, dst_ref, sem).wait()


def _zero_fill_rows(zero_ref, xb_ref, start, length, sem):
    plan = []
    aligned = (start + 7) & -8
    for r in range(7):
        plan.append(((start + r < aligned) & (r < length),
                     pltpu.make_async_copy(zero_ref.at[pl.ds(0, 1)], xb_ref.at[pl.ds(start + r, 1)], sem)))
    rest = start + length - aligned
    off = aligned
    for bit in reversed(range(3, MOE_BLK.bit_length() - 1)):
        size = 1 << bit
        take = (rest & size) != 0
        dst = xb_ref.at[pl.ds(pl.multiple_of(off, 8), size)]
        plan.append((take, pltpu.make_async_copy(zero_ref.at[pl.ds(0, size)], dst, sem)))
        off = off + jnp.where(take, size, 0)
    for take, copy in plan:
        pl.when(take)(copy.start)
    for take, copy in plan:
        pl.when(take)(copy.wait)


def _dest_kernel(seg_ref, idx_ref, rank_ref, o_ref):
    idx = idx_ref[...]
    first = jnp.zeros(idx.shape, jnp.int32)
    for e in range(N_EXP):
        first = jnp.where(idx == e, seg_ref[e], first)
    o_ref[...] = first + rank_ref[...]


def _dest_rows(seg, top_idx, rank):
    n_tiles = top_idx.shape[0] // 8
    blk = lambda i, seg: (i, 0)
    grid_spec = pltpu.PrefetchScalarGridSpec(
        num_scalar_prefetch=1,
        grid=(n_tiles // 8,),
        in_specs=[pl.BlockSpec((64, TM), blk), pl.BlockSpec((64, TM), blk)],
        out_specs=pl.BlockSpec((64, TM), blk),
    )
    return pl.pallas_call(
        _dest_kernel,
        grid_spec=grid_spec,
        out_shape=jax.ShapeDtypeStruct(top_idx.shape, jnp.int32),
        compiler_params=_cp("arbitrary"),
        name="moe_dest_rows",
    )(seg, top_idx, rank)


def _dispatch_kernel(seg_ref, dest_ref, h2p_ref, xb_ref, zero_ref, ring_ref, sems, zsem):
    i = pl.program_id(0)
    sem = sems.at[i % 2]
    tile_ref = ring_ref.at[i % 2]
    tile_ref[...] = h2p_ref[...]

    @pl.when(i == 0)
    def _():
        zero_ref[...] = jnp.zeros_like(zero_ref)

    @pl.when(i < N_EXP)
    def _():
        first_pad = seg_ref[N_EXP + i]
        next_start = seg_ref[jnp.where(i == N_EXP - 1, 2 * N_EXP, i + 1)]
        _zero_fill_rows(zero_ref, xb_ref, first_pad, next_start - first_pad, zsem)

    tail_row = seg_ref[2 * N_EXP] + (i - N_EXP) * MOE_BLK

    @pl.when((i >= N_EXP) & (tail_row < xb_ref.shape[0]))
    def _():
        copy = pltpu.make_async_copy(zero_ref, xb_ref.at[pl.ds(pl.multiple_of(tail_row, MOE_BLK), MOE_BLK)], zsem)
        copy.start()
        copy.wait()

    def issue(t, carry):
        src = tile_ref.at[pl.ds(t, 1)]
        for k in range(TOP_K):
            pltpu.make_async_copy(src, xb_ref.at[pl.ds(dest_ref[k, t], 1)], sem).start(priority=k % 2)
        return carry

    lax.fori_loop(0, TM, issue, 0, unroll=True)

    @pl.when(i > 0)
    def _():
        for _ in range(TOP_K):
            _row_copy_wait(h2p_ref, xb_ref.at[pl.ds(0, TM)], sems.at[(i + 1) % 2])

    @pl.when(i == pl.num_programs(0) - 1)
    def _():
        for _ in range(TOP_K):
            _row_copy_wait(h2p_ref, xb_ref.at[pl.ds(0, TM)], sem)


def _dispatch_rows(seg, dest, h2p, n_rows):
    n_tiles = h2p.shape[0] // TM
    assert n_tiles >= 2 * N_EXP
    grid_spec = pltpu.PrefetchScalarGridSpec(
        num_scalar_prefetch=1,
        grid=(n_tiles,),
        in_specs=[
            pl.BlockSpec((8, TM), lambda i, seg: (i, 0), memory_space=pltpu.SMEM),
            pl.BlockSpec((TM, D // 2), lambda i, seg: (i, 0)),
        ],
        out_specs=pl.BlockSpec(memory_space=pl.ANY),
        scratch_shapes=[pltpu.VMEM((MOE_BLK, D // 2), jnp.uint32), pltpu.VMEM((2, TM, D // 2), jnp.uint32),
                        pltpu.SemaphoreType.DMA((2,)), pltpu.SemaphoreType.DMA(())],
    )
    return pl.pallas_call(
        _dispatch_kernel,
        grid_spec=grid_spec,
        out_shape=jax.ShapeDtypeStruct((n_rows, D // 2), jnp.uint32),
        compiler_params=_cp("arbitrary"),
        name="moe_dispatch",
    )(seg, dest, h2p)


def _moe_kernel(be_ref, x_ref, wg_ref, wu_ref, wd_ref, o_ref, wgb_ref, wub_ref, wdb_ref):
    i = pl.program_id(0)
    in_use = i < be_ref[pl.num_programs(0)]

    @pl.when(in_use & ((i == 0) | (be_ref[i] != be_ref[jnp.maximum(i - 1, 0)])))
    def _():
        wgb_ref[...] = wg_ref[0, 0].astype(BF16)
        wub_ref[...] = wu_ref[0, 0].astype(BF16)
        wdb_ref[...] = wd_ref[0, 0].astype(BF16)

    @pl.when(in_use)
    def _():
        x = _unpack_bf16_pairs(x_ref[...]).astype(BF16)
        g = _dot(x, wgb_ref[...])
        u = _dot(x, wub_ref[...])
        hid = (_silu(g) * u).astype(BF16)
        o_ref[...] = _pack_bf16_pairs(_dot(hid, wdb_ref[...]))

    @pl.when(jnp.logical_not(in_use))
    def _():
        o_ref[...] = jnp.zeros_like(o_ref)


def _moe_experts(blk_e, xb, l, wg, wu, wd):
    n_rows = xb.shape[0]
    grid_spec = pltpu.PrefetchScalarGridSpec(
        num_scalar_prefetch=1,
        grid=(n_rows // MOE_BLK,),
        in_specs=[
            pl.BlockSpec((MOE_BLK, D // 2), lambda i, be: (i, 0)),
            pl.BlockSpec((1, 1, D, D_EXP), lambda i, be: (l, be[i], 0, 0)),
            pl.BlockSpec((1, 1, D, D_EXP), lambda i, be: (l, be[i], 0, 0)),
            pl.BlockSpec((1, 1, D_EXP, D), lambda i, be: (l, be[i], 0, 0)),
        ],
        out_specs=pl.BlockSpec((MOE_BLK, D // 2), lambda i, be: (i, 0)),
        scratch_shapes=[pltpu.VMEM((D, D_EXP), BF16), pltpu.VMEM((D, D_EXP), BF16), pltpu.VMEM((D_EXP, D), BF16)],
    )
    return pl.pallas_call(
        _moe_kernel,
        grid_spec=grid_spec,
        out_shape=jax.ShapeDtypeStruct((n_rows, D // 2), jnp.uint32),
        compiler_params=_cp("arbitrary"),
        name="moe_experts",
    )(blk_e, xb, wg, wu, wd)


def _ffn_out_kernel(dest_ref, x_ref, mod_ref, h2_ref, tw_ref, wg_ref, wu_ref, wd_ref, nf_ref,
                    yb_ref, o_ref, gat_ref, sems, *, final):
    s = pl.program_id(0)

    def start_gathers(slot):
        buf = gat_ref.at[slot]

        def issue(t, carry):
            for k in range(TOP_K):
                copy = pltpu.make_async_copy(yb_ref.at[pl.ds(dest_ref[k, t], 1)], buf.at[pl.ds(k * TM + t, 1)],
                                             sems.at[slot])
                copy.start(priority=k % 2)
            return carry

        lax.fori_loop(0, TM, issue, 0, unroll=True)

    def wait_gathers(slot):
        for _ in range(TOP_K):
            _row_copy_wait(yb_ref.at[pl.ds(0, TM)], gat_ref.at[slot, pl.ds(0, TM)], sems.at[slot])

    pl.when(s == 0)(lambda: start_gathers(1))
    start_gathers(s % 2)
    m = mod_ref[0]
    h2 = h2_ref[...]
    hid = (_silu(_dot(h2, wg_ref[...])) * _dot(h2, wu_ref[...])).astype(BF16)
    shared = _dot(hid, wd_ref[...])
    done = (s + 1) % 2
    wait_gathers(done)
    buf = gat_ref.at[done]
    tw = tw_ref[...]
    routed = _unpack_bf16_pairs(buf[0:TM, :]) * tw[:, 0:1]
    for k in range(1, TOP_K):
        routed = routed + _unpack_bf16_pairs(buf[k * TM:(k + 1) * TM, :]) * tw[:, k:k + 1]
    x = x_ref[...] + m[5:6] * (routed + shared)
    if final:
        ms = jnp.mean(x * x, axis=-1, keepdims=True)
        x = x * lax.rsqrt(ms + EPS) * nf_ref[...]
    o_ref[...] = x
    pl.when(s == pl.num_programs(0) - 1)(lambda: wait_gathers(s % 2))


def _ffn_out(dest, xn, mod, l, h2, top_w, wg, wu, wd, norm_f, yb, n_tiles, final):
    row = lambda i: (jnp.maximum(i - 1, 0), 0)
    ahead = lambda i: (jnp.minimum(i, n_tiles - 1), 0)
    const = lambda i: (0, 0)
    return pl.pallas_call(
        functools.partial(_ffn_out_kernel, final=final),
        grid=(n_tiles + 1,),
        in_specs=[
            pl.BlockSpec((8, TM), ahead, memory_space=pltpu.SMEM),
            pl.BlockSpec((TM, D), row),
            pl.BlockSpec((1, 6, D), lambda i: (l * 16 + (jnp.maximum(i - 1, 0) * TM) // S, 0, 0)),
            pl.BlockSpec((TM, D), row),
            pl.BlockSpec((TM, 128), row),
            pl.BlockSpec((D, D_EXP), const),
            pl.BlockSpec((D, D_EXP), const),
            pl.BlockSpec((D_EXP, D), const),
            pl.BlockSpec((1, D), const),
            pl.BlockSpec(memory_space=pl.ANY),
        ],
        out_specs=pl.BlockSpec((TM, D), row),
        scratch_shapes=[pltpu.VMEM((2, ASG_TILE, D // 2), jnp.uint32), pltpu.SemaphoreType.DMA((2,))],
        out_shape=jax.ShapeDtypeStruct((n_tiles * TM, D), F32),
        compiler_params=_cp("arbitrary"),
        name="shared_ffn_residual",
    )(dest, xn, mod, h2, top_w, wg, wu, wd, norm_f, yb)


def kernel(x, c, ctx, c_ctx, w_ada, b_ada, norm1, norm2, norm_f, w_in, w_out, hy_conv, hy_w1, hy_b1, hy_w2, hy_b2, hy_w3, hy_freq, hy_bias, wa_sink, na_rpb, gdn_conv, gdn_a_log, gdn_dt_bias, gdn_norm, moe_router, moe_bias, moe_gate, moe_up, moe_down, sh_gate, sh_up, sh_down):
    depth = w_ada.shape[0]
    xa = jnp.concatenate([x.reshape(NL, D), ctx.reshape(NC, D)], axis=0)
    cvec = jnp.concatenate([c, c_ctx[None, :], jnp.zeros((16 - B - 1, D), F32)], axis=0)
    mod = _ada(cvec, w_ada, b_ada).reshape(depth * 16, 6, D)
    rope = _rope_tables()
    dft_lat = _dft_tables(S)
    dft_ctx = _dft_tables(LC)
    o1, o2, o3 = 768, 768 + 512, 768 + 512 + 768

    for l in range(depth):
        need_ctx = l < depth - 1
        n_tiles = NT_ALL if need_ctx else NT_LAT
        wl = w_in[l].astype(BF16)
        ws = (wl[:, :o1], wl[:, o1:o2], wl[:, o2:o3], wl[:, o3:o3 + 1024],
              jnp.pad(wl[:, o3 + 1024:], ((0, 0), (0, 128 - 16))))
        p_hy, p_wa, p_na, p_gd, p_ab = _inproj(xa, mod, l, norm1[l][None, :], ws, rope)

        filt = (hy_w1[l], hy_b1[l], hy_w2[l], hy_b2[l], hy_w3[l], hy_freq[l])
        kre, kim = _hyena_filter_spectrum(S, *filt, dft_lat)
        y_hy = _hyena(p_hy, S, 0, hy_conv[l], hy_bias[l], dft_lat, kre, kim)
        y_wa = _window_attention(p_wa, wa_sink[l])
        y_na = _neighborhood_attention(p_na, _na_bias_table(na_rpb[l]))
        if need_ctx:
            kre_c, kim_c = _hyena_filter_spectrum(LC, *filt, dft_ctx)
            yc_hy = _hyena(p_hy, LC, CTX_BLK0, hy_conv[l], hy_bias[l], dft_ctx, kre_c, kim_c)
            yc_wa, yc_na = _ctx_attention(p_wa, p_na, wa_sink[l])
            y_hy = jnp.concatenate([y_hy, yc_hy], axis=0)
            y_wa = jnp.concatenate([y_wa, yc_wa], axis=0)
            y_na = jnp.concatenate([y_na, yc_na], axis=0)
        qkv, gb = _gdn_prep(p_gd, p_ab, gdn_conv[l], gdn_a_log[l], gdn_dt_bias[l])
        y_gd = _gdn_finish(*_gdn_scan(qkv, gb), p_gd, gdn_norm[l], need_ctx)

        wr = jnp.pad(moe_router[l], ((0, 0), (0, 128 - N_EXP)))
        eb = jnp.pad(moe_bias[l], (0, 128 - N_EXP))[None, :]
        xn, h2, h2p, top_idx, top_w, rank, counts = _outproj(
            xa, mod, l, norm2[l][None, :], (y_hy, y_wa, y_na, y_gd), w_out[l].astype(BF16), wr, eb, n_tiles)
        n_asg = n_tiles * ASG_TILE
        n_blocks = n_asg // MOE_BLK + N_EXP
        seg, blk_e = _segment_tables(counts, n_blocks)
        dest = _dest_rows(seg, top_idx, rank)
        xb = _dispatch_rows(seg, dest, h2p, n_blocks * MOE_BLK)
        yb = _moe_experts(blk_e, xb, l, moe_gate, moe_up, moe_down)
        xa_new = _ffn_out(dest, xn, mod, l, h2, top_w, sh_gate[l].astype(BF16),
                          sh_up[l].astype(BF16), sh_down[l].astype(BF16), norm_f[None, :], yb, n_tiles,
                          final=not need_ctx)
        if need_ctx:
            xa = xa_new
        else:
            return xa_new.reshape(B, S, D)
```

```python
import functools
import math

import jax
import jax.numpy as jnp
from jax import lax
from jax.experimental import pallas as pl
from jax.experimental.pallas import tpu as pltpu

F32 = jnp.float32
BF16 = jnp.bfloat16
HI = lax.Precision.HIGHEST

D = 1024
B = 8
S = 2048
LC = 256
GW = 64
HD = 64
NL = B * S
NC = B * LC
ROWS = NL + NC
TM = 256
NT_LAT = NL // TM
NT_ALL = ROWS // TM
CTX_BLK0 = NL // LC

HY_CH = 256
HY_BANDS = 16
HY_DECAY_MIN = -math.log(1e-2) / 1.5
HY_DECAY_MAX = -math.log(1e-2) / 0.3
WINDOW = 128
NA_KR = 8
NA_KC = 16
CHUNK = 64
N_EXP = 32
TOP_K = 4
D_EXP = 256
ROUTED_SCALE = 2.5
MOE_BLK = 512
WA_BLKS = 2
NA_ROWS = 4
GDN_SEQS = 4
EPS = 1e-6
NEG = -1e30
SCALE = HD ** -0.5
VMEM_LIMIT = 56 * 1024 * 1024


def _cp(*sem):
    return pltpu.CompilerParams(dimension_semantics=tuple(sem), vmem_limit_bytes=VMEM_LIMIT)


def _dot(a, b, precision=None):
    return jnp.dot(a, b, preferred_element_type=F32, precision=precision)


def _dot_nt(a, b, precision=None):
    return lax.dot_general(a, b, (((1,), (1,)), ((), ())), preferred_element_type=F32, precision=precision)


def _dot_tn(a, b, precision=None):
    return lax.dot_general(a, b, (((0,), (0,)), ((), ())), preferred_element_type=F32, precision=precision)


def _sigmoid(x):
    return 1.0 / (1.0 + jnp.exp(-x))


def _silu(x):
    return x * _sigmoid(x)


def _softplus(x):
    return jnp.maximum(x, 0.0) + jnp.log(1.0 + jnp.exp(-jnp.abs(x)))


def _shift_rows(x, d):
    n = x.shape[0]
    if d == 0:
        return x
    y = pltpu.roll(x, (-d) % n, axis=0)
    t = lax.broadcasted_iota(jnp.int32, x.shape, 0)
    ok = (t + d >= 0) & (t + d < n)
    return jnp.where(ok, y, 0.0)


def _ada_kernel(c_ref, w_ref, b_ref, o_ref):
    s = _silu(c_ref[...])
    o_ref[0] = _dot(s.astype(BF16), w_ref[0].astype(BF16)) + b_ref[0]


def _ada(cvec, w_ada, b_ada):
    nl = w_ada.shape[0]
    tn = 1536
    return pl.pallas_call(
        _ada_kernel,
        grid=(nl, 6 * D // tn),
        in_specs=[
            pl.BlockSpec((16, D), lambda l, j: (0, 0)),
            pl.BlockSpec((1, D, tn), lambda l, j: (l, 0, j)),
            pl.BlockSpec((1, 1, tn), lambda l, j: (l, 0, j)),
        ],
        out_specs=pl.BlockSpec((1, 16, tn), lambda l, j: (l, 0, j)),
        out_shape=jax.ShapeDtypeStruct((nl, 16, 6 * D), F32),
        compiler_params=_cp("arbitrary", "arbitrary"),
        name="adaln",
    )(cvec, w_ada, b_ada.reshape(nl, 1, 6 * D))


def _modulated_norm(x, g, shift, scale):
    ms = jnp.mean(x * x, axis=-1, keepdims=True)
    y = x * lax.rsqrt(ms + EPS) * g
    return y * (1.0 + scale) + shift


def _inproj_kernel(x_ref, *refs):
    _inproj_tile(x_ref[...], *refs)


def _inproj_tile(x, mod_ref, g_ref, why_ref, wwa_ref, wna_ref, wgd_ref, wab_ref,
                 cos_ref, sa_ref, sb_ref, ohy, owa, ona, ogd, oab):
    m = mod_ref[0]
    h = _modulated_norm(x, g_ref[...], m[0:1], m[1:2]).astype(BF16)
    ohy[...] = _dot(h, why_ref[...]).astype(ohy.dtype)
    ona[...] = _dot(h, wna_ref[...]).astype(ona.dtype)
    ogd[...] = _dot(h, wgd_ref[...]).astype(ogd.dtype)
    oab[...] = _dot(h, wab_ref[...])
    a = _dot(h, wwa_ref[...])
    for c in range(4):
        sl = slice(c * 128, (c + 1) * 128)
        ac = a[:, sl]
        r = (ac * cos_ref[:, sl] + pltpu.roll(ac, 112, axis=1) * sa_ref[:, sl]
             + pltpu.roll(ac, 16, axis=1) * sb_ref[:, sl])
        owa[:, sl] = r.astype(owa.dtype)


def _inproj_specs(l, tile_of):
    row = lambda i: (tile_of(i), 0)
    const = lambda i: (0, 0)
    tab = lambda i: (jnp.where(tile_of(i) < NT_LAT, tile_of(i) % (S // TM), S // TM), 0)
    in_specs = [
        pl.BlockSpec((1, 6, D), lambda i: (l * 16 + (tile_of(i) * TM) // S, 0, 0)),
        pl.BlockSpec((1, D), const),
        pl.BlockSpec((D, 768), const),
        pl.BlockSpec((D, 512), const),
        pl.BlockSpec((D, 768), const),
        pl.BlockSpec((D, 1024), const),
        pl.BlockSpec((D, 128), const),
        pl.BlockSpec((TM, 512), tab),
        pl.BlockSpec((TM, 512), tab),
        pl.BlockSpec((TM, 512), tab),
    ]
    widths = (768, 512, 768, 1024, 128)
    out_specs = [pl.BlockSpec((TM, w), row) for w in widths]
    out_shape = [jax.ShapeDtypeStruct((ROWS, w), F32 if w == 128 else BF16) for w in widths]
    return in_specs, out_specs, out_shape


def _inproj(xa, mod, l, g, ws, tabs):
    in_specs, out_specs, out_shape = _inproj_specs(l, lambda i: i)
    return pl.pallas_call(
        _inproj_kernel,
        grid=(NT_ALL,),
        in_specs=[pl.BlockSpec((TM, D), lambda i: (i, 0))] + in_specs,
        out_specs=out_specs,
        out_shape=out_shape,
        compiler_params=_cp("arbitrary"),
        name="inproj",
    )(xa, mod, g, *ws, *tabs)


def _rope_tables():
    quarter = HD // 4
    pos = jnp.arange(S)
    inv = 10000.0 ** (-jnp.arange(quarter, dtype=F32) / quarter)
    ang_r = (pos // GW).astype(F32)[:, None] * inv[None, :]
    ang_c = (pos % GW).astype(F32)[:, None] * inv[None, :]
    z = jnp.zeros_like(ang_r)
    cos_h = jnp.concatenate([jnp.cos(ang_r)] * 2 + [jnp.cos(ang_c)] * 2, axis=1)
    sa_h = jnp.concatenate([-jnp.sin(ang_r), z, -jnp.sin(ang_c), z], axis=1)
    sb_h = jnp.concatenate([z, jnp.sin(ang_r), z, jnp.sin(ang_c)], axis=1)
    ones = jnp.ones((S, 128), F32)
    zeros = jnp.zeros((S, 128), F32)
    cos_t = jnp.concatenate([jnp.tile(cos_h, (1, 6)), ones], axis=1)
    sa_t = jnp.concatenate([jnp.tile(sa_h, (1, 6)), zeros], axis=1)
    sb_t = jnp.concatenate([jnp.tile(sb_h, (1, 6)), zeros], axis=1)
    ident = jnp.ones((TM, 512), F32)
    none = jnp.zeros((TM, 512), F32)
    return (jnp.concatenate([cos_t, ident], axis=0), jnp.concatenate([sa_t, none], axis=0),
            jnp.concatenate([sb_t, none], axis=0))


def _dft_tables(length):
    n = 2 * length
    f = jnp.arange(length, dtype=jnp.int32)
    m = ((2 * f[:, None] + 1) * (2 * f[None, :] + 1)) % (4 * n)
    th = m.astype(F32) * (2.0 * math.pi / (4 * n))
    phi = (2 * f + 1).astype(F32) * (math.pi / (2 * n))
    return (jnp.cos(th).astype(BF16), jnp.sin(th).astype(BF16),
            jnp.cos(phi)[:, None], jnp.sin(phi)[:, None])


def _hyena_features(length):
    t = jnp.arange(length, dtype=F32)
    t_norm = t / max(length - 1, 1)
    bands = jnp.linspace(1e-4, HY_BANDS - 1, HY_BANDS, dtype=F32)
    ang = (2.0 * math.pi / length) * t[:, None] * bands[None, :]
    z = jnp.concatenate([t_norm[:, None], jnp.cos(ang), -jnp.sin(ang)], axis=-1)
    z = jnp.pad(z, ((0, 0), (0, 128 - z.shape[1])))
    decay = jnp.tile(jnp.linspace(HY_DECAY_MIN, HY_DECAY_MAX, HY_CH, dtype=F32), 2)
    return z, jnp.exp(-t_norm[:, None] * decay[None, :])


def _hyfilt_kernel(z_ref, w1_ref, b1_ref, w2_ref, b2_ref, w3_ref, fr_ref, dec_ref,
                   c_ref, s_ref, cp_ref, sp_ref, kre_ref, kim_ref, pq_ref, *, length):
    @pl.when(pl.program_id(0) == 0)
    def _():
        fr = fr_ref[...]
        h = jnp.sin(fr * (_dot(z_ref[...], w1_ref[...], HI) + b1_ref[...]))
        h = jnp.sin(fr * (_dot(h, w2_ref[...], HI) + b2_ref[...]))
        h = _dot(h, w3_ref[...], HI) * dec_ref[...]
        hf = h[:, :HY_CH]
        t = lax.broadcasted_iota(jnp.int32, (length, HY_CH), 0)
        hb = jnp.where(t == 0, 0.0, h[:, HY_CH:])
        pq_ref[:, :HY_CH] = (hf + hb).astype(BF16)
        pq_ref[:, HY_CH:] = (hb - hf).astype(BF16)

    pq = pq_ref[...]
    cpq = _dot(c_ref[...], pq)
    spq = _dot(s_ref[...], pq)
    cphi, sphi = cp_ref[...], sp_ref[...]
    norm = 1.0 / length
    kre_ref[...] = (cphi * cpq[:, :HY_CH] + sphi * spq[:, :HY_CH]) * norm
    kim_ref[...] = (cphi * spq[:, HY_CH:] - sphi * cpq[:, HY_CH:]) * norm


def _hyena_filter_spectrum(length, w1, b1, w2, b2, w3, freq, dft):
    z, dec = _hyena_features(length)
    c_m, s_m, cphi, sphi = dft
    w1p = jnp.pad(w1, ((0, 128 - w1.shape[0]), (0, 0)))
    tf = min(512, length)
    const = lambda j: (0, 0)
    blk = lambda j: (j, 0)
    return pl.pallas_call(
        functools.partial(_hyfilt_kernel, length=length),
        grid=(length // tf,),
        in_specs=[
            pl.BlockSpec((length, 128), const),
            pl.BlockSpec((128, 64), const),
            pl.BlockSpec((1, 64), const),
            pl.BlockSpec((64, 64), const),
            pl.BlockSpec((1, 64), const),
            pl.BlockSpec((64, 2 * HY_CH), const),
            pl.BlockSpec((1, 64), const),
            pl.BlockSpec((length, 2 * HY_CH), const),
            pl.BlockSpec((tf, length), blk),
            pl.BlockSpec((tf, length), blk),
            pl.BlockSpec((tf, 1), blk),
            pl.BlockSpec((tf, 1), blk),
        ],
        out_specs=[pl.BlockSpec((tf, HY_CH), blk)] * 2,
        out_shape=[jax.ShapeDtypeStruct((length, HY_CH), F32)] * 2,
        scratch_shapes=[pltpu.VMEM((length, 2 * HY_CH), BF16)],
        compiler_params=_cp("arbitrary"),
        name=f"hyena_filter_{length}",
    )(z, w1p, b1[None, :], w2, b2[None, :], w3, freq[None, :], dec, c_m, s_m, cphi, sphi)


def _short_conv3(x_ref, cw):
    x = x_ref[...].astype(F32)
    return _shift_rows(x, -1) * cw[0:1] + x * cw[1:2] + _shift_rows(x, 1) * cw[2:3]


def _hyena_kernel(p0_ref, p1_ref, p2_ref, cw_ref, bias_ref, c_ref, s_ref, kre_ref, kim_ref, o_ref):
    cw = cw_ref[...]
    x0 = _short_conv3(p0_ref, cw[:, :HY_CH])
    k = _short_conv3(p1_ref, cw[:, HY_CH:2 * HY_CH]) * _short_conv3(p2_ref, cw[:, 2 * HY_CH:])
    kb = k.astype(BF16)
    a = _dot(c_ref[...], kb)
    b = _dot(s_ref[...], kb)
    kre, kim = kre_ref[...], kim_ref[...]
    yre = (a * kre + b * kim).astype(BF16)
    yim = (b * kre - a * kim).astype(BF16)
    y = _dot(c_ref[...], yre) + _dot(s_ref[...], yim)
    o_ref[...] = (x0 * (y + k * bias_ref[...])).astype(o_ref.dtype)


def _hyena(p_hy, length, blk0, conv_w, bias, dft, kre, kim):
    c_m, s_m = dft[0], dft[1]
    const = lambda b: (0, 0)
    once = pl.Buffered(1)
    in_specs = [
        pl.BlockSpec((length, HY_CH), lambda b: (blk0 + b, 0)),
        pl.BlockSpec((length, HY_CH), lambda b: (blk0 + b, 1)),
        pl.BlockSpec((length, HY_CH), lambda b: (blk0 + b, 2)),
        pl.BlockSpec((3, 768), const),
        pl.BlockSpec((1, HY_CH), const),
        pl.BlockSpec((length, length), const, pipeline_mode=once),
        pl.BlockSpec((length, length), const, pipeline_mode=once),
        pl.BlockSpec((length, HY_CH), const, pipeline_mode=once),
        pl.BlockSpec((length, HY_CH), const, pipeline_mode=once),
    ]
    return pl.pallas_call(
        _hyena_kernel,
        grid=(B,),
        in_specs=in_specs,
        out_specs=pl.BlockSpec((length, HY_CH), lambda b: (b, 0)),
        out_shape=jax.ShapeDtypeStruct((B * length, HY_CH), BF16),
        compiler_params=_cp("arbitrary"),
        name=f"hyena_{length}",
    )(p_hy, p_hy, p_hy, conv_w, bias[None, :], c_m, s_m, kre, kim)


def _ones_half(vwin, half):
    lane = lax.broadcasted_iota(jnp.int32, vwin.shape, 1)
    keep = (lane < HD) if half == 0 else (lane >= HD)
    return jnp.where(keep, vwin, jnp.ones_like(vwin))


def _softmax_pv(heads):
    probs, maxes = [], []
    for parts, sink, _ in heads:
        m = None
        for s, _ in parts:
            mi = jnp.max(s, axis=-1, keepdims=True)
            m = mi if m is None else jnp.maximum(m, mi)
        if sink is not None:
            m = jnp.maximum(m, sink)
        probs.append([jnp.exp((s - m).astype(BF16)) for s, _ in parts])
        maxes.append(m)
    outs = []
    for (parts, sink, half), es, m in zip(heads, probs, maxes):
        acc = None
        for e, (_, v) in zip(es, parts):
            o = _dot(e, v)
            acc = o if acc is None else acc + o
        den = acc[:, HD:HD + 1] if half == 0 else acc[:, 0:1]
        if sink is not None:
            den = den + jnp.exp(sink - m)
        outs.append(acc / den)
    return outs


def _wa_kernel(sink_ref, q_ref, k_ref, v_ref, kc_ref, vc_ref, o_ref):
    kc, vc = kc_ref[...], vc_ref[...]
    vcs = [_ones_half(vc, hk) for hk in range(2)]
    rr = lax.broadcasted_iota(jnp.int32, (256, 384), 0)
    r1 = lax.broadcasted_iota(jnp.int32, (256, 1), 0)
    heads = []
    for j in range(WA_BLKS):
        n = pl.program_id(1) * WA_BLKS + j
        start = pl.multiple_of(jnp.clip((n - 1) * 128, 0, S - 384), 128)
        kw = k_ref[pl.ds(start, 384), :]
        vw = v_ref[pl.ds(start, 384), :]
        q = q_ref[j * 128:(j + 1) * 128, :] * SCALE
        qpos = n * 128 + jnp.where(rr >= 128, rr - 128, rr)
        kpos = start + lax.broadcasted_iota(jnp.int32, (256, 384), 1)
        valid = jnp.abs(qpos - kpos) <= WINDOW
        for hk in range(2):
            q2 = jnp.concatenate([q[:, (2 * hk) * HD:(2 * hk + 1) * HD],
                                  q[:, (2 * hk + 1) * HD:(2 * hk + 2) * HD]], axis=0)
            hs = slice(hk * HD, (hk + 1) * HD)
            s_loc = jnp.where(valid, _dot_nt(q2, kw[:, hs]), NEG)
            s_ctx = _dot_nt(q2, kc[:, hs])
            sink = jnp.where(r1 >= 128, sink_ref[2 * hk + 1], sink_ref[2 * hk])
            heads.append(([(s_ctx, vcs[hk]), (s_loc, _ones_half(vw, hk))], sink, hk))
    for i, res in enumerate(_softmax_pv(heads)):
        j, hk = divmod(i, 2)
        o = res[:, hk * HD:(hk + 1) * HD]
        rows = slice(j * 128, (j + 1) * 128)
        o_ref[rows, (2 * hk) * HD:(2 * hk + 1) * HD] = o[:128].astype(o_ref.dtype)
        o_ref[rows, (2 * hk + 1) * HD:(2 * hk + 2) * HD] = o[128:].astype(o_ref.dtype)


def _window_attention(p_wa, sink):
    nb = S // (128 * WA_BLKS)
    return pl.pallas_call(
        _wa_kernel,
        grid=(B, nb),
        in_specs=[
            pl.BlockSpec(memory_space=pltpu.SMEM),
            pl.BlockSpec((128 * WA_BLKS, 256), lambda b, n: (b * nb + n, 0)),
            pl.BlockSpec((S, 128), lambda b, n: (b, 2)),
            pl.BlockSpec((S, 128), lambda b, n: (b, 3)),
            pl.BlockSpec((LC, 128), lambda b, n: (CTX_BLK0 + b, 2)),
            pl.BlockSpec((LC, 128), lambda b, n: (CTX_BLK0 + b, 3)),
        ],
        out_specs=pl.BlockSpec((128 * WA_BLKS, 256), lambda b, n: (b * nb + n, 0)),
        out_shape=jax.ShapeDtypeStruct((NL, 256), BF16),
        compiler_params=_cp("arbitrary", "arbitrary"),
        name="window_attn",
    )(sink, p_wa, p_wa, p_wa, p_wa, p_wa)


def _na_kernel(q_ref, k_ref, v_ref, kc_ref, vc_ref, bias_ref, o_ref):
    kc, vc = kc_ref[...], vc_ref[...]
    vcs = [_ones_half(vc[:, (h // 2) * 128:(h // 2 + 1) * 128], h % 2) for h in range(4)]
    heads = []
    for j in range(NA_ROWS):
        r = pl.program_id(1) * NA_ROWS + j
        first = jnp.clip(r - NA_KR // 2, 0, S // GW - NA_KR)
        start = pl.multiple_of(first * GW, GW)
        off = first - r + NA_KR - 1
        kw = k_ref[pl.ds(start, NA_KR * GW), :]
        vw = v_ref[pl.ds(start, NA_KR * GW), :]
        q = q_ref[j * GW:(j + 1) * GW, :] * SCALE
        for h in range(4):
            hs = slice(h * HD, (h + 1) * HD)
            win = slice((h // 2) * 128, (h // 2 + 1) * 128)
            s_loc = _dot_nt(q[:, hs], kw[:, hs]) + bias_ref[h, off]
            s_ctx = _dot_nt(q[:, hs], kc[:, hs])
            heads.append(([(s_ctx, vcs[h]), (s_loc, _ones_half(vw[:, win], h % 2))], None, h % 2))
    outs = _softmax_pv(heads)
    lane = lax.broadcasted_iota(jnp.int32, (GW, 128), 1)
    for j in range(NA_ROWS):
        for hp in range(2):
            pair = jnp.where(lane < HD, outs[4 * j + 2 * hp], outs[4 * j + 2 * hp + 1])
            o_ref[j * GW:(j + 1) * GW, hp * 128:(hp + 1) * 128] = pair.astype(o_ref.dtype)


def _na_bias_kernel(rpb_ref, sel_ref, o_ref):
    o_ref[0] = _dot(rpb_ref[0], sel_ref[...], HI)


def _na_bias_table(rpb):
    qc = jnp.arange(GW)[:, None]
    kcol = jnp.arange(GW)[None, :]
    ci = jnp.clip(kcol - qc + NA_KC - 1, 0, 2 * NA_KC - 2).reshape(1, GW * GW)
    sel = (jnp.arange(128)[:, None] == ci).astype(F32)
    n_ri = 2 * NA_KR - 1
    rpb_p = jnp.pad(rpb.astype(F32), ((0, 0), (0, 16 - n_ri), (0, 128 - (2 * NA_KC - 1))))
    per_row = pl.pallas_call(
        _na_bias_kernel,
        grid=(4,),
        in_specs=[pl.BlockSpec((1, 16, 128), lambda h: (h, 0, 0)), pl.BlockSpec((128, GW * GW), lambda h: (0, 0))],
        out_specs=pl.BlockSpec((1, 16, GW * GW), lambda h: (h, 0, 0)),
        out_shape=jax.ShapeDtypeStruct((4, 16, GW * GW), F32),
        compiler_params=_cp("arbitrary"),
        name="na_bias",
    )(rpb_p, sel).reshape(4, 16, GW, GW)
    tbl = jnp.stack([per_row[:, o:o + NA_KR] for o in range(NA_KR)], axis=1)
    tbl = jnp.transpose(tbl, (0, 1, 3, 2, 4))
    c_start = jnp.clip(qc - NA_KC // 2, 0, GW - NA_KC)
    ok = (kcol >= c_start) & (kcol < c_start + NA_KC)
    tbl = jnp.where(ok[None, None, :, None, :], tbl, NEG)
    return tbl.reshape(4, NA_KR, GW, NA_KR * GW)


def _neighborhood_attention(p_na, bias_tbl):
    steps = S // GW // NA_ROWS
    return pl.pallas_call(
        _na_kernel,
        grid=(B, steps),
        in_specs=[
            pl.BlockSpec((NA_ROWS * GW, 256), lambda b, r: (b * steps + r, 0)),
            pl.BlockSpec((S, 256), lambda b, r: (b, 1)),
            pl.BlockSpec((S, 256), lambda b, r: (b, 2)),
            pl.BlockSpec((LC, 256), lambda b, r: (CTX_BLK0 + b, 1)),
            pl.BlockSpec((LC, 256), lambda b, r: (CTX_BLK0 + b, 2)),
            pl.BlockSpec((4, NA_KR, GW, NA_KR * GW), lambda b, r: (0, 0, 0, 0)),
        ],
        out_specs=pl.BlockSpec((NA_ROWS * GW, 256), lambda b, r: (b * steps + r, 0)),
        out_shape=jax.ShapeDtypeStruct((NL, 256), BF16),
        compiler_params=_cp("arbitrary", "arbitrary"),
        name="neighborhood_attn",
    )(p_na, p_na, p_na, p_na, p_na, bias_tbl)


def _ctx_attn_kernel(sink_ref, qw_ref, kw_ref, vw_ref, qn_ref, kn_ref, vn_ref, owa_ref, ona_ref):
    q, k, v = qw_ref[...] * SCALE, kw_ref[...], vw_ref[...]
    r1 = lax.broadcasted_iota(jnp.int32, (2 * LC, 1), 0)
    heads = []
    for hk in range(2):
        q2 = jnp.concatenate([q[:, (2 * hk) * HD:(2 * hk + 1) * HD],
                              q[:, (2 * hk + 1) * HD:(2 * hk + 2) * HD]], axis=0)
        hs = slice(hk * HD, (hk + 1) * HD)
        sink = jnp.where(r1 >= LC, sink_ref[2 * hk + 1], sink_ref[2 * hk])
        heads.append(([(_dot_nt(q2, k[:, hs]), _ones_half(v, hk))], sink, hk))
    for hk, res in enumerate(_softmax_pv(heads)):
        o = res[:, hk * HD:(hk + 1) * HD]
        owa_ref[:, (2 * hk) * HD:(2 * hk + 1) * HD] = o[:LC].astype(owa_ref.dtype)
        owa_ref[:, (2 * hk + 1) * HD:(2 * hk + 2) * HD] = o[LC:].astype(owa_ref.dtype)
    q, k, v = qn_ref[...] * SCALE, kn_ref[...], vn_ref[...]
    heads = []
    for h in range(4):
        hs = slice(h * HD, (h + 1) * HD)
        win = slice((h // 2) * 128, (h // 2 + 1) * 128)
        heads.append(([(_dot_nt(q[:, hs], k[:, hs]), _ones_half(v[:, win], h % 2))], None, h % 2))
    outs = _softmax_pv(heads)
    lane = lax.broadcasted_iota(jnp.int32, (LC, 128), 1)
    for hp in range(2):
        pair = jnp.where(lane < HD, outs[2 * hp], outs[2 * hp + 1])
        ona_ref[:, hp * 128:(hp + 1) * 128] = pair.astype(ona_ref.dtype)


def _ctx_attention(p_wa, p_na, sink):
    blk = lambda c: (lambda b: (CTX_BLK0 + b, c))
    return pl.pallas_call(
        _ctx_attn_kernel,
        grid=(B,),
        in_specs=[
            pl.BlockSpec(memory_space=pltpu.SMEM),
            pl.BlockSpec((LC, 256), blk(0)),
            pl.BlockSpec((LC, 128), blk(2)),
            pl.BlockSpec((LC, 128), blk(3)),
            pl.BlockSpec((LC, 256), blk(0)),
            pl.BlockSpec((LC, 256), blk(1)),
            pl.BlockSpec((LC, 256), blk(2)),
        ],
        out_specs=[pl.BlockSpec((LC, 256), lambda b: (b, 0))] * 2,
        out_shape=[jax.ShapeDtypeStruct((NC, 256), BF16)] * 2,
        compiler_params=_cp("arbitrary"),
        name="ctx_attn",
    )(sink, p_wa, p_wa, p_wa, p_na, p_na, p_na)


def _gdn_prep_seq(p_ref, ab_ref, cw, alog, dtb, qkv_ref, gb_ref, row0, length):
    x = p_ref[:, :768].astype(F32)
    u = (_shift_rows(x, -2) * cw[0:1] + _shift_rows(x, -1) * cw[1:2] + x * cw[2:3]
         + _shift_rows(x, 1) * cw[3:4])
    u = _silu(u)
    rows = slice(row0, row0 + length)
    for j in range(8):
        sl = slice(j * HD, (j + 1) * HD)
        xs = u[:, sl]
        nrm = lax.rsqrt(jnp.sum(xs * xs, axis=-1, keepdims=True) + EPS)
        qkv_ref[0, rows, sl] = xs * (nrm * SCALE if j < 4 else nrm)
    qkv_ref[0, rows, 512:768] = u[:, 512:768]
    ab = ab_ref[...]
    g = -jnp.exp(alog) * _softplus(ab + dtb)
    lane = lax.broadcasted_iota(jnp.int32, ab.shape, 1)
    gb_ref[0, rows, :] = jnp.where(lane < 8, g, _sigmoid(ab))


def _gdn_prep_kernel(pl_ref, abl_ref, pc_ref, abc_ref, cw_ref, alog_ref, dtb_ref, qkv_ref, gb_ref):
    cw, alog, dtb = cw_ref[...], alog_ref[...], dtb_ref[...]
    _gdn_prep_seq(pl_ref, abl_ref, cw, alog, dtb, qkv_ref, gb_ref, 0, S)
    _gdn_prep_seq(pc_ref, abc_ref, cw, alog, dtb, qkv_ref, gb_ref, S, LC)


def _gdn_prep(p_gd, p_ab, conv_w, a_log, dt_bias):
    const = lambda b: (0, 0)
    alog = jnp.pad(a_log.reshape(1, 8), ((0, 0), (0, 120)))
    dtb = jnp.pad(dt_bias.reshape(1, 8), ((0, 0), (0, 120)))
    return pl.pallas_call(
        _gdn_prep_kernel,
        grid=(B,),
        in_specs=[
            pl.BlockSpec((S, 1024), lambda b: (b, 0)),
            pl.BlockSpec((S, 128), lambda b: (b, 0)),
            pl.BlockSpec((LC, 1024), lambda b: (CTX_BLK0 + b, 0)),
            pl.BlockSpec((LC, 128), lambda b: (CTX_BLK0 + b, 0)),
            pl.BlockSpec((4, 768), const),
            pl.BlockSpec((1, 128), const),
            pl.BlockSpec((1, 128), const),
        ],
        out_specs=[pl.BlockSpec((1, S + LC, 768), lambda b: (b, 0, 0)),
                   pl.BlockSpec((1, S + LC, 128), lambda b: (b, 0, 0))],
        out_shape=[jax.ShapeDtypeStruct((B, S + LC, 768), F32), jax.ShapeDtypeStruct((B, S + LC, 128), F32)],
        compiler_params=_cp("arbitrary"),
        name="gdn_prep",
    )(p_gd, p_ab, p_gd, p_ab, conv_w, alog, dtb)


def _gdn_chains(xs, gbs, cums, cum_ts, states, masks):
    chains = [(g, h) for g in range(len(xs)) for h in range(4)]
    pre = []
    for g, h in chains:
        dirn = g % 2
        x, gb = xs[g], gbs[g]
        tri, strict = masks[dirn]
        col = dirn * 4 + h
        qh = x[:, h * HD:(h + 1) * HD]
        kh = x[:, 256 + h * HD:256 + (h + 1) * HD]
        vh = x[:, 512 + h * HD:512 + (h + 1) * HD]
        gc = cums[g][:, col:col + 1]
        gc_row = cum_ts[g][col:col + 1, :]
        beta = gb[:, 8 + col:9 + col]
        decay = jnp.exp(jnp.where(tri, gc - gc_row, NEG))
        kb = kh * beta
        eg = jnp.exp(gc)
        g_last = gc[0:1, :] if dirn else gc[CHUNK - 1:CHUNK, :]
        pre.append(dict(strict=strict, decay=decay, kbf=kb.astype(BF16), khf=kh.astype(BF16), qf=qh.astype(BF16),
                        xw=jnp.concatenate([vh * beta, kb * eg], axis=1), qdec=(qh * eg).astype(BF16),
                        kdec=(kh * jnp.exp(g_last - gc)).astype(BF16), gl=jnp.exp(g_last)))
    n = len(chains)
    gram = [_dot_nt(p["kbf"], p["khf"]) for p in pre]
    attn = [_dot_nt(p["qf"], p["khf"]) for p in pre]
    lmat = [jnp.where(p["strict"], g * p["decay"], 0.0) for p, g in zip(pre, gram)]
    attn = [(a * p["decay"]).astype(BF16) for p, a in zip(pre, attn)]
    ii = lax.broadcasted_iota(jnp.int32, (CHUNK, CHUNK), 0)
    jj = lax.broadcasted_iota(jnp.int32, (CHUNK, CHUNK), 1)

    def merged_off_blocks(log_s):
        return ((ii >> (log_s + 1)) == (jj >> (log_s + 1))) & ((ii >> log_s) != (jj >> log_s))

    eye = (ii == jj).astype(F32)
    dinv = [eye - jnp.where(merged_off_blocks(0), lm, 0.0) for lm in lmat]
    for log_s in range(1, 6):
        m = merged_off_blocks(log_s)
        dinv_b = [d.astype(BF16) for d in dinv]
        ld = [_dot(jnp.where(m, lmat[c], 0.0).astype(BF16), dinv_b[c]) for c in range(n)]
        upd = [_dot(dinv_b[c], ld[c].astype(BF16)) for c in range(n)]
        dinv = [dinv[c] - upd[c] for c in range(n)]
    xw = [_dot(dinv[c].astype(BF16), pre[c]["xw"].astype(BF16)) for c in range(n)]
    sts = [states[g][h] for g, h in chains]
    stb = [s.astype(BF16) for s in sts]
    ws = [_dot(xw[c][:, HD:].astype(BF16), stb[c]) for c in range(n)]
    qs = [_dot(pre[c]["qdec"], stb[c]) for c in range(n)]
    u_new = [(xw[c][:, :HD] - ws[c]).astype(BF16) for c in range(n)]
    au = [_dot(attn[c], u_new[c]) for c in range(n)]
    ku = [_dot_tn(pre[c]["kdec"], u_new[c]) for c in range(n)]
    outs = [qs[c] + au[c] for c in range(n)]
    new_states = [sts[c] * pre[c]["gl"] + ku[c] for c in range(n)]
    return outs, new_states


def _gdn_kernel(x0_ref, gb0_ref, x1_ref, gb1_ref, o0_ref, o1_ref, state_ref):
    @pl.when(pl.program_id(0) == 0)
    def _():
        state_ref[...] = jnp.zeros_like(state_ref)

    ii = lax.broadcasted_iota(jnp.int32, (CHUNK, CHUNK), 0)
    jj = lax.broadcasted_iota(jnp.int32, (CHUNK, CHUNK), 1)
    masks = ((jj <= ii, jj < ii), (jj >= ii, jj > ii))
    ones_tri = ((jj <= ii).astype(F32), (jj >= ii).astype(F32))

    def body(bp, carry):
        bs = [bp * GDN_SEQS + i for i in range(GDN_SEQS)]
        states = [[state_ref[b, dirn, h] for h in range(4)] for b in bs for dirn in range(2)]
        xs = [r[b] for b in bs for r in (x0_ref, x1_ref)]
        gbs = [r[b] for b in bs for r in (gb0_ref, gb1_ref)]
        cums = [_dot(ones_tri[g % 2], gbs[g], HI) for g in range(2 * GDN_SEQS)]
        cum_ts = [cm.T for cm in cums]
        outs, new_states = _gdn_chains(xs, gbs, cums, cum_ts, states, masks)
        for c, st in enumerate(new_states):
            state_ref[bs[c // 8], (c // 4) % 2, c % 4] = st
        for i, b in enumerate(bs):
            o0_ref[b] = jnp.concatenate(outs[8 * i:8 * i + 4], axis=1)
            o1_ref[b] = jnp.concatenate(outs[8 * i + 4:8 * i + 8], axis=1)
        return carry

    lax.fori_loop(0, B // GDN_SEQS, body, 0)


def _gdn_scan(qkv, gb):
    nc = (S + LC) // CHUNK
    fwd = lambda c: (0, (c + S // CHUNK) % nc, 0)
    bwd = lambda c: (0, nc - 1 - c, 0)
    return pl.pallas_call(
        _gdn_kernel,
        grid=(nc,),
        in_specs=[
            pl.BlockSpec((B, CHUNK, 768), fwd),
            pl.BlockSpec((B, CHUNK, 128), fwd),
            pl.BlockSpec((B, CHUNK, 768), bwd),
            pl.BlockSpec((B, CHUNK, 128), bwd),
        ],
        out_specs=[pl.BlockSpec((B, CHUNK, 256), fwd), pl.BlockSpec((B, CHUNK, 256), bwd)],
        out_shape=[jax.ShapeDtypeStruct((B, S + LC, 256), F32)] * 2,
        scratch_shapes=[pltpu.VMEM((B, 2, 4, HD, HD), F32)],
        compiler_params=_cp("arbitrary"),
        name="gdn_scan",
    )(qkv, gb, qkv, gb)


def _gdn_finish_kernel(o0_ref, o1_ref, gate_ref, ng_ref, y_ref):
    o = o0_ref[0] + o1_ref[0]
    gi = lax.broadcasted_iota(jnp.int32, (256, 256), 0) // HD
    gj = lax.broadcasted_iota(jnp.int32, (256, 256), 1) // HD
    group_mean = jnp.where(gi == gj, 1.0 / HD, 0.0)
    ms = _dot(o * o, group_mean, HI)
    y = o * lax.rsqrt(ms + EPS) * ng_ref[...]
    y_ref[...] = (y * _silu(gate_ref[...].astype(F32))).astype(y_ref.dtype)


def _gdn_finish(o0, o1, p_gd, norm_g, need_ctx):
    lat_tiles = S // TM
    row_blk = lambda b, j: jnp.where(j < lat_tiles, b * lat_tiles + j, CTX_BLK0 + b)
    return pl.pallas_call(
        _gdn_finish_kernel,
        grid=(B, lat_tiles + (1 if need_ctx else 0)),
        in_specs=[
            pl.BlockSpec((1, TM, 256), lambda b, j: (b, j, 0)),
            pl.BlockSpec((1, TM, 256), lambda b, j: (b, j, 0)),
            pl.BlockSpec((TM, 256), lambda b, j: (row_blk(b, j), 3)),
            pl.BlockSpec((1, 256), lambda b, j: (0, 0)),
        ],
        out_specs=pl.BlockSpec((TM, 256), lambda b, j: (row_blk(b, j), 0)),
        out_shape=jax.ShapeDtypeStruct((ROWS if need_ctx else NL, 256), BF16),
        compiler_params=_cp("arbitrary", "arbitrary"),
        name="gdn_finish",
    )(o0, o1, p_gd, jnp.tile(norm_g, 4)[None, :])


def _pack_bf16_pairs(x):
    n = x.shape[1] // 2
    lo = pltpu.bitcast(x[:, :n].astype(BF16).astype(F32), jnp.uint32)
    hi = pltpu.bitcast(x[:, n:].astype(BF16).astype(F32), jnp.uint32)
    return hi | (lo >> 16)


def _unpack_bf16_pairs(w):
    lo = pltpu.bitcast(w << 16, F32)
    hi = pltpu.bitcast(w & jnp.uint32(0xFFFF0000), F32)
    return jnp.concatenate([lo, hi], axis=1)


def _outproj_kernel(x_ref, mod_ref, g2_ref, yh_ref, yw_ref, yn_ref, yg_ref, wo_ref, wr_ref, eb_ref,
                    xo_ref, h2_ref, h2p_ref, idx_ref, tw_ref, rank_ref, cnt_ref, carry_ref):
    @pl.when(pl.program_id(0) == 0)
    def _():
        carry_ref[...] = jnp.zeros_like(carry_ref)

    m = mod_ref[0]
    acc = (_dot(yh_ref[...], wo_ref[0:256, :]) + _dot(yw_ref[...], wo_ref[256:512, :])
           + _dot(yn_ref[...], wo_ref[512:768, :]) + _dot(yg_ref[...], wo_ref[768:1024, :]))
    x = x_ref[...] + m[2:3] * acc
    xo_ref[...] = x
    h2 = _modulated_norm(x, g2_ref[...], m[3:4], m[4:5])
    h2_ref[...] = h2.astype(h2_ref.dtype)
    h2p_ref[...] = _pack_bf16_pairs(h2)
    scores = _sigmoid(_dot(h2, wr_ref[...], HI))
    lane = lax.broadcasted_iota(jnp.int32, scores.shape, 1)
    sel = jnp.where(lane < N_EXP, scores + eb_ref[...], -jnp.inf)
    lane_f = lane.astype(F32)
    idx_out = jnp.zeros(scores.shape, F32)
    s_out = jnp.zeros(scores.shape, F32)
    hits = []
    for kk in range(TOP_K):
        mx = jnp.max(sel, axis=-1, keepdims=True)
        idx = jnp.min(jnp.where(sel == mx, lane_f, 128.0), axis=-1, keepdims=True)
        hit = lane_f == idx
        sk = jnp.sum(jnp.where(hit, scores, 0.0), axis=-1, keepdims=True)
        sel = jnp.where(hit, -jnp.inf, sel)
        idx_out = jnp.where(lane == kk, idx, idx_out)
        s_out = jnp.where(lane == kk, sk, s_out)
        hits.append(hit)
    tot = jnp.sum(s_out, axis=-1, keepdims=True)
    tw_ref[...] = s_out / tot * ROUTED_SCALE
    cnt = jnp.zeros(scores.shape, F32)
    for hit in hits:
        cnt = cnt + jnp.where(hit, 1.0, 0.0)
    ri = lax.broadcasted_iota(jnp.int32, (TM, TM), 0)
    rj = lax.broadcasted_iota(jnp.int32, (TM, TM), 1)
    before = _dot(jnp.where(rj < ri, 1.0, 0.0).astype(BF16), cnt.astype(BF16)) + carry_ref[...]
    rank_out = jnp.zeros(scores.shape, F32)
    for kk, hit in enumerate(hits):
        rk = jnp.sum(jnp.where(hit, before, 0.0), axis=-1, keepdims=True)
        rank_out = jnp.where(lane == kk, rk, rank_out)
    idx_ref[...] = idx_out.T[:8].astype(jnp.int32)
    rank_ref[...] = rank_out.T[:8].astype(jnp.int32)
    carry_ref[...] = carry_ref[...] + jnp.sum(cnt, axis=0, keepdims=True)
    cnt_ref[...] = carry_ref[...].astype(jnp.int32)


def _outproj(xa, mod, l, g2, ys, wo, wr, eb, n_tiles):
    row = lambda i: (i, 0)
    const = lambda i: (0, 0)
    rows = n_tiles * TM
    return pl.pallas_call(
        _outproj_kernel,
        grid=(n_tiles,),
        in_specs=[
            pl.BlockSpec((TM, D), row),
            pl.BlockSpec((1, 6, D), lambda i: (l * 16 + (i * TM) // S, 0, 0)),
            pl.BlockSpec((1, D), const),
            pl.BlockSpec((TM, 256), row),
            pl.BlockSpec((TM, 256), row),
            pl.BlockSpec((TM, 256), row),
            pl.BlockSpec((TM, 256), row),
            pl.BlockSpec((D, D), const),
            pl.BlockSpec((D, 128), const),
            pl.BlockSpec((1, 128), const),
        ],
        out_specs=[
            pl.BlockSpec((TM, D), row),
            pl.BlockSpec((TM, D), row),
            pl.BlockSpec((TM, D // 2), row),
            pl.BlockSpec((8, TM), row),
            pl.BlockSpec((TM, 128), row),
            pl.BlockSpec((8, TM), row),
            pl.BlockSpec((1, 128), const),
        ],
        out_shape=[
            jax.ShapeDtypeStruct((rows, D), F32),
            jax.ShapeDtypeStruct((rows, D), BF16),
            jax.ShapeDtypeStruct((rows, D // 2), jnp.uint32),
            jax.ShapeDtypeStruct((n_tiles * 8, TM), jnp.int32),
            jax.ShapeDtypeStruct((rows, 128), F32),
            jax.ShapeDtypeStruct((n_tiles * 8, TM), jnp.int32),
            jax.ShapeDtypeStruct((1, 128), jnp.int32),
        ],
        scratch_shapes=[pltpu.VMEM((1, 128), F32)],
        compiler_params=_cp("arbitrary"),
        name="outproj_router",
    )(xa, mod, g2, *ys, wo, wr, eb)


ASG_TILE = TM * TOP_K


def _segment_tables(counts, n_blocks):
    counts = counts[0, :N_EXP]
    padded = (counts + MOE_BLK - 1) // MOE_BLK * MOE_BLK
    pad_end = jnp.cumsum(padded)
    seg_start = pad_end - padded
    starts = jnp.arange(n_blocks, dtype=jnp.int32) * MOE_BLK
    blk_e = jnp.minimum(jnp.sum((pad_end[None, :] <= starts[:, None]).astype(jnp.int32), axis=1), N_EXP - 1)
    seg = jnp.concatenate([seg_start, seg_start + counts, pad_end[-1:]])
    blk = jnp.concatenate([blk_e, pad_end[-1:] // MOE_BLK])
    return seg.astype(jnp.int32), blk.astype(jnp.int32)


def _row_copy_wait(shape_ref, dst_ref, sem):
    pltpu.make_async_copy(shape_ref, dst_ref, sem).wait()


def _zero_fill_rows(zero_ref, xb_ref, start, length, sem):
    plan = []
    aligned = (start + 7) & -8
    for r in range(7):
        plan.append(((start + r < aligned) & (r < length),
                     pltpu.make_async_copy(zero_ref.at[pl.ds(0, 1)], xb_ref.at[pl.ds(start + r, 1)], sem)))
    rest = start + length - aligned
    off = aligned
    for bit in reversed(range(3, MOE_BLK.bit_length() - 1)):
        size = 1 << bit
        take = (rest & size) != 0
        dst = xb_ref.at[pl.ds(pl.multiple_of(off, 8), size)]
        plan.append((take, pltpu.make_async_copy(zero_ref.at[pl.ds(0, size)], dst, sem)))
        off = off + jnp.where(take, size, 0)
    for take, copy in plan:
        pl.when(take)(copy.start)
    for take, copy in plan:
        pl.when(take)(copy.wait)


def _dest_kernel(seg_ref, idx_ref, rank_ref, o_ref):
    idx = idx_ref[...]
    first = jnp.zeros(idx.shape, jnp.int32)
    for e in range(N_EXP):
        first = jnp.where(idx == e, seg_ref[e], first)
    o_ref[...] = first + rank_ref[...]


def _dest_rows(seg, top_idx, rank):
    n_tiles = top_idx.shape[0] // 8
    blk = lambda i, seg: (i, 0)
    grid_spec = pltpu.PrefetchScalarGridSpec(
        num_scalar_prefetch=1,
        grid=(n_tiles // 8,),
        in_specs=[pl.BlockSpec((64, TM), blk), pl.BlockSpec((64, TM), blk)],
        out_specs=pl.BlockSpec((64, TM), blk),
    )
    return pl.pallas_call(
        _dest_kernel,
        grid_spec=grid_spec,
        out_shape=jax.ShapeDtypeStruct(top_idx.shape, jnp.int32),
        compiler_params=_cp("arbitrary"),
        name="moe_dest_rows",
    )(seg, top_idx, rank)


def _dispatch_kernel(seg_ref, dest_ref, h2p_ref, xb_ref, zero_ref, ring_ref, sems, zsem):
    i = pl.program_id(0)
    sem = sems.at[i % 2]
    tile_ref = ring_ref.at[i % 2]
    tile_ref[...] = h2p_ref[...]

    @pl.when(i == 0)
    def _():
        zero_ref[...] = jnp.zeros_like(zero_ref)

    @pl.when(i < N_EXP)
    def _():
        first_pad = seg_ref[N_EXP + i]
        next_start = seg_ref[jnp.where(i == N_EXP - 1, 2 * N_EXP, i + 1)]
        _zero_fill_rows(zero_ref, xb_ref, first_pad, next_start - first_pad, zsem)

    tail_row = seg_ref[2 * N_EXP] + (i - N_EXP) * MOE_BLK

    @pl.when((i >= N_EXP) & (tail_row < xb_ref.shape[0]))
    def _():
        copy = pltpu.make_async_copy(zero_ref, xb_ref.at[pl.ds(pl.multiple_of(tail_row, MOE_BLK), MOE_BLK)], zsem)
        copy.start()
        copy.wait()

    def issue(t, carry):
        src = tile_ref.at[pl.ds(t, 1)]
        for k in range(TOP_K):
            pltpu.make_async_copy(src, xb_ref.at[pl.ds(dest_ref[k, t], 1)], sem).start(priority=k % 2)
        return carry

    lax.fori_loop(0, TM, issue, 0, unroll=True)

    @pl.when(i > 0)
    def _():
        for _ in range(TOP_K):
            _row_copy_wait(h2p_ref, xb_ref.at[pl.ds(0, TM)], sems.at[(i + 1) % 2])

    @pl.when(i == pl.num_programs(0) - 1)
    def _():
        for _ in range(TOP_K):
            _row_copy_wait(h2p_ref, xb_ref.at[pl.ds(0, TM)], sem)


def _dispatch_rows(seg, dest, h2p, n_rows):
    n_tiles = h2p.shape[0] // TM
    assert n_tiles >= 2 * N_EXP
    grid_spec = pltpu.PrefetchScalarGridSpec(
        num_scalar_prefetch=1,
        grid=(n_tiles,),
        in_specs=[
            pl.BlockSpec((8, TM), lambda i, seg: (i, 0), memory_space=pltpu.SMEM),
            pl.BlockSpec((TM, D // 2), lambda i, seg: (i, 0)),
        ],
        out_specs=pl.BlockSpec(memory_space=pl.ANY),
        scratch_shapes=[pltpu.VMEM((MOE_BLK, D // 2), jnp.uint32), pltpu.VMEM((2, TM, D // 2), jnp.uint32),
                        pltpu.SemaphoreType.DMA((2,)), pltpu.SemaphoreType.DMA(())],
    )
    return pl.pallas_call(
        _dispatch_kernel,
        grid_spec=grid_spec,
        out_shape=jax.ShapeDtypeStruct((n_rows, D // 2), jnp.uint32),
        compiler_params=_cp("arbitrary"),
        name="moe_dispatch",
    )(seg, dest, h2p)


def _moe_kernel(be_ref, x_ref, wg_ref, wu_ref, wd_ref, o_ref, wgb_ref, wub_ref, wdb_ref):
    i = pl.program_id(0)
    in_use = i < be_ref[pl.num_programs(0)]

    @pl.when(in_use & ((i == 0) | (be_ref[i] != be_ref[jnp.maximum(i - 1, 0)])))
    def _():
        wgb_ref[...] = wg_ref[0, 0].astype(BF16)
        wub_ref[...] = wu_ref[0, 0].astype(BF16)
        wdb_ref[...] = wd_ref[0, 0].astype(BF16)

    @pl.when(in_use)
    def _():
        x = _unpack_bf16_pairs(x_ref[...]).astype(BF16)
        g = _dot(x, wgb_ref[...])
        u = _dot(x, wub_ref[...])
        hid = (_silu(g) * u).astype(BF16)
        o_ref[...] = _pack_bf16_pairs(_dot(hid, wdb_ref[...]))

    @pl.when(jnp.logical_not(in_use))
    def _():
        o_ref[...] = jnp.zeros_like(o_ref)


def _moe_experts(blk_e, xb, l, wg, wu, wd):
    n_rows = xb.shape[0]
    grid_spec = pltpu.PrefetchScalarGridSpec(
        num_scalar_prefetch=1,
        grid=(n_rows // MOE_BLK,),
        in_specs=[
            pl.BlockSpec((MOE_BLK, D // 2), lambda i, be: (i, 0)),
            pl.BlockSpec((1, 1, D, D_EXP), lambda i, be: (l, be[i], 0, 0)),
            pl.BlockSpec((1, 1, D, D_EXP), lambda i, be: (l, be[i], 0, 0)),
            pl.BlockSpec((1, 1, D_EXP, D), lambda i, be: (l, be[i], 0, 0)),
        ],
        out_specs=pl.BlockSpec((MOE_BLK, D // 2), lambda i, be: (i, 0)),
        scratch_shapes=[pltpu.VMEM((D, D_EXP), BF16), pltpu.VMEM((D, D_EXP), BF16), pltpu.VMEM((D_EXP, D), BF16)],
    )
    return pl.pallas_call(
        _moe_kernel,
        grid_spec=grid_spec,
        out_shape=jax.ShapeDtypeStruct((n_rows, D // 2), jnp.uint32),
        compiler_params=_cp("arbitrary"),
        name="moe_experts",
    )(blk_e, xb, wg, wu, wd)


def _ffn_out_kernel(dest_ref, x_ref, mod_ref, h2_ref, tw_ref, wg_ref, wu_ref, wd_ref, nf_ref, yb_ref, *rest, final):
    if final:
        o_ref, gat_ref, sems = rest
    else:
        proj_in, o_ref, proj_out, gat_ref, sems = rest[:10], rest[10], rest[11:16], rest[16], rest[17]
    s = pl.program_id(0)

    def start_gathers(slot):
        buf = gat_ref.at[slot]

        def issue(t, carry):
            for k in range(TOP_K):
                copy = pltpu.make_async_copy(yb_ref.at[pl.ds(dest_ref[k, t], 1)], buf.at[pl.ds(k * TM + t, 1)],
                                             sems.at[slot])
                copy.start(priority=k % 2)
            return carry

        lax.fori_loop(0, TM, issue, 0, unroll=True)

    def wait_gathers(slot):
        for _ in range(TOP_K):
            _row_copy_wait(yb_ref.at[pl.ds(0, TM)], gat_ref.at[slot, pl.ds(0, TM)], sems.at[slot])

    pl.when(s == 0)(lambda: start_gathers(1))
    done = (s + 1) % 2
    wait_gathers(done)
    start_gathers(s % 2)
    m = mod_ref[0]
    h2 = h2_ref[...]
    hid = (_silu(_dot(h2, wg_ref[...])) * _dot(h2, wu_ref[...])).astype(BF16)
    shared = _dot(hid, wd_ref[...])
    buf = gat_ref.at[done]
    tw = tw_ref[...]
    routed = _unpack_bf16_pairs(buf[0:TM, :]) * tw[:, 0:1]
    for k in range(1, TOP_K):
        routed = routed + _unpack_bf16_pairs(buf[k * TM:(k + 1) * TM, :]) * tw[:, k:k + 1]
    x = x_ref[...] + m[5:6] * (routed + shared)
    if final:
        ms = jnp.mean(x * x, axis=-1, keepdims=True)
        x = x * lax.rsqrt(ms + EPS) * nf_ref[...]
    o_ref[...] = x
    if not final:
        _inproj_tile(x, *proj_in, *proj_out)
    pl.when(s == pl.num_programs(0) - 1)(lambda: wait_gathers(s % 2))


def _ffn_out(dest, xn, mod, l, h2, top_w, wg, wu, wd, norm_f, yb, n_tiles, next_proj=None):
    tile_of = lambda i: jnp.maximum(i - 1, 0)
    row = lambda i: (tile_of(i), 0)
    ahead = lambda i: (jnp.minimum(i, n_tiles - 1), 0)
    const = lambda i: (0, 0)
    in_specs = [
        pl.BlockSpec((8, TM), ahead, memory_space=pltpu.SMEM),
        pl.BlockSpec((TM, D), row),
        pl.BlockSpec((1, 6, D), lambda i: (l * 16 + (tile_of(i) * TM) // S, 0, 0)),
        pl.BlockSpec((TM, D), row),
        pl.BlockSpec((TM, 128), row),
        pl.BlockSpec((D, D_EXP), const),
        pl.BlockSpec((D, D_EXP), const),
        pl.BlockSpec((D_EXP, D), const),
        pl.BlockSpec((1, D), const),
        pl.BlockSpec(memory_space=pl.ANY),
    ]
    args = [dest, xn, mod, h2, top_w, wg, wu, wd, norm_f, yb]
    out_specs = [pl.BlockSpec((TM, D), row)]
    out_shape = [jax.ShapeDtypeStruct((n_tiles * TM, D), F32)]
    if next_proj is not None:
        g, ws, tabs = next_proj
        proj_in, proj_out, proj_shape = _inproj_specs(l + 1, tile_of)
        in_specs += proj_in
        args += [mod, g, *ws, *tabs]
        out_specs += proj_out
        out_shape += proj_shape
    return pl.pallas_call(
        functools.partial(_ffn_out_kernel, final=next_proj is None),
        grid=(n_tiles + 1,),
        in_specs=in_specs,
        out_specs=out_specs,
        scratch_shapes=[pltpu.VMEM((2, ASG_TILE, D // 2), jnp.uint32), pltpu.SemaphoreType.DMA((2,))],
        out_shape=out_shape,
        compiler_params=_cp("arbitrary"),
        name="shared_ffn_residual",
    )(*args)


def kernel(x, c, ctx, c_ctx, w_ada, b_ada, norm1, norm2, norm_f, w_in, w_out, hy_conv, hy_w1, hy_b1, hy_w2, hy_b2, hy_w3, hy_freq, hy_bias, wa_sink, na_rpb, gdn_conv, gdn_a_log, gdn_dt_bias, gdn_norm, moe_router, moe_bias, moe_gate, moe_up, moe_down, sh_gate, sh_up, sh_down):
    depth = w_ada.shape[0]
    xa = jnp.concatenate([x.reshape(NL, D), ctx.reshape(NC, D)], axis=0)
    cvec = jnp.concatenate([c, c_ctx[None, :], jnp.zeros((16 - B - 1, D), F32)], axis=0)
    mod = _ada(cvec, w_ada, b_ada).reshape(depth * 16, 6, D)
    rope = _rope_tables()
    dft_lat = _dft_tables(S)
    dft_ctx = _dft_tables(LC)
    o1, o2, o3 = 768, 768 + 512, 768 + 512 + 768

    def proj_params(l):
        wl = w_in[l].astype(BF16)
        ws = (wl[:, :o1], wl[:, o1:o2], wl[:, o2:o3], wl[:, o3:o3 + 1024],
              jnp.pad(wl[:, o3 + 1024:], ((0, 0), (0, 128 - 16))))
        return norm1[l][None, :], ws, rope

    p_hy, p_wa, p_na, p_gd, p_ab = _inproj(xa, mod, 0, *proj_params(0))
    for l in range(depth):
        need_ctx = l < depth - 1
        n_tiles = NT_ALL if need_ctx else NT_LAT

        filt = (hy_w1[l], hy_b1[l], hy_w2[l], hy_b2[l], hy_w3[l], hy_freq[l])
        kre, kim = _hyena_filter_spectrum(S, *filt, dft_lat)
        y_hy = _hyena(p_hy, S, 0, hy_conv[l], hy_bias[l], dft_lat, kre, kim)
        y_wa = _window_attention(p_wa, wa_sink[l])
        y_na = _neighborhood_attention(p_na, _na_bias_table(na_rpb[l]))
        if need_ctx:
            kre_c, kim_c = _hyena_filter_spectrum(LC, *filt, dft_ctx)
            yc_hy = _hyena(p_hy, LC, CTX_BLK0, hy_conv[l], hy_bias[l], dft_ctx, kre_c, kim_c)
            yc_wa, yc_na = _ctx_attention(p_wa, p_na, wa_sink[l])
            y_hy = jnp.concatenate([y_hy, yc_hy], axis=0)
            y_wa = jnp.concatenate([y_wa, yc_wa], axis=0)
            y_na = jnp.concatenate([y_na, yc_na], axis=0)
        qkv, gb = _gdn_prep(p_gd, p_ab, gdn_conv[l], gdn_a_log[l], gdn_dt_bias[l])
        y_gd = _gdn_finish(*_gdn_scan(qkv, gb), p_gd, gdn_norm[l], need_ctx)

        wr = jnp.pad(moe_router[l], ((0, 0), (0, 128 - N_EXP)))
        eb = jnp.pad(moe_bias[l], (0, 128 - N_EXP))[None, :]
        xn, h2, h2p, top_idx, top_w, rank, counts = _outproj(
            xa, mod, l, norm2[l][None, :], (y_hy, y_wa, y_na, y_gd), w_out[l].astype(BF16), wr, eb, n_tiles)
        n_asg = n_tiles * ASG_TILE
        n_blocks = n_asg // MOE_BLK + N_EXP
        seg, blk_e = _segment_tables(counts, n_blocks)
        dest = _dest_rows(seg, top_idx, rank)
        xb = _dispatch_rows(seg, dest, h2p, n_blocks * MOE_BLK)
        yb = _moe_experts(blk_e, xb, l, moe_gate, moe_up, moe_down)
        res = _ffn_out(dest, xn, mod, l, h2, top_w, sh_gate[l].astype(BF16), sh_up[l].astype(BF16),
                       sh_down[l].astype(BF16), norm_f[None, :], yb, n_tiles,
                       next_proj=proj_params(l + 1) if need_ctx else None)
        if need_ctx:
            xa, p_hy, p_wa, p_na, p_gd, p_ab = res
        else:
            return res[0].reshape(B, S, D)
```

```python
import functools
import math

import jax
import jax.numpy as jnp
from jax import lax
from jax.experimental import pallas as pl
from jax.experimental.pallas import tpu as pltpu

F32 = jnp.float32
BF16 = jnp.bfloat16
HI = lax.Precision.HIGHEST

D = 1024
B = 8
S = 2048
LC = 256
GW = 64
HD = 64
NL = B * S
NC = B * LC
ROWS = NL + NC
TM = 256
NT_LAT = NL // TM
NT_ALL = ROWS // TM
CTX_BLK0 = NL // LC

HY_CH = 256
HY_BANDS = 16
HY_DECAY_MIN = -math.log(1e-2) / 1.5
HY_DECAY_MAX = -math.log(1e-2) / 0.3
WINDOW = 128
NA_KR = 8
NA_KC = 16
CHUNK = 64
N_EXP = 32
TOP_K = 4
D_EXP = 256
ROUTED_SCALE = 2.5
MOE_BLK = 512
WA_BLKS = 2
NA_ROWS = 4
GDN_SEQS = 4
EPS = 1e-6
NEG = -1e30
SCALE = HD ** -0.5
VMEM_LIMIT = 56 * 1024 * 1024


def _cp(*sem):
    return pltpu.CompilerParams(dimension_semantics=tuple(sem), vmem_limit_bytes=VMEM_LIMIT)


def _dot(a, b, precision=None):
    return jnp.dot(a, b, preferred_element_type=F32, precision=precision)


def _dot_nt(a, b, precision=None):
    return lax.dot_general(a, b, (((1,), (1,)), ((), ())), preferred_element_type=F32, precision=precision)


def _dot_tn(a, b, precision=None):
    return lax.dot_general(a, b, (((0,), (0,)), ((), ())), preferred_element_type=F32, precision=precision)


def _sigmoid(x):
    return 1.0 / (1.0 + jnp.exp(-x))


def _silu(x):
    return x * _sigmoid(x)


def _softplus(x):
    return jnp.maximum(x, 0.0) + jnp.log(1.0 + jnp.exp(-jnp.abs(x)))


def _shift_rows(x, d):
    n = x.shape[0]
    if d == 0:
        return x
    y = pltpu.roll(x, (-d) % n, axis=0)
    t = lax.broadcasted_iota(jnp.int32, x.shape, 0)
    ok = (t + d >= 0) & (t + d < n)
    return jnp.where(ok, y, 0.0)


def _ada_kernel(c_ref, w_ref, b_ref, o_ref):
    s = _silu(c_ref[...])
    o_ref[0] = _dot(s.astype(BF16), w_ref[0].astype(BF16)) + b_ref[0]


def _ada(cvec, w_ada, b_ada):
    nl = w_ada.shape[0]
    tn = 1536
    return pl.pallas_call(
        _ada_kernel,
        grid=(nl, 6 * D // tn),
        in_specs=[
            pl.BlockSpec((16, D), lambda l, j: (0, 0)),
            pl.BlockSpec((1, D, tn), lambda l, j: (l, 0, j)),
            pl.BlockSpec((1, 1, tn), lambda l, j: (l, 0, j)),
        ],
        out_specs=pl.BlockSpec((1, 16, tn), lambda l, j: (l, 0, j)),
        out_shape=jax.ShapeDtypeStruct((nl, 16, 6 * D), F32),
        compiler_params=_cp("arbitrary", "arbitrary"),
        name="adaln",
    )(cvec, w_ada, b_ada.reshape(nl, 1, 6 * D))


def _modulated_norm(x, g, shift, scale):
    ms = jnp.mean(x * x, axis=-1, keepdims=True)
    y = x * lax.rsqrt(ms + EPS) * g
    return y * (1.0 + scale) + shift


def _inproj_kernel(x_ref, *refs):
    _inproj_tile(x_ref[...], *refs)


def _inproj_tile(x, mod_ref, g_ref, why_ref, wwa_ref, wna_ref, wgd_ref, wab_ref,
                 cos_ref, sa_ref, sb_ref, ohy, owa, ona, ogd, oab):
    m = mod_ref[0]
    h = _modulated_norm(x, g_ref[...], m[0:1], m[1:2]).astype(BF16)
    ohy[...] = _dot(h, why_ref[...]).astype(ohy.dtype)
    ona[...] = _dot(h, wna_ref[...]).astype(ona.dtype)
    ogd[...] = _dot(h, wgd_ref[...]).astype(ogd.dtype)
    oab[...] = _dot(h, wab_ref[...])
    a = _dot(h, wwa_ref[...])
    for c in range(4):
        sl = slice(c * 128, (c + 1) * 128)
        ac = a[:, sl]
        r = (ac * cos_ref[:, sl] + pltpu.roll(ac, 112, axis=1) * sa_ref[:, sl]
             + pltpu.roll(ac, 16, axis=1) * sb_ref[:, sl])
        owa[:, sl] = r.astype(owa.dtype)


def _inproj_specs(l, tile_of):
    row = lambda i: (tile_of(i), 0)
    const = lambda i: (0, 0)
    tab = lambda i: (jnp.where(tile_of(i) < NT_LAT, tile_of(i) % (S // TM), S // TM), 0)
    in_specs = [
        pl.BlockSpec((1, 6, D), lambda i: (l * 16 + (tile_of(i) * TM) // S, 0, 0)),
        pl.BlockSpec((1, D), const),
        pl.BlockSpec((D, 768), const),
        pl.BlockSpec((D, 512), const),
        pl.BlockSpec((D, 768), const),
        pl.BlockSpec((D, 1024), const),
        pl.BlockSpec((D, 128), const),
        pl.BlockSpec((TM, 512), tab),
        pl.BlockSpec((TM, 512), tab),
        pl.BlockSpec((TM, 512), tab),
    ]
    widths = (768, 512, 768, 1024, 128)
    out_specs = [pl.BlockSpec((TM, w), row) for w in widths]
    out_shape = [jax.ShapeDtypeStruct((ROWS, w), F32 if w == 128 else BF16) for w in widths]
    return in_specs, out_specs, out_shape


def _inproj(xa, mod, l, g, ws, tabs):
    in_specs, out_specs, out_shape = _inproj_specs(l, lambda i: i)
    return pl.pallas_call(
        _inproj_kernel,
        grid=(NT_ALL,),
        in_specs=[pl.BlockSpec((TM, D), lambda i: (i, 0))] + in_specs,
        out_specs=out_specs,
        out_shape=out_shape,
        compiler_params=_cp("arbitrary"),
        name="inproj",
    )(xa, mod, g, *ws, *tabs)


def _rope_tables():
    quarter = HD // 4
    pos = jnp.arange(S)
    inv = 10000.0 ** (-jnp.arange(quarter, dtype=F32) / quarter)
    ang_r = (pos // GW).astype(F32)[:, None] * inv[None, :]
    ang_c = (pos % GW).astype(F32)[:, None] * inv[None, :]
    z = jnp.zeros_like(ang_r)
    cos_h = jnp.concatenate([jnp.cos(ang_r)] * 2 + [jnp.cos(ang_c)] * 2, axis=1)
    sa_h = jnp.concatenate([-jnp.sin(ang_r), z, -jnp.sin(ang_c), z], axis=1)
    sb_h = jnp.concatenate([z, jnp.sin(ang_r), z, jnp.sin(ang_c)], axis=1)
    ones = jnp.ones((S, 128), F32)
    zeros = jnp.zeros((S, 128), F32)
    cos_t = jnp.concatenate([jnp.tile(cos_h, (1, 6)), ones], axis=1)
    sa_t = jnp.concatenate([jnp.tile(sa_h, (1, 6)), zeros], axis=1)
    sb_t = jnp.concatenate([jnp.tile(sb_h, (1, 6)), zeros], axis=1)
    ident = jnp.ones((TM, 512), F32)
    none = jnp.zeros((TM, 512), F32)
    return (jnp.concatenate([cos_t, ident], axis=0), jnp.concatenate([sa_t, none], axis=0),
            jnp.concatenate([sb_t, none], axis=0))


def _dft_tables(length):
    n = 2 * length
    f = jnp.arange(length, dtype=jnp.int32)
    m = ((2 * f[:, None] + 1) * (2 * f[None, :] + 1)) % (4 * n)
    th = m.astype(F32) * (2.0 * math.pi / (4 * n))
    phi = (2 * f + 1).astype(F32) * (math.pi / (2 * n))
    return (jnp.cos(th).astype(BF16), jnp.sin(th).astype(BF16),
            jnp.cos(phi)[:, None], jnp.sin(phi)[:, None])


def _hyena_features(length):
    t = jnp.arange(length, dtype=F32)
    t_norm = t / max(length - 1, 1)
    bands = jnp.linspace(1e-4, HY_BANDS - 1, HY_BANDS, dtype=F32)
    band_lane = jnp.concatenate([jnp.zeros((1,), F32), bands, bands, jnp.zeros((128 - 1 - 2 * HY_BANDS,), F32)])
    ang = (2.0 * math.pi / length) * t[:, None] * band_lane[None, :]
    lane = jnp.arange(128)[None, :]
    z = jnp.where(lane == 0, t_norm[:, None],
                  jnp.where(lane <= HY_BANDS, jnp.cos(ang), jnp.where(lane <= 2 * HY_BANDS, -jnp.sin(ang), 0.0)))
    decay = jnp.tile(jnp.linspace(HY_DECAY_MIN, HY_DECAY_MAX, HY_CH, dtype=F32), 2)
    return z, jnp.exp(-t_norm[:, None] * decay[None, :])


def _hyfilt_kernel(z_ref, w1_ref, b1_ref, w2_ref, b2_ref, w3_ref, fr_ref, dec_ref,
                   c_ref, s_ref, cp_ref, sp_ref, kre_ref, kim_ref, pq_ref, *, length):
    @pl.when(pl.program_id(0) == 0)
    def _():
        fr = fr_ref[...]
        h = jnp.sin(fr * (_dot(z_ref[...], w1_ref[...], HI) + b1_ref[...]))
        h = jnp.sin(fr * (_dot(h, w2_ref[...], HI) + b2_ref[...]))
        h = _dot(h, w3_ref[...], HI) * dec_ref[...]
        hf = h[:, :HY_CH]
        t = lax.broadcasted_iota(jnp.int32, (length, HY_CH), 0)
        hb = jnp.where(t == 0, 0.0, h[:, HY_CH:])
        pq_ref[:, :HY_CH] = (hf + hb).astype(BF16)
        pq_ref[:, HY_CH:] = (hb - hf).astype(BF16)

    pq = pq_ref[...]
    cpq = _dot(c_ref[...], pq)
    spq = _dot(s_ref[...], pq)
    cphi, sphi = cp_ref[...], sp_ref[...]
    norm = 1.0 / length
    kre_ref[...] = (cphi * cpq[:, :HY_CH] + sphi * spq[:, :HY_CH]) * norm
    kim_ref[...] = (cphi * spq[:, HY_CH:] - sphi * cpq[:, HY_CH:]) * norm


def _hyena_filter_spectrum(length, w1, b1, w2, b2, w3, freq, dft):
    z, dec = _hyena_features(length)
    c_m, s_m, cphi, sphi = dft
    w1p = jnp.pad(w1, ((0, 128 - w1.shape[0]), (0, 0)))
    tf = min(512, length)
    const = lambda j: (0, 0)
    blk = lambda j: (j, 0)
    return pl.pallas_call(
        functools.partial(_hyfilt_kernel, length=length),
        grid=(length // tf,),
        in_specs=[
            pl.BlockSpec((length, 128), const),
            pl.BlockSpec((128, 64), const),
            pl.BlockSpec((1, 64), const),
            pl.BlockSpec((64, 64), const),
            pl.BlockSpec((1, 64), const),
            pl.BlockSpec((64, 2 * HY_CH), const),
            pl.BlockSpec((1, 64), const),
            pl.BlockSpec((length, 2 * HY_CH), const),
            pl.BlockSpec((tf, length), blk),
            pl.BlockSpec((tf, length), blk),
            pl.BlockSpec((tf, 1), blk),
            pl.BlockSpec((tf, 1), blk),
        ],
        out_specs=[pl.BlockSpec((tf, HY_CH), blk)] * 2,
        out_shape=[jax.ShapeDtypeStruct((length, HY_CH), F32)] * 2,
        scratch_shapes=[pltpu.VMEM((length, 2 * HY_CH), BF16)],
        compiler_params=_cp("arbitrary"),
        name=f"hyena_filter_{length}",
    )(z, w1p, b1[None, :], w2, b2[None, :], w3, freq[None, :], dec, c_m, s_m, cphi, sphi)


def _short_conv3(x_ref, cw):
    x = x_ref[...].astype(F32)
    return _shift_rows(x, -1) * cw[0:1] + x * cw[1:2] + _shift_rows(x, 1) * cw[2:3]


def _hyena_kernel(p0_ref, p1_ref, p2_ref, cw_ref, bias_ref, c_ref, s_ref, kre_ref, kim_ref, o_ref):
    cw = cw_ref[...]
    x0 = _short_conv3(p0_ref, cw[:, :HY_CH])
    k = _short_conv3(p1_ref, cw[:, HY_CH:2 * HY_CH]) * _short_conv3(p2_ref, cw[:, 2 * HY_CH:])
    kb = k.astype(BF16)
    a = _dot(c_ref[...], kb)
    b = _dot(s_ref[...], kb)
    kre, kim = kre_ref[...], kim_ref[...]
    yre = (a * kre + b * kim).astype(BF16)
    yim = (b * kre - a * kim).astype(BF16)
    y = _dot(c_ref[...], yre) + _dot(s_ref[...], yim)
    o_ref[...] = (x0 * (y + k * bias_ref[...])).astype(o_ref.dtype)


def _hyena(p_hy, length, blk0, conv_w, bias, dft, kre, kim):
    c_m, s_m = dft[0], dft[1]
    const = lambda b: (0, 0)
    once = pl.Buffered(1)
    in_specs = [
        pl.BlockSpec((length, HY_CH), lambda b: (blk0 + b, 0)),
        pl.BlockSpec((length, HY_CH), lambda b: (blk0 + b, 1)),
        pl.BlockSpec((length, HY_CH), lambda b: (blk0 + b, 2)),
        pl.BlockSpec((3, 768), const),
        pl.BlockSpec((1, HY_CH), const),
        pl.BlockSpec((length, length), const, pipeline_mode=once),
        pl.BlockSpec((length, length), const, pipeline_mode=once),
        pl.BlockSpec((length, HY_CH), const, pipeline_mode=once),
        pl.BlockSpec((length, HY_CH), const, pipeline_mode=once),
    ]
    return pl.pallas_call(
        _hyena_kernel,
        grid=(B,),
        in_specs=in_specs,
        out_specs=pl.BlockSpec((length, HY_CH), lambda b: (b, 0)),
        out_shape=jax.ShapeDtypeStruct((B * length, HY_CH), BF16),
        compiler_params=_cp("arbitrary"),
        name=f"hyena_{length}",
    )(p_hy, p_hy, p_hy, conv_w, bias[None, :], c_m, s_m, kre, kim)


def _ones_half(vwin, half):
    lane = lax.broadcasted_iota(jnp.int32, vwin.shape, 1)
    keep = (lane < HD) if half == 0 else (lane >= HD)
    return jnp.where(keep, vwin, jnp.ones_like(vwin))


def _softmax_pv(heads):
    probs, maxes = [], []
    for parts, sink, _ in heads:
        m = None
        for s, _ in parts:
            mi = jnp.max(s, axis=-1, keepdims=True)
            m = mi if m is None else jnp.maximum(m, mi)
        if sink is not None:
            m = jnp.maximum(m, sink)
        probs.append([jnp.exp((s - m).astype(BF16)) for s, _ in parts])
        maxes.append(m)
    outs = []
    for (parts, sink, half), es, m in zip(heads, probs, maxes):
        acc = None
        for e, (_, v) in zip(es, parts):
            o = _dot(e, v)
            acc = o if acc is None else acc + o
        den = acc[:, HD:HD + 1] if half == 0 else acc[:, 0:1]
        if sink is not None:
            den = den + jnp.exp(sink - m)
        outs.append(acc / den)
    return outs


def _wa_kernel(sink_ref, q_ref, k_ref, v_ref, kc_ref, vc_ref, o_ref):
    kc, vc = kc_ref[...], vc_ref[...]
    vcs = [_ones_half(vc, hk) for hk in range(2)]
    rr = lax.broadcasted_iota(jnp.int32, (256, 384), 0)
    r1 = lax.broadcasted_iota(jnp.int32, (256, 1), 0)
    heads = []
    for j in range(WA_BLKS):
        n = pl.program_id(1) * WA_BLKS + j
        start = pl.multiple_of(jnp.clip((n - 1) * 128, 0, S - 384), 128)
        kw = k_ref[pl.ds(start, 384), :]
        vw = v_ref[pl.ds(start, 384), :]
        q = q_ref[j * 128:(j + 1) * 128, :] * SCALE
        qpos = n * 128 + jnp.where(rr >= 128, rr - 128, rr)
        kpos = start + lax.broadcasted_iota(jnp.int32, (256, 384), 1)
        valid = jnp.abs(qpos - kpos) <= WINDOW
        for hk in range(2):
            q2 = jnp.concatenate([q[:, (2 * hk) * HD:(2 * hk + 1) * HD],
                                  q[:, (2 * hk + 1) * HD:(2 * hk + 2) * HD]], axis=0)
            hs = slice(hk * HD, (hk + 1) * HD)
            s_loc = jnp.where(valid, _dot_nt(q2, kw[:, hs]), NEG)
            s_ctx = _dot_nt(q2, kc[:, hs])
            sink = jnp.where(r1 >= 128, sink_ref[2 * hk + 1], sink_ref[2 * hk])
            heads.append(([(s_ctx, vcs[hk]), (s_loc, _ones_half(vw, hk))], sink, hk))
    for i, res in enumerate(_softmax_pv(heads)):
        j, hk = divmod(i, 2)
        o = res[:, hk * HD:(hk + 1) * HD]
        rows = slice(j * 128, (j + 1) * 128)
        o_ref[rows, (2 * hk) * HD:(2 * hk + 1) * HD] = o[:128].astype(o_ref.dtype)
        o_ref[rows, (2 * hk + 1) * HD:(2 * hk + 2) * HD] = o[128:].astype(o_ref.dtype)


def _window_attention(p_wa, sink):
    nb = S // (128 * WA_BLKS)
    return pl.pallas_call(
        _wa_kernel,
        grid=(B, nb),
        in_specs=[
            pl.BlockSpec(memory_space=pltpu.SMEM),
            pl.BlockSpec((128 * WA_BLKS, 256), lambda b, n: (b * nb + n, 0)),
            pl.BlockSpec((S, 128), lambda b, n: (b, 2)),
            pl.BlockSpec((S, 128), lambda b, n: (b, 3)),
            pl.BlockSpec((LC, 128), lambda b, n: (CTX_BLK0 + b, 2)),
            pl.BlockSpec((LC, 128), lambda b, n: (CTX_BLK0 + b, 3)),
        ],
        out_specs=pl.BlockSpec((128 * WA_BLKS, 256), lambda b, n: (b * nb + n, 0)),
        out_shape=jax.ShapeDtypeStruct((NL, 256), BF16),
        compiler_params=_cp("arbitrary", "arbitrary"),
        name="window_attn",
    )(sink, p_wa, p_wa, p_wa, p_wa, p_wa)


def _na_kernel(q_ref, k_ref, v_ref, kc_ref, vc_ref, bias_ref, o_ref):
    kc, vc = kc_ref[...], vc_ref[...]
    vcs = [_ones_half(vc[:, (h // 2) * 128:(h // 2 + 1) * 128], h % 2) for h in range(4)]
    heads = []
    for j in range(NA_ROWS):
        r = pl.program_id(1) * NA_ROWS + j
        first = jnp.clip(r - NA_KR // 2, 0, S // GW - NA_KR)
        start = pl.multiple_of(first * GW, GW)
        off = first - r + NA_KR - 1
        kw = k_ref[pl.ds(start, NA_KR * GW), :]
        vw = v_ref[pl.ds(start, NA_KR * GW), :]
        q = q_ref[j * GW:(j + 1) * GW, :] * SCALE
        for h in range(4):
            hs = slice(h * HD, (h + 1) * HD)
            win = slice((h // 2) * 128, (h // 2 + 1) * 128)
            s_loc = _dot_nt(q[:, hs], kw[:, hs]) + bias_ref[h, off]
            s_ctx = _dot_nt(q[:, hs], kc[:, hs])
            heads.append(([(s_ctx, vcs[h]), (s_loc, _ones_half(vw[:, win], h % 2))], None, h % 2))
    outs = _softmax_pv(heads)
    lane = lax.broadcasted_iota(jnp.int32, (GW, 128), 1)
    for j in range(NA_ROWS):
        for hp in range(2):
            pair = jnp.where(lane < HD, outs[4 * j + 2 * hp], outs[4 * j + 2 * hp + 1])
            o_ref[j * GW:(j + 1) * GW, hp * 128:(hp + 1) * 128] = pair.astype(o_ref.dtype)


def _na_bias_kernel(rpb_ref, sel_ref, o_ref):
    o_ref[0] = _dot(rpb_ref[0], sel_ref[...], HI)


def _na_bias_table(rpb):
    qc = jnp.arange(GW)[:, None]
    kcol = jnp.arange(GW)[None, :]
    ci = jnp.clip(kcol - qc + NA_KC - 1, 0, 2 * NA_KC - 2).reshape(1, GW * GW)
    sel = (jnp.arange(128)[:, None] == ci).astype(F32)
    n_ri = 2 * NA_KR - 1
    rpb_p = jnp.pad(rpb.astype(F32), ((0, 0), (0, 16 - n_ri), (0, 128 - (2 * NA_KC - 1))))
    per_row = pl.pallas_call(
        _na_bias_kernel,
        grid=(4,),
        in_specs=[pl.BlockSpec((1, 16, 128), lambda h: (h, 0, 0)), pl.BlockSpec((128, GW * GW), lambda h: (0, 0))],
        out_specs=pl.BlockSpec((1, 16, GW * GW), lambda h: (h, 0, 0)),
        out_shape=jax.ShapeDtypeStruct((4, 16, GW * GW), F32),
        compiler_params=_cp("arbitrary"),
        name="na_bias",
    )(rpb_p, sel).reshape(4, 16, GW, GW)
    tbl = jnp.stack([per_row[:, o:o + NA_KR] for o in range(NA_KR)], axis=1)
    tbl = jnp.transpose(tbl, (0, 1, 3, 2, 4))
    c_start = jnp.clip(qc - NA_KC // 2, 0, GW - NA_KC)
    ok = (kcol >= c_start) & (kcol < c_start + NA_KC)
    tbl = jnp.where(ok[None, None, :, None, :], tbl, NEG)
    return tbl.reshape(4, NA_KR, GW, NA_KR * GW)


def _neighborhood_attention(p_na, bias_tbl):
    steps = S // GW // NA_ROWS
    return pl.pallas_call(
        _na_kernel,
        grid=(B, steps),
        in_specs=[
            pl.BlockSpec((NA_ROWS * GW, 256), lambda b, r: (b * steps + r, 0)),
            pl.BlockSpec((S, 256), lambda b, r: (b, 1)),
            pl.BlockSpec((S, 256), lambda b, r: (b, 2)),
            pl.BlockSpec((LC, 256), lambda b, r: (CTX_BLK0 + b, 1)),
            pl.BlockSpec((LC, 256), lambda b, r: (CTX_BLK0 + b, 2)),
            pl.BlockSpec((4, NA_KR, GW, NA_KR * GW), lambda b, r: (0, 0, 0, 0)),
        ],
        out_specs=pl.BlockSpec((NA_ROWS * GW, 256), lambda b, r: (b * steps + r, 0)),
        out_shape=jax.ShapeDtypeStruct((NL, 256), BF16),
        compiler_params=_cp("arbitrary", "arbitrary"),
        name="neighborhood_attn",
    )(p_na, p_na, p_na, p_na, p_na, bias_tbl)


def _ctx_attn_kernel(sink_ref, qw_ref, kw_ref, vw_ref, qn_ref, kn_ref, vn_ref, owa_ref, ona_ref):
    q, k, v = qw_ref[...] * SCALE, kw_ref[...], vw_ref[...]
    r1 = lax.broadcasted_iota(jnp.int32, (2 * LC, 1), 0)
    heads = []
    for hk in range(2):
        q2 = jnp.concatenate([q[:, (2 * hk) * HD:(2 * hk + 1) * HD],
                              q[:, (2 * hk + 1) * HD:(2 * hk + 2) * HD]], axis=0)
        hs = slice(hk * HD, (hk + 1) * HD)
        sink = jnp.where(r1 >= LC, sink_ref[2 * hk + 1], sink_ref[2 * hk])
        heads.append(([(_dot_nt(q2, k[:, hs]), _ones_half(v, hk))], sink, hk))
    for hk, res in enumerate(_softmax_pv(heads)):
        o = res[:, hk * HD:(hk + 1) * HD]
        owa_ref[:, (2 * hk) * HD:(2 * hk + 1) * HD] = o[:LC].astype(owa_ref.dtype)
        owa_ref[:, (2 * hk + 1) * HD:(2 * hk + 2) * HD] = o[LC:].astype(owa_ref.dtype)
    q, k, v = qn_ref[...] * SCALE, kn_ref[...], vn_ref[...]
    heads = []
    for h in range(4):
        hs = slice(h * HD, (h + 1) * HD)
        win = slice((h // 2) * 128, (h // 2 + 1) * 128)
        heads.append(([(_dot_nt(q[:, hs], k[:, hs]), _ones_half(v[:, win], h % 2))], None, h % 2))
    outs = _softmax_pv(heads)
    lane = lax.broadcasted_iota(jnp.int32, (LC, 128), 1)
    for hp in range(2):
        pair = jnp.where(lane < HD, outs[2 * hp], outs[2 * hp + 1])
        ona_ref[:, hp * 128:(hp + 1) * 128] = pair.astype(ona_ref.dtype)


def _ctx_attention(p_wa, p_na, sink):
    blk = lambda c: (lambda b: (CTX_BLK0 + b, c))
    return pl.pallas_call(
        _ctx_attn_kernel,
        grid=(B,),
        in_specs=[
            pl.BlockSpec(memory_space=pltpu.SMEM),
            pl.BlockSpec((LC, 256), blk(0)),
            pl.BlockSpec((LC, 128), blk(2)),
            pl.BlockSpec((LC, 128), blk(3)),
            pl.BlockSpec((LC, 256), blk(0)),
            pl.BlockSpec((LC, 256), blk(1)),
            pl.BlockSpec((LC, 256), blk(2)),
        ],
        out_specs=[pl.BlockSpec((LC, 256), lambda b: (b, 0))] * 2,
        out_shape=[jax.ShapeDtypeStruct((NC, 256), BF16)] * 2,
        compiler_params=_cp("arbitrary"),
        name="ctx_attn",
    )(sink, p_wa, p_wa, p_wa, p_na, p_na, p_na)


def _gdn_prep_seq(p_ref, ab_ref, cw, alog, dtb, qkv_ref, gb_ref, row0, length):
    x = p_ref[:, :768].astype(F32)
    u = (_shift_rows(x, -2) * cw[0:1] + _shift_rows(x, -1) * cw[1:2] + x * cw[2:3]
         + _shift_rows(x, 1) * cw[3:4])
    u = _silu(u)
    rows = slice(row0, row0 + length)
    for j in range(8):
        sl = slice(j * HD, (j + 1) * HD)
        xs = u[:, sl]
        nrm = lax.rsqrt(jnp.sum(xs * xs, axis=-1, keepdims=True) + EPS)
        qkv_ref[0, rows, sl] = xs * (nrm * SCALE if j < 4 else nrm)
    qkv_ref[0, rows, 512:768] = u[:, 512:768]
    ab = ab_ref[...]
    g = -jnp.exp(alog) * _softplus(ab + dtb)
    lane = lax.broadcasted_iota(jnp.int32, ab.shape, 1)
    gb_ref[0, rows, :] = jnp.where(lane < 8, g, _sigmoid(ab))


def _gdn_prep_kernel(pl_ref, abl_ref, pc_ref, abc_ref, cw_ref, alog_ref, dtb_ref, qkv_ref, gb_ref):
    cw, alog, dtb = cw_ref[...], alog_ref[...], dtb_ref[...]
    _gdn_prep_seq(pl_ref, abl_ref, cw, alog, dtb, qkv_ref, gb_ref, 0, S)
    _gdn_prep_seq(pc_ref, abc_ref, cw, alog, dtb, qkv_ref, gb_ref, S, LC)


def _gdn_prep(p_gd, p_ab, conv_w, a_log, dt_bias):
    const = lambda b: (0, 0)
    alog = jnp.pad(a_log.reshape(1, 8), ((0, 0), (0, 120)))
    dtb = jnp.pad(dt_bias.reshape(1, 8), ((0, 0), (0, 120)))
    return pl.pallas_call(
        _gdn_prep_kernel,
        grid=(B,),
        in_specs=[
            pl.BlockSpec((S, 1024), lambda b: (b, 0)),
            pl.BlockSpec((S, 128), lambda b: (b, 0)),
            pl.BlockSpec((LC, 1024), lambda b: (CTX_BLK0 + b, 0)),
            pl.BlockSpec((LC, 128), lambda b: (CTX_BLK0 + b, 0)),
            pl.BlockSpec((4, 768), const),
            pl.BlockSpec((1, 128), const),
            pl.BlockSpec((1, 128), const),
        ],
        out_specs=[pl.BlockSpec((1, S + LC, 768), lambda b: (b, 0, 0)),
                   pl.BlockSpec((1, S + LC, 128), lambda b: (b, 0, 0))],
        out_shape=[jax.ShapeDtypeStruct((B, S + LC, 768), F32), jax.ShapeDtypeStruct((B, S + LC, 128), F32)],
        compiler_params=_cp("arbitrary"),
        name="gdn_prep",
    )(p_gd, p_ab, p_gd, p_ab, conv_w, alog, dtb)


def _gdn_chains(xs, gbs, cums, cum_ts, states, masks):
    chains = [(g, h) for g in range(len(xs)) for h in range(4)]
    pre = []
    for g, h in chains:
        dirn = g % 2
        x, gb = xs[g], gbs[g]
        tri, strict = masks[dirn]
        col = dirn * 4 + h
        qh = x[:, h * HD:(h + 1) * HD]
        kh = x[:, 256 + h * HD:256 + (h + 1) * HD]
        vh = x[:, 512 + h * HD:512 + (h + 1) * HD]
        gc = cums[g][:, col:col + 1]
        gc_row = cum_ts[g][col:col + 1, :]
        beta = gb[:, 8 + col:9 + col]
        decay = jnp.exp(jnp.where(tri, gc - gc_row, NEG))
        kb = kh * beta
        eg = jnp.exp(gc)
        g_last = gc[0:1, :] if dirn else gc[CHUNK - 1:CHUNK, :]
        pre.append(dict(strict=strict, decay=decay, kbf=kb.astype(BF16), khf=kh.astype(BF16), qf=qh.astype(BF16),
                        xw=jnp.concatenate([vh * beta, kb * eg], axis=1), qdec=(qh * eg).astype(BF16),
                        kdec=(kh * jnp.exp(g_last - gc)).astype(BF16), gl=jnp.exp(g_last)))
    n = len(chains)
    gram = [_dot_nt(p["kbf"], p["khf"]) for p in pre]
    attn = [_dot_nt(p["qf"], p["khf"]) for p in pre]
    lmat = [jnp.where(p["strict"], g * p["decay"], 0.0) for p, g in zip(pre, gram)]
    attn = [(a * p["decay"]).astype(BF16) for p, a in zip(pre, attn)]
    ii = lax.broadcasted_iota(jnp.int32, (CHUNK, CHUNK), 0)
    jj = lax.broadcasted_iota(jnp.int32, (CHUNK, CHUNK), 1)

    def merged_off_blocks(log_s):
        return ((ii >> (log_s + 1)) == (jj >> (log_s + 1))) & ((ii >> log_s) != (jj >> log_s))

    eye = (ii == jj).astype(F32)
    dinv = [eye - jnp.where(merged_off_blocks(0), lm, 0.0) for lm in lmat]
    for log_s in range(1, 6):
        m = merged_off_blocks(log_s)
        dinv_b = [d.astype(BF16) for d in dinv]
        ld = [_dot(jnp.where(m, lmat[c], 0.0).astype(BF16), dinv_b[c]) for c in range(n)]
        upd = [_dot(dinv_b[c], ld[c].astype(BF16)) for c in range(n)]
        dinv = [dinv[c] - upd[c] for c in range(n)]
    xw = [_dot(dinv[c].astype(BF16), pre[c]["xw"].astype(BF16)) for c in range(n)]
    sts = [states[g][h] for g, h in chains]
    stb = [s.astype(BF16) for s in sts]
    ws = [_dot(xw[c][:, HD:].astype(BF16), stb[c]) for c in range(n)]
    qs = [_dot(pre[c]["qdec"], stb[c]) for c in range(n)]
    u_new = [(xw[c][:, :HD] - ws[c]).astype(BF16) for c in range(n)]
    au = [_dot(attn[c], u_new[c]) for c in range(n)]
    ku = [_dot_tn(pre[c]["kdec"], u_new[c]) for c in range(n)]
    outs = [qs[c] + au[c] for c in range(n)]
    new_states = [sts[c] * pre[c]["gl"] + ku[c] for c in range(n)]
    return outs, new_states


def _gdn_kernel(x0_ref, gb0_ref, x1_ref, gb1_ref, o0_ref, o1_ref, state_ref):
    @pl.when(pl.program_id(0) == 0)
    def _():
        state_ref[...] = jnp.zeros_like(state_ref)

    ii = lax.broadcasted_iota(jnp.int32, (CHUNK, CHUNK), 0)
    jj = lax.broadcasted_iota(jnp.int32, (CHUNK, CHUNK), 1)
    masks = ((jj <= ii, jj < ii), (jj >= ii, jj > ii))
    ones_tri = ((jj <= ii).astype(F32), (jj >= ii).astype(F32))

    def body(bp, carry):
        bs = [bp * GDN_SEQS + i for i in range(GDN_SEQS)]
        states = [[state_ref[b, dirn, h] for h in range(4)] for b in bs for dirn in range(2)]
        xs = [r[b] for b in bs for r in (x0_ref, x1_ref)]
        gbs = [r[b] for b in bs for r in (gb0_ref, gb1_ref)]
        cums = [_dot(ones_tri[g % 2], gbs[g], HI) for g in range(2 * GDN_SEQS)]
        cum_ts = [cm.T for cm in cums]
        outs, new_states = _gdn_chains(xs, gbs, cums, cum_ts, states, masks)
        for c, st in enumerate(new_states):
            state_ref[bs[c // 8], (c // 4) % 2, c % 4] = st
        for i, b in enumerate(bs):
            o0_ref[b] = jnp.concatenate(outs[8 * i:8 * i + 4], axis=1)
            o1_ref[b] = jnp.concatenate(outs[8 * i + 4:8 * i + 8], axis=1)
        return carry

    lax.fori_loop(0, B // GDN_SEQS, body, 0)


def _gdn_scan(qkv, gb):
    nc = (S + LC) // CHUNK
    fwd = lambda c: (0, (c + S // CHUNK) % nc, 0)
    bwd = lambda c: (0, nc - 1 - c, 0)
    return pl.pallas_call(
        _gdn_kernel,
        grid=(nc,),
        in_specs=[
            pl.BlockSpec((B, CHUNK, 768), fwd),
            pl.BlockSpec((B, CHUNK, 128), fwd),
            pl.BlockSpec((B, CHUNK, 768), bwd),
            pl.BlockSpec((B, CHUNK, 128), bwd),
        ],
        out_specs=[pl.BlockSpec((B, CHUNK, 256), fwd), pl.BlockSpec((B, CHUNK, 256), bwd)],
        out_shape=[jax.ShapeDtypeStruct((B, S + LC, 256), F32)] * 2,
        scratch_shapes=[pltpu.VMEM((B, 2, 4, HD, HD), F32)],
        compiler_params=_cp("arbitrary"),
        name="gdn_scan",
    )(qkv, gb, qkv, gb)


def _gdn_gated_norm(o, gate, gain):
    gi = lax.broadcasted_iota(jnp.int32, (256, 256), 0) // HD
    gj = lax.broadcasted_iota(jnp.int32, (256, 256), 1) // HD
    group_mean = jnp.where(gi == gj, 1.0 / HD, 0.0).astype(BF16)
    sq = o * o
    sq_hi = sq.astype(BF16)
    sq_lo = (sq - sq_hi.astype(F32)).astype(BF16)
    ms = _dot(sq_hi, group_mean) + _dot(sq_lo, group_mean)
    return o * lax.rsqrt(ms + EPS) * gain * _silu(gate.astype(F32))


def _pack_bf16_pairs(x):
    n = x.shape[1] // 2
    lo = pltpu.bitcast(x[:, :n].astype(BF16).astype(F32), jnp.uint32)
    hi = pltpu.bitcast(x[:, n:].astype(BF16).astype(F32), jnp.uint32)
    return hi | (lo >> 16)


def _unpack_bf16_pairs(w):
    lo = pltpu.bitcast(w << 16, F32)
    hi = pltpu.bitcast(w & jnp.uint32(0xFFFF0000), F32)
    return jnp.concatenate([lo, hi], axis=1)


def _outproj_kernel(x_ref, mod_ref, g2_ref, yh_ref, yw_ref, yn_ref, o0_ref, o1_ref, gate_ref, ng_ref, wo_ref, wr_ref,
                    eb_ref, xo_ref, h2_ref, h2p_ref, idx_ref, tw_ref, rank_ref, cnt_ref, carry_ref):
    @pl.when(pl.program_id(0) == 0)
    def _():
        carry_ref[...] = jnp.zeros_like(carry_ref)

    m = mod_ref[0]
    y_gd = _gdn_gated_norm(o0_ref[0] + o1_ref[0], gate_ref[...], ng_ref[...]).astype(BF16)
    acc = (_dot(yh_ref[...], wo_ref[0:256, :]) + _dot(yw_ref[...], wo_ref[256:512, :])
           + _dot(yn_ref[...], wo_ref[512:768, :]) + _dot(y_gd, wo_ref[768:1024, :]))
    x = x_ref[...] + m[2:3] * acc
    xo_ref[...] = x
    h2 = _modulated_norm(x, g2_ref[...], m[3:4], m[4:5])
    h2_ref[...] = h2.astype(h2_ref.dtype)
    h2p_ref[...] = _pack_bf16_pairs(h2)
    scores = _sigmoid(_dot(h2, wr_ref[...], HI))
    lane = lax.broadcasted_iota(jnp.int32, scores.shape, 1)
    sel = jnp.where(lane < N_EXP, scores + eb_ref[...], -jnp.inf)
    lane_f = lane.astype(F32)
    idx_out = jnp.zeros(scores.shape, F32)
    s_out = jnp.zeros(scores.shape, F32)
    hits = []
    for kk in range(TOP_K):
        mx = jnp.max(sel, axis=-1, keepdims=True)
        idx = jnp.min(jnp.where(sel == mx, lane_f, 128.0), axis=-1, keepdims=True)
        hit = lane_f == idx
        sk = jnp.sum(jnp.where(hit, scores, 0.0), axis=-1, keepdims=True)
        sel = jnp.where(hit, -jnp.inf, sel)
        idx_out = jnp.where(lane == kk, idx, idx_out)
        s_out = jnp.where(lane == kk, sk, s_out)
        hits.append(hit)
    tot = jnp.sum(s_out, axis=-1, keepdims=True)
    tw_ref[...] = s_out / tot * ROUTED_SCALE
    cnt = jnp.zeros(scores.shape, F32)
    for hit in hits:
        cnt = cnt + jnp.where(hit, 1.0, 0.0)
    ri = lax.broadcasted_iota(jnp.int32, (TM, TM), 0)
    rj = lax.broadcasted_iota(jnp.int32, (TM, TM), 1)
    before = _dot(jnp.where(rj < ri, 1.0, 0.0).astype(BF16), cnt.astype(BF16)) + carry_ref[...]
    rank_out = jnp.zeros(scores.shape, F32)
    for kk, hit in enumerate(hits):
        rk = jnp.sum(jnp.where(hit, before, 0.0), axis=-1, keepdims=True)
        rank_out = jnp.where(lane == kk, rk, rank_out)
    idx_ref[...] = idx_out.T[:8].astype(jnp.int32)
    rank_ref[...] = rank_out.T[:8].astype(jnp.int32)
    carry_ref[...] = carry_ref[...] + jnp.sum(cnt, axis=0, keepdims=True)
    cnt_ref[...] = carry_ref[...].astype(jnp.int32)


def _outproj(xa, mod, l, g2, ys, gdn, wo, wr, eb, n_tiles):
    row = lambda i: (i, 0)
    const = lambda i: (0, 0)
    lat_tiles = S // TM
    seq = lambda i: (jnp.where(i < NT_LAT, i // lat_tiles, i - NT_LAT), jnp.where(i < NT_LAT, i % lat_tiles, lat_tiles), 0)
    rows = n_tiles * TM
    o0, o1, p_gd, norm_g = gdn
    return pl.pallas_call(
        _outproj_kernel,
        grid=(n_tiles,),
        in_specs=[
            pl.BlockSpec((TM, D), row),
            pl.BlockSpec((1, 6, D), lambda i: (l * 16 + (i * TM) // S, 0, 0)),
            pl.BlockSpec((1, D), const),
            pl.BlockSpec((TM, 256), row),
            pl.BlockSpec((TM, 256), row),
            pl.BlockSpec((TM, 256), row),
            pl.BlockSpec((1, TM, 256), seq),
            pl.BlockSpec((1, TM, 256), seq),
            pl.BlockSpec((TM, 256), lambda i: (i, 3)),
            pl.BlockSpec((1, 256), const),
            pl.BlockSpec((D, D), const),
            pl.BlockSpec((D, 128), const),
            pl.BlockSpec((1, 128), const),
        ],
        out_specs=[
            pl.BlockSpec((TM, D), row),
            pl.BlockSpec((TM, D), row),
            pl.BlockSpec((TM, D // 2), row),
            pl.BlockSpec((8, TM), row),
            pl.BlockSpec((TM, 128), row),
            pl.BlockSpec((8, TM), row),
            pl.BlockSpec((1, 128), const),
        ],
        out_shape=[
            jax.ShapeDtypeStruct((rows, D), F32),
            jax.ShapeDtypeStruct((rows, D), BF16),
            jax.ShapeDtypeStruct((rows, D // 2), jnp.uint32),
            jax.ShapeDtypeStruct((n_tiles * 8, TM), jnp.int32),
            jax.ShapeDtypeStruct((rows, 128), F32),
            jax.ShapeDtypeStruct((n_tiles * 8, TM), jnp.int32),
            jax.ShapeDtypeStruct((1, 128), jnp.int32),
        ],
        scratch_shapes=[pltpu.VMEM((1, 128), F32)],
        compiler_params=_cp("arbitrary"),
        name="outproj_router",
    )(xa, mod, g2, *ys, o0, o1, p_gd, jnp.tile(norm_g, 4)[None, :], wo, wr, eb)


ASG_TILE = TM * TOP_K


def _segment_tables(counts, n_blocks):
    counts = counts[0, :N_EXP]
    padded = (counts + MOE_BLK - 1) // MOE_BLK * MOE_BLK
    pad_end = jnp.cumsum(padded)
    seg_start = pad_end - padded
    starts = jnp.arange(n_blocks, dtype=jnp.int32) * MOE_BLK
    blk_e = jnp.minimum(jnp.sum((pad_end[None, :] <= starts[:, None]).astype(jnp.int32), axis=1), N_EXP - 1)
    seg = jnp.concatenate([seg_start, seg_start + counts, pad_end[-1:]])
    blk = jnp.concatenate([blk_e, pad_end[-1:] // MOE_BLK])
    return seg.astype(jnp.int32), blk.astype(jnp.int32)


def _row_copy_wait(shape_ref, dst_ref, sem):
    pltpu.make_async_copy(shape_ref, dst_ref, sem).wait()


def _zero_fill_rows(zero_ref, xb_ref, start, length, sem):
    plan = []
    aligned = (start + 7) & -8
    for r in range(7):
        plan.append(((start + r < aligned) & (r < length),
                     pltpu.make_async_copy(zero_ref.at[pl.ds(0, 1)], xb_ref.at[pl.ds(start + r, 1)], sem)))
    rest = start + length - aligned
    off = aligned
    for bit in reversed(range(3, MOE_BLK.bit_length() - 1)):
        size = 1 << bit
        take = (rest & size) != 0
        dst = xb_ref.at[pl.ds(pl.multiple_of(off, 8), size)]
        plan.append((take, pltpu.make_async_copy(zero_ref.at[pl.ds(0, size)], dst, sem)))
        off = off + jnp.where(take, size, 0)
    for take, copy in plan:
        pl.when(take)(copy.start)
    for take, copy in plan:
        pl.when(take)(copy.wait)


def _dest_kernel(seg_ref, idx_ref, rank_ref, o_ref):
    idx = idx_ref[...]
    first = jnp.zeros(idx.shape, jnp.int32)
    for e in range(N_EXP):
        first = jnp.where(idx == e, seg_ref[e], first)
    o_ref[...] = first + rank_ref[...]


def _dest_rows(seg, top_idx, rank):
    n_tiles = top_idx.shape[0] // 8
    blk = lambda i, seg: (i, 0)
    grid_spec = pltpu.PrefetchScalarGridSpec(
        num_scalar_prefetch=1,
        grid=(n_tiles // 8,),
        in_specs=[pl.BlockSpec((64, TM), blk), pl.BlockSpec((64, TM), blk)],
        out_specs=pl.BlockSpec((64, TM), blk),
    )
    return pl.pallas_call(
        _dest_kernel,
        grid_spec=grid_spec,
        out_shape=jax.ShapeDtypeStruct(top_idx.shape, jnp.int32),
        compiler_params=_cp("arbitrary"),
        name="moe_dest_rows",
    )(seg, top_idx, rank)


def _dispatch_kernel(seg_ref, dest_ref, h2p_ref, xb_ref, zero_ref, ring_ref, sems, zsem):
    i = pl.program_id(0)
    sem = sems.at[i % 2]
    tile_ref = ring_ref.at[i % 2]
    tile_ref[...] = h2p_ref[...]

    @pl.when(i == 0)
    def _():
        zero_ref[...] = jnp.zeros_like(zero_ref)

    @pl.when(i < N_EXP)
    def _():
        first_pad = seg_ref[N_EXP + i]
        next_start = seg_ref[jnp.where(i == N_EXP - 1, 2 * N_EXP, i + 1)]
        _zero_fill_rows(zero_ref, xb_ref, first_pad, next_start - first_pad, zsem)

    tail_row = seg_ref[2 * N_EXP] + (i - N_EXP) * MOE_BLK

    @pl.when((i >= N_EXP) & (tail_row < xb_ref.shape[0]))
    def _():
        copy = pltpu.make_async_copy(zero_ref, xb_ref.at[pl.ds(pl.multiple_of(tail_row, MOE_BLK), MOE_BLK)], zsem)
        copy.start()
        copy.wait()

    def issue(t, carry):
        src = tile_ref.at[pl.ds(t, 1)]
        for k in range(TOP_K):
            pltpu.make_async_copy(src, xb_ref.at[pl.ds(dest_ref[k, t], 1)], sem).start(priority=k % 2)
        return carry

    lax.fori_loop(0, TM, issue, 0, unroll=True)

    @pl.when(i > 0)
    def _():
        for _ in range(TOP_K):
            _row_copy_wait(h2p_ref, xb_ref.at[pl.ds(0, TM)], sems.at[(i + 1) % 2])

    @pl.when(i == pl.num_programs(0) - 1)
    def _():
        for _ in range(TOP_K):
            _row_copy_wait(h2p_ref, xb_ref.at[pl.ds(0, TM)], sem)


def _dispatch_rows(seg, dest, h2p, n_rows):
    n_tiles = h2p.shape[0] // TM
    assert n_tiles >= 2 * N_EXP
    grid_spec = pltpu.PrefetchScalarGridSpec(
        num_scalar_prefetch=1,
        grid=(n_tiles,),
        in_specs=[
            pl.BlockSpec((8, TM), lambda i, seg: (i, 0), memory_space=pltpu.SMEM),
            pl.BlockSpec((TM, D // 2), lambda i, seg: (i, 0)),
        ],
        out_specs=pl.BlockSpec(memory_space=pl.ANY),
        scratch_shapes=[pltpu.VMEM((MOE_BLK, D // 2), jnp.uint32), pltpu.VMEM((2, TM, D // 2), jnp.uint32),
                        pltpu.SemaphoreType.DMA((2,)), pltpu.SemaphoreType.DMA(())],
    )
    return pl.pallas_call(
        _dispatch_kernel,
        grid_spec=grid_spec,
        out_shape=jax.ShapeDtypeStruct((n_rows, D // 2), jnp.uint32),
        compiler_params=_cp("arbitrary"),
        name="moe_dispatch",
    )(seg, dest, h2p)


def _moe_kernel(be_ref, x_ref, wg_ref, wu_ref, wd_ref, o_ref, wgb_ref, wub_ref, wdb_ref):
    i = pl.program_id(0)
    in_use = i < be_ref[pl.num_programs(0)]

    @pl.when(in_use & ((i == 0) | (be_ref[i] != be_ref[jnp.maximum(i - 1, 0)])))
    def _():
        wgb_ref[...] = wg_ref[0, 0].astype(BF16)
        wub_ref[...] = wu_ref[0, 0].astype(BF16)
        wdb_ref[...] = wd_ref[0, 0].astype(BF16)

    @pl.when(in_use)
    def _():
        x = _unpack_bf16_pairs(x_ref[...]).astype(BF16)
        g = _dot(x, wgb_ref[...])
        u = _dot(x, wub_ref[...])
        hid = (_silu(g) * u).astype(BF16)
        o_ref[...] = _pack_bf16_pairs(_dot(hid, wdb_ref[...]))

    @pl.when(jnp.logical_not(in_use))
    def _():
        o_ref[...] = jnp.zeros_like(o_ref)


def _moe_experts(blk_e, xb, l, wg, wu, wd):
    n_rows = xb.shape[0]
    grid_spec = pltpu.PrefetchScalarGridSpec(
        num_scalar_prefetch=1,
        grid=(n_rows // MOE_BLK,),
        in_specs=[
            pl.BlockSpec((MOE_BLK, D // 2), lambda i, be: (i, 0)),
            pl.BlockSpec((1, 1, D, D_EXP), lambda i, be: (l, be[i], 0, 0)),
            pl.BlockSpec((1, 1, D, D_EXP), lambda i, be: (l, be[i], 0, 0)),
            pl.BlockSpec((1, 1, D_EXP, D), lambda i, be: (l, be[i], 0, 0)),
        ],
        out_specs=pl.BlockSpec((MOE_BLK, D // 2), lambda i, be: (i, 0)),
        scratch_shapes=[pltpu.VMEM((D, D_EXP), BF16), pltpu.VMEM((D, D_EXP), BF16), pltpu.VMEM((D_EXP, D), BF16)],
    )
    return pl.pallas_call(
        _moe_kernel,
        grid_spec=grid_spec,
        out_shape=jax.ShapeDtypeStruct((n_rows, D // 2), jnp.uint32),
        compiler_params=_cp("arbitrary"),
        name="moe_experts",
    )(blk_e, xb, wg, wu, wd)


def _ffn_out_kernel(dest_ref, x_ref, mod_ref, h2_ref, tw_ref, wg_ref, wu_ref, wd_ref, nf_ref, yb_ref, *rest, final):
    if final:
        o_ref, gat_ref, sems = rest
    else:
        proj_in, o_ref, proj_out, gat_ref, sems = rest[:10], rest[10], rest[11:16], rest[16], rest[17]
    s = pl.program_id(0)

    def start_gathers(slot):
        buf = gat_ref.at[slot]

        def issue(t, carry):
            for k in range(TOP_K):
                copy = pltpu.make_async_copy(yb_ref.at[pl.ds(dest_ref[k, t], 1)], buf.at[pl.ds(k * TM + t, 1)],
                                             sems.at[slot])
                copy.start(priority=k % 2)
            return carry

        lax.fori_loop(0, TM, issue, 0, unroll=True)

    def wait_gathers(slot):
        for _ in range(TOP_K):
            _row_copy_wait(yb_ref.at[pl.ds(0, TM)], gat_ref.at[slot, pl.ds(0, TM)], sems.at[slot])

    pl.when(s == 0)(lambda: start_gathers(1))
    done = (s + 1) % 2
    wait_gathers(done)
    start_gathers(s % 2)
    m = mod_ref[0]
    h2 = h2_ref[...]
    hid = (_silu(_dot(h2, wg_ref[...])) * _dot(h2, wu_ref[...])).astype(BF16)
    shared = _dot(hid, wd_ref[...])
    buf = gat_ref.at[done]
    tw = tw_ref[...]
    routed = _unpack_bf16_pairs(buf[0:TM, :]) * tw[:, 0:1]
    for k in range(1, TOP_K):
        routed = routed + _unpack_bf16_pairs(buf[k * TM:(k + 1) * TM, :]) * tw[:, k:k + 1]
    x = x_ref[...] + m[5:6] * (routed + shared)
    if final:
        ms = jnp.mean(x * x, axis=-1, keepdims=True)
        x = x * lax.rsqrt(ms + EPS) * nf_ref[...]
    o_ref[...] = x
    if not final:
        _inproj_tile(x, *proj_in, *proj_out)
    pl.when(s == pl.num_programs(0) - 1)(lambda: wait_gathers(s % 2))


def _ffn_out(dest, xn, mod, l, h2, top_w, wg, wu, wd, norm_f, yb, n_tiles, next_proj=None):
    tile_of = lambda i: jnp.maximum(i - 1, 0)
    row = lambda i: (tile_of(i), 0)
    ahead = lambda i: (jnp.minimum(i, n_tiles - 1), 0)
    const = lambda i: (0, 0)
    in_specs = [
        pl.BlockSpec((8, TM), ahead, memory_space=pltpu.SMEM),
        pl.BlockSpec((TM, D), row),
        pl.BlockSpec((1, 6, D), lambda i: (l * 16 + (tile_of(i) * TM) // S, 0, 0)),
        pl.BlockSpec((TM, D), row),
        pl.BlockSpec((TM, 128), row),
        pl.BlockSpec((D, D_EXP), const),
        pl.BlockSpec((D, D_EXP), const),
        pl.BlockSpec((D_EXP, D), const),
        pl.BlockSpec((1, D), const),
        pl.BlockSpec(memory_space=pl.ANY),
    ]
    args = [dest, xn, mod, h2, top_w, wg, wu, wd, norm_f, yb]
    out_specs = [pl.BlockSpec((TM, D), row)]
    out_shape = [jax.ShapeDtypeStruct((n_tiles * TM, D), F32)]
    if next_proj is not None:
        g, ws, tabs = next_proj
        proj_in, proj_out, proj_shape = _inproj_specs(l + 1, tile_of)
        in_specs += proj_in
        args += [mod, g, *ws, *tabs]
        out_specs += proj_out
        out_shape += proj_shape
    return pl.pallas_call(
        functools.partial(_ffn_out_kernel, final=next_proj is None),
        grid=(n_tiles + 1,),
        in_specs=in_specs,
        out_specs=out_specs,
        scratch_shapes=[pltpu.VMEM((2, ASG_TILE, D // 2), jnp.uint32), pltpu.SemaphoreType.DMA((2,))],
        out_shape=out_shape,
        compiler_params=_cp("arbitrary"),
        name="shared_ffn_residual",
    )(*args)


def kernel(x, c, ctx, c_ctx, w_ada, b_ada, norm1, norm2, norm_f, w_in, w_out, hy_conv, hy_w1, hy_b1, hy_w2, hy_b2, hy_w3, hy_freq, hy_bias, wa_sink, na_rpb, gdn_conv, gdn_a_log, gdn_dt_bias, gdn_norm, moe_router, moe_bias, moe_gate, moe_up, moe_down, sh_gate, sh_up, sh_down):
    depth = w_ada.shape[0]
    xa = jnp.concatenate([x.reshape(NL, D), ctx.reshape(NC, D)], axis=0)
    cvec = jnp.concatenate([c, c_ctx[None, :], jnp.zeros((16 - B - 1, D), F32)], axis=0)
    mod = _ada(cvec, w_ada, b_ada).reshape(depth * 16, 6, D)
    rope = _rope_tables()
    dft_lat = _dft_tables(S)
    dft_ctx = _dft_tables(LC)
    o1, o2, o3 = 768, 768 + 512, 768 + 512 + 768

    def proj_params(l):
        wl = w_in[l].astype(BF16)
        ws = (wl[:, :o1], wl[:, o1:o2], wl[:, o2:o3], wl[:, o3:o3 + 1024],
              jnp.pad(wl[:, o3 + 1024:], ((0, 0), (0, 128 - 16))))
        return norm1[l][None, :], ws, rope

    p_hy, p_wa, p_na, p_gd, p_ab = _inproj(xa, mod, 0, *proj_params(0))
    for l in range(depth):
        need_ctx = l < depth - 1
        n_tiles = NT_ALL if need_ctx else NT_LAT

        filt = (hy_w1[l], hy_b1[l], hy_w2[l], hy_b2[l], hy_w3[l], hy_freq[l])
        kre, kim = _hyena_filter_spectrum(S, *filt, dft_lat)
        y_hy = _hyena(p_hy, S, 0, hy_conv[l], hy_bias[l], dft_lat, kre, kim)
        y_wa = _window_attention(p_wa, wa_sink[l])
        y_na = _neighborhood_attention(p_na, _na_bias_table(na_rpb[l]))
        if need_ctx:
            kre_c, kim_c = _hyena_filter_spectrum(LC, *filt, dft_ctx)
            yc_hy = _hyena(p_hy, LC, CTX_BLK0, hy_conv[l], hy_bias[l], dft_ctx, kre_c, kim_c)
            yc_wa, yc_na = _ctx_attention(p_wa, p_na, wa_sink[l])
            y_hy = jnp.concatenate([y_hy, yc_hy], axis=0)
            y_wa = jnp.concatenate([y_wa, yc_wa], axis=0)
            y_na = jnp.concatenate([y_na, yc_na], axis=0)
        qkv, gb = _gdn_prep(p_gd, p_ab, gdn_conv[l], gdn_a_log[l], gdn_dt_bias[l])
        gd_fwd, gd_bwd = _gdn_scan(qkv, gb)

        wr = jnp.pad(moe_router[l], ((0, 0), (0, 128 - N_EXP)))
        eb = jnp.pad(moe_bias[l], (0, 128 - N_EXP))[None, :]
        xn, h2, h2p, top_idx, top_w, rank, counts = _outproj(
            xa, mod, l, norm2[l][None, :], (y_hy, y_wa, y_na), (gd_fwd, gd_bwd, p_gd, gdn_norm[l]),
            w_out[l].astype(BF16), wr, eb, n_tiles)
        n_asg = n_tiles * ASG_TILE
        n_blocks = n_asg // MOE_BLK + N_EXP
        seg, blk_e = _segment_tables(counts, n_blocks)
        dest = _dest_rows(seg, top_idx, rank)
        xb = _dispatch_rows(seg, dest, h2p, n_blocks * MOE_BLK)
        yb = _moe_experts(blk_e, xb, l, moe_gate, moe_up, moe_down)
        res = _ffn_out(dest, xn, mod, l, h2, top_w, sh_gate[l].astype(BF16), sh_up[l].astype(BF16),
                       sh_down[l].astype(BF16), norm_f[None, :], yb, n_tiles,
                       next_proj=proj_params(l + 1) if need_ctx else None)
        if need_ctx:
            xa, p_hy, p_wa, p_na, p_gd, p_ab = res
        else:
            return res[0].reshape(B, S, D)
```

```python
import functools
import math

import jax
import jax.numpy as jnp
from jax import lax
from jax.experimental import pallas as pl
from jax.experimental.pallas import tpu as pltpu

F32 = jnp.float32
BF16 = jnp.bfloat16
HI = lax.Precision.HIGHEST

D = 1024
B = 8
S = 2048
LC = 256
GW = 64
HD = 64
NL = B * S
NC = B * LC
ROWS = NL + NC
TM = 256
NT_LAT = NL // TM
NT_ALL = ROWS // TM
CTX_BLK0 = NL // LC

HY_CH = 256
HY_BANDS = 16
HY_DECAY_MIN = -math.log(1e-2) / 1.5
HY_DECAY_MAX = -math.log(1e-2) / 0.3
WINDOW = 128
NA_KR = 8
NA_KC = 16
CHUNK = 64
N_EXP = 32
TOP_K = 4
D_EXP = 256
ROUTED_SCALE = 2.5
MOE_BLK = 512
WA_BLKS = 4
NA_ROWS = 8
GDN_SEQS = 4
EPS = 1e-6
NEG = -1e30
SCALE = HD ** -0.5
VMEM_LIMIT = 56 * 1024 * 1024


def _cp(*sem):
    return pltpu.CompilerParams(dimension_semantics=tuple(sem), vmem_limit_bytes=VMEM_LIMIT)


def _dot(a, b, precision=None):
    return jnp.dot(a, b, preferred_element_type=F32, precision=precision)


def _dot_nt(a, b, precision=None):
    return lax.dot_general(a, b, (((1,), (1,)), ((), ())), preferred_element_type=F32, precision=precision)


def _dot_tn(a, b, precision=None):
    return lax.dot_general(a, b, (((0,), (0,)), ((), ())), preferred_element_type=F32, precision=precision)


def _sigmoid(x):
    return 1.0 / (1.0 + jnp.exp(-x))


def _silu(x):
    return x * _sigmoid(x)


def _softplus(x):
    return jnp.maximum(x, 0.0) + jnp.log(1.0 + jnp.exp(-jnp.abs(x)))


def _shift_rows(x, d):
    n = x.shape[0]
    if d == 0:
        return x
    y = pltpu.roll(x, (-d) % n, axis=0)
    t = lax.broadcasted_iota(jnp.int32, x.shape, 0)
    ok = (t + d >= 0) & (t + d < n)
    return jnp.where(ok, y, 0.0)


def _ada_kernel(c_ref, w_ref, b_ref, o_ref):
    s = _silu(c_ref[...])
    o_ref[0] = _dot(s.astype(BF16), w_ref[0].astype(BF16)) + b_ref[0]


def _ada(cvec, w_ada, b_ada):
    nl = w_ada.shape[0]
    tn = 1536
    return pl.pallas_call(
        _ada_kernel,
        grid=(nl, 6 * D // tn),
        in_specs=[
            pl.BlockSpec((16, D), lambda l, j: (0, 0)),
            pl.BlockSpec((1, D, tn), lambda l, j: (l, 0, j)),
            pl.BlockSpec((1, 1, tn), lambda l, j: (l, 0, j)),
        ],
        out_specs=pl.BlockSpec((1, 16, tn), lambda l, j: (l, 0, j)),
        out_shape=jax.ShapeDtypeStruct((nl, 16, 6 * D), F32),
        compiler_params=_cp("arbitrary", "arbitrary"),
        name="adaln",
    )(cvec, w_ada, b_ada.reshape(nl, 1, 6 * D))


def _modulated_norm(x, g, shift, scale):
    ms = jnp.mean(x * x, axis=-1, keepdims=True)
    y = x * lax.rsqrt(ms + EPS) * g
    return y * (1.0 + scale) + shift


def _row_tile_specs():
    return [pl.BlockSpec((TM, D), lambda i: (jnp.minimum(i, NT_LAT - 1), 0)),
            pl.BlockSpec((TM, D), lambda i: (jnp.maximum(i - NT_LAT, 0), 0))]


def _row_tile(xl_ref, xc_ref):
    return jnp.where(pl.program_id(0) < NT_LAT, xl_ref[...], xc_ref[...])


def _inproj_kernel(xl_ref, xc_ref, *refs):
    _inproj_tile(_row_tile(xl_ref, xc_ref), *refs)


def _inproj_tile(x, mod_ref, g_ref, why_ref, wwa_ref, wna_ref, wgd_ref, wab_ref,
                 cos_ref, sa_ref, sb_ref, ohy, owa, ona, ogd, oab):
    m = mod_ref[0]
    h = _modulated_norm(x, g_ref[...], m[0:1], m[1:2]).astype(BF16)
    ohy[...] = _dot(h, why_ref[...]).astype(ohy.dtype)
    ona[...] = _dot(h, wna_ref[...]).astype(ona.dtype)
    ogd[...] = _dot(h, wgd_ref[...]).astype(ogd.dtype)
    oab[...] = _dot(h, wab_ref[...])
    a = _dot(h, wwa_ref[...])
    for c in range(4):
        sl = slice(c * 128, (c + 1) * 128)
        ac = a[:, sl]
        r = (ac * cos_ref[:, sl] + pltpu.roll(ac, 112, axis=1) * sa_ref[:, sl]
             + pltpu.roll(ac, 16, axis=1) * sb_ref[:, sl])
        owa[:, sl] = r.astype(owa.dtype)


def _inproj_specs(l, tile_of):
    row = lambda i: (tile_of(i), 0)
    const = lambda i: (0, 0)
    tab = lambda i: (jnp.where(tile_of(i) < NT_LAT, tile_of(i) % (S // TM), S // TM), 0)
    in_specs = [
        pl.BlockSpec((1, 6, D), lambda i: (l * 16 + (tile_of(i) * TM) // S, 0, 0)),
        pl.BlockSpec((1, D), const),
        pl.BlockSpec((D, 768), const),
        pl.BlockSpec((D, 512), const),
        pl.BlockSpec((D, 768), const),
        pl.BlockSpec((D, 1024), const),
        pl.BlockSpec((D, 128), const),
        pl.BlockSpec((TM, 512), tab),
        pl.BlockSpec((TM, 512), tab),
        pl.BlockSpec((TM, 512), tab),
    ]
    widths = (768, 512, 768, 1024, 128)
    out_specs = [pl.BlockSpec((TM, w), row) for w in widths]
    out_shape = [jax.ShapeDtypeStruct((ROWS, w), F32 if w == 128 else BF16) for w in widths]
    return in_specs, out_specs, out_shape


def _inproj(x_lat, x_ctx, mod, l, g, ws, tabs):
    in_specs, out_specs, out_shape = _inproj_specs(l, lambda i: i)
    return pl.pallas_call(
        _inproj_kernel,
        grid=(NT_ALL,),
        in_specs=_row_tile_specs() + in_specs,
        out_specs=out_specs,
        out_shape=out_shape,
        compiler_params=_cp("arbitrary"),
        name="inproj",
    )(x_lat, x_ctx, mod, g, *ws, *tabs)


def _rope_tables():
    quarter = HD // 4
    pos = jnp.arange(S)
    inv = 10000.0 ** (-jnp.arange(quarter, dtype=F32) / quarter)
    ang_r = (pos // GW).astype(F32)[:, None] * inv[None, :]
    ang_c = (pos % GW).astype(F32)[:, None] * inv[None, :]
    z = jnp.zeros_like(ang_r)
    cos_h = jnp.concatenate([jnp.cos(ang_r)] * 2 + [jnp.cos(ang_c)] * 2, axis=1)
    sa_h = jnp.concatenate([-jnp.sin(ang_r), z, -jnp.sin(ang_c), z], axis=1)
    sb_h = jnp.concatenate([z, jnp.sin(ang_r), z, jnp.sin(ang_c)], axis=1)
    ones = jnp.ones((S, 128), F32)
    zeros = jnp.zeros((S, 128), F32)
    cos_t = jnp.concatenate([jnp.tile(cos_h, (1, 6)), ones], axis=1)
    sa_t = jnp.concatenate([jnp.tile(sa_h, (1, 6)), zeros], axis=1)
    sb_t = jnp.concatenate([jnp.tile(sb_h, (1, 6)), zeros], axis=1)
    ident = jnp.ones((TM, 512), F32)
    none = jnp.zeros((TM, 512), F32)
    return (jnp.concatenate([cos_t, ident], axis=0), jnp.concatenate([sa_t, none], axis=0),
            jnp.concatenate([sb_t, none], axis=0))


def _dft_tables(length):
    n = 2 * length
    f = jnp.arange(length, dtype=jnp.int32)
    m = ((2 * f[:, None] + 1) * (2 * f[None, :] + 1)) % (4 * n)
    th = m.astype(F32) * (2.0 * math.pi / (4 * n))
    phi = (2 * f + 1).astype(F32) * (math.pi / (2 * n))
    return (jnp.cos(th).astype(BF16), jnp.sin(th).astype(BF16),
            jnp.cos(phi)[:, None], jnp.sin(phi)[:, None])


def _hyena_features(length):
    t = jnp.arange(length, dtype=F32)
    t_norm = t / max(length - 1, 1)
    bands = jnp.linspace(1e-4, HY_BANDS - 1, HY_BANDS, dtype=F32)
    band_lane = jnp.concatenate([jnp.zeros((1,), F32), bands, bands, jnp.zeros((128 - 1 - 2 * HY_BANDS,), F32)])
    ang = (2.0 * math.pi / length) * t[:, None] * band_lane[None, :]
    lane = jnp.arange(128)[None, :]
    z = jnp.where(lane == 0, t_norm[:, None],
                  jnp.where(lane <= HY_BANDS, jnp.cos(ang), jnp.where(lane <= 2 * HY_BANDS, -jnp.sin(ang), 0.0)))
    decay = jnp.tile(jnp.linspace(HY_DECAY_MIN, HY_DECAY_MAX, HY_CH, dtype=F32), 2)
    return z, jnp.exp(-t_norm[:, None] * decay[None, :])


def _hyfilt_kernel(z_ref, w1_ref, b1_ref, w2_ref, b2_ref, w3_ref, fr_ref, dec_ref,
                   c_ref, s_ref, cp_ref, sp_ref, kre_ref, kim_ref, pq_ref, *, length):
    @pl.when(pl.program_id(0) == 0)
    def _():
        fr = fr_ref[...]
        h = jnp.sin(fr * (_dot(z_ref[...], w1_ref[...], HI) + b1_ref[...]))
        h = jnp.sin(fr * (_dot(h, w2_ref[...], HI) + b2_ref[...]))
        h = _dot(h, w3_ref[...], HI) * dec_ref[...]
        hf = h[:, :HY_CH]
        t = lax.broadcasted_iota(jnp.int32, (length, HY_CH), 0)
        hb = jnp.where(t == 0, 0.0, h[:, HY_CH:])
        pq_ref[:, :HY_CH] = (hf + hb).astype(BF16)
        pq_ref[:, HY_CH:] = (hb - hf).astype(BF16)

    pq = pq_ref[...]
    cpq = _dot(c_ref[...], pq)
    spq = _dot(s_ref[...], pq)
    cphi, sphi = cp_ref[...], sp_ref[...]
    norm = 1.0 / length
    kre_ref[...] = (cphi * cpq[:, :HY_CH] + sphi * spq[:, :HY_CH]) * norm
    kim_ref[...] = (cphi * spq[:, HY_CH:] - sphi * cpq[:, HY_CH:]) * norm


def _hyena_filter_spectrum(length, w1, b1, w2, b2, w3, freq, dft):
    z, dec = _hyena_features(length)
    c_m, s_m, cphi, sphi = dft
    w1p = jnp.pad(w1, ((0, 128 - w1.shape[0]), (0, 0)))
    tf = min(512, length)
    const = lambda j: (0, 0)
    blk = lambda j: (j, 0)
    return pl.pallas_call(
        functools.partial(_hyfilt_kernel, length=length),
        grid=(length // tf,),
        in_specs=[
            pl.BlockSpec((length, 128), const),
            pl.BlockSpec((128, 64), const),
            pl.BlockSpec((1, 64), const),
            pl.BlockSpec((64, 64), const),
            pl.BlockSpec((1, 64), const),
            pl.BlockSpec((64, 2 * HY_CH), const),
            pl.BlockSpec((1, 64), const),
            pl.BlockSpec((length, 2 * HY_CH), const),
            pl.BlockSpec((tf, length), blk),
            pl.BlockSpec((tf, length), blk),
            pl.BlockSpec((tf, 1), blk),
            pl.BlockSpec((tf, 1), blk),
        ],
        out_specs=[pl.BlockSpec((tf, HY_CH), blk)] * 2,
        out_shape=[jax.ShapeDtypeStruct((length, HY_CH), F32)] * 2,
        scratch_shapes=[pltpu.VMEM((length, 2 * HY_CH), BF16)],
        compiler_params=_cp("arbitrary"),
        name=f"hyena_filter_{length}",
    )(z, w1p, b1[None, :], w2, b2[None, :], w3, freq[None, :], dec, c_m, s_m, cphi, sphi)


def _short_conv3(x_ref, cw):
    x = x_ref[...].astype(F32)
    return _shift_rows(x, -1) * cw[0:1] + x * cw[1:2] + _shift_rows(x, 1) * cw[2:3]


def _hyena_kernel(p0_ref, p1_ref, p2_ref, cw_ref, bias_ref, c_ref, s_ref, kre_ref, kim_ref, o_ref):
    cw = cw_ref[...]
    x0 = _short_conv3(p0_ref, cw[:, :HY_CH])
    k = _short_conv3(p1_ref, cw[:, HY_CH:2 * HY_CH]) * _short_conv3(p2_ref, cw[:, 2 * HY_CH:])
    kb = k.astype(BF16)
    a = _dot(c_ref[...], kb)
    b = _dot(s_ref[...], kb)
    kre, kim = kre_ref[...], kim_ref[...]
    yre = (a * kre + b * kim).astype(BF16)
    yim = (b * kre - a * kim).astype(BF16)
    y = _dot(c_ref[...], yre) + _dot(s_ref[...], yim)
    o_ref[...] = (x0 * (y + k * bias_ref[...])).astype(o_ref.dtype)


def _hyena(p_hy, length, blk0, conv_w, bias, dft, kre, kim):
    c_m, s_m = dft[0], dft[1]
    const = lambda b: (0, 0)
    once = pl.Buffered(1)
    in_specs = [
        pl.BlockSpec((length, HY_CH), lambda b: (blk0 + b, 0)),
        pl.BlockSpec((length, HY_CH), lambda b: (blk0 + b, 1)),
        pl.BlockSpec((length, HY_CH), lambda b: (blk0 + b, 2)),
        pl.BlockSpec((3, 768), const),
        pl.BlockSpec((1, HY_CH), const),
        pl.BlockSpec((length, length), const, pipeline_mode=once),
        pl.BlockSpec((length, length), const, pipeline_mode=once),
        pl.BlockSpec((length, HY_CH), const, pipeline_mode=once),
        pl.BlockSpec((length, HY_CH), const, pipeline_mode=once),
    ]
    return pl.pallas_call(
        _hyena_kernel,
        grid=(B,),
        in_specs=in_specs,
        out_specs=pl.BlockSpec((length, HY_CH), lambda b: (b, 0)),
        out_shape=jax.ShapeDtypeStruct((B * length, HY_CH), BF16),
        compiler_params=_cp("arbitrary"),
        name=f"hyena_{length}",
    )(p_hy, p_hy, p_hy, conv_w, bias[None, :], c_m, s_m, kre, kim)


def _ones_half(vwin, half):
    lane = lax.broadcasted_iota(jnp.int32, vwin.shape, 1)
    keep = (lane < HD) if half == 0 else (lane >= HD)
    return jnp.where(keep, vwin, jnp.ones_like(vwin))


def _softmax_pv(heads):
    probs, maxes = [], []
    for parts, sink, _ in heads:
        m = None
        for s, _ in parts:
            mi = jnp.max(s, axis=-1, keepdims=True)
            m = mi if m is None else jnp.maximum(m, mi)
        if sink is not None:
            m = jnp.maximum(m, sink)
        probs.append([jnp.exp((s - m).astype(BF16)) for s, _ in parts])
        maxes.append(m)
    outs = []
    for (parts, sink, half), es, m in zip(heads, probs, maxes):
        acc = None
        for e, (_, v) in zip(es, parts):
            o = _dot(e, v)
            acc = o if acc is None else acc + o
        den = acc[:, HD:HD + 1] if half == 0 else acc[:, 0:1]
        if sink is not None:
            den = den + jnp.exp(sink - m)
        outs.append(acc / den)
    return outs


def _wa_kernel(sink_ref, q_ref, k_ref, v_ref, kc_ref, vc_ref, o_ref):
    kc, vc = kc_ref[...], vc_ref[...]
    vcs = [_ones_half(vc, hk) for hk in range(2)]
    rr = lax.broadcasted_iota(jnp.int32, (256, 384), 0)
    r1 = lax.broadcasted_iota(jnp.int32, (256, 1), 0)
    heads = []
    for j in range(WA_BLKS):
        n = pl.program_id(1) * WA_BLKS + j
        start = pl.multiple_of(jnp.clip((n - 1) * 128, 0, S - 384), 128)
        kw = k_ref[pl.ds(start, 384), :]
        vw = v_ref[pl.ds(start, 384), :]
        q = q_ref[j * 128:(j + 1) * 128, :] * SCALE
        qpos = n * 128 + jnp.where(rr >= 128, rr - 128, rr)
        kpos = start + lax.broadcasted_iota(jnp.int32, (256, 384), 1)
        valid = jnp.abs(qpos - kpos) <= WINDOW
        for hk in range(2):
            q2 = jnp.concatenate([q[:, (2 * hk) * HD:(2 * hk + 1) * HD],
                                  q[:, (2 * hk + 1) * HD:(2 * hk + 2) * HD]], axis=0)
            hs = slice(hk * HD, (hk + 1) * HD)
            s_loc = jnp.where(valid, _dot_nt(q2, kw[:, hs]), NEG)
            s_ctx = _dot_nt(q2, kc[:, hs])
            sink = jnp.where(r1 >= 128, sink_ref[2 * hk + 1], sink_ref[2 * hk])
            heads.append(([(s_ctx, vcs[hk]), (s_loc, _ones_half(vw, hk))], sink, hk))
    for i, res in enumerate(_softmax_pv(heads)):
        j, hk = divmod(i, 2)
        o = res[:, hk * HD:(hk + 1) * HD]
        rows = slice(j * 128, (j + 1) * 128)
        o_ref[rows, (2 * hk) * HD:(2 * hk + 1) * HD] = o[:128].astype(o_ref.dtype)
        o_ref[rows, (2 * hk + 1) * HD:(2 * hk + 2) * HD] = o[128:].astype(o_ref.dtype)


def _window_attention(p_wa, sink):
    nb = S // (128 * WA_BLKS)
    return pl.pallas_call(
        _wa_kernel,
        grid=(B, nb),
        in_specs=[
            pl.BlockSpec(memory_space=pltpu.SMEM),
            pl.BlockSpec((128 * WA_BLKS, 256), lambda b, n: (b * nb + n, 0)),
            pl.BlockSpec((S, 128), lambda b, n: (b, 2)),
            pl.BlockSpec((S, 128), lambda b, n: (b, 3)),
            pl.BlockSpec((LC, 128), lambda b, n: (CTX_BLK0 + b, 2)),
            pl.BlockSpec((LC, 128), lambda b, n: (CTX_BLK0 + b, 3)),
        ],
        out_specs=pl.BlockSpec((128 * WA_BLKS, 256), lambda b, n: (b * nb + n, 0)),
        out_shape=jax.ShapeDtypeStruct((NL, 256), BF16),
        compiler_params=_cp("arbitrary", "arbitrary"),
        name="window_attn",
    )(sink, p_wa, p_wa, p_wa, p_wa, p_wa)


def _na_kernel(q_ref, k_ref, v_ref, kc_ref, vc_ref, bias_ref, o_ref):
    kc, vc = kc_ref[...], vc_ref[...]
    vcs = [_ones_half(vc[:, (h // 2) * 128:(h // 2 + 1) * 128], h % 2) for h in range(4)]
    heads = []
    for j in range(NA_ROWS):
        r = pl.program_id(1) * NA_ROWS + j
        first = jnp.clip(r - NA_KR // 2, 0, S // GW - NA_KR)
        start = pl.multiple_of(first * GW, GW)
        off = first - r + NA_KR - 1
        kw = k_ref[pl.ds(start, NA_KR * GW), :]
        vw = v_ref[pl.ds(start, NA_KR * GW), :]
        q = q_ref[j * GW:(j + 1) * GW, :] * SCALE
        for h in range(4):
            hs = slice(h * HD, (h + 1) * HD)
            win = slice((h // 2) * 128, (h // 2 + 1) * 128)
            s_loc = _dot_nt(q[:, hs], kw[:, hs]) + bias_ref[h, off]
            s_ctx = _dot_nt(q[:, hs], kc[:, hs])
            heads.append(([(s_ctx, vcs[h]), (s_loc, _ones_half(vw[:, win], h % 2))], None, h % 2))
    outs = _softmax_pv(heads)
    lane = lax.broadcasted_iota(jnp.int32, (GW, 128), 1)
    for j in range(NA_ROWS):
        for hp in range(2):
            pair = jnp.where(lane < HD, outs[4 * j + 2 * hp], outs[4 * j + 2 * hp + 1])
            o_ref[j * GW:(j + 1) * GW, hp * 128:(hp + 1) * 128] = pair.astype(o_ref.dtype)


def _na_bias_kernel(rpb_ref, sel_ref, o_ref):
    o_ref[0] = _dot(rpb_ref[0], sel_ref[...], HI)


def _na_bias_table(rpb):
    qc = jnp.arange(GW)[:, None]
    kcol = jnp.arange(GW)[None, :]
    ci = jnp.clip(kcol - qc + NA_KC - 1, 0, 2 * NA_KC - 2).reshape(1, GW * GW)
    sel = (jnp.arange(128)[:, None] == ci).astype(F32)
    n_ri = 2 * NA_KR - 1
    rpb_p = jnp.pad(rpb.astype(F32), ((0, 0), (0, 16 - n_ri), (0, 128 - (2 * NA_KC - 1))))
    per_row = pl.pallas_call(
        _na_bias_kernel,
        grid=(4,),
        in_specs=[pl.BlockSpec((1, 16, 128), lambda h: (h, 0, 0)), pl.BlockSpec((128, GW * GW), lambda h: (0, 0))],
        out_specs=pl.BlockSpec((1, 16, GW * GW), lambda h: (h, 0, 0)),
        out_shape=jax.ShapeDtypeStruct((4, 16, GW * GW), F32),
        compiler_params=_cp("arbitrary"),
        name="na_bias",
    )(rpb_p, sel).reshape(4, 16, GW, GW)
    tbl = jnp.stack([per_row[:, o:o + NA_KR] for o in range(NA_KR)], axis=1)
    tbl = jnp.transpose(tbl, (0, 1, 3, 2, 4))
    c_start = jnp.clip(qc - NA_KC // 2, 0, GW - NA_KC)
    ok = (kcol >= c_start) & (kcol < c_start + NA_KC)
    tbl = jnp.where(ok[None, None, :, None, :], tbl, NEG)
    return tbl.reshape(4, NA_KR, GW, NA_KR * GW)


def _neighborhood_attention(p_na, bias_tbl):
    steps = S // GW // NA_ROWS
    return pl.pallas_call(
        _na_kernel,
        grid=(B, steps),
        in_specs=[
            pl.BlockSpec((NA_ROWS * GW, 256), lambda b, r: (b * steps + r, 0)),
            pl.BlockSpec((S, 256), lambda b, r: (b, 1)),
            pl.BlockSpec((S, 256), lambda b, r: (b, 2)),
            pl.BlockSpec((LC, 256), lambda b, r: (CTX_BLK0 + b, 1)),
            pl.BlockSpec((LC, 256), lambda b, r: (CTX_BLK0 + b, 2)),
            pl.BlockSpec((4, NA_KR, GW, NA_KR * GW), lambda b, r: (0, 0, 0, 0)),
        ],
        out_specs=pl.BlockSpec((NA_ROWS * GW, 256), lambda b, r: (b * steps + r, 0)),
        out_shape=jax.ShapeDtypeStruct((NL, 256), BF16),
        compiler_params=_cp("arbitrary", "arbitrary"),
        name="neighborhood_attn",
    )(p_na, p_na, p_na, p_na, p_na, bias_tbl)


def _ctx_attn_kernel(sink_ref, qw_ref, kw_ref, vw_ref, qn_ref, kn_ref, vn_ref, owa_ref, ona_ref):
    q, k, v = qw_ref[...] * SCALE, kw_ref[...], vw_ref[...]
    r1 = lax.broadcasted_iota(jnp.int32, (2 * LC, 1), 0)
    heads = []
    for hk in range(2):
        q2 = jnp.concatenate([q[:, (2 * hk) * HD:(2 * hk + 1) * HD],
                              q[:, (2 * hk + 1) * HD:(2 * hk + 2) * HD]], axis=0)
        hs = slice(hk * HD, (hk + 1) * HD)
        sink = jnp.where(r1 >= LC, sink_ref[2 * hk + 1], sink_ref[2 * hk])
        heads.append(([(_dot_nt(q2, k[:, hs]), _ones_half(v, hk))], sink, hk))
    for hk, res in enumerate(_softmax_pv(heads)):
        o = res[:, hk * HD:(hk + 1) * HD]
        owa_ref[:, (2 * hk) * HD:(2 * hk + 1) * HD] = o[:LC].astype(owa_ref.dtype)
        owa_ref[:, (2 * hk + 1) * HD:(2 * hk + 2) * HD] = o[LC:].astype(owa_ref.dtype)
    q, k, v = qn_ref[...] * SCALE, kn_ref[...], vn_ref[...]
    heads = []
    for h in range(4):
        hs = slice(h * HD, (h + 1) * HD)
        win = slice((h // 2) * 128, (h // 2 + 1) * 128)
        heads.append(([(_dot_nt(q[:, hs], k[:, hs]), _ones_half(v[:, win], h % 2))], None, h % 2))
    outs = _softmax_pv(heads)
    lane = lax.broadcasted_iota(jnp.int32, (LC, 128), 1)
    for hp in range(2):
        pair = jnp.where(lane < HD, outs[2 * hp], outs[2 * hp + 1])
        ona_ref[:, hp * 128:(hp + 1) * 128] = pair.astype(ona_ref.dtype)


def _ctx_attention(p_wa, p_na, sink):
    blk = lambda c: (lambda b: (CTX_BLK0 + b, c))
    return pl.pallas_call(
        _ctx_attn_kernel,
        grid=(B,),
        in_specs=[
            pl.BlockSpec(memory_space=pltpu.SMEM),
            pl.BlockSpec((LC, 256), blk(0)),
            pl.BlockSpec((LC, 128), blk(2)),
            pl.BlockSpec((LC, 128), blk(3)),
            pl.BlockSpec((LC, 256), blk(0)),
            pl.BlockSpec((LC, 256), blk(1)),
            pl.BlockSpec((LC, 256), blk(2)),
        ],
        out_specs=[pl.BlockSpec((LC, 256), lambda b: (b, 0))] * 2,
        out_shape=[jax.ShapeDtypeStruct((NC, 256), BF16)] * 2,
        compiler_params=_cp("arbitrary"),
        name="ctx_attn",
    )(sink, p_wa, p_wa, p_wa, p_na, p_na, p_na)


def _gdn_prep_seq(p_ref, ab_ref, cw, alog, dtb, qkv_ref, gb_ref, row0, length):
    x = p_ref[:, :768].astype(F32)
    u = (_shift_rows(x, -2) * cw[0:1] + _shift_rows(x, -1) * cw[1:2] + x * cw[2:3]
         + _shift_rows(x, 1) * cw[3:4])
    u = _silu(u)
    rows = slice(row0, row0 + length)
    for j in range(8):
        sl = slice(j * HD, (j + 1) * HD)
        xs = u[:, sl]
        nrm = lax.rsqrt(jnp.sum(xs * xs, axis=-1, keepdims=True) + EPS)
        qkv_ref[0, rows, sl] = xs * (nrm * SCALE if j < 4 else nrm)
    qkv_ref[0, rows, 512:768] = u[:, 512:768]
    ab = ab_ref[...]
    g = -jnp.exp(alog) * _softplus(ab + dtb)
    lane = lax.broadcasted_iota(jnp.int32, ab.shape, 1)
    gb_ref[0, rows, :] = jnp.where(lane < 8, g, _sigmoid(ab))


def _gdn_prep_kernel(pl_ref, abl_ref, pc_ref, abc_ref, cw_ref, alog_ref, dtb_ref, qkv_ref, gb_ref):
    cw, alog, dtb = cw_ref[...], alog_ref[...], dtb_ref[...]
    _gdn_prep_seq(pl_ref, abl_ref, cw, alog, dtb, qkv_ref, gb_ref, 0, S)
    _gdn_prep_seq(pc_ref, abc_ref, cw, alog, dtb, qkv_ref, gb_ref, S, LC)


def _gdn_prep(p_gd, p_ab, conv_w, a_log, dt_bias):
    const = lambda b: (0, 0)
    alog = jnp.pad(a_log.reshape(1, 8), ((0, 0), (0, 120)))
    dtb = jnp.pad(dt_bias.reshape(1, 8), ((0, 0), (0, 120)))
    return pl.pallas_call(
        _gdn_prep_kernel,
        grid=(B,),
        in_specs=[
            pl.BlockSpec((S, 1024), lambda b: (b, 0)),
            pl.BlockSpec((S, 128), lambda b: (b, 0)),
            pl.BlockSpec((LC, 1024), lambda b: (CTX_BLK0 + b, 0)),
            pl.BlockSpec((LC, 128), lambda b: (CTX_BLK0 + b, 0)),
            pl.BlockSpec((4, 768), const),
            pl.BlockSpec((1, 128), const),
            pl.BlockSpec((1, 128), const),
        ],
        out_specs=[pl.BlockSpec((1, S + LC, 768), lambda b: (b, 0, 0)),
                   pl.BlockSpec((1, S + LC, 128), lambda b: (b, 0, 0))],
        out_shape=[jax.ShapeDtypeStruct((B, S + LC, 768), F32), jax.ShapeDtypeStruct((B, S + LC, 128), F32)],
        compiler_params=_cp("arbitrary"),
        name="gdn_prep",
    )(p_gd, p_ab, p_gd, p_ab, conv_w, alog, dtb)


def _gdn_chains(xs, gbs, cums, cum_ts, states, masks):
    chains = [(g, h) for g in range(len(xs)) for h in range(4)]
    pre = []
    for g, h in chains:
        dirn = g % 2
        x, gb = xs[g], gbs[g]
        tri, strict = masks[dirn]
        col = dirn * 4 + h
        qh = x[:, h * HD:(h + 1) * HD]
        kh = x[:, 256 + h * HD:256 + (h + 1) * HD]
        vh = x[:, 512 + h * HD:512 + (h + 1) * HD]
        gc = cums[g][:, col:col + 1]
        gc_row = cum_ts[g][col:col + 1, :]
        beta = gb[:, 8 + col:9 + col]
        decay = jnp.exp(jnp.where(tri, gc - gc_row, NEG))
        kb = kh * beta
        eg = jnp.exp(gc)
        g_last = gc[0:1, :] if dirn else gc[CHUNK - 1:CHUNK, :]
        pre.append(dict(strict=strict, decay=decay, kbf=kb.astype(BF16), khf=kh.astype(BF16), qf=qh.astype(BF16),
                        xw=jnp.concatenate([vh * beta, kb * eg], axis=1), qdec=(qh * eg).astype(BF16),
                        kdec=(kh * jnp.exp(g_last - gc)).astype(BF16), gl=jnp.exp(g_last)))
    n = len(chains)
    gram = [_dot_nt(p["kbf"], p["khf"]) for p in pre]
    attn = [_dot_nt(p["qf"], p["khf"]) for p in pre]
    lmat = [jnp.where(p["strict"], g * p["decay"], 0.0) for p, g in zip(pre, gram)]
    attn = [(a * p["decay"]).astype(BF16) for p, a in zip(pre, attn)]
    ii = lax.broadcasted_iota(jnp.int32, (CHUNK, CHUNK), 0)
    jj = lax.broadcasted_iota(jnp.int32, (CHUNK, CHUNK), 1)

    def merged_off_blocks(log_s):
        return ((ii >> (log_s + 1)) == (jj >> (log_s + 1))) & ((ii >> log_s) != (jj >> log_s))

    eye = (ii == jj).astype(F32)
    dinv = [eye - jnp.where(merged_off_blocks(0), lm, 0.0) for lm in lmat]
    for log_s in range(1, 6):
        m = merged_off_blocks(log_s)
        dinv_b = [d.astype(BF16) for d in dinv]
        ld = [_dot(jnp.where(m, lmat[c], 0.0).astype(BF16), dinv_b[c]) for c in range(n)]
        upd = [_dot(dinv_b[c], ld[c].astype(BF16)) for c in range(n)]
        dinv = [dinv[c] - upd[c] for c in range(n)]
    xw = [_dot(dinv[c].astype(BF16), pre[c]["xw"].astype(BF16)) for c in range(n)]
    sts = [states[g][h] for g, h in chains]
    stb = [s.astype(BF16) for s in sts]
    ws = [_dot(xw[c][:, HD:].astype(BF16), stb[c]) for c in range(n)]
    qs = [_dot(pre[c]["qdec"], stb[c]) for c in range(n)]
    u_new = [(xw[c][:, :HD] - ws[c]).astype(BF16) for c in range(n)]
    au = [_dot(attn[c], u_new[c]) for c in range(n)]
    ku = [_dot_tn(pre[c]["kdec"], u_new[c]) for c in range(n)]
    outs = [qs[c] + au[c] for c in range(n)]
    new_states = [sts[c] * pre[c]["gl"] + ku[c] for c in range(n)]
    return outs, new_states


def _gdn_kernel(x0_ref, gb0_ref, x1_ref, gb1_ref, o0_ref, o1_ref, state_ref):
    @pl.when(pl.program_id(0) == 0)
    def _():
        state_ref[...] = jnp.zeros_like(state_ref)

    ii = lax.broadcasted_iota(jnp.int32, (CHUNK, CHUNK), 0)
    jj = lax.broadcasted_iota(jnp.int32, (CHUNK, CHUNK), 1)
    masks = ((jj <= ii, jj < ii), (jj >= ii, jj > ii))
    ones_tri = ((jj <= ii).astype(F32), (jj >= ii).astype(F32))

    def body(bp, carry):
        bs = [bp * GDN_SEQS + i for i in range(GDN_SEQS)]
        states = [[state_ref[b, dirn, h] for h in range(4)] for b in bs for dirn in range(2)]
        xs = [r[b] for b in bs for r in (x0_ref, x1_ref)]
        gbs = [r[b] for b in bs for r in (gb0_ref, gb1_ref)]
        cums = [_dot(ones_tri[g % 2], gbs[g], HI) for g in range(2 * GDN_SEQS)]
        cum_ts = [cm.T for cm in cums]
        outs, new_states = _gdn_chains(xs, gbs, cums, cum_ts, states, masks)
        for c, st in enumerate(new_states):
            state_ref[bs[c // 8], (c // 4) % 2, c % 4] = st
        for i, b in enumerate(bs):
            o0_ref[b] = jnp.concatenate(outs[8 * i:8 * i + 4], axis=1)
            o1_ref[b] = jnp.concatenate(outs[8 * i + 4:8 * i + 8], axis=1)
        return carry

    lax.fori_loop(0, B // GDN_SEQS, body, 0)


def _gdn_scan(qkv, gb):
    nc = (S + LC) // CHUNK
    fwd = lambda c: (0, (c + S // CHUNK) % nc, 0)
    bwd = lambda c: (0, nc - 1 - c, 0)
    return pl.pallas_call(
        _gdn_kernel,
        grid=(nc,),
        in_specs=[
            pl.BlockSpec((B, CHUNK, 768), fwd),
            pl.BlockSpec((B, CHUNK, 128), fwd),
            pl.BlockSpec((B, CHUNK, 768), bwd),
            pl.BlockSpec((B, CHUNK, 128), bwd),
        ],
        out_specs=[pl.BlockSpec((B, CHUNK, 256), fwd), pl.BlockSpec((B, CHUNK, 256), bwd)],
        out_shape=[jax.ShapeDtypeStruct((B, S + LC, 256), F32)] * 2,
        scratch_shapes=[pltpu.VMEM((B, 2, 4, HD, HD), F32)],
        compiler_params=_cp("arbitrary"),
        name="gdn_scan",
    )(qkv, gb, qkv, gb)


def _gdn_gated_norm(o, gate, gain):
    gi = lax.broadcasted_iota(jnp.int32, (256, 256), 0) // HD
    gj = lax.broadcasted_iota(jnp.int32, (256, 256), 1) // HD
    group_mean = jnp.where(gi == gj, 1.0 / HD, 0.0).astype(BF16)
    sq = o * o
    sq_hi = sq.astype(BF16)
    sq_lo = (sq - sq_hi.astype(F32)).astype(BF16)
    ms = _dot(sq_hi, group_mean) + _dot(sq_lo, group_mean)
    return o * lax.rsqrt(ms + EPS) * gain * _silu(gate.astype(F32))


def _pack_bf16_pairs(x):
    n = x.shape[1] // 2
    lo = pltpu.bitcast(x[:, :n].astype(BF16).astype(F32), jnp.uint32)
    hi = pltpu.bitcast(x[:, n:].astype(BF16).astype(F32), jnp.uint32)
    return hi | (lo >> 16)


def _unpack_bf16_pairs(w):
    lo = pltpu.bitcast(w << 16, F32)
    hi = pltpu.bitcast(w & jnp.uint32(0xFFFF0000), F32)
    return jnp.concatenate([lo, hi], axis=1)


def _outproj_kernel(xl_ref, xc_ref, mod_ref, g2_ref, yh_ref, yw_ref, yn_ref, o0_ref, o1_ref, gate_ref, ng_ref, wo_ref,
                    wr_ref, eb_ref, xo_ref, h2_ref, h2p_ref, idx_ref, tw_ref, rank_ref, cnt_ref, carry_ref):
    @pl.when(pl.program_id(0) == 0)
    def _():
        carry_ref[...] = jnp.zeros_like(carry_ref)

    m = mod_ref[0]
    y_gd = _gdn_gated_norm(o0_ref[0] + o1_ref[0], gate_ref[...], ng_ref[...]).astype(BF16)
    acc = (_dot(yh_ref[...], wo_ref[0:256, :]) + _dot(yw_ref[...], wo_ref[256:512, :])
           + _dot(yn_ref[...], wo_ref[512:768, :]) + _dot(y_gd, wo_ref[768:1024, :]))
    x = _row_tile(xl_ref, xc_ref) + m[2:3] * acc
    xo_ref[...] = x
    h2 = _modulated_norm(x, g2_ref[...], m[3:4], m[4:5])
    h2_ref[...] = h2.astype(h2_ref.dtype)
    h2p_ref[...] = _pack_bf16_pairs(h2)
    scores = _sigmoid(_dot(h2, wr_ref[...], HI))
    lane = lax.broadcasted_iota(jnp.int32, scores.shape, 1)
    sel = jnp.where(lane < N_EXP, scores + eb_ref[...], -jnp.inf)
    lane_f = lane.astype(F32)
    idx_out = jnp.zeros(scores.shape, F32)
    s_out = jnp.zeros(scores.shape, F32)
    hits = []
    for kk in range(TOP_K):
        mx = jnp.max(sel, axis=-1, keepdims=True)
        idx = jnp.min(jnp.where(sel == mx, lane_f, 128.0), axis=-1, keepdims=True)
        hit = lane_f == idx
        sk = jnp.sum(jnp.where(hit, scores, 0.0), axis=-1, keepdims=True)
        sel = jnp.where(hit, -jnp.inf, sel)
        idx_out = jnp.where(lane == kk, idx, idx_out)
        s_out = jnp.where(lane == kk, sk, s_out)
        hits.append(hit)
    tot = jnp.sum(s_out, axis=-1, keepdims=True)
    tw_ref[...] = s_out / tot * ROUTED_SCALE
    cnt = jnp.zeros(scores.shape, F32)
    for hit in hits:
        cnt = cnt + jnp.where(hit, 1.0, 0.0)
    ri = lax.broadcasted_iota(jnp.int32, (TM, TM), 0)
    rj = lax.broadcasted_iota(jnp.int32, (TM, TM), 1)
    before = _dot(jnp.where(rj < ri, 1.0, 0.0).astype(BF16), cnt.astype(BF16)) + carry_ref[...]
    rank_out = jnp.zeros(scores.shape, F32)
    for kk, hit in enumerate(hits):
        rk = jnp.sum(jnp.where(hit, before, 0.0), axis=-1, keepdims=True)
        rank_out = jnp.where(lane == kk, rk, rank_out)
    idx_ref[...] = idx_out.T[:8].astype(jnp.int32)
    rank_ref[...] = rank_out.T[:8].astype(jnp.int32)
    carry_ref[...] = carry_ref[...] + jnp.sum(cnt, axis=0, keepdims=True)
    cnt_ref[...] = carry_ref[...].astype(jnp.int32)


def _outproj(x_lat, x_ctx, mod, l, g2, ys, gdn, wo, wr, eb, n_tiles):
    row = lambda i: (i, 0)
    const = lambda i: (0, 0)
    lat_tiles = S // TM
    seq = lambda i: (jnp.where(i < NT_LAT, i // lat_tiles, i - NT_LAT), jnp.where(i < NT_LAT, i % lat_tiles, lat_tiles), 0)
    rows = n_tiles * TM
    o0, o1, p_gd, norm_g = gdn
    return pl.pallas_call(
        _outproj_kernel,
        grid=(n_tiles,),
        in_specs=_row_tile_specs() + [
            pl.BlockSpec((1, 6, D), lambda i: (l * 16 + (i * TM) // S, 0, 0)),
            pl.BlockSpec((1, D), const),
            pl.BlockSpec((TM, 256), row),
            pl.BlockSpec((TM, 256), row),
            pl.BlockSpec((TM, 256), row),
            pl.BlockSpec((1, TM, 256), seq),
            pl.BlockSpec((1, TM, 256), seq),
            pl.BlockSpec((TM, 256), lambda i: (i, 3)),
            pl.BlockSpec((1, 256), const),
            pl.BlockSpec((D, D), const),
            pl.BlockSpec((D, 128), const),
            pl.BlockSpec((1, 128), const),
        ],
        out_specs=[
            pl.BlockSpec((TM, D), row),
            pl.BlockSpec((TM, D), row),
            pl.BlockSpec((TM, D // 2), row),
            pl.BlockSpec((8, TM), row),
            pl.BlockSpec((TM, 128), row),
            pl.BlockSpec((8, TM), row),
            pl.BlockSpec((1, 128), const),
        ],
        out_shape=[
            jax.ShapeDtypeStruct((rows, D), F32),
            jax.ShapeDtypeStruct((rows, D), BF16),
            jax.ShapeDtypeStruct((rows, D // 2), jnp.uint32),
            jax.ShapeDtypeStruct((n_tiles * 8, TM), jnp.int32),
            jax.ShapeDtypeStruct((rows, 128), F32),
            jax.ShapeDtypeStruct((n_tiles * 8, TM), jnp.int32),
            jax.ShapeDtypeStruct((1, 128), jnp.int32),
        ],
        scratch_shapes=[pltpu.VMEM((1, 128), F32)],
        compiler_params=_cp("arbitrary"),
        name="outproj_router",
    )(x_lat, x_ctx, mod, g2, *ys, o0, o1, p_gd, jnp.tile(norm_g, 4)[None, :], wo, wr, eb)


ASG_TILE = TM * TOP_K


def _segment_tables(counts, n_blocks):
    counts = counts[0, :N_EXP]
    padded = (counts + MOE_BLK - 1) // MOE_BLK * MOE_BLK
    pad_end = jnp.cumsum(padded)
    seg_start = pad_end - padded
    starts = jnp.arange(n_blocks, dtype=jnp.int32) * MOE_BLK
    blk_e = jnp.minimum(jnp.sum((pad_end[None, :] <= starts[:, None]).astype(jnp.int32), axis=1), N_EXP - 1)
    seg = jnp.concatenate([seg_start, seg_start + counts, pad_end[-1:]])
    blk = jnp.concatenate([blk_e, pad_end[-1:] // MOE_BLK])
    return seg.astype(jnp.int32), blk.astype(jnp.int32)


def _row_copy_wait(shape_ref, dst_ref, sem):
    pltpu.make_async_copy(shape_ref, dst_ref, sem).wait()


def _zero_fill_rows(zero_ref, xb_ref, start, length, sem):
    plan = []
    aligned = (start + 7) & -8
    for r in range(7):
        plan.append(((start + r < aligned) & (r < length),
                     pltpu.make_async_copy(zero_ref.at[pl.ds(0, 1)], xb_ref.at[pl.ds(start + r, 1)], sem)))
    rest = start + length - aligned
    off = aligned
    for bit in reversed(range(3, MOE_BLK.bit_length() - 1)):
        size = 1 << bit
        take = (rest & size) != 0
        dst = xb_ref.at[pl.ds(pl.multiple_of(off, 8), size)]
        plan.append((take, pltpu.make_async_copy(zero_ref.at[pl.ds(0, size)], dst, sem)))
        off = off + jnp.where(take, size, 0)
    for take, copy in plan:
        pl.when(take)(copy.start)
    for take, copy in plan:
        pl.when(take)(copy.wait)


def _dest_kernel(seg_ref, idx_ref, rank_ref, o_ref):
    idx = idx_ref[...]
    first = jnp.zeros(idx.shape, jnp.int32)
    for e in range(N_EXP):
        first = jnp.where(idx == e, seg_ref[e], first)
    o_ref[...] = first + rank_ref[...]


def _dest_rows(seg, top_idx, rank):
    n_tiles = top_idx.shape[0] // 8
    blk = lambda i, seg: (i, 0)
    grid_spec = pltpu.PrefetchScalarGridSpec(
        num_scalar_prefetch=1,
        grid=(n_tiles // 8,),
        in_specs=[pl.BlockSpec((64, TM), blk), pl.BlockSpec((64, TM), blk)],
        out_specs=pl.BlockSpec((64, TM), blk),
    )
    return pl.pallas_call(
        _dest_kernel,
        grid_spec=grid_spec,
        out_shape=jax.ShapeDtypeStruct(top_idx.shape, jnp.int32),
        compiler_params=_cp("arbitrary"),
        name="moe_dest_rows",
    )(seg, top_idx, rank)


def _dispatch_kernel(seg_ref, dest_ref, h2p_ref, xb_ref, zero_ref, ring_ref, sems, zsem):
    i = pl.program_id(0)
    sem = sems.at[i % 2]
    tile_ref = ring_ref.at[i % 2]
    tile_ref[...] = h2p_ref[...]

    @pl.when(i == 0)
    def _():
        zero_ref[...] = jnp.zeros_like(zero_ref)

    @pl.when(i < N_EXP)
    def _():
        first_pad = seg_ref[N_EXP + i]
        next_start = seg_ref[jnp.where(i == N_EXP - 1, 2 * N_EXP, i + 1)]
        _zero_fill_rows(zero_ref, xb_ref, first_pad, next_start - first_pad, zsem)

    tail_row = seg_ref[2 * N_EXP] + (i - N_EXP) * MOE_BLK

    @pl.when((i >= N_EXP) & (tail_row < xb_ref.shape[0]))
    def _():
        copy = pltpu.make_async_copy(zero_ref, xb_ref.at[pl.ds(pl.multiple_of(tail_row, MOE_BLK), MOE_BLK)], zsem)
        copy.start()
        copy.wait()

    def issue(t, carry):
        src = tile_ref.at[pl.ds(t, 1)]
        for k in range(TOP_K):
            pltpu.make_async_copy(src, xb_ref.at[pl.ds(dest_ref[k, t], 1)], sem).start(priority=k % 2)
        return carry

    lax.fori_loop(0, TM, issue, 0, unroll=True)

    @pl.when(i > 0)
    def _():
        for _ in range(TOP_K):
            _row_copy_wait(h2p_ref, xb_ref.at[pl.ds(0, TM)], sems.at[(i + 1) % 2])

    @pl.when(i == pl.num_programs(0) - 1)
    def _():
        for _ in range(TOP_K):
            _row_copy_wait(h2p_ref, xb_ref.at[pl.ds(0, TM)], sem)


def _dispatch_rows(seg, dest, h2p, n_rows):
    n_tiles = h2p.shape[0] // TM
    assert n_tiles >= 2 * N_EXP
    grid_spec = pltpu.PrefetchScalarGridSpec(
        num_scalar_prefetch=1,
        grid=(n_tiles,),
        in_specs=[
            pl.BlockSpec((8, TM), lambda i, seg: (i, 0), memory_space=pltpu.SMEM),
            pl.BlockSpec((TM, D // 2), lambda i, seg: (i, 0)),
        ],
        out_specs=pl.BlockSpec(memory_space=pl.ANY),
        scratch_shapes=[pltpu.VMEM((MOE_BLK, D // 2), jnp.uint32), pltpu.VMEM((2, TM, D // 2), jnp.uint32),
                        pltpu.SemaphoreType.DMA((2,)), pltpu.SemaphoreType.DMA(())],
    )
    return pl.pallas_call(
        _dispatch_kernel,
        grid_spec=grid_spec,
        out_shape=jax.ShapeDtypeStruct((n_rows, D // 2), jnp.uint32),
        compiler_params=_cp("arbitrary"),
        name="moe_dispatch",
    )(seg, dest, h2p)


def _moe_kernel(be_ref, x_ref, wg_ref, wu_ref, wd_ref, o_ref, wgb_ref, wub_ref, wdb_ref):
    i = pl.program_id(0)
    in_use = i < be_ref[pl.num_programs(0)]

    @pl.when(in_use & ((i == 0) | (be_ref[i] != be_ref[jnp.maximum(i - 1, 0)])))
    def _():
        wgb_ref[...] = wg_ref[0, 0].astype(BF16)
        wub_ref[...] = wu_ref[0, 0].astype(BF16)
        wdb_ref[...] = wd_ref[0, 0].astype(BF16)

    @pl.when(in_use)
    def _():
        x = _unpack_bf16_pairs(x_ref[...]).astype(BF16)
        g = _dot(x, wgb_ref[...])
        u = _dot(x, wub_ref[...])
        hid = (_silu(g) * u).astype(BF16)
        o_ref[...] = _pack_bf16_pairs(_dot(hid, wdb_ref[...]))

    @pl.when(jnp.logical_not(in_use))
    def _():
        o_ref[...] = jnp.zeros_like(o_ref)


def _moe_experts(blk_e, xb, l, wg, wu, wd):
    n_rows = xb.shape[0]
    grid_spec = pltpu.PrefetchScalarGridSpec(
        num_scalar_prefetch=1,
        grid=(n_rows // MOE_BLK,),
        in_specs=[
            pl.BlockSpec((MOE_BLK, D // 2), lambda i, be: (i, 0)),
            pl.BlockSpec((1, 1, D, D_EXP), lambda i, be: (l, be[i], 0, 0)),
            pl.BlockSpec((1, 1, D, D_EXP), lambda i, be: (l, be[i], 0, 0)),
            pl.BlockSpec((1, 1, D_EXP, D), lambda i, be: (l, be[i], 0, 0)),
        ],
        out_specs=pl.BlockSpec((MOE_BLK, D // 2), lambda i, be: (i, 0)),
        scratch_shapes=[pltpu.VMEM((D, D_EXP), BF16), pltpu.VMEM((D, D_EXP), BF16), pltpu.VMEM((D_EXP, D), BF16)],
    )
    return pl.pallas_call(
        _moe_kernel,
        grid_spec=grid_spec,
        out_shape=jax.ShapeDtypeStruct((n_rows, D // 2), jnp.uint32),
        compiler_params=_cp("arbitrary"),
        name="moe_experts",
    )(blk_e, xb, wg, wu, wd)


def _ffn_out_kernel(dest_ref, x_ref, mod_ref, h2_ref, tw_ref, wg_ref, wu_ref, wd_ref, nf_ref, yb_ref, *rest, final):
    if final:
        o_ref, gat_ref, sems = rest
    else:
        proj_in, o_ref, proj_out, gat_ref, sems = rest[:10], rest[10], rest[11:16], rest[16], rest[17]
    s = pl.program_id(0)

    def start_gathers(slot):
        buf = gat_ref.at[slot]

        def issue(t, carry):
            for k in range(TOP_K):
                copy = pltpu.make_async_copy(yb_ref.at[pl.ds(dest_ref[k, t], 1)], buf.at[pl.ds(k * TM + t, 1)],
                                             sems.at[slot])
                copy.start(priority=k % 2)
            return carry

        lax.fori_loop(0, TM, issue, 0, unroll=True)

    def wait_gathers(slot):
        for _ in range(TOP_K):
            _row_copy_wait(yb_ref.at[pl.ds(0, TM)], gat_ref.at[slot, pl.ds(0, TM)], sems.at[slot])

    pl.when(s == 0)(lambda: start_gathers(1))
    done = (s + 1) % 2
    wait_gathers(done)
    start_gathers(s % 2)
    m = mod_ref[0]
    h2 = h2_ref[...]
    hid = (_silu(_dot(h2, wg_ref[...])) * _dot(h2, wu_ref[...])).astype(BF16)
    shared = _dot(hid, wd_ref[...])
    buf = gat_ref.at[done]
    tw = tw_ref[...]
    routed = _unpack_bf16_pairs(buf[0:TM, :]) * tw[:, 0:1]
    for k in range(1, TOP_K):
        routed = routed + _unpack_bf16_pairs(buf[k * TM:(k + 1) * TM, :]) * tw[:, k:k + 1]
    x = x_ref[...] + m[5:6] * (routed + shared)
    if final:
        ms = jnp.mean(x * x, axis=-1, keepdims=True)
        x = x * lax.rsqrt(ms + EPS) * nf_ref[...]
    o_ref[...] = x
    if not final:
        _inproj_tile(x, *proj_in, *proj_out)
    pl.when(s == pl.num_programs(0) - 1)(lambda: wait_gathers(s % 2))


def _ffn_out(dest, xn, mod, l, h2, top_w, wg, wu, wd, norm_f, yb, n_tiles, next_proj=None):
    tile_of = lambda i: jnp.maximum(i - 1, 0)
    row = lambda i: (tile_of(i), 0)
    ahead = lambda i: (jnp.minimum(i, n_tiles - 1), 0)
    const = lambda i: (0, 0)
    in_specs = [
        pl.BlockSpec((8, TM), ahead, memory_space=pltpu.SMEM),
        pl.BlockSpec((TM, D), row),
        pl.BlockSpec((1, 6, D), lambda i: (l * 16 + (tile_of(i) * TM) // S, 0, 0)),
        pl.BlockSpec((TM, D), row),
        pl.BlockSpec((TM, 128), row),
        pl.BlockSpec((D, D_EXP), const),
        pl.BlockSpec((D, D_EXP), const),
        pl.BlockSpec((D_EXP, D), const),
        pl.BlockSpec((1, D), const),
        pl.BlockSpec(memory_space=pl.ANY),
    ]
    args = [dest, xn, mod, h2, top_w, wg, wu, wd, norm_f, yb]
    out_specs = [pl.BlockSpec((TM, D), row)]
    out_shape = [jax.ShapeDtypeStruct((n_tiles * TM, D), F32)]
    if next_proj is not None:
        g, ws, tabs = next_proj
        proj_in, proj_out, proj_shape = _inproj_specs(l + 1, tile_of)
        in_specs += proj_in
        args += [mod, g, *ws, *tabs]
        out_specs += proj_out
        out_shape += proj_shape
    return pl.pallas_call(
        functools.partial(_ffn_out_kernel, final=next_proj is None),
        grid=(n_tiles + 1,),
        in_specs=in_specs,
        out_specs=out_specs,
        scratch_shapes=[pltpu.VMEM((2, ASG_TILE, D // 2), jnp.uint32), pltpu.SemaphoreType.DMA((2,))],
        out_shape=out_shape,
        compiler_params=_cp("arbitrary"),
        name="shared_ffn_residual",
    )(*args)


def kernel(x, c, ctx, c_ctx, w_ada, b_ada, norm1, norm2, norm_f, w_in, w_out, hy_conv, hy_w1, hy_b1, hy_w2, hy_b2, hy_w3, hy_freq, hy_bias, wa_sink, na_rpb, gdn_conv, gdn_a_log, gdn_dt_bias, gdn_norm, moe_router, moe_bias, moe_gate, moe_up, moe_down, sh_gate, sh_up, sh_down):
    depth = w_ada.shape[0]
    x_lat, x_ctx = x.reshape(NL, D), ctx.reshape(NC, D)
    cvec = jnp.concatenate([c, c_ctx[None, :], jnp.zeros((16 - B - 1, D), F32)], axis=0)
    mod = _ada(cvec, w_ada, b_ada).reshape(depth * 16, 6, D)
    rope = _rope_tables()
    dft_lat = _dft_tables(S)
    dft_ctx = _dft_tables(LC)
    o1, o2, o3 = 768, 768 + 512, 768 + 512 + 768

    def proj_params(l):
        wl = w_in[l].astype(BF16)
        ws = (wl[:, :o1], wl[:, o1:o2], wl[:, o2:o3], wl[:, o3:o3 + 1024],
              jnp.pad(wl[:, o3 + 1024:], ((0, 0), (0, 128 - 16))))
        return norm1[l][None, :], ws, rope

    p_hy, p_wa, p_na, p_gd, p_ab = _inproj(x_lat, x_ctx, mod, 0, *proj_params(0))
    for l in range(depth):
        need_ctx = l < depth - 1
        n_tiles = NT_ALL if need_ctx else NT_LAT

        filt = (hy_w1[l], hy_b1[l], hy_w2[l], hy_b2[l], hy_w3[l], hy_freq[l])
        kre, kim = _hyena_filter_spectrum(S, *filt, dft_lat)
        y_hy = _hyena(p_hy, S, 0, hy_conv[l], hy_bias[l], dft_lat, kre, kim)
        y_wa = _window_attention(p_wa, wa_sink[l])
        y_na = _neighborhood_attention(p_na, _na_bias_table(na_rpb[l]))
        if need_ctx:
            kre_c, kim_c = _hyena_filter_spectrum(LC, *filt, dft_ctx)
            yc_hy = _hyena(p_hy, LC, CTX_BLK0, hy_conv[l], hy_bias[l], dft_ctx, kre_c, kim_c)
            yc_wa, yc_na = _ctx_attention(p_wa, p_na, wa_sink[l])
            y_hy = jnp.concatenate([y_hy, yc_hy], axis=0)
            y_wa = jnp.concatenate([y_wa, yc_wa], axis=0)
            y_na = jnp.concatenate([y_na, yc_na], axis=0)
        qkv, gb = _gdn_prep(p_gd, p_ab, gdn_conv[l], gdn_a_log[l], gdn_dt_bias[l])
        gd_fwd, gd_bwd = _gdn_scan(qkv, gb)

        wr = jnp.pad(moe_router[l], ((0, 0), (0, 128 - N_EXP)))
        eb = jnp.pad(moe_bias[l], (0, 128 - N_EXP))[None, :]
        xn, h2, h2p, top_idx, top_w, rank, counts = _outproj(
            x_lat, x_ctx, mod, l, norm2[l][None, :], (y_hy, y_wa, y_na), (gd_fwd, gd_bwd, p_gd, gdn_norm[l]),
            w_out[l].astype(BF16), wr, eb, n_tiles)
        n_asg = n_tiles * ASG_TILE
        n_blocks = n_asg // MOE_BLK + N_EXP
        seg, blk_e = _segment_tables(counts, n_blocks)
        dest = _dest_rows(seg, top_idx, rank)
        xb = _dispatch_rows(seg, dest, h2p, n_blocks * MOE_BLK)
        yb = _moe_experts(blk_e, xb, l, moe_gate, moe_up, moe_down)
        res = _ffn_out(dest, xn, mod, l, h2, top_w, sh_gate[l].astype(BF16), sh_up[l].astype(BF16),
                       sh_down[l].astype(BF16), norm_f[None, :], yb, n_tiles,
                       next_proj=proj_params(l + 1) if need_ctx else None)
        if need_ctx:
            x_lat, p_hy, p_wa, p_na, p_gd, p_ab = res
            x_ctx = x_lat[NL:] if l + 2 < depth else x_lat
        else:
            return res[0].reshape(B, S, D)
```

```python
import functools
import math

import jax
import jax.numpy as jnp
from jax import lax
from jax.experimental import pallas as pl
from jax.experimental.pallas import tpu as pltpu

F32 = jnp.float32
BF16 = jnp.bfloat16
HI = lax.Precision.HIGHEST

D = 1024
B = 8
S = 2048
LC = 256
GW = 64
HD = 64
NL = B * S
NC = B * LC
ROWS = NL + NC
TM = 256
NT_LAT = NL // TM
NT_ALL = ROWS // TM
CTX_BLK0 = NL // LC

HY_CH = 256
HY_BANDS = 16
HY_DECAY_MIN = -math.log(1e-2) / 1.5
HY_DECAY_MAX = -math.log(1e-2) / 0.3
WINDOW = 128
NA_KR = 8
NA_KC = 16
CHUNK = 64
N_EXP = 32
TOP_K = 4
D_EXP = 256
ROUTED_SCALE = 2.5
MOE_BLK = 512
WA_BLKS = 4
NA_ROWS = 8
GDN_SEQS = 4
EPS = 1e-6
NEG = -1e30
SCALE = HD ** -0.5
VMEM_LIMIT = 56 * 1024 * 1024


def _cp(*sem):
    return pltpu.CompilerParams(dimension_semantics=tuple(sem), vmem_limit_bytes=VMEM_LIMIT)


def _dot(a, b, precision=None):
    return jnp.dot(a, b, preferred_element_type=F32, precision=precision)


def _dot_nt(a, b, precision=None):
    return lax.dot_general(a, b, (((1,), (1,)), ((), ())), preferred_element_type=F32, precision=precision)


def _dot_tn(a, b, precision=None):
    return lax.dot_general(a, b, (((0,), (0,)), ((), ())), preferred_element_type=F32, precision=precision)


def _sigmoid(x):
    return 1.0 / (1.0 + jnp.exp(-x))


def _silu(x):
    return x * _sigmoid(x)


def _softplus(x):
    return jnp.maximum(x, 0.0) + jnp.log(1.0 + jnp.exp(-jnp.abs(x)))


def _shift_rows(x, d):
    n = x.shape[0]
    if d == 0:
        return x
    y = pltpu.roll(x, (-d) % n, axis=0)
    t = lax.broadcasted_iota(jnp.int32, x.shape, 0)
    ok = (t + d >= 0) & (t + d < n)
    return jnp.where(ok, y, 0.0)


def _ada_kernel(c_ref, w_ref, b_ref, o_ref):
    s = _silu(c_ref[...])
    o_ref[0] = _dot(s.astype(BF16), w_ref[0].astype(BF16)) + b_ref[0]


def _ada(cvec, w_ada, b_ada):
    nl = w_ada.shape[0]
    tn = 1536
    return pl.pallas_call(
        _ada_kernel,
        grid=(nl, 6 * D // tn),
        in_specs=[
            pl.BlockSpec((16, D), lambda l, j: (0, 0)),
            pl.BlockSpec((1, D, tn), lambda l, j: (l, 0, j)),
            pl.BlockSpec((1, 1, tn), lambda l, j: (l, 0, j)),
        ],
        out_specs=pl.BlockSpec((1, 16, tn), lambda l, j: (l, 0, j)),
        out_shape=jax.ShapeDtypeStruct((nl, 16, 6 * D), F32),
        compiler_params=_cp("arbitrary", "arbitrary"),
        name="adaln",
    )(cvec, w_ada, b_ada.reshape(nl, 1, 6 * D))


def _modulated_norm(x, g, shift, scale):
    ms = jnp.mean(x * x, axis=-1, keepdims=True)
    y = x * lax.rsqrt(ms + EPS) * g
    return y * (1.0 + scale) + shift


def _row_tile_specs(width=D):
    return [pl.BlockSpec((TM, width), lambda i: (jnp.minimum(i, NT_LAT - 1), 0)),
            pl.BlockSpec((TM, width), lambda i: (jnp.maximum(i - NT_LAT, 0), 0))]


def _row_tile(xl_ref, xc_ref):
    return jnp.where(pl.program_id(0) < NT_LAT, xl_ref[...], xc_ref[...])


def _inproj_kernel(xl_ref, xc_ref, *refs):
    _inproj_tile(_row_tile(xl_ref, xc_ref), *refs)


def _inproj_tile(x, mod_ref, g_ref, why_ref, wwa_ref, wna_ref, wgd_ref, wab_ref,
                 cos_ref, sa_ref, sb_ref, ohy, owa, ona, ogd, oab):
    m = mod_ref[0]
    h = _modulated_norm(x, g_ref[...], m[0:1], m[1:2]).astype(BF16)
    ohy[...] = _dot(h, why_ref[...]).astype(ohy.dtype)
    ona[...] = _dot(h, wna_ref[...]).astype(ona.dtype)
    ogd[...] = _dot(h, wgd_ref[...]).astype(ogd.dtype)
    oab[...] = _dot(h, wab_ref[...])
    a = _dot(h, wwa_ref[...])
    for c in range(4):
        sl = slice(c * 128, (c + 1) * 128)
        ac = a[:, sl]
        r = (ac * cos_ref[:, sl] + pltpu.roll(ac, 112, axis=1) * sa_ref[:, sl]
             + pltpu.roll(ac, 16, axis=1) * sb_ref[:, sl])
        owa[:, sl] = r.astype(owa.dtype)


def _inproj_specs(l, tile_of):
    row = lambda i: (tile_of(i), 0)
    const = lambda i: (0, 0)
    tab = lambda i: (jnp.where(tile_of(i) < NT_LAT, tile_of(i) % (S // TM), S // TM), 0)
    in_specs = [
        pl.BlockSpec((1, 6, D), lambda i: (l * 16 + (tile_of(i) * TM) // S, 0, 0)),
        pl.BlockSpec((1, D), const),
        pl.BlockSpec((D, 768), const),
        pl.BlockSpec((D, 512), const),
        pl.BlockSpec((D, 768), const),
        pl.BlockSpec((D, 1024), const),
        pl.BlockSpec((D, 128), const),
        pl.BlockSpec((TM, 512), tab),
        pl.BlockSpec((TM, 512), tab),
        pl.BlockSpec((TM, 512), tab),
    ]
    widths = (768, 512, 768, 1024, 128)
    out_specs = [pl.BlockSpec((TM, w), row) for w in widths]
    out_shape = [jax.ShapeDtypeStruct((ROWS, w), F32 if w == 128 else BF16) for w in widths]
    return in_specs, out_specs, out_shape


def _inproj(x_lat, x_ctx, mod, l, g, ws, tabs):
    in_specs, out_specs, out_shape = _inproj_specs(l, lambda i: i)
    return pl.pallas_call(
        _inproj_kernel,
        grid=(NT_ALL,),
        in_specs=_row_tile_specs() + in_specs,
        out_specs=out_specs,
        out_shape=out_shape,
        compiler_params=_cp("arbitrary"),
        name="inproj",
    )(x_lat, x_ctx, mod, g, *ws, *tabs)


def _rope_tables():
    quarter = HD // 4
    pos = jnp.arange(S)
    inv = 10000.0 ** (-jnp.arange(quarter, dtype=F32) / quarter)
    ang_r = (pos // GW).astype(F32)[:, None] * inv[None, :]
    ang_c = (pos % GW).astype(F32)[:, None] * inv[None, :]
    z = jnp.zeros_like(ang_r)
    cos_h = jnp.concatenate([jnp.cos(ang_r)] * 2 + [jnp.cos(ang_c)] * 2, axis=1)
    sa_h = jnp.concatenate([-jnp.sin(ang_r), z, -jnp.sin(ang_c), z], axis=1)
    sb_h = jnp.concatenate([z, jnp.sin(ang_r), z, jnp.sin(ang_c)], axis=1)
    ones = jnp.ones((S, 128), F32)
    zeros = jnp.zeros((S, 128), F32)
    cos_t = jnp.concatenate([jnp.tile(cos_h, (1, 6)), ones], axis=1)
    sa_t = jnp.concatenate([jnp.tile(sa_h, (1, 6)), zeros], axis=1)
    sb_t = jnp.concatenate([jnp.tile(sb_h, (1, 6)), zeros], axis=1)
    ident = jnp.ones((TM, 512), F32)
    none = jnp.zeros((TM, 512), F32)
    return (jnp.concatenate([cos_t, ident], axis=0), jnp.concatenate([sa_t, none], axis=0),
            jnp.concatenate([sb_t, none], axis=0))


def _dft_tables(length):
    n = 2 * length
    f = jnp.arange(length, dtype=jnp.int32)
    m = ((2 * f[:, None] + 1) * (2 * f[None, :] + 1)) % (4 * n)
    th = m.astype(F32) * (2.0 * math.pi / (4 * n))
    phi = (2 * f + 1).astype(F32) * (math.pi / (2 * n))
    return (jnp.cos(th).astype(BF16), jnp.sin(th).astype(BF16),
            jnp.cos(phi)[:, None], jnp.sin(phi)[:, None])


def _hyena_features(length):
    t = jnp.arange(length, dtype=F32)
    t_norm = t / max(length - 1, 1)
    bands = jnp.linspace(1e-4, HY_BANDS - 1, HY_BANDS, dtype=F32)
    band_lane = jnp.concatenate([jnp.zeros((1,), F32), bands, bands, jnp.zeros((128 - 1 - 2 * HY_BANDS,), F32)])
    ang = (2.0 * math.pi / length) * t[:, None] * band_lane[None, :]
    lane = jnp.arange(128)[None, :]
    z = jnp.where(lane == 0, t_norm[:, None],
                  jnp.where(lane <= HY_BANDS, jnp.cos(ang), jnp.where(lane <= 2 * HY_BANDS, -jnp.sin(ang), 0.0)))
    decay = jnp.tile(jnp.linspace(HY_DECAY_MIN, HY_DECAY_MAX, HY_CH, dtype=F32), 2)
    return z, jnp.exp(-t_norm[:, None] * decay[None, :])


def _hyfilt_kernel(z_ref, w1_ref, b1_ref, w2_ref, b2_ref, w3_ref, fr_ref, dec_ref,
                   c_ref, s_ref, cp_ref, sp_ref, kre_ref, kim_ref, pq_ref, *, length):
    @pl.when(pl.program_id(0) == 0)
    def _():
        fr = fr_ref[...]
        h = jnp.sin(fr * (_dot(z_ref[...], w1_ref[...], HI) + b1_ref[...]))
        h = jnp.sin(fr * (_dot(h, w2_ref[...], HI) + b2_ref[...]))
        h = _dot(h, w3_ref[...], HI) * dec_ref[...]
        hf = h[:, :HY_CH]
        t = lax.broadcasted_iota(jnp.int32, (length, HY_CH), 0)
        hb = jnp.where(t == 0, 0.0, h[:, HY_CH:])
        pq_ref[:, :HY_CH] = (hf + hb).astype(BF16)
        pq_ref[:, HY_CH:] = (hb - hf).astype(BF16)

    pq = pq_ref[...]
    cpq = _dot(c_ref[...], pq)
    spq = _dot(s_ref[...], pq)
    cphi, sphi = cp_ref[...], sp_ref[...]
    norm = 1.0 / length
    kre_ref[...] = (cphi * cpq[:, :HY_CH] + sphi * spq[:, :HY_CH]) * norm
    kim_ref[...] = (cphi * spq[:, HY_CH:] - sphi * cpq[:, HY_CH:]) * norm


def _hyena_filter_spectrum(length, w1, b1, w2, b2, w3, freq, dft):
    z, dec = _hyena_features(length)
    c_m, s_m, cphi, sphi = dft
    w1p = jnp.pad(w1, ((0, 128 - w1.shape[0]), (0, 0)))
    tf = min(512, length)
    const = lambda j: (0, 0)
    blk = lambda j: (j, 0)
    return pl.pallas_call(
        functools.partial(_hyfilt_kernel, length=length),
        grid=(length // tf,),
        in_specs=[
            pl.BlockSpec((length, 128), const),
            pl.BlockSpec((128, 64), const),
            pl.BlockSpec((1, 64), const),
            pl.BlockSpec((64, 64), const),
            pl.BlockSpec((1, 64), const),
            pl.BlockSpec((64, 2 * HY_CH), const),
            pl.BlockSpec((1, 64), const),
            pl.BlockSpec((length, 2 * HY_CH), const),
            pl.BlockSpec((tf, length), blk),
            pl.BlockSpec((tf, length), blk),
            pl.BlockSpec((tf, 1), blk),
            pl.BlockSpec((tf, 1), blk),
        ],
        out_specs=[pl.BlockSpec((tf, HY_CH), blk)] * 2,
        out_shape=[jax.ShapeDtypeStruct((length, HY_CH), F32)] * 2,
        scratch_shapes=[pltpu.VMEM((length, 2 * HY_CH), BF16)],
        compiler_params=_cp("arbitrary"),
        name=f"hyena_filter_{length}",
    )(z, w1p, b1[None, :], w2, b2[None, :], w3, freq[None, :], dec, c_m, s_m, cphi, sphi)


def _short_conv3(x_ref, cw):
    x = x_ref[...].astype(F32)
    return _shift_rows(x, -1) * cw[0:1] + x * cw[1:2] + _shift_rows(x, 1) * cw[2:3]


def _hyena_kernel(p0_ref, p1_ref, p2_ref, cw_ref, bias_ref, c_ref, s_ref, kre_ref, kim_ref, o_ref):
    cw = cw_ref[...]
    x0 = _short_conv3(p0_ref, cw[:, :HY_CH])
    k = _short_conv3(p1_ref, cw[:, HY_CH:2 * HY_CH]) * _short_conv3(p2_ref, cw[:, 2 * HY_CH:])
    kb = k.astype(BF16)
    a = _dot(c_ref[...], kb)
    b = _dot(s_ref[...], kb)
    kre, kim = kre_ref[...], kim_ref[...]
    yre = (a * kre + b * kim).astype(BF16)
    yim = (b * kre - a * kim).astype(BF16)
    y = _dot(c_ref[...], yre) + _dot(s_ref[...], yim)
    o_ref[...] = (x0 * (y + k * bias_ref[...])).astype(o_ref.dtype)


def _hyena(p_hy, length, blk0, conv_w, bias, dft, kre, kim):
    c_m, s_m = dft[0], dft[1]
    const = lambda b: (0, 0)
    once = pl.Buffered(1)
    in_specs = [
        pl.BlockSpec((length, HY_CH), lambda b: (blk0 + b, 0)),
        pl.BlockSpec((length, HY_CH), lambda b: (blk0 + b, 1)),
        pl.BlockSpec((length, HY_CH), lambda b: (blk0 + b, 2)),
        pl.BlockSpec((3, 768), const),
        pl.BlockSpec((1, HY_CH), const),
        pl.BlockSpec((length, length), const, pipeline_mode=once),
        pl.BlockSpec((length, length), const, pipeline_mode=once),
        pl.BlockSpec((length, HY_CH), const, pipeline_mode=once),
        pl.BlockSpec((length, HY_CH), const, pipeline_mode=once),
    ]
    return pl.pallas_call(
        _hyena_kernel,
        grid=(B,),
        in_specs=in_specs,
        out_specs=pl.BlockSpec((length, HY_CH), lambda b: (b, 0)),
        out_shape=jax.ShapeDtypeStruct((B * length, HY_CH), BF16),
        compiler_params=_cp("arbitrary"),
        name=f"hyena_{length}",
    )(p_hy, p_hy, p_hy, conv_w, bias[None, :], c_m, s_m, kre, kim)


def _ones_half(vwin, half):
    lane = lax.broadcasted_iota(jnp.int32, vwin.shape, 1)
    keep = (lane < HD) if half == 0 else (lane >= HD)
    return jnp.where(keep, vwin, jnp.ones_like(vwin))


def _softmax_pv(heads):
    probs, maxes = [], []
    for parts, sink, _ in heads:
        m = None
        for s, _ in parts:
            mi = jnp.max(s, axis=-1, keepdims=True)
            m = mi if m is None else jnp.maximum(m, mi)
        if sink is not None:
            m = jnp.maximum(m, sink)
        probs.append([jnp.exp((s - m).astype(BF16)) for s, _ in parts])
        maxes.append(m)
    outs = []
    for (parts, sink, half), es, m in zip(heads, probs, maxes):
        acc = None
        for e, (_, v) in zip(es, parts):
            o = _dot(e, v)
            acc = o if acc is None else acc + o
        den = acc[:, HD:HD + 1] if half == 0 else acc[:, 0:1]
        if sink is not None:
            den = den + jnp.exp(sink - m)
        outs.append(acc / den)
    return outs


def _wa_kernel(sink_ref, q_ref, k_ref, v_ref, kc_ref, vc_ref, o_ref):
    kc, vc = kc_ref[...], vc_ref[...]
    vcs = [_ones_half(vc, hk) for hk in range(2)]
    rr = lax.broadcasted_iota(jnp.int32, (256, 384), 0)
    r1 = lax.broadcasted_iota(jnp.int32, (256, 1), 0)
    heads = []
    for j in range(WA_BLKS):
        n = pl.program_id(1) * WA_BLKS + j
        start = pl.multiple_of(jnp.clip((n - 1) * 128, 0, S - 384), 128)
        kw = k_ref[pl.ds(start, 384), :]
        vw = v_ref[pl.ds(start, 384), :]
        q = q_ref[j * 128:(j + 1) * 128, :] * SCALE
        qpos = n * 128 + jnp.where(rr >= 128, rr - 128, rr)
        kpos = start + lax.broadcasted_iota(jnp.int32, (256, 384), 1)
        valid = jnp.abs(qpos - kpos) <= WINDOW
        for hk in range(2):
            q2 = jnp.concatenate([q[:, (2 * hk) * HD:(2 * hk + 1) * HD],
                                  q[:, (2 * hk + 1) * HD:(2 * hk + 2) * HD]], axis=0)
            hs = slice(hk * HD, (hk + 1) * HD)
            s_loc = jnp.where(valid, _dot_nt(q2, kw[:, hs]), NEG)
            s_ctx = _dot_nt(q2, kc[:, hs])
            sink = jnp.where(r1 >= 128, sink_ref[2 * hk + 1], sink_ref[2 * hk])
            heads.append(([(s_ctx, vcs[hk]), (s_loc, _ones_half(vw, hk))], sink, hk))
    for i, res in enumerate(_softmax_pv(heads)):
        j, hk = divmod(i, 2)
        o = res[:, hk * HD:(hk + 1) * HD]
        rows = slice(j * 128, (j + 1) * 128)
        o_ref[rows, (2 * hk) * HD:(2 * hk + 1) * HD] = o[:128].astype(o_ref.dtype)
        o_ref[rows, (2 * hk + 1) * HD:(2 * hk + 2) * HD] = o[128:].astype(o_ref.dtype)


def _window_attention(p_wa, sink):
    nb = S // (128 * WA_BLKS)
    return pl.pallas_call(
        _wa_kernel,
        grid=(B, nb),
        in_specs=[
            pl.BlockSpec(memory_space=pltpu.SMEM),
            pl.BlockSpec((128 * WA_BLKS, 256), lambda b, n: (b * nb + n, 0)),
            pl.BlockSpec((S, 128), lambda b, n: (b, 2)),
            pl.BlockSpec((S, 128), lambda b, n: (b, 3)),
            pl.BlockSpec((LC, 128), lambda b, n: (CTX_BLK0 + b, 2)),
            pl.BlockSpec((LC, 128), lambda b, n: (CTX_BLK0 + b, 3)),
        ],
        out_specs=pl.BlockSpec((128 * WA_BLKS, 256), lambda b, n: (b * nb + n, 0)),
        out_shape=jax.ShapeDtypeStruct((NL, 256), BF16),
        compiler_params=_cp("arbitrary", "arbitrary"),
        name="window_attn",
    )(sink, p_wa, p_wa, p_wa, p_wa, p_wa)


def _na_kernel(q_ref, k_ref, v_ref, kc_ref, vc_ref, bias_ref, o_ref):
    kc, vc = kc_ref[...], vc_ref[...]
    vcs = [_ones_half(vc[:, (h // 2) * 128:(h // 2 + 1) * 128], h % 2) for h in range(4)]
    heads = []
    for j in range(NA_ROWS):
        r = pl.program_id(1) * NA_ROWS + j
        first = jnp.clip(r - NA_KR // 2, 0, S // GW - NA_KR)
        start = pl.multiple_of(first * GW, GW)
        off = first - r + NA_KR - 1
        kw = k_ref[pl.ds(start, NA_KR * GW), :]
        vw = v_ref[pl.ds(start, NA_KR * GW), :]
        q = q_ref[j * GW:(j + 1) * GW, :] * SCALE
        for h in range(4):
            hs = slice(h * HD, (h + 1) * HD)
            win = slice((h // 2) * 128, (h // 2 + 1) * 128)
            s_loc = _dot_nt(q[:, hs], kw[:, hs]) + bias_ref[h, off]
            s_ctx = _dot_nt(q[:, hs], kc[:, hs])
            heads.append(([(s_ctx, vcs[h]), (s_loc, _ones_half(vw[:, win], h % 2))], None, h % 2))
    outs = _softmax_pv(heads)
    lane = lax.broadcasted_iota(jnp.int32, (GW, 128), 1)
    for j in range(NA_ROWS):
        for hp in range(2):
            pair = jnp.where(lane < HD, outs[4 * j + 2 * hp], outs[4 * j + 2 * hp + 1])
            o_ref[j * GW:(j + 1) * GW, hp * 128:(hp + 1) * 128] = pair.astype(o_ref.dtype)


def _na_bias_kernel(rpb_ref, sel_ref, o_ref):
    o_ref[0] = _dot(rpb_ref[0], sel_ref[...], HI)


def _na_bias_table(rpb):
    qc = jnp.arange(GW)[:, None]
    kcol = jnp.arange(GW)[None, :]
    ci = jnp.clip(kcol - qc + NA_KC - 1, 0, 2 * NA_KC - 2).reshape(1, GW * GW)
    sel = (jnp.arange(128)[:, None] == ci).astype(F32)
    n_ri = 2 * NA_KR - 1
    rpb_p = jnp.pad(rpb.astype(F32), ((0, 0), (0, 16 - n_ri), (0, 128 - (2 * NA_KC - 1))))
    per_row = pl.pallas_call(
        _na_bias_kernel,
        grid=(4,),
        in_specs=[pl.BlockSpec((1, 16, 128), lambda h: (h, 0, 0)), pl.BlockSpec((128, GW * GW), lambda h: (0, 0))],
        out_specs=pl.BlockSpec((1, 16, GW * GW), lambda h: (h, 0, 0)),
        out_shape=jax.ShapeDtypeStruct((4, 16, GW * GW), F32),
        compiler_params=_cp("arbitrary"),
        name="na_bias",
    )(rpb_p, sel).reshape(4, 16, GW, GW)
    tbl = jnp.stack([per_row[:, o:o + NA_KR] for o in range(NA_KR)], axis=1)
    tbl = jnp.transpose(tbl, (0, 1, 3, 2, 4))
    c_start = jnp.clip(qc - NA_KC // 2, 0, GW - NA_KC)
    ok = (kcol >= c_start) & (kcol < c_start + NA_KC)
    tbl = jnp.where(ok[None, None, :, None, :], tbl, NEG)
    return tbl.reshape(4, NA_KR, GW, NA_KR * GW)


def _neighborhood_attention(p_na, bias_tbl):
    steps = S // GW // NA_ROWS
    return pl.pallas_call(
        _na_kernel,
        grid=(B, steps),
        in_specs=[
            pl.BlockSpec((NA_ROWS * GW, 256), lambda b, r: (b * steps + r, 0)),
            pl.BlockSpec((S, 256), lambda b, r: (b, 1)),
            pl.BlockSpec((S, 256), lambda b, r: (b, 2)),
            pl.BlockSpec((LC, 256), lambda b, r: (CTX_BLK0 + b, 1)),
            pl.BlockSpec((LC, 256), lambda b, r: (CTX_BLK0 + b, 2)),
            pl.BlockSpec((4, NA_KR, GW, NA_KR * GW), lambda b, r: (0, 0, 0, 0)),
        ],
        out_specs=pl.BlockSpec((NA_ROWS * GW, 256), lambda b, r: (b * steps + r, 0)),
        out_shape=jax.ShapeDtypeStruct((NL, 256), BF16),
        compiler_params=_cp("arbitrary", "arbitrary"),
        name="neighborhood_attn",
    )(p_na, p_na, p_na, p_na, p_na, bias_tbl)


def _ctx_attn_kernel(sink_ref, qw_ref, kw_ref, vw_ref, qn_ref, kn_ref, vn_ref, owa_ref, ona_ref):
    q, k, v = qw_ref[...] * SCALE, kw_ref[...], vw_ref[...]
    r1 = lax.broadcasted_iota(jnp.int32, (2 * LC, 1), 0)
    heads = []
    for hk in range(2):
        q2 = jnp.concatenate([q[:, (2 * hk) * HD:(2 * hk + 1) * HD],
                              q[:, (2 * hk + 1) * HD:(2 * hk + 2) * HD]], axis=0)
        hs = slice(hk * HD, (hk + 1) * HD)
        sink = jnp.where(r1 >= LC, sink_ref[2 * hk + 1], sink_ref[2 * hk])
        heads.append(([(_dot_nt(q2, k[:, hs]), _ones_half(v, hk))], sink, hk))
    for hk, res in enumerate(_softmax_pv(heads)):
        o = res[:, hk * HD:(hk + 1) * HD]
        owa_ref[:, (2 * hk) * HD:(2 * hk + 1) * HD] = o[:LC].astype(owa_ref.dtype)
        owa_ref[:, (2 * hk + 1) * HD:(2 * hk + 2) * HD] = o[LC:].astype(owa_ref.dtype)
    q, k, v = qn_ref[...] * SCALE, kn_ref[...], vn_ref[...]
    heads = []
    for h in range(4):
        hs = slice(h * HD, (h + 1) * HD)
        win = slice((h // 2) * 128, (h // 2 + 1) * 128)
        heads.append(([(_dot_nt(q[:, hs], k[:, hs]), _ones_half(v[:, win], h % 2))], None, h % 2))
    outs = _softmax_pv(heads)
    lane = lax.broadcasted_iota(jnp.int32, (LC, 128), 1)
    for hp in range(2):
        pair = jnp.where(lane < HD, outs[2 * hp], outs[2 * hp + 1])
        ona_ref[:, hp * 128:(hp + 1) * 128] = pair.astype(ona_ref.dtype)


def _ctx_attention(p_wa, p_na, sink):
    blk = lambda c: (lambda b: (CTX_BLK0 + b, c))
    return pl.pallas_call(
        _ctx_attn_kernel,
        grid=(B,),
        in_specs=[
            pl.BlockSpec(memory_space=pltpu.SMEM),
            pl.BlockSpec((LC, 256), blk(0)),
            pl.BlockSpec((LC, 128), blk(2)),
            pl.BlockSpec((LC, 128), blk(3)),
            pl.BlockSpec((LC, 256), blk(0)),
            pl.BlockSpec((LC, 256), blk(1)),
            pl.BlockSpec((LC, 256), blk(2)),
        ],
        out_specs=[pl.BlockSpec((LC, 256), lambda b: (b, 0))] * 2,
        out_shape=[jax.ShapeDtypeStruct((NC, 256), BF16)] * 2,
        compiler_params=_cp("arbitrary"),
        name="ctx_attn",
    )(sink, p_wa, p_wa, p_wa, p_na, p_na, p_na)


def _gdn_prep_seq(p_ref, ab_ref, cw, alog, dtb, qkv_ref, gb_ref, row0, length):
    x = p_ref[:, :768].astype(F32)
    u = (_shift_rows(x, -2) * cw[0:1] + _shift_rows(x, -1) * cw[1:2] + x * cw[2:3]
         + _shift_rows(x, 1) * cw[3:4])
    u = _silu(u)
    rows = slice(row0, row0 + length)
    for j in range(8):
        sl = slice(j * HD, (j + 1) * HD)
        xs = u[:, sl]
        nrm = lax.rsqrt(jnp.sum(xs * xs, axis=-1, keepdims=True) + EPS)
        qkv_ref[0, rows, sl] = xs * (nrm * SCALE if j < 4 else nrm)
    qkv_ref[0, rows, 512:768] = u[:, 512:768]
    ab = ab_ref[...]
    g = -jnp.exp(alog) * _softplus(ab + dtb)
    lane = lax.broadcasted_iota(jnp.int32, ab.shape, 1)
    gb_ref[0, rows, :] = jnp.where(lane < 8, g, _sigmoid(ab))


def _gdn_prep_kernel(pl_ref, abl_ref, pc_ref, abc_ref, cw_ref, alog_ref, dtb_ref, qkv_ref, gb_ref):
    cw, alog, dtb = cw_ref[...], alog_ref[...], dtb_ref[...]
    _gdn_prep_seq(pl_ref, abl_ref, cw, alog, dtb, qkv_ref, gb_ref, 0, S)
    _gdn_prep_seq(pc_ref, abc_ref, cw, alog, dtb, qkv_ref, gb_ref, S, LC)


def _gdn_prep(p_gd, p_ab, conv_w, a_log, dt_bias):
    const = lambda b: (0, 0)
    alog = jnp.pad(a_log.reshape(1, 8), ((0, 0), (0, 120)))
    dtb = jnp.pad(dt_bias.reshape(1, 8), ((0, 0), (0, 120)))
    return pl.pallas_call(
        _gdn_prep_kernel,
        grid=(B,),
        in_specs=[
            pl.BlockSpec((S, 1024), lambda b: (b, 0)),
            pl.BlockSpec((S, 128), lambda b: (b, 0)),
            pl.BlockSpec((LC, 1024), lambda b: (CTX_BLK0 + b, 0)),
            pl.BlockSpec((LC, 128), lambda b: (CTX_BLK0 + b, 0)),
            pl.BlockSpec((4, 768), const),
            pl.BlockSpec((1, 128), const),
            pl.BlockSpec((1, 128), const),
        ],
        out_specs=[pl.BlockSpec((1, S + LC, 768), lambda b: (b, 0, 0)),
                   pl.BlockSpec((1, S + LC, 128), lambda b: (b, 0, 0))],
        out_shape=[jax.ShapeDtypeStruct((B, S + LC, 768), F32), jax.ShapeDtypeStruct((B, S + LC, 128), F32)],
        compiler_params=_cp("arbitrary"),
        name="gdn_prep",
    )(p_gd, p_ab, p_gd, p_ab, conv_w, alog, dtb)


def _gdn_chains(xs, gbs, cums, cum_ts, states, masks):
    chains = [(g, h) for g in range(len(xs)) for h in range(4)]
    pre = []
    for g, h in chains:
        dirn = g % 2
        x, gb = xs[g], gbs[g]
        tri, strict = masks[dirn]
        col = dirn * 4 + h
        qh = x[:, h * HD:(h + 1) * HD]
        kh = x[:, 256 + h * HD:256 + (h + 1) * HD]
        vh = x[:, 512 + h * HD:512 + (h + 1) * HD]
        gc = cums[g][:, col:col + 1]
        gc_row = cum_ts[g][col:col + 1, :]
        beta = gb[:, 8 + col:9 + col]
        decay = jnp.exp(jnp.where(tri, gc - gc_row, NEG))
        kb = kh * beta
        eg = jnp.exp(gc)
        g_last = gc[0:1, :] if dirn else gc[CHUNK - 1:CHUNK, :]
        pre.append(dict(strict=strict, decay=decay, kbf=kb.astype(BF16), khf=kh.astype(BF16), qf=qh.astype(BF16),
                        xw=jnp.concatenate([vh * beta, kb * eg], axis=1), qdec=(qh * eg).astype(BF16),
                        kdec=(kh * jnp.exp(g_last - gc)).astype(BF16), gl=jnp.exp(g_last)))
    n = len(chains)
    gram = [_dot_nt(p["kbf"], p["khf"]) for p in pre]
    attn = [_dot_nt(p["qf"], p["khf"]) for p in pre]
    lmat = [jnp.where(p["strict"], g * p["decay"], 0.0) for p, g in zip(pre, gram)]
    attn = [(a * p["decay"]).astype(BF16) for p, a in zip(pre, attn)]
    ii = lax.broadcasted_iota(jnp.int32, (CHUNK, CHUNK), 0)
    jj = lax.broadcasted_iota(jnp.int32, (CHUNK, CHUNK), 1)

    def merged_off_blocks(log_s):
        return ((ii >> (log_s + 1)) == (jj >> (log_s + 1))) & ((ii >> log_s) != (jj >> log_s))

    eye = (ii == jj).astype(F32)
    dinv = [eye - jnp.where(merged_off_blocks(0), lm, 0.0) for lm in lmat]
    for log_s in range(1, 6):
        m = merged_off_blocks(log_s)
        dinv_b = [d.astype(BF16) for d in dinv]
        ld = [_dot(jnp.where(m, lmat[c], 0.0).astype(BF16), dinv_b[c]) for c in range(n)]
        upd = [_dot(dinv_b[c], ld[c].astype(BF16)) for c in range(n)]
        dinv = [dinv[c] - upd[c] for c in range(n)]
    xw = [_dot(dinv[c].astype(BF16), pre[c]["xw"].astype(BF16)) for c in range(n)]
    sts = [states[g][h] for g, h in chains]
    stb = [s.astype(BF16) for s in sts]
    ws = [_dot(xw[c][:, HD:].astype(BF16), stb[c]) for c in range(n)]
    qs = [_dot(pre[c]["qdec"], stb[c]) for c in range(n)]
    u_new = [(xw[c][:, :HD] - ws[c]).astype(BF16) for c in range(n)]
    au = [_dot(attn[c], u_new[c]) for c in range(n)]
    ku = [_dot_tn(pre[c]["kdec"], u_new[c]) for c in range(n)]
    outs = [qs[c] + au[c] for c in range(n)]
    new_states = [sts[c] * pre[c]["gl"] + ku[c] for c in range(n)]
    return outs, new_states


def _gdn_kernel(x0_ref, gb0_ref, x1_ref, gb1_ref, o0_ref, o1_ref, state_ref):
    @pl.when(pl.program_id(0) == 0)
    def _():
        state_ref[...] = jnp.zeros_like(state_ref)

    ii = lax.broadcasted_iota(jnp.int32, (CHUNK, CHUNK), 0)
    jj = lax.broadcasted_iota(jnp.int32, (CHUNK, CHUNK), 1)
    masks = ((jj <= ii, jj < ii), (jj >= ii, jj > ii))
    ones_tri = ((jj <= ii).astype(F32), (jj >= ii).astype(F32))

    def body(bp, carry):
        bs = [bp * GDN_SEQS + i for i in range(GDN_SEQS)]
        states = [[state_ref[b, dirn, h] for h in range(4)] for b in bs for dirn in range(2)]
        xs = [r[b] for b in bs for r in (x0_ref, x1_ref)]
        gbs = [r[b] for b in bs for r in (gb0_ref, gb1_ref)]
        cums = [_dot(ones_tri[g % 2], gbs[g], HI) for g in range(2 * GDN_SEQS)]
        cum_ts = [cm.T for cm in cums]
        outs, new_states = _gdn_chains(xs, gbs, cums, cum_ts, states, masks)
        for c, st in enumerate(new_states):
            state_ref[bs[c // 8], (c // 4) % 2, c % 4] = st
        for i, b in enumerate(bs):
            o0_ref[b] = jnp.concatenate(outs[8 * i:8 * i + 4], axis=1)
            o1_ref[b] = jnp.concatenate(outs[8 * i + 4:8 * i + 8], axis=1)
        return carry

    lax.fori_loop(0, B // GDN_SEQS, body, 0)


def _gdn_scan(qkv, gb):
    nc = (S + LC) // CHUNK
    fwd = lambda c: (0, (c + S // CHUNK) % nc, 0)
    bwd = lambda c: (0, nc - 1 - c, 0)
    return pl.pallas_call(
        _gdn_kernel,
        grid=(nc,),
        in_specs=[
            pl.BlockSpec((B, CHUNK, 768), fwd),
            pl.BlockSpec((B, CHUNK, 128), fwd),
            pl.BlockSpec((B, CHUNK, 768), bwd),
            pl.BlockSpec((B, CHUNK, 128), bwd),
        ],
        out_specs=[pl.BlockSpec((B, CHUNK, 256), fwd), pl.BlockSpec((B, CHUNK, 256), bwd)],
        out_shape=[jax.ShapeDtypeStruct((B, S + LC, 256), F32)] * 2,
        scratch_shapes=[pltpu.VMEM((B, 2, 4, HD, HD), F32)],
        compiler_params=_cp("arbitrary"),
        name="gdn_scan",
    )(qkv, gb, qkv, gb)


def _gdn_gated_norm(o, gate, gain):
    gi = lax.broadcasted_iota(jnp.int32, (256, 256), 0) // HD
    gj = lax.broadcasted_iota(jnp.int32, (256, 256), 1) // HD
    group_mean = jnp.where(gi == gj, 1.0 / HD, 0.0).astype(BF16)
    sq = o * o
    sq_hi = sq.astype(BF16)
    sq_lo = (sq - sq_hi.astype(F32)).astype(BF16)
    ms = _dot(sq_hi, group_mean) + _dot(sq_lo, group_mean)
    return o * lax.rsqrt(ms + EPS) * gain * _silu(gate.astype(F32))


def _pack_bf16_pairs(x):
    n = x.shape[1] // 2
    lo = pltpu.bitcast(x[:, :n].astype(BF16).astype(F32), jnp.uint32)
    hi = pltpu.bitcast(x[:, n:].astype(BF16).astype(F32), jnp.uint32)
    return hi | (lo >> 16)


def _unpack_bf16_pairs(w):
    lo = pltpu.bitcast(w << 16, F32)
    hi = pltpu.bitcast(w & jnp.uint32(0xFFFF0000), F32)
    return jnp.concatenate([lo, hi], axis=1)


def _outproj_kernel(xl_ref, xc_ref, mod_ref, g2_ref, yhl_ref, yhc_ref, ywl_ref, ywc_ref, ynl_ref, ync_ref, o0_ref, o1_ref,
                    gate_ref, ng_ref, wo_ref, wr_ref, eb_ref, xo_ref, h2_ref, h2p_ref, idx_ref, tw_ref, rank_ref, cnt_ref,
                    carry_ref):
    @pl.when(pl.program_id(0) == 0)
    def _():
        carry_ref[...] = jnp.zeros_like(carry_ref)

    m = mod_ref[0]
    y_gd = _gdn_gated_norm(o0_ref[0] + o1_ref[0], gate_ref[...], ng_ref[...]).astype(BF16)
    acc = (_dot(_row_tile(yhl_ref, yhc_ref), wo_ref[0:256, :]) + _dot(_row_tile(ywl_ref, ywc_ref), wo_ref[256:512, :])
           + _dot(_row_tile(ynl_ref, ync_ref), wo_ref[512:768, :]) + _dot(y_gd, wo_ref[768:1024, :]))
    x = _row_tile(xl_ref, xc_ref) + m[2:3] * acc
    xo_ref[...] = x
    h2 = _modulated_norm(x, g2_ref[...], m[3:4], m[4:5])
    h2_ref[...] = h2.astype(h2_ref.dtype)
    h2p_ref[...] = _pack_bf16_pairs(h2)
    scores = _sigmoid(_dot(h2, wr_ref[...], HI))
    lane = lax.broadcasted_iota(jnp.int32, scores.shape, 1)
    sel = jnp.where(lane < N_EXP, scores + eb_ref[...], -jnp.inf)
    lane_f = lane.astype(F32)
    idx_out = jnp.zeros(scores.shape, F32)
    s_out = jnp.zeros(scores.shape, F32)
    hits = []
    for kk in range(TOP_K):
        mx = jnp.max(sel, axis=-1, keepdims=True)
        idx = jnp.min(jnp.where(sel == mx, lane_f, 128.0), axis=-1, keepdims=True)
        hit = lane_f == idx
        sk = jnp.sum(jnp.where(hit, scores, 0.0), axis=-1, keepdims=True)
        sel = jnp.where(hit, -jnp.inf, sel)
        idx_out = jnp.where(lane == kk, idx, idx_out)
        s_out = jnp.where(lane == kk, sk, s_out)
        hits.append(hit)
    tot = jnp.sum(s_out, axis=-1, keepdims=True)
    tw_ref[...] = s_out / tot * ROUTED_SCALE
    cnt = jnp.zeros(scores.shape, F32)
    for hit in hits:
        cnt = cnt + jnp.where(hit, 1.0, 0.0)
    ri = lax.broadcasted_iota(jnp.int32, (TM, TM), 0)
    rj = lax.broadcasted_iota(jnp.int32, (TM, TM), 1)
    before = _dot(jnp.where(rj < ri, 1.0, 0.0).astype(BF16), cnt.astype(BF16)) + carry_ref[...]
    rank_out = jnp.zeros(scores.shape, F32)
    for kk, hit in enumerate(hits):
        rk = jnp.sum(jnp.where(hit, before, 0.0), axis=-1, keepdims=True)
        rank_out = jnp.where(lane == kk, rk, rank_out)
    idx_ref[...] = idx_out.T[:8].astype(jnp.int32)
    rank_ref[...] = rank_out.T[:8].astype(jnp.int32)
    carry_ref[...] = carry_ref[...] + jnp.sum(cnt, axis=0, keepdims=True)
    cnt_ref[...] = carry_ref[...].astype(jnp.int32)


def _outproj(x_lat, x_ctx, mod, l, g2, ys, gdn, wo, wr, eb, n_tiles):
    row = lambda i: (i, 0)
    const = lambda i: (0, 0)
    lat_tiles = S // TM
    seq = lambda i: (jnp.where(i < NT_LAT, i // lat_tiles, i - NT_LAT), jnp.where(i < NT_LAT, i % lat_tiles, lat_tiles), 0)
    rows = n_tiles * TM
    o0, o1, p_gd, norm_g = gdn
    return pl.pallas_call(
        _outproj_kernel,
        grid=(n_tiles,),
        in_specs=_row_tile_specs() + [
            pl.BlockSpec((1, 6, D), lambda i: (l * 16 + (i * TM) // S, 0, 0)),
            pl.BlockSpec((1, D), const),
            *_row_tile_specs(256), *_row_tile_specs(256), *_row_tile_specs(256),
            pl.BlockSpec((1, TM, 256), seq),
            pl.BlockSpec((1, TM, 256), seq),
            pl.BlockSpec((TM, 256), lambda i: (i, 3)),
            pl.BlockSpec((1, 256), const),
            pl.BlockSpec((D, D), const),
            pl.BlockSpec((D, 128), const),
            pl.BlockSpec((1, 128), const),
        ],
        out_specs=[
            pl.BlockSpec((TM, D), row),
            pl.BlockSpec((TM, D), row),
            pl.BlockSpec((TM, D // 2), row),
            pl.BlockSpec((8, TM), row),
            pl.BlockSpec((TM, 128), row),
            pl.BlockSpec((8, TM), row),
            pl.BlockSpec((1, 128), const),
        ],
        out_shape=[
            jax.ShapeDtypeStruct((rows, D), F32),
            jax.ShapeDtypeStruct((rows, D), BF16),
            jax.ShapeDtypeStruct((rows, D // 2), jnp.uint32),
            jax.ShapeDtypeStruct((n_tiles * 8, TM), jnp.int32),
            jax.ShapeDtypeStruct((rows, 128), F32),
            jax.ShapeDtypeStruct((n_tiles * 8, TM), jnp.int32),
            jax.ShapeDtypeStruct((1, 128), jnp.int32),
        ],
        scratch_shapes=[pltpu.VMEM((1, 128), F32)],
        compiler_params=_cp("arbitrary"),
        name="outproj_router",
    )(x_lat, x_ctx, mod, g2, *[a for pair in ys for a in pair], o0, o1, p_gd, jnp.tile(norm_g, 4)[None, :], wo, wr, eb)


ASG_TILE = TM * TOP_K


def _segment_tables(counts, n_blocks):
    counts = counts[0, :N_EXP]
    padded = (counts + MOE_BLK - 1) // MOE_BLK * MOE_BLK
    pad_end = jnp.cumsum(padded)
    seg_start = pad_end - padded
    starts = jnp.arange(n_blocks, dtype=jnp.int32) * MOE_BLK
    blk_e = jnp.minimum(jnp.sum((pad_end[None, :] <= starts[:, None]).astype(jnp.int32), axis=1), N_EXP - 1)
    seg = jnp.concatenate([seg_start, seg_start + counts, pad_end[-1:]])
    blk = jnp.concatenate([blk_e, pad_end[-1:] // MOE_BLK])
    return seg.astype(jnp.int32), blk.astype(jnp.int32)


def _row_copy_wait(shape_ref, dst_ref, sem):
    pltpu.make_async_copy(shape_ref, dst_ref, sem).wait()


def _zero_fill_rows(zero_ref, xb_ref, start, length, sem):
    plan = []
    aligned = (start + 7) & -8
    for r in range(7):
        plan.append(((start + r < aligned) & (r < length),
                     pltpu.make_async_copy(zero_ref.at[pl.ds(0, 1)], xb_ref.at[pl.ds(start + r, 1)], sem)))
    rest = start + length - aligned
    off = aligned
    for bit in reversed(range(3, MOE_BLK.bit_length() - 1)):
        size = 1 << bit
        take = (rest & size) != 0
        dst = xb_ref.at[pl.ds(pl.multiple_of(off, 8), size)]
        plan.append((take, pltpu.make_async_copy(zero_ref.at[pl.ds(0, size)], dst, sem)))
        off = off + jnp.where(take, size, 0)
    for take, copy in plan:
        pl.when(take)(copy.start)
    for take, copy in plan:
        pl.when(take)(copy.wait)


def _dest_kernel(seg_ref, idx_ref, rank_ref, o_ref):
    idx = idx_ref[...]
    first = jnp.zeros(idx.shape, jnp.int32)
    for e in range(N_EXP):
        first = jnp.where(idx == e, seg_ref[e], first)
    o_ref[...] = first + rank_ref[...]


def _dest_rows(seg, top_idx, rank):
    n_tiles = top_idx.shape[0] // 8
    blk = lambda i, seg: (i, 0)
    grid_spec = pltpu.PrefetchScalarGridSpec(
        num_scalar_prefetch=1,
        grid=(n_tiles // 8,),
        in_specs=[pl.BlockSpec((64, TM), blk), pl.BlockSpec((64, TM), blk)],
        out_specs=pl.BlockSpec((64, TM), blk),
    )
    return pl.pallas_call(
        _dest_kernel,
        grid_spec=grid_spec,
        out_shape=jax.ShapeDtypeStruct(top_idx.shape, jnp.int32),
        compiler_params=_cp("arbitrary"),
        name="moe_dest_rows",
    )(seg, top_idx, rank)


def _dispatch_kernel(seg_ref, dest_ref, h2p_ref, xb_ref, zero_ref, ring_ref, sems, zsem):
    i = pl.program_id(0)
    sem = sems.at[i % 2]
    tile_ref = ring_ref.at[i % 2]
    tile_ref[...] = h2p_ref[...]

    @pl.when(i == 0)
    def _():
        zero_ref[...] = jnp.zeros_like(zero_ref)

    @pl.when(i < N_EXP)
    def _():
        first_pad = seg_ref[N_EXP + i]
        next_start = seg_ref[jnp.where(i == N_EXP - 1, 2 * N_EXP, i + 1)]
        _zero_fill_rows(zero_ref, xb_ref, first_pad, next_start - first_pad, zsem)

    tail_row = seg_ref[2 * N_EXP] + (i - N_EXP) * MOE_BLK

    @pl.when((i >= N_EXP) & (tail_row < xb_ref.shape[0]))
    def _():
        copy = pltpu.make_async_copy(zero_ref, xb_ref.at[pl.ds(pl.multiple_of(tail_row, MOE_BLK), MOE_BLK)], zsem)
        copy.start()
        copy.wait()

    def issue(t, carry):
        src = tile_ref.at[pl.ds(t, 1)]
        for k in range(TOP_K):
            pltpu.make_async_copy(src, xb_ref.at[pl.ds(dest_ref[k, t], 1)], sem).start(priority=k % 2)
        return carry

    lax.fori_loop(0, TM, issue, 0, unroll=True)

    @pl.when(i > 0)
    def _():
        for _ in range(TOP_K):
            _row_copy_wait(h2p_ref, xb_ref.at[pl.ds(0, TM)], sems.at[(i + 1) % 2])

    @pl.when(i == pl.num_programs(0) - 1)
    def _():
        for _ in range(TOP_K):
            _row_copy_wait(h2p_ref, xb_ref.at[pl.ds(0, TM)], sem)


def _dispatch_rows(seg, dest, h2p, n_rows):
    n_tiles = h2p.shape[0] // TM
    assert n_tiles >= 2 * N_EXP
    grid_spec = pltpu.PrefetchScalarGridSpec(
        num_scalar_prefetch=1,
        grid=(n_tiles,),
        in_specs=[
            pl.BlockSpec((8, TM), lambda i, seg: (i, 0), memory_space=pltpu.SMEM),
            pl.BlockSpec((TM, D // 2), lambda i, seg: (i, 0)),
        ],
        out_specs=pl.BlockSpec(memory_space=pl.ANY),
        scratch_shapes=[pltpu.VMEM((MOE_BLK, D // 2), jnp.uint32), pltpu.VMEM((2, TM, D // 2), jnp.uint32),
                        pltpu.SemaphoreType.DMA((2,)), pltpu.SemaphoreType.DMA(())],
    )
    return pl.pallas_call(
        _dispatch_kernel,
        grid_spec=grid_spec,
        out_shape=jax.ShapeDtypeStruct((n_rows, D // 2), jnp.uint32),
        compiler_params=_cp("arbitrary"),
        name="moe_dispatch",
    )(seg, dest, h2p)


def _moe_kernel(be_ref, x_ref, wg_ref, wu_ref, wd_ref, o_ref, wgb_ref, wub_ref, wdb_ref):
    i = pl.program_id(0)
    in_use = i < be_ref[pl.num_programs(0)]

    @pl.when(in_use & ((i == 0) | (be_ref[i] != be_ref[jnp.maximum(i - 1, 0)])))
    def _():
        wgb_ref[...] = wg_ref[0, 0].astype(BF16)
        wub_ref[...] = wu_ref[0, 0].astype(BF16)
        wdb_ref[...] = wd_ref[0, 0].astype(BF16)

    @pl.when(in_use)
    def _():
        x = _unpack_bf16_pairs(x_ref[...]).astype(BF16)
        g = _dot(x, wgb_ref[...])
        u = _dot(x, wub_ref[...])
        hid = (_silu(g) * u).astype(BF16)
        o_ref[...] = _pack_bf16_pairs(_dot(hid, wdb_ref[...]))

    @pl.when(jnp.logical_not(in_use))
    def _():
        o_ref[...] = jnp.zeros_like(o_ref)


def _moe_experts(blk_e, xb, l, wg, wu, wd):
    n_rows = xb.shape[0]
    grid_spec = pltpu.PrefetchScalarGridSpec(
        num_scalar_prefetch=1,
        grid=(n_rows // MOE_BLK,),
        in_specs=[
            pl.BlockSpec((MOE_BLK, D // 2), lambda i, be: (i, 0)),
            pl.BlockSpec((1, 1, D, D_EXP), lambda i, be: (l, be[i], 0, 0)),
            pl.BlockSpec((1, 1, D, D_EXP), lambda i, be: (l, be[i], 0, 0)),
            pl.BlockSpec((1, 1, D_EXP, D), lambda i, be: (l, be[i], 0, 0)),
        ],
        out_specs=pl.BlockSpec((MOE_BLK, D // 2), lambda i, be: (i, 0)),
        scratch_shapes=[pltpu.VMEM((D, D_EXP), BF16), pltpu.VMEM((D, D_EXP), BF16), pltpu.VMEM((D_EXP, D), BF16)],
    )
    return pl.pallas_call(
        _moe_kernel,
        grid_spec=grid_spec,
        out_shape=jax.ShapeDtypeStruct((n_rows, D // 2), jnp.uint32),
        compiler_params=_cp("arbitrary"),
        name="moe_experts",
    )(blk_e, xb, wg, wu, wd)


def _ffn_out_kernel(dest_ref, x_ref, mod_ref, h2_ref, tw_ref, wg_ref, wu_ref, wd_ref, nf_ref, yb_ref, *rest, final):
    if final:
        o_ref, gat_ref, sems = rest
    else:
        proj_in, o_ref, proj_out, gat_ref, sems = rest[:10], rest[10], rest[11:16], rest[16], rest[17]
    s = pl.program_id(0)

    def start_gathers(slot):
        buf = gat_ref.at[slot]

        def issue(t, carry):
            for k in range(TOP_K):
                copy = pltpu.make_async_copy(yb_ref.at[pl.ds(dest_ref[k, t], 1)], buf.at[pl.ds(k * TM + t, 1)],
                                             sems.at[slot])
                copy.start(priority=k % 2)
            return carry

        lax.fori_loop(0, TM, issue, 0, unroll=True)

    def wait_gathers(slot):
        for _ in range(TOP_K):
            _row_copy_wait(yb_ref.at[pl.ds(0, TM)], gat_ref.at[slot, pl.ds(0, TM)], sems.at[slot])

    pl.when(s == 0)(lambda: start_gathers(1))
    done = (s + 1) % 2
    wait_gathers(done)
    start_gathers(s % 2)
    m = mod_ref[0]
    h2 = h2_ref[...]
    hid = (_silu(_dot(h2, wg_ref[...])) * _dot(h2, wu_ref[...])).astype(BF16)
    shared = _dot(hid, wd_ref[...])
    buf = gat_ref.at[done]
    tw = tw_ref[...]
    routed = _unpack_bf16_pairs(buf[0:TM, :]) * tw[:, 0:1]
    for k in range(1, TOP_K):
        routed = routed + _unpack_bf16_pairs(buf[k * TM:(k + 1) * TM, :]) * tw[:, k:k + 1]
    x = x_ref[...] + m[5:6] * (routed + shared)
    if final:
        ms = jnp.mean(x * x, axis=-1, keepdims=True)
        x = x * lax.rsqrt(ms + EPS) * nf_ref[...]
    o_ref[...] = x
    if not final:
        _inproj_tile(x, *proj_in, *proj_out)
    pl.when(s == pl.num_programs(0) - 1)(lambda: wait_gathers(s % 2))


def _ffn_out(dest, xn, mod, l, h2, top_w, wg, wu, wd, norm_f, yb, n_tiles, next_proj=None):
    tile_of = lambda i: jnp.maximum(i - 1, 0)
    row = lambda i: (tile_of(i), 0)
    ahead = lambda i: (jnp.minimum(i, n_tiles - 1), 0)
    const = lambda i: (0, 0)
    in_specs = [
        pl.BlockSpec((8, TM), ahead, memory_space=pltpu.SMEM),
        pl.BlockSpec((TM, D), row),
        pl.BlockSpec((1, 6, D), lambda i: (l * 16 + (tile_of(i) * TM) // S, 0, 0)),
        pl.BlockSpec((TM, D), row),
        pl.BlockSpec((TM, 128), row),
        pl.BlockSpec((D, D_EXP), const),
        pl.BlockSpec((D, D_EXP), const),
        pl.BlockSpec((D_EXP, D), const),
        pl.BlockSpec((1, D), const),
        pl.BlockSpec(memory_space=pl.ANY),
    ]
    args = [dest, xn, mod, h2, top_w, wg, wu, wd, norm_f, yb]
    out_specs = [pl.BlockSpec((TM, D), row)]
    out_shape = [jax.ShapeDtypeStruct((n_tiles * TM, D), F32)]
    if next_proj is not None:
        g, ws, tabs = next_proj
        proj_in, proj_out, proj_shape = _inproj_specs(l + 1, tile_of)
        in_specs += proj_in
        args += [mod, g, *ws, *tabs]
        out_specs += proj_out
        out_shape += proj_shape
    return pl.pallas_call(
        functools.partial(_ffn_out_kernel, final=next_proj is None),
        grid=(n_tiles + 1,),
        in_specs=in_specs,
        out_specs=out_specs,
        scratch_shapes=[pltpu.VMEM((2, ASG_TILE, D // 2), jnp.uint32), pltpu.SemaphoreType.DMA((2,))],
        out_shape=out_shape,
        compiler_params=_cp("arbitrary"),
        name="shared_ffn_residual",
    )(*args)


def kernel(x, c, ctx, c_ctx, w_ada, b_ada, norm1, norm2, norm_f, w_in, w_out, hy_conv, hy_w1, hy_b1, hy_w2, hy_b2, hy_w3, hy_freq, hy_bias, wa_sink, na_rpb, gdn_conv, gdn_a_log, gdn_dt_bias, gdn_norm, moe_router, moe_bias, moe_gate, moe_up, moe_down, sh_gate, sh_up, sh_down):
    depth = w_ada.shape[0]
    x_lat, x_ctx = x.reshape(NL, D), ctx.reshape(NC, D)
    cvec = jnp.concatenate([c, c_ctx[None, :], jnp.zeros((16 - B - 1, D), F32)], axis=0)
    mod = _ada(cvec, w_ada, b_ada).reshape(depth * 16, 6, D)
    rope = _rope_tables()
    dft_lat = _dft_tables(S)
    dft_ctx = _dft_tables(LC)
    o1, o2, o3 = 768, 768 + 512, 768 + 512 + 768

    def proj_params(l):
        wl = w_in[l].astype(BF16)
        ws = (wl[:, :o1], wl[:, o1:o2], wl[:, o2:o3], wl[:, o3:o3 + 1024],
              jnp.pad(wl[:, o3 + 1024:], ((0, 0), (0, 128 - 16))))
        return norm1[l][None, :], ws, rope

    p_hy, p_wa, p_na, p_gd, p_ab = _inproj(x_lat, x_ctx, mod, 0, *proj_params(0))
    for l in range(depth):
        need_ctx = l < depth - 1
        n_tiles = NT_ALL if need_ctx else NT_LAT

        filt = (hy_w1[l], hy_b1[l], hy_w2[l], hy_b2[l], hy_w3[l], hy_freq[l])
        kre, kim = _hyena_filter_spectrum(S, *filt, dft_lat)
        y_hy = _hyena(p_hy, S, 0, hy_conv[l], hy_bias[l], dft_lat, kre, kim)
        y_wa = _window_attention(p_wa, wa_sink[l])
        y_na = _neighborhood_attention(p_na, _na_bias_table(na_rpb[l]))
        if need_ctx:
            kre_c, kim_c = _hyena_filter_spectrum(LC, *filt, dft_ctx)
            yc_hy = _hyena(p_hy, LC, CTX_BLK0, hy_conv[l], hy_bias[l], dft_ctx, kre_c, kim_c)
            yc_wa, yc_na = _ctx_attention(p_wa, p_na, wa_sink[l])
        else:
            yc_hy, yc_wa, yc_na = y_hy, y_wa, y_na
        qkv, gb = _gdn_prep(p_gd, p_ab, gdn_conv[l], gdn_a_log[l], gdn_dt_bias[l])
        gd_fwd, gd_bwd = _gdn_scan(qkv, gb)

        wr = jnp.pad(moe_router[l], ((0, 0), (0, 128 - N_EXP)))
        eb = jnp.pad(moe_bias[l], (0, 128 - N_EXP))[None, :]
        xn, h2, h2p, top_idx, top_w, rank, counts = _outproj(
            x_lat, x_ctx, mod, l, norm2[l][None, :], ((y_hy, yc_hy), (y_wa, yc_wa), (y_na, yc_na)),
            (gd_fwd, gd_bwd, p_gd, gdn_norm[l]),
            w_out[l].astype(BF16), wr, eb, n_tiles)
        n_asg = n_tiles * ASG_TILE
        n_blocks = n_asg // MOE_BLK + N_EXP
        seg, blk_e = _segment_tables(counts, n_blocks)
        dest = _dest_rows(seg, top_idx, rank)
        xb = _dispatch_rows(seg, dest, h2p, n_blocks * MOE_BLK)
        yb = _moe_experts(blk_e, xb, l, moe_gate, moe_up, moe_down)
        res = _ffn_out(dest, xn, mod, l, h2, top_w, sh_gate[l].astype(BF16), sh_up[l].astype(BF16),
                       sh_down[l].astype(BF16), norm_f[None, :], yb, n_tiles,
                       next_proj=proj_params(l + 1) if need_ctx else None)
        if need_ctx:
            x_lat, p_hy, p_wa, p_na, p_gd, p_ab = res
            x_ctx = x_lat[NL:] if l + 2 < depth else x_lat
        else:
            return res[0].reshape(B, S, D)
```

```python
import functools
import math

import jax
import jax.numpy as jnp
from jax import lax
from jax.experimental import pallas as pl
from jax.experimental.pallas import tpu as pltpu

F32 = jnp.float32
BF16 = jnp.bfloat16
HI = lax.Precision.HIGHEST

D = 1024
B = 8
S = 2048
LC = 256
GW = 64
HD = 64
NL = B * S
NC = B * LC
ROWS = NL + NC
TM = 256
NT_LAT = NL // TM
NT_ALL = ROWS // TM
CTX_BLK0 = NL // LC

HY_CH = 256
HY_BANDS = 16
HY_DECAY_MIN = -math.log(1e-2) / 1.5
HY_DECAY_MAX = -math.log(1e-2) / 0.3
WINDOW = 128
NA_KR = 8
NA_KC = 16
CHUNK = 64
N_EXP = 32
TOP_K = 4
D_EXP = 256
ROUTED_SCALE = 2.5
MOE_BLK = 512
WA_BLKS = 4
NA_ROWS = 8
GDN_SEQS = 4
EPS = 1e-6
NEG = -1e30
SCALE = HD ** -0.5
VMEM_LIMIT = 56 * 1024 * 1024


def _cp(*sem):
    return pltpu.CompilerParams(dimension_semantics=tuple(sem), vmem_limit_bytes=VMEM_LIMIT)


def _dot(a, b, precision=None):
    return jnp.dot(a, b, preferred_element_type=F32, precision=precision)


def _dot_nt(a, b, precision=None):
    return lax.dot_general(a, b, (((1,), (1,)), ((), ())), preferred_element_type=F32, precision=precision)


def _dot_tn(a, b, precision=None):
    return lax.dot_general(a, b, (((0,), (0,)), ((), ())), preferred_element_type=F32, precision=precision)


def _sigmoid(x):
    return 1.0 / (1.0 + jnp.exp(-x))


def _silu(x):
    return x * _sigmoid(x)


def _softplus(x):
    return jnp.maximum(x, 0.0) + jnp.log(1.0 + jnp.exp(-jnp.abs(x)))


def _shift_rows(x, d):
    n = x.shape[0]
    if d == 0:
        return x
    y = pltpu.roll(x, (-d) % n, axis=0)
    t = lax.broadcasted_iota(jnp.int32, x.shape, 0)
    ok = (t + d >= 0) & (t + d < n)
    return jnp.where(ok, y, 0.0)


def _ada_kernel(c_ref, w_ref, b_ref, o_ref):
    s = _silu(c_ref[...])
    o_ref[0] = _dot(s.astype(BF16), w_ref[0].astype(BF16)) + b_ref[0]


def _ada(cvec, w_ada, b_ada):
    nl = w_ada.shape[0]
    tn = 1536
    return pl.pallas_call(
        _ada_kernel,
        grid=(nl, 6 * D // tn),
        in_specs=[
            pl.BlockSpec((16, D), lambda l, j: (0, 0)),
            pl.BlockSpec((1, D, tn), lambda l, j: (l, 0, j)),
            pl.BlockSpec((1, 1, tn), lambda l, j: (l, 0, j)),
        ],
        out_specs=pl.BlockSpec((1, 16, tn), lambda l, j: (l, 0, j)),
        out_shape=jax.ShapeDtypeStruct((nl, 16, 6 * D), F32),
        compiler_params=_cp("arbitrary", "arbitrary"),
        name="adaln",
    )(cvec, w_ada, b_ada.reshape(nl, 1, 6 * D))


def _modulated_norm(x, g, shift, scale):
    ms = jnp.mean(x * x, axis=-1, keepdims=True)
    y = x * lax.rsqrt(ms + EPS) * g
    return y * (1.0 + scale) + shift


def _row_tile_specs(width=D):
    return [pl.BlockSpec((TM, width), lambda i: (jnp.minimum(i, NT_LAT - 1), 0)),
            pl.BlockSpec((TM, width), lambda i: (jnp.maximum(i - NT_LAT, 0), 0))]


def _row_tile(xl_ref, xc_ref):
    return jnp.where(pl.program_id(0) < NT_LAT, xl_ref[...], xc_ref[...])


def _inproj_kernel(xl_ref, xc_ref, *refs):
    _inproj_tile(_row_tile(xl_ref, xc_ref), *refs)


def _inproj_tile(x, mod_ref, g_ref, why_ref, wwa_ref, wna_ref, wgd_ref, wab_ref,
                 cos_ref, sa_ref, sb_ref, ohy, owa, ona, ogd, oab):
    m = mod_ref[0]
    h = _modulated_norm(x, g_ref[...], m[0:1], m[1:2]).astype(BF16)
    ohy[...] = _dot(h, why_ref[...]).astype(ohy.dtype)
    ona[...] = _dot(h, wna_ref[...]).astype(ona.dtype)
    ogd[...] = _dot(h, wgd_ref[...]).astype(ogd.dtype)
    oab[...] = _dot(h, wab_ref[...])
    a = _dot(h, wwa_ref[...])
    for c in range(4):
        sl = slice(c * 128, (c + 1) * 128)
        ac = a[:, sl]
        r = (ac * cos_ref[:, sl] + pltpu.roll(ac, 112, axis=1) * sa_ref[:, sl]
             + pltpu.roll(ac, 16, axis=1) * sb_ref[:, sl])
        owa[:, sl] = r.astype(owa.dtype)


def _inproj_specs(l, tile_of):
    row = lambda i: (tile_of(i), 0)
    const = lambda i: (0, 0)
    tab = lambda i: (jnp.where(tile_of(i) < NT_LAT, tile_of(i) % (S // TM), S // TM), 0)
    in_specs = [
        pl.BlockSpec((1, 6, D), lambda i: (l * 16 + (tile_of(i) * TM) // S, 0, 0)),
        pl.BlockSpec((1, D), const),
        pl.BlockSpec((D, 768), const),
        pl.BlockSpec((D, 512), const),
        pl.BlockSpec((D, 768), const),
        pl.BlockSpec((D, 1024), const),
        pl.BlockSpec((D, 128), const),
        pl.BlockSpec((TM, 512), tab),
        pl.BlockSpec((TM, 512), tab),
        pl.BlockSpec((TM, 512), tab),
    ]
    widths = (768, 512, 768, 1024, 128)
    out_specs = [pl.BlockSpec((TM, w), row) for w in widths]
    out_shape = [jax.ShapeDtypeStruct((ROWS, w), F32 if w == 128 else BF16) for w in widths]
    return in_specs, out_specs, out_shape


def _inproj(x_lat, x_ctx, mod, l, g, ws, tabs):
    in_specs, out_specs, out_shape = _inproj_specs(l, lambda i: i)
    return pl.pallas_call(
        _inproj_kernel,
        grid=(NT_ALL,),
        in_specs=_row_tile_specs() + in_specs,
        out_specs=out_specs,
        out_shape=out_shape,
        compiler_params=_cp("arbitrary"),
        name="inproj",
    )(x_lat, x_ctx, mod, g, *ws, *tabs)


def _rope_tables():
    quarter = HD // 4
    pos = jnp.arange(S)
    inv = 10000.0 ** (-jnp.arange(quarter, dtype=F32) / quarter)
    ang_r = (pos // GW).astype(F32)[:, None] * inv[None, :]
    ang_c = (pos % GW).astype(F32)[:, None] * inv[None, :]
    z = jnp.zeros_like(ang_r)
    cos_h = jnp.concatenate([jnp.cos(ang_r)] * 2 + [jnp.cos(ang_c)] * 2, axis=1)
    sa_h = jnp.concatenate([-jnp.sin(ang_r), z, -jnp.sin(ang_c), z], axis=1)
    sb_h = jnp.concatenate([z, jnp.sin(ang_r), z, jnp.sin(ang_c)], axis=1)
    ones = jnp.ones((S, 128), F32)
    zeros = jnp.zeros((S, 128), F32)
    cos_t = jnp.concatenate([jnp.tile(cos_h, (1, 6)), ones], axis=1)
    sa_t = jnp.concatenate([jnp.tile(sa_h, (1, 6)), zeros], axis=1)
    sb_t = jnp.concatenate([jnp.tile(sb_h, (1, 6)), zeros], axis=1)
    ident = jnp.ones((TM, 512), F32)
    none = jnp.zeros((TM, 512), F32)
    return (jnp.concatenate([cos_t, ident], axis=0), jnp.concatenate([sa_t, none], axis=0),
            jnp.concatenate([sb_t, none], axis=0))


def _dft_tables(length):
    n = 2 * length
    f = jnp.arange(length, dtype=jnp.int32)
    m = ((2 * f[:, None] + 1) * (2 * f[None, :] + 1)) % (4 * n)
    th = m.astype(F32) * (2.0 * math.pi / (4 * n))
    phi = (2 * f + 1).astype(F32) * (math.pi / (2 * n))
    return (jnp.cos(th).astype(BF16), jnp.sin(th).astype(BF16),
            jnp.cos(phi)[:, None], jnp.sin(phi)[:, None])


def _hyena_features(length):
    t = jnp.arange(length, dtype=F32)
    t_norm = t / max(length - 1, 1)
    bands = jnp.linspace(1e-4, HY_BANDS - 1, HY_BANDS, dtype=F32)
    band_lane = jnp.concatenate([jnp.zeros((1,), F32), bands, bands, jnp.zeros((128 - 1 - 2 * HY_BANDS,), F32)])
    ang = (2.0 * math.pi / length) * t[:, None] * band_lane[None, :]
    lane = jnp.arange(128)[None, :]
    z = jnp.where(lane == 0, t_norm[:, None],
                  jnp.where(lane <= HY_BANDS, jnp.cos(ang), jnp.where(lane <= 2 * HY_BANDS, -jnp.sin(ang), 0.0)))
    decay = jnp.tile(jnp.linspace(HY_DECAY_MIN, HY_DECAY_MAX, HY_CH, dtype=F32), 2)
    return z, jnp.exp(-t_norm[:, None] * decay[None, :])


def _hyfilt_kernel(z_ref, w1_ref, b1_ref, w2_ref, b2_ref, w3_ref, fr_ref, dec_ref,
                   c_ref, s_ref, cp_ref, sp_ref, kre_ref, kim_ref, pq_ref, *, length):
    @pl.when(pl.program_id(0) == 0)
    def _():
        fr = fr_ref[...]
        h = jnp.sin(fr * (_dot(z_ref[...], w1_ref[...], HI) + b1_ref[...]))
        h = jnp.sin(fr * (_dot(h, w2_ref[...], HI) + b2_ref[...]))
        h = _dot(h, w3_ref[...], HI) * dec_ref[...]
        hf = h[:, :HY_CH]
        t = lax.broadcasted_iota(jnp.int32, (length, HY_CH), 0)
        hb = jnp.where(t == 0, 0.0, h[:, HY_CH:])
        pq_ref[:, :HY_CH] = (hf + hb).astype(BF16)
        pq_ref[:, HY_CH:] = (hb - hf).astype(BF16)

    pq = pq_ref[...]
    cpq = _dot(c_ref[...], pq)
    spq = _dot(s_ref[...], pq)
    cphi, sphi = cp_ref[...], sp_ref[...]
    norm = 1.0 / length
    kre_ref[...] = (cphi * cpq[:, :HY_CH] + sphi * spq[:, :HY_CH]) * norm
    kim_ref[...] = (cphi * spq[:, HY_CH:] - sphi * cpq[:, HY_CH:]) * norm


def _hyena_filter_spectrum(length, w1, b1, w2, b2, w3, freq, dft):
    z, dec = _hyena_features(length)
    c_m, s_m, cphi, sphi = dft
    w1p = jnp.pad(w1, ((0, 128 - w1.shape[0]), (0, 0)))
    tf = min(512, length)
    const = lambda j: (0, 0)
    blk = lambda j: (j, 0)
    return pl.pallas_call(
        functools.partial(_hyfilt_kernel, length=length),
        grid=(length // tf,),
        in_specs=[
            pl.BlockSpec((length, 128), const),
            pl.BlockSpec((128, 64), const),
            pl.BlockSpec((1, 64), const),
            pl.BlockSpec((64, 64), const),
            pl.BlockSpec((1, 64), const),
            pl.BlockSpec((64, 2 * HY_CH), const),
            pl.BlockSpec((1, 64), const),
            pl.BlockSpec((length, 2 * HY_CH), const),
            pl.BlockSpec((tf, length), blk),
            pl.BlockSpec((tf, length), blk),
            pl.BlockSpec((tf, 1), blk),
            pl.BlockSpec((tf, 1), blk),
        ],
        out_specs=[pl.BlockSpec((tf, HY_CH), blk)] * 2,
        out_shape=[jax.ShapeDtypeStruct((length, HY_CH), F32)] * 2,
        scratch_shapes=[pltpu.VMEM((length, 2 * HY_CH), BF16)],
        compiler_params=_cp("arbitrary"),
        name=f"hyena_filter_{length}",
    )(z, w1p, b1[None, :], w2, b2[None, :], w3, freq[None, :], dec, c_m, s_m, cphi, sphi)


def _short_conv3(x_ref, cw):
    x = x_ref[...].astype(F32)
    return _shift_rows(x, -1) * cw[0:1] + x * cw[1:2] + _shift_rows(x, 1) * cw[2:3]


def _hyena_kernel(p0_ref, p1_ref, p2_ref, cw_ref, bias_ref, c_ref, s_ref, kre_ref, kim_ref, o_ref):
    cw = cw_ref[...]
    x0 = _short_conv3(p0_ref, cw[:, :HY_CH])
    k = _short_conv3(p1_ref, cw[:, HY_CH:2 * HY_CH]) * _short_conv3(p2_ref, cw[:, 2 * HY_CH:])
    kb = k.astype(BF16)
    a = _dot(c_ref[...], kb)
    b = _dot(s_ref[...], kb)
    kre, kim = kre_ref[...], kim_ref[...]
    yre = (a * kre + b * kim).astype(BF16)
    yim = (b * kre - a * kim).astype(BF16)
    y = _dot(c_ref[...], yre) + _dot(s_ref[...], yim)
    o_ref[...] = (x0 * (y + k * bias_ref[...])).astype(o_ref.dtype)


def _hyena(p_hy, length, blk0, conv_w, bias, dft, kre, kim):
    c_m, s_m = dft[0], dft[1]
    const = lambda b: (0, 0)
    once = pl.Buffered(1)
    in_specs = [
        pl.BlockSpec((length, HY_CH), lambda b: (blk0 + b, 0)),
        pl.BlockSpec((length, HY_CH), lambda b: (blk0 + b, 1)),
        pl.BlockSpec((length, HY_CH), lambda b: (blk0 + b, 2)),
        pl.BlockSpec((3, 768), const),
        pl.BlockSpec((1, HY_CH), const),
        pl.BlockSpec((length, length), const, pipeline_mode=once),
        pl.BlockSpec((length, length), const, pipeline_mode=once),
        pl.BlockSpec((length, HY_CH), const, pipeline_mode=once),
        pl.BlockSpec((length, HY_CH), const, pipeline_mode=once),
    ]
    return pl.pallas_call(
        _hyena_kernel,
        grid=(B,),
        in_specs=in_specs,
        out_specs=pl.BlockSpec((length, HY_CH), lambda b: (b, 0)),
        out_shape=jax.ShapeDtypeStruct((B * length, HY_CH), BF16),
        compiler_params=_cp("arbitrary"),
        name=f"hyena_{length}",
    )(p_hy, p_hy, p_hy, conv_w, bias[None, :], c_m, s_m, kre, kim)


def _ones_half(vwin, half):
    lane = lax.broadcasted_iota(jnp.int32, vwin.shape, 1)
    keep = (lane < HD) if half == 0 else (lane >= HD)
    return jnp.where(keep, vwin, jnp.ones_like(vwin))


def _softmax_pv(heads):
    probs, maxes = [], []
    for parts, sink, _ in heads:
        m = None
        for s, _ in parts:
            mi = jnp.max(s, axis=-1, keepdims=True)
            m = mi if m is None else jnp.maximum(m, mi)
        if sink is not None:
            m = jnp.maximum(m, sink)
        probs.append([jnp.exp((s - m).astype(BF16)) for s, _ in parts])
        maxes.append(m)
    outs = []
    for (parts, sink, half), es, m in zip(heads, probs, maxes):
        acc = None
        for e, (_, v) in zip(es, parts):
            o = _dot(e, v)
            acc = o if acc is None else acc + o
        den = acc[:, HD:HD + 1] if half == 0 else acc[:, 0:1]
        if sink is not None:
            den = den + jnp.exp(sink - m)
        outs.append(acc / den)
    return outs


def _wa_kernel(sink_ref, q_ref, k_ref, v_ref, kc_ref, vc_ref, o_ref):
    kc, vc = kc_ref[...], vc_ref[...]
    vcs = [_ones_half(vc, hk) for hk in range(2)]
    rr = lax.broadcasted_iota(jnp.int32, (256, 384), 0)
    r1 = lax.broadcasted_iota(jnp.int32, (256, 1), 0)
    heads = []
    for j in range(WA_BLKS):
        n = pl.program_id(1) * WA_BLKS + j
        start = pl.multiple_of(jnp.clip((n - 1) * 128, 0, S - 384), 128)
        kw = k_ref[pl.ds(start, 384), :]
        vw = v_ref[pl.ds(start, 384), :]
        q = q_ref[j * 128:(j + 1) * 128, :] * SCALE
        qpos = n * 128 + jnp.where(rr >= 128, rr - 128, rr)
        kpos = start + lax.broadcasted_iota(jnp.int32, (256, 384), 1)
        valid = jnp.abs(qpos - kpos) <= WINDOW
        for hk in range(2):
            q2 = jnp.concatenate([q[:, (2 * hk) * HD:(2 * hk + 1) * HD],
                                  q[:, (2 * hk + 1) * HD:(2 * hk + 2) * HD]], axis=0)
            hs = slice(hk * HD, (hk + 1) * HD)
            s_loc = jnp.where(valid, _dot_nt(q2, kw[:, hs]), NEG)
            s_ctx = _dot_nt(q2, kc[:, hs])
            sink = jnp.where(r1 >= 128, sink_ref[2 * hk + 1], sink_ref[2 * hk])
            heads.append(([(s_ctx, vcs[hk]), (s_loc, _ones_half(vw, hk))], sink, hk))
    for i, res in enumerate(_softmax_pv(heads)):
        j, hk = divmod(i, 2)
        o = res[:, hk * HD:(hk + 1) * HD]
        rows = slice(j * 128, (j + 1) * 128)
        o_ref[rows, (2 * hk) * HD:(2 * hk + 1) * HD] = o[:128].astype(o_ref.dtype)
        o_ref[rows, (2 * hk + 1) * HD:(2 * hk + 2) * HD] = o[128:].astype(o_ref.dtype)


def _window_attention(p_wa, sink):
    nb = S // (128 * WA_BLKS)
    return pl.pallas_call(
        _wa_kernel,
        grid=(B, nb),
        in_specs=[
            pl.BlockSpec(memory_space=pltpu.SMEM),
            pl.BlockSpec((128 * WA_BLKS, 256), lambda b, n: (b * nb + n, 0)),
            pl.BlockSpec((S, 128), lambda b, n: (b, 2)),
            pl.BlockSpec((S, 128), lambda b, n: (b, 3)),
            pl.BlockSpec((LC, 128), lambda b, n: (CTX_BLK0 + b, 2)),
            pl.BlockSpec((LC, 128), lambda b, n: (CTX_BLK0 + b, 3)),
        ],
        out_specs=pl.BlockSpec((128 * WA_BLKS, 256), lambda b, n: (b * nb + n, 0)),
        out_shape=jax.ShapeDtypeStruct((NL, 256), BF16),
        compiler_params=_cp("arbitrary", "arbitrary"),
        name="window_attn",
    )(sink, p_wa, p_wa, p_wa, p_wa, p_wa)


def _na_kernel(q_ref, k_ref, v_ref, kc_ref, vc_ref, bias_ref, o_ref):
    kc, vc = kc_ref[...], vc_ref[...]
    vcs = [_ones_half(vc[:, (h // 2) * 128:(h // 2 + 1) * 128], h % 2) for h in range(4)]
    heads = []
    for j in range(NA_ROWS):
        r = pl.program_id(1) * NA_ROWS + j
        first = jnp.clip(r - NA_KR // 2, 0, S // GW - NA_KR)
        start = pl.multiple_of(first * GW, GW)
        off = first - r + NA_KR - 1
        kw = k_ref[pl.ds(start, NA_KR * GW), :]
        vw = v_ref[pl.ds(start, NA_KR * GW), :]
        q = q_ref[j * GW:(j + 1) * GW, :] * SCALE
        for h in range(4):
            hs = slice(h * HD, (h + 1) * HD)
            win = slice((h // 2) * 128, (h // 2 + 1) * 128)
            s_loc = _dot_nt(q[:, hs], kw[:, hs]) + bias_ref[h, off]
            s_ctx = _dot_nt(q[:, hs], kc[:, hs])
            heads.append(([(s_ctx, vcs[h]), (s_loc, _ones_half(vw[:, win], h % 2))], None, h % 2))
    outs = _softmax_pv(heads)
    lane = lax.broadcasted_iota(jnp.int32, (GW, 128), 1)
    for j in range(NA_ROWS):
        for hp in range(2):
            pair = jnp.where(lane < HD, outs[4 * j + 2 * hp], outs[4 * j + 2 * hp + 1])
            o_ref[j * GW:(j + 1) * GW, hp * 128:(hp + 1) * 128] = pair.astype(o_ref.dtype)


def _na_bias_kernel(rpb_ref, sel_ref, o_ref):
    o_ref[0] = _dot(rpb_ref[0], sel_ref[...], HI)


def _na_bias_table(rpb):
    qc = jnp.arange(GW)[:, None]
    kcol = jnp.arange(GW)[None, :]
    ci = jnp.clip(kcol - qc + NA_KC - 1, 0, 2 * NA_KC - 2).reshape(1, GW * GW)
    sel = (jnp.arange(128)[:, None] == ci).astype(F32)
    n_ri = 2 * NA_KR - 1
    rpb_p = jnp.pad(rpb.astype(F32), ((0, 0), (0, 16 - n_ri), (0, 128 - (2 * NA_KC - 1))))
    per_row = pl.pallas_call(
        _na_bias_kernel,
        grid=(4,),
        in_specs=[pl.BlockSpec((1, 16, 128), lambda h: (h, 0, 0)), pl.BlockSpec((128, GW * GW), lambda h: (0, 0))],
        out_specs=pl.BlockSpec((1, 16, GW * GW), lambda h: (h, 0, 0)),
        out_shape=jax.ShapeDtypeStruct((4, 16, GW * GW), F32),
        compiler_params=_cp("arbitrary"),
        name="na_bias",
    )(rpb_p, sel).reshape(4, 16, GW, GW)
    tbl = jnp.stack([per_row[:, o:o + NA_KR] for o in range(NA_KR)], axis=1)
    tbl = jnp.transpose(tbl, (0, 1, 3, 2, 4))
    c_start = jnp.clip(qc - NA_KC // 2, 0, GW - NA_KC)
    ok = (kcol >= c_start) & (kcol < c_start + NA_KC)
    tbl = jnp.where(ok[None, None, :, None, :], tbl, NEG)
    return tbl.reshape(4, NA_KR, GW, NA_KR * GW)


def _neighborhood_attention(p_na, bias_tbl):
    steps = S // GW // NA_ROWS
    return pl.pallas_call(
        _na_kernel,
        grid=(B, steps),
        in_specs=[
            pl.BlockSpec((NA_ROWS * GW, 256), lambda b, r: (b * steps + r, 0)),
            pl.BlockSpec((S, 256), lambda b, r: (b, 1)),
            pl.BlockSpec((S, 256), lambda b, r: (b, 2)),
            pl.BlockSpec((LC, 256), lambda b, r: (CTX_BLK0 + b, 1)),
            pl.BlockSpec((LC, 256), lambda b, r: (CTX_BLK0 + b, 2)),
            pl.BlockSpec((4, NA_KR, GW, NA_KR * GW), lambda b, r: (0, 0, 0, 0)),
        ],
        out_specs=pl.BlockSpec((NA_ROWS * GW, 256), lambda b, r: (b * steps + r, 0)),
        out_shape=jax.ShapeDtypeStruct((NL, 256), BF16),
        compiler_params=_cp("arbitrary", "arbitrary"),
        name="neighborhood_attn",
    )(p_na, p_na, p_na, p_na, p_na, bias_tbl)


def _ctx_attn_kernel(sink_ref, qw_ref, kw_ref, vw_ref, qn_ref, kn_ref, vn_ref, owa_ref, ona_ref):
    q, k, v = qw_ref[...] * SCALE, kw_ref[...], vw_ref[...]
    r1 = lax.broadcasted_iota(jnp.int32, (2 * LC, 1), 0)
    heads = []
    for hk in range(2):
        q2 = jnp.concatenate([q[:, (2 * hk) * HD:(2 * hk + 1) * HD],
                              q[:, (2 * hk + 1) * HD:(2 * hk + 2) * HD]], axis=0)
        hs = slice(hk * HD, (hk + 1) * HD)
        sink = jnp.where(r1 >= LC, sink_ref[2 * hk + 1], sink_ref[2 * hk])
        heads.append(([(_dot_nt(q2, k[:, hs]), _ones_half(v, hk))], sink, hk))
    for hk, res in enumerate(_softmax_pv(heads)):
        o = res[:, hk * HD:(hk + 1) * HD]
        owa_ref[:, (2 * hk) * HD:(2 * hk + 1) * HD] = o[:LC].astype(owa_ref.dtype)
        owa_ref[:, (2 * hk + 1) * HD:(2 * hk + 2) * HD] = o[LC:].astype(owa_ref.dtype)
    q, k, v = qn_ref[...] * SCALE, kn_ref[...], vn_ref[...]
    heads = []
    for h in range(4):
        hs = slice(h * HD, (h + 1) * HD)
        win = slice((h // 2) * 128, (h // 2 + 1) * 128)
        heads.append(([(_dot_nt(q[:, hs], k[:, hs]), _ones_half(v[:, win], h % 2))], None, h % 2))
    outs = _softmax_pv(heads)
    lane = lax.broadcasted_iota(jnp.int32, (LC, 128), 1)
    for hp in range(2):
        pair = jnp.where(lane < HD, outs[2 * hp], outs[2 * hp + 1])
        ona_ref[:, hp * 128:(hp + 1) * 128] = pair.astype(ona_ref.dtype)


def _ctx_attention(p_wa, p_na, sink):
    blk = lambda c: (lambda b: (CTX_BLK0 + b, c))
    return pl.pallas_call(
        _ctx_attn_kernel,
        grid=(B,),
        in_specs=[
            pl.BlockSpec(memory_space=pltpu.SMEM),
            pl.BlockSpec((LC, 256), blk(0)),
            pl.BlockSpec((LC, 128), blk(2)),
            pl.BlockSpec((LC, 128), blk(3)),
            pl.BlockSpec((LC, 256), blk(0)),
            pl.BlockSpec((LC, 256), blk(1)),
            pl.BlockSpec((LC, 256), blk(2)),
        ],
        out_specs=[pl.BlockSpec((LC, 256), lambda b: (b, 0))] * 2,
        out_shape=[jax.ShapeDtypeStruct((NC, 256), BF16)] * 2,
        compiler_params=_cp("arbitrary"),
        name="ctx_attn",
    )(sink, p_wa, p_wa, p_wa, p_na, p_na, p_na)


def _gdn_prep_seq(p_ref, ab_ref, cw, alog, dtb, qkv_ref, gb_ref, row0, length):
    x = p_ref[:, :768].astype(F32)
    u = (_shift_rows(x, -2) * cw[0:1] + _shift_rows(x, -1) * cw[1:2] + x * cw[2:3]
         + _shift_rows(x, 1) * cw[3:4])
    u = _silu(u)
    rows = slice(row0, row0 + length)
    for j in range(8):
        sl = slice(j * HD, (j + 1) * HD)
        xs = u[:, sl]
        nrm = lax.rsqrt(jnp.sum(xs * xs, axis=-1, keepdims=True) + EPS)
        qkv_ref[0, rows, sl] = xs * (nrm * SCALE if j < 4 else nrm)
    qkv_ref[0, rows, 512:768] = u[:, 512:768]
    ab = ab_ref[...]
    g = -jnp.exp(alog) * _softplus(ab + dtb)
    lane = lax.broadcasted_iota(jnp.int32, ab.shape, 1)
    gb_ref[0, rows, :] = jnp.where(lane < 8, g, _sigmoid(ab))


def _gdn_prep_kernel(pl_ref, abl_ref, pc_ref, abc_ref, cw_ref, alog_ref, dtb_ref, qkv_ref, gb_ref):
    cw, alog, dtb = cw_ref[...], alog_ref[...], dtb_ref[...]
    _gdn_prep_seq(pl_ref, abl_ref, cw, alog, dtb, qkv_ref, gb_ref, 0, S)
    _gdn_prep_seq(pc_ref, abc_ref, cw, alog, dtb, qkv_ref, gb_ref, S, LC)


def _gdn_prep(p_gd, p_ab, conv_w, a_log, dt_bias):
    const = lambda b: (0, 0)
    alog = jnp.pad(a_log.reshape(1, 8), ((0, 0), (0, 120)))
    dtb = jnp.pad(dt_bias.reshape(1, 8), ((0, 0), (0, 120)))
    return pl.pallas_call(
        _gdn_prep_kernel,
        grid=(B,),
        in_specs=[
            pl.BlockSpec((S, 1024), lambda b: (b, 0)),
            pl.BlockSpec((S, 128), lambda b: (b, 0)),
            pl.BlockSpec((LC, 1024), lambda b: (CTX_BLK0 + b, 0)),
            pl.BlockSpec((LC, 128), lambda b: (CTX_BLK0 + b, 0)),
            pl.BlockSpec((4, 768), const),
            pl.BlockSpec((1, 128), const),
            pl.BlockSpec((1, 128), const),
        ],
        out_specs=[pl.BlockSpec((1, S + LC, 768), lambda b: (b, 0, 0)),
                   pl.BlockSpec((1, S + LC, 128), lambda b: (b, 0, 0))],
        out_shape=[jax.ShapeDtypeStruct((B, S + LC, 768), F32), jax.ShapeDtypeStruct((B, S + LC, 128), F32)],
        compiler_params=_cp("arbitrary"),
        name="gdn_prep",
    )(p_gd, p_ab, p_gd, p_ab, conv_w, alog, dtb)


def _gdn_chains(xs, gbs, cums, cum_ts, states, masks):
    chains = [(g, h) for g in range(len(xs)) for h in range(4)]
    pre = []
    for g, h in chains:
        dirn = g % 2
        x, gb = xs[g], gbs[g]
        tri, strict = masks[dirn]
        col = dirn * 4 + h
        qh = x[:, h * HD:(h + 1) * HD]
        kh = x[:, 256 + h * HD:256 + (h + 1) * HD]
        vh = x[:, 512 + h * HD:512 + (h + 1) * HD]
        gc = cums[g][:, col:col + 1]
        gc_row = cum_ts[g][col:col + 1, :]
        beta = gb[:, 8 + col:9 + col]
        decay = jnp.exp(jnp.where(tri, gc - gc_row, NEG))
        kb = kh * beta
        eg = jnp.exp(gc)
        g_last = gc[0:1, :] if dirn else gc[CHUNK - 1:CHUNK, :]
        pre.append(dict(strict=strict, decay=decay, kbf=kb.astype(BF16), khf=kh.astype(BF16), qf=qh.astype(BF16),
                        xw=jnp.concatenate([vh * beta, kb * eg], axis=1), qdec=(qh * eg).astype(BF16),
                        kdec=(kh * jnp.exp(g_last - gc)).astype(BF16), gl=jnp.exp(g_last)))
    n = len(chains)
    gram = [_dot_nt(p["kbf"], p["khf"]) for p in pre]
    attn = [_dot_nt(p["qf"], p["khf"]) for p in pre]
    lmat = [jnp.where(p["strict"], g * p["decay"], 0.0) for p, g in zip(pre, gram)]
    attn = [(a * p["decay"]).astype(BF16) for p, a in zip(pre, attn)]
    ii = lax.broadcasted_iota(jnp.int32, (CHUNK, CHUNK), 0)
    jj = lax.broadcasted_iota(jnp.int32, (CHUNK, CHUNK), 1)

    def merged_off_blocks(log_s):
        return ((ii >> (log_s + 1)) == (jj >> (log_s + 1))) & ((ii >> log_s) != (jj >> log_s))

    eye = (ii == jj).astype(F32)
    dinv = [eye - jnp.where(merged_off_blocks(0), lm, 0.0) for lm in lmat]
    for log_s in range(1, 6):
        m = merged_off_blocks(log_s)
        dinv_b = [d.astype(BF16) for d in dinv]
        ld = [_dot(jnp.where(m, lmat[c], 0.0).astype(BF16), dinv_b[c]) for c in range(n)]
        upd = [_dot(dinv_b[c], ld[c].astype(BF16)) for c in range(n)]
        dinv = [dinv[c] - upd[c] for c in range(n)]
    xw = [_dot(dinv[c].astype(BF16), pre[c]["xw"].astype(BF16)) for c in range(n)]
    sts = [states[g][h] for g, h in chains]
    stb = [s.astype(BF16) for s in sts]
    ws = [_dot(xw[c][:, HD:].astype(BF16), stb[c]) for c in range(n)]
    qs = [_dot(pre[c]["qdec"], stb[c]) for c in range(n)]
    u_new = [(xw[c][:, :HD] - ws[c]).astype(BF16) for c in range(n)]
    au = [_dot(attn[c], u_new[c]) for c in range(n)]
    ku = [_dot_tn(pre[c]["kdec"], u_new[c]) for c in range(n)]
    outs = [qs[c] + au[c] for c in range(n)]
    new_states = [sts[c] * pre[c]["gl"] + ku[c] for c in range(n)]
    return outs, new_states


def _gdn_kernel(x0_ref, gb0_ref, x1_ref, gb1_ref, o0_ref, o1_ref, state_ref):
    @pl.when(pl.program_id(0) == 0)
    def _():
        state_ref[...] = jnp.zeros_like(state_ref)

    ii = lax.broadcasted_iota(jnp.int32, (CHUNK, CHUNK), 0)
    jj = lax.broadcasted_iota(jnp.int32, (CHUNK, CHUNK), 1)
    masks = ((jj <= ii, jj < ii), (jj >= ii, jj > ii))
    ones_tri = ((jj <= ii).astype(F32), (jj >= ii).astype(F32))

    def body(bp, carry):
        bs = [bp * GDN_SEQS + i for i in range(GDN_SEQS)]
        states = [[state_ref[b, dirn, h] for h in range(4)] for b in bs for dirn in range(2)]
        xs = [r[b] for b in bs for r in (x0_ref, x1_ref)]
        gbs = [r[b] for b in bs for r in (gb0_ref, gb1_ref)]
        cums = [_dot(ones_tri[g % 2], gbs[g], HI) for g in range(2 * GDN_SEQS)]
        cum_ts = [cm.T for cm in cums]
        outs, new_states = _gdn_chains(xs, gbs, cums, cum_ts, states, masks)
        for c, st in enumerate(new_states):
            state_ref[bs[c // 8], (c // 4) % 2, c % 4] = st
        for i, b in enumerate(bs):
            o0_ref[b] = jnp.concatenate(outs[8 * i:8 * i + 4], axis=1)
            o1_ref[b] = jnp.concatenate(outs[8 * i + 4:8 * i + 8], axis=1)
        return carry

    lax.fori_loop(0, B // GDN_SEQS, body, 0)


def _gdn_scan(qkv, gb):
    nc = (S + LC) // CHUNK
    fwd = lambda c: (0, (c + S // CHUNK) % nc, 0)
    bwd = lambda c: (0, nc - 1 - c, 0)
    return pl.pallas_call(
        _gdn_kernel,
        grid=(nc,),
        in_specs=[
            pl.BlockSpec((B, CHUNK, 768), fwd),
            pl.BlockSpec((B, CHUNK, 128), fwd),
            pl.BlockSpec((B, CHUNK, 768), bwd),
            pl.BlockSpec((B, CHUNK, 128), bwd),
        ],
        out_specs=[pl.BlockSpec((B, CHUNK, 256), fwd), pl.BlockSpec((B, CHUNK, 256), bwd)],
        out_shape=[jax.ShapeDtypeStruct((B, S + LC, 256), F32)] * 2,
        scratch_shapes=[pltpu.VMEM((B, 2, 4, HD, HD), F32)],
        compiler_params=_cp("arbitrary"),
        name="gdn_scan",
    )(qkv, gb, qkv, gb)


def _gdn_gated_norm(o, gate, gain):
    gi = lax.broadcasted_iota(jnp.int32, (256, 256), 0) // HD
    gj = lax.broadcasted_iota(jnp.int32, (256, 256), 1) // HD
    group_mean = jnp.where(gi == gj, 1.0 / HD, 0.0).astype(BF16)
    sq = o * o
    sq_hi = sq.astype(BF16)
    sq_lo = (sq - sq_hi.astype(F32)).astype(BF16)
    ms = _dot(sq_hi, group_mean) + _dot(sq_lo, group_mean)
    return o * lax.rsqrt(ms + EPS) * gain * _silu(gate.astype(F32))


def _pack_bf16_pairs(x):
    n = x.shape[1] // 2
    lo = pltpu.bitcast(x[:, :n].astype(BF16).astype(F32), jnp.uint32)
    hi = pltpu.bitcast(x[:, n:].astype(BF16).astype(F32), jnp.uint32)
    return hi | (lo >> 16)


def _unpack_bf16_pairs(w):
    lo = pltpu.bitcast(w << 16, F32)
    hi = pltpu.bitcast(w & jnp.uint32(0xFFFF0000), F32)
    return jnp.concatenate([lo, hi], axis=1)


def _outproj_kernel(xl_ref, xc_ref, mod_ref, g2_ref, yhl_ref, yhc_ref, ywl_ref, ywc_ref, ynl_ref, ync_ref, o0_ref, o1_ref,
                    gate_ref, ng_ref, wo_ref, wr_ref, eb_ref, xo_ref, h2_ref, h2p_ref, idx_ref, tw_ref, rank_ref, cnt_ref,
                    carry_ref):
    @pl.when(pl.program_id(0) == 0)
    def _():
        carry_ref[...] = jnp.zeros_like(carry_ref)

    m = mod_ref[0]
    y_gd = _gdn_gated_norm(o0_ref[0] + o1_ref[0], gate_ref[...], ng_ref[...]).astype(BF16)
    acc = (_dot(_row_tile(yhl_ref, yhc_ref), wo_ref[0:256, :]) + _dot(_row_tile(ywl_ref, ywc_ref), wo_ref[256:512, :])
           + _dot(_row_tile(ynl_ref, ync_ref), wo_ref[512:768, :]) + _dot(y_gd, wo_ref[768:1024, :]))
    x = _row_tile(xl_ref, xc_ref) + m[2:3] * acc
    xo_ref[...] = x
    h2 = _modulated_norm(x, g2_ref[...], m[3:4], m[4:5])
    h2_ref[...] = h2.astype(h2_ref.dtype)
    h2p_ref[...] = _pack_bf16_pairs(h2)
    wr = wr_ref[...]
    h2_hi, wr_hi = h2.astype(BF16), wr.astype(BF16)
    h2_lo, wr_lo = (h2 - h2_hi.astype(F32)).astype(BF16), (wr - wr_hi.astype(F32)).astype(BF16)
    scores = _sigmoid(_dot(h2_hi, wr_hi) + _dot(h2_lo, wr_hi) + _dot(h2_hi, wr_lo))
    lane = lax.broadcasted_iota(jnp.int32, scores.shape, 1)
    sel = jnp.where(lane < N_EXP, scores + eb_ref[...], -jnp.inf)
    lane_f = lane.astype(F32)
    idx_out = jnp.zeros(scores.shape, F32)
    s_out = jnp.zeros(scores.shape, F32)
    hits = []
    for kk in range(TOP_K):
        mx = jnp.max(sel, axis=-1, keepdims=True)
        idx = jnp.min(jnp.where(sel == mx, lane_f, 128.0), axis=-1, keepdims=True)
        hit = lane_f == idx
        sk = jnp.sum(jnp.where(hit, scores, 0.0), axis=-1, keepdims=True)
        sel = jnp.where(hit, -jnp.inf, sel)
        idx_out = jnp.where(lane == kk, idx, idx_out)
        s_out = jnp.where(lane == kk, sk, s_out)
        hits.append(hit)
    tot = jnp.sum(s_out, axis=-1, keepdims=True)
    tw_ref[...] = s_out / tot * ROUTED_SCALE
    cnt = jnp.zeros(scores.shape, F32)
    for hit in hits:
        cnt = cnt + jnp.where(hit, 1.0, 0.0)
    ri = lax.broadcasted_iota(jnp.int32, (TM, TM), 0)
    rj = lax.broadcasted_iota(jnp.int32, (TM, TM), 1)
    before = _dot(jnp.where(rj < ri, 1.0, 0.0).astype(BF16), cnt.astype(BF16)) + carry_ref[...]
    rank_out = jnp.zeros(scores.shape, F32)
    for kk, hit in enumerate(hits):
        rk = jnp.sum(jnp.where(hit, before, 0.0), axis=-1, keepdims=True)
        rank_out = jnp.where(lane == kk, rk, rank_out)
    idx_ref[...] = idx_out.T[:8].astype(jnp.int32)
    rank_ref[...] = rank_out.T[:8].astype(jnp.int32)
    carry_ref[...] = carry_ref[...] + jnp.sum(cnt, axis=0, keepdims=True)
    cnt_ref[...] = carry_ref[...].astype(jnp.int32)


def _outproj(x_lat, x_ctx, mod, l, g2, ys, gdn, wo, wr, eb, n_tiles):
    row = lambda i: (i, 0)
    const = lambda i: (0, 0)
    lat_tiles = S // TM
    seq = lambda i: (jnp.where(i < NT_LAT, i // lat_tiles, i - NT_LAT), jnp.where(i < NT_LAT, i % lat_tiles, lat_tiles), 0)
    rows = n_tiles * TM
    o0, o1, p_gd, norm_g = gdn
    return pl.pallas_call(
        _outproj_kernel,
        grid=(n_tiles,),
        in_specs=_row_tile_specs() + [
            pl.BlockSpec((1, 6, D), lambda i: (l * 16 + (i * TM) // S, 0, 0)),
            pl.BlockSpec((1, D), const),
            *_row_tile_specs(256), *_row_tile_specs(256), *_row_tile_specs(256),
            pl.BlockSpec((1, TM, 256), seq),
            pl.BlockSpec((1, TM, 256), seq),
            pl.BlockSpec((TM, 256), lambda i: (i, 3)),
            pl.BlockSpec((1, 256), const),
            pl.BlockSpec((D, D), const),
            pl.BlockSpec((D, 128), const),
            pl.BlockSpec((1, 128), const),
        ],
        out_specs=[
            pl.BlockSpec((TM, D), row),
            pl.BlockSpec((TM, D), row),
            pl.BlockSpec((TM, D // 2), row),
            pl.BlockSpec((8, TM), row),
            pl.BlockSpec((TM, 128), row),
            pl.BlockSpec((8, TM), row),
            pl.BlockSpec((1, 128), const),
        ],
        out_shape=[
            jax.ShapeDtypeStruct((rows, D), F32),
            jax.ShapeDtypeStruct((rows, D), BF16),
            jax.ShapeDtypeStruct((rows, D // 2), jnp.uint32),
            jax.ShapeDtypeStruct((n_tiles * 8, TM), jnp.int32),
            jax.ShapeDtypeStruct((rows, 128), F32),
            jax.ShapeDtypeStruct((n_tiles * 8, TM), jnp.int32),
            jax.ShapeDtypeStruct((1, 128), jnp.int32),
        ],
        scratch_shapes=[pltpu.VMEM((1, 128), F32)],
        compiler_params=_cp("arbitrary"),
        name="outproj_router",
    )(x_lat, x_ctx, mod, g2, *[a for pair in ys for a in pair], o0, o1, p_gd, jnp.tile(norm_g, 4)[None, :], wo, wr, eb)


ASG_TILE = TM * TOP_K


def _segment_tables(counts, n_blocks):
    counts = counts[0, :N_EXP]
    padded = (counts + MOE_BLK - 1) // MOE_BLK * MOE_BLK
    pad_end = jnp.cumsum(padded)
    seg_start = pad_end - padded
    starts = jnp.arange(n_blocks, dtype=jnp.int32) * MOE_BLK
    blk_e = jnp.minimum(jnp.sum((pad_end[None, :] <= starts[:, None]).astype(jnp.int32), axis=1), N_EXP - 1)
    seg = jnp.concatenate([seg_start, seg_start + counts, pad_end[-1:]])
    blk = jnp.concatenate([blk_e, pad_end[-1:] // MOE_BLK])
    return seg.astype(jnp.int32), blk.astype(jnp.int32)


def _row_copy_wait(shape_ref, dst_ref, sem):
    pltpu.make_async_copy(shape_ref, dst_ref, sem).wait()


def _zero_fill_rows(zero_ref, xb_ref, start, length, sem):
    plan = []
    aligned = (start + 7) & -8
    for r in range(7):
        plan.append(((start + r < aligned) & (r < length),
                     pltpu.make_async_copy(zero_ref.at[pl.ds(0, 1)], xb_ref.at[pl.ds(start + r, 1)], sem)))
    rest = start + length - aligned
    off = aligned
    for bit in reversed(range(3, MOE_BLK.bit_length() - 1)):
        size = 1 << bit
        take = (rest & size) != 0
        dst = xb_ref.at[pl.ds(pl.multiple_of(off, 8), size)]
        plan.append((take, pltpu.make_async_copy(zero_ref.at[pl.ds(0, size)], dst, sem)))
        off = off + jnp.where(take, size, 0)
    for take, copy in plan:
        pl.when(take)(copy.start)
    for take, copy in plan:
        pl.when(take)(copy.wait)


def _dest_kernel(seg_ref, idx_ref, rank_ref, o_ref):
    idx = idx_ref[...]
    first = jnp.zeros(idx.shape, jnp.int32)
    for e in range(N_EXP):
        first = jnp.where(idx == e, seg_ref[e], first)
    o_ref[...] = first + rank_ref[...]


def _dest_rows(seg, top_idx, rank):
    n_tiles = top_idx.shape[0] // 8
    blk = lambda i, seg: (i, 0)
    grid_spec = pltpu.PrefetchScalarGridSpec(
        num_scalar_prefetch=1,
        grid=(n_tiles // 8,),
        in_specs=[pl.BlockSpec((64, TM), blk), pl.BlockSpec((64, TM), blk)],
        out_specs=pl.BlockSpec((64, TM), blk),
    )
    return pl.pallas_call(
        _dest_kernel,
        grid_spec=grid_spec,
        out_shape=jax.ShapeDtypeStruct(top_idx.shape, jnp.int32),
        compiler_params=_cp("arbitrary"),
        name="moe_dest_rows",
    )(seg, top_idx, rank)


def _dispatch_kernel(seg_ref, dest_ref, h2p_ref, xb_ref, zero_ref, ring_ref, sems, zsem):
    i = pl.program_id(0)
    sem = sems.at[i % 2]
    tile_ref = ring_ref.at[i % 2]
    tile_ref[...] = h2p_ref[...]

    @pl.when(i == 0)
    def _():
        zero_ref[...] = jnp.zeros_like(zero_ref)

    @pl.when(i < N_EXP)
    def _():
        first_pad = seg_ref[N_EXP + i]
        next_start = seg_ref[jnp.where(i == N_EXP - 1, 2 * N_EXP, i + 1)]
        _zero_fill_rows(zero_ref, xb_ref, first_pad, next_start - first_pad, zsem)

    tail_row = seg_ref[2 * N_EXP] + (i - N_EXP) * MOE_BLK

    @pl.when((i >= N_EXP) & (tail_row < xb_ref.shape[0]))
    def _():
        copy = pltpu.make_async_copy(zero_ref, xb_ref.at[pl.ds(pl.multiple_of(tail_row, MOE_BLK), MOE_BLK)], zsem)
        copy.start()
        copy.wait()

    def issue(t, carry):
        src = tile_ref.at[pl.ds(t, 1)]
        for k in range(TOP_K):
            pltpu.make_async_copy(src, xb_ref.at[pl.ds(dest_ref[k, t], 1)], sem).start(priority=k % 2)
        return carry

    lax.fori_loop(0, TM, issue, 0, unroll=True)

    @pl.when(i > 0)
    def _():
        for _ in range(TOP_K):
            _row_copy_wait(h2p_ref, xb_ref.at[pl.ds(0, TM)], sems.at[(i + 1) % 2])

    @pl.when(i == pl.num_programs(0) - 1)
    def _():
        for _ in range(TOP_K):
            _row_copy_wait(h2p_ref, xb_ref.at[pl.ds(0, TM)], sem)


def _dispatch_rows(seg, dest, h2p, n_rows):
    n_tiles = h2p.shape[0] // TM
    assert n_tiles >= 2 * N_EXP
    grid_spec = pltpu.PrefetchScalarGridSpec(
        num_scalar_prefetch=1,
        grid=(n_tiles,),
        in_specs=[
            pl.BlockSpec((8, TM), lambda i, seg: (i, 0), memory_space=pltpu.SMEM),
            pl.BlockSpec((TM, D // 2), lambda i, seg: (i, 0)),
        ],
        out_specs=pl.BlockSpec(memory_space=pl.ANY),
        scratch_shapes=[pltpu.VMEM((MOE_BLK, D // 2), jnp.uint32), pltpu.VMEM((2, TM, D // 2), jnp.uint32),
                        pltpu.SemaphoreType.DMA((2,)), pltpu.SemaphoreType.DMA(())],
    )
    return pl.pallas_call(
        _dispatch_kernel,
        grid_spec=grid_spec,
        out_shape=jax.ShapeDtypeStruct((n_rows, D // 2), jnp.uint32),
        compiler_params=_cp("arbitrary"),
        name="moe_dispatch",
    )(seg, dest, h2p)


def _moe_kernel(be_ref, x_ref, wg_ref, wu_ref, wd_ref, o_ref, wgb_ref, wub_ref, wdb_ref):
    i = pl.program_id(0)
    in_use = i < be_ref[pl.num_programs(0)]

    @pl.when(in_use & ((i == 0) | (be_ref[i] != be_ref[jnp.maximum(i - 1, 0)])))
    def _():
        wgb_ref[...] = wg_ref[0, 0].astype(BF16)
        wub_ref[...] = wu_ref[0, 0].astype(BF16)
        wdb_ref[...] = wd_ref[0, 0].astype(BF16)

    @pl.when(in_use)
    def _():
        x = _unpack_bf16_pairs(x_ref[...]).astype(BF16)
        g = _dot(x, wgb_ref[...])
        u = _dot(x, wub_ref[...])
        hid = (_silu(g) * u).astype(BF16)
        o_ref[...] = _pack_bf16_pairs(_dot(hid, wdb_ref[...]))

    @pl.when(jnp.logical_not(in_use))
    def _():
        o_ref[...] = jnp.zeros_like(o_ref)


def _moe_experts(blk_e, xb, l, wg, wu, wd):
    n_rows = xb.shape[0]
    grid_spec = pltpu.PrefetchScalarGridSpec(
        num_scalar_prefetch=1,
        grid=(n_rows // MOE_BLK,),
        in_specs=[
            pl.BlockSpec((MOE_BLK, D // 2), lambda i, be: (i, 0)),
            pl.BlockSpec((1, 1, D, D_EXP), lambda i, be: (l, be[i], 0, 0)),
            pl.BlockSpec((1, 1, D, D_EXP), lambda i, be: (l, be[i], 0, 0)),
            pl.BlockSpec((1, 1, D_EXP, D), lambda i, be: (l, be[i], 0, 0)),
        ],
        out_specs=pl.BlockSpec((MOE_BLK, D // 2), lambda i, be: (i, 0)),
        scratch_shapes=[pltpu.VMEM((D, D_EXP), BF16), pltpu.VMEM((D, D_EXP), BF16), pltpu.VMEM((D_EXP, D), BF16)],
    )
    return pl.pallas_call(
        _moe_kernel,
        grid_spec=grid_spec,
        out_shape=jax.ShapeDtypeStruct((n_rows, D // 2), jnp.uint32),
        compiler_params=_cp("arbitrary"),
        name="moe_experts",
    )(blk_e, xb, wg, wu, wd)


def _ffn_out_kernel(dest_ref, x_ref, mod_ref, h2_ref, tw_ref, wg_ref, wu_ref, wd_ref, nf_ref, yb_ref, *rest, final):
    if final:
        o_ref, gat_ref, sems = rest
    else:
        proj_in, o_ref, proj_out, gat_ref, sems = rest[:10], rest[10], rest[11:16], rest[16], rest[17]
    s = pl.program_id(0)

    def start_gathers(slot):
        buf = gat_ref.at[slot]

        def issue(t, carry):
            for k in range(TOP_K):
                copy = pltpu.make_async_copy(yb_ref.at[pl.ds(dest_ref[k, t], 1)], buf.at[pl.ds(k * TM + t, 1)],
                                             sems.at[slot])
                copy.start(priority=k % 2)
            return carry

        lax.fori_loop(0, TM, issue, 0, unroll=True)

    def wait_gathers(slot):
        for _ in range(TOP_K):
            _row_copy_wait(yb_ref.at[pl.ds(0, TM)], gat_ref.at[slot, pl.ds(0, TM)], sems.at[slot])

    pl.when(s == 0)(lambda: start_gathers(1))
    done = (s + 1) % 2
    wait_gathers(done)
    start_gathers(s % 2)
    m = mod_ref[0]
    h2 = h2_ref[...]
    hid = (_silu(_dot(h2, wg_ref[...])) * _dot(h2, wu_ref[...])).astype(BF16)
    shared = _dot(hid, wd_ref[...])
    buf = gat_ref.at[done]
    tw = tw_ref[...]
    routed = _unpack_bf16_pairs(buf[0:TM, :]) * tw[:, 0:1]
    for k in range(1, TOP_K):
        routed = routed + _unpack_bf16_pairs(buf[k * TM:(k + 1) * TM, :]) * tw[:, k:k + 1]
    x = x_ref[...] + m[5:6] * (routed + shared)
    if final:
        ms = jnp.mean(x * x, axis=-1, keepdims=True)
        x = x * lax.rsqrt(ms + EPS) * nf_ref[...]
    o_ref[...] = x
    if not final:
        _inproj_tile(x, *proj_in, *proj_out)
    pl.when(s == pl.num_programs(0) - 1)(lambda: wait_gathers(s % 2))


def _ffn_out(dest, xn, mod, l, h2, top_w, wg, wu, wd, norm_f, yb, n_tiles, next_proj=None):
    tile_of = lambda i: jnp.maximum(i - 1, 0)
    row = lambda i: (tile_of(i), 0)
    ahead = lambda i: (jnp.minimum(i, n_tiles - 1), 0)
    const = lambda i: (0, 0)
    in_specs = [
        pl.BlockSpec((8, TM), ahead, memory_space=pltpu.SMEM),
        pl.BlockSpec((TM, D), row),
        pl.BlockSpec((1, 6, D), lambda i: (l * 16 + (tile_of(i) * TM) // S, 0, 0)),
        pl.BlockSpec((TM, D), row),
        pl.BlockSpec((TM, 128), row),
        pl.BlockSpec((D, D_EXP), const),
        pl.BlockSpec((D, D_EXP), const),
        pl.BlockSpec((D_EXP, D), const),
        pl.BlockSpec((1, D), const),
        pl.BlockSpec(memory_space=pl.ANY),
    ]
    args = [dest, xn, mod, h2, top_w, wg, wu, wd, norm_f, yb]
    out_specs = [pl.BlockSpec((TM, D), row)]
    out_shape = [jax.ShapeDtypeStruct((n_tiles * TM, D), F32)]
    if next_proj is not None:
        g, ws, tabs = next_proj
        proj_in, proj_out, proj_shape = _inproj_specs(l + 1, tile_of)
        in_specs += proj_in
        args += [mod, g, *ws, *tabs]
        out_specs += proj_out
        out_shape += proj_shape
    return pl.pallas_call(
        functools.partial(_ffn_out_kernel, final=next_proj is None),
        grid=(n_tiles + 1,),
        in_specs=in_specs,
        out_specs=out_specs,
        scratch_shapes=[pltpu.VMEM((2, ASG_TILE, D // 2), jnp.uint32), pltpu.SemaphoreType.DMA((2,))],
        out_shape=out_shape,
        compiler_params=_cp("arbitrary"),
        name="shared_ffn_residual",
    )(*args)


def kernel(x, c, ctx, c_ctx, w_ada, b_ada, norm1, norm2, norm_f, w_in, w_out, hy_conv, hy_w1, hy_b1, hy_w2, hy_b2, hy_w3, hy_freq, hy_bias, wa_sink, na_rpb, gdn_conv, gdn_a_log, gdn_dt_bias, gdn_norm, moe_router, moe_bias, moe_gate, moe_up, moe_down, sh_gate, sh_up, sh_down):
    depth = w_ada.shape[0]
    x_lat, x_ctx = x.reshape(NL, D), ctx.reshape(NC, D)
    cvec = jnp.concatenate([c, c_ctx[None, :], jnp.zeros((16 - B - 1, D), F32)], axis=0)
    mod = _ada(cvec, w_ada, b_ada).reshape(depth * 16, 6, D)
    rope = _rope_tables()
    dft_lat = _dft_tables(S)
    dft_ctx = _dft_tables(LC)
    o1, o2, o3 = 768, 768 + 512, 768 + 512 + 768

    def proj_params(l):
        wl = w_in[l].astype(BF16)
        ws = (wl[:, :o1], wl[:, o1:o2], wl[:, o2:o3], wl[:, o3:o3 + 1024],
              jnp.pad(wl[:, o3 + 1024:], ((0, 0), (0, 128 - 16))))
        return norm1[l][None, :], ws, rope

    p_hy, p_wa, p_na, p_gd, p_ab = _inproj(x_lat, x_ctx, mod, 0, *proj_params(0))
    for l in range(depth):
        need_ctx = l < depth - 1
        n_tiles = NT_ALL if need_ctx else NT_LAT

        filt = (hy_w1[l], hy_b1[l], hy_w2[l], hy_b2[l], hy_w3[l], hy_freq[l])
        kre, kim = _hyena_filter_spectrum(S, *filt, dft_lat)
        y_hy = _hyena(p_hy, S, 0, hy_conv[l], hy_bias[l], dft_lat, kre, kim)
        y_wa = _window_attention(p_wa, wa_sink[l])
        y_na = _neighborhood_attention(p_na, _na_bias_table(na_rpb[l]))
        if need_ctx:
            kre_c, kim_c = _hyena_filter_spectrum(LC, *filt, dft_ctx)
            yc_hy = _hyena(p_hy, LC, CTX_BLK0, hy_conv[l], hy_bias[l], dft_ctx, kre_c, kim_c)
            yc_wa, yc_na = _ctx_attention(p_wa, p_na, wa_sink[l])
        else:
            yc_hy, yc_wa, yc_na = y_hy, y_wa, y_na
        qkv, gb = _gdn_prep(p_gd, p_ab, gdn_conv[l], gdn_a_log[l], gdn_dt_bias[l])
        gd_fwd, gd_bwd = _gdn_scan(qkv, gb)

        wr = jnp.pad(moe_router[l], ((0, 0), (0, 128 - N_EXP)))
        eb = jnp.pad(moe_bias[l], (0, 128 - N_EXP))[None, :]
        xn, h2, h2p, top_idx, top_w, rank, counts = _outproj(
            x_lat, x_ctx, mod, l, norm2[l][None, :], ((y_hy, yc_hy), (y_wa, yc_wa), (y_na, yc_na)),
            (gd_fwd, gd_bwd, p_gd, gdn_norm[l]),
            w_out[l].astype(BF16), wr, eb, n_tiles)
        n_asg = n_tiles * ASG_TILE
        n_blocks = n_asg // MOE_BLK + N_EXP
        seg, blk_e = _segment_tables(counts, n_blocks)
        dest = _dest_rows(seg, top_idx, rank)
        xb = _dispatch_rows(seg, dest, h2p, n_blocks * MOE_BLK)
        yb = _moe_experts(blk_e, xb, l, moe_gate, moe_up, moe_down)
        res = _ffn_out(dest, xn, mod, l, h2, top_w, sh_gate[l].astype(BF16), sh_up[l].astype(BF16),
                       sh_down[l].astype(BF16), norm_f[None, :], yb, n_tiles,
                       next_proj=proj_params(l + 1) if need_ctx else None)
        if need_ctx:
            x_lat, p_hy, p_wa, p_na, p_gd, p_ab = res
            x_ctx = x_lat[NL:] if l + 2 < depth else x_lat
        else:
            return res[0].reshape(B, S, D)
```

```python
import functools
import math

import jax
import jax.numpy as jnp
from jax import lax
from jax.experimental import pallas as pl
from jax.experimental.pallas import tpu as pltpu

F32 = jnp.float32
BF16 = jnp.bfloat16
HI = lax.Precision.HIGHEST

D = 1024
B = 8
S = 2048
LC = 256
GW = 64
HD = 64
NL = B * S
NC = B * LC
ROWS = NL + NC
TM = 256
NT_LAT = NL // TM
NT_ALL = ROWS // TM
CTX_BLK0 = NL // LC

HY_CH = 256
HY_BANDS = 16
HY_DECAY_MIN = -math.log(1e-2) / 1.5
HY_DECAY_MAX = -math.log(1e-2) / 0.3
WINDOW = 128
NA_KR = 8
NA_KC = 16
CHUNK = 64
N_EXP = 32
TOP_K = 4
D_EXP = 256
ROUTED_SCALE = 2.5
MOE_BLK = 512
WA_BLKS = 8
NA_ROWS = 16
GDN_SEQS = 4
EPS = 1e-6
NEG = -1e30
SCALE = HD ** -0.5
VMEM_LIMIT = 56 * 1024 * 1024


def _cp(*sem):
    return pltpu.CompilerParams(dimension_semantics=tuple(sem), vmem_limit_bytes=VMEM_LIMIT)


def _dot(a, b, precision=None):
    return jnp.dot(a, b, preferred_element_type=F32, precision=precision)


def _dot_nt(a, b, precision=None):
    return lax.dot_general(a, b, (((1,), (1,)), ((), ())), preferred_element_type=F32, precision=precision)


def _dot_tn(a, b, precision=None):
    return lax.dot_general(a, b, (((0,), (0,)), ((), ())), preferred_element_type=F32, precision=precision)


def _sigmoid(x):
    return 1.0 / (1.0 + jnp.exp(-x))


def _silu(x):
    return x * _sigmoid(x)


def _softplus(x):
    return jnp.maximum(x, 0.0) + jnp.log(1.0 + jnp.exp(-jnp.abs(x)))


def _shift_rows(x, d):
    n = x.shape[0]
    if d == 0:
        return x
    y = pltpu.roll(x, (-d) % n, axis=0)
    t = lax.broadcasted_iota(jnp.int32, x.shape, 0)
    ok = (t + d >= 0) & (t + d < n)
    return jnp.where(ok, y, 0.0)


def _ada_kernel(c_ref, w_ref, b_ref, o_ref):
    s = _silu(c_ref[...])
    o_ref[0] = _dot(s.astype(BF16), w_ref[0].astype(BF16)) + b_ref[0]


def _ada(cvec, w_ada, b_ada):
    nl = w_ada.shape[0]
    tn = 1536
    return pl.pallas_call(
        _ada_kernel,
        grid=(nl, 6 * D // tn),
        in_specs=[
            pl.BlockSpec((16, D), lambda l, j: (0, 0)),
            pl.BlockSpec((1, D, tn), lambda l, j: (l, 0, j)),
            pl.BlockSpec((1, 1, tn), lambda l, j: (l, 0, j)),
        ],
        out_specs=pl.BlockSpec((1, 16, tn), lambda l, j: (l, 0, j)),
        out_shape=jax.ShapeDtypeStruct((nl, 16, 6 * D), F32),
        compiler_params=_cp("arbitrary", "arbitrary"),
        name="adaln",
    )(cvec, w_ada, b_ada.reshape(nl, 1, 6 * D))


def _modulated_norm(x, g, shift, scale):
    ms = jnp.mean(x * x, axis=-1, keepdims=True)
    y = x * lax.rsqrt(ms + EPS) * g
    return y * (1.0 + scale) + shift


def _row_tile_specs(width=D):
    return [pl.BlockSpec((TM, width), lambda i: (jnp.minimum(i, NT_LAT - 1), 0)),
            pl.BlockSpec((TM, width), lambda i: (jnp.maximum(i - NT_LAT, 0), 0))]


def _row_tile(xl_ref, xc_ref):
    return jnp.where(pl.program_id(0) < NT_LAT, xl_ref[...], xc_ref[...])


def _inproj_kernel(xl_ref, xc_ref, *refs):
    _inproj_tile(_row_tile(xl_ref, xc_ref), *refs)


def _inproj_tile(x, mod_ref, g_ref, why_ref, wwa_ref, wna_ref, wgd_ref, wab_ref,
                 cos_ref, sa_ref, sb_ref, ohy, owa, ona, ogd, oab):
    m = mod_ref[0]
    h = _modulated_norm(x, g_ref[...], m[0:1], m[1:2]).astype(BF16)
    ohy[...] = _dot(h, why_ref[...]).astype(ohy.dtype)
    ona[...] = _dot(h, wna_ref[...]).astype(ona.dtype)
    ogd[...] = _dot(h, wgd_ref[...]).astype(ogd.dtype)
    oab[...] = _dot(h, wab_ref[...])
    a = _dot(h, wwa_ref[...])
    for c in range(4):
        sl = slice(c * 128, (c + 1) * 128)
        ac = a[:, sl]
        r = (ac * cos_ref[:, sl] + pltpu.roll(ac, 112, axis=1) * sa_ref[:, sl]
             + pltpu.roll(ac, 16, axis=1) * sb_ref[:, sl])
        owa[:, sl] = r.astype(owa.dtype)


def _inproj_specs(l, tile_of):
    row = lambda i: (tile_of(i), 0)
    const = lambda i: (0, 0)
    tab = lambda i: (jnp.where(tile_of(i) < NT_LAT, tile_of(i) % (S // TM), S // TM), 0)
    in_specs = [
        pl.BlockSpec((1, 6, D), lambda i: (l * 16 + (tile_of(i) * TM) // S, 0, 0)),
        pl.BlockSpec((1, D), const),
        pl.BlockSpec((D, 768), const),
        pl.BlockSpec((D, 512), const),
        pl.BlockSpec((D, 768), const),
        pl.BlockSpec((D, 1024), const),
        pl.BlockSpec((D, 128), const),
        pl.BlockSpec((TM, 512), tab),
        pl.BlockSpec((TM, 512), tab),
        pl.BlockSpec((TM, 512), tab),
    ]
    widths = (768, 512, 768, 1024, 128)
    out_specs = [pl.BlockSpec((TM, w), row) for w in widths]
    out_shape = [jax.ShapeDtypeStruct((ROWS, w), F32 if w == 128 else BF16) for w in widths]
    return in_specs, out_specs, out_shape


def _inproj(x_lat, x_ctx, mod, l, g, ws, tabs):
    in_specs, out_specs, out_shape = _inproj_specs(l, lambda i: i)
    return pl.pallas_call(
        _inproj_kernel,
        grid=(NT_ALL,),
        in_specs=_row_tile_specs() + in_specs,
        out_specs=out_specs,
        out_shape=out_shape,
        compiler_params=_cp("arbitrary"),
        name="inproj",
    )(x_lat, x_ctx, mod, g, *ws, *tabs)


def _rope_tables():
    quarter = HD // 4
    pos = jnp.arange(S)
    inv = 10000.0 ** (-jnp.arange(quarter, dtype=F32) / quarter)
    ang_r = (pos // GW).astype(F32)[:, None] * inv[None, :]
    ang_c = (pos % GW).astype(F32)[:, None] * inv[None, :]
    z = jnp.zeros_like(ang_r)
    cos_h = jnp.concatenate([jnp.cos(ang_r)] * 2 + [jnp.cos(ang_c)] * 2, axis=1)
    sa_h = jnp.concatenate([-jnp.sin(ang_r), z, -jnp.sin(ang_c), z], axis=1)
    sb_h = jnp.concatenate([z, jnp.sin(ang_r), z, jnp.sin(ang_c)], axis=1)
    ones = jnp.ones((S, 128), F32)
    zeros = jnp.zeros((S, 128), F32)
    cos_t = jnp.concatenate([jnp.tile(cos_h, (1, 6)), ones], axis=1)
    sa_t = jnp.concatenate([jnp.tile(sa_h, (1, 6)), zeros], axis=1)
    sb_t = jnp.concatenate([jnp.tile(sb_h, (1, 6)), zeros], axis=1)
    ident = jnp.ones((TM, 512), F32)
    none = jnp.zeros((TM, 512), F32)
    return (jnp.concatenate([cos_t, ident], axis=0), jnp.concatenate([sa_t, none], axis=0),
            jnp.concatenate([sb_t, none], axis=0))


def _dft_tables(length):
    n = 2 * length
    f = jnp.arange(length, dtype=jnp.int32)
    m = ((2 * f[:, None] + 1) * (2 * f[None, :] + 1)) % (4 * n)
    th = m.astype(F32) * (2.0 * math.pi / (4 * n))
    phi = (2 * f + 1).astype(F32) * (math.pi / (2 * n))
    return (jnp.cos(th).astype(BF16), jnp.sin(th).astype(BF16),
            jnp.cos(phi)[:, None], jnp.sin(phi)[:, None])


def _hyena_features(length):
    t = jnp.arange(length, dtype=F32)
    t_norm = t / max(length - 1, 1)
    bands = jnp.linspace(1e-4, HY_BANDS - 1, HY_BANDS, dtype=F32)
    band_lane = jnp.concatenate([jnp.zeros((1,), F32), bands, bands, jnp.zeros((128 - 1 - 2 * HY_BANDS,), F32)])
    ang = (2.0 * math.pi / length) * t[:, None] * band_lane[None, :]
    lane = jnp.arange(128)[None, :]
    z = jnp.where(lane == 0, t_norm[:, None],
                  jnp.where(lane <= HY_BANDS, jnp.cos(ang), jnp.where(lane <= 2 * HY_BANDS, -jnp.sin(ang), 0.0)))
    decay = jnp.tile(jnp.linspace(HY_DECAY_MIN, HY_DECAY_MAX, HY_CH, dtype=F32), 2)
    return z, jnp.exp(-t_norm[:, None] * decay[None, :])


def _hyfilt_kernel(z_ref, w1_ref, b1_ref, w2_ref, b2_ref, w3_ref, fr_ref, dec_ref,
                   c_ref, s_ref, cp_ref, sp_ref, kre_ref, kim_ref, pq_ref, *, length):
    @pl.when(pl.program_id(0) == 0)
    def _():
        fr = fr_ref[...]
        h = jnp.sin(fr * (_dot(z_ref[...], w1_ref[...], HI) + b1_ref[...]))
        h = jnp.sin(fr * (_dot(h, w2_ref[...], HI) + b2_ref[...]))
        h = _dot(h, w3_ref[...], HI) * dec_ref[...]
        hf = h[:, :HY_CH]
        t = lax.broadcasted_iota(jnp.int32, (length, HY_CH), 0)
        hb = jnp.where(t == 0, 0.0, h[:, HY_CH:])
        pq_ref[:, :HY_CH] = (hf + hb).astype(BF16)
        pq_ref[:, HY_CH:] = (hb - hf).astype(BF16)

    pq = pq_ref[...]
    cpq = _dot(c_ref[...], pq)
    spq = _dot(s_ref[...], pq)
    cphi, sphi = cp_ref[...], sp_ref[...]
    norm = 1.0 / length
    kre_ref[...] = (cphi * cpq[:, :HY_CH] + sphi * spq[:, :HY_CH]) * norm
    kim_ref[...] = (cphi * spq[:, HY_CH:] - sphi * cpq[:, HY_CH:]) * norm


def _hyena_filter_spectrum(length, w1, b1, w2, b2, w3, freq, dft):
    z, dec = _hyena_features(length)
    c_m, s_m, cphi, sphi = dft
    w1p = jnp.pad(w1, ((0, 128 - w1.shape[0]), (0, 0)))
    tf = min(512, length)
    const = lambda j: (0, 0)
    blk = lambda j: (j, 0)
    return pl.pallas_call(
        functools.partial(_hyfilt_kernel, length=length),
        grid=(length // tf,),
        in_specs=[
            pl.BlockSpec((length, 128), const),
            pl.BlockSpec((128, 64), const),
            pl.BlockSpec((1, 64), const),
            pl.BlockSpec((64, 64), const),
            pl.BlockSpec((1, 64), const),
            pl.BlockSpec((64, 2 * HY_CH), const),
            pl.BlockSpec((1, 64), const),
            pl.BlockSpec((length, 2 * HY_CH), const),
            pl.BlockSpec((tf, length), blk),
            pl.BlockSpec((tf, length), blk),
            pl.BlockSpec((tf, 1), blk),
            pl.BlockSpec((tf, 1), blk),
        ],
        out_specs=[pl.BlockSpec((tf, HY_CH), blk)] * 2,
        out_shape=[jax.ShapeDtypeStruct((length, HY_CH), F32)] * 2,
        scratch_shapes=[pltpu.VMEM((length, 2 * HY_CH), BF16)],
        compiler_params=_cp("arbitrary"),
        name=f"hyena_filter_{length}",
    )(z, w1p, b1[None, :], w2, b2[None, :], w3, freq[None, :], dec, c_m, s_m, cphi, sphi)


def _short_conv3(x_ref, cw):
    x = x_ref[...].astype(F32)
    return _shift_rows(x, -1) * cw[0:1] + x * cw[1:2] + _shift_rows(x, 1) * cw[2:3]


def _hyena_kernel(p0_ref, p1_ref, p2_ref, cw_ref, bias_ref, c_ref, s_ref, kre_ref, kim_ref, o_ref):
    cw = cw_ref[...]
    x0 = _short_conv3(p0_ref, cw[:, :HY_CH])
    k = _short_conv3(p1_ref, cw[:, HY_CH:2 * HY_CH]) * _short_conv3(p2_ref, cw[:, 2 * HY_CH:])
    kb = k.astype(BF16)
    a = _dot(c_ref[...], kb)
    b = _dot(s_ref[...], kb)
    kre, kim = kre_ref[...], kim_ref[...]
    yre = (a * kre + b * kim).astype(BF16)
    yim = (b * kre - a * kim).astype(BF16)
    y = _dot(c_ref[...], yre) + _dot(s_ref[...], yim)
    o_ref[...] = (x0 * (y + k * bias_ref[...])).astype(o_ref.dtype)


def _hyena(p_hy, length, blk0, conv_w, bias, dft, kre, kim):
    c_m, s_m = dft[0], dft[1]
    const = lambda b: (0, 0)
    once = pl.Buffered(1)
    in_specs = [
        pl.BlockSpec((length, HY_CH), lambda b: (blk0 + b, 0)),
        pl.BlockSpec((length, HY_CH), lambda b: (blk0 + b, 1)),
        pl.BlockSpec((length, HY_CH), lambda b: (blk0 + b, 2)),
        pl.BlockSpec((3, 768), const),
        pl.BlockSpec((1, HY_CH), const),
        pl.BlockSpec((length, length), const, pipeline_mode=once),
        pl.BlockSpec((length, length), const, pipeline_mode=once),
        pl.BlockSpec((length, HY_CH), const, pipeline_mode=once),
        pl.BlockSpec((length, HY_CH), const, pipeline_mode=once),
    ]
    return pl.pallas_call(
        _hyena_kernel,
        grid=(B,),
        in_specs=in_specs,
        out_specs=pl.BlockSpec((length, HY_CH), lambda b: (b, 0)),
        out_shape=jax.ShapeDtypeStruct((B * length, HY_CH), BF16),
        compiler_params=_cp("arbitrary"),
        name=f"hyena_{length}",
    )(p_hy, p_hy, p_hy, conv_w, bias[None, :], c_m, s_m, kre, kim)


def _ones_half(vwin, half):
    lane = lax.broadcasted_iota(jnp.int32, vwin.shape, 1)
    keep = (lane < HD) if half == 0 else (lane >= HD)
    return jnp.where(keep, vwin, jnp.ones_like(vwin))


def _softmax_pv(heads):
    probs, maxes = [], []
    for parts, sink, _ in heads:
        m = None
        for s, _ in parts:
            mi = jnp.max(s, axis=-1, keepdims=True)
            m = mi if m is None else jnp.maximum(m, mi)
        if sink is not None:
            m = jnp.maximum(m, sink)
        probs.append([jnp.exp((s - m).astype(BF16)) for s, _ in parts])
        maxes.append(m)
    outs = []
    for (parts, sink, half), es, m in zip(heads, probs, maxes):
        acc = None
        for e, (_, v) in zip(es, parts):
            o = _dot(e, v)
            acc = o if acc is None else acc + o
        den = acc[:, HD:HD + 1] if half == 0 else acc[:, 0:1]
        if sink is not None:
            den = den + jnp.exp(sink - m)
        outs.append(acc / den)
    return outs


def _wa_kernel(sink_ref, q_ref, k_ref, v_ref, kc_ref, vc_ref, o_ref):
    kc, vc = kc_ref[...], vc_ref[...]
    vcs = [_ones_half(vc, hk) for hk in range(2)]
    rr = lax.broadcasted_iota(jnp.int32, (256, 384), 0)
    r1 = lax.broadcasted_iota(jnp.int32, (256, 1), 0)
    heads = []
    for j in range(WA_BLKS):
        n = pl.program_id(1) * WA_BLKS + j
        start = pl.multiple_of(jnp.clip((n - 1) * 128, 0, S - 384), 128)
        kw = k_ref[pl.ds(start, 384), :]
        vw = v_ref[pl.ds(start, 384), :]
        q = q_ref[j * 128:(j + 1) * 128, :] * SCALE
        qpos = n * 128 + jnp.where(rr >= 128, rr - 128, rr)
        kpos = start + lax.broadcasted_iota(jnp.int32, (256, 384), 1)
        valid = jnp.abs(qpos - kpos) <= WINDOW
        for hk in range(2):
            q2 = jnp.concatenate([q[:, (2 * hk) * HD:(2 * hk + 1) * HD],
                                  q[:, (2 * hk + 1) * HD:(2 * hk + 2) * HD]], axis=0)
            hs = slice(hk * HD, (hk + 1) * HD)
            s_loc = jnp.where(valid, _dot_nt(q2, kw[:, hs]), NEG)
            s_ctx = _dot_nt(q2, kc[:, hs])
            sink = jnp.where(r1 >= 128, sink_ref[2 * hk + 1], sink_ref[2 * hk])
            heads.append(([(s_ctx, vcs[hk]), (s_loc, _ones_half(vw, hk))], sink, hk))
    for i, res in enumerate(_softmax_pv(heads)):
        j, hk = divmod(i, 2)
        o = res[:, hk * HD:(hk + 1) * HD]
        rows = slice(j * 128, (j + 1) * 128)
        o_ref[rows, (2 * hk) * HD:(2 * hk + 1) * HD] = o[:128].astype(o_ref.dtype)
        o_ref[rows, (2 * hk + 1) * HD:(2 * hk + 2) * HD] = o[128:].astype(o_ref.dtype)


def _window_attention(p_wa, sink):
    nb = S // (128 * WA_BLKS)
    return pl.pallas_call(
        _wa_kernel,
        grid=(B, nb),
        in_specs=[
            pl.BlockSpec(memory_space=pltpu.SMEM),
            pl.BlockSpec((128 * WA_BLKS, 256), lambda b, n: (b * nb + n, 0)),
            pl.BlockSpec((S, 128), lambda b, n: (b, 2)),
            pl.BlockSpec((S, 128), lambda b, n: (b, 3)),
            pl.BlockSpec((LC, 128), lambda b, n: (CTX_BLK0 + b, 2)),
            pl.BlockSpec((LC, 128), lambda b, n: (CTX_BLK0 + b, 3)),
        ],
        out_specs=pl.BlockSpec((128 * WA_BLKS, 256), lambda b, n: (b * nb + n, 0)),
        out_shape=jax.ShapeDtypeStruct((NL, 256), BF16),
        compiler_params=_cp("arbitrary", "arbitrary"),
        name="window_attn",
    )(sink, p_wa, p_wa, p_wa, p_wa, p_wa)


def _na_kernel(q_ref, k_ref, v_ref, kc_ref, vc_ref, bias_ref, o_ref):
    kc, vc = kc_ref[...], vc_ref[...]
    vcs = [_ones_half(vc[:, (h // 2) * 128:(h // 2 + 1) * 128], h % 2) for h in range(4)]
    heads = []
    for j in range(NA_ROWS):
        r = pl.program_id(1) * NA_ROWS + j
        first = jnp.clip(r - NA_KR // 2, 0, S // GW - NA_KR)
        start = pl.multiple_of(first * GW, GW)
        off = first - r + NA_KR - 1
        kw = k_ref[pl.ds(start, NA_KR * GW), :]
        vw = v_ref[pl.ds(start, NA_KR * GW), :]
        q = q_ref[j * GW:(j + 1) * GW, :] * SCALE
        for h in range(4):
            hs = slice(h * HD, (h + 1) * HD)
            win = slice((h // 2) * 128, (h // 2 + 1) * 128)
            s_loc = _dot_nt(q[:, hs], kw[:, hs]) + bias_ref[h, off]
            s_ctx = _dot_nt(q[:, hs], kc[:, hs])
            heads.append(([(s_ctx, vcs[h]), (s_loc, _ones_half(vw[:, win], h % 2))], None, h % 2))
    outs = _softmax_pv(heads)
    lane = lax.broadcasted_iota(jnp.int32, (GW, 128), 1)
    for j in range(NA_ROWS):
        for hp in range(2):
            pair = jnp.where(lane < HD, outs[4 * j + 2 * hp], outs[4 * j + 2 * hp + 1])
            o_ref[j * GW:(j + 1) * GW, hp * 128:(hp + 1) * 128] = pair.astype(o_ref.dtype)


def _na_bias_kernel(rpb_ref, sel_ref, o_ref):
    o_ref[0] = _dot(rpb_ref[0], sel_ref[...], HI)


def _na_bias_table(rpb):
    qc = jnp.arange(GW)[:, None]
    kcol = jnp.arange(GW)[None, :]
    ci = jnp.clip(kcol - qc + NA_KC - 1, 0, 2 * NA_KC - 2).reshape(1, GW * GW)
    sel = (jnp.arange(128)[:, None] == ci).astype(F32)
    n_ri = 2 * NA_KR - 1
    rpb_p = jnp.pad(rpb.astype(F32), ((0, 0), (0, 16 - n_ri), (0, 128 - (2 * NA_KC - 1))))
    per_row = pl.pallas_call(
        _na_bias_kernel,
        grid=(4,),
        in_specs=[pl.BlockSpec((1, 16, 128), lambda h: (h, 0, 0)), pl.BlockSpec((128, GW * GW), lambda h: (0, 0))],
        out_specs=pl.BlockSpec((1, 16, GW * GW), lambda h: (h, 0, 0)),
        out_shape=jax.ShapeDtypeStruct((4, 16, GW * GW), F32),
        compiler_params=_cp("arbitrary"),
        name="na_bias",
    )(rpb_p, sel).reshape(4, 16, GW, GW)
    tbl = jnp.stack([per_row[:, o:o + NA_KR] for o in range(NA_KR)], axis=1)
    tbl = jnp.transpose(tbl, (0, 1, 3, 2, 4))
    c_start = jnp.clip(qc - NA_KC // 2, 0, GW - NA_KC)
    ok = (kcol >= c_start) & (kcol < c_start + NA_KC)
    tbl = jnp.where(ok[None, None, :, None, :], tbl, NEG)
    return tbl.reshape(4, NA_KR, GW, NA_KR * GW)


def _neighborhood_attention(p_na, bias_tbl):
    steps = S // GW // NA_ROWS
    return pl.pallas_call(
        _na_kernel,
        grid=(B, steps),
        in_specs=[
            pl.BlockSpec((NA_ROWS * GW, 256), lambda b, r: (b * steps + r, 0)),
            pl.BlockSpec((S, 256), lambda b, r: (b, 1)),
            pl.BlockSpec((S, 256), lambda b, r: (b, 2)),
            pl.BlockSpec((LC, 256), lambda b, r: (CTX_BLK0 + b, 1)),
            pl.BlockSpec((LC, 256), lambda b, r: (CTX_BLK0 + b, 2)),
            pl.BlockSpec((4, NA_KR, GW, NA_KR * GW), lambda b, r: (0, 0, 0, 0)),
        ],
        out_specs=pl.BlockSpec((NA_ROWS * GW, 256), lambda b, r: (b * steps + r, 0)),
        out_shape=jax.ShapeDtypeStruct((NL, 256), BF16),
        compiler_params=_cp("arbitrary", "arbitrary"),
        name="neighborhood_attn",
    )(p_na, p_na, p_na, p_na, p_na, bias_tbl)


def _ctx_attn_kernel(sink_ref, qw_ref, kw_ref, vw_ref, qn_ref, kn_ref, vn_ref, owa_ref, ona_ref):
    q, k, v = qw_ref[...] * SCALE, kw_ref[...], vw_ref[...]
    r1 = lax.broadcasted_iota(jnp.int32, (2 * LC, 1), 0)
    heads = []
    for hk in range(2):
        q2 = jnp.concatenate([q[:, (2 * hk) * HD:(2 * hk + 1) * HD],
                              q[:, (2 * hk + 1) * HD:(2 * hk + 2) * HD]], axis=0)
        hs = slice(hk * HD, (hk + 1) * HD)
        sink = jnp.where(r1 >= LC, sink_ref[2 * hk + 1], sink_ref[2 * hk])
        heads.append(([(_dot_nt(q2, k[:, hs]), _ones_half(v, hk))], sink, hk))
    for hk, res in enumerate(_softmax_pv(heads)):
        o = res[:, hk * HD:(hk + 1) * HD]
        owa_ref[:, (2 * hk) * HD:(2 * hk + 1) * HD] = o[:LC].astype(owa_ref.dtype)
        owa_ref[:, (2 * hk + 1) * HD:(2 * hk + 2) * HD] = o[LC:].astype(owa_ref.dtype)
    q, k, v = qn_ref[...] * SCALE, kn_ref[...], vn_ref[...]
    heads = []
    for h in range(4):
        hs = slice(h * HD, (h + 1) * HD)
        win = slice((h // 2) * 128, (h // 2 + 1) * 128)
        heads.append(([(_dot_nt(q[:, hs], k[:, hs]), _ones_half(v[:, win], h % 2))], None, h % 2))
    outs = _softmax_pv(heads)
    lane = lax.broadcasted_iota(jnp.int32, (LC, 128), 1)
    for hp in range(2):
        pair = jnp.where(lane < HD, outs[2 * hp], outs[2 * hp + 1])
        ona_ref[:, hp * 128:(hp + 1) * 128] = pair.astype(ona_ref.dtype)


def _ctx_attention(p_wa, p_na, sink):
    blk = lambda c: (lambda b: (CTX_BLK0 + b, c))
    return pl.pallas_call(
        _ctx_attn_kernel,
        grid=(B,),
        in_specs=[
            pl.BlockSpec(memory_space=pltpu.SMEM),
            pl.BlockSpec((LC, 256), blk(0)),
            pl.BlockSpec((LC, 128), blk(2)),
            pl.BlockSpec((LC, 128), blk(3)),
            pl.BlockSpec((LC, 256), blk(0)),
            pl.BlockSpec((LC, 256), blk(1)),
            pl.BlockSpec((LC, 256), blk(2)),
        ],
        out_specs=[pl.BlockSpec((LC, 256), lambda b: (b, 0))] * 2,
        out_shape=[jax.ShapeDtypeStruct((NC, 256), BF16)] * 2,
        compiler_params=_cp("arbitrary"),
        name="ctx_attn",
    )(sink, p_wa, p_wa, p_wa, p_na, p_na, p_na)


def _gdn_prep_seq(p_ref, ab_ref, cw, alog, dtb, qkv_ref, gb_ref, row0, length):
    x = p_ref[:, :768].astype(F32)
    u = (_shift_rows(x, -2) * cw[0:1] + _shift_rows(x, -1) * cw[1:2] + x * cw[2:3]
         + _shift_rows(x, 1) * cw[3:4])
    u = _silu(u)
    rows = slice(row0, row0 + length)
    for j in range(8):
        sl = slice(j * HD, (j + 1) * HD)
        xs = u[:, sl]
        nrm = lax.rsqrt(jnp.sum(xs * xs, axis=-1, keepdims=True) + EPS)
        qkv_ref[0, rows, sl] = xs * (nrm * SCALE if j < 4 else nrm)
    qkv_ref[0, rows, 512:768] = u[:, 512:768]
    ab = ab_ref[...]
    g = -jnp.exp(alog) * _softplus(ab + dtb)
    lane = lax.broadcasted_iota(jnp.int32, ab.shape, 1)
    gb_ref[0, rows, :] = jnp.where(lane < 8, g, _sigmoid(ab))


def _gdn_prep_kernel(pl_ref, abl_ref, pc_ref, abc_ref, cw_ref, alog_ref, dtb_ref, qkv_ref, gb_ref):
    cw, alog, dtb = cw_ref[...], alog_ref[...], dtb_ref[...]
    _gdn_prep_seq(pl_ref, abl_ref, cw, alog, dtb, qkv_ref, gb_ref, 0, S)
    _gdn_prep_seq(pc_ref, abc_ref, cw, alog, dtb, qkv_ref, gb_ref, S, LC)


def _gdn_prep(p_gd, p_ab, conv_w, a_log, dt_bias):
    const = lambda b: (0, 0)
    alog = jnp.pad(a_log.reshape(1, 8), ((0, 0), (0, 120)))
    dtb = jnp.pad(dt_bias.reshape(1, 8), ((0, 0), (0, 120)))
    return pl.pallas_call(
        _gdn_prep_kernel,
        grid=(B,),
        in_specs=[
            pl.BlockSpec((S, 1024), lambda b: (b, 0)),
            pl.BlockSpec((S, 128), lambda b: (b, 0)),
            pl.BlockSpec((LC, 1024), lambda b: (CTX_BLK0 + b, 0)),
            pl.BlockSpec((LC, 128), lambda b: (CTX_BLK0 + b, 0)),
            pl.BlockSpec((4, 768), const),
            pl.BlockSpec((1, 128), const),
            pl.BlockSpec((1, 128), const),
        ],
        out_specs=[pl.BlockSpec((1, S + LC, 768), lambda b: (b, 0, 0)),
                   pl.BlockSpec((1, S + LC, 128), lambda b: (b, 0, 0))],
        out_shape=[jax.ShapeDtypeStruct((B, S + LC, 768), F32), jax.ShapeDtypeStruct((B, S + LC, 128), F32)],
        compiler_params=_cp("arbitrary"),
        name="gdn_prep",
    )(p_gd, p_ab, p_gd, p_ab, conv_w, alog, dtb)


def _gdn_chains(xs, gbs, cums, cum_ts, states, masks):
    chains = [(g, h) for g in range(len(xs)) for h in range(4)]
    pre = []
    for g, h in chains:
        dirn = g % 2
        x, gb = xs[g], gbs[g]
        tri, strict = masks[dirn]
        col = dirn * 4 + h
        qh = x[:, h * HD:(h + 1) * HD]
        kh = x[:, 256 + h * HD:256 + (h + 1) * HD]
        vh = x[:, 512 + h * HD:512 + (h + 1) * HD]
        gc = cums[g][:, col:col + 1]
        gc_row = cum_ts[g][col:col + 1, :]
        beta = gb[:, 8 + col:9 + col]
        decay = jnp.exp(jnp.where(tri, gc - gc_row, NEG))
        kb = kh * beta
        eg = jnp.exp(gc)
        g_last = gc[0:1, :] if dirn else gc[CHUNK - 1:CHUNK, :]
        pre.append(dict(strict=strict, decay=decay, kbf=kb.astype(BF16), khf=kh.astype(BF16), qf=qh.astype(BF16),
                        xw=jnp.concatenate([vh * beta, kb * eg], axis=1), qdec=(qh * eg).astype(BF16),
                        kdec=(kh * jnp.exp(g_last - gc)).astype(BF16), gl=jnp.exp(g_last)))
    n = len(chains)
    gram = [_dot_nt(p["kbf"], p["khf"]) for p in pre]
    attn = [_dot_nt(p["qf"], p["khf"]) for p in pre]
    lmat = [jnp.where(p["strict"], g * p["decay"], 0.0) for p, g in zip(pre, gram)]
    attn = [(a * p["decay"]).astype(BF16) for p, a in zip(pre, attn)]
    ii = lax.broadcasted_iota(jnp.int32, (CHUNK, CHUNK), 0)
    jj = lax.broadcasted_iota(jnp.int32, (CHUNK, CHUNK), 1)

    def merged_off_blocks(log_s):
        return ((ii >> (log_s + 1)) == (jj >> (log_s + 1))) & ((ii >> log_s) != (jj >> log_s))

    eye = (ii == jj).astype(F32)
    dinv = [eye - jnp.where(merged_off_blocks(0), lm, 0.0) for lm in lmat]
    for log_s in range(1, 6):
        m = merged_off_blocks(log_s)
        dinv_b = [d.astype(BF16) for d in dinv]
        ld = [_dot(jnp.where(m, lmat[c], 0.0).astype(BF16), dinv_b[c]) for c in range(n)]
        upd = [_dot(dinv_b[c], ld[c].astype(BF16)) for c in range(n)]
        dinv = [dinv[c] - upd[c] for c in range(n)]
    xw = [_dot(dinv[c].astype(BF16), pre[c]["xw"].astype(BF16)) for c in range(n)]
    sts = [states[g][h] for g, h in chains]
    stb = [s.astype(BF16) for s in sts]
    ws = [_dot(xw[c][:, HD:].astype(BF16), stb[c]) for c in range(n)]
    qs = [_dot(pre[c]["qdec"], stb[c]) for c in range(n)]
    u_new = [(xw[c][:, :HD] - ws[c]).astype(BF16) for c in range(n)]
    au = [_dot(attn[c], u_new[c]) for c in range(n)]
    ku = [_dot_tn(pre[c]["kdec"], u_new[c]) for c in range(n)]
    outs = [qs[c] + au[c] for c in range(n)]
    new_states = [sts[c] * pre[c]["gl"] + ku[c] for c in range(n)]
    return outs, new_states


def _gdn_kernel(x0_ref, gb0_ref, x1_ref, gb1_ref, o0_ref, o1_ref, state_ref):
    @pl.when(pl.program_id(0) == 0)
    def _():
        state_ref[...] = jnp.zeros_like(state_ref)

    ii = lax.broadcasted_iota(jnp.int32, (CHUNK, CHUNK), 0)
    jj = lax.broadcasted_iota(jnp.int32, (CHUNK, CHUNK), 1)
    masks = ((jj <= ii, jj < ii), (jj >= ii, jj > ii))
    ones_tri = ((jj <= ii).astype(F32), (jj >= ii).astype(F32))

    def body(bp, carry):
        bs = [bp * GDN_SEQS + i for i in range(GDN_SEQS)]
        states = [[state_ref[b, dirn, h] for h in range(4)] for b in bs for dirn in range(2)]
        xs = [r[b] for b in bs for r in (x0_ref, x1_ref)]
        gbs = [r[b] for b in bs for r in (gb0_ref, gb1_ref)]
        cums = [_dot(ones_tri[g % 2], gbs[g], HI) for g in range(2 * GDN_SEQS)]
        cum_ts = [cm.T for cm in cums]
        outs, new_states = _gdn_chains(xs, gbs, cums, cum_ts, states, masks)
        for c, st in enumerate(new_states):
            state_ref[bs[c // 8], (c // 4) % 2, c % 4] = st
        for i, b in enumerate(bs):
            o0_ref[b] = jnp.concatenate(outs[8 * i:8 * i + 4], axis=1)
            o1_ref[b] = jnp.concatenate(outs[8 * i + 4:8 * i + 8], axis=1)
        return carry

    lax.fori_loop(0, B // GDN_SEQS, body, 0)


def _gdn_scan(qkv, gb):
    nc = (S + LC) // CHUNK
    fwd = lambda c: (0, (c + S // CHUNK) % nc, 0)
    bwd = lambda c: (0, nc - 1 - c, 0)
    return pl.pallas_call(
        _gdn_kernel,
        grid=(nc,),
        in_specs=[
            pl.BlockSpec((B, CHUNK, 768), fwd),
            pl.BlockSpec((B, CHUNK, 128), fwd),
            pl.BlockSpec((B, CHUNK, 768), bwd),
            pl.BlockSpec((B, CHUNK, 128), bwd),
        ],
        out_specs=[pl.BlockSpec((B, CHUNK, 256), fwd), pl.BlockSpec((B, CHUNK, 256), bwd)],
        out_shape=[jax.ShapeDtypeStruct((B, S + LC, 256), F32)] * 2,
        scratch_shapes=[pltpu.VMEM((B, 2, 4, HD, HD), F32)],
        compiler_params=_cp("arbitrary"),
        name="gdn_scan",
    )(qkv, gb, qkv, gb)


def _gdn_gated_norm(o, gate, gain):
    gi = lax.broadcasted_iota(jnp.int32, (256, 256), 0) // HD
    gj = lax.broadcasted_iota(jnp.int32, (256, 256), 1) // HD
    group_mean = jnp.where(gi == gj, 1.0 / HD, 0.0).astype(BF16)
    sq = o * o
    sq_hi = sq.astype(BF16)
    sq_lo = (sq - sq_hi.astype(F32)).astype(BF16)
    ms = _dot(sq_hi, group_mean) + _dot(sq_lo, group_mean)
    return o * lax.rsqrt(ms + EPS) * gain * _silu(gate.astype(F32))


def _pack_bf16_pairs(x):
    n = x.shape[1] // 2
    lo = pltpu.bitcast(x[:, :n].astype(BF16).astype(F32), jnp.uint32)
    hi = pltpu.bitcast(x[:, n:].astype(BF16).astype(F32), jnp.uint32)
    return hi | (lo >> 16)


def _unpack_bf16_pairs(w):
    lo = pltpu.bitcast(w << 16, F32)
    hi = pltpu.bitcast(w & jnp.uint32(0xFFFF0000), F32)
    return jnp.concatenate([lo, hi], axis=1)


def _outproj_kernel(xl_ref, xc_ref, mod_ref, g2_ref, yhl_ref, yhc_ref, ywl_ref, ywc_ref, ynl_ref, ync_ref, o0_ref, o1_ref,
                    gate_ref, ng_ref, wo_ref, wr_ref, eb_ref, xo_ref, h2_ref, h2p_ref, idx_ref, tw_ref, rank_ref, cnt_ref,
                    carry_ref):
    @pl.when(pl.program_id(0) == 0)
    def _():
        carry_ref[...] = jnp.zeros_like(carry_ref)

    m = mod_ref[0]
    y_gd = _gdn_gated_norm(o0_ref[0] + o1_ref[0], gate_ref[...], ng_ref[...]).astype(BF16)
    acc = (_dot(_row_tile(yhl_ref, yhc_ref), wo_ref[0:256, :]) + _dot(_row_tile(ywl_ref, ywc_ref), wo_ref[256:512, :])
           + _dot(_row_tile(ynl_ref, ync_ref), wo_ref[512:768, :]) + _dot(y_gd, wo_ref[768:1024, :]))
    x = _row_tile(xl_ref, xc_ref) + m[2:3] * acc
    xo_ref[...] = x
    h2 = _modulated_norm(x, g2_ref[...], m[3:4], m[4:5])
    h2_ref[...] = h2.astype(h2_ref.dtype)
    h2p_ref[...] = _pack_bf16_pairs(h2)
    wr = wr_ref[...]
    h2_hi, wr_hi = h2.astype(BF16), wr.astype(BF16)
    h2_lo, wr_lo = (h2 - h2_hi.astype(F32)).astype(BF16), (wr - wr_hi.astype(F32)).astype(BF16)
    scores = _sigmoid(_dot(h2_hi, wr_hi) + _dot(h2_lo, wr_hi) + _dot(h2_hi, wr_lo))
    lane = lax.broadcasted_iota(jnp.int32, scores.shape, 1)
    sel = jnp.where(lane < N_EXP, scores + eb_ref[...], -jnp.inf)
    lane_f = lane.astype(F32)
    idx_out = jnp.zeros(scores.shape, F32)
    s_out = jnp.zeros(scores.shape, F32)
    hits = []
    for kk in range(TOP_K):
        mx = jnp.max(sel, axis=-1, keepdims=True)
        idx = jnp.min(jnp.where(sel == mx, lane_f, 128.0), axis=-1, keepdims=True)
        hit = lane_f == idx
        sk = jnp.sum(jnp.where(hit, scores, 0.0), axis=-1, keepdims=True)
        sel = jnp.where(hit, -jnp.inf, sel)
        idx_out = jnp.where(lane == kk, idx, idx_out)
        s_out = jnp.where(lane == kk, sk, s_out)
        hits.append(hit)
    tot = jnp.sum(s_out, axis=-1, keepdims=True)
    tw_ref[...] = s_out / tot * ROUTED_SCALE
    cnt = jnp.zeros(scores.shape, F32)
    for hit in hits:
        cnt = cnt + jnp.where(hit, 1.0, 0.0)
    ri = lax.broadcasted_iota(jnp.int32, (TM, TM), 0)
    rj = lax.broadcasted_iota(jnp.int32, (TM, TM), 1)
    before = _dot(jnp.where(rj < ri, 1.0, 0.0).astype(BF16), cnt.astype(BF16)) + carry_ref[...]
    rank_out = jnp.zeros(scores.shape, F32)
    for kk, hit in enumerate(hits):
        rk = jnp.sum(jnp.where(hit, before, 0.0), axis=-1, keepdims=True)
        rank_out = jnp.where(lane == kk, rk, rank_out)
    idx_ref[...] = idx_out.T[:8].astype(jnp.int32)
    rank_ref[...] = rank_out.T[:8].astype(jnp.int32)
    carry_ref[...] = carry_ref[...] + jnp.sum(cnt, axis=0, keepdims=True)
    cnt_ref[...] = carry_ref[...].astype(jnp.int32)


def _outproj(x_lat, x_ctx, mod, l, g2, ys, gdn, wo, wr, eb, n_tiles):
    row = lambda i: (i, 0)
    const = lambda i: (0, 0)
    lat_tiles = S // TM
    seq = lambda i: (jnp.where(i < NT_LAT, i // lat_tiles, i - NT_LAT), jnp.where(i < NT_LAT, i % lat_tiles, lat_tiles), 0)
    rows = n_tiles * TM
    o0, o1, p_gd, norm_g = gdn
    return pl.pallas_call(
        _outproj_kernel,
        grid=(n_tiles,),
        in_specs=_row_tile_specs() + [
            pl.BlockSpec((1, 6, D), lambda i: (l * 16 + (i * TM) // S, 0, 0)),
            pl.BlockSpec((1, D), const),
            *_row_tile_specs(256), *_row_tile_specs(256), *_row_tile_specs(256),
            pl.BlockSpec((1, TM, 256), seq),
            pl.BlockSpec((1, TM, 256), seq),
            pl.BlockSpec((TM, 256), lambda i: (i, 3)),
            pl.BlockSpec((1, 256), const),
            pl.BlockSpec((D, D), const),
            pl.BlockSpec((D, 128), const),
            pl.BlockSpec((1, 128), const),
        ],
        out_specs=[
            pl.BlockSpec((TM, D), row),
            pl.BlockSpec((TM, D), row),
            pl.BlockSpec((TM, D // 2), row),
            pl.BlockSpec((8, TM), row),
            pl.BlockSpec((TM, 128), row),
            pl.BlockSpec((8, TM), row),
            pl.BlockSpec((1, 128), const),
        ],
        out_shape=[
            jax.ShapeDtypeStruct((rows, D), F32),
            jax.ShapeDtypeStruct((rows, D), BF16),
            jax.ShapeDtypeStruct((rows, D // 2), jnp.uint32),
            jax.ShapeDtypeStruct((n_tiles * 8, TM), jnp.int32),
            jax.ShapeDtypeStruct((rows, 128), F32),
            jax.ShapeDtypeStruct((n_tiles * 8, TM), jnp.int32),
            jax.ShapeDtypeStruct((1, 128), jnp.int32),
        ],
        scratch_shapes=[pltpu.VMEM((1, 128), F32)],
        compiler_params=_cp("arbitrary"),
        name="outproj_router",
    )(x_lat, x_ctx, mod, g2, *[a for pair in ys for a in pair], o0, o1, p_gd, jnp.tile(norm_g, 4)[None, :], wo, wr, eb)


ASG_TILE = TM * TOP_K


def _segment_tables(counts, n_blocks):
    counts = counts[0, :N_EXP]
    padded = (counts + MOE_BLK - 1) // MOE_BLK * MOE_BLK
    pad_end = jnp.cumsum(padded)
    seg_start = pad_end - padded
    starts = jnp.arange(n_blocks, dtype=jnp.int32) * MOE_BLK
    blk_e = jnp.minimum(jnp.sum((pad_end[None, :] <= starts[:, None]).astype(jnp.int32), axis=1), N_EXP - 1)
    seg = jnp.concatenate([seg_start, seg_start + counts, pad_end[-1:]])
    blk = jnp.concatenate([blk_e, pad_end[-1:] // MOE_BLK])
    return seg.astype(jnp.int32), blk.astype(jnp.int32)


def _row_copy_wait(shape_ref, dst_ref, sem):
    pltpu.make_async_copy(shape_ref, dst_ref, sem).wait()


def _zero_fill_rows(zero_ref, xb_ref, start, length, sem):
    plan = []
    aligned = (start + 7) & -8
    for r in range(7):
        plan.append(((start + r < aligned) & (r < length),
                     pltpu.make_async_copy(zero_ref.at[pl.ds(0, 1)], xb_ref.at[pl.ds(start + r, 1)], sem)))
    rest = start + length - aligned
    off = aligned
    for bit in reversed(range(3, MOE_BLK.bit_length() - 1)):
        size = 1 << bit
        take = (rest & size) != 0
        dst = xb_ref.at[pl.ds(pl.multiple_of(off, 8), size)]
        plan.append((take, pltpu.make_async_copy(zero_ref.at[pl.ds(0, size)], dst, sem)))
        off = off + jnp.where(take, size, 0)
    for take, copy in plan:
        pl.when(take)(copy.start)
    for take, copy in plan:
        pl.when(take)(copy.wait)


def _dest_kernel(seg_ref, idx_ref, rank_ref, o_ref):
    idx = idx_ref[...]
    first = jnp.zeros(idx.shape, jnp.int32)
    for e in range(N_EXP):
        first = jnp.where(idx == e, seg_ref[e], first)
    o_ref[...] = first + rank_ref[...]


def _dest_rows(seg, top_idx, rank):
    n_tiles = top_idx.shape[0] // 8
    blk = lambda i, seg: (i, 0)
    grid_spec = pltpu.PrefetchScalarGridSpec(
        num_scalar_prefetch=1,
        grid=(n_tiles // 8,),
        in_specs=[pl.BlockSpec((64, TM), blk), pl.BlockSpec((64, TM), blk)],
        out_specs=pl.BlockSpec((64, TM), blk),
    )
    return pl.pallas_call(
        _dest_kernel,
        grid_spec=grid_spec,
        out_shape=jax.ShapeDtypeStruct(top_idx.shape, jnp.int32),
        compiler_params=_cp("arbitrary"),
        name="moe_dest_rows",
    )(seg, top_idx, rank)


def _dispatch_kernel(seg_ref, dest_ref, h2p_ref, xb_ref, zero_ref, ring_ref, sems, zsem):
    i = pl.program_id(0)
    sem = sems.at[i % 2]
    tile_ref = ring_ref.at[i % 2]
    tile_ref[...] = h2p_ref[...]

    @pl.when(i == 0)
    def _():
        zero_ref[...] = jnp.zeros_like(zero_ref)

    @pl.when(i < N_EXP)
    def _():
        first_pad = seg_ref[N_EXP + i]
        next_start = seg_ref[jnp.where(i == N_EXP - 1, 2 * N_EXP, i + 1)]
        _zero_fill_rows(zero_ref, xb_ref, first_pad, next_start - first_pad, zsem)

    tail_row = seg_ref[2 * N_EXP] + (i - N_EXP) * MOE_BLK

    @pl.when((i >= N_EXP) & (tail_row < xb_ref.shape[0]))
    def _():
        copy = pltpu.make_async_copy(zero_ref, xb_ref.at[pl.ds(pl.multiple_of(tail_row, MOE_BLK), MOE_BLK)], zsem)
        copy.start()
        copy.wait()

    def issue(t, carry):
        src = tile_ref.at[pl.ds(t, 1)]
        for k in range(TOP_K):
            pltpu.make_async_copy(src, xb_ref.at[pl.ds(dest_ref[k, t], 1)], sem).start(priority=k % 2)
        return carry

    lax.fori_loop(0, TM, issue, 0, unroll=True)

    @pl.when(i > 0)
    def _():
        for _ in range(TOP_K):
            _row_copy_wait(h2p_ref, xb_ref.at[pl.ds(0, TM)], sems.at[(i + 1) % 2])

    @pl.when(i == pl.num_programs(0) - 1)
    def _():
        for _ in range(TOP_K):
            _row_copy_wait(h2p_ref, xb_ref.at[pl.ds(0, TM)], sem)


def _dispatch_rows(seg, dest, h2p, n_rows):
    n_tiles = h2p.shape[0] // TM
    assert n_tiles >= 2 * N_EXP
    grid_spec = pltpu.PrefetchScalarGridSpec(
        num_scalar_prefetch=1,
        grid=(n_tiles,),
        in_specs=[
            pl.BlockSpec((8, TM), lambda i, seg: (i, 0), memory_space=pltpu.SMEM),
            pl.BlockSpec((TM, D // 2), lambda i, seg: (i, 0)),
        ],
        out_specs=pl.BlockSpec(memory_space=pl.ANY),
        scratch_shapes=[pltpu.VMEM((MOE_BLK, D // 2), jnp.uint32), pltpu.VMEM((2, TM, D // 2), jnp.uint32),
                        pltpu.SemaphoreType.DMA((2,)), pltpu.SemaphoreType.DMA(())],
    )
    return pl.pallas_call(
        _dispatch_kernel,
        grid_spec=grid_spec,
        out_shape=jax.ShapeDtypeStruct((n_rows, D // 2), jnp.uint32),
        compiler_params=_cp("arbitrary"),
        name="moe_dispatch",
    )(seg, dest, h2p)


def _moe_kernel(be_ref, x_ref, wg_ref, wu_ref, wd_ref, o_ref, wgb_ref, wub_ref, wdb_ref):
    i = pl.program_id(0)
    in_use = i < be_ref[pl.num_programs(0)]

    @pl.when(in_use & ((i == 0) | (be_ref[i] != be_ref[jnp.maximum(i - 1, 0)])))
    def _():
        wgb_ref[...] = wg_ref[0, 0].astype(BF16)
        wub_ref[...] = wu_ref[0, 0].astype(BF16)
        wdb_ref[...] = wd_ref[0, 0].astype(BF16)

    @pl.when(in_use)
    def _():
        x = _unpack_bf16_pairs(x_ref[...]).astype(BF16)
        g = _dot(x, wgb_ref[...])
        u = _dot(x, wub_ref[...])
        hid = (_silu(g) * u).astype(BF16)
        o_ref[...] = _pack_bf16_pairs(_dot(hid, wdb_ref[...]))

    @pl.when(jnp.logical_not(in_use))
    def _():
        o_ref[...] = jnp.zeros_like(o_ref)


def _moe_experts(blk_e, xb, l, wg, wu, wd):
    n_rows = xb.shape[0]
    grid_spec = pltpu.PrefetchScalarGridSpec(
        num_scalar_prefetch=1,
        grid=(n_rows // MOE_BLK,),
        in_specs=[
            pl.BlockSpec((MOE_BLK, D // 2), lambda i, be: (i, 0)),
            pl.BlockSpec((1, 1, D, D_EXP), lambda i, be: (l, be[i], 0, 0)),
            pl.BlockSpec((1, 1, D, D_EXP), lambda i, be: (l, be[i], 0, 0)),
            pl.BlockSpec((1, 1, D_EXP, D), lambda i, be: (l, be[i], 0, 0)),
        ],
        out_specs=pl.BlockSpec((MOE_BLK, D // 2), lambda i, be: (i, 0)),
        scratch_shapes=[pltpu.VMEM((D, D_EXP), BF16), pltpu.VMEM((D, D_EXP), BF16), pltpu.VMEM((D_EXP, D), BF16)],
    )
    return pl.pallas_call(
        _moe_kernel,
        grid_spec=grid_spec,
        out_shape=jax.ShapeDtypeStruct((n_rows, D // 2), jnp.uint32),
        compiler_params=_cp("arbitrary"),
        name="moe_experts",
    )(blk_e, xb, wg, wu, wd)


def _ffn_out_kernel(dest_ref, x_ref, mod_ref, h2_ref, tw_ref, wg_ref, wu_ref, wd_ref, nf_ref, yb_ref, *rest, final):
    if final:
        o_ref, gat_ref, sems = rest
    else:
        proj_in, o_ref, proj_out, gat_ref, sems = rest[:10], rest[10], rest[11:16], rest[16], rest[17]
    s = pl.program_id(0)

    def start_gathers(slot):
        buf = gat_ref.at[slot]

        def issue(t, carry):
            for k in range(TOP_K):
                copy = pltpu.make_async_copy(yb_ref.at[pl.ds(dest_ref[k, t], 1)], buf.at[pl.ds(k * TM + t, 1)],
                                             sems.at[slot])
                copy.start(priority=k % 2)
            return carry

        lax.fori_loop(0, TM, issue, 0, unroll=True)

    def wait_gathers(slot):
        for _ in range(TOP_K):
            _row_copy_wait(yb_ref.at[pl.ds(0, TM)], gat_ref.at[slot, pl.ds(0, TM)], sems.at[slot])

    pl.when(s == 0)(lambda: start_gathers(1))
    done = (s + 1) % 2
    wait_gathers(done)
    start_gathers(s % 2)
    m = mod_ref[0]
    h2 = h2_ref[...]
    hid = (_silu(_dot(h2, wg_ref[...])) * _dot(h2, wu_ref[...])).astype(BF16)
    shared = _dot(hid, wd_ref[...])
    buf = gat_ref.at[done]
    tw = tw_ref[...]
    routed = _unpack_bf16_pairs(buf[0:TM, :]) * tw[:, 0:1]
    for k in range(1, TOP_K):
        routed = routed + _unpack_bf16_pairs(buf[k * TM:(k + 1) * TM, :]) * tw[:, k:k + 1]
    x = x_ref[...] + m[5:6] * (routed + shared)
    if final:
        ms = jnp.mean(x * x, axis=-1, keepdims=True)
        x = x * lax.rsqrt(ms + EPS) * nf_ref[...]
    o_ref[...] = x
    if not final:
        _inproj_tile(x, *proj_in, *proj_out)
    pl.when(s == pl.num_programs(0) - 1)(lambda: wait_gathers(s % 2))


def _ffn_out(dest, xn, mod, l, h2, top_w, wg, wu, wd, norm_f, yb, n_tiles, next_proj=None):
    tile_of = lambda i: jnp.maximum(i - 1, 0)
    row = lambda i: (tile_of(i), 0)
    ahead = lambda i: (jnp.minimum(i, n_tiles - 1), 0)
    const = lambda i: (0, 0)
    in_specs = [
        pl.BlockSpec((8, TM), ahead, memory_space=pltpu.SMEM),
        pl.BlockSpec((TM, D), row),
        pl.BlockSpec((1, 6, D), lambda i: (l * 16 + (tile_of(i) * TM) // S, 0, 0)),
        pl.BlockSpec((TM, D), row),
        pl.BlockSpec((TM, 128), row),
        pl.BlockSpec((D, D_EXP), const),
        pl.BlockSpec((D, D_EXP), const),
        pl.BlockSpec((D_EXP, D), const),
        pl.BlockSpec((1, D), const),
        pl.BlockSpec(memory_space=pl.ANY),
    ]
    args = [dest, xn, mod, h2, top_w, wg, wu, wd, norm_f, yb]
    out_specs = [pl.BlockSpec((TM, D), row)]
    out_shape = [jax.ShapeDtypeStruct((n_tiles * TM, D), F32)]
    if next_proj is not None:
        g, ws, tabs = next_proj
        proj_in, proj_out, proj_shape = _inproj_specs(l + 1, tile_of)
        in_specs += proj_in
        args += [mod, g, *ws, *tabs]
        out_specs += proj_out
        out_shape += proj_shape
    return pl.pallas_call(
        functools.partial(_ffn_out_kernel, final=next_proj is None),
        grid=(n_tiles + 1,),
        in_specs=in_specs,
        out_specs=out_specs,
        scratch_shapes=[pltpu.VMEM((2, ASG_TILE, D // 2), jnp.uint32), pltpu.SemaphoreType.DMA((2,))],
        out_shape=out_shape,
        compiler_params=_cp("arbitrary"),
        name="shared_ffn_residual",
    )(*args)


def kernel(x, c, ctx, c_ctx, w_ada, b_ada, norm1, norm2, norm_f, w_in, w_out, hy_conv, hy_w1, hy_b1, hy_w2, hy_b2, hy_w3, hy_freq, hy_bias, wa_sink, na_rpb, gdn_conv, gdn_a_log, gdn_dt_bias, gdn_norm, moe_router, moe_bias, moe_gate, moe_up, moe_down, sh_gate, sh_up, sh_down):
    depth = w_ada.shape[0]
    x_lat, x_ctx = x.reshape(NL, D), ctx.reshape(NC, D)
    cvec = jnp.concatenate([c, c_ctx[None, :], jnp.zeros((16 - B - 1, D), F32)], axis=0)
    mod = _ada(cvec, w_ada, b_ada).reshape(depth * 16, 6, D)
    rope = _rope_tables()
    dft_lat = _dft_tables(S)
    dft_ctx = _dft_tables(LC)
    o1, o2, o3 = 768, 768 + 512, 768 + 512 + 768

    def proj_params(l):
        wl = w_in[l].astype(BF16)
        ws = (wl[:, :o1], wl[:, o1:o2], wl[:, o2:o3], wl[:, o3:o3 + 1024],
              jnp.pad(wl[:, o3 + 1024:], ((0, 0), (0, 128 - 16))))
        return norm1[l][None, :], ws, rope

    p_hy, p_wa, p_na, p_gd, p_ab = _inproj(x_lat, x_ctx, mod, 0, *proj_params(0))
    for l in range(depth):
        need_ctx = l < depth - 1
        n_tiles = NT_ALL if need_ctx else NT_LAT

        filt = (hy_w1[l], hy_b1[l], hy_w2[l], hy_b2[l], hy_w3[l], hy_freq[l])
        kre, kim = _hyena_filter_spectrum(S, *filt, dft_lat)
        y_hy = _hyena(p_hy, S, 0, hy_conv[l], hy_bias[l], dft_lat, kre, kim)
        y_wa = _window_attention(p_wa, wa_sink[l])
        y_na = _neighborhood_attention(p_na, _na_bias_table(na_rpb[l]))
        if need_ctx:
            kre_c, kim_c = _hyena_filter_spectrum(LC, *filt, dft_ctx)
            yc_hy = _hyena(p_hy, LC, CTX_BLK0, hy_conv[l], hy_bias[l], dft_ctx, kre_c, kim_c)
            yc_wa, yc_na = _ctx_attention(p_wa, p_na, wa_sink[l])
        else:
            yc_hy, yc_wa, yc_na = y_hy, y_wa, y_na
        qkv, gb = _gdn_prep(p_gd, p_ab, gdn_conv[l], gdn_a_log[l], gdn_dt_bias[l])
        gd_fwd, gd_bwd = _gdn_scan(qkv, gb)

        wr = jnp.pad(moe_router[l], ((0, 0), (0, 128 - N_EXP)))
        eb = jnp.pad(moe_bias[l], (0, 128 - N_EXP))[None, :]
        xn, h2, h2p, top_idx, top_w, rank, counts = _outproj(
            x_lat, x_ctx, mod, l, norm2[l][None, :], ((y_hy, yc_hy), (y_wa, yc_wa), (y_na, yc_na)),
            (gd_fwd, gd_bwd, p_gd, gdn_norm[l]),
            w_out[l].astype(BF16), wr, eb, n_tiles)
        n_asg = n_tiles * ASG_TILE
        n_blocks = n_asg // MOE_BLK + N_EXP
        seg, blk_e = _segment_tables(counts, n_blocks)
        dest = _dest_rows(seg, top_idx, rank)
        xb = _dispatch_rows(seg, dest, h2p, n_blocks * MOE_BLK)
        yb = _moe_experts(blk_e, xb, l, moe_gate, moe_up, moe_down)
        res = _ffn_out(dest, xn, mod, l, h2, top_w, sh_gate[l].astype(BF16), sh_up[l].astype(BF16),
                       sh_down[l].astype(BF16), norm_f[None, :], yb, n_tiles,
                       next_proj=proj_params(l + 1) if need_ctx else None)
        if need_ctx:
            x_lat, p_hy, p_wa, p_na, p_gd, p_ab = res
            x_ctx = x_lat[NL:] if l + 2 < depth else x_lat
        else:
            return res[0].reshape(B, S, D)
```
